```python
import math
import jax
import jax.numpy as jnp
from jax import lax
import numpy as np

D_MODEL = 1024
BATCH = 8
SEQ = 4096
DEPTH = 4

CHUNK = 64

MIX_WIDTH = D_MODEL
POOL_WIDTH = MIX_WIDTH // 2
POOL_WINDOWS = (2, 4, 8, 16)
POOL_GROUPS = len(POOL_WINDOWS)
POOL_GROUP_DIM = POOL_WIDTH // POOL_GROUPS
HEAD_DIM = 64
ATTN_WIDTH = MIX_WIDTH - POOL_WIDTH
ATTN_HEADS = ATTN_WIDTH // HEAD_DIM

OFF_POOL_U = 0
OFF_POOL_G = OFF_POOL_U + POOL_WIDTH
OFF_Q = OFF_POOL_G + POOL_WIDTH
OFF_K = OFF_Q + ATTN_WIDTH
OFF_V = OFF_K + ATTN_WIDTH
OFF_ATTN_G = OFF_V + ATTN_WIDTH
OFF_F = OFF_ATTN_G + ATTN_WIDTH
IN_COLS = OFF_F + ATTN_HEADS

Q_BLOCK = 128
RMS_EPS = 1e-6
NEG_INF = -1e30

kernel_name = "hymba_pool_fox_trunk"


def rmsnorm(x, g):
    xf = x.astype(jnp.float32)
    y = xf * lax.rsqrt(jnp.mean(xf * xf, axis=-1, keepdims=True) + RMS_EPS)
    return (y * g.astype(jnp.float32)).astype(x.dtype)


def multi_scale_pool(u, pool_w, pool_scale):
    b, s, _ = u.shape
    uf = u.astype(jnp.float32)
    cs = jnp.concatenate([jnp.zeros((b, 1, POOL_WIDTH), jnp.float32), jnp.cumsum(uf, axis=1)], axis=1)
    t = jnp.arange(s)
    parts = []
    for gi, w in enumerate(POOL_WINDOWS):
        sl = slice(gi * POOL_GROUP_DIM, (gi + 1) * POOL_GROUP_DIM)
        c_g = cs[:, :, sl]
        lower = jnp.concatenate([jnp.zeros((b, w - 1, POOL_GROUP_DIM), jnp.float32), c_g[:, : s + 1 - w]], axis=1)
        count = jnp.minimum(t + 1, w).astype(jnp.float32)[None, :, None]
        parts.append((c_g[:, 1:] - lower) / count - uf[:, :, sl])
    d = jnp.stack(parts, axis=2)
    y = jnp.einsum('bsgc,gcd->bsgd', d, pool_w.astype(jnp.float32)).reshape(b, s, POOL_WIDTH)
    y = y * pool_scale.astype(jnp.float32)
    return y.astype(u.dtype)


def forgetting_attention(q, k, v, log_f):
    b, s, h, dh = q.shape
    q = q.transpose(0, 2, 1, 3)
    k = k.transpose(0, 2, 1, 3)
    v = v.transpose(0, 2, 1, 3)
    c = jnp.cumsum(log_f, axis=1).transpose(0, 2, 1)
    scale = 1.0 / math.sqrt(dh)
    outs = []
    for i in range(s // Q_BLOCK):
        q0, q1 = i * Q_BLOCK, (i + 1) * Q_BLOCK
        qb = q[:, :, q0:q1]
        kb = k[:, :, :q1]
        vb = v[:, :, :q1]
        logits = jnp.einsum('bhqd,bhkd->bhqk', qb, kb).astype(jnp.float32) * scale
        logits = logits + (c[:, :, q0:q1, None] - c[:, :, None, :q1])
        mask = jnp.arange(q0, q1)[:, None] >= jnp.arange(q1)[None, :]
        logits = jnp.where(mask[None, None], logits, NEG_INF)
        p = jax.nn.softmax(logits, axis=-1)
        outs.append(jnp.einsum('bhqk,bhkd->bhqd', p.astype(vb.dtype), vb))
    o = jnp.concatenate(outs, axis=2)
    return o.transpose(0, 2, 1, 3).reshape(b, s, h * dh)


def _fwd_setup_inputs(seed: int = 0) -> dict:
    key = jax.random.key(seed)
    ks = jax.random.split(key, 9)
    x = jax.random.normal(ks[0], (BATCH, SEQ, D_MODEL), jnp.float32)
    norm_g = 1.0 + 0.02 * jax.random.normal(ks[1], (DEPTH, D_MODEL), jnp.float32)
    w_in = jax.random.normal(ks[2], (DEPTH, D_MODEL, IN_COLS), jnp.float32) * D_MODEL ** -0.5
    forget_bias = jax.random.uniform(ks[3], (DEPTH, ATTN_HEADS), jnp.float32, minval=1.0, maxval=3.0)
    pool_w = jax.random.normal(ks[4], (DEPTH, POOL_GROUPS, POOL_GROUP_DIM, POOL_GROUP_DIM), jnp.float32) * POOL_GROUP_DIM ** -0.5
    pool_scale = 1.0 + 0.02 * jax.random.normal(ks[5], (DEPTH, POOL_WIDTH), jnp.float32)
    w_out = jax.random.normal(ks[6], (DEPTH, MIX_WIDTH, D_MODEL), jnp.float32) * MIX_WIDTH ** -0.5
    final_g = 1.0 + 0.02 * jax.random.normal(ks[7], (D_MODEL,), jnp.float32)
    return {"x": x, "norm_g": norm_g, "w_in": w_in, "forget_bias": forget_bias,
            "pool_w": pool_w, "pool_scale": pool_scale, "w_out": w_out, "final_g": final_g}


def _fwd_reference(x, norm_g, w_in, forget_bias, pool_w, pool_scale, w_out, final_g):
    b, s, _ = x.shape
    for layer in range(DEPTH):
        h = rmsnorm(x, norm_g[layer])
        proj = h @ w_in[layer]
        pool_u = proj[..., OFF_POOL_U:OFF_POOL_G]
        pool_g = proj[..., OFF_POOL_G:OFF_Q]
        q = proj[..., OFF_Q:OFF_K].reshape(b, s, ATTN_HEADS, HEAD_DIM)
        k = proj[..., OFF_K:OFF_V].reshape(b, s, ATTN_HEADS, HEAD_DIM)
        v = proj[..., OFF_V:OFF_ATTN_G].reshape(b, s, ATTN_HEADS, HEAD_DIM)
        attn_g = proj[..., OFF_ATTN_G:OFF_F]
        log_f = jax.nn.log_sigmoid(proj[..., OFF_F:IN_COLS].astype(jnp.float32)
                                   + forget_bias[layer].astype(jnp.float32))
        pool_out = multi_scale_pool(pool_u, pool_w[layer], pool_scale[layer]) * jax.nn.silu(pool_g)
        attn_out = forgetting_attention(q, k, v, log_f) * jax.nn.silu(attn_g)
        mixed = jnp.concatenate([pool_out, attn_out], axis=-1)
        x = x + mixed @ w_out[layer]
    return rmsnorm(x, final_g)


import jax as _jax
import jax.numpy as _jnp

TWIN_FORMAT = 'train_step'
FWD_PARAMS = ['x', 'norm_g', 'w_in', 'forget_bias', 'pool_w', 'pool_scale', 'w_out', 'final_g']
TWIN_WEIGHTS = ['norm_g', 'w_in', 'forget_bias', 'pool_w', 'pool_scale', 'w_out', 'final_g']
TWIN_DIFF_INPUT = 'x'
TWIN_INPUTS = ['x', 'norm_g', 'w_in', 'forget_bias', 'pool_w', 'pool_scale', 'w_out', 'final_g', 'loss_target', 'm_norm_g', 'm_w_in', 'm_forget_bias', 'm_pool_w', 'm_pool_scale', 'm_w_out', 'm_final_g', 'v_norm_g', 'v_w_in', 'v_forget_bias', 'v_pool_w', 'v_pool_scale', 'v_w_out', 'v_final_g']
TWIN_OUTPUTS = ['loss', 'grad_x', 'grad_norm_g', 'grad_w_in', 'grad_forget_bias', 'grad_pool_w', 'grad_pool_scale', 'grad_w_out', 'grad_final_g', 'delta_norm_g', 'delta_w_in', 'delta_forget_bias', 'delta_pool_w', 'delta_pool_scale', 'delta_w_out', 'delta_final_g', 'new_m_norm_g', 'new_m_w_in', 'new_m_forget_bias', 'new_m_pool_w', 'new_m_pool_scale', 'new_m_w_out', 'new_m_final_g', 'new_v_norm_g', 'new_v_w_in', 'new_v_forget_bias', 'new_v_pool_w', 'new_v_pool_scale', 'new_v_w_out', 'new_v_final_g']
TWIN_LEAF_KINDS = {'loss': 'loss', 'grad_x': 'grad_x', 'grad_norm_g': 'grad_w', 'grad_w_in': 'grad_w', 'grad_forget_bias': 'grad_w', 'grad_pool_w': 'grad_w', 'grad_pool_scale': 'grad_w', 'grad_w_out': 'grad_w', 'grad_final_g': 'grad_w', 'delta_norm_g': 'delta_w', 'delta_w_in': 'delta_w', 'delta_forget_bias': 'delta_w', 'delta_pool_w': 'delta_w', 'delta_pool_scale': 'delta_w', 'delta_w_out': 'delta_w', 'delta_final_g': 'delta_w', 'new_m_norm_g': 'new_m', 'new_m_w_in': 'new_m', 'new_m_forget_bias': 'new_m', 'new_m_pool_w': 'new_m', 'new_m_pool_scale': 'new_m', 'new_m_w_out': 'new_m', 'new_m_final_g': 'new_m', 'new_v_norm_g': 'new_v', 'new_v_w_in': 'new_v', 'new_v_forget_bias': 'new_v', 'new_v_pool_w': 'new_v', 'new_v_pool_scale': 'new_v', 'new_v_w_out': 'new_v', 'new_v_final_g': 'new_v'}


def _forward(args):
    return _fwd_reference(*[args[k] for k in FWD_PARAMS])


def _output_shape():
    out = _jax.eval_shape(lambda: _forward(_fwd_setup_inputs(0)))
    return out.shape, out.dtype

N_MICROBATCH = 1
ADAM_LR = 0.001
ADAM_B1 = 0.9
ADAM_B2 = 0.999
ADAM_EPS = 1e-08
ADAM_WD = 0.01
ADAM_STEP = 10
PER_EXAMPLE_BATCH_AXIS = {'x': 0, 'loss_target': 0}
SHARED_INPUTS = []
_WEIGHT_DTYPES = {'norm_g': _jnp.float32, 'w_in': _jnp.float32, 'forget_bias': _jnp.float32, 'pool_w': _jnp.float32, 'pool_scale': _jnp.float32, 'w_out': _jnp.float32, 'final_g': _jnp.float32}
MOMENT_SCALE = {'norm_g': 1.049228e-01, 'w_in': 6.012798e-02, 'forget_bias': 2.094895e-01, 'pool_w': 8.762176e-02, 'pool_scale': 8.973433e-02, 'w_out': 6.885595e-02, 'final_g': 3.194622e+01}


def _to_microbatches(a, axis):
    t = _jnp.moveaxis(a, axis, 0)
    t = t.reshape((N_MICROBATCH, t.shape[0] // N_MICROBATCH) + t.shape[1:])
    return _jnp.moveaxis(t, 1, axis + 1)


def setup_inputs(seed: int = 0) -> dict:
    inp = _fwd_setup_inputs(seed)
    key = _jax.random.fold_in(_jax.random.key(seed), 7919)
    shape, _ = _output_shape()
    out = dict(inp)
    out["loss_target"] = _jax.random.normal(_jax.random.fold_in(key, 0), shape, _jnp.float32)
    for i, name in enumerate(TWIN_WEIGHTS):
        w = inp[name].astype(_jnp.float32)
        if MOMENT_SCALE is None:
            s = _jnp.sqrt(_jnp.mean(_jnp.square(w)) + 1e-30)
        else:
            s = MOMENT_SCALE[name]
        km, kv = _jax.random.split(_jax.random.fold_in(key, i + 1))
        out[name] = w
        out["m_" + name] = s * _jax.random.normal(km, w.shape, _jnp.float32)
        out["v_" + name] = (s * s) * _jax.random.uniform(kv, w.shape, _jnp.float32, 0.5, 1.5)
    if N_MICROBATCH > 1:
        for name, axis in PER_EXAMPLE_BATCH_AXIS.items():
            out[name] = _to_microbatches(out[name], axis)
    return {'x': out['x'], 'norm_g': out['norm_g'], 'w_in': out['w_in'], 'forget_bias': out['forget_bias'], 'pool_w': out['pool_w'], 'pool_scale': out['pool_scale'], 'w_out': out['w_out'], 'final_g': out['final_g'], 'loss_target': out['loss_target'], 'm_norm_g': out['m_norm_g'], 'm_w_in': out['m_w_in'], 'm_forget_bias': out['m_forget_bias'], 'm_pool_w': out['m_pool_w'], 'm_pool_scale': out['m_pool_scale'], 'm_w_out': out['m_w_out'], 'm_final_g': out['m_final_g'], 'v_norm_g': out['v_norm_g'], 'v_w_in': out['v_w_in'], 'v_forget_bias': out['v_forget_bias'], 'v_pool_w': out['v_pool_w'], 'v_pool_scale': out['v_pool_scale'], 'v_w_out': out['v_w_out'], 'v_final_g': out['v_final_g']}


def _loss(weights, diff, rest, loss_target):
    with _jax.named_scope("forward"):
        args = {**rest, TWIN_DIFF_INPUT: diff, **{k: w.astype(_WEIGHT_DTYPES[k]) for k, w in weights.items()}}
        y = _forward(args)
    with _jax.named_scope("loss_head"):
        err = _jnp.square(y.astype(_jnp.float32) - loss_target)
        return 0.5 * _jnp.sum(_jnp.mean(err, axis=-1)) if err.ndim else 0.5 * err


def _adamw(w, g, m, v):
    m = ADAM_B1 * m + (1.0 - ADAM_B1) * g
    v = ADAM_B2 * v + (1.0 - ADAM_B2) * _jnp.square(g)
    m_hat = m / (1.0 - ADAM_B1 ** ADAM_STEP)
    v_hat = v / (1.0 - ADAM_B2 ** ADAM_STEP)
    delta = -ADAM_LR * (m_hat / (_jnp.sqrt(v_hat) + ADAM_EPS) + ADAM_WD * w)
    return delta, m, v


def reference(x, norm_g, w_in, forget_bias, pool_w, pool_scale, w_out, final_g, loss_target, m_norm_g, m_w_in, m_forget_bias, m_pool_w, m_pool_scale, m_w_out, m_final_g, v_norm_g, v_w_in, v_forget_bias, v_pool_w, v_pool_scale, v_w_out, v_final_g):
    given = dict(x=x, norm_g=norm_g, w_in=w_in, forget_bias=forget_bias, pool_w=pool_w, pool_scale=pool_scale, w_out=w_out, final_g=final_g, loss_target=loss_target, m_norm_g=m_norm_g, m_w_in=m_w_in, m_forget_bias=m_forget_bias, m_pool_w=m_pool_w, m_pool_scale=m_pool_scale, m_w_out=m_w_out, m_final_g=m_final_g, v_norm_g=v_norm_g, v_w_in=v_w_in, v_forget_bias=v_forget_bias, v_pool_w=v_pool_w, v_pool_scale=v_pool_scale, v_w_out=v_w_out, v_final_g=v_final_g)
    weights = {n: given[n] for n in TWIN_WEIGHTS}
    shared = {n: given[n] for n in SHARED_INPUTS}
    per_example = {n: given[n] for n in ['x']}
    grad_fn = _jax.value_and_grad(_loss, argnums=(0, 1))

    def one_microbatch(ex, loss_target):
        ex = dict(ex)
        diff = ex.pop(TWIN_DIFF_INPUT)
        return grad_fn(weights, diff, {**shared, **ex}, loss_target)

    if N_MICROBATCH == 1:
        loss, (grad_w, grad_x) = one_microbatch(per_example, given["loss_target"])
    else:
        def body(carry, xs):
            loss_sum, grad_sum = carry
            l_k, (gw_k, gx_k) = one_microbatch(xs[0], xs[1])
            with _jax.named_scope("update"):
                return (loss_sum + l_k, _jax.tree.map(_jnp.add, grad_sum, gw_k)), gx_k

        init = (_jnp.zeros((), _jnp.float32), _jax.tree.map(_jnp.zeros_like, weights))
        (loss, grad_w), grad_x = _jax.lax.scan(body, init, (per_example, given["loss_target"]))
    with _jax.named_scope("update"):
        delta_w, new_m, new_v = {}, {}, {}
        for n in TWIN_WEIGHTS:
            delta_w[n], new_m[n], new_v[n] = _adamw(weights[n], grad_w[n], given["m_" + n], given["v_" + n])
    return (loss, grad_x, *[grad_w[n] for n in TWIN_WEIGHTS], *[delta_w[n] for n in TWIN_WEIGHTS],
            *[new_m[n] for n in TWIN_WEIGHTS], *[new_v[n] for n in TWIN_WEIGHTS])
```

```python
import jax
import jax.numpy as jnp
from jax import lax
from jax.experimental import pallas as pl
from jax.experimental.pallas import tpu as pltpu

F32 = jnp.float32
BF16 = jnp.bfloat16

D_MODEL = 1024
DEPTH = 4
POOL_WIDTH = 512
ATTN_WIDTH = 512
HEAD_DIM = 64
PAIR = 2 * HEAD_DIM
N_PAIRS = ATTN_WIDTH // PAIR
POOL_WINDOWS = (2, 4, 8, 16)
GROUP_DIM = 128
HALO = 16
IN_COLS = 3080
OFF_F = 3072
IN_COLS_PAD = 3200
SECTIONS = ((0, 512), (512, 512), (1024, 512), (1536, 512), (2048, 512), (2560, 512), (OFF_F, 128))
N_CHIPS = 4
SHARD_COLS = IN_COLS // N_CHIPS
SHARD_ROWS = D_MODEL // N_CHIPS
RMS_EPS = 1e-6
NEG = -1e30
Q_SCALE = 0.125

ADAM_LR = 0.001
ADAM_B1 = 0.9
ADAM_B2 = 0.999
ADAM_EPS = 1e-08
ADAM_WD = 0.01
ADAM_STEP = 10

SMALL_ROWS = 2112
SMALL_HALF = SMALL_ROWS // 2

NT = (((1,), (1,)), ((), ()))
TN = (((0,), (0,)), ((), ()))
MESH = pl.DeviceIdType.MESH
ANY = pl.BlockSpec(memory_space=pl.ANY)


def _params(semantics, vmem_mb=48):
    return pltpu.CompilerParams(dimension_semantics=semantics, vmem_limit_bytes=vmem_mb << 20)


def _row_spec(tm, cols):
    return pl.BlockSpec((tm, cols), lambda i: (i, 0))


def _full_spec(shape):
    return pl.BlockSpec(shape, lambda *_: (0,) * len(shape))


def _sigmoid(x):
    return jax.nn.sigmoid(x)


def _scan_rows(a, reverse=False):
    n = a.shape[0]
    row = lax.broadcasted_iota(jnp.int32, a.shape, 0)
    k = 1
    while k < n:
        if reverse:
            a = a + jnp.where(row < n - k, pltpu.roll(a, n - k, 0), 0.0)
        else:
            a = a + jnp.where(row >= k, pltpu.roll(a, k, 0), 0.0)
        k *= 2
    return a


def _inproj_fwd(x, g, w, fb, tm=256):
    s = x.shape[0]

    def body(x_ref, g_ref, w_ref, fb_ref, pu_ref, pg_ref, q_ref, k_ref, v_ref, ag_ref,
             lf_ref, c_ref, ct_ref, carry_ref):
        @pl.when(pl.program_id(0) == 0)
        def _():
            carry_ref[...] = jnp.zeros_like(carry_ref)

        xf = x_ref[...]
        r = lax.rsqrt(jnp.mean(xf * xf, axis=-1, keepdims=True) + RMS_EPS)
        h = (xf * r * g_ref[...]).astype(BF16)

        def proj(sec):
            off, n = SECTIONS[sec]
            return jnp.dot(h, w_ref[:, off:off + n], preferred_element_type=F32)

        pu_ref[...] = proj(0).astype(BF16)
        pg_ref[...] = proj(1).astype(BF16)
        q_ref[...] = (proj(2) * Q_SCALE).astype(BF16)
        k_ref[...] = proj(3).astype(BF16)
        v_ref[...] = proj(4).astype(BF16)
        ag_ref[...] = proj(5).astype(BF16)
        z = proj(6) + fb_ref[...]
        lf = jnp.minimum(z, 0.0) - jnp.log(1.0 + jnp.exp(-jnp.abs(z)))
        lf_ref[...] = lf
        c_ref[...] = _scan_rows(lf) + carry_ref[0:1, :]
        carry_ref[0:1, :] = c_ref[tm - 1:tm, :]
        ct_ref[...] = jnp.transpose(c_ref[...])[0:8, :]

    act = jax.ShapeDtypeStruct((s, 512), BF16)
    return pl.pallas_call(
        body, name="inproj_fwd", grid=(s // tm,),
        in_specs=[_row_spec(tm, D_MODEL), _full_spec((1, D_MODEL)), _full_spec((D_MODEL, IN_COLS_PAD)),
                  _full_spec((1, 128))],
        out_specs=[_row_spec(tm, 512)] * 6 + [_row_spec(tm, 128), _row_spec(tm, 128),
                                              pl.BlockSpec((8, tm), lambda i: (0, i))],
        out_shape=[act] * 6 + [jax.ShapeDtypeStruct((s, 128), F32), jax.ShapeDtypeStruct((s, 128), F32),
                               jax.ShapeDtypeStruct((8, s), F32)],
        scratch_shapes=[pltpu.VMEM((8, 128), F32)],
        compiler_params=_params(("arbitrary",)),
    )(x, g, w, fb)


def _window_sums(ext, forward):
    n = ext.shape[0]
    outs = []
    for gi, w in enumerate(POOL_WINDOWS):
        a = ext[:, gi * GROUP_DIM:(gi + 1) * GROUP_DIM]
        k = 1
        while k < w:
            a = a + pltpu.roll(a, (n - k) if forward else k, 0)
            k *= 2
        outs.append(a)
    return outs


def _window_counts(row0, tm):
    t = row0 + lax.broadcasted_iota(jnp.int32, (tm, GROUP_DIM), 0)
    return [jnp.minimum(t + 1, w).astype(F32) for w in POOL_WINDOWS]


def _pool_delta(u, halo, row0):
    tm = u.shape[0]
    sums = _window_sums(jnp.concatenate([halo, u], axis=0), forward=False)
    cnt = _window_counts(row0, tm)
    return [sums[gi][HALO:, :] / cnt[gi] - u[:, gi * GROUP_DIM:(gi + 1) * GROUP_DIM]
            for gi in range(len(POOL_WINDOWS))]


def _pool_fwd(pu, pg, pw, ps, tm=512):
    s = pu.shape[0]
    hb = tm // HALO

    def body(pu_ref, halo_ref, pg_ref, pw_ref, ps_ref, po_ref):
        i = pl.program_id(0)
        u = pu_ref[...].astype(F32)
        halo = jnp.where(i == 0, 0.0, halo_ref[...].astype(F32))
        d = _pool_delta(u, halo, i * tm)
        for gi in range(len(POOL_WINDOWS)):
            cols = slice(gi * GROUP_DIM, (gi + 1) * GROUP_DIM)
            z = jnp.dot(d[gi].astype(BF16), pw_ref[gi], preferred_element_type=F32)
            gate = pg_ref[:, cols].astype(F32)
            po_ref[:, cols] = (z * ps_ref[:, cols] * (gate * _sigmoid(gate))).astype(BF16)

    return pl.pallas_call(
        body, name="pool_fwd", grid=(s // tm,),
        in_specs=[_row_spec(tm, 512),
                  pl.BlockSpec((HALO, 512), lambda i: (jnp.maximum(i * hb - 1, 0), 0)),
                  _row_spec(tm, 512), _full_spec((4, GROUP_DIM, GROUP_DIM)), _full_spec((1, 512))],
        out_specs=_row_spec(tm, 512),
        out_shape=jax.ShapeDtypeStruct((s, 512), BF16),
        compiler_params=_params(("parallel",)),
    )(pu, pu, pg, pw, ps)


def _attn_fwd(q, k, v, ct, ag, tq=256):
    s = q.shape[0]
    tk = tq

    def body(q_ref, k_ref, v_ref, ct_ref, ag_ref, ma_ref, o_ref, lse_ref, vm_ref):
        i = pl.program_id(1)
        lane = lax.broadcasted_iota(jnp.int32, (tq, PAIR), 1)
        first = lane < HEAD_DIM

        @pl.when(i == 0)
        def _():
            vv = v_ref[...]
            lane_s = lax.broadcasted_iota(jnp.int32, vv.shape, 1)
            vm_ref[0] = jnp.where(lane_s < HEAD_DIM, vv, jnp.zeros_like(vv))
            vm_ref[1] = jnp.where(lane_s >= HEAD_DIM, vv, jnp.zeros_like(vv))

        qq = q_ref[...]
        qm = (jnp.where(first, qq, jnp.zeros_like(qq)), jnp.where(first, jnp.zeros_like(qq), qq))

        def step(kb, carry, masked):
            m, l, acc = carry
            start = pl.multiple_of(kb * tk, tk)
            kblk = k_ref[pl.ds(start, tk), :]
            new_m, new_l, alpha, pv = [], [], [], []
            for h in range(2):
                sc = lax.dot_general(qm[h], kblk, NT, preferred_element_type=F32)
                sc = sc - ct_ref[pl.ds(h, 1), pl.ds(start, tk)]
                if masked:
                    row = lax.broadcasted_iota(jnp.int32, (tq, tk), 0)
                    col = lax.broadcasted_iota(jnp.int32, (tq, tk), 1)
                    sc = jnp.where(row >= col, sc, NEG)
                mh = jnp.maximum(m[h], jnp.max(sc, axis=1, keepdims=True))
                p = jnp.exp(sc - mh)
                a = jnp.exp(m[h] - mh)
                new_m.append(mh)
                new_l.append(a * l[h] + jnp.sum(p, axis=1, keepdims=True))
                alpha.append(a)
                pv.append(jnp.dot(p.astype(BF16), vm_ref[h, pl.ds(start, tk), :],
                                  preferred_element_type=F32))
            acc = acc * jnp.where(first, alpha[0], alpha[1]) + pv[0] + pv[1]
            return tuple(new_m), tuple(new_l), acc

        col0 = jnp.full((tq, 1), NEG, F32)
        zero = jnp.zeros((tq, 1), F32)
        carry = ((col0, col0), (zero, zero), jnp.zeros((tq, PAIR), F32))
        carry = lax.fori_loop(0, i, lambda kb, cr: step(kb, cr, False), carry)
        m, l, acc = step(i, carry, True)

        o = acc / jnp.where(first, l[0], l[1])
        o_ref[...] = o.astype(BF16)
        gate = ag_ref[...].astype(F32)
        ma_ref[...] = (o * (gate * _sigmoid(gate))).astype(BF16)
        rows = [jnp.transpose(jnp.broadcast_to(m[h] + jnp.log(l[h]), (tq, 128)))[0:8, :] for h in range(2)]
        sub = lax.broadcasted_iota(jnp.int32, (8, tq), 0)
        lse_ref[...] = jnp.where(sub == 0, rows[0], rows[1])

    return pl.pallas_call(
        body, name="attn_fwd", grid=(N_PAIRS, s // tq),
        in_specs=[pl.BlockSpec((tq, PAIR), lambda j, i: (i, j)),
                  pl.BlockSpec((s, PAIR), lambda j, i: (0, j)),
                  pl.BlockSpec((s, PAIR), lambda j, i: (0, j)),
                  pl.BlockSpec((None, 2, s), lambda j, i: (j, 0, 0)),
                  pl.BlockSpec((tq, PAIR), lambda j, i: (i, j))],
        out_specs=[pl.BlockSpec((tq, PAIR), lambda j, i: (i, j)),
                   pl.BlockSpec((tq, PAIR), lambda j, i: (i, j)),
                   pl.BlockSpec((None, 8, tq), lambda j, i: (j, 0, i))],
        out_shape=[jax.ShapeDtypeStruct((s, ATTN_WIDTH), BF16), jax.ShapeDtypeStruct((s, ATTN_WIDTH), BF16),
                   jax.ShapeDtypeStruct((N_PAIRS, 8, s), F32)],
        scratch_shapes=[pltpu.VMEM((2, s, PAIR), BF16)],
        compiler_params=_params(("arbitrary", "arbitrary")),
    )(q, k, v, ct, ag)


def _outproj_fwd(x, po, ma, wo, tm=512):
    s = x.shape[0]

    def body(x_ref, po_ref, ma_ref, wo_ref, xn_ref):
        xn_ref[...] = (x_ref[...]
                       + jnp.dot(po_ref[...], wo_ref[0:POOL_WIDTH, :], preferred_element_type=F32)
                       + jnp.dot(ma_ref[...], wo_ref[POOL_WIDTH:, :], preferred_element_type=F32))

    return pl.pallas_call(
        body, name="outproj_fwd", grid=(s // tm,),
        in_specs=[_row_spec(tm, D_MODEL), _row_spec(tm, 512), _row_spec(tm, 512),
                  _full_spec((D_MODEL, D_MODEL))],
        out_specs=_row_spec(tm, D_MODEL),
        out_shape=jax.ShapeDtypeStruct((s, D_MODEL), F32),
        compiler_params=_params(("parallel",)),
    )(x, po, ma, wo)


def _loss_head(x, tgt, g, tm=512):
    s = x.shape[0]

    def body(x_ref, t_ref, g_ref, dx_ref, dg_ref, loss_ref):
        @pl.when(pl.program_id(0) == 0)
        def _():
            dg_ref[...] = jnp.zeros_like(dg_ref)
            loss_ref[...] = jnp.zeros_like(loss_ref)

        xf = x_ref[...]
        r = lax.rsqrt(jnp.mean(xf * xf, axis=-1, keepdims=True) + RMS_EPS)
        xh = xf * r
        gg = g_ref[...]
        e = xh * gg - t_ref[...]
        loss_ref[...] += 0.5 * jnp.sum(jnp.mean(e * e, axis=-1, keepdims=True))
        dy = e * (1.0 / D_MODEL)
        u = dy * gg
        dx_ref[...] = r * (u - xh * jnp.mean(xh * u, axis=-1, keepdims=True))
        dg_ref[...] += jnp.sum(dy * xh, axis=0, keepdims=True)

    return pl.pallas_call(
        body, name="loss_head", grid=(s // tm,),
        in_specs=[_row_spec(tm, D_MODEL), _row_spec(tm, D_MODEL), _full_spec((1, D_MODEL))],
        out_specs=[_row_spec(tm, D_MODEL), _full_spec((1, D_MODEL)), _full_spec((8, 128))],
        out_shape=[jax.ShapeDtypeStruct((s, D_MODEL), F32), jax.ShapeDtypeStruct((1, D_MODEL), F32),
                   jax.ShapeDtypeStruct((8, 128), F32)],
        compiler_params=_params(("arbitrary",)),
    )(x, tgt, g)


def _outproj_bwd(dx, po, ma, ag, o, wo, head_sel, tm=512):
    s = dx.shape[0]

    def body(dx_ref, po_ref, ma_ref, ag_ref, o_ref, wo_ref, sel_ref, dmp_ref, do_ref, dag_ref, dt_ref, gwo_ref):
        @pl.when(pl.program_id(0) == 0)
        def _():
            gwo_ref[...] = jnp.zeros_like(gwo_ref)

        dxb = dx_ref[...].astype(BF16)
        dm = lax.dot_general(dxb, wo_ref[...], NT, preferred_element_type=F32)
        dmp_ref[...] = dm[:, 0:POOL_WIDTH].astype(BF16)
        dma = dm[:, POOL_WIDTH:]
        gate = ag_ref[...].astype(F32)
        of = o_ref[...].astype(F32)
        sg = _sigmoid(gate)
        do = dma * (gate * sg)
        do_ref[...] = do.astype(BF16)
        dag_ref[...] = (dma * of * (sg * (1.0 + gate * (1.0 - sg)))).astype(BF16)
        prod = do * of
        hi = prod.astype(BF16)
        lo = (prod - hi.astype(F32)).astype(BF16)
        dt_ref[...] = (lax.dot_general(sel_ref[...], hi, NT, preferred_element_type=F32)
                       + lax.dot_general(sel_ref[...], lo, NT, preferred_element_type=F32))
        gwo_ref[0:POOL_WIDTH, :] += lax.dot_general(po_ref[...], dxb, TN, preferred_element_type=F32)
        gwo_ref[POOL_WIDTH:, :] += lax.dot_general(ma_ref[...], dxb, TN, preferred_element_type=F32)

    act = jax.ShapeDtypeStruct((s, 512), BF16)
    return pl.pallas_call(
        body, name="outproj_bwd", grid=(s // tm,),
        in_specs=[_row_spec(tm, D_MODEL)] + [_row_spec(tm, 512)] * 4
                 + [_full_spec((D_MODEL, D_MODEL)), _full_spec((8, ATTN_WIDTH))],
        out_specs=[_row_spec(tm, 512)] * 3 + [pl.BlockSpec((8, tm), lambda i: (0, i)),
                                              _full_spec((D_MODEL, D_MODEL))],
        out_shape=[act, act, act, jax.ShapeDtypeStruct((8, s), F32),
                   jax.ShapeDtypeStruct((D_MODEL, D_MODEL), F32)],
        compiler_params=_params(("arbitrary",)),
    )(dx, po, ma, ag, o, wo, head_sel)


def _attn_bwd(q, k, v, do, c, lse, dt, tq=256):
    s = q.shape[0]
    tk = tq
    nq = s // tq

    def body(q_ref, k_ref, v_ref, do_ref, c_ref, lse_ref, dt_ref, dq_ref, dk_ref, dv_ref, dcs_ref, drs_ref,
             qm_ref, dox_ref, dqt_ref, dkx_ref, dvx_ref, csum_ref, rsum_ref):
        j = pl.program_id(0)
        b = pl.program_id(1)

        @pl.when(b == 0)
        def _():
            qq = q_ref[...]
            dd = do_ref[...]
            lane = lax.broadcasted_iota(jnp.int32, qq.shape, 1)
            zq = jnp.zeros_like(qq)
            qm_ref[0] = jnp.where(lane < HEAD_DIM, qq, zq)
            qm_ref[1] = jnp.where(lane >= HEAD_DIM, qq, zq)
            dox_ref[0] = jnp.where(lane < HEAD_DIM, dd, zq)
            dox_ref[1] = jnp.where(lane >= HEAD_DIM, dd, zq)
            dqt_ref[...] = jnp.zeros_like(dqt_ref)
            rsum_ref[...] = jnp.zeros_like(rsum_ref)

        kblk = k_ref[...]
        vblk = v_ref[...]
        kt = jnp.transpose(kblk.astype(F32))
        sub = lax.broadcasted_iota(jnp.int32, kt.shape, 0)
        ktm = (jnp.where(sub < HEAD_DIM, kt, 0.0).astype(BF16), jnp.where(sub >= HEAD_DIM, kt, 0.0).astype(BF16))
        cblk = c_ref[...]
        lane_c = lax.broadcasted_iota(jnp.int32, cblk.shape, 1)
        ccol = [jnp.sum(jnp.where(lane_c == 2 * j + h, cblk, 0.0), axis=1, keepdims=True) for h in range(2)]
        dkx_ref[...] = jnp.zeros_like(dkx_ref)
        dvx_ref[...] = jnp.zeros_like(dvx_ref)
        csum_ref[...] = jnp.zeros_like(csum_ref)

        def step(qb, masked):
            start = pl.multiple_of(qb * tq, tq)
            rows = pl.ds(start, tq)
            for h in range(2):
                qm = qm_ref[h, rows, :]
                dox = dox_ref[h, rows, :]
                arg = (lax.dot_general(kblk, qm, NT, preferred_element_type=F32)
                       - ccol[h] - lse_ref[pl.ds(h, 1), rows])
                if masked:
                    krow = lax.broadcasted_iota(jnp.int32, (tk, tq), 0)
                    qcol = lax.broadcasted_iota(jnp.int32, (tk, tq), 1)
                    arg = jnp.where(qcol >= krow, arg, NEG)
                pt = jnp.exp(arg)
                dpt = lax.dot_general(vblk, dox, NT, preferred_element_type=F32)
                ds = pt * (dpt - dt_ref[pl.ds(h, 1), rows])
                part = ds[:, 0:128]
                for t in range(1, tq // 128):
                    part = part + ds[:, t * 128:(t + 1) * 128]
                csum_ref[h] += part
                rsum_ref[h, :, rows] += jnp.sum(ds.reshape(tk // 8, 8, tq), axis=0)
                dst = ds.astype(BF16)
                dvx_ref[...] += jnp.dot(pt.astype(BF16), dox, preferred_element_type=F32)
                dkx_ref[...] += jnp.dot(dst, qm, preferred_element_type=F32)
                dqt_ref[:, rows] += jnp.dot(ktm[h], dst, preferred_element_type=F32)

        step(b, True)

        def loop_body(qb, carry):
            step(qb, False)
            return carry

        lax.fori_loop(b + 1, nq, loop_body, 0)

        dv_ref[...] = dvx_ref[...].astype(BF16)
        dk_ref[...] = dkx_ref[...].astype(BF16)
        lane_o = lax.broadcasted_iota(jnp.int32, (tk, 128), 1)
        col = [jnp.sum(csum_ref[h], axis=1, keepdims=True) for h in range(2)]
        dcs_ref[...] = jnp.where(lane_o == 0, col[0], jnp.where(lane_o == 1, col[1], 0.0))

        @pl.when(b == nq - 1)
        def _():
            sub_o = lax.broadcasted_iota(jnp.int32, (8, s), 0)
            row = [jnp.sum(rsum_ref[h], axis=0, keepdims=True) for h in range(2)]
            drs_ref[...] = jnp.where(sub_o == 0, row[0], jnp.where(sub_o == 1, row[1], 0.0))
            for ch in range(s // 512):
                rows = pl.ds(ch * 512, 512)
                dq_ref[rows, :] = (jnp.transpose(dqt_ref[:, rows]) * Q_SCALE).astype(BF16)

    act = jax.ShapeDtypeStruct((s, ATTN_WIDTH), BF16)
    pair_rows = pl.BlockSpec((s, PAIR), lambda j, b: (0, j))
    pair_blk = pl.BlockSpec((tk, PAIR), lambda j, b: (b, j))
    stat = pl.BlockSpec((None, 2, s), lambda j, b: (j, 0, 0))
    return pl.pallas_call(
        body, name="attn_bwd", grid=(N_PAIRS, s // tk),
        in_specs=[pair_rows, pair_blk, pair_blk, pair_rows,
                  pl.BlockSpec((tk, 128), lambda j, b: (b, 0)),
                  pl.BlockSpec((None, 8, s), lambda j, b: (j, 0, 0)), stat],
        out_specs=[pair_rows, pair_blk, pair_blk, pair_blk, pl.BlockSpec((None, 8, s), lambda j, b: (j, 0, 0))],
        out_shape=[act, act, act, jax.ShapeDtypeStruct((s, N_PAIRS * 128), F32),
                   jax.ShapeDtypeStruct((N_PAIRS, 8, s), F32)],
        scratch_shapes=[pltpu.VMEM((2, s, PAIR), BF16), pltpu.VMEM((2, s, PAIR), BF16),
                        pltpu.VMEM((PAIR, s), F32), pltpu.VMEM((tk, PAIR), F32),
                        pltpu.VMEM((tk, PAIR), F32), pltpu.VMEM((2, tk, 128), F32),
                        pltpu.VMEM((2, 8, s), F32)],
        compiler_params=_params(("arbitrary", "arbitrary")),
    )(q, k, v, do, c, lse, dt)


def _forget_bwd(dcs, drs, lf, tm=256):
    s = lf.shape[0]
    n = s // tm

    def body(dcs_ref, drs_ref, lf_ref, dpf_ref, dfb_ref, carry_ref):
        @pl.when(pl.program_id(0) == 0)
        def _():
            carry_ref[...] = jnp.zeros_like(carry_ref)
            dfb_ref[...] = jnp.zeros_like(dfb_ref)

        lane = lax.broadcasted_iota(jnp.int32, (tm, 128), 1)
        dc = drs_ref[...]
        for pj in range(N_PAIRS):
            blk = dcs_ref[:, pj * 128:(pj + 1) * 128]
            if pj:
                blk = pltpu.roll(blk, 2 * pj, 1)
            dc = dc - jnp.where((lane == 2 * pj) | (lane == 2 * pj + 1), blk, 0.0)
        dlf = _scan_rows(dc, reverse=True) + carry_ref[0:1, :]
        carry_ref[...] = dlf[0:8, :]
        dz = jnp.where(lane < 8, dlf * (1.0 - jnp.exp(lf_ref[...])), 0.0)
        dpf_ref[...] = dz.astype(BF16)
        dfb_ref[...] += jnp.sum(dz, axis=0, keepdims=True)

    return pl.pallas_call(
        body, name="forget_bwd", grid=(n,),
        in_specs=[pl.BlockSpec((tm, N_PAIRS * 128), lambda i: (n - 1 - i, 0)),
                  pl.BlockSpec((tm, 128), lambda i: (n - 1 - i, 0)),
                  pl.BlockSpec((tm, 128), lambda i: (n - 1 - i, 0))],
        out_specs=[pl.BlockSpec((tm, 128), lambda i: (n - 1 - i, 0)), _full_spec((1, 128))],
        out_shape=[jax.ShapeDtypeStruct((s, 128), BF16), jax.ShapeDtypeStruct((1, 128), F32)],
        scratch_shapes=[pltpu.VMEM((8, 128), F32)],
        compiler_params=_params(("arbitrary",)),
    )(dcs, drs, lf)


def _pool_bwd(pu, pg, dmp, pw, ps, tm=512):
    s = pu.shape[0]
    hb = tm // HALO
    n = s // tm
    last_halo = s // HALO - 1

    def body(pu_ref, halo_ref, pg_ref, dmp_ref, pgn_ref, dmpn_ref, pw_ref, ps_ref,
             dpu_ref, dpg_ref, gpw_ref, gps_ref):
        i = pl.program_id(0)

        @pl.when(i == 0)
        def _():
            gpw_ref[...] = jnp.zeros_like(gpw_ref)
            gps_ref[...] = jnp.zeros_like(gps_ref)

        u = pu_ref[...].astype(F32)
        halo = jnp.where(i == 0, 0.0, halo_ref[...].astype(F32))
        d = _pool_delta(u, halo, i * tm)
        cnt = _window_counts(i * tm, tm)
        cnt_next = _window_counts((i + 1) * tm, HALO)
        e_parts, dd_parts = [], []
        for gi in range(len(POOL_WINDOWS)):
            cols = slice(gi * GROUP_DIM, (gi + 1) * GROUP_DIM)
            wg = pw_ref[gi]
            scale = ps_ref[:, cols]
            db = d[gi].astype(BF16)
            z = jnp.dot(db, wg, preferred_element_type=F32)
            gate = pg_ref[:, cols].astype(F32)
            sg = _sigmoid(gate)
            dm = dmp_ref[:, cols].astype(F32)
            dy = dm * (gate * sg)
            dpg_ref[:, cols] = (dm * (z * scale) * (sg * (1.0 + gate * (1.0 - sg)))).astype(BF16)
            gps_ref[:, cols] += jnp.sum(dy * z, axis=0, keepdims=True)
            dz = (dy * scale).astype(BF16)
            gpw_ref[gi] += lax.dot_general(db, dz, TN, preferred_element_type=F32)
            dd = lax.dot_general(dz, wg, NT, preferred_element_type=F32)
            gate_n = pgn_ref[:, cols].astype(F32)
            dz_n = (dmpn_ref[:, cols].astype(F32) * (gate_n * _sigmoid(gate_n)) * scale).astype(BF16)
            dd_n = lax.dot_general(dz_n, wg, NT, preferred_element_type=F32)
            dd_n = jnp.where(i == n - 1, 0.0, dd_n)
            dd_parts.append(dd)
            e_parts.append(jnp.concatenate([dd / cnt[gi], dd_n / cnt_next[gi]], axis=0))
        lead = _window_sums(jnp.concatenate(e_parts, axis=1), forward=True)
        for gi in range(len(POOL_WINDOWS)):
            cols = slice(gi * GROUP_DIM, (gi + 1) * GROUP_DIM)
            dpu_ref[:, cols] = (lead[gi][0:tm, :] - dd_parts[gi]).astype(BF16)

    act = jax.ShapeDtypeStruct((s, 512), BF16)
    prev_halo = pl.BlockSpec((HALO, 512), lambda i: (jnp.maximum(i * hb - 1, 0), 0))
    next_halo = pl.BlockSpec((HALO, 512), lambda i: (jnp.minimum((i + 1) * hb, last_halo), 0))
    return pl.pallas_call(
        body, name="pool_bwd", grid=(n,),
        in_specs=[_row_spec(tm, 512), prev_halo, _row_spec(tm, 512), _row_spec(tm, 512), next_halo, next_halo,
                  _full_spec((4, GROUP_DIM, GROUP_DIM)), _full_spec((1, 512))],
        out_specs=[_row_spec(tm, 512), _row_spec(tm, 512), _full_spec((4, GROUP_DIM, GROUP_DIM)),
                   _full_spec((1, 512))],
        out_shape=[act, act, jax.ShapeDtypeStruct((4, GROUP_DIM, GROUP_DIM), F32),
                   jax.ShapeDtypeStruct((1, 512), F32)],
        compiler_params=_params(("arbitrary",)),
    )(pu, pu, pg, dmp, pg, dmp, pw, ps)


def _inproj_bwd_x(dsec, w, x, g, dxo, tm=512):
    s = x.shape[0]

    def body(*refs):
        d_refs = refs[0:7]
        w_ref, x_ref, g_ref, dxo_ref, dxi_ref, h_ref, dg_ref = refs[7:]

        @pl.when(pl.program_id(0) == 0)
        def _():
            dg_ref[...] = jnp.zeros_like(dg_ref)

        dh = None
        for d_ref, (off, ncol) in zip(d_refs, SECTIONS):
            t = lax.dot_general(d_ref[...], w_ref[:, off:off + ncol], NT, preferred_element_type=F32)
            dh = t if dh is None else dh + t
        xf = x_ref[...]
        r = lax.rsqrt(jnp.mean(xf * xf, axis=-1, keepdims=True) + RMS_EPS)
        xh = xf * r
        gg = g_ref[...]
        h_ref[...] = (xh * gg).astype(BF16)
        u = dh * gg
        dxi_ref[...] = dxo_ref[...] + r * (u - xh * jnp.mean(xh * u, axis=-1, keepdims=True))
        dg_ref[...] += jnp.sum(dh * xh, axis=0, keepdims=True)

    return pl.pallas_call(
        body, name="inproj_bwd_x", grid=(s // tm,),
        in_specs=[_row_spec(tm, ncol) for _, ncol in SECTIONS]
                 + [_full_spec((D_MODEL, IN_COLS_PAD)), _row_spec(tm, D_MODEL), _full_spec((1, D_MODEL)),
                    _row_spec(tm, D_MODEL)],
        out_specs=[_row_spec(tm, D_MODEL), _row_spec(tm, D_MODEL), _full_spec((1, D_MODEL))],
        out_shape=[jax.ShapeDtypeStruct((s, D_MODEL), F32), jax.ShapeDtypeStruct((s, D_MODEL), BF16),
                   jax.ShapeDtypeStruct((1, D_MODEL), F32)],
        compiler_params=_params(("arbitrary",)),
    )(*dsec, w, x, g, dxo)


def _inproj_bwd_w(h, dsec, tm=512):
    s = h.shape[0]

    def body(*refs):
        h_ref = refs[0]
        d_refs = refs[1:8]
        gw_ref = refs[8]

        @pl.when(pl.program_id(0) == 0)
        def _():
            gw_ref[...] = jnp.zeros_like(gw_ref)

        hh = h_ref[...]
        for d_ref, (off, ncol) in zip(d_refs, SECTIONS):
            gw_ref[:, off:off + ncol] += lax.dot_general(hh, d_ref[...], TN, preferred_element_type=F32)

    return pl.pallas_call(
        body, name="inproj_bwd_w", grid=(s // tm,),
        in_specs=[_row_spec(tm, D_MODEL)] + [_row_spec(tm, ncol) for _, ncol in SECTIONS],
        out_specs=_full_spec((D_MODEL, IN_COLS_PAD)),
        out_shape=jax.ShapeDtypeStruct((D_MODEL, IN_COLS_PAD), F32),
        compiler_params=_params(("arbitrary",), vmem_mb=56),
    )(h, *dsec)


def _elementwise(fn, name, n_out, arrays, tm):
    rows, cols = arrays[0].shape
    n_in = len(arrays)

    def body(*refs):
        outs = fn(*[r[...] for r in refs[:n_in]])
        for r, val in zip(refs[n_in:], outs):
            r[...] = val

    spec = _row_spec(tm, cols)
    shape = jax.ShapeDtypeStruct((rows, cols), F32)
    return pl.pallas_call(
        body, name=name, grid=(rows // tm,),
        in_specs=[spec] * n_in, out_specs=[spec] * n_out, out_shape=[shape] * n_out,
        compiler_params=_params(("parallel",)),
    )(*arrays)


def _add2(a, b, name, tm):
    return _elementwise(lambda p, q: (p + q,), name, 1, [a, b], tm)[0]


def _sum4(parts, name, tm):
    _, rows, cols = parts.shape

    def body(p_ref, o_ref):
        o_ref[...] = (p_ref[0] + p_ref[1]) + (p_ref[2] + p_ref[3])

    return pl.pallas_call(
        body, name=name, grid=(rows // tm,),
        in_specs=[pl.BlockSpec((4, tm, cols), lambda i: (0, i, 0))],
        out_specs=_row_spec(tm, cols), out_shape=jax.ShapeDtypeStruct((rows, cols), F32),
        compiler_params=_params(("parallel",)),
    )(parts)


def _adamw(g, w, m, v, name, tm):
    def fn(g, w, m, v):
        m = ADAM_B1 * m + (1.0 - ADAM_B1) * g
        v = ADAM_B2 * v + (1.0 - ADAM_B2) * (g * g)
        m_hat = m / (1.0 - ADAM_B1 ** ADAM_STEP)
        v_hat = v / (1.0 - ADAM_B2 ** ADAM_STEP)
        delta = -ADAM_LR * (m_hat / (jnp.sqrt(v_hat) + ADAM_EPS) + ADAM_WD * w)
        return g, delta, m, v

    return _elementwise(fn, name, 4, [g, w, m, v], tm)


def _position():
    x, y, c = lax.axis_index("x"), lax.axis_index("y"), lax.axis_index("c")
    other_chips = [(1 - x, y), (x, 1 - y), (1 - x, 1 - y)]
    return x, y, c, other_chips


def _remote(src, dst, sems, k, to):
    send_sems, recv_sems = sems
    return pltpu.make_async_remote_copy(src_ref=src, dst_ref=dst, send_sem=send_sems.at[k],
                                        recv_sem=recv_sems.at[k], device_id=to, device_id_type=MESH)


def _comm_call(body, name, n_in, out_shape, n_remote, n_local):
    return pl.pallas_call(
        body, name=name, in_specs=[ANY] * n_in, out_specs=[ANY] * len(out_shape), out_shape=out_shape,
        scratch_shapes=[pltpu.SemaphoreType.DMA((n_remote,)), pltpu.SemaphoreType.DMA((n_remote,)),
                        pltpu.SemaphoreType.DMA((n_local,))],
    )


def _gather_weights(wi, wo):
    def body(wi_ref, wo_ref, gi_ref, go_ref, send_sems, recv_sems, local_sems):
        x, y, c, chips = _position()
        sems = (send_sems, recv_sems)
        me = 2 * x + y
        mine, other = pl.ds(2 * c, 2), pl.ds(2 * (1 - c), 2)
        own = [pltpu.make_async_copy(wi_ref, gi_ref.at[me], local_sems.at[0]),
               pltpu.make_async_copy(wo_ref, go_ref.at[:, me], local_sems.at[1])]
        for cp in own:
            cp.start()
        sent = []
        for k, (cx, cy) in enumerate(chips):
            sent.append(_remote(wi_ref.at[mine], gi_ref.at[me, mine], sems, 2 * k, (cx, cy, c)))
            sent.append(_remote(wo_ref.at[mine], go_ref.at[mine, me], sems, 2 * k + 1, (cx, cy, c)))
        for cp in sent:
            cp.start()
        for k, (cx, cy) in enumerate(chips):
            ck = 2 * cx + cy
            for a, blk in ((0, gi_ref.at[ck, mine]), (1, go_ref.at[mine, ck])):
                _remote(blk, blk, sems, 2 * k + a, (x, y, c)).wait_recv()
                fwd = _remote(blk, blk, sems, 6 + 2 * k + a, (x, y, 1 - c))
                fwd.start()
                sent.append(fwd)
        for k, (cx, cy) in enumerate(chips):
            ck = 2 * cx + cy
            for a, blk in ((0, gi_ref.at[ck, other]), (1, go_ref.at[other, ck])):
                _remote(blk, blk, sems, 6 + 2 * k + a, (x, y, c)).wait_recv()
        for cp in sent:
            cp.wait_send()
        for cp in own:
            cp.wait()

    out_shape = [jax.ShapeDtypeStruct((N_CHIPS,) + wi.shape, wi.dtype),
                 jax.ShapeDtypeStruct((DEPTH, N_CHIPS) + wo.shape[1:], wo.dtype)]
    return _comm_call(body, "gather_weights", 2, out_shape, 12, 2)(wi, wo)


def _exchange_sibling(gi, go, sm):
    def body(gi_ref, go_ref, sm_ref, ri_ref, ro_ref, rs_ref, send_sems, recv_sems, local_sems):
        x, y, c, _ = _position()
        sems = (send_sems, recv_sems)
        other = pl.ds(2 * (1 - c), 2)
        cps = [_remote(gi_ref.at[:, other], ri_ref, sems, 0, (x, y, 1 - c)),
               _remote(go_ref.at[other], ro_ref, sems, 1, (x, y, 1 - c)),
               _remote(sm_ref, rs_ref, sems, 2, (x, y, 1 - c))]
        for cp in cps:
            cp.start()
        for cp in cps:
            cp.wait()

    out_shape = [jax.ShapeDtypeStruct((N_CHIPS, 2) + gi.shape[2:], F32),
                 jax.ShapeDtypeStruct((2,) + go.shape[1:], F32),
                 jax.ShapeDtypeStruct(sm.shape, F32)]
    return _comm_call(body, "exchange_sibling", 3, out_shape, 3, 1)(gi, go, sm)


def _scatter_chips(pi, po, ps):
    def body(pi_ref, po_ref, ps_ref, ri_ref, ro_ref, rs_ref, send_sems, recv_sems, local_sems):
        x, y, c, chips = _position()
        sems = (send_sems, recv_sems)
        me = 2 * x + y
        half = pl.ds(SMALL_HALF * c, SMALL_HALF)
        own = [pltpu.make_async_copy(pi_ref.at[me], ri_ref.at[me], local_sems.at[0]),
               pltpu.make_async_copy(po_ref.at[:, me], ro_ref.at[me], local_sems.at[1]),
               pltpu.make_async_copy(ps_ref.at[half], rs_ref.at[me], local_sems.at[2])]
        for cp in own:
            cp.start()
        sent = []
        for k, (cx, cy) in enumerate(chips):
            ck = 2 * cx + cy
            sent.append(_remote(pi_ref.at[ck], ri_ref.at[me], sems, 3 * k, (cx, cy, c)))
            sent.append(_remote(po_ref.at[:, ck], ro_ref.at[me], sems, 3 * k + 1, (cx, cy, c)))
            sent.append(_remote(ps_ref.at[half], rs_ref.at[me], sems, 3 * k + 2, (cx, cy, c)))
        for cp in sent:
            cp.start()
        for k, (cx, cy) in enumerate(chips):
            ck = 2 * cx + cy
            for a, blk in ((0, ri_ref.at[ck]), (1, ro_ref.at[ck]), (2, rs_ref.at[ck])):
                _remote(blk, blk, sems, 3 * k + a, (x, y, c)).wait_recv()
        for cp in sent:
            cp.wait_send()
        for cp in own:
            cp.wait()

    out_shape = [jax.ShapeDtypeStruct(pi.shape, F32),
                 jax.ShapeDtypeStruct((N_CHIPS, 2) + po.shape[2:], F32),
                 jax.ShapeDtypeStruct((N_CHIPS, SMALL_HALF, 128), F32)]
    return _comm_call(body, "scatter_chips", 3, out_shape, 9, 3)(pi, po, ps)


def _join_sibling(hi, ho, hs):
    def body(hi_ref, ho_ref, hs_ref, fi_ref, fo_ref, fs_ref, send_sems, recv_sems, local_sems):
        x, y, c, _ = _position()
        sems = (send_sems, recv_sems)
        mine, other = pl.ds(2 * c, 2), pl.ds(2 * (1 - c), 2)
        half, other_half = pl.ds(SMALL_HALF * c, SMALL_HALF), pl.ds(SMALL_HALF * (1 - c), SMALL_HALF)
        own = [pltpu.make_async_copy(hi_ref, fi_ref.at[mine], local_sems.at[0]),
               pltpu.make_async_copy(ho_ref, fo_ref.at[mine], local_sems.at[1]),
               pltpu.make_async_copy(hs_ref, fs_ref.at[half], local_sems.at[2])]
        for cp in own:
            cp.start()
        sent = [_remote(hi_ref, fi_ref.at[mine], sems, 0, (x, y, 1 - c)),
                _remote(ho_ref, fo_ref.at[mine], sems, 1, (x, y, 1 - c)),
                _remote(hs_ref, fs_ref.at[half], sems, 2, (x, y, 1 - c))]
        for cp in sent:
            cp.start()
        for a, blk in ((0, fi_ref.at[other]), (1, fo_ref.at[other]), (2, fs_ref.at[other_half])):
            _remote(blk, blk, sems, a, (x, y, c)).wait_recv()
        for cp in sent:
            cp.wait_send()
        for cp in own:
            cp.wait()

    out_shape = [jax.ShapeDtypeStruct((DEPTH,) + hi.shape[1:], F32),
                 jax.ShapeDtypeStruct((DEPTH,) + ho.shape[1:], F32),
                 jax.ShapeDtypeStruct((SMALL_ROWS, 128), F32)]
    return _comm_call(body, "join_sibling", 3, out_shape, 3, 3)(hi, ho, hs)


def _pack_small(norm_g, final_g, pool_scale, forget_bias, pool_w):
    fb = jnp.pad(forget_bias.reshape(-1), (0, 128 - forget_bias.size))
    flat = jnp.concatenate([norm_g.reshape(-1), final_g.reshape(-1), pool_scale.reshape(-1), fb,
                            pool_w.reshape(-1)])
    return jnp.pad(flat, (0, SMALL_ROWS * 128 - flat.size)).reshape(SMALL_ROWS, 128)


def _unpack_small(packed):
    flat = packed.reshape(-1)
    sizes = (DEPTH * D_MODEL, D_MODEL, DEPTH * POOL_WIDTH, 128, DEPTH * 4 * GROUP_DIM * GROUP_DIM)
    offs = [0]
    for n in sizes:
        offs.append(offs[-1] + n)
    norm_g, final_g, pool_scale, fb, pool_w = [flat[offs[i]:offs[i + 1]] for i in range(5)]
    return (norm_g.reshape(DEPTH, D_MODEL), final_g, pool_scale.reshape(DEPTH, POOL_WIDTH),
            fb[:DEPTH * 8].reshape(DEPTH, 8), pool_w.reshape(DEPTH, 4, GROUP_DIM, GROUP_DIM))


def kernel(x, norm_g, w_in, forget_bias, pool_w, pool_scale, w_out, final_g, loss_target, m_norm_g, m_w_in, m_forget_bias, m_pool_w, m_pool_scale, m_w_out, m_final_g, v_norm_g, v_w_in, v_forget_bias, v_pool_w, v_pool_scale, v_w_out, v_final_g):
    x0 = x[0]
    tgt = loss_target[0]
    s = x0.shape[0]

    gi, go = _gather_weights(w_in.astype(BF16), w_out.astype(BF16))
    w_in_full = jnp.concatenate([gi[k] for k in range(N_CHIPS)], axis=-1)
    w_in_full = jnp.pad(w_in_full, ((0, 0), (0, 0), (0, IN_COLS_PAD - IN_COLS)))
    w_out_full = go.reshape(DEPTH, D_MODEL, D_MODEL)
    pool_w_b = pool_w.astype(BF16)
    fb_pad = jnp.pad(forget_bias, ((0, 0), (0, 128 - forget_bias.shape[1])))
    head_sel = (jnp.arange(ATTN_WIDTH)[None, :] // HEAD_DIM == jnp.arange(8)[:, None]).astype(BF16)

    saved = []
    xl = x0
    for l in range(DEPTH):
        pu, pg, q, k, v, ag, lf, c, ct = _inproj_fwd(xl, norm_g[l][None], w_in_full[l], fb_pad[l][None])
        po = _pool_fwd(pu, pg, pool_w_b[l], pool_scale[l][None])
        ma, o, lse = _attn_fwd(q, k, v, ct.reshape(N_PAIRS, 2, s), ag)
        saved.append((xl, pu, pg, q, k, v, ag, lf, c, po, ma, o, lse))
        xl = _outproj_fwd(xl, po, ma, w_out_full[l])

    dx, g_final, loss_part = _loss_head(xl, tgt, final_g[None])
    loss = lax.psum(loss_part[0, 0], ("x", "y", "c"))

    g_norm, g_win, g_fb, g_pw, g_ps, g_wo = [], [], [], [], [], []
    for l in reversed(range(DEPTH)):
        xl, pu, pg, q, k, v, ag, lf, c, po, ma, o, lse = saved[l]
        dmp, do, dag, dt, gwo = _outproj_bwd(dx, po, ma, ag, o, w_out_full[l], head_sel)
        dq, dk, dv, dcs, drs = _attn_bwd(q, k, v, do, c, lse, dt.reshape(N_PAIRS, 2, s))
        drs = jnp.pad(drs[:, 0:2, :].reshape(8, s).T, ((0, 0), (0, 120)))
        dpf, dfb = _forget_bwd(dcs, drs, lf)
        dpu, dpg, gpw, gps = _pool_bwd(pu, pg, dmp, pool_w_b[l], pool_scale[l][None])
        dsec = (dpu, dpg, dq, dk, dv, dag, dpf)
        dx, h, gn = _inproj_bwd_x(dsec, w_in_full[l], xl, norm_g[l][None], dx)
        g_win.append(_inproj_bwd_w(h, dsec))
        g_norm.append(gn[0])
        g_fb.append(dfb[0, :8])
        g_pw.append(gpw)
        g_ps.append(gps[0])
        g_wo.append(gwo)
    g_norm, g_win, g_fb, g_pw, g_ps, g_wo = [t[::-1] for t in (g_norm, g_win, g_fb, g_pw, g_ps, g_wo)]

    c_idx = lax.axis_index("c")
    gwin = jnp.stack(g_win)
    send_i = jnp.stack([gwin[:, :, k * SHARD_COLS:(k + 1) * SHARD_COLS] for k in range(N_CHIPS)])
    send_o = jnp.stack(g_wo).reshape(DEPTH, N_CHIPS, SHARD_ROWS, D_MODEL)
    send_s = _pack_small(jnp.stack(g_norm), g_final[0], jnp.stack(g_ps), jnp.stack(g_fb), jnp.stack(g_pw))
    ri, ro, rs = _exchange_sibling(send_i, send_o, send_s)
    keep_i = lax.dynamic_slice_in_dim(send_i, 2 * c_idx, 2, axis=1)
    keep_o = lax.dynamic_slice_in_dim(send_o, 2 * c_idx, 2, axis=0)
    pi = _add2(keep_i.reshape(-1, SHARD_COLS), ri.reshape(-1, SHARD_COLS), "chip_sum_w_in", 256).reshape(ri.shape)
    po_ = _add2(keep_o.reshape(-1, D_MODEL), ro.reshape(-1, D_MODEL), "chip_sum_w_out", 256).reshape(ro.shape)
    ps_ = _add2(send_s, rs, "chip_sum_small", SMALL_ROWS // 4)
    qi, qo, qs = _scatter_chips(pi, po_, ps_)
    hi = _sum4(qi.reshape(N_CHIPS, -1, SHARD_COLS), "mesh_sum_w_in", 256).reshape(2, D_MODEL, SHARD_COLS)
    ho = _sum4(qo.reshape(N_CHIPS, -1, D_MODEL), "mesh_sum_w_out", 256).reshape(2, SHARD_ROWS, D_MODEL)
    hs = _sum4(qs, "mesh_sum_small", SMALL_HALF // 4)
    fi, fo, fs = _join_sibling(hi, ho, hs)

    flat_i = lambda a: a.reshape(-1, SHARD_COLS)
    flat_o = lambda a: a.reshape(-1, D_MODEL)
    out_i = _adamw(flat_i(fi), flat_i(w_in), flat_i(m_w_in), flat_i(v_w_in), "adamw_w_in", 256)
    out_o = _adamw(flat_o(fo), flat_o(w_out), flat_o(m_w_out), flat_o(v_w_out), "adamw_w_out", 256)
    out_s = _adamw(fs, _pack_small(norm_g, final_g, pool_scale, forget_bias, pool_w),
                   _pack_small(m_norm_g, m_final_g, m_pool_scale, m_forget_bias, m_pool_w),
                   _pack_small(v_norm_g, v_final_g, v_pool_scale, v_forget_bias, v_pool_w),
                   "adamw_small", SMALL_ROWS // 4)

    groups = []
    for oi, oo, os_ in zip(out_i, out_o, out_s):
        sn, sf, sp, sb, sw = _unpack_small(os_)
        groups.append((sn, oi.reshape(w_in.shape), sb, sw, sp, oo.reshape(w_out.shape), sf))
    return (loss, dx[None]) + tuple(t for grp in groups for t in grp)
```

```python
import jax
import jax.numpy as jnp
from jax import lax
from jax.experimental import pallas as pl
from jax.experimental.pallas import tpu as pltpu

F32 = jnp.float32
BF16 = jnp.bfloat16

D_MODEL = 1024
DEPTH = 4
POOL_WIDTH = 512
ATTN_WIDTH = 512
HEAD_DIM = 64
PAIR = 2 * HEAD_DIM
N_PAIRS = ATTN_WIDTH // PAIR
POOL_WINDOWS = (2, 4, 8, 16)
GROUP_DIM = 128
HALO = 16
IN_COLS = 3080
OFF_F = 3072
IN_COLS_PAD = 3200
SECTIONS = ((0, 512), (512, 512), (1024, 512), (1536, 512), (2048, 512), (2560, 512), (OFF_F, 128))
N_CHIPS = 4
SHARD_COLS = IN_COLS // N_CHIPS
SHARD_ROWS = D_MODEL // N_CHIPS
RMS_EPS = 1e-6
NEG = -1e30
Q_SCALE = 0.125

ADAM_LR = 0.001
ADAM_B1 = 0.9
ADAM_B2 = 0.999
ADAM_EPS = 1e-08
ADAM_WD = 0.01
ADAM_STEP = 10

SMALL_ROWS = 2112
SMALL_HALF = SMALL_ROWS // 2

NT = (((1,), (1,)), ((), ()))
TN = (((0,), (0,)), ((), ()))
MESH = pl.DeviceIdType.MESH
ANY = pl.BlockSpec(memory_space=pl.ANY)


def _params(semantics, vmem_mb=48):
    return pltpu.CompilerParams(dimension_semantics=semantics, vmem_limit_bytes=vmem_mb << 20)


def _row_spec(tm, cols):
    return pl.BlockSpec((tm, cols), lambda i: (i, 0))


def _full_spec(shape):
    return pl.BlockSpec(shape, lambda *_: (0,) * len(shape))


def _sigmoid(x):
    return jax.nn.sigmoid(x)


def _scan_rows(a, reverse=False):
    n = a.shape[0]
    row = lax.broadcasted_iota(jnp.int32, a.shape, 0)
    k = 1
    while k < n:
        if reverse:
            a = a + jnp.where(row < n - k, pltpu.roll(a, n - k, 0), 0.0)
        else:
            a = a + jnp.where(row >= k, pltpu.roll(a, k, 0), 0.0)
        k *= 2
    return a


def _inproj_fwd(x, g, w, fb, tm=256):
    s = x.shape[0]

    def body(x_ref, g_ref, w_ref, fb_ref, pu_ref, pg_ref, q_ref, k_ref, v_ref, ag_ref,
             lf_ref, c_ref, carry_ref):
        @pl.when(pl.program_id(0) == 0)
        def _():
            carry_ref[...] = jnp.zeros_like(carry_ref)

        xf = x_ref[...]
        r = lax.rsqrt(jnp.mean(xf * xf, axis=-1, keepdims=True) + RMS_EPS)
        h = (xf * r * g_ref[...]).astype(BF16)

        def proj(sec):
            off, n = SECTIONS[sec]
            return jnp.dot(h, w_ref[:, off:off + n], preferred_element_type=F32)

        pu_ref[...] = proj(0).astype(BF16)
        pg_ref[...] = proj(1).astype(BF16)
        q_ref[...] = (proj(2) * Q_SCALE).astype(BF16)
        k_ref[...] = proj(3).astype(BF16)
        v_ref[...] = proj(4).astype(BF16)
        ag_ref[...] = proj(5).astype(BF16)
        z = proj(6) + fb_ref[...]
        lf = jnp.minimum(z, 0.0) - jnp.log(1.0 + jnp.exp(-jnp.abs(z)))
        lf_ref[...] = lf
        c_ref[...] = _scan_rows(lf) + carry_ref[0:1, :]
        carry_ref[0:1, :] = c_ref[tm - 1:tm, :]

    act = jax.ShapeDtypeStruct((s, 512), BF16)
    return pl.pallas_call(
        body, name="inproj_fwd", grid=(s // tm,),
        in_specs=[_row_spec(tm, D_MODEL), _full_spec((1, D_MODEL)), _full_spec((D_MODEL, IN_COLS_PAD)),
                  _full_spec((1, 128))],
        out_specs=[_row_spec(tm, 512)] * 6 + [_row_spec(tm, 128), _row_spec(tm, 128)],
        out_shape=[act] * 6 + [jax.ShapeDtypeStruct((s, 128), F32), jax.ShapeDtypeStruct((s, 128), F32)],
        scratch_shapes=[pltpu.VMEM((8, 128), F32)],
        compiler_params=_params(("arbitrary",)),
    )(x, g, w, fb)


def _window_sums(ext, forward):
    n = ext.shape[0]
    outs = []
    for gi, w in enumerate(POOL_WINDOWS):
        a = ext[:, gi * GROUP_DIM:(gi + 1) * GROUP_DIM]
        k = 1
        while k < w:
            a = a + pltpu.roll(a, (n - k) if forward else k, 0)
            k *= 2
        outs.append(a)
    return outs


def _window_counts(row0, tm):
    t = row0 + lax.broadcasted_iota(jnp.int32, (tm, GROUP_DIM), 0)
    return [jnp.minimum(t + 1, w).astype(F32) for w in POOL_WINDOWS]


def _pool_delta(u, halo, row0):
    tm = u.shape[0]
    sums = _window_sums(jnp.concatenate([halo, u], axis=0), forward=False)
    cnt = _window_counts(row0, tm)
    return [sums[gi][HALO:, :] / cnt[gi] - u[:, gi * GROUP_DIM:(gi + 1) * GROUP_DIM]
            for gi in range(len(POOL_WINDOWS))]


def _pool_fwd(pu, pg, pw, ps, tm=512):
    s = pu.shape[0]
    hb = tm // HALO

    def body(pu_ref, halo_ref, pg_ref, pw_ref, ps_ref, po_ref):
        i = pl.program_id(0)
        u = pu_ref[...].astype(F32)
        halo = jnp.where(i == 0, 0.0, halo_ref[...].astype(F32))
        d = _pool_delta(u, halo, i * tm)
        for gi in range(len(POOL_WINDOWS)):
            cols = slice(gi * GROUP_DIM, (gi + 1) * GROUP_DIM)
            z = jnp.dot(d[gi].astype(BF16), pw_ref[gi], preferred_element_type=F32)
            gate = pg_ref[:, cols].astype(F32)
            po_ref[:, cols] = (z * ps_ref[:, cols] * (gate * _sigmoid(gate))).astype(BF16)

    return pl.pallas_call(
        body, name="pool_fwd", grid=(s // tm,),
        in_specs=[_row_spec(tm, 512),
                  pl.BlockSpec((HALO, 512), lambda i: (jnp.maximum(i * hb - 1, 0), 0)),
                  _row_spec(tm, 512), _full_spec((4, GROUP_DIM, GROUP_DIM)), _full_spec((1, 512))],
        out_specs=_row_spec(tm, 512),
        out_shape=jax.ShapeDtypeStruct((s, 512), BF16),
        compiler_params=_params(("parallel",)),
    )(pu, pu, pg, pw, ps)


def _split3(x):
    hi = x.astype(BF16).astype(F32)
    r1 = x - hi
    lo = r1.astype(BF16).astype(F32)
    return hi, lo, (r1 - lo).astype(BF16).astype(F32)


def _own(idx, h):
    return (idx < HEAD_DIM) if h == 0 else (idx >= HEAD_DIM)


def _spare(h):
    return HEAD_DIM if h == 0 else 0


def _attn_fwd(q, k, v, c, ag, tq=256, tk=512, chunk=512):
    s = q.shape[0]

    def body(q_ref, k_ref, v_ref, c_ref, ag_ref, ma_ref, o_ref, lse_ref, kx_ref, vt_ref, acc0_ref, acc1_ref,
             sta_ref, stb_ref):
        j = pl.program_id(0)
        i = pl.program_id(1)

        @pl.when(i == 0)
        def _():
            def prep(ch, carry):
                rows = pl.ds(pl.multiple_of(ch * chunk, chunk), chunk)
                kk = k_ref[rows, :].astype(F32)
                cc = c_ref[rows, :]
                vt = jnp.transpose(v_ref[rows, :].astype(F32))
                lane = lax.broadcasted_iota(jnp.int32, (chunk, PAIR), 1)
                sub = lax.broadcasted_iota(jnp.int32, (PAIR, chunk), 0)
                for h in range(2):
                    ccol = jnp.sum(jnp.where(lane == 2 * j + h, cc, 0.0), axis=1, keepdims=True)
                    kx = jnp.where(_own(lane, h), kk, 0.0)
                    for t, part in enumerate(_split3(-ccol)):
                        kx = jnp.where(lane == _spare(h) + t, part, kx)
                    kx_ref[h, rows, :] = kx.astype(BF16)
                    vt_ref[h, :, rows] = jnp.where(_own(sub, h), vt,
                                                   jnp.where(sub == _spare(h), 1.0, 0.0)).astype(BF16)
                return carry

            lax.fori_loop(0, s // chunk, prep, 0)

        qq = q_ref[...].astype(F32)
        lane_q = lax.broadcasted_iota(jnp.int32, (tq, PAIR), 1)
        qx = []
        for h in range(2):
            ones = (lane_q >= _spare(h)) & (lane_q < _spare(h) + 3)
            qx.append(jnp.transpose(jnp.where(_own(lane_q, h), qq, jnp.where(ones, 1.0, 0.0))).astype(BF16))
        accs = (acc0_ref, acc1_ref)
        for acc in accs:
            acc[...] = jnp.zeros_like(acc)

        def rows_of(kb):
            return pl.ds(pl.multiple_of(kb * tk, tk), tk)

        def scores(kb, dst):
            for h in range(2):
                dst[h] = jnp.dot(kx_ref[h, rows_of(kb), :], qx[h], preferred_element_type=F32)

        def consume(kb, src, m, masked):
            m_out, alpha, pv = [], [], []
            for h in range(2):
                sh = src[h]
                if masked:
                    kpos = kb * tk + lax.broadcasted_iota(jnp.int32, (tk, tq), 0)
                    qpos = i * tq + lax.broadcasted_iota(jnp.int32, (tk, tq), 1)
                    sh = jnp.where(qpos >= kpos, sh, NEG)
                m_new = jnp.maximum(m[h], jnp.max(sh, axis=0, keepdims=True))
                pt = jnp.exp(sh - m_new).astype(BF16)
                alpha.append(jnp.exp(m[h] - m_new))
                pv.append(jnp.dot(vt_ref[h, :, rows_of(kb)], pt, preferred_element_type=F32))
                m_out.append(m_new)
            for h in range(2):
                accs[h][...] = accs[h][...] * alpha[h] + pv[h]
            return tuple(m_out)

        n_full = (i * tq) // tk

        def two_blocks(t, m):
            kb = 2 * t
            scores(kb + 1, stb_ref)
            m = consume(kb, sta_ref, m, False)
            scores(kb + 2, sta_ref)
            return consume(kb + 1, stb_ref, m, False)

        def odd_tail(m):
            scores(n_full, stb_ref)
            m = consume(n_full - 1, sta_ref, m, False)
            return consume(n_full, stb_ref, m, True)

        def even_tail(m):
            return consume(n_full, sta_ref, m, True)

        scores(0, sta_ref)
        m0 = jnp.full((1, tq), NEG, F32)
        m = lax.fori_loop(0, n_full // 2, two_blocks, (m0, m0))
        m = lax.cond(lax.rem(n_full, 2) == 1, odd_tail, even_tail, m)

        sub_o = lax.broadcasted_iota(jnp.int32, (PAIR, tq), 0)
        den = [accs[h][_spare(h):_spare(h) + 1, :] for h in range(2)]
        ot = jnp.where(sub_o < HEAD_DIM, acc0_ref[...] / den[0], acc1_ref[...] / den[1])
        o = jnp.transpose(ot)
        o_ref[...] = o.astype(BF16)
        gate = ag_ref[...].astype(F32)
        ma_ref[...] = (o * (gate * _sigmoid(gate))).astype(BF16)
        sub8 = lax.broadcasted_iota(jnp.int32, (8, tq), 0)
        lse_ref[...] = jnp.where(sub8 == 0, m[0] + jnp.log(den[0]), m[1] + jnp.log(den[1]))

    return pl.pallas_call(
        body, name="attn_fwd", grid=(N_PAIRS, s // tq),
        in_specs=[pl.BlockSpec((tq, PAIR), lambda j, i: (i, j)),
                  pl.BlockSpec((s, PAIR), lambda j, i: (0, j)),
                  pl.BlockSpec((s, PAIR), lambda j, i: (0, j)),
                  pl.BlockSpec((s, 128), lambda j, i: (0, 0)),
                  pl.BlockSpec((tq, PAIR), lambda j, i: (i, j))],
        out_specs=[pl.BlockSpec((tq, PAIR), lambda j, i: (i, j)),
                   pl.BlockSpec((tq, PAIR), lambda j, i: (i, j)),
                   pl.BlockSpec((None, 8, tq), lambda j, i: (j, 0, i))],
        out_shape=[jax.ShapeDtypeStruct((s, ATTN_WIDTH), BF16), jax.ShapeDtypeStruct((s, ATTN_WIDTH), BF16),
                   jax.ShapeDtypeStruct((N_PAIRS, 8, s), F32)],
        scratch_shapes=[pltpu.VMEM((2, s, PAIR), BF16), pltpu.VMEM((2, PAIR, s), BF16),
                        pltpu.VMEM((PAIR, tq), F32), pltpu.VMEM((PAIR, tq), F32),
                        pltpu.VMEM((2, tk, tq), F32), pltpu.VMEM((2, tk, tq), F32)],
        compiler_params=_params(("arbitrary", "arbitrary")),
    )(q, k, v, c, ag)


def _outproj_fwd(x, po, ma, wo, tm=512):
    s = x.shape[0]

    def body(x_ref, po_ref, ma_ref, wo_ref, xn_ref):
        xn_ref[...] = (x_ref[...]
                       + jnp.dot(po_ref[...], wo_ref[0:POOL_WIDTH, :], preferred_element_type=F32)
                       + jnp.dot(ma_ref[...], wo_ref[POOL_WIDTH:, :], preferred_element_type=F32))

    return pl.pallas_call(
        body, name="outproj_fwd", grid=(s // tm,),
        in_specs=[_row_spec(tm, D_MODEL), _row_spec(tm, 512), _row_spec(tm, 512),
                  _full_spec((D_MODEL, D_MODEL))],
        out_specs=_row_spec(tm, D_MODEL),
        out_shape=jax.ShapeDtypeStruct((s, D_MODEL), F32),
        compiler_params=_params(("parallel",)),
    )(x, po, ma, wo)


def _loss_head(x, tgt, g, tm=512):
    s = x.shape[0]

    def body(x_ref, t_ref, g_ref, dx_ref, dg_ref, loss_ref):
        @pl.when(pl.program_id(0) == 0)
        def _():
            dg_ref[...] = jnp.zeros_like(dg_ref)
            loss_ref[...] = jnp.zeros_like(loss_ref)

        xf = x_ref[...]
        r = lax.rsqrt(jnp.mean(xf * xf, axis=-1, keepdims=True) + RMS_EPS)
        xh = xf * r
        gg = g_ref[...]
        e = xh * gg - t_ref[...]
        loss_ref[...] += 0.5 * jnp.sum(jnp.mean(e * e, axis=-1, keepdims=True))
        dy = e * (1.0 / D_MODEL)
        u = dy * gg
        dx_ref[...] = r * (u - xh * jnp.mean(xh * u, axis=-1, keepdims=True))
        dg_ref[...] += jnp.sum(dy * xh, axis=0, keepdims=True)

    return pl.pallas_call(
        body, name="loss_head", grid=(s // tm,),
        in_specs=[_row_spec(tm, D_MODEL), _row_spec(tm, D_MODEL), _full_spec((1, D_MODEL))],
        out_specs=[_row_spec(tm, D_MODEL), _full_spec((1, D_MODEL)), _full_spec((8, 128))],
        out_shape=[jax.ShapeDtypeStruct((s, D_MODEL), F32), jax.ShapeDtypeStruct((1, D_MODEL), F32),
                   jax.ShapeDtypeStruct((8, 128), F32)],
        compiler_params=_params(("arbitrary",)),
    )(x, tgt, g)


def _outproj_bwd(dx, po, ma, ag, o, wo, head_sel, tm=512):
    s = dx.shape[0]

    def body(dx_ref, po_ref, ma_ref, ag_ref, o_ref, wo_ref, sel_ref, dmp_ref, do_ref, dag_ref, dt_ref, gwo_ref):
        @pl.when(pl.program_id(0) == 0)
        def _():
            gwo_ref[...] = jnp.zeros_like(gwo_ref)

        dxb = dx_ref[...].astype(BF16)
        dm = lax.dot_general(dxb, wo_ref[...], NT, preferred_element_type=F32)
        dmp_ref[...] = dm[:, 0:POOL_WIDTH].astype(BF16)
        dma = dm[:, POOL_WIDTH:]
        gate = ag_ref[...].astype(F32)
        of = o_ref[...].astype(F32)
        sg = _sigmoid(gate)
        do = dma * (gate * sg)
        do_ref[...] = do.astype(BF16)
        dag_ref[...] = (dma * of * (sg * (1.0 + gate * (1.0 - sg)))).astype(BF16)
        prod = do * of
        hi = prod.astype(BF16)
        lo = (prod - hi.astype(F32)).astype(BF16)
        dt_ref[...] = (lax.dot_general(sel_ref[...], hi, NT, preferred_element_type=F32)
                       + lax.dot_general(sel_ref[...], lo, NT, preferred_element_type=F32))
        gwo_ref[0:POOL_WIDTH, :] += lax.dot_general(po_ref[...], dxb, TN, preferred_element_type=F32)
        gwo_ref[POOL_WIDTH:, :] += lax.dot_general(ma_ref[...], dxb, TN, preferred_element_type=F32)

    act = jax.ShapeDtypeStruct((s, 512), BF16)
    return pl.pallas_call(
        body, name="outproj_bwd", grid=(s // tm,),
        in_specs=[_row_spec(tm, D_MODEL)] + [_row_spec(tm, 512)] * 4
                 + [_full_spec((D_MODEL, D_MODEL)), _full_spec((8, ATTN_WIDTH))],
        out_specs=[_row_spec(tm, 512)] * 3 + [pl.BlockSpec((8, tm), lambda i: (0, i)),
                                              _full_spec((D_MODEL, D_MODEL))],
        out_shape=[act, act, act, jax.ShapeDtypeStruct((8, s), F32),
                   jax.ShapeDtypeStruct((D_MODEL, D_MODEL), F32)],
        compiler_params=_params(("arbitrary",)),
    )(dx, po, ma, ag, o, wo, head_sel)


def _attn_bwd(q, k, v, do, c, lse, dt, tq=256):
    s = q.shape[0]
    tk = tq
    nq = s // tq

    def body(q_ref, k_ref, v_ref, do_ref, c_ref, lse_ref, dt_ref, dq_ref, dk_ref, dv_ref, dcs_ref, drs_ref,
             qm_ref, dox_ref, dqt_ref, dkx_ref, dvx_ref, csum_ref, rsum_ref):
        j = pl.program_id(0)
        b = pl.program_id(1)

        @pl.when(b == 0)
        def _():
            qq = q_ref[...]
            dd = do_ref[...]
            lane = lax.broadcasted_iota(jnp.int32, qq.shape, 1)
            zq = jnp.zeros_like(qq)
            qm_ref[0] = jnp.where(lane < HEAD_DIM, qq, zq)
            qm_ref[1] = jnp.where(lane >= HEAD_DIM, qq, zq)
            dox_ref[0] = jnp.where(lane < HEAD_DIM, dd, zq)
            dox_ref[1] = jnp.where(lane >= HEAD_DIM, dd, zq)
            dqt_ref[...] = jnp.zeros_like(dqt_ref)
            rsum_ref[...] = jnp.zeros_like(rsum_ref)

        kblk = k_ref[...]
        vblk = v_ref[...]
        kt = jnp.transpose(kblk.astype(F32))
        sub = lax.broadcasted_iota(jnp.int32, kt.shape, 0)
        ktm = (jnp.where(sub < HEAD_DIM, kt, 0.0).astype(BF16), jnp.where(sub >= HEAD_DIM, kt, 0.0).astype(BF16))
        cblk = c_ref[...]
        lane_c = lax.broadcasted_iota(jnp.int32, cblk.shape, 1)
        ccol = [jnp.sum(jnp.where(lane_c == 2 * j + h, cblk, 0.0), axis=1, keepdims=True) for h in range(2)]
        dkx_ref[...] = jnp.zeros_like(dkx_ref)
        dvx_ref[...] = jnp.zeros_like(dvx_ref)
        csum_ref[...] = jnp.zeros_like(csum_ref)

        def step(qb, masked):
            start = pl.multiple_of(qb * tq, tq)
            rows = pl.ds(start, tq)
            for h in range(2):
                qm = qm_ref[h, rows, :]
                dox = dox_ref[h, rows, :]
                arg = (lax.dot_general(kblk, qm, NT, preferred_element_type=F32)
                       - ccol[h] - lse_ref[pl.ds(h, 1), rows])
                if masked:
                    krow = lax.broadcasted_iota(jnp.int32, (tk, tq), 0)
                    qcol = lax.broadcasted_iota(jnp.int32, (tk, tq), 1)
                    arg = jnp.where(qcol >= krow, arg, NEG)
                pt = jnp.exp(arg)
                dpt = lax.dot_general(vblk, dox, NT, preferred_element_type=F32)
                ds = pt * (dpt - dt_ref[pl.ds(h, 1), rows])
                part = ds[:, 0:128]
                for t in range(1, tq // 128):
                    part = part + ds[:, t * 128:(t + 1) * 128]
                csum_ref[h] += part
                rsum_ref[h, :, rows] += jnp.sum(ds.reshape(tk // 8, 8, tq), axis=0)
                dst = ds.astype(BF16)
                dvx_ref[...] += jnp.dot(pt.astype(BF16), dox, preferred_element_type=F32)
                dkx_ref[...] += jnp.dot(dst, qm, preferred_element_type=F32)
                dqt_ref[:, rows] += jnp.dot(ktm[h], dst, preferred_element_type=F32)

        step(b, True)

        def loop_body(qb, carry):
            step(qb, False)
            return carry

        lax.fori_loop(b + 1, nq, loop_body, 0)

        dv_ref[...] = dvx_ref[...].astype(BF16)
        dk_ref[...] = dkx_ref[...].astype(BF16)
        lane_o = lax.broadcasted_iota(jnp.int32, (tk, 128), 1)
        col = [jnp.sum(csum_ref[h], axis=1, keepdims=True) for h in range(2)]
        dcs_ref[...] = jnp.where(lane_o == 0, col[0], jnp.where(lane_o == 1, col[1], 0.0))

        @pl.when(b == nq - 1)
        def _():
            sub_o = lax.broadcasted_iota(jnp.int32, (8, s), 0)
            row = [jnp.sum(rsum_ref[h], axis=0, keepdims=True) for h in range(2)]
            drs_ref[...] = jnp.where(sub_o == 0, row[0], jnp.where(sub_o == 1, row[1], 0.0))
            for ch in range(s // 512):
                rows = pl.ds(ch * 512, 512)
                dq_ref[rows, :] = (jnp.transpose(dqt_ref[:, rows]) * Q_SCALE).astype(BF16)

    act = jax.ShapeDtypeStruct((s, ATTN_WIDTH), BF16)
    pair_rows = pl.BlockSpec((s, PAIR), lambda j, b: (0, j))
    pair_blk = pl.BlockSpec((tk, PAIR), lambda j, b: (b, j))
    stat = pl.BlockSpec((None, 2, s), lambda j, b: (j, 0, 0))
    return pl.pallas_call(
        body, name="attn_bwd", grid=(N_PAIRS, s // tk),
        in_specs=[pair_rows, pair_blk, pair_blk, pair_rows,
                  pl.BlockSpec((tk, 128), lambda j, b: (b, 0)),
                  pl.BlockSpec((None, 8, s), lambda j, b: (j, 0, 0)), stat],
        out_specs=[pair_rows, pair_blk, pair_blk, pair_blk, pl.BlockSpec((None, 8, s), lambda j, b: (j, 0, 0))],
        out_shape=[act, act, act, jax.ShapeDtypeStruct((s, N_PAIRS * 128), F32),
                   jax.ShapeDtypeStruct((N_PAIRS, 8, s), F32)],
        scratch_shapes=[pltpu.VMEM((2, s, PAIR), BF16), pltpu.VMEM((2, s, PAIR), BF16),
                        pltpu.VMEM((PAIR, s), F32), pltpu.VMEM((tk, PAIR), F32),
                        pltpu.VMEM((tk, PAIR), F32), pltpu.VMEM((2, tk, 128), F32),
                        pltpu.VMEM((2, 8, s), F32)],
        compiler_params=_params(("arbitrary", "arbitrary")),
    )(q, k, v, do, c, lse, dt)


def _forget_bwd(dcs, drs, lf, tm=256):
    s = lf.shape[0]
    n = s // tm

    def body(dcs_ref, drs_ref, lf_ref, dpf_ref, dfb_ref, carry_ref):
        @pl.when(pl.program_id(0) == 0)
        def _():
            carry_ref[...] = jnp.zeros_like(carry_ref)
            dfb_ref[...] = jnp.zeros_like(dfb_ref)

        lane = lax.broadcasted_iota(jnp.int32, (tm, 128), 1)
        dc = drs_ref[...]
        for pj in range(N_PAIRS):
            blk = dcs_ref[:, pj * 128:(pj + 1) * 128]
            if pj:
                blk = pltpu.roll(blk, 2 * pj, 1)
            dc = dc - jnp.where((lane == 2 * pj) | (lane == 2 * pj + 1), blk, 0.0)
        dlf = _scan_rows(dc, reverse=True) + carry_ref[0:1, :]
        carry_ref[...] = dlf[0:8, :]
        dz = jnp.where(lane < 8, dlf * (1.0 - jnp.exp(lf_ref[...])), 0.0)
        dpf_ref[...] = dz.astype(BF16)
        dfb_ref[...] += jnp.sum(dz, axis=0, keepdims=True)

    return pl.pallas_call(
        body, name="forget_bwd", grid=(n,),
        in_specs=[pl.BlockSpec((tm, N_PAIRS * 128), lambda i: (n - 1 - i, 0)),
                  pl.BlockSpec((tm, 128), lambda i: (n - 1 - i, 0)),
                  pl.BlockSpec((tm, 128), lambda i: (n - 1 - i, 0))],
        out_specs=[pl.BlockSpec((tm, 128), lambda i: (n - 1 - i, 0)), _full_spec((1, 128))],
        out_shape=[jax.ShapeDtypeStruct((s, 128), BF16), jax.ShapeDtypeStruct((1, 128), F32)],
        scratch_shapes=[pltpu.VMEM((8, 128), F32)],
        compiler_params=_params(("arbitrary",)),
    )(dcs, drs, lf)


def _pool_bwd(pu, pg, dmp, pw, ps, tm=512):
    s = pu.shape[0]
    hb = tm // HALO
    n = s // tm
    last_halo = s // HALO - 1

    def body(pu_ref, halo_ref, pg_ref, dmp_ref, pgn_ref, dmpn_ref, pw_ref, ps_ref,
             dpu_ref, dpg_ref, gpw_ref, gps_ref):
        i = pl.program_id(0)

        @pl.when(i == 0)
        def _():
            gpw_ref[...] = jnp.zeros_like(gpw_ref)
            gps_ref[...] = jnp.zeros_like(gps_ref)

        u = pu_ref[...].astype(F32)
        halo = jnp.where(i == 0, 0.0, halo_ref[...].astype(F32))
        d = _pool_delta(u, halo, i * tm)
        cnt = _window_counts(i * tm, tm)
        cnt_next = _window_counts((i + 1) * tm, HALO)
        e_parts, dd_parts = [], []
        for gi in range(len(POOL_WINDOWS)):
            cols = slice(gi * GROUP_DIM, (gi + 1) * GROUP_DIM)
            wg = pw_ref[gi]
            scale = ps_ref[:, cols]
            db = d[gi].astype(BF16)
            z = jnp.dot(db, wg, preferred_element_type=F32)
            gate = pg_ref[:, cols].astype(F32)
            sg = _sigmoid(gate)
            dm = dmp_ref[:, cols].astype(F32)
            dy = dm * (gate * sg)
            dpg_ref[:, cols] = (dm * (z * scale) * (sg * (1.0 + gate * (1.0 - sg)))).astype(BF16)
            gps_ref[:, cols] += jnp.sum(dy * z, axis=0, keepdims=True)
            dz = (dy * scale).astype(BF16)
            gpw_ref[gi] += lax.dot_general(db, dz, TN, preferred_element_type=F32)
            dd = lax.dot_general(dz, wg, NT, preferred_element_type=F32)
            gate_n = pgn_ref[:, cols].astype(F32)
            dz_n = (dmpn_ref[:, cols].astype(F32) * (gate_n * _sigmoid(gate_n)) * scale).astype(BF16)
            dd_n = lax.dot_general(dz_n, wg, NT, preferred_element_type=F32)
            dd_n = jnp.where(i == n - 1, 0.0, dd_n)
            dd_parts.append(dd)
            e_parts.append(jnp.concatenate([dd / cnt[gi], dd_n / cnt_next[gi]], axis=0))
        lead = _window_sums(jnp.concatenate(e_parts, axis=1), forward=True)
        for gi in range(len(POOL_WINDOWS)):
            cols = slice(gi * GROUP_DIM, (gi + 1) * GROUP_DIM)
            dpu_ref[:, cols] = (lead[gi][0:tm, :] - dd_parts[gi]).astype(BF16)

    act = jax.ShapeDtypeStruct((s, 512), BF16)
    prev_halo = pl.BlockSpec((HALO, 512), lambda i: (jnp.maximum(i * hb - 1, 0), 0))
    next_halo = pl.BlockSpec((HALO, 512), lambda i: (jnp.minimum((i + 1) * hb, last_halo), 0))
    return pl.pallas_call(
        body, name="pool_bwd", grid=(n,),
        in_specs=[_row_spec(tm, 512), prev_halo, _row_spec(tm, 512), _row_spec(tm, 512), next_halo, next_halo,
                  _full_spec((4, GROUP_DIM, GROUP_DIM)), _full_spec((1, 512))],
        out_specs=[_row_spec(tm, 512), _row_spec(tm, 512), _full_spec((4, GROUP_DIM, GROUP_DIM)),
                   _full_spec((1, 512))],
        out_shape=[act, act, jax.ShapeDtypeStruct((4, GROUP_DIM, GROUP_DIM), F32),
                   jax.ShapeDtypeStruct((1, 512), F32)],
        compiler_params=_params(("arbitrary",)),
    )(pu, pu, pg, dmp, pg, dmp, pw, ps)


def _inproj_bwd_x(dsec, w, x, g, dxo, tm=512):
    s = x.shape[0]

    def body(*refs):
        d_refs = refs[0:7]
        w_ref, x_ref, g_ref, dxo_ref, dxi_ref, h_ref, dg_ref = refs[7:]

        @pl.when(pl.program_id(0) == 0)
        def _():
            dg_ref[...] = jnp.zeros_like(dg_ref)

        dh = None
        for d_ref, (off, ncol) in zip(d_refs, SECTIONS):
            t = lax.dot_general(d_ref[...], w_ref[:, off:off + ncol], NT, preferred_element_type=F32)
            dh = t if dh is None else dh + t
        xf = x_ref[...]
        r = lax.rsqrt(jnp.mean(xf * xf, axis=-1, keepdims=True) + RMS_EPS)
        xh = xf * r
        gg = g_ref[...]
        h_ref[...] = (xh * gg).astype(BF16)
        u = dh * gg
        dxi_ref[...] = dxo_ref[...] + r * (u - xh * jnp.mean(xh * u, axis=-1, keepdims=True))
        dg_ref[...] += jnp.sum(dh * xh, axis=0, keepdims=True)

    return pl.pallas_call(
        body, name="inproj_bwd_x", grid=(s // tm,),
        in_specs=[_row_spec(tm, ncol) for _, ncol in SECTIONS]
                 + [_full_spec((D_MODEL, IN_COLS_PAD)), _row_spec(tm, D_MODEL), _full_spec((1, D_MODEL)),
                    _row_spec(tm, D_MODEL)],
        out_specs=[_row_spec(tm, D_MODEL), _row_spec(tm, D_MODEL), _full_spec((1, D_MODEL))],
        out_shape=[jax.ShapeDtypeStruct((s, D_MODEL), F32), jax.ShapeDtypeStruct((s, D_MODEL), BF16),
                   jax.ShapeDtypeStruct((1, D_MODEL), F32)],
        compiler_params=_params(("arbitrary",)),
    )(*dsec, w, x, g, dxo)


def _inproj_bwd_w(h, dsec, tm=512):
    s = h.shape[0]

    def body(*refs):
        h_ref = refs[0]
        d_refs = refs[1:8]
        gw_ref = refs[8]

        @pl.when(pl.program_id(0) == 0)
        def _():
            gw_ref[...] = jnp.zeros_like(gw_ref)

        hh = h_ref[...]
        for d_ref, (off, ncol) in zip(d_refs, SECTIONS):
            gw_ref[:, off:off + ncol] += lax.dot_general(hh, d_ref[...], TN, preferred_element_type=F32)

    return pl.pallas_call(
        body, name="inproj_bwd_w", grid=(s // tm,),
        in_specs=[_row_spec(tm, D_MODEL)] + [_row_spec(tm, ncol) for _, ncol in SECTIONS],
        out_specs=_full_spec((D_MODEL, IN_COLS_PAD)),
        out_shape=jax.ShapeDtypeStruct((D_MODEL, IN_COLS_PAD), F32),
        compiler_params=_params(("arbitrary",), vmem_mb=56),
    )(h, *dsec)


def _elementwise(fn, name, n_out, arrays, tm):
    rows, cols = arrays[0].shape
    n_in = len(arrays)

    def body(*refs):
        outs = fn(*[r[...] for r in refs[:n_in]])
        for r, val in zip(refs[n_in:], outs):
            r[...] = val

    spec = _row_spec(tm, cols)
    shape = jax.ShapeDtypeStruct((rows, cols), F32)
    return pl.pallas_call(
        body, name=name, grid=(rows // tm,),
        in_specs=[spec] * n_in, out_specs=[spec] * n_out, out_shape=[shape] * n_out,
        compiler_params=_params(("parallel",)),
    )(*arrays)


def _add2(a, b, name, tm):
    return _elementwise(lambda p, q: (p + q,), name, 1, [a, b], tm)[0]


def _sum4(parts, name, tm):
    _, rows, cols = parts.shape

    def body(p_ref, o_ref):
        o_ref[...] = (p_ref[0] + p_ref[1]) + (p_ref[2] + p_ref[3])

    return pl.pallas_call(
        body, name=name, grid=(rows // tm,),
        in_specs=[pl.BlockSpec((4, tm, cols), lambda i: (0, i, 0))],
        out_specs=_row_spec(tm, cols), out_shape=jax.ShapeDtypeStruct((rows, cols), F32),
        compiler_params=_params(("parallel",)),
    )(parts)


def _adamw(g, w, m, v, name, tm):
    def fn(g, w, m, v):
        m = ADAM_B1 * m + (1.0 - ADAM_B1) * g
        v = ADAM_B2 * v + (1.0 - ADAM_B2) * (g * g)
        m_hat = m / (1.0 - ADAM_B1 ** ADAM_STEP)
        v_hat = v / (1.0 - ADAM_B2 ** ADAM_STEP)
        delta = -ADAM_LR * (m_hat / (jnp.sqrt(v_hat) + ADAM_EPS) + ADAM_WD * w)
        return g, delta, m, v

    return _elementwise(fn, name, 4, [g, w, m, v], tm)


def _position():
    x, y, c = lax.axis_index("x"), lax.axis_index("y"), lax.axis_index("c")
    other_chips = [(1 - x, y), (x, 1 - y), (1 - x, 1 - y)]
    return x, y, c, other_chips


def _remote(src, dst, sems, k, to):
    send_sems, recv_sems = sems
    return pltpu.make_async_remote_copy(src_ref=src, dst_ref=dst, send_sem=send_sems.at[k],
                                        recv_sem=recv_sems.at[k], device_id=to, device_id_type=MESH)


def _comm_call(body, name, n_in, out_shape, n_remote, n_local):
    return pl.pallas_call(
        body, name=name, in_specs=[ANY] * n_in, out_specs=[ANY] * len(out_shape), out_shape=out_shape,
        scratch_shapes=[pltpu.SemaphoreType.DMA((n_remote,)), pltpu.SemaphoreType.DMA((n_remote,)),
                        pltpu.SemaphoreType.DMA((n_local,))],
    )


def _gather_weights(wi, wo):
    def body(wi_ref, wo_ref, gi_ref, go_ref, send_sems, recv_sems, local_sems):
        x, y, c, chips = _position()
        sems = (send_sems, recv_sems)
        me = 2 * x + y
        mine, other = pl.ds(2 * c, 2), pl.ds(2 * (1 - c), 2)
        own = [pltpu.make_async_copy(wi_ref, gi_ref.at[me], local_sems.at[0]),
               pltpu.make_async_copy(wo_ref, go_ref.at[:, me], local_sems.at[1])]
        for cp in own:
            cp.start()
        sent = []
        for k, (cx, cy) in enumerate(chips):
            sent.append(_remote(wi_ref.at[mine], gi_ref.at[me, mine], sems, 2 * k, (cx, cy, c)))
            sent.append(_remote(wo_ref.at[mine], go_ref.at[mine, me], sems, 2 * k + 1, (cx, cy, c)))
        for cp in sent:
            cp.start()
        for k, (cx, cy) in enumerate(chips):
            ck = 2 * cx + cy
            for a, blk in ((0, gi_ref.at[ck, mine]), (1, go_ref.at[mine, ck])):
                _remote(blk, blk, sems, 2 * k + a, (x, y, c)).wait_recv()
                fwd = _remote(blk, blk, sems, 6 + 2 * k + a, (x, y, 1 - c))
                fwd.start()
                sent.append(fwd)
        for k, (cx, cy) in enumerate(chips):
            ck = 2 * cx + cy
            for a, blk in ((0, gi_ref.at[ck, other]), (1, go_ref.at[other, ck])):
                _remote(blk, blk, sems, 6 + 2 * k + a, (x, y, c)).wait_recv()
        for cp in sent:
            cp.wait_send()
        for cp in own:
            cp.wait()

    out_shape = [jax.ShapeDtypeStruct((N_CHIPS,) + wi.shape, wi.dtype),
                 jax.ShapeDtypeStruct((DEPTH, N_CHIPS) + wo.shape[1:], wo.dtype)]
    return _comm_call(body, "gather_weights", 2, out_shape, 12, 2)(wi, wo)


def _exchange_sibling(gi, go, sm):
    def body(gi_ref, go_ref, sm_ref, ri_ref, ro_ref, rs_ref, send_sems, recv_sems, local_sems):
        x, y, c, _ = _position()
        sems = (send_sems, recv_sems)
        other = pl.ds(2 * (1 - c), 2)
        cps = [_remote(gi_ref.at[:, other], ri_ref, sems, 0, (x, y, 1 - c)),
               _remote(go_ref.at[other], ro_ref, sems, 1, (x, y, 1 - c)),
               _remote(sm_ref, rs_ref, sems, 2, (x, y, 1 - c))]
        for cp in cps:
            cp.start()
        for cp in cps:
            cp.wait()

    out_shape = [jax.ShapeDtypeStruct((N_CHIPS, 2) + gi.shape[2:], F32),
                 jax.ShapeDtypeStruct((2,) + go.shape[1:], F32),
                 jax.ShapeDtypeStruct(sm.shape, F32)]
    return _comm_call(body, "exchange_sibling", 3, out_shape, 3, 1)(gi, go, sm)


def _scatter_chips(pi, po, ps):
    def body(pi_ref, po_ref, ps_ref, ri_ref, ro_ref, rs_ref, send_sems, recv_sems, local_sems):
        x, y, c, chips = _position()
        sems = (send_sems, recv_sems)
        me = 2 * x + y
        half = pl.ds(SMALL_HALF * c, SMALL_HALF)
        own = [pltpu.make_async_copy(pi_ref.at[me], ri_ref.at[me], local_sems.at[0]),
               pltpu.make_async_copy(po_ref.at[:, me], ro_ref.at[me], local_sems.at[1]),
               pltpu.make_async_copy(ps_ref.at[half], rs_ref.at[me], local_sems.at[2])]
        for cp in own:
            cp.start()
        sent = []
        for k, (cx, cy) in enumerate(chips):
            ck = 2 * cx + cy
            sent.append(_remote(pi_ref.at[ck], ri_ref.at[me], sems, 3 * k, (cx, cy, c)))
            sent.append(_remote(po_ref.at[:, ck], ro_ref.at[me], sems, 3 * k + 1, (cx, cy, c)))
            sent.append(_remote(ps_ref.at[half], rs_ref.at[me], sems, 3 * k + 2, (cx, cy, c)))
        for cp in sent:
            cp.start()
        for k, (cx, cy) in enumerate(chips):
            ck = 2 * cx + cy
            for a, blk in ((0, ri_ref.at[ck]), (1, ro_ref.at[ck]), (2, rs_ref.at[ck])):
                _remote(blk, blk, sems, 3 * k + a, (x, y, c)).wait_recv()
        for cp in sent:
            cp.wait_send()
        for cp in own:
            cp.wait()

    out_shape = [jax.ShapeDtypeStruct(pi.shape, F32),
                 jax.ShapeDtypeStruct((N_CHIPS, 2) + po.shape[2:], F32),
                 jax.ShapeDtypeStruct((N_CHIPS, SMALL_HALF, 128), F32)]
    return _comm_call(body, "scatter_chips", 3, out_shape, 9, 3)(pi, po, ps)


def _join_sibling(hi, ho, hs):
    def body(hi_ref, ho_ref, hs_ref, fi_ref, fo_ref, fs_ref, send_sems, recv_sems, local_sems):
        x, y, c, _ = _position()
        sems = (send_sems, recv_sems)
        mine, other = pl.ds(2 * c, 2), pl.ds(2 * (1 - c), 2)
        half, other_half = pl.ds(SMALL_HALF * c, SMALL_HALF), pl.ds(SMALL_HALF * (1 - c), SMALL_HALF)
        own = [pltpu.make_async_copy(hi_ref, fi_ref.at[mine], local_sems.at[0]),
               pltpu.make_async_copy(ho_ref, fo_ref.at[mine], local_sems.at[1]),
               pltpu.make_async_copy(hs_ref, fs_ref.at[half], local_sems.at[2])]
        for cp in own:
            cp.start()
        sent = [_remote(hi_ref, fi_ref.at[mine], sems, 0, (x, y, 1 - c)),
                _remote(ho_ref, fo_ref.at[mine], sems, 1, (x, y, 1 - c)),
                _remote(hs_ref, fs_ref.at[half], sems, 2, (x, y, 1 - c))]
        for cp in sent:
            cp.start()
        for a, blk in ((0, fi_ref.at[other]), (1, fo_ref.at[other]), (2, fs_ref.at[other_half])):
            _remote(blk, blk, sems, a, (x, y, c)).wait_recv()
        for cp in sent:
            cp.wait_send()
        for cp in own:
            cp.wait()

    out_shape = [jax.ShapeDtypeStruct((DEPTH,) + hi.shape[1:], F32),
                 jax.ShapeDtypeStruct((DEPTH,) + ho.shape[1:], F32),
                 jax.ShapeDtypeStruct((SMALL_ROWS, 128), F32)]
    return _comm_call(body, "join_sibling", 3, out_shape, 3, 3)(hi, ho, hs)


def _pack_small(norm_g, final_g, pool_scale, forget_bias, pool_w):
    fb = jnp.pad(forget_bias.reshape(-1), (0, 128 - forget_bias.size))
    flat = jnp.concatenate([norm_g.reshape(-1), final_g.reshape(-1), pool_scale.reshape(-1), fb,
                            pool_w.reshape(-1)])
    return jnp.pad(flat, (0, SMALL_ROWS * 128 - flat.size)).reshape(SMALL_ROWS, 128)


def _unpack_small(packed):
    flat = packed.reshape(-1)
    sizes = (DEPTH * D_MODEL, D_MODEL, DEPTH * POOL_WIDTH, 128, DEPTH * 4 * GROUP_DIM * GROUP_DIM)
    offs = [0]
    for n in sizes:
        offs.append(offs[-1] + n)
    norm_g, final_g, pool_scale, fb, pool_w = [flat[offs[i]:offs[i + 1]] for i in range(5)]
    return (norm_g.reshape(DEPTH, D_MODEL), final_g, pool_scale.reshape(DEPTH, POOL_WIDTH),
            fb[:DEPTH * 8].reshape(DEPTH, 8), pool_w.reshape(DEPTH, 4, GROUP_DIM, GROUP_DIM))


def kernel(x, norm_g, w_in, forget_bias, pool_w, pool_scale, w_out, final_g, loss_target, m_norm_g, m_w_in, m_forget_bias, m_pool_w, m_pool_scale, m_w_out, m_final_g, v_norm_g, v_w_in, v_forget_bias, v_pool_w, v_pool_scale, v_w_out, v_final_g):
    x0 = x[0]
    tgt = loss_target[0]
    s = x0.shape[0]

    gi, go = _gather_weights(w_in.astype(BF16), w_out.astype(BF16))
    w_in_full = jnp.concatenate([gi[k] for k in range(N_CHIPS)], axis=-1)
    w_in_full = jnp.pad(w_in_full, ((0, 0), (0, 0), (0, IN_COLS_PAD - IN_COLS)))
    w_out_full = go.reshape(DEPTH, D_MODEL, D_MODEL)
    pool_w_b = pool_w.astype(BF16)
    fb_pad = jnp.pad(forget_bias, ((0, 0), (0, 128 - forget_bias.shape[1])))
    head_sel = (jnp.arange(ATTN_WIDTH)[None, :] // HEAD_DIM == jnp.arange(8)[:, None]).astype(BF16)

    saved = []
    xl = x0
    for l in range(DEPTH):
        pu, pg, q, k, v, ag, lf, c = _inproj_fwd(xl, norm_g[l][None], w_in_full[l], fb_pad[l][None])
        po = _pool_fwd(pu, pg, pool_w_b[l], pool_scale[l][None])
        ma, o, lse = _attn_fwd(q, k, v, c, ag)
        saved.append((xl, pu, pg, q, k, v, ag, lf, c, po, ma, o, lse))
        xl = _outproj_fwd(xl, po, ma, w_out_full[l])

    dx, g_final, loss_part = _loss_head(xl, tgt, final_g[None])
    loss = lax.psum(loss_part[0, 0], ("x", "y", "c"))

    g_norm, g_win, g_fb, g_pw, g_ps, g_wo = [], [], [], [], [], []
    for l in reversed(range(DEPTH)):
        xl, pu, pg, q, k, v, ag, lf, c, po, ma, o, lse = saved[l]
        dmp, do, dag, dt, gwo = _outproj_bwd(dx, po, ma, ag, o, w_out_full[l], head_sel)
        dq, dk, dv, dcs, drs = _attn_bwd(q, k, v, do, c, lse, dt.reshape(N_PAIRS, 2, s))
        drs = jnp.pad(drs[:, 0:2, :].reshape(8, s).T, ((0, 0), (0, 120)))
        dpf, dfb = _forget_bwd(dcs, drs, lf)
        dpu, dpg, gpw, gps = _pool_bwd(pu, pg, dmp, pool_w_b[l], pool_scale[l][None])
        dsec = (dpu, dpg, dq, dk, dv, dag, dpf)
        dx, h, gn = _inproj_bwd_x(dsec, w_in_full[l], xl, norm_g[l][None], dx)
        g_win.append(_inproj_bwd_w(h, dsec))
        g_norm.append(gn[0])
        g_fb.append(dfb[0, :8])
        g_pw.append(gpw)
        g_ps.append(gps[0])
        g_wo.append(gwo)
    g_norm, g_win, g_fb, g_pw, g_ps, g_wo = [t[::-1] for t in (g_norm, g_win, g_fb, g_pw, g_ps, g_wo)]

    c_idx = lax.axis_index("c")
    gwin = jnp.stack(g_win)
    send_i = jnp.stack([gwin[:, :, k * SHARD_COLS:(k + 1) * SHARD_COLS] for k in range(N_CHIPS)])
    send_o = jnp.stack(g_wo).reshape(DEPTH, N_CHIPS, SHARD_ROWS, D_MODEL)
    send_s = _pack_small(jnp.stack(g_norm), g_final[0], jnp.stack(g_ps), jnp.stack(g_fb), jnp.stack(g_pw))
    ri, ro, rs = _exchange_sibling(send_i, send_o, send_s)
    keep_i = lax.dynamic_slice_in_dim(send_i, 2 * c_idx, 2, axis=1)
    keep_o = lax.dynamic_slice_in_dim(send_o, 2 * c_idx, 2, axis=0)
    pi = _add2(keep_i.reshape(-1, SHARD_COLS), ri.reshape(-1, SHARD_COLS), "chip_sum_w_in", 256).reshape(ri.shape)
    po_ = _add2(keep_o.reshape(-1, D_MODEL), ro.reshape(-1, D_MODEL), "chip_sum_w_out", 256).reshape(ro.shape)
    ps_ = _add2(send_s, rs, "chip_sum_small", SMALL_ROWS // 4)
    qi, qo, qs = _scatter_chips(pi, po_, ps_)
    hi = _sum4(qi.reshape(N_CHIPS, -1, SHARD_COLS), "mesh_sum_w_in", 256).reshape(2, D_MODEL, SHARD_COLS)
    ho = _sum4(qo.reshape(N_CHIPS, -1, D_MODEL), "mesh_sum_w_out", 256).reshape(2, SHARD_ROWS, D_MODEL)
    hs = _sum4(qs, "mesh_sum_small", SMALL_HALF // 4)
    fi, fo, fs = _join_sibling(hi, ho, hs)

    flat_i = lambda a: a.reshape(-1, SHARD_COLS)
    flat_o = lambda a: a.reshape(-1, D_MODEL)
    out_i = _adamw(flat_i(fi), flat_i(w_in), flat_i(m_w_in), flat_i(v_w_in), "adamw_w_in", 256)
    out_o = _adamw(flat_o(fo), flat_o(w_out), flat_o(m_w_out), flat_o(v_w_out), "adamw_w_out", 256)
    out_s = _adamw(fs, _pack_small(norm_g, final_g, pool_scale, forget_bias, pool_w),
                   _pack_small(m_norm_g, m_final_g, m_pool_scale, m_forget_bias, m_pool_w),
                   _pack_small(v_norm_g, v_final_g, v_pool_scale, v_forget_bias, v_pool_w),
                   "adamw_small", SMALL_ROWS // 4)

    groups = []
    for oi, oo, os_ in zip(out_i, out_o, out_s):
        sn, sf, sp, sb, sw = _unpack_small(os_)
        groups.append((sn, oi.reshape(w_in.shape), sb, sw, sp, oo.reshape(w_out.shape), sf))
    return (loss, dx[None]) + tuple(t for grp in groups for t in grp)
```

```python
import jax
import jax.numpy as jnp
from jax import lax
from jax.experimental import pallas as pl
from jax.experimental.pallas import tpu as pltpu

F32 = jnp.float32
BF16 = jnp.bfloat16

D_MODEL = 1024
DEPTH = 4
POOL_WIDTH = 512
ATTN_WIDTH = 512
HEAD_DIM = 64
PAIR = 2 * HEAD_DIM
N_PAIRS = ATTN_WIDTH // PAIR
POOL_WINDOWS = (2, 4, 8, 16)
GROUP_DIM = 128
HALO = 16
IN_COLS = 3080
OFF_F = 3072
IN_COLS_PAD = 3200
SECTIONS = ((0, 512), (512, 512), (1024, 512), (1536, 512), (2048, 512), (2560, 512), (OFF_F, 128))
N_CHIPS = 4
SHARD_COLS = IN_COLS // N_CHIPS
SHARD_ROWS = D_MODEL // N_CHIPS
RMS_EPS = 1e-6
NEG = -1e30
Q_SCALE = 0.125

ADAM_LR = 0.001
ADAM_B1 = 0.9
ADAM_B2 = 0.999
ADAM_EPS = 1e-08
ADAM_WD = 0.01
ADAM_STEP = 10

SMALL_ROWS = 2112
SMALL_HALF = SMALL_ROWS // 2

NT = (((1,), (1,)), ((), ()))
TN = (((0,), (0,)), ((), ()))
MESH = pl.DeviceIdType.MESH
ANY = pl.BlockSpec(memory_space=pl.ANY)


def _params(semantics, vmem_mb=48):
    return pltpu.CompilerParams(dimension_semantics=semantics, vmem_limit_bytes=vmem_mb << 20)


def _row_spec(tm, cols):
    return pl.BlockSpec((tm, cols), lambda i: (i, 0))


def _full_spec(shape):
    return pl.BlockSpec(shape, lambda *_: (0,) * len(shape))


def _sigmoid(x):
    return jax.nn.sigmoid(x)


def _scan_rows(a, reverse=False):
    n = a.shape[0]
    row = lax.broadcasted_iota(jnp.int32, a.shape, 0)
    k = 1
    while k < n:
        if reverse:
            a = a + jnp.where(row < n - k, pltpu.roll(a, n - k, 0), 0.0)
        else:
            a = a + jnp.where(row >= k, pltpu.roll(a, k, 0), 0.0)
        k *= 2
    return a


def _inproj_fwd(x, g, w, fb, tm=256):
    s = x.shape[0]

    def body(x_ref, g_ref, w_ref, fb_ref, pu_ref, pg_ref, q_ref, k_ref, v_ref, ag_ref,
             lf_ref, c_ref, carry_ref):
        @pl.when(pl.program_id(0) == 0)
        def _():
            carry_ref[...] = jnp.zeros_like(carry_ref)

        xf = x_ref[...]
        r = lax.rsqrt(jnp.mean(xf * xf, axis=-1, keepdims=True) + RMS_EPS)
        h = (xf * r * g_ref[...]).astype(BF16)

        def proj(sec):
            off, n = SECTIONS[sec]
            return jnp.dot(h, w_ref[:, off:off + n], preferred_element_type=F32)

        pu_ref[...] = proj(0).astype(BF16)
        pg_ref[...] = proj(1).astype(BF16)
        q_ref[...] = (proj(2) * Q_SCALE).astype(BF16)
        k_ref[...] = proj(3).astype(BF16)
        v_ref[...] = proj(4).astype(BF16)
        ag_ref[...] = proj(5).astype(BF16)
        z = proj(6) + fb_ref[...]
        lf = jnp.minimum(z, 0.0) - jnp.log(1.0 + jnp.exp(-jnp.abs(z)))
        lf_ref[...] = lf
        c_ref[...] = _scan_rows(lf) + carry_ref[0:1, :]
        carry_ref[0:1, :] = c_ref[tm - 1:tm, :]

    act = jax.ShapeDtypeStruct((s, 512), BF16)
    return pl.pallas_call(
        body, name="inproj_fwd", grid=(s // tm,),
        in_specs=[_row_spec(tm, D_MODEL), _full_spec((1, D_MODEL)), _full_spec((D_MODEL, IN_COLS_PAD)),
                  _full_spec((1, 128))],
        out_specs=[_row_spec(tm, 512)] * 6 + [_row_spec(tm, 128), _row_spec(tm, 128)],
        out_shape=[act] * 6 + [jax.ShapeDtypeStruct((s, 128), F32), jax.ShapeDtypeStruct((s, 128), F32)],
        scratch_shapes=[pltpu.VMEM((8, 128), F32)],
        compiler_params=_params(("arbitrary",)),
    )(x, g, w, fb)


def _window_sums(ext, forward):
    n = ext.shape[0]
    outs = []
    for gi, w in enumerate(POOL_WINDOWS):
        a = ext[:, gi * GROUP_DIM:(gi + 1) * GROUP_DIM]
        k = 1
        while k < w:
            a = a + pltpu.roll(a, (n - k) if forward else k, 0)
            k *= 2
        outs.append(a)
    return outs


def _window_counts(row0, tm):
    t = row0 + lax.broadcasted_iota(jnp.int32, (tm, GROUP_DIM), 0)
    return [jnp.minimum(t + 1, w).astype(F32) for w in POOL_WINDOWS]


def _pool_delta(u, halo, row0):
    tm = u.shape[0]
    sums = _window_sums(jnp.concatenate([halo, u], axis=0), forward=False)
    cnt = _window_counts(row0, tm)
    return [sums[gi][HALO:, :] / cnt[gi] - u[:, gi * GROUP_DIM:(gi + 1) * GROUP_DIM]
            for gi in range(len(POOL_WINDOWS))]


def _pool_fwd(pu, pg, pw, ps, tm=512):
    s = pu.shape[0]
    hb = tm // HALO

    def body(pu_ref, halo_ref, pg_ref, pw_ref, ps_ref, po_ref):
        i = pl.program_id(0)
        u = pu_ref[...].astype(F32)
        halo = jnp.where(i == 0, 0.0, halo_ref[...].astype(F32))
        d = _pool_delta(u, halo, i * tm)
        for gi in range(len(POOL_WINDOWS)):
            cols = slice(gi * GROUP_DIM, (gi + 1) * GROUP_DIM)
            z = jnp.dot(d[gi].astype(BF16), pw_ref[gi], preferred_element_type=F32)
            gate = pg_ref[:, cols].astype(F32)
            po_ref[:, cols] = (z * ps_ref[:, cols] * (gate * _sigmoid(gate))).astype(BF16)

    return pl.pallas_call(
        body, name="pool_fwd", grid=(s // tm,),
        in_specs=[_row_spec(tm, 512),
                  pl.BlockSpec((HALO, 512), lambda i: (jnp.maximum(i * hb - 1, 0), 0)),
                  _row_spec(tm, 512), _full_spec((4, GROUP_DIM, GROUP_DIM)), _full_spec((1, 512))],
        out_specs=_row_spec(tm, 512),
        out_shape=jax.ShapeDtypeStruct((s, 512), BF16),
        compiler_params=_params(("parallel",)),
    )(pu, pu, pg, pw, ps)


def _split3(x):
    hi = x.astype(BF16).astype(F32)
    r1 = x - hi
    lo = r1.astype(BF16).astype(F32)
    return hi, lo, (r1 - lo).astype(BF16).astype(F32)


def _own(idx, h):
    return (idx < HEAD_DIM) if h == 0 else (idx >= HEAD_DIM)


def _spare(h):
    return HEAD_DIM if h == 0 else 0


def _attn_fwd(q, k, v, c, ag, tq=256, tk=512, chunk=512):
    s = q.shape[0]

    def body(q_ref, k_ref, v_ref, c_ref, ag_ref, ma_ref, o_ref, lse_ref, kx_ref, vt_ref, acc0_ref, acc1_ref,
             sta_ref, stb_ref):
        j = pl.program_id(0)
        i = pl.program_id(1)

        @pl.when(i == 0)
        def _():
            def prep(ch, carry):
                rows = pl.ds(pl.multiple_of(ch * chunk, chunk), chunk)
                kk = k_ref[rows, :].astype(F32)
                cc = c_ref[rows, :]
                vt = jnp.transpose(v_ref[rows, :].astype(F32))
                lane = lax.broadcasted_iota(jnp.int32, (chunk, PAIR), 1)
                sub = lax.broadcasted_iota(jnp.int32, (PAIR, chunk), 0)
                for h in range(2):
                    ccol = jnp.sum(jnp.where(lane == 2 * j + h, cc, 0.0), axis=1, keepdims=True)
                    kx = jnp.where(_own(lane, h), kk, 0.0)
                    for t, part in enumerate(_split3(-ccol)):
                        kx = jnp.where(lane == _spare(h) + t, part, kx)
                    kx_ref[h, rows, :] = kx.astype(BF16)
                    vt_ref[h, :, rows] = jnp.where(_own(sub, h), vt,
                                                   jnp.where(sub == _spare(h), 1.0, 0.0)).astype(BF16)
                return carry

            lax.fori_loop(0, s // chunk, prep, 0)

        qq = q_ref[...].astype(F32)
        lane_q = lax.broadcasted_iota(jnp.int32, (tq, PAIR), 1)
        qx = []
        for h in range(2):
            ones = (lane_q >= _spare(h)) & (lane_q < _spare(h) + 3)
            qx.append(jnp.transpose(jnp.where(_own(lane_q, h), qq, jnp.where(ones, 1.0, 0.0))).astype(BF16))
        accs = (acc0_ref, acc1_ref)
        for acc in accs:
            acc[...] = jnp.zeros_like(acc)

        def rows_of(kb):
            return pl.ds(pl.multiple_of(kb * tk, tk), tk)

        def scores(kb, dst):
            for h in range(2):
                dst[h] = jnp.dot(kx_ref[h, rows_of(kb), :], qx[h], preferred_element_type=F32)

        def consume(kb, src, m, masked):
            m_out, alpha, pv = [], [], []
            for h in range(2):
                sh = src[h]
                if masked:
                    kpos = kb * tk + lax.broadcasted_iota(jnp.int32, (tk, tq), 0)
                    qpos = i * tq + lax.broadcasted_iota(jnp.int32, (tk, tq), 1)
                    sh = jnp.where(qpos >= kpos, sh, NEG)
                m_new = jnp.maximum(m[h], jnp.max(sh, axis=0, keepdims=True))
                pt = jnp.exp(sh - m_new).astype(BF16)
                alpha.append(jnp.exp(m[h] - m_new))
                pv.append(jnp.dot(vt_ref[h, :, rows_of(kb)], pt, preferred_element_type=F32))
                m_out.append(m_new)
            for h in range(2):
                accs[h][...] = accs[h][...] * alpha[h] + pv[h]
            return tuple(m_out)

        n_full = (i * tq) // tk

        def two_blocks(t, m):
            kb = 2 * t
            scores(kb + 1, stb_ref)
            m = consume(kb, sta_ref, m, False)
            scores(kb + 2, sta_ref)
            return consume(kb + 1, stb_ref, m, False)

        def odd_tail(m):
            scores(n_full, stb_ref)
            m = consume(n_full - 1, sta_ref, m, False)
            return consume(n_full, stb_ref, m, True)

        def even_tail(m):
            return consume(n_full, sta_ref, m, True)

        scores(0, sta_ref)
        m0 = jnp.full((1, tq), NEG, F32)
        m = lax.fori_loop(0, n_full // 2, two_blocks, (m0, m0))
        m = lax.cond(lax.rem(n_full, 2) == 1, odd_tail, even_tail, m)

        sub_o = lax.broadcasted_iota(jnp.int32, (PAIR, tq), 0)
        den = [accs[h][_spare(h):_spare(h) + 1, :] for h in range(2)]
        ot = jnp.where(sub_o < HEAD_DIM, acc0_ref[...] / den[0], acc1_ref[...] / den[1])
        o = jnp.transpose(ot)
        o_ref[...] = o.astype(BF16)
        gate = ag_ref[...].astype(F32)
        ma_ref[...] = (o * (gate * _sigmoid(gate))).astype(BF16)
        sub8 = lax.broadcasted_iota(jnp.int32, (8, tq), 0)
        lse_ref[...] = jnp.where(sub8 == 0, m[0] + jnp.log(den[0]), m[1] + jnp.log(den[1]))

    return pl.pallas_call(
        body, name="attn_fwd", grid=(N_PAIRS, s // tq),
        in_specs=[pl.BlockSpec((tq, PAIR), lambda j, i: (i, j)),
                  pl.BlockSpec((s, PAIR), lambda j, i: (0, j)),
                  pl.BlockSpec((s, PAIR), lambda j, i: (0, j)),
                  pl.BlockSpec((s, 128), lambda j, i: (0, 0)),
                  pl.BlockSpec((tq, PAIR), lambda j, i: (i, j))],
        out_specs=[pl.BlockSpec((tq, PAIR), lambda j, i: (i, j)),
                   pl.BlockSpec((tq, PAIR), lambda j, i: (i, j)),
                   pl.BlockSpec((None, 8, tq), lambda j, i: (j, 0, i))],
        out_shape=[jax.ShapeDtypeStruct((s, ATTN_WIDTH), BF16), jax.ShapeDtypeStruct((s, ATTN_WIDTH), BF16),
                   jax.ShapeDtypeStruct((N_PAIRS, 8, s), F32)],
        scratch_shapes=[pltpu.VMEM((2, s, PAIR), BF16), pltpu.VMEM((2, PAIR, s), BF16),
                        pltpu.VMEM((PAIR, tq), F32), pltpu.VMEM((PAIR, tq), F32),
                        pltpu.VMEM((2, tk, tq), F32), pltpu.VMEM((2, tk, tq), F32)],
        compiler_params=_params(("arbitrary", "arbitrary")),
    )(q, k, v, c, ag)


def _outproj_fwd(x, po, ma, wo, tm=512):
    s = x.shape[0]

    def body(x_ref, po_ref, ma_ref, wo_ref, xn_ref):
        xn_ref[...] = (x_ref[...]
                       + jnp.dot(po_ref[...], wo_ref[0:POOL_WIDTH, :], preferred_element_type=F32)
                       + jnp.dot(ma_ref[...], wo_ref[POOL_WIDTH:, :], preferred_element_type=F32))

    return pl.pallas_call(
        body, name="outproj_fwd", grid=(s // tm,),
        in_specs=[_row_spec(tm, D_MODEL), _row_spec(tm, 512), _row_spec(tm, 512),
                  _full_spec((D_MODEL, D_MODEL))],
        out_specs=_row_spec(tm, D_MODEL),
        out_shape=jax.ShapeDtypeStruct((s, D_MODEL), F32),
        compiler_params=_params(("parallel",)),
    )(x, po, ma, wo)


def _loss_head(x, tgt, g, tm=512):
    s = x.shape[0]

    def body(x_ref, t_ref, g_ref, dx_ref, dg_ref, loss_ref):
        @pl.when(pl.program_id(0) == 0)
        def _():
            dg_ref[...] = jnp.zeros_like(dg_ref)
            loss_ref[...] = jnp.zeros_like(loss_ref)

        xf = x_ref[...]
        r = lax.rsqrt(jnp.mean(xf * xf, axis=-1, keepdims=True) + RMS_EPS)
        xh = xf * r
        gg = g_ref[...]
        e = xh * gg - t_ref[...]
        loss_ref[...] += 0.5 * jnp.sum(jnp.mean(e * e, axis=-1, keepdims=True))
        dy = e * (1.0 / D_MODEL)
        u = dy * gg
        dx_ref[...] = r * (u - xh * jnp.mean(xh * u, axis=-1, keepdims=True))
        dg_ref[...] += jnp.sum(dy * xh, axis=0, keepdims=True)

    return pl.pallas_call(
        body, name="loss_head", grid=(s // tm,),
        in_specs=[_row_spec(tm, D_MODEL), _row_spec(tm, D_MODEL), _full_spec((1, D_MODEL))],
        out_specs=[_row_spec(tm, D_MODEL), _full_spec((1, D_MODEL)), _full_spec((8, 128))],
        out_shape=[jax.ShapeDtypeStruct((s, D_MODEL), F32), jax.ShapeDtypeStruct((1, D_MODEL), F32),
                   jax.ShapeDtypeStruct((8, 128), F32)],
        compiler_params=_params(("arbitrary",)),
    )(x, tgt, g)


def _outproj_bwd(dx, po, ma, ag, o, wo, head_sel, tm=512):
    s = dx.shape[0]

    def body(dx_ref, po_ref, ma_ref, ag_ref, o_ref, wo_ref, sel_ref, dmp_ref, do_ref, dag_ref, dt_ref, gwo_ref):
        @pl.when(pl.program_id(0) == 0)
        def _():
            gwo_ref[...] = jnp.zeros_like(gwo_ref)

        dxb = dx_ref[...].astype(BF16)
        dm = lax.dot_general(dxb, wo_ref[...], NT, preferred_element_type=F32)
        dmp_ref[...] = dm[:, 0:POOL_WIDTH].astype(BF16)
        dma = dm[:, POOL_WIDTH:]
        gate = ag_ref[...].astype(F32)
        of = o_ref[...].astype(F32)
        sg = _sigmoid(gate)
        do = dma * (gate * sg)
        do_ref[...] = do.astype(BF16)
        dag_ref[...] = (dma * of * (sg * (1.0 + gate * (1.0 - sg)))).astype(BF16)
        prod = do * of
        hi = prod.astype(BF16)
        lo = (prod - hi.astype(F32)).astype(BF16)
        dt_ref[...] = (lax.dot_general(sel_ref[...], hi, NT, preferred_element_type=F32)
                       + lax.dot_general(sel_ref[...], lo, NT, preferred_element_type=F32))
        gwo_ref[0:POOL_WIDTH, :] += lax.dot_general(po_ref[...], dxb, TN, preferred_element_type=F32)
        gwo_ref[POOL_WIDTH:, :] += lax.dot_general(ma_ref[...], dxb, TN, preferred_element_type=F32)

    act = jax.ShapeDtypeStruct((s, 512), BF16)
    return pl.pallas_call(
        body, name="outproj_bwd", grid=(s // tm,),
        in_specs=[_row_spec(tm, D_MODEL)] + [_row_spec(tm, 512)] * 4
                 + [_full_spec((D_MODEL, D_MODEL)), _full_spec((8, ATTN_WIDTH))],
        out_specs=[_row_spec(tm, 512)] * 3 + [pl.BlockSpec((8, tm), lambda i: (0, i)),
                                              _full_spec((D_MODEL, D_MODEL))],
        out_shape=[act, act, act, jax.ShapeDtypeStruct((8, s), F32),
                   jax.ShapeDtypeStruct((D_MODEL, D_MODEL), F32)],
        compiler_params=_params(("arbitrary",)),
    )(dx, po, ma, ag, o, wo, head_sel)


def _attn_bwd(q, k, v, do, c, lse, dt, tq=256, chunk=512):
    s = q.shape[0]
    tk = 2 * tq
    nq = s // tq

    def body(q_ref, k_ref, v_ref, do_ref, c_ref, lse_ref, dt_ref, dq_ref, dk_ref, dv_ref, dcs_ref, drs_ref,
             qxt_ref, doxt_ref, qm_ref, dox_ref, dqt0_ref, dqt1_ref, dk0_ref, dk1_ref, dvx_ref,
             kx_ref, vx_ref, ktm_ref, sa_ref, sb_ref):
        j = pl.program_id(0)
        b = pl.program_id(1)
        dqts = (dqt0_ref, dqt1_ref)
        dks = (dk0_ref, dk1_ref)

        @pl.when(b == 0)
        def _():
            def prep(ch, carry):
                rows = pl.ds(pl.multiple_of(ch * chunk, chunk), chunk)
                qq = q_ref[rows, :].astype(F32)
                dd = do_ref[rows, :].astype(F32)
                qt = jnp.transpose(qq)
                ddt = jnp.transpose(dd)
                lane = lax.broadcasted_iota(jnp.int32, (chunk, PAIR), 1)
                sub = lax.broadcasted_iota(jnp.int32, (PAIR, chunk), 0)
                for h in range(2):
                    sp = _spare(h)
                    qm_ref[h, rows, :] = jnp.where(_own(lane, h), qq, jnp.where(lane == sp, 1.0, 0.0)).astype(BF16)
                    dox_ref[h, rows, :] = jnp.where(_own(lane, h), dd, 0.0).astype(BF16)
                    qx = jnp.where(_own(sub, h), qt, jnp.where((sub >= sp) & (sub < sp + 3), 1.0, 0.0))
                    for t, part in enumerate(_split3(-lse_ref[pl.ds(h, 1), rows])):
                        qx = jnp.where(sub == sp + 3 + t, part, qx)
                    qxt_ref[h, :, rows] = qx.astype(BF16)
                    dx = jnp.where(_own(sub, h), ddt, 0.0)
                    for t, part in enumerate(_split3(-dt_ref[pl.ds(h, 1), rows])):
                        dx = jnp.where(sub == sp + t, part, dx)
                    doxt_ref[h, :, rows] = dx.astype(BF16)
                return carry

            lax.fori_loop(0, s // chunk, prep, 0)
            for ref in dqts:
                ref[...] = jnp.zeros_like(ref)

        kk = k_ref[...].astype(F32)
        vv = v_ref[...].astype(F32)
        cc = c_ref[...]
        kt = jnp.transpose(kk)
        lane = lax.broadcasted_iota(jnp.int32, (tk, PAIR), 1)
        sub = lax.broadcasted_iota(jnp.int32, (PAIR, tk), 0)
        for h in range(2):
            sp = _spare(h)
            ccol = jnp.sum(jnp.where(lane == 2 * j + h, cc, 0.0), axis=1, keepdims=True)
            kx = jnp.where(_own(lane, h), kk, jnp.where((lane >= sp + 3) & (lane < sp + 6), 1.0, 0.0))
            for t, part in enumerate(_split3(-ccol)):
                kx = jnp.where(lane == sp + t, part, kx)
            kx_ref[h] = kx.astype(BF16)
            vx_ref[h] = jnp.where(_own(lane, h), vv, jnp.where((lane >= sp) & (lane < sp + 3), 1.0, 0.0)).astype(BF16)
            ktm_ref[h] = jnp.where(_own(sub, h), kt, jnp.where(sub == sp, 1.0, 0.0)).astype(BF16)
        for ref in dks:
            ref[...] = jnp.zeros_like(ref)
        dvx_ref[...] = jnp.zeros_like(dvx_ref)

        def cols_of(i):
            return pl.ds(pl.multiple_of(i * tq, tq), tq)

        def scores(i, dst):
            cols = cols_of(jnp.minimum(i, nq - 1))
            for h in range(2):
                dst[h] = jnp.dot(kx_ref[h], qxt_ref[h, :, cols], preferred_element_type=F32)
                dst[2 + h] = jnp.dot(vx_ref[h], doxt_ref[h, :, cols], preferred_element_type=F32)

        def consume(i, src, masked):
            cols = cols_of(i)
            for h in range(2):
                arg = src[h]
                if masked:
                    kpos = b * tk + lax.broadcasted_iota(jnp.int32, (tk, tq), 0)
                    qpos = i * tq + lax.broadcasted_iota(jnp.int32, (tk, tq), 1)
                    arg = jnp.where(qpos >= kpos, arg, NEG)
                pt = jnp.exp(arg)
                dst = (pt * src[2 + h]).astype(BF16)
                dvx_ref[...] += jnp.dot(pt.astype(BF16), dox_ref[h, cols, :], preferred_element_type=F32)
                dks[h][...] += jnp.dot(dst, qm_ref[h, cols, :], preferred_element_type=F32)
                dqts[h][:, cols] += jnp.dot(ktm_ref[h], dst, preferred_element_type=F32)

        i0 = 2 * b

        def two_blocks(t, masked):
            i = i0 + 2 * t
            scores(i + 1, sb_ref)
            consume(i, sa_ref, masked)
            scores(i + 2, sa_ref)
            consume(i + 1, sb_ref, masked)

        def loop_body(t, carry):
            two_blocks(t, False)
            return carry

        scores(i0, sa_ref)
        two_blocks(0, True)
        lax.fori_loop(1, (nq - i0) // 2, loop_body, 0)

        dv_ref[...] = dvx_ref[...].astype(BF16)
        dk_ref[...] = jnp.where(lane < HEAD_DIM, dk0_ref[...], dk1_ref[...]).astype(BF16)
        dcs_ref[...] = jnp.where(lane == _spare(0), dk0_ref[...], jnp.where(lane == _spare(1), dk1_ref[...], 0.0))

        @pl.when(b == s // tk - 1)
        def _():
            sub8 = lax.broadcasted_iota(jnp.int32, (8, s), 0)
            drs_ref[...] = jnp.where(sub8 == 0, dqt0_ref[_spare(0):_spare(0) + 1, :],
                                     jnp.where(sub8 == 1, dqt1_ref[_spare(1):_spare(1) + 1, :], 0.0))
            sub_c = lax.broadcasted_iota(jnp.int32, (PAIR, chunk), 0)
            for ch in range(s // chunk):
                rows = pl.ds(ch * chunk, chunk)
                both = jnp.where(sub_c < HEAD_DIM, dqt0_ref[:, rows], dqt1_ref[:, rows])
                dq_ref[rows, :] = (jnp.transpose(both) * Q_SCALE).astype(BF16)

    act = jax.ShapeDtypeStruct((s, ATTN_WIDTH), BF16)
    pair_rows = pl.BlockSpec((s, PAIR), lambda j, b: (0, j))
    pair_blk = pl.BlockSpec((tk, PAIR), lambda j, b: (b, j))
    stat = pl.BlockSpec((None, 2, s), lambda j, b: (j, 0, 0))
    return pl.pallas_call(
        body, name="attn_bwd", grid=(N_PAIRS, s // tk),
        in_specs=[pair_rows, pair_blk, pair_blk, pair_rows,
                  pl.BlockSpec((tk, 128), lambda j, b: (b, 0)),
                  pl.BlockSpec((None, 8, s), lambda j, b: (j, 0, 0)), stat],
        out_specs=[pair_rows, pair_blk, pair_blk, pair_blk, pl.BlockSpec((None, 8, s), lambda j, b: (j, 0, 0))],
        out_shape=[act, act, act, jax.ShapeDtypeStruct((s, N_PAIRS * 128), F32),
                   jax.ShapeDtypeStruct((N_PAIRS, 8, s), F32)],
        scratch_shapes=[pltpu.VMEM((2, PAIR, s), BF16), pltpu.VMEM((2, PAIR, s), BF16),
                        pltpu.VMEM((2, s, PAIR), BF16), pltpu.VMEM((2, s, PAIR), BF16),
                        pltpu.VMEM((PAIR, s), F32), pltpu.VMEM((PAIR, s), F32),
                        pltpu.VMEM((tk, PAIR), F32), pltpu.VMEM((tk, PAIR), F32), pltpu.VMEM((tk, PAIR), F32),
                        pltpu.VMEM((2, tk, PAIR), BF16), pltpu.VMEM((2, tk, PAIR), BF16),
                        pltpu.VMEM((2, PAIR, tk), BF16),
                        pltpu.VMEM((4, tk, tq), F32), pltpu.VMEM((4, tk, tq), F32)],
        compiler_params=_params(("arbitrary", "arbitrary")),
    )(q, k, v, do, c, lse, dt)


def _forget_bwd(dcs, drs, lf, tm=256):
    s = lf.shape[0]
    n = s // tm

    def body(dcs_ref, drs_ref, lf_ref, dpf_ref, dfb_ref, carry_ref):
        @pl.when(pl.program_id(0) == 0)
        def _():
            carry_ref[...] = jnp.zeros_like(carry_ref)
            dfb_ref[...] = jnp.zeros_like(dfb_ref)

        lane = lax.broadcasted_iota(jnp.int32, (tm, 128), 1)
        dc = drs_ref[...]
        for pj in range(N_PAIRS):
            blk = dcs_ref[:, pj * 128:(pj + 1) * 128]
            for h in range(2):
                head = 2 * pj + h
                moved = pltpu.roll(blk, (head - _spare(h)) % 128, 1) if head != _spare(h) else blk
                dc = dc - jnp.where(lane == head, moved, 0.0)
        dlf = _scan_rows(dc, reverse=True) + carry_ref[0:1, :]
        carry_ref[...] = dlf[0:8, :]
        dz = jnp.where(lane < 8, dlf * (1.0 - jnp.exp(lf_ref[...])), 0.0)
        dpf_ref[...] = dz.astype(BF16)
        dfb_ref[...] += jnp.sum(dz, axis=0, keepdims=True)

    return pl.pallas_call(
        body, name="forget_bwd", grid=(n,),
        in_specs=[pl.BlockSpec((tm, N_PAIRS * 128), lambda i: (n - 1 - i, 0)),
                  pl.BlockSpec((tm, 128), lambda i: (n - 1 - i, 0)),
                  pl.BlockSpec((tm, 128), lambda i: (n - 1 - i, 0))],
        out_specs=[pl.BlockSpec((tm, 128), lambda i: (n - 1 - i, 0)), _full_spec((1, 128))],
        out_shape=[jax.ShapeDtypeStruct((s, 128), BF16), jax.ShapeDtypeStruct((1, 128), F32)],
        scratch_shapes=[pltpu.VMEM((8, 128), F32)],
        compiler_params=_params(("arbitrary",)),
    )(dcs, drs, lf)


def _pool_bwd(pu, pg, dmp, pw, ps, tm=512):
    s = pu.shape[0]
    hb = tm // HALO
    n = s // tm
    last_halo = s // HALO - 1

    def body(pu_ref, halo_ref, pg_ref, dmp_ref, pgn_ref, dmpn_ref, pw_ref, ps_ref,
             dpu_ref, dpg_ref, gpw_ref, gps_ref):
        i = pl.program_id(0)

        @pl.when(i == 0)
        def _():
            gpw_ref[...] = jnp.zeros_like(gpw_ref)
            gps_ref[...] = jnp.zeros_like(gps_ref)

        u = pu_ref[...].astype(F32)
        halo = jnp.where(i == 0, 0.0, halo_ref[...].astype(F32))
        d = _pool_delta(u, halo, i * tm)
        cnt = _window_counts(i * tm, tm)
        cnt_next = _window_counts((i + 1) * tm, HALO)
        e_parts, dd_parts = [], []
        for gi in range(len(POOL_WINDOWS)):
            cols = slice(gi * GROUP_DIM, (gi + 1) * GROUP_DIM)
            wg = pw_ref[gi]
            scale = ps_ref[:, cols]
            db = d[gi].astype(BF16)
            z = jnp.dot(db, wg, preferred_element_type=F32)
            gate = pg_ref[:, cols].astype(F32)
            sg = _sigmoid(gate)
            dm = dmp_ref[:, cols].astype(F32)
            dy = dm * (gate * sg)
            dpg_ref[:, cols] = (dm * (z * scale) * (sg * (1.0 + gate * (1.0 - sg)))).astype(BF16)
            gps_ref[:, cols] += jnp.sum(dy * z, axis=0, keepdims=True)
            dz = (dy * scale).astype(BF16)
            gpw_ref[gi] += lax.dot_general(db, dz, TN, preferred_element_type=F32)
            dd = lax.dot_general(dz, wg, NT, preferred_element_type=F32)
            gate_n = pgn_ref[:, cols].astype(F32)
            dz_n = (dmpn_ref[:, cols].astype(F32) * (gate_n * _sigmoid(gate_n)) * scale).astype(BF16)
            dd_n = lax.dot_general(dz_n, wg, NT, preferred_element_type=F32)
            dd_n = jnp.where(i == n - 1, 0.0, dd_n)
            dd_parts.append(dd)
            e_parts.append(jnp.concatenate([dd / cnt[gi], dd_n / cnt_next[gi]], axis=0))
        lead = _window_sums(jnp.concatenate(e_parts, axis=1), forward=True)
        for gi in range(len(POOL_WINDOWS)):
            cols = slice(gi * GROUP_DIM, (gi + 1) * GROUP_DIM)
            dpu_ref[:, cols] = (lead[gi][0:tm, :] - dd_parts[gi]).astype(BF16)

    act = jax.ShapeDtypeStruct((s, 512), BF16)
    prev_halo = pl.BlockSpec((HALO, 512), lambda i: (jnp.maximum(i * hb - 1, 0), 0))
    next_halo = pl.BlockSpec((HALO, 512), lambda i: (jnp.minimum((i + 1) * hb, last_halo), 0))
    return pl.pallas_call(
        body, name="pool_bwd", grid=(n,),
        in_specs=[_row_spec(tm, 512), prev_halo, _row_spec(tm, 512), _row_spec(tm, 512), next_halo, next_halo,
                  _full_spec((4, GROUP_DIM, GROUP_DIM)), _full_spec((1, 512))],
        out_specs=[_row_spec(tm, 512), _row_spec(tm, 512), _full_spec((4, GROUP_DIM, GROUP_DIM)),
                   _full_spec((1, 512))],
        out_shape=[act, act, jax.ShapeDtypeStruct((4, GROUP_DIM, GROUP_DIM), F32),
                   jax.ShapeDtypeStruct((1, 512), F32)],
        compiler_params=_params(("arbitrary",)),
    )(pu, pu, pg, dmp, pg, dmp, pw, ps)


def _inproj_bwd_x(dsec, w, x, g, dxo, tm=512):
    s = x.shape[0]

    def body(*refs):
        d_refs = refs[0:7]
        w_ref, x_ref, g_ref, dxo_ref, dxi_ref, h_ref, dg_ref = refs[7:]

        @pl.when(pl.program_id(0) == 0)
        def _():
            dg_ref[...] = jnp.zeros_like(dg_ref)

        dh = None
        for d_ref, (off, ncol) in zip(d_refs, SECTIONS):
            t = lax.dot_general(d_ref[...], w_ref[:, off:off + ncol], NT, preferred_element_type=F32)
            dh = t if dh is None else dh + t
        xf = x_ref[...]
        r = lax.rsqrt(jnp.mean(xf * xf, axis=-1, keepdims=True) + RMS_EPS)
        xh = xf * r
        gg = g_ref[...]
        h_ref[...] = (xh * gg).astype(BF16)
        u = dh * gg
        dxi_ref[...] = dxo_ref[...] + r * (u - xh * jnp.mean(xh * u, axis=-1, keepdims=True))
        dg_ref[...] += jnp.sum(dh * xh, axis=0, keepdims=True)

    return pl.pallas_call(
        body, name="inproj_bwd_x", grid=(s // tm,),
        in_specs=[_row_spec(tm, ncol) for _, ncol in SECTIONS]
                 + [_full_spec((D_MODEL, IN_COLS_PAD)), _row_spec(tm, D_MODEL), _full_spec((1, D_MODEL)),
                    _row_spec(tm, D_MODEL)],
        out_specs=[_row_spec(tm, D_MODEL), _row_spec(tm, D_MODEL), _full_spec((1, D_MODEL))],
        out_shape=[jax.ShapeDtypeStruct((s, D_MODEL), F32), jax.ShapeDtypeStruct((s, D_MODEL), BF16),
                   jax.ShapeDtypeStruct((1, D_MODEL), F32)],
        compiler_params=_params(("arbitrary",)),
    )(*dsec, w, x, g, dxo)


def _inproj_bwd_w(h, dsec, tm=512):
    s = h.shape[0]

    def body(*refs):
        h_ref = refs[0]
        d_refs = refs[1:8]
        gw_ref = refs[8]

        @pl.when(pl.program_id(0) == 0)
        def _():
            gw_ref[...] = jnp.zeros_like(gw_ref)

        hh = h_ref[...]
        for d_ref, (off, ncol) in zip(d_refs, SECTIONS):
            gw_ref[:, off:off + ncol] += lax.dot_general(hh, d_ref[...], TN, preferred_element_type=F32)

    return pl.pallas_call(
        body, name="inproj_bwd_w", grid=(s // tm,),
        in_specs=[_row_spec(tm, D_MODEL)] + [_row_spec(tm, ncol) for _, ncol in SECTIONS],
        out_specs=_full_spec((D_MODEL, IN_COLS_PAD)),
        out_shape=jax.ShapeDtypeStruct((D_MODEL, IN_COLS_PAD), F32),
        compiler_params=_params(("arbitrary",), vmem_mb=56),
    )(h, *dsec)


def _elementwise(fn, name, n_out, arrays, tm):
    rows, cols = arrays[0].shape
    n_in = len(arrays)

    def body(*refs):
        outs = fn(*[r[...] for r in refs[:n_in]])
        for r, val in zip(refs[n_in:], outs):
            r[...] = val

    spec = _row_spec(tm, cols)
    shape = jax.ShapeDtypeStruct((rows, cols), F32)
    return pl.pallas_call(
        body, name=name, grid=(rows // tm,),
        in_specs=[spec] * n_in, out_specs=[spec] * n_out, out_shape=[shape] * n_out,
        compiler_params=_params(("parallel",)),
    )(*arrays)


def _add2(a, b, name, tm):
    return _elementwise(lambda p, q: (p + q,), name, 1, [a, b], tm)[0]


def _sum4(parts, name, tm):
    _, rows, cols = parts.shape

    def body(p_ref, o_ref):
        o_ref[...] = (p_ref[0] + p_ref[1]) + (p_ref[2] + p_ref[3])

    return pl.pallas_call(
        body, name=name, grid=(rows // tm,),
        in_specs=[pl.BlockSpec((4, tm, cols), lambda i: (0, i, 0))],
        out_specs=_row_spec(tm, cols), out_shape=jax.ShapeDtypeStruct((rows, cols), F32),
        compiler_params=_params(("parallel",)),
    )(parts)


def _adamw(g, w, m, v, name, tm):
    def fn(g, w, m, v):
        m = ADAM_B1 * m + (1.0 - ADAM_B1) * g
        v = ADAM_B2 * v + (1.0 - ADAM_B2) * (g * g)
        m_hat = m / (1.0 - ADAM_B1 ** ADAM_STEP)
        v_hat = v / (1.0 - ADAM_B2 ** ADAM_STEP)
        delta = -ADAM_LR * (m_hat / (jnp.sqrt(v_hat) + ADAM_EPS) + ADAM_WD * w)
        return g, delta, m, v

    return _elementwise(fn, name, 4, [g, w, m, v], tm)


def _position():
    x, y, c = lax.axis_index("x"), lax.axis_index("y"), lax.axis_index("c")
    other_chips = [(1 - x, y), (x, 1 - y), (1 - x, 1 - y)]
    return x, y, c, other_chips


def _remote(src, dst, sems, k, to):
    send_sems, recv_sems = sems
    return pltpu.make_async_remote_copy(src_ref=src, dst_ref=dst, send_sem=send_sems.at[k],
                                        recv_sem=recv_sems.at[k], device_id=to, device_id_type=MESH)


def _comm_call(body, name, n_in, out_shape, n_remote, n_local):
    return pl.pallas_call(
        body, name=name, in_specs=[ANY] * n_in, out_specs=[ANY] * len(out_shape), out_shape=out_shape,
        scratch_shapes=[pltpu.SemaphoreType.DMA((n_remote,)), pltpu.SemaphoreType.DMA((n_remote,)),
                        pltpu.SemaphoreType.DMA((n_local,))],
    )


def _gather_weights(wi, wo):
    def body(wi_ref, wo_ref, gi_ref, go_ref, send_sems, recv_sems, local_sems):
        x, y, c, chips = _position()
        sems = (send_sems, recv_sems)
        me = 2 * x + y
        mine, other = pl.ds(2 * c, 2), pl.ds(2 * (1 - c), 2)
        own = [pltpu.make_async_copy(wi_ref, gi_ref.at[me], local_sems.at[0]),
               pltpu.make_async_copy(wo_ref, go_ref.at[:, me], local_sems.at[1])]
        for cp in own:
            cp.start()
        sent = []
        for k, (cx, cy) in enumerate(chips):
            sent.append(_remote(wi_ref.at[mine], gi_ref.at[me, mine], sems, 2 * k, (cx, cy, c)))
            sent.append(_remote(wo_ref.at[mine], go_ref.at[mine, me], sems, 2 * k + 1, (cx, cy, c)))
        for cp in sent:
            cp.start()
        for k, (cx, cy) in enumerate(chips):
            ck = 2 * cx + cy
            for a, blk in ((0, gi_ref.at[ck, mine]), (1, go_ref.at[mine, ck])):
                _remote(blk, blk, sems, 2 * k + a, (x, y, c)).wait_recv()
                fwd = _remote(blk, blk, sems, 6 + 2 * k + a, (x, y, 1 - c))
                fwd.start()
                sent.append(fwd)
        for k, (cx, cy) in enumerate(chips):
            ck = 2 * cx + cy
            for a, blk in ((0, gi_ref.at[ck, other]), (1, go_ref.at[other, ck])):
                _remote(blk, blk, sems, 6 + 2 * k + a, (x, y, c)).wait_recv()
        for cp in sent:
            cp.wait_send()
        for cp in own:
            cp.wait()

    out_shape = [jax.ShapeDtypeStruct((N_CHIPS,) + wi.shape, wi.dtype),
                 jax.ShapeDtypeStruct((DEPTH, N_CHIPS) + wo.shape[1:], wo.dtype)]
    return _comm_call(body, "gather_weights", 2, out_shape, 12, 2)(wi, wo)


def _exchange_sibling(gi, go, sm):
    def body(gi_ref, go_ref, sm_ref, ri_ref, ro_ref, rs_ref, send_sems, recv_sems, local_sems):
        x, y, c, _ = _position()
        sems = (send_sems, recv_sems)
        other = pl.ds(2 * (1 - c), 2)
        cps = [_remote(gi_ref.at[:, other], ri_ref, sems, 0, (x, y, 1 - c)),
               _remote(go_ref.at[other], ro_ref, sems, 1, (x, y, 1 - c)),
               _remote(sm_ref, rs_ref, sems, 2, (x, y, 1 - c))]
        for cp in cps:
            cp.start()
        for cp in cps:
            cp.wait()

    out_shape = [jax.ShapeDtypeStruct((N_CHIPS, 2) + gi.shape[2:], F32),
                 jax.ShapeDtypeStruct((2,) + go.shape[1:], F32),
                 jax.ShapeDtypeStruct(sm.shape, F32)]
    return _comm_call(body, "exchange_sibling", 3, out_shape, 3, 1)(gi, go, sm)


def _scatter_chips(pi, po, ps):
    def body(pi_ref, po_ref, ps_ref, ri_ref, ro_ref, rs_ref, send_sems, recv_sems, local_sems):
        x, y, c, chips = _position()
        sems = (send_sems, recv_sems)
        me = 2 * x + y
        half = pl.ds(SMALL_HALF * c, SMALL_HALF)
        own = [pltpu.make_async_copy(pi_ref.at[me], ri_ref.at[me], local_sems.at[0]),
               pltpu.make_async_copy(po_ref.at[:, me], ro_ref.at[me], local_sems.at[1]),
               pltpu.make_async_copy(ps_ref.at[half], rs_ref.at[me], local_sems.at[2])]
        for cp in own:
            cp.start()
        sent = []
        for k, (cx, cy) in enumerate(chips):
            ck = 2 * cx + cy
            sent.append(_remote(pi_ref.at[ck], ri_ref.at[me], sems, 3 * k, (cx, cy, c)))
            sent.append(_remote(po_ref.at[:, ck], ro_ref.at[me], sems, 3 * k + 1, (cx, cy, c)))
            sent.append(_remote(ps_ref.at[half], rs_ref.at[me], sems, 3 * k + 2, (cx, cy, c)))
        for cp in sent:
            cp.start()
        for k, (cx, cy) in enumerate(chips):
            ck = 2 * cx + cy
            for a, blk in ((0, ri_ref.at[ck]), (1, ro_ref.at[ck]), (2, rs_ref.at[ck])):
                _remote(blk, blk, sems, 3 * k + a, (x, y, c)).wait_recv()
        for cp in sent:
            cp.wait_send()
        for cp in own:
            cp.wait()

    out_shape = [jax.ShapeDtypeStruct(pi.shape, F32),
                 jax.ShapeDtypeStruct((N_CHIPS, 2) + po.shape[2:], F32),
                 jax.ShapeDtypeStruct((N_CHIPS, SMALL_HALF, 128), F32)]
    return _comm_call(body, "scatter_chips", 3, out_shape, 9, 3)(pi, po, ps)


def _join_sibling(hi, ho, hs):
    def body(hi_ref, ho_ref, hs_ref, fi_ref, fo_ref, fs_ref, send_sems, recv_sems, local_sems):
        x, y, c, _ = _position()
        sems = (send_sems, recv_sems)
        mine, other = pl.ds(2 * c, 2), pl.ds(2 * (1 - c), 2)
        half, other_half = pl.ds(SMALL_HALF * c, SMALL_HALF), pl.ds(SMALL_HALF * (1 - c), SMALL_HALF)
        own = [pltpu.make_async_copy(hi_ref, fi_ref.at[mine], local_sems.at[0]),
               pltpu.make_async_copy(ho_ref, fo_ref.at[mine], local_sems.at[1]),
               pltpu.make_async_copy(hs_ref, fs_ref.at[half], local_sems.at[2])]
        for cp in own:
            cp.start()
        sent = [_remote(hi_ref, fi_ref.at[mine], sems, 0, (x, y, 1 - c)),
                _remote(ho_ref, fo_ref.at[mine], sems, 1, (x, y, 1 - c)),
                _remote(hs_ref, fs_ref.at[half], sems, 2, (x, y, 1 - c))]
        for cp in sent:
            cp.start()
        for a, blk in ((0, fi_ref.at[other]), (1, fo_ref.at[other]), (2, fs_ref.at[other_half])):
            _remote(blk, blk, sems, a, (x, y, c)).wait_recv()
        for cp in sent:
            cp.wait_send()
        for cp in own:
            cp.wait()

    out_shape = [jax.ShapeDtypeStruct((DEPTH,) + hi.shape[1:], F32),
                 jax.ShapeDtypeStruct((DEPTH,) + ho.shape[1:], F32),
                 jax.ShapeDtypeStruct((SMALL_ROWS, 128), F32)]
    return _comm_call(body, "join_sibling", 3, out_shape, 3, 3)(hi, ho, hs)


def _pack_small(norm_g, final_g, pool_scale, forget_bias, pool_w):
    fb = jnp.pad(forget_bias.reshape(-1), (0, 128 - forget_bias.size))
    flat = jnp.concatenate([norm_g.reshape(-1), final_g.reshape(-1), pool_scale.reshape(-1), fb,
                            pool_w.reshape(-1)])
    return jnp.pad(flat, (0, SMALL_ROWS * 128 - flat.size)).reshape(SMALL_ROWS, 128)


def _unpack_small(packed):
    flat = packed.reshape(-1)
    sizes = (DEPTH * D_MODEL, D_MODEL, DEPTH * POOL_WIDTH, 128, DEPTH * 4 * GROUP_DIM * GROUP_DIM)
    offs = [0]
    for n in sizes:
        offs.append(offs[-1] + n)
    norm_g, final_g, pool_scale, fb, pool_w = [flat[offs[i]:offs[i + 1]] for i in range(5)]
    return (norm_g.reshape(DEPTH, D_MODEL), final_g, pool_scale.reshape(DEPTH, POOL_WIDTH),
            fb[:DEPTH * 8].reshape(DEPTH, 8), pool_w.reshape(DEPTH, 4, GROUP_DIM, GROUP_DIM))


def kernel(x, norm_g, w_in, forget_bias, pool_w, pool_scale, w_out, final_g, loss_target, m_norm_g, m_w_in, m_forget_bias, m_pool_w, m_pool_scale, m_w_out, m_final_g, v_norm_g, v_w_in, v_forget_bias, v_pool_w, v_pool_scale, v_w_out, v_final_g):
    x0 = x[0]
    tgt = loss_target[0]
    s = x0.shape[0]

    gi, go = _gather_weights(w_in.astype(BF16), w_out.astype(BF16))
    w_in_full = jnp.concatenate([gi[k] for k in range(N_CHIPS)], axis=-1)
    w_in_full = jnp.pad(w_in_full, ((0, 0), (0, 0), (0, IN_COLS_PAD - IN_COLS)))
    w_out_full = go.reshape(DEPTH, D_MODEL, D_MODEL)
    pool_w_b = pool_w.astype(BF16)
    fb_pad = jnp.pad(forget_bias, ((0, 0), (0, 128 - forget_bias.shape[1])))
    head_sel = (jnp.arange(ATTN_WIDTH)[None, :] // HEAD_DIM == jnp.arange(8)[:, None]).astype(BF16)

    saved = []
    xl = x0
    for l in range(DEPTH):
        pu, pg, q, k, v, ag, lf, c = _inproj_fwd(xl, norm_g[l][None], w_in_full[l], fb_pad[l][None])
        po = _pool_fwd(pu, pg, pool_w_b[l], pool_scale[l][None])
        ma, o, lse = _attn_fwd(q, k, v, c, ag)
        saved.append((xl, pu, pg, q, k, v, ag, lf, c, po, ma, o, lse))
        xl = _outproj_fwd(xl, po, ma, w_out_full[l])

    dx, g_final, loss_part = _loss_head(xl, tgt, final_g[None])
    loss = lax.psum(loss_part[0, 0], ("x", "y", "c"))

    g_norm, g_win, g_fb, g_pw, g_ps, g_wo = [], [], [], [], [], []
    for l in reversed(range(DEPTH)):
        xl, pu, pg, q, k, v, ag, lf, c, po, ma, o, lse = saved[l]
        dmp, do, dag, dt, gwo = _outproj_bwd(dx, po, ma, ag, o, w_out_full[l], head_sel)
        dq, dk, dv, dcs, drs = _attn_bwd(q, k, v, do, c, lse, dt.reshape(N_PAIRS, 2, s))
        drs = jnp.pad(drs[:, 0:2, :].reshape(8, s).T, ((0, 0), (0, 120)))
        dpf, dfb = _forget_bwd(dcs, drs, lf)
        dpu, dpg, gpw, gps = _pool_bwd(pu, pg, dmp, pool_w_b[l], pool_scale[l][None])
        dsec = (dpu, dpg, dq, dk, dv, dag, dpf)
        dx, h, gn = _inproj_bwd_x(dsec, w_in_full[l], xl, norm_g[l][None], dx)
        g_win.append(_inproj_bwd_w(h, dsec))
        g_norm.append(gn[0])
        g_fb.append(dfb[0, :8])
        g_pw.append(gpw)
        g_ps.append(gps[0])
        g_wo.append(gwo)
    g_norm, g_win, g_fb, g_pw, g_ps, g_wo = [t[::-1] for t in (g_norm, g_win, g_fb, g_pw, g_ps, g_wo)]

    c_idx = lax.axis_index("c")
    gwin = jnp.stack(g_win)
    send_i = jnp.stack([gwin[:, :, k * SHARD_COLS:(k + 1) * SHARD_COLS] for k in range(N_CHIPS)])
    send_o = jnp.stack(g_wo).reshape(DEPTH, N_CHIPS, SHARD_ROWS, D_MODEL)
    send_s = _pack_small(jnp.stack(g_norm), g_final[0], jnp.stack(g_ps), jnp.stack(g_fb), jnp.stack(g_pw))
    ri, ro, rs = _exchange_sibling(send_i, send_o, send_s)
    keep_i = lax.dynamic_slice_in_dim(send_i, 2 * c_idx, 2, axis=1)
    keep_o = lax.dynamic_slice_in_dim(send_o, 2 * c_idx, 2, axis=0)
    pi = _add2(keep_i.reshape(-1, SHARD_COLS), ri.reshape(-1, SHARD_COLS), "chip_sum_w_in", 256).reshape(ri.shape)
    po_ = _add2(keep_o.reshape(-1, D_MODEL), ro.reshape(-1, D_MODEL), "chip_sum_w_out", 256).reshape(ro.shape)
    ps_ = _add2(send_s, rs, "chip_sum_small", SMALL_ROWS // 4)
    qi, qo, qs = _scatter_chips(pi, po_, ps_)
    hi = _sum4(qi.reshape(N_CHIPS, -1, SHARD_COLS), "mesh_sum_w_in", 256).reshape(2, D_MODEL, SHARD_COLS)
    ho = _sum4(qo.reshape(N_CHIPS, -1, D_MODEL), "mesh_sum_w_out", 256).reshape(2, SHARD_ROWS, D_MODEL)
    hs = _sum4(qs, "mesh_sum_small", SMALL_HALF // 4)
    fi, fo, fs = _join_sibling(hi, ho, hs)

    flat_i = lambda a: a.reshape(-1, SHARD_COLS)
    flat_o = lambda a: a.reshape(-1, D_MODEL)
    out_i = _adamw(flat_i(fi), flat_i(w_in), flat_i(m_w_in), flat_i(v_w_in), "adamw_w_in", 256)
    out_o = _adamw(flat_o(fo), flat_o(w_out), flat_o(m_w_out), flat_o(v_w_out), "adamw_w_out", 256)
    out_s = _adamw(fs, _pack_small(norm_g, final_g, pool_scale, forget_bias, pool_w),
                   _pack_small(m_norm_g, m_final_g, m_pool_scale, m_forget_bias, m_pool_w),
                   _pack_small(v_norm_g, v_final_g, v_pool_scale, v_forget_bias, v_pool_w),
                   "adamw_small", SMALL_ROWS // 4)

    groups = []
    for oi, oo, os_ in zip(out_i, out_o, out_s):
        sn, sf, sp, sb, sw = _unpack_small(os_)
        groups.append((sn, oi.reshape(w_in.shape), sb, sw, sp, oo.reshape(w_out.shape), sf))
    return (loss, dx[None]) + tuple(t for grp in groups for t in grp)
```

```python
import jax
import jax.numpy as jnp
from jax import lax
from jax.experimental import pallas as pl
from jax.experimental.pallas import tpu as pltpu

F32 = jnp.float32
BF16 = jnp.bfloat16

D_MODEL = 1024
DEPTH = 4
POOL_WIDTH = 512
ATTN_WIDTH = 512
HEAD_DIM = 64
PAIR = 2 * HEAD_DIM
N_PAIRS = ATTN_WIDTH // PAIR
POOL_WINDOWS = (2, 4, 8, 16)
GROUP_DIM = 128
HALO = 16
IN_COLS = 3080
OFF_F = 3072
IN_COLS_PAD = 3200
SECTIONS = ((0, 512), (512, 512), (1024, 512), (1536, 512), (2048, 512), (2560, 512), (OFF_F, 128))
N_CHIPS = 4
SHARD_COLS = IN_COLS // N_CHIPS
SHARD_ROWS = D_MODEL // N_CHIPS
RMS_EPS = 1e-6
NEG = -1e30
Q_SCALE = 0.125

ADAM_LR = 0.001
ADAM_B1 = 0.9
ADAM_B2 = 0.999
ADAM_EPS = 1e-08
ADAM_WD = 0.01
ADAM_STEP = 10

SMALL_ROWS = 2112
SMALL_HALF = SMALL_ROWS // 2

NT = (((1,), (1,)), ((), ()))
TN = (((0,), (0,)), ((), ()))
MESH = pl.DeviceIdType.MESH
ANY = pl.BlockSpec(memory_space=pl.ANY)


def _params(semantics, vmem_mb=48):
    return pltpu.CompilerParams(dimension_semantics=semantics, vmem_limit_bytes=vmem_mb << 20)


def _row_spec(tm, cols):
    return pl.BlockSpec((tm, cols), lambda i: (i, 0))


def _full_spec(shape):
    return pl.BlockSpec(shape, lambda *_: (0,) * len(shape))


def _sigmoid(x):
    return jax.nn.sigmoid(x)


def _scan_rows(a, reverse=False):
    n = a.shape[0]
    row = lax.broadcasted_iota(jnp.int32, a.shape, 0)
    k = 1
    while k < n:
        if reverse:
            a = a + jnp.where(row < n - k, pltpu.roll(a, n - k, 0), 0.0)
        else:
            a = a + jnp.where(row >= k, pltpu.roll(a, k, 0), 0.0)
        k *= 2
    return a


def _inproj_fwd(x, g, w, fb, tm=256):
    s = x.shape[0]

    def body(x_ref, g_ref, w_ref, fb_ref, pu_ref, pg_ref, q_ref, k_ref, v_ref, ag_ref,
             lf_ref, c_ref, carry_ref):
        @pl.when(pl.program_id(0) == 0)
        def _():
            carry_ref[...] = jnp.zeros_like(carry_ref)

        xf = x_ref[...]
        r = lax.rsqrt(jnp.mean(xf * xf, axis=-1, keepdims=True) + RMS_EPS)
        h = (xf * r * g_ref[...]).astype(BF16)

        def proj(sec):
            off, n = SECTIONS[sec]
            return jnp.dot(h, w_ref[:, off:off + n], preferred_element_type=F32)

        pu_ref[...] = proj(0).astype(BF16)
        pg_ref[...] = proj(1).astype(BF16)
        q_ref[...] = (proj(2) * Q_SCALE).astype(BF16)
        k_ref[...] = proj(3).astype(BF16)
        v_ref[...] = proj(4).astype(BF16)
        ag_ref[...] = proj(5).astype(BF16)
        z = proj(6) + fb_ref[...]
        lf = jnp.minimum(z, 0.0) - jnp.log(1.0 + jnp.exp(-jnp.abs(z)))
        lf_ref[...] = lf
        c_ref[...] = _scan_rows(lf) + carry_ref[0:1, :]
        carry_ref[0:1, :] = c_ref[tm - 1:tm, :]

    act = jax.ShapeDtypeStruct((s, 512), BF16)
    return pl.pallas_call(
        body, name="inproj_fwd", grid=(s // tm,),
        in_specs=[_row_spec(tm, D_MODEL), _full_spec((1, D_MODEL)), _full_spec((D_MODEL, IN_COLS_PAD)),
                  _full_spec((1, 128))],
        out_specs=[_row_spec(tm, 512)] * 6 + [_row_spec(tm, 128), _row_spec(tm, 128)],
        out_shape=[act] * 6 + [jax.ShapeDtypeStruct((s, 128), F32), jax.ShapeDtypeStruct((s, 128), F32)],
        scratch_shapes=[pltpu.VMEM((8, 128), F32)],
        compiler_params=_params(("arbitrary",)),
    )(x, g, w, fb)


def _window_sums(ext, forward):
    n = ext.shape[0]
    outs = []
    for gi, w in enumerate(POOL_WINDOWS):
        a = ext[:, gi * GROUP_DIM:(gi + 1) * GROUP_DIM]
        k = 1
        while k < w:
            a = a + pltpu.roll(a, (n - k) if forward else k, 0)
            k *= 2
        outs.append(a)
    return outs


def _window_counts(row0, tm):
    t = row0 + lax.broadcasted_iota(jnp.int32, (tm, GROUP_DIM), 0)
    return [jnp.minimum(t + 1, w).astype(F32) for w in POOL_WINDOWS]


def _pool_delta(u, halo, row0):
    tm = u.shape[0]
    sums = _window_sums(jnp.concatenate([halo, u], axis=0), forward=False)
    cnt = _window_counts(row0, tm)
    return [sums[gi][HALO:, :] / cnt[gi] - u[:, gi * GROUP_DIM:(gi + 1) * GROUP_DIM]
            for gi in range(len(POOL_WINDOWS))]


def _pool_fwd(pu, pg, pw, ps, tm=512):
    s = pu.shape[0]
    hb = tm // HALO

    def body(pu_ref, halo_ref, pg_ref, pw_ref, ps_ref, po_ref):
        i = pl.program_id(0)
        u = pu_ref[...].astype(F32)
        halo = jnp.where(i == 0, 0.0, halo_ref[...].astype(F32))
        d = _pool_delta(u, halo, i * tm)
        for gi in range(len(POOL_WINDOWS)):
            cols = slice(gi * GROUP_DIM, (gi + 1) * GROUP_DIM)
            z = jnp.dot(d[gi].astype(BF16), pw_ref[gi], preferred_element_type=F32)
            gate = pg_ref[:, cols].astype(F32)
            po_ref[:, cols] = (z * ps_ref[:, cols] * (gate * _sigmoid(gate))).astype(BF16)

    return pl.pallas_call(
        body, name="pool_fwd", grid=(s // tm,),
        in_specs=[_row_spec(tm, 512),
                  pl.BlockSpec((HALO, 512), lambda i: (jnp.maximum(i * hb - 1, 0), 0)),
                  _row_spec(tm, 512), _full_spec((4, GROUP_DIM, GROUP_DIM)), _full_spec((1, 512))],
        out_specs=_row_spec(tm, 512),
        out_shape=jax.ShapeDtypeStruct((s, 512), BF16),
        compiler_params=_params(("parallel",)),
    )(pu, pu, pg, pw, ps)


def _split3(x):
    hi = x.astype(BF16).astype(F32)
    r1 = x - hi
    lo = r1.astype(BF16).astype(F32)
    return hi, lo, (r1 - lo).astype(BF16).astype(F32)


def _own(idx, h):
    return (idx < HEAD_DIM) if h == 0 else (idx >= HEAD_DIM)


def _spare(h):
    return HEAD_DIM if h == 0 else 0


def _attn_fwd(q, k, v, c, ag, tq=256, tk=512, chunk=512):
    s = q.shape[0]

    def body(q_ref, k_ref, v_ref, c_ref, ag_ref, ma_ref, o_ref, lse_ref, kx_ref, vt_ref, acc0_ref, acc1_ref,
             sta_ref, stb_ref):
        j = pl.program_id(0)
        i = pl.program_id(1)

        @pl.when(i == 0)
        def _():
            def prep(ch, carry):
                rows = pl.ds(pl.multiple_of(ch * chunk, chunk), chunk)
                kk = k_ref[rows, :].astype(F32)
                cc = c_ref[rows, :]
                vt = jnp.transpose(v_ref[rows, :].astype(F32))
                lane = lax.broadcasted_iota(jnp.int32, (chunk, PAIR), 1)
                sub = lax.broadcasted_iota(jnp.int32, (PAIR, chunk), 0)
                for h in range(2):
                    ccol = jnp.sum(jnp.where(lane == 2 * j + h, cc, 0.0), axis=1, keepdims=True)
                    kx = jnp.where(_own(lane, h), kk, 0.0)
                    for t, part in enumerate(_split3(-ccol)):
                        kx = jnp.where(lane == _spare(h) + t, part, kx)
                    kx_ref[h, rows, :] = kx.astype(BF16)
                    vt_ref[h, :, rows] = jnp.where(_own(sub, h), vt,
                                                   jnp.where(sub == _spare(h), 1.0, 0.0)).astype(BF16)
                return carry

            lax.fori_loop(0, s // chunk, prep, 0)

        qq = q_ref[...].astype(F32)
        lane_q = lax.broadcasted_iota(jnp.int32, (tq, PAIR), 1)
        qx = []
        for h in range(2):
            ones = (lane_q >= _spare(h)) & (lane_q < _spare(h) + 3)
            qx.append(jnp.transpose(jnp.where(_own(lane_q, h), qq, jnp.where(ones, 1.0, 0.0))).astype(BF16))
        accs = (acc0_ref, acc1_ref)
        for acc in accs:
            acc[...] = jnp.zeros_like(acc)

        def rows_of(kb):
            return pl.ds(pl.multiple_of(kb * tk, tk), tk)

        def scores(kb, dst):
            for h in range(2):
                dst[h] = jnp.dot(kx_ref[h, rows_of(kb), :], qx[h], preferred_element_type=F32)

        def consume(kb, src, m, masked):
            m_out, alpha, pv = [], [], []
            for h in range(2):
                sh = src[h]
                if masked:
                    kpos = kb * tk + lax.broadcasted_iota(jnp.int32, (tk, tq), 0)
                    qpos = i * tq + lax.broadcasted_iota(jnp.int32, (tk, tq), 1)
                    sh = jnp.where(qpos >= kpos, sh, NEG)
                m_new = jnp.maximum(m[h], jnp.max(sh, axis=0, keepdims=True))
                pt = jnp.exp(sh - m_new).astype(BF16)
                alpha.append(jnp.exp(m[h] - m_new))
                pv.append(jnp.dot(vt_ref[h, :, rows_of(kb)], pt, preferred_element_type=F32))
                m_out.append(m_new)
            for h in range(2):
                accs[h][...] = accs[h][...] * alpha[h] + pv[h]
            return tuple(m_out)

        n_full = (i * tq) // tk

        def two_blocks(t, m):
            kb = 2 * t
            scores(kb + 1, stb_ref)
            m = consume(kb, sta_ref, m, False)
            scores(kb + 2, sta_ref)
            return consume(kb + 1, stb_ref, m, False)

        def odd_tail(m):
            scores(n_full, stb_ref)
            m = consume(n_full - 1, sta_ref, m, False)
            return consume(n_full, stb_ref, m, True)

        def even_tail(m):
            return consume(n_full, sta_ref, m, True)

        scores(0, sta_ref)
        m0 = jnp.full((1, tq), NEG, F32)
        m = lax.fori_loop(0, n_full // 2, two_blocks, (m0, m0))
        m = lax.cond(lax.rem(n_full, 2) == 1, odd_tail, even_tail, m)

        sub_o = lax.broadcasted_iota(jnp.int32, (PAIR, tq), 0)
        den = [accs[h][_spare(h):_spare(h) + 1, :] for h in range(2)]
        ot = jnp.where(sub_o < HEAD_DIM, acc0_ref[...] / den[0], acc1_ref[...] / den[1])
        o = jnp.transpose(ot)
        o_ref[...] = o.astype(BF16)
        gate = ag_ref[...].astype(F32)
        ma_ref[...] = (o * (gate * _sigmoid(gate))).astype(BF16)
        sub8 = lax.broadcasted_iota(jnp.int32, (8, tq), 0)
        lse_ref[...] = jnp.where(sub8 == 0, m[0] + jnp.log(den[0]), m[1] + jnp.log(den[1]))

    return pl.pallas_call(
        body, name="attn_fwd", grid=(N_PAIRS, s // tq),
        in_specs=[pl.BlockSpec((tq, PAIR), lambda j, i: (i, j)),
                  pl.BlockSpec((s, PAIR), lambda j, i: (0, j)),
                  pl.BlockSpec((s, PAIR), lambda j, i: (0, j)),
                  pl.BlockSpec((s, 128), lambda j, i: (0, 0)),
                  pl.BlockSpec((tq, PAIR), lambda j, i: (i, j))],
        out_specs=[pl.BlockSpec((tq, PAIR), lambda j, i: (i, j)),
                   pl.BlockSpec((tq, PAIR), lambda j, i: (i, j)),
                   pl.BlockSpec((None, 8, tq), lambda j, i: (j, 0, i))],
        out_shape=[jax.ShapeDtypeStruct((s, ATTN_WIDTH), BF16), jax.ShapeDtypeStruct((s, ATTN_WIDTH), BF16),
                   jax.ShapeDtypeStruct((N_PAIRS, 8, s), F32)],
        scratch_shapes=[pltpu.VMEM((2, s, PAIR), BF16), pltpu.VMEM((2, PAIR, s), BF16),
                        pltpu.VMEM((PAIR, tq), F32), pltpu.VMEM((PAIR, tq), F32),
                        pltpu.VMEM((2, tk, tq), F32), pltpu.VMEM((2, tk, tq), F32)],
        compiler_params=_params(("arbitrary", "arbitrary")),
    )(q, k, v, c, ag)


def _outproj_fwd(x, po, ma, wo, tm=512):
    s = x.shape[0]

    def body(x_ref, po_ref, ma_ref, wo_ref, xn_ref):
        xn_ref[...] = (x_ref[...]
                       + jnp.dot(po_ref[...], wo_ref[0:POOL_WIDTH, :], preferred_element_type=F32)
                       + jnp.dot(ma_ref[...], wo_ref[POOL_WIDTH:, :], preferred_element_type=F32))

    return pl.pallas_call(
        body, name="outproj_fwd", grid=(s // tm,),
        in_specs=[_row_spec(tm, D_MODEL), _row_spec(tm, 512), _row_spec(tm, 512),
                  _full_spec((D_MODEL, D_MODEL))],
        out_specs=_row_spec(tm, D_MODEL),
        out_shape=jax.ShapeDtypeStruct((s, D_MODEL), F32),
        compiler_params=_params(("parallel",)),
    )(x, po, ma, wo)


def _loss_head(x, tgt, g, tm=512):
    s = x.shape[0]

    def body(x_ref, t_ref, g_ref, dx_ref, dg_ref, loss_ref):
        @pl.when(pl.program_id(0) == 0)
        def _():
            dg_ref[...] = jnp.zeros_like(dg_ref)
            loss_ref[...] = jnp.zeros_like(loss_ref)

        xf = x_ref[...]
        r = lax.rsqrt(jnp.mean(xf * xf, axis=-1, keepdims=True) + RMS_EPS)
        xh = xf * r
        gg = g_ref[...]
        e = xh * gg - t_ref[...]
        loss_ref[...] += 0.5 * jnp.sum(jnp.mean(e * e, axis=-1, keepdims=True))
        dy = e * (1.0 / D_MODEL)
        u = dy * gg
        dx_ref[...] = r * (u - xh * jnp.mean(xh * u, axis=-1, keepdims=True))
        dg_ref[...] += jnp.sum(dy * xh, axis=0, keepdims=True)

    return pl.pallas_call(
        body, name="loss_head", grid=(s // tm,),
        in_specs=[_row_spec(tm, D_MODEL), _row_spec(tm, D_MODEL), _full_spec((1, D_MODEL))],
        out_specs=[_row_spec(tm, D_MODEL), _full_spec((1, D_MODEL)), _full_spec((8, 128))],
        out_shape=[jax.ShapeDtypeStruct((s, D_MODEL), F32), jax.ShapeDtypeStruct((1, D_MODEL), F32),
                   jax.ShapeDtypeStruct((8, 128), F32)],
        compiler_params=_params(("arbitrary",)),
    )(x, tgt, g)


def _outproj_bwd(dx, po, ma, ag, o, wo, head_sel, tm=512):
    s = dx.shape[0]

    def body(dx_ref, po_ref, ma_ref, ag_ref, o_ref, wo_ref, sel_ref, dmp_ref, do_ref, dag_ref, dt_ref, gwo_ref):
        @pl.when(pl.program_id(0) == 0)
        def _():
            gwo_ref[...] = jnp.zeros_like(gwo_ref)

        dxb = dx_ref[...].astype(BF16)
        dm = lax.dot_general(dxb, wo_ref[...], NT, preferred_element_type=F32)
        dmp_ref[...] = dm[:, 0:POOL_WIDTH].astype(BF16)
        dma = dm[:, POOL_WIDTH:]
        gate = ag_ref[...].astype(F32)
        of = o_ref[...].astype(F32)
        sg = _sigmoid(gate)
        do = dma * (gate * sg)
        do_ref[...] = do.astype(BF16)
        dag_ref[...] = (dma * of * (sg * (1.0 + gate * (1.0 - sg)))).astype(BF16)
        prod = do * of
        hi = prod.astype(BF16)
        lo = (prod - hi.astype(F32)).astype(BF16)
        dt_ref[...] = (lax.dot_general(sel_ref[...], hi, NT, preferred_element_type=F32)
                       + lax.dot_general(sel_ref[...], lo, NT, preferred_element_type=F32))
        gwo_ref[0:POOL_WIDTH, :] += lax.dot_general(po_ref[...], dxb, TN, preferred_element_type=F32)
        gwo_ref[POOL_WIDTH:, :] += lax.dot_general(ma_ref[...], dxb, TN, preferred_element_type=F32)

    act = jax.ShapeDtypeStruct((s, 512), BF16)
    return pl.pallas_call(
        body, name="outproj_bwd", grid=(s // tm,),
        in_specs=[_row_spec(tm, D_MODEL)] + [_row_spec(tm, 512)] * 4
                 + [_full_spec((D_MODEL, D_MODEL)), _full_spec((8, ATTN_WIDTH))],
        out_specs=[_row_spec(tm, 512)] * 3 + [pl.BlockSpec((8, tm), lambda i: (0, i)),
                                              _full_spec((D_MODEL, D_MODEL))],
        out_shape=[act, act, act, jax.ShapeDtypeStruct((8, s), F32),
                   jax.ShapeDtypeStruct((D_MODEL, D_MODEL), F32)],
        compiler_params=_params(("arbitrary",)),
    )(dx, po, ma, ag, o, wo, head_sel)


def _attn_bwd(q, k, v, do, c, lse, dt, tq=256, chunk=512):
    s = q.shape[0]
    tk = 2 * tq
    nq = s // tq

    def body(q_ref, k_ref, v_ref, do_ref, c_ref, lse_ref, dt_ref, dq_ref, dk_ref, dv_ref, dcs_ref, drs_ref,
             qxt_ref, doxt_ref, qm_ref, dox_ref, dqt0_ref, dqt1_ref, dk0_ref, dk1_ref, dvx_ref,
             kx_ref, vx_ref, ktm_ref, sa_ref, sb_ref):
        j = pl.program_id(0)
        b = pl.program_id(1)
        dqts = (dqt0_ref, dqt1_ref)
        dks = (dk0_ref, dk1_ref)

        @pl.when(b == 0)
        def _():
            def prep(ch, carry):
                rows = pl.ds(pl.multiple_of(ch * chunk, chunk), chunk)
                qq = q_ref[rows, :].astype(F32)
                dd = do_ref[rows, :].astype(F32)
                qt = jnp.transpose(qq)
                ddt = jnp.transpose(dd)
                lane = lax.broadcasted_iota(jnp.int32, (chunk, PAIR), 1)
                sub = lax.broadcasted_iota(jnp.int32, (PAIR, chunk), 0)
                for h in range(2):
                    sp = _spare(h)
                    qm_ref[h, rows, :] = jnp.where(_own(lane, h), qq, jnp.where(lane == sp, 1.0, 0.0)).astype(BF16)
                    dox_ref[h, rows, :] = jnp.where(_own(lane, h), dd, 0.0).astype(BF16)
                    qx = jnp.where(_own(sub, h), qt, jnp.where((sub >= sp) & (sub < sp + 3), 1.0, 0.0))
                    for t, part in enumerate(_split3(-lse_ref[pl.ds(h, 1), rows])):
                        qx = jnp.where(sub == sp + 3 + t, part, qx)
                    qxt_ref[h, :, rows] = qx.astype(BF16)
                    dx = jnp.where(_own(sub, h), ddt, 0.0)
                    for t, part in enumerate(_split3(-dt_ref[pl.ds(h, 1), rows])):
                        dx = jnp.where(sub == sp + t, part, dx)
                    doxt_ref[h, :, rows] = dx.astype(BF16)
                return carry

            lax.fori_loop(0, s // chunk, prep, 0)
            for ref in dqts:
                ref[...] = jnp.zeros_like(ref)

        kk = k_ref[...].astype(F32)
        vv = v_ref[...].astype(F32)
        cc = c_ref[...]
        kt = jnp.transpose(kk)
        lane = lax.broadcasted_iota(jnp.int32, (tk, PAIR), 1)
        sub = lax.broadcasted_iota(jnp.int32, (PAIR, tk), 0)
        for h in range(2):
            sp = _spare(h)
            ccol = jnp.sum(jnp.where(lane == 2 * j + h, cc, 0.0), axis=1, keepdims=True)
            kx = jnp.where(_own(lane, h), kk, jnp.where((lane >= sp + 3) & (lane < sp + 6), 1.0, 0.0))
            for t, part in enumerate(_split3(-ccol)):
                kx = jnp.where(lane == sp + t, part, kx)
            kx_ref[h] = kx.astype(BF16)
            vx_ref[h] = jnp.where(_own(lane, h), vv, jnp.where((lane >= sp) & (lane < sp + 3), 1.0, 0.0)).astype(BF16)
            ktm_ref[h] = jnp.where(_own(sub, h), kt, jnp.where(sub == sp, 1.0, 0.0)).astype(BF16)
        for ref in dks:
            ref[...] = jnp.zeros_like(ref)
        dvx_ref[...] = jnp.zeros_like(dvx_ref)

        def cols_of(i):
            return pl.ds(pl.multiple_of(i * tq, tq), tq)

        def scores(i, dst):
            cols = cols_of(jnp.minimum(i, nq - 1))
            for h in range(2):
                dst[h] = jnp.dot(kx_ref[h], qxt_ref[h, :, cols], preferred_element_type=F32)
                dst[2 + h] = jnp.dot(vx_ref[h], doxt_ref[h, :, cols], preferred_element_type=F32)

        def consume(i, src, masked):
            cols = cols_of(i)
            for h in range(2):
                arg = src[h]
                if masked:
                    kpos = b * tk + lax.broadcasted_iota(jnp.int32, (tk, tq), 0)
                    qpos = i * tq + lax.broadcasted_iota(jnp.int32, (tk, tq), 1)
                    arg = jnp.where(qpos >= kpos, arg, NEG)
                pt = jnp.exp(arg)
                dst = (pt * src[2 + h]).astype(BF16)
                dvx_ref[...] += jnp.dot(pt.astype(BF16), dox_ref[h, cols, :], preferred_element_type=F32)
                dks[h][...] += jnp.dot(dst, qm_ref[h, cols, :], preferred_element_type=F32)
                dqts[h][:, cols] += jnp.dot(ktm_ref[h], dst, preferred_element_type=F32)

        i0 = 2 * b

        def two_blocks(t, masked):
            i = i0 + 2 * t
            scores(i + 1, sb_ref)
            consume(i, sa_ref, masked)
            scores(i + 2, sa_ref)
            consume(i + 1, sb_ref, masked)

        def loop_body(t, carry):
            two_blocks(t, False)
            return carry

        scores(i0, sa_ref)
        two_blocks(0, True)
        lax.fori_loop(1, (nq - i0) // 2, loop_body, 0)

        dv_ref[...] = dvx_ref[...].astype(BF16)
        dk_ref[...] = jnp.where(lane < HEAD_DIM, dk0_ref[...], dk1_ref[...]).astype(BF16)
        dcs_ref[...] = jnp.where(lane == _spare(0), dk0_ref[...], jnp.where(lane == _spare(1), dk1_ref[...], 0.0))

        @pl.when(b == s // tk - 1)
        def _():
            sub8 = lax.broadcasted_iota(jnp.int32, (8, s), 0)
            drs_ref[...] = jnp.where(sub8 == 0, dqt0_ref[_spare(0):_spare(0) + 1, :],
                                     jnp.where(sub8 == 1, dqt1_ref[_spare(1):_spare(1) + 1, :], 0.0))
            sub_c = lax.broadcasted_iota(jnp.int32, (PAIR, chunk), 0)
            for ch in range(s // chunk):
                rows = pl.ds(ch * chunk, chunk)
                both = jnp.where(sub_c < HEAD_DIM, dqt0_ref[:, rows], dqt1_ref[:, rows])
                dq_ref[rows, :] = (jnp.transpose(both) * Q_SCALE).astype(BF16)

    act = jax.ShapeDtypeStruct((s, ATTN_WIDTH), BF16)
    pair_rows = pl.BlockSpec((s, PAIR), lambda j, b: (0, j))
    pair_blk = pl.BlockSpec((tk, PAIR), lambda j, b: (b, j))
    stat = pl.BlockSpec((None, 2, s), lambda j, b: (j, 0, 0))
    return pl.pallas_call(
        body, name="attn_bwd", grid=(N_PAIRS, s // tk),
        in_specs=[pair_rows, pair_blk, pair_blk, pair_rows,
                  pl.BlockSpec((tk, 128), lambda j, b: (b, 0)),
                  pl.BlockSpec((None, 8, s), lambda j, b: (j, 0, 0)), stat],
        out_specs=[pair_rows, pair_blk, pair_blk, pair_blk, pl.BlockSpec((None, 8, s), lambda j, b: (j, 0, 0))],
        out_shape=[act, act, act, jax.ShapeDtypeStruct((s, N_PAIRS * 128), F32),
                   jax.ShapeDtypeStruct((N_PAIRS, 8, s), F32)],
        scratch_shapes=[pltpu.VMEM((2, PAIR, s), BF16), pltpu.VMEM((2, PAIR, s), BF16),
                        pltpu.VMEM((2, s, PAIR), BF16), pltpu.VMEM((2, s, PAIR), BF16),
                        pltpu.VMEM((PAIR, s), F32), pltpu.VMEM((PAIR, s), F32),
                        pltpu.VMEM((tk, PAIR), F32), pltpu.VMEM((tk, PAIR), F32), pltpu.VMEM((tk, PAIR), F32),
                        pltpu.VMEM((2, tk, PAIR), BF16), pltpu.VMEM((2, tk, PAIR), BF16),
                        pltpu.VMEM((2, PAIR, tk), BF16),
                        pltpu.VMEM((4, tk, tq), F32), pltpu.VMEM((4, tk, tq), F32)],
        compiler_params=_params(("arbitrary", "arbitrary")),
    )(q, k, v, do, c, lse, dt)


def _forget_bwd(dcs, drs, lf, tm=256):
    s = lf.shape[0]
    n = s // tm

    def body(dcs_ref, drs_ref, lf_ref, dpf_ref, dfb_ref, carry_ref):
        @pl.when(pl.program_id(0) == 0)
        def _():
            carry_ref[...] = jnp.zeros_like(carry_ref)
            dfb_ref[...] = jnp.zeros_like(dfb_ref)

        lane = lax.broadcasted_iota(jnp.int32, (tm, 128), 1)
        dc = drs_ref[...]
        for pj in range(N_PAIRS):
            blk = dcs_ref[:, pj * 128:(pj + 1) * 128]
            for h in range(2):
                head = 2 * pj + h
                moved = pltpu.roll(blk, (head - _spare(h)) % 128, 1) if head != _spare(h) else blk
                dc = dc - jnp.where(lane == head, moved, 0.0)
        dlf = _scan_rows(dc, reverse=True) + carry_ref[0:1, :]
        carry_ref[...] = dlf[0:8, :]
        dz = jnp.where(lane < 8, dlf * (1.0 - jnp.exp(lf_ref[...])), 0.0)
        dpf_ref[...] = dz.astype(BF16)
        dfb_ref[...] += jnp.sum(dz, axis=0, keepdims=True)

    return pl.pallas_call(
        body, name="forget_bwd", grid=(n,),
        in_specs=[pl.BlockSpec((tm, N_PAIRS * 128), lambda i: (n - 1 - i, 0)),
                  pl.BlockSpec((tm, 128), lambda i: (n - 1 - i, 0)),
                  pl.BlockSpec((tm, 128), lambda i: (n - 1 - i, 0))],
        out_specs=[pl.BlockSpec((tm, 128), lambda i: (n - 1 - i, 0)), _full_spec((1, 128))],
        out_shape=[jax.ShapeDtypeStruct((s, 128), BF16), jax.ShapeDtypeStruct((1, 128), F32)],
        scratch_shapes=[pltpu.VMEM((8, 128), F32)],
        compiler_params=_params(("arbitrary",)),
    )(dcs, drs, lf)


def _pool_bwd(pu, pg, dmp, pw, ps, tm=512):
    s = pu.shape[0]
    hb = tm // HALO
    n = s // tm
    last_halo = s // HALO - 1

    def body(pu_ref, halo_ref, pg_ref, dmp_ref, pgn_ref, dmpn_ref, pw_ref, ps_ref,
             dpu_ref, dpg_ref, gpw_ref, gps_ref):
        i = pl.program_id(0)

        @pl.when(i == 0)
        def _():
            gpw_ref[...] = jnp.zeros_like(gpw_ref)
            gps_ref[...] = jnp.zeros_like(gps_ref)

        u = pu_ref[...].astype(F32)
        halo = jnp.where(i == 0, 0.0, halo_ref[...].astype(F32))
        d = _pool_delta(u, halo, i * tm)
        cnt = _window_counts(i * tm, tm)
        cnt_next = _window_counts((i + 1) * tm, HALO)
        e_parts, dd_parts = [], []
        for gi in range(len(POOL_WINDOWS)):
            cols = slice(gi * GROUP_DIM, (gi + 1) * GROUP_DIM)
            wg = pw_ref[gi]
            scale = ps_ref[:, cols]
            db = d[gi].astype(BF16)
            z = jnp.dot(db, wg, preferred_element_type=F32)
            gate = pg_ref[:, cols].astype(F32)
            sg = _sigmoid(gate)
            dm = dmp_ref[:, cols].astype(F32)
            dy = dm * (gate * sg)
            dpg_ref[:, cols] = (dm * (z * scale) * (sg * (1.0 + gate * (1.0 - sg)))).astype(BF16)
            gps_ref[:, cols] += jnp.sum(dy * z, axis=0, keepdims=True)
            dz = (dy * scale).astype(BF16)
            gpw_ref[gi] += lax.dot_general(db, dz, TN, preferred_element_type=F32)
            dd = lax.dot_general(dz, wg, NT, preferred_element_type=F32)
            gate_n = pgn_ref[:, cols].astype(F32)
            dz_n = (dmpn_ref[:, cols].astype(F32) * (gate_n * _sigmoid(gate_n)) * scale).astype(BF16)
            dd_n = lax.dot_general(dz_n, wg, NT, preferred_element_type=F32)
            dd_n = jnp.where(i == n - 1, 0.0, dd_n)
            dd_parts.append(dd)
            e_parts.append(jnp.concatenate([dd / cnt[gi], dd_n / cnt_next[gi]], axis=0))
        lead = _window_sums(jnp.concatenate(e_parts, axis=1), forward=True)
        for gi in range(len(POOL_WINDOWS)):
            cols = slice(gi * GROUP_DIM, (gi + 1) * GROUP_DIM)
            dpu_ref[:, cols] = (lead[gi][0:tm, :] - dd_parts[gi]).astype(BF16)

    act = jax.ShapeDtypeStruct((s, 512), BF16)
    prev_halo = pl.BlockSpec((HALO, 512), lambda i: (jnp.maximum(i * hb - 1, 0), 0))
    next_halo = pl.BlockSpec((HALO, 512), lambda i: (jnp.minimum((i + 1) * hb, last_halo), 0))
    return pl.pallas_call(
        body, name="pool_bwd", grid=(n,),
        in_specs=[_row_spec(tm, 512), prev_halo, _row_spec(tm, 512), _row_spec(tm, 512), next_halo, next_halo,
                  _full_spec((4, GROUP_DIM, GROUP_DIM)), _full_spec((1, 512))],
        out_specs=[_row_spec(tm, 512), _row_spec(tm, 512), _full_spec((4, GROUP_DIM, GROUP_DIM)),
                   _full_spec((1, 512))],
        out_shape=[act, act, jax.ShapeDtypeStruct((4, GROUP_DIM, GROUP_DIM), F32),
                   jax.ShapeDtypeStruct((1, 512), F32)],
        compiler_params=_params(("arbitrary",)),
    )(pu, pu, pg, dmp, pg, dmp, pw, ps)


def _inproj_bwd_x(dsec, w, x, g, dxo, tm=512):
    s = x.shape[0]

    def body(*refs):
        d_refs = refs[0:7]
        w_ref, x_ref, g_ref, dxo_ref, dxi_ref, h_ref, dg_ref = refs[7:]

        @pl.when(pl.program_id(0) == 0)
        def _():
            dg_ref[...] = jnp.zeros_like(dg_ref)

        dh = None
        for d_ref, (off, ncol) in zip(d_refs, SECTIONS):
            t = lax.dot_general(d_ref[...], w_ref[:, off:off + ncol], NT, preferred_element_type=F32)
            dh = t if dh is None else dh + t
        xf = x_ref[...]
        r = lax.rsqrt(jnp.mean(xf * xf, axis=-1, keepdims=True) + RMS_EPS)
        xh = xf * r
        gg = g_ref[...]
        h_ref[...] = (xh * gg).astype(BF16)
        u = dh * gg
        dxi_ref[...] = dxo_ref[...] + r * (u - xh * jnp.mean(xh * u, axis=-1, keepdims=True))
        dg_ref[...] += jnp.sum(dh * xh, axis=0, keepdims=True)

    return pl.pallas_call(
        body, name="inproj_bwd_x", grid=(s // tm,),
        in_specs=[_row_spec(tm, ncol) for _, ncol in SECTIONS]
                 + [_full_spec((D_MODEL, IN_COLS_PAD)), _row_spec(tm, D_MODEL), _full_spec((1, D_MODEL)),
                    _row_spec(tm, D_MODEL)],
        out_specs=[_row_spec(tm, D_MODEL), _row_spec(tm, D_MODEL), _full_spec((1, D_MODEL))],
        out_shape=[jax.ShapeDtypeStruct((s, D_MODEL), F32), jax.ShapeDtypeStruct((s, D_MODEL), BF16),
                   jax.ShapeDtypeStruct((1, D_MODEL), F32)],
        compiler_params=_params(("arbitrary",)),
    )(*dsec, w, x, g, dxo)


def _inproj_bwd_w(h, dsec, tm=512):
    s = h.shape[0]

    def body(*refs):
        h_ref = refs[0]
        d_refs = refs[1:8]
        gw_ref = refs[8]

        @pl.when(pl.program_id(0) == 0)
        def _():
            gw_ref[...] = jnp.zeros_like(gw_ref)

        hh = h_ref[...]
        for d_ref, (off, ncol) in zip(d_refs, SECTIONS):
            gw_ref[:, off:off + ncol] += lax.dot_general(hh, d_ref[...], TN, preferred_element_type=F32)

    return pl.pallas_call(
        body, name="inproj_bwd_w", grid=(s // tm,),
        in_specs=[_row_spec(tm, D_MODEL)] + [_row_spec(tm, ncol) for _, ncol in SECTIONS],
        out_specs=_full_spec((D_MODEL, IN_COLS_PAD)),
        out_shape=jax.ShapeDtypeStruct((D_MODEL, IN_COLS_PAD), F32),
        compiler_params=_params(("arbitrary",), vmem_mb=56),
    )(h, *dsec)


def _elementwise(fn, name, n_out, arrays, tm):
    rows, cols = arrays[0].shape
    n_in = len(arrays)

    def body(*refs):
        outs = fn(*[r[...] for r in refs[:n_in]])
        for r, val in zip(refs[n_in:], outs):
            r[...] = val

    spec = _row_spec(tm, cols)
    shape = jax.ShapeDtypeStruct((rows, cols), F32)
    return pl.pallas_call(
        body, name=name, grid=(rows // tm,),
        in_specs=[spec] * n_in, out_specs=[spec] * n_out, out_shape=[shape] * n_out,
        compiler_params=_params(("parallel",)),
    )(*arrays)


def _add2(a, b, name, tm):
    return _elementwise(lambda p, q: (p + q,), name, 1, [a, b], tm)[0]


def _chip_sum(place, g, r, name, tm):
    n_layers, _, rows, cols = g.shape
    half = rows // 2
    nb = half // tm

    def body(place_ref, g_ref, r_ref, o_ref):
        o_ref[...] = (g_ref[...] + r_ref[...]).astype(BF16)

    blk = (None, None, tm, cols)
    return pl.pallas_call(
        body, name=name,
        grid_spec=pltpu.PrefetchScalarGridSpec(
            num_scalar_prefetch=1, grid=(n_layers, N_CHIPS, nb),
            in_specs=[pl.BlockSpec(blk, lambda l, k, i, p: (l, k, p[1] * nb + i, 0)),
                      pl.BlockSpec(blk, lambda l, k, i, p: (l, k, i, 0))],
            out_specs=pl.BlockSpec(blk, lambda l, k, i, p: (l, k, i, 0))),
        out_shape=jax.ShapeDtypeStruct((n_layers, N_CHIPS, half, cols), BF16),
        compiler_params=_params(("parallel", "parallel", "parallel")),
    )(place, g, r)


def _mesh_sum(place, g, r, got, name, tm):
    n_layers, _, rows, cols = g.shape
    half = rows // 2
    nb = half // tm

    def body(place_ref, g_ref, r_ref, got_ref, o_ref):
        own = g_ref[...] + r_ref[...]
        o_ref[...] = (own + got_ref[0].astype(F32)) + (got_ref[1].astype(F32) + got_ref[2].astype(F32))

    blk = (None, None, tm, cols)
    return pl.pallas_call(
        body, name=name,
        grid_spec=pltpu.PrefetchScalarGridSpec(
            num_scalar_prefetch=1, grid=(n_layers, nb),
            in_specs=[pl.BlockSpec(blk, lambda l, i, p: (l, p[0], p[1] * nb + i, 0)),
                      pl.BlockSpec(blk, lambda l, i, p: (l, p[0], i, 0)),
                      pl.BlockSpec((None, 3, tm, cols), lambda l, i, p: (l, 0, i, 0))],
            out_specs=pl.BlockSpec((None, tm, cols), lambda l, i, p: (l, p[1] * nb + i, 0))),
        out_shape=jax.ShapeDtypeStruct((n_layers, rows, cols), F32),
        compiler_params=_params(("parallel", "parallel")),
    )(place, g, r, got)


def _mesh_sum_small(place, ps, got, tm):
    nb = SMALL_HALF // tm

    def body(place_ref, ps_ref, got_ref, o_ref):
        o_ref[...] = (ps_ref[...] + got_ref[0]) + (got_ref[1] + got_ref[2])

    return pl.pallas_call(
        body, name="mesh_sum_small",
        grid_spec=pltpu.PrefetchScalarGridSpec(
            num_scalar_prefetch=1, grid=(nb,),
            in_specs=[pl.BlockSpec((tm, 128), lambda i, p: (p[1] * nb + i, 0)),
                      pl.BlockSpec((3, tm, 128), lambda i, p: (0, i, 0))],
            out_specs=pl.BlockSpec((tm, 128), lambda i, p: (p[1] * nb + i, 0))),
        out_shape=jax.ShapeDtypeStruct((SMALL_ROWS, 128), F32),
        compiler_params=_params(("parallel",)),
    )(place, ps, got)


def _adamw(g, w, m, v, name, tm):
    def fn(g, w, m, v):
        m = ADAM_B1 * m + (1.0 - ADAM_B1) * g
        v = ADAM_B2 * v + (1.0 - ADAM_B2) * (g * g)
        m_hat = m / (1.0 - ADAM_B1 ** ADAM_STEP)
        v_hat = v / (1.0 - ADAM_B2 ** ADAM_STEP)
        delta = -ADAM_LR * (m_hat / (jnp.sqrt(v_hat) + ADAM_EPS) + ADAM_WD * w)
        return g, delta, m, v

    return _elementwise(fn, name, 4, [g, w, m, v], tm)


def _position():
    x, y, c = lax.axis_index("x"), lax.axis_index("y"), lax.axis_index("c")
    other_chips = [(x, 1 - y), (1 - x, y), (1 - x, 1 - y)]
    return x, y, c, other_chips


def _remote(src, dst, sems, k, to):
    send_sems, recv_sems = sems
    return pltpu.make_async_remote_copy(src_ref=src, dst_ref=dst, send_sem=send_sems.at[k],
                                        recv_sem=recv_sems.at[k], device_id=to, device_id_type=MESH)


def _comm_call(body, name, n_in, out_shape, n_remote, aliases=None):
    return pl.pallas_call(
        body, name=name, in_specs=[ANY] * n_in, out_specs=[ANY] * len(out_shape), out_shape=out_shape,
        scratch_shapes=[pltpu.SemaphoreType.DMA((n_remote,)), pltpu.SemaphoreType.DMA((n_remote,))],
        input_output_aliases=aliases or {},
    )


def _halves(ref_rows, c):
    half = ref_rows // 2
    return pl.ds(half * c, half), pl.ds(half * (1 - c), half)


def _gather_weights(wi, wo):
    def body(wi_ref, wo_ref, gi_ref, go_ref, send_sems, recv_sems):
        x, y, c, chips = _position()
        sems = (send_sems, recv_sems)
        sent = []
        for k, (cx, cy) in enumerate(chips):
            for a, (src, dst) in enumerate(((wi_ref, gi_ref), (wo_ref, go_ref))):
                mine, _ = _halves(src.shape[1], c)
                sent.append(_remote(src.at[:, mine], dst.at[:, k, mine], sems, 2 * k + a, (cx, cy, c)))
        for cp in sent:
            cp.start()
        for k in range(3):
            for a, dst in enumerate((gi_ref, go_ref)):
                mine, _ = _halves(dst.shape[2], c)
                blk = dst.at[:, k, mine]
                _remote(blk, blk, sems, 2 * k + a, (x, y, c)).wait_recv()
                fwd = _remote(blk, blk, sems, 6 + 2 * k + a, (x, y, 1 - c))
                fwd.start()
                sent.append(fwd)
        for k in range(3):
            for a, dst in enumerate((gi_ref, go_ref)):
                _, other = _halves(dst.shape[2], c)
                blk = dst.at[:, k, other]
                _remote(blk, blk, sems, 6 + 2 * k + a, (x, y, c)).wait_recv()
        for cp in sent:
            cp.wait_send()

    out_shape = [jax.ShapeDtypeStruct((wi.shape[0], 3) + wi.shape[1:], wi.dtype),
                 jax.ShapeDtypeStruct((wo.shape[0], 3) + wo.shape[1:], wo.dtype)]
    return _comm_call(body, "gather_weights", 2, out_shape, 12)(wi, wo)


def _exchange_sibling(gi, go, sm):
    def body(gi_ref, go_ref, sm_ref, ri_ref, ro_ref, rs_ref, send_sems, recv_sems):
        x, y, c, _ = _position()
        sems = (send_sems, recv_sems)
        cps = [_remote(src.at[:, :, _halves(src.shape[2], c)[1]], dst, sems, a, (x, y, 1 - c))
               for a, (src, dst) in enumerate(((gi_ref, ri_ref), (go_ref, ro_ref)))]
        cps.append(_remote(sm_ref, rs_ref, sems, 2, (x, y, 1 - c)))
        for cp in cps:
            cp.start()
        for cp in cps:
            cp.wait()

    out_shape = [jax.ShapeDtypeStruct(gi.shape[:2] + (gi.shape[2] // 2, gi.shape[3]), F32),
                 jax.ShapeDtypeStruct(go.shape[:2] + (go.shape[2] // 2, go.shape[3]), F32),
                 jax.ShapeDtypeStruct(sm.shape, F32)]
    return _comm_call(body, "exchange_sibling", 3, out_shape, 3)(gi, go, sm)


def _scatter_chips(pi, po, ps):
    def body(pi_ref, po_ref, ps_ref, ri_ref, ro_ref, rs_ref, send_sems, recv_sems):
        x, y, c, chips = _position()
        sems = (send_sems, recv_sems)
        half, _ = _halves(SMALL_ROWS, c)
        sent = []
        for k, (cx, cy) in enumerate(chips):
            ck = 2 * cx + cy
            sent.append(_remote(pi_ref.at[:, ck], ri_ref.at[:, k], sems, 3 * k, (cx, cy, c)))
            sent.append(_remote(po_ref.at[:, ck], ro_ref.at[:, k], sems, 3 * k + 1, (cx, cy, c)))
            sent.append(_remote(ps_ref.at[half], rs_ref.at[k], sems, 3 * k + 2, (cx, cy, c)))
        for cp in sent:
            cp.start()
        for k in range(3):
            for a, blk in ((0, ri_ref.at[:, k]), (1, ro_ref.at[:, k]), (2, rs_ref.at[k])):
                _remote(blk, blk, sems, 3 * k + a, (x, y, c)).wait_recv()
        for cp in sent:
            cp.wait_send()

    out_shape = [jax.ShapeDtypeStruct((pi.shape[0], 3) + pi.shape[2:], pi.dtype),
                 jax.ShapeDtypeStruct((po.shape[0], 3) + po.shape[2:], po.dtype),
                 jax.ShapeDtypeStruct((3, SMALL_HALF, 128), F32)]
    return _comm_call(body, "scatter_chips", 3, out_shape, 9)(pi, po, ps)


def _join_sibling(fi, fo, fs):
    def body(fi_ref, fo_ref, fs_ref, oi_ref, oo_ref, os_ref, send_sems, recv_sems):
        x, y, c, _ = _position()
        sems = (send_sems, recv_sems)
        sent, got = [], []
        for a, (src, dst) in enumerate(((fi_ref, oi_ref), (fo_ref, oo_ref))):
            mine, other = _halves(src.shape[1], c)
            sent.append(_remote(src.at[:, mine], dst.at[:, mine], sems, a, (x, y, 1 - c)))
            got.append(_remote(dst.at[:, other], dst.at[:, other], sems, a, (x, y, c)))
        mine, other = _halves(SMALL_ROWS, c)
        sent.append(_remote(fs_ref.at[mine], os_ref.at[mine], sems, 2, (x, y, 1 - c)))
        got.append(_remote(os_ref.at[other], os_ref.at[other], sems, 2, (x, y, c)))
        for cp in sent:
            cp.start()
        for cp in got:
            cp.wait_recv()
        for cp in sent:
            cp.wait_send()

    out_shape = [jax.ShapeDtypeStruct(a.shape, F32) for a in (fi, fo, fs)]
    return _comm_call(body, "join_sibling", 3, out_shape, 3, aliases={0: 0, 1: 1, 2: 2})(fi, fo, fs)


def _pack_small(norm_g, final_g, pool_scale, forget_bias, pool_w):
    fb = jnp.pad(forget_bias.reshape(-1), (0, 128 - forget_bias.size))
    flat = jnp.concatenate([norm_g.reshape(-1), final_g.reshape(-1), pool_scale.reshape(-1), fb,
                            pool_w.reshape(-1)])
    return jnp.pad(flat, (0, SMALL_ROWS * 128 - flat.size)).reshape(SMALL_ROWS, 128)


def _unpack_small(packed):
    flat = packed.reshape(-1)
    sizes = (DEPTH * D_MODEL, D_MODEL, DEPTH * POOL_WIDTH, 128, DEPTH * 4 * GROUP_DIM * GROUP_DIM)
    offs = [0]
    for n in sizes:
        offs.append(offs[-1] + n)
    norm_g, final_g, pool_scale, fb, pool_w = [flat[offs[i]:offs[i + 1]] for i in range(5)]
    return (norm_g.reshape(DEPTH, D_MODEL), final_g, pool_scale.reshape(DEPTH, POOL_WIDTH),
            fb[:DEPTH * 8].reshape(DEPTH, 8), pool_w.reshape(DEPTH, 4, GROUP_DIM, GROUP_DIM))


def kernel(x, norm_g, w_in, forget_bias, pool_w, pool_scale, w_out, final_g, loss_target, m_norm_g, m_w_in, m_forget_bias, m_pool_w, m_pool_scale, m_w_out, m_final_g, v_norm_g, v_w_in, v_forget_bias, v_pool_w, v_pool_scale, v_w_out, v_final_g):
    x0 = x[0]
    tgt = loss_target[0]
    s = x0.shape[0]

    me = 2 * lax.axis_index("x") + lax.axis_index("y")
    place = jnp.stack([me, lax.axis_index("c")]).astype(jnp.int32)
    wi_b, wo_b = w_in.astype(BF16), w_out.astype(BF16)
    gi, go = _gather_weights(wi_b, wo_b)

    def chip_piece(own, got, chip):
        rel = jnp.bitwise_xor(chip, me)
        other = lax.dynamic_index_in_dim(got, jnp.maximum(rel - 1, 0), axis=1, keepdims=False)
        return jnp.where(rel == 0, own, other)

    w_in_full = jnp.concatenate([chip_piece(wi_b, gi, k) for k in range(N_CHIPS)], axis=-1)
    w_in_full = jnp.pad(w_in_full, ((0, 0), (0, 0), (0, IN_COLS_PAD - IN_COLS)))
    w_out_full = jnp.concatenate([chip_piece(wo_b, go, k) for k in range(N_CHIPS)], axis=1)
    pool_w_b = pool_w.astype(BF16)
    fb_pad = jnp.pad(forget_bias, ((0, 0), (0, 128 - forget_bias.shape[1])))
    head_sel = (jnp.arange(ATTN_WIDTH)[None, :] // HEAD_DIM == jnp.arange(8)[:, None]).astype(BF16)

    saved = []
    xl = x0
    for l in range(DEPTH):
        pu, pg, q, k, v, ag, lf, c = _inproj_fwd(xl, norm_g[l][None], w_in_full[l], fb_pad[l][None])
        po = _pool_fwd(pu, pg, pool_w_b[l], pool_scale[l][None])
        ma, o, lse = _attn_fwd(q, k, v, c, ag)
        saved.append((xl, pu, pg, q, k, v, ag, lf, c, po, ma, o, lse))
        xl = _outproj_fwd(xl, po, ma, w_out_full[l])

    dx, g_final, loss_part = _loss_head(xl, tgt, final_g[None])
    loss = lax.psum(loss_part[0, 0], ("x", "y", "c"))

    g_norm, g_win, g_fb, g_pw, g_ps, g_wo = [], [], [], [], [], []
    for l in reversed(range(DEPTH)):
        xl, pu, pg, q, k, v, ag, lf, c, po, ma, o, lse = saved[l]
        dmp, do, dag, dt, gwo = _outproj_bwd(dx, po, ma, ag, o, w_out_full[l], head_sel)
        dq, dk, dv, dcs, drs = _attn_bwd(q, k, v, do, c, lse, dt.reshape(N_PAIRS, 2, s))
        drs = jnp.pad(drs[:, 0:2, :].reshape(8, s).T, ((0, 0), (0, 120)))
        dpf, dfb = _forget_bwd(dcs, drs, lf)
        dpu, dpg, gpw, gps = _pool_bwd(pu, pg, dmp, pool_w_b[l], pool_scale[l][None])
        dsec = (dpu, dpg, dq, dk, dv, dag, dpf)
        dx, h, gn = _inproj_bwd_x(dsec, w_in_full[l], xl, norm_g[l][None], dx)
        g_win.append(_inproj_bwd_w(h, dsec))
        g_norm.append(gn[0])
        g_fb.append(dfb[0, :8])
        g_pw.append(gpw)
        g_ps.append(gps[0])
        g_wo.append(gwo)
    g_norm, g_win, g_fb, g_pw, g_ps, g_wo = [t[::-1] for t in (g_norm, g_win, g_fb, g_pw, g_ps, g_wo)]

    gwin = jnp.stack(g_win)
    send_i = jnp.stack([gwin[:, :, k * SHARD_COLS:(k + 1) * SHARD_COLS] for k in range(N_CHIPS)], axis=1)
    send_o = jnp.stack(g_wo).reshape(DEPTH, N_CHIPS, SHARD_ROWS, D_MODEL)
    send_s = _pack_small(jnp.stack(g_norm), g_final[0], jnp.stack(g_ps), jnp.stack(g_fb), jnp.stack(g_pw))
    ri, ro, rs = _exchange_sibling(send_i, send_o, send_s)
    pi = _chip_sum(place, send_i, ri, "chip_sum_w_in", 256)
    po_ = _chip_sum(place, send_o, ro, "chip_sum_w_out", SHARD_ROWS // 2)
    ps_ = _add2(send_s, rs, "chip_sum_small", SMALL_ROWS // 4)
    qi, qo, qs = _scatter_chips(pi, po_, ps_)
    hi = _mesh_sum(place, send_i, ri, qi, "mesh_sum_w_in", 256)
    ho = _mesh_sum(place, send_o, ro, qo, "mesh_sum_w_out", SHARD_ROWS // 2)
    hs = _mesh_sum_small(place, ps_, qs, SMALL_HALF // 4)
    fi, fo, fs = _join_sibling(hi, ho, hs)

    flat_i = lambda a: a.reshape(-1, SHARD_COLS)
    flat_o = lambda a: a.reshape(-1, D_MODEL)
    out_i = _adamw(flat_i(fi), flat_i(w_in), flat_i(m_w_in), flat_i(v_w_in), "adamw_w_in", 256)
    out_o = _adamw(flat_o(fo), flat_o(w_out), flat_o(m_w_out), flat_o(v_w_out), "adamw_w_out", 256)
    out_s = _adamw(fs, _pack_small(norm_g, final_g, pool_scale, forget_bias, pool_w),
                   _pack_small(m_norm_g, m_final_g, m_pool_scale, m_forget_bias, m_pool_w),
                   _pack_small(v_norm_g, v_final_g, v_pool_scale, v_forget_bias, v_pool_w),
                   "adamw_small", SMALL_ROWS // 4)

    groups = []
    for oi, oo, os_ in zip(out_i, out_o, out_s):
        sn, sf, sp, sb, sw = _unpack_small(os_)
        groups.append((sn, oi.reshape(w_in.shape), sb, sw, sp, oo.reshape(w_out.shape), sf))
    return (loss, dx[None]) + tuple(t for grp in groups for t in grp)
```

```python
import jax
import jax.numpy as jnp
from jax import lax
from jax.experimental import pallas as pl
from jax.experimental.pallas import tpu as pltpu

F32 = jnp.float32
BF16 = jnp.bfloat16

D_MODEL = 1024
DEPTH = 4
POOL_WIDTH = 512
ATTN_WIDTH = 512
HEAD_DIM = 64
PAIR = 2 * HEAD_DIM
N_PAIRS = ATTN_WIDTH // PAIR
POOL_WINDOWS = (2, 4, 8, 16)
GROUP_DIM = 128
HALO = 16
IN_COLS = 3080
OFF_F = 3072
IN_COLS_PAD = 3200
SECTIONS = ((0, 512), (512, 512), (1024, 512), (1536, 512), (2048, 512), (2560, 512), (OFF_F, 128))
N_CHIPS = 4
SHARD_COLS = IN_COLS // N_CHIPS
SHARD_ROWS = D_MODEL // N_CHIPS
RMS_EPS = 1e-6
NEG = -1e30
Q_SCALE = 0.125

ADAM_LR = 0.001
ADAM_B1 = 0.9
ADAM_B2 = 0.999
ADAM_EPS = 1e-08
ADAM_WD = 0.01
ADAM_STEP = 10

SMALL_ROWS = 2112
SMALL_HALF = SMALL_ROWS // 2

NT = (((1,), (1,)), ((), ()))
TN = (((0,), (0,)), ((), ()))
MESH = pl.DeviceIdType.MESH
ANY = pl.BlockSpec(memory_space=pl.ANY)


def _params(semantics, vmem_mb=48):
    return pltpu.CompilerParams(dimension_semantics=semantics, vmem_limit_bytes=vmem_mb << 20)


def _row_spec(tm, cols):
    return pl.BlockSpec((tm, cols), lambda i: (i, 0))


def _full_spec(shape):
    return pl.BlockSpec(shape, lambda *_: (0,) * len(shape))


def _sigmoid(x):
    return jax.nn.sigmoid(x)


def _scan_rows(a, reverse=False):
    n = a.shape[0]
    row = lax.broadcasted_iota(jnp.int32, a.shape, 0)
    k = 1
    while k < n:
        if reverse:
            a = a + jnp.where(row < n - k, pltpu.roll(a, n - k, 0), 0.0)
        else:
            a = a + jnp.where(row >= k, pltpu.roll(a, k, 0), 0.0)
        k *= 2
    return a


def _inproj_fwd(x, g, w, fb, tm=256):
    s = x.shape[0]

    def body(x_ref, g_ref, w_ref, fb_ref, pu_ref, pg_ref, q_ref, k_ref, v_ref, ag_ref,
             lf_ref, c_ref, carry_ref):
        @pl.when(pl.program_id(0) == 0)
        def _():
            carry_ref[...] = jnp.zeros_like(carry_ref)

        xf = x_ref[...]
        r = lax.rsqrt(jnp.mean(xf * xf, axis=-1, keepdims=True) + RMS_EPS)
        h = (xf * r * g_ref[...]).astype(BF16)

        def proj(sec):
            off, n = SECTIONS[sec]
            return jnp.dot(h, w_ref[:, off:off + n], preferred_element_type=F32)

        pu_ref[...] = proj(0).astype(BF16)
        pg_ref[...] = proj(1).astype(BF16)
        q_ref[...] = (proj(2) * Q_SCALE).astype(BF16)
        k_ref[...] = proj(3).astype(BF16)
        v_ref[...] = proj(4).astype(BF16)
        ag_ref[...] = proj(5).astype(BF16)
        z = proj(6) + fb_ref[...]
        lf = jnp.minimum(z, 0.0) - jnp.log(1.0 + jnp.exp(-jnp.abs(z)))
        lf_ref[...] = lf
        c_ref[...] = _scan_rows(lf) + carry_ref[0:1, :]
        carry_ref[0:1, :] = c_ref[tm - 1:tm, :]

    act = jax.ShapeDtypeStruct((s, 512), BF16)
    return pl.pallas_call(
        body, name="inproj_fwd", grid=(s // tm,),
        in_specs=[_row_spec(tm, D_MODEL), _full_spec((1, D_MODEL)), _full_spec((D_MODEL, IN_COLS_PAD)),
                  _full_spec((1, 128))],
        out_specs=[_row_spec(tm, 512)] * 6 + [_row_spec(tm, 128), _row_spec(tm, 128)],
        out_shape=[act] * 6 + [jax.ShapeDtypeStruct((s, 128), F32), jax.ShapeDtypeStruct((s, 128), F32)],
        scratch_shapes=[pltpu.VMEM((8, 128), F32)],
        compiler_params=_params(("arbitrary",)),
    )(x, g, w, fb)


def _window_sums(ext, forward):
    n = ext.shape[0]
    outs = []
    for gi, w in enumerate(POOL_WINDOWS):
        a = ext[:, gi * GROUP_DIM:(gi + 1) * GROUP_DIM]
        k = 1
        while k < w:
            a = a + pltpu.roll(a, (n - k) if forward else k, 0)
            k *= 2
        outs.append(a)
    return outs


def _window_counts(row0, tm):
    t = row0 + lax.broadcasted_iota(jnp.int32, (tm, GROUP_DIM), 0)
    return [jnp.minimum(t + 1, w).astype(F32) for w in POOL_WINDOWS]


def _pool_delta(u, halo, row0):
    tm = u.shape[0]
    sums = _window_sums(jnp.concatenate([halo, u], axis=0), forward=False)
    cnt = _window_counts(row0, tm)
    return [sums[gi][HALO:, :] / cnt[gi] - u[:, gi * GROUP_DIM:(gi + 1) * GROUP_DIM]
            for gi in range(len(POOL_WINDOWS))]


def _pool_fwd(pu, pg, pw, ps, tm=512):
    s = pu.shape[0]
    hb = tm // HALO

    def body(pu_ref, halo_ref, pg_ref, pw_ref, ps_ref, po_ref):
        i = pl.program_id(0)
        u = pu_ref[...].astype(F32)
        halo = jnp.where(i == 0, 0.0, halo_ref[...].astype(F32))
        d = _pool_delta(u, halo, i * tm)
        for gi in range(len(POOL_WINDOWS)):
            cols = slice(gi * GROUP_DIM, (gi + 1) * GROUP_DIM)
            z = jnp.dot(d[gi].astype(BF16), pw_ref[gi], preferred_element_type=F32)
            gate = pg_ref[:, cols].astype(F32)
            po_ref[:, cols] = (z * ps_ref[:, cols] * (gate * _sigmoid(gate))).astype(BF16)

    return pl.pallas_call(
        body, name="pool_fwd", grid=(s // tm,),
        in_specs=[_row_spec(tm, 512),
                  pl.BlockSpec((HALO, 512), lambda i: (jnp.maximum(i * hb - 1, 0), 0)),
                  _row_spec(tm, 512), _full_spec((4, GROUP_DIM, GROUP_DIM)), _full_spec((1, 512))],
        out_specs=_row_spec(tm, 512),
        out_shape=jax.ShapeDtypeStruct((s, 512), BF16),
        compiler_params=_params(("parallel",)),
    )(pu, pu, pg, pw, ps)


def _split3(x):
    hi = x.astype(BF16).astype(F32)
    r1 = x - hi
    lo = r1.astype(BF16).astype(F32)
    return hi, lo, (r1 - lo).astype(BF16).astype(F32)


def _own(idx, h):
    return (idx < HEAD_DIM) if h == 0 else (idx >= HEAD_DIM)


def _spare(h):
    return HEAD_DIM if h == 0 else 0


def _attn_fwd(q, k, v, c, ag, carry=None, tq=256, tk=512, chunk=512):
    s = q.shape[0]
    n_carry = 0 if carry is None else len(carry)

    def body(*refs):
        q_ref, k_ref, v_ref, c_ref, ag_ref = refs[:5]
        carry_src = refs[5:5 + n_carry]
        ma_ref, o_ref, lse_ref = refs[5 + n_carry:8 + n_carry]
        carry_dst = refs[8 + n_carry:8 + 2 * n_carry]
        kx_ref, vt_ref, acc0_ref, acc1_ref, sta_ref, stb_ref = refs[8 + 2 * n_carry:14 + 2 * n_carry]
        carry_sems = refs[14 + 2 * n_carry:]
        j = pl.program_id(0)
        i = pl.program_id(1)

        if n_carry:
            @pl.when((j == 0) & (i == 0))
            def _():
                _gather_start(carry_src, carry_dst, carry_sems)

        @pl.when(i == 0)
        def _():
            def prep(ch, carry):
                rows = pl.ds(pl.multiple_of(ch * chunk, chunk), chunk)
                kk = k_ref[rows, :].astype(F32)
                cc = c_ref[rows, :]
                vt = jnp.transpose(v_ref[rows, :].astype(F32))
                lane = lax.broadcasted_iota(jnp.int32, (chunk, PAIR), 1)
                sub = lax.broadcasted_iota(jnp.int32, (PAIR, chunk), 0)
                for h in range(2):
                    ccol = jnp.sum(jnp.where(lane == 2 * j + h, cc, 0.0), axis=1, keepdims=True)
                    kx = jnp.where(_own(lane, h), kk, 0.0)
                    for t, part in enumerate(_split3(-ccol)):
                        kx = jnp.where(lane == _spare(h) + t, part, kx)
                    kx_ref[h, rows, :] = kx.astype(BF16)
                    vt_ref[h, :, rows] = jnp.where(_own(sub, h), vt,
                                                   jnp.where(sub == _spare(h), 1.0, 0.0)).astype(BF16)
                return carry

            lax.fori_loop(0, s // chunk, prep, 0)

        qq = q_ref[...].astype(F32)
        lane_q = lax.broadcasted_iota(jnp.int32, (tq, PAIR), 1)
        qx = []
        for h in range(2):
            ones = (lane_q >= _spare(h)) & (lane_q < _spare(h) + 3)
            qx.append(jnp.transpose(jnp.where(_own(lane_q, h), qq, jnp.where(ones, 1.0, 0.0))).astype(BF16))
        accs = (acc0_ref, acc1_ref)
        for acc in accs:
            acc[...] = jnp.zeros_like(acc)

        def rows_of(kb):
            return pl.ds(pl.multiple_of(kb * tk, tk), tk)

        def scores(kb, dst):
            for h in range(2):
                dst[h] = jnp.dot(kx_ref[h, rows_of(kb), :], qx[h], preferred_element_type=F32)

        def consume(kb, src, m, masked):
            m_out, alpha, pv = [], [], []
            for h in range(2):
                sh = src[h]
                if masked:
                    kpos = kb * tk + lax.broadcasted_iota(jnp.int32, (tk, tq), 0)
                    qpos = i * tq + lax.broadcasted_iota(jnp.int32, (tk, tq), 1)
                    sh = jnp.where(qpos >= kpos, sh, NEG)
                m_new = jnp.maximum(m[h], jnp.max(sh, axis=0, keepdims=True))
                pt = jnp.exp(sh - m_new).astype(BF16)
                alpha.append(jnp.exp(m[h] - m_new))
                pv.append(jnp.dot(vt_ref[h, :, rows_of(kb)], pt, preferred_element_type=F32))
                m_out.append(m_new)
            for h in range(2):
                accs[h][...] = accs[h][...] * alpha[h] + pv[h]
            return tuple(m_out)

        n_full = (i * tq) // tk

        def two_blocks(t, m):
            kb = 2 * t
            scores(kb + 1, stb_ref)
            m = consume(kb, sta_ref, m, False)
            scores(kb + 2, sta_ref)
            return consume(kb + 1, stb_ref, m, False)

        def odd_tail(m):
            scores(n_full, stb_ref)
            m = consume(n_full - 1, sta_ref, m, False)
            return consume(n_full, stb_ref, m, True)

        def even_tail(m):
            return consume(n_full, sta_ref, m, True)

        scores(0, sta_ref)
        m0 = jnp.full((1, tq), NEG, F32)
        m = lax.fori_loop(0, n_full // 2, two_blocks, (m0, m0))
        m = lax.cond(lax.rem(n_full, 2) == 1, odd_tail, even_tail, m)

        sub_o = lax.broadcasted_iota(jnp.int32, (PAIR, tq), 0)
        den = [accs[h][_spare(h):_spare(h) + 1, :] for h in range(2)]
        ot = jnp.where(sub_o < HEAD_DIM, acc0_ref[...] / den[0], acc1_ref[...] / den[1])
        o = jnp.transpose(ot)
        o_ref[...] = o.astype(BF16)
        gate = ag_ref[...].astype(F32)
        ma_ref[...] = (o * (gate * _sigmoid(gate))).astype(BF16)
        sub8 = lax.broadcasted_iota(jnp.int32, (8, tq), 0)
        lse_ref[...] = jnp.where(sub8 == 0, m[0] + jnp.log(den[0]), m[1] + jnp.log(den[1]))

        if n_carry:
            @pl.when((j == N_PAIRS - 1) & (i == s // tq - 1))
            def _():
                _gather_finish(carry_src, carry_dst, carry_sems)

    carried = [] if carry is None else list(carry)
    carry_sem = [pltpu.SemaphoreType.DMA((GATHER_SEMS,))] * 2 if n_carry else []
    return pl.pallas_call(
        body, name="attn_fwd_gather" if n_carry else "attn_fwd", grid=(N_PAIRS, s // tq),
        in_specs=[pl.BlockSpec((tq, PAIR), lambda j, i: (i, j)),
                  pl.BlockSpec((s, PAIR), lambda j, i: (0, j)),
                  pl.BlockSpec((s, PAIR), lambda j, i: (0, j)),
                  pl.BlockSpec((s, 128), lambda j, i: (0, 0)),
                  pl.BlockSpec((tq, PAIR), lambda j, i: (i, j))] + [ANY] * n_carry,
        out_specs=[pl.BlockSpec((tq, PAIR), lambda j, i: (i, j)),
                   pl.BlockSpec((tq, PAIR), lambda j, i: (i, j)),
                   pl.BlockSpec((None, 8, tq), lambda j, i: (j, 0, i))] + [ANY] * n_carry,
        out_shape=[jax.ShapeDtypeStruct((s, ATTN_WIDTH), BF16), jax.ShapeDtypeStruct((s, ATTN_WIDTH), BF16),
                   jax.ShapeDtypeStruct((N_PAIRS, 8, s), F32)] + (_gather_shapes(*carried) if n_carry else []),
        scratch_shapes=[pltpu.VMEM((2, s, PAIR), BF16), pltpu.VMEM((2, PAIR, s), BF16),
                        pltpu.VMEM((PAIR, tq), F32), pltpu.VMEM((PAIR, tq), F32),
                        pltpu.VMEM((2, tk, tq), F32), pltpu.VMEM((2, tk, tq), F32)] + carry_sem,
        compiler_params=_params(("arbitrary", "arbitrary")),
    )(q, k, v, c, ag, *carried)


def _outproj_fwd(x, po, ma, wo, tm=512):
    s = x.shape[0]

    def body(x_ref, po_ref, ma_ref, wo_ref, xn_ref):
        xn_ref[...] = (x_ref[...]
                       + jnp.dot(po_ref[...], wo_ref[0:POOL_WIDTH, :], preferred_element_type=F32)
                       + jnp.dot(ma_ref[...], wo_ref[POOL_WIDTH:, :], preferred_element_type=F32))

    return pl.pallas_call(
        body, name="outproj_fwd", grid=(s // tm,),
        in_specs=[_row_spec(tm, D_MODEL), _row_spec(tm, 512), _row_spec(tm, 512),
                  _full_spec((D_MODEL, D_MODEL))],
        out_specs=_row_spec(tm, D_MODEL),
        out_shape=jax.ShapeDtypeStruct((s, D_MODEL), F32),
        compiler_params=_params(("parallel",)),
    )(x, po, ma, wo)


def _loss_head(x, tgt, g, tm=512):
    s = x.shape[0]

    def body(x_ref, t_ref, g_ref, dx_ref, dg_ref, loss_ref):
        @pl.when(pl.program_id(0) == 0)
        def _():
            dg_ref[...] = jnp.zeros_like(dg_ref)
            loss_ref[...] = jnp.zeros_like(loss_ref)

        xf = x_ref[...]
        r = lax.rsqrt(jnp.mean(xf * xf, axis=-1, keepdims=True) + RMS_EPS)
        xh = xf * r
        gg = g_ref[...]
        e = xh * gg - t_ref[...]
        loss_ref[...] += 0.5 * jnp.sum(jnp.mean(e * e, axis=-1, keepdims=True))
        dy = e * (1.0 / D_MODEL)
        u = dy * gg
        dx_ref[...] = r * (u - xh * jnp.mean(xh * u, axis=-1, keepdims=True))
        dg_ref[...] += jnp.sum(dy * xh, axis=0, keepdims=True)

    return pl.pallas_call(
        body, name="loss_head", grid=(s // tm,),
        in_specs=[_row_spec(tm, D_MODEL), _row_spec(tm, D_MODEL), _full_spec((1, D_MODEL))],
        out_specs=[_row_spec(tm, D_MODEL), _full_spec((1, D_MODEL)), _full_spec((8, 128))],
        out_shape=[jax.ShapeDtypeStruct((s, D_MODEL), F32), jax.ShapeDtypeStruct((1, D_MODEL), F32),
                   jax.ShapeDtypeStruct((8, 128), F32)],
        compiler_params=_params(("arbitrary",)),
    )(x, tgt, g)


def _outproj_bwd(dx, po, ma, ag, o, wo, head_sel, tm=512):
    s = dx.shape[0]

    def body(dx_ref, po_ref, ma_ref, ag_ref, o_ref, wo_ref, sel_ref, dmp_ref, do_ref, dag_ref, dt_ref, gwo_ref):
        @pl.when(pl.program_id(0) == 0)
        def _():
            gwo_ref[...] = jnp.zeros_like(gwo_ref)

        dxb = dx_ref[...].astype(BF16)
        dm = lax.dot_general(dxb, wo_ref[...], NT, preferred_element_type=F32)
        dmp_ref[...] = dm[:, 0:POOL_WIDTH].astype(BF16)
        dma = dm[:, POOL_WIDTH:]
        gate = ag_ref[...].astype(F32)
        of = o_ref[...].astype(F32)
        sg = _sigmoid(gate)
        do = dma * (gate * sg)
        do_ref[...] = do.astype(BF16)
        dag_ref[...] = (dma * of * (sg * (1.0 + gate * (1.0 - sg)))).astype(BF16)
        prod = do * of
        hi = prod.astype(BF16)
        lo = (prod - hi.astype(F32)).astype(BF16)
        dt_ref[...] = (lax.dot_general(sel_ref[...], hi, NT, preferred_element_type=F32)
                       + lax.dot_general(sel_ref[...], lo, NT, preferred_element_type=F32))
        gwo_ref[0:POOL_WIDTH, :] += lax.dot_general(po_ref[...], dxb, TN, preferred_element_type=F32)
        gwo_ref[POOL_WIDTH:, :] += lax.dot_general(ma_ref[...], dxb, TN, preferred_element_type=F32)

    act = jax.ShapeDtypeStruct((s, 512), BF16)
    return pl.pallas_call(
        body, name="outproj_bwd", grid=(s // tm,),
        in_specs=[_row_spec(tm, D_MODEL)] + [_row_spec(tm, 512)] * 4
                 + [_full_spec((D_MODEL, D_MODEL)), _full_spec((8, ATTN_WIDTH))],
        out_specs=[_row_spec(tm, 512)] * 3 + [pl.BlockSpec((8, tm), lambda i: (0, i)),
                                              _full_spec((D_MODEL, D_MODEL))],
        out_shape=[act, act, act, jax.ShapeDtypeStruct((8, s), F32),
                   jax.ShapeDtypeStruct((D_MODEL, D_MODEL), F32)],
        compiler_params=_params(("arbitrary",)),
    )(dx, po, ma, ag, o, wo, head_sel)


def _attn_bwd(q, k, v, do, c, lse, dt, carry=None, tq=256, chunk=512):
    s = q.shape[0]
    tk = 2 * tq
    nq = s // tq
    n_carry = 0 if carry is None else len(carry)

    def body(*refs):
        q_ref, k_ref, v_ref, do_ref, c_ref, lse_ref, dt_ref = refs[:7]
        carry_src = refs[7:7 + n_carry]
        dq_ref, dk_ref, dv_ref, dcs_ref, drs_ref = refs[7 + n_carry:12 + n_carry]
        carry_dst = refs[12 + n_carry:12 + 2 * n_carry]
        (qxt_ref, doxt_ref, qm_ref, dox_ref, dqt0_ref, dqt1_ref, dk0_ref, dk1_ref, dvx_ref,
         kx_ref, vx_ref, ktm_ref, sa_ref, sb_ref) = refs[12 + 2 * n_carry:26 + 2 * n_carry]
        carry_sems = refs[26 + 2 * n_carry:]
        carry_big = tuple(zip(carry_src, carry_dst))
        j = pl.program_id(0)
        b = pl.program_id(1)
        dqts = (dqt0_ref, dqt1_ref)
        dks = (dk0_ref, dk1_ref)

        if n_carry:
            @pl.when((j == 0) & (b == 0))
            def _():
                _scatter_start(carry_big, None, carry_sems)

        @pl.when(b == 0)
        def _():
            def prep(ch, carry):
                rows = pl.ds(pl.multiple_of(ch * chunk, chunk), chunk)
                qq = q_ref[rows, :].astype(F32)
                dd = do_ref[rows, :].astype(F32)
                qt = jnp.transpose(qq)
                ddt = jnp.transpose(dd)
                lane = lax.broadcasted_iota(jnp.int32, (chunk, PAIR), 1)
                sub = lax.broadcasted_iota(jnp.int32, (PAIR, chunk), 0)
                for h in range(2):
                    sp = _spare(h)
                    qm_ref[h, rows, :] = jnp.where(_own(lane, h), qq, jnp.where(lane == sp, 1.0, 0.0)).astype(BF16)
                    dox_ref[h, rows, :] = jnp.where(_own(lane, h), dd, 0.0).astype(BF16)
                    qx = jnp.where(_own(sub, h), qt, jnp.where((sub >= sp) & (sub < sp + 3), 1.0, 0.0))
                    for t, part in enumerate(_split3(-lse_ref[pl.ds(h, 1), rows])):
                        qx = jnp.where(sub == sp + 3 + t, part, qx)
                    qxt_ref[h, :, rows] = qx.astype(BF16)
                    dx = jnp.where(_own(sub, h), ddt, 0.0)
                    for t, part in enumerate(_split3(-dt_ref[pl.ds(h, 1), rows])):
                        dx = jnp.where(sub == sp + t, part, dx)
                    doxt_ref[h, :, rows] = dx.astype(BF16)
                return carry

            lax.fori_loop(0, s // chunk, prep, 0)
            for ref in dqts:
                ref[...] = jnp.zeros_like(ref)

        kk = k_ref[...].astype(F32)
        vv = v_ref[...].astype(F32)
        cc = c_ref[...]
        kt = jnp.transpose(kk)
        lane = lax.broadcasted_iota(jnp.int32, (tk, PAIR), 1)
        sub = lax.broadcasted_iota(jnp.int32, (PAIR, tk), 0)
        for h in range(2):
            sp = _spare(h)
            ccol = jnp.sum(jnp.where(lane == 2 * j + h, cc, 0.0), axis=1, keepdims=True)
            kx = jnp.where(_own(lane, h), kk, jnp.where((lane >= sp + 3) & (lane < sp + 6), 1.0, 0.0))
            for t, part in enumerate(_split3(-ccol)):
                kx = jnp.where(lane == sp + t, part, kx)
            kx_ref[h] = kx.astype(BF16)
            vx_ref[h] = jnp.where(_own(lane, h), vv, jnp.where((lane >= sp) & (lane < sp + 3), 1.0, 0.0)).astype(BF16)
            ktm_ref[h] = jnp.where(_own(sub, h), kt, jnp.where(sub == sp, 1.0, 0.0)).astype(BF16)
        for ref in dks:
            ref[...] = jnp.zeros_like(ref)
        dvx_ref[...] = jnp.zeros_like(dvx_ref)

        def cols_of(i):
            return pl.ds(pl.multiple_of(i * tq, tq), tq)

        def scores(i, dst):
            cols = cols_of(jnp.minimum(i, nq - 1))
            for h in range(2):
                dst[h] = jnp.dot(kx_ref[h], qxt_ref[h, :, cols], preferred_element_type=F32)
                dst[2 + h] = jnp.dot(vx_ref[h], doxt_ref[h, :, cols], preferred_element_type=F32)

        def consume(i, src, masked):
            cols = cols_of(i)
            for h in range(2):
                arg = src[h]
                if masked:
                    kpos = b * tk + lax.broadcasted_iota(jnp.int32, (tk, tq), 0)
                    qpos = i * tq + lax.broadcasted_iota(jnp.int32, (tk, tq), 1)
                    arg = jnp.where(qpos >= kpos, arg, NEG)
                pt = jnp.exp(arg)
                dst = (pt * src[2 + h]).astype(BF16)
                dvx_ref[...] += jnp.dot(pt.astype(BF16), dox_ref[h, cols, :], preferred_element_type=F32)
                dks[h][...] += jnp.dot(dst, qm_ref[h, cols, :], preferred_element_type=F32)
                dqts[h][:, cols] += jnp.dot(ktm_ref[h], dst, preferred_element_type=F32)

        i0 = 2 * b

        def two_blocks(t, masked):
            i = i0 + 2 * t
            scores(i + 1, sb_ref)
            consume(i, sa_ref, masked)
            scores(i + 2, sa_ref)
            consume(i + 1, sb_ref, masked)

        def loop_body(t, carry):
            two_blocks(t, False)
            return carry

        scores(i0, sa_ref)
        two_blocks(0, True)
        lax.fori_loop(1, (nq - i0) // 2, loop_body, 0)

        dv_ref[...] = dvx_ref[...].astype(BF16)
        dk_ref[...] = jnp.where(lane < HEAD_DIM, dk0_ref[...], dk1_ref[...]).astype(BF16)
        dcs_ref[...] = jnp.where(lane == _spare(0), dk0_ref[...], jnp.where(lane == _spare(1), dk1_ref[...], 0.0))

        @pl.when(b == s // tk - 1)
        def _():
            sub8 = lax.broadcasted_iota(jnp.int32, (8, s), 0)
            drs_ref[...] = jnp.where(sub8 == 0, dqt0_ref[_spare(0):_spare(0) + 1, :],
                                     jnp.where(sub8 == 1, dqt1_ref[_spare(1):_spare(1) + 1, :], 0.0))
            sub_c = lax.broadcasted_iota(jnp.int32, (PAIR, chunk), 0)
            for ch in range(s // chunk):
                rows = pl.ds(ch * chunk, chunk)
                both = jnp.where(sub_c < HEAD_DIM, dqt0_ref[:, rows], dqt1_ref[:, rows])
                dq_ref[rows, :] = (jnp.transpose(both) * Q_SCALE).astype(BF16)

        if n_carry:
            @pl.when((j == N_PAIRS - 1) & (b == s // tk - 1))
            def _():
                _scatter_finish(carry_big, None, carry_sems)

    act = jax.ShapeDtypeStruct((s, ATTN_WIDTH), BF16)
    pair_rows = pl.BlockSpec((s, PAIR), lambda j, b: (0, j))
    pair_blk = pl.BlockSpec((tk, PAIR), lambda j, b: (b, j))
    stat = pl.BlockSpec((None, 2, s), lambda j, b: (j, 0, 0))
    carried = [] if carry is None else list(carry)
    carry_sem = [pltpu.SemaphoreType.DMA((SCATTER_SEMS,))] * 2 if n_carry else []
    return pl.pallas_call(
        body, name="attn_bwd_scatter" if n_carry else "attn_bwd", grid=(N_PAIRS, s // tk),
        in_specs=[pair_rows, pair_blk, pair_blk, pair_rows,
                  pl.BlockSpec((tk, 128), lambda j, b: (b, 0)),
                  pl.BlockSpec((None, 8, s), lambda j, b: (j, 0, 0)), stat] + [ANY] * n_carry,
        out_specs=[pair_rows, pair_blk, pair_blk, pair_blk,
                   pl.BlockSpec((None, 8, s), lambda j, b: (j, 0, 0))] + [ANY] * n_carry,
        out_shape=[act, act, act, jax.ShapeDtypeStruct((s, N_PAIRS * 128), F32),
                   jax.ShapeDtypeStruct((N_PAIRS, 8, s), F32)] + (_scatter_shapes(*carried) if n_carry else []),
        scratch_shapes=[pltpu.VMEM((2, PAIR, s), BF16), pltpu.VMEM((2, PAIR, s), BF16),
                        pltpu.VMEM((2, s, PAIR), BF16), pltpu.VMEM((2, s, PAIR), BF16),
                        pltpu.VMEM((PAIR, s), F32), pltpu.VMEM((PAIR, s), F32),
                        pltpu.VMEM((tk, PAIR), F32), pltpu.VMEM((tk, PAIR), F32), pltpu.VMEM((tk, PAIR), F32),
                        pltpu.VMEM((2, tk, PAIR), BF16), pltpu.VMEM((2, tk, PAIR), BF16),
                        pltpu.VMEM((2, PAIR, tk), BF16),
                        pltpu.VMEM((4, tk, tq), F32), pltpu.VMEM((4, tk, tq), F32)] + carry_sem,
        compiler_params=_params(("arbitrary", "arbitrary")),
    )(q, k, v, do, c, lse, dt, *carried)


def _forget_bwd(dcs, drs, lf, tm=256):
    s = lf.shape[0]
    n = s // tm

    def body(dcs_ref, drs_ref, lf_ref, dpf_ref, dfb_ref, carry_ref):
        @pl.when(pl.program_id(0) == 0)
        def _():
            carry_ref[...] = jnp.zeros_like(carry_ref)
            dfb_ref[...] = jnp.zeros_like(dfb_ref)

        lane = lax.broadcasted_iota(jnp.int32, (tm, 128), 1)
        dc = drs_ref[...]
        for pj in range(N_PAIRS):
            blk = dcs_ref[:, pj * 128:(pj + 1) * 128]
            for h in range(2):
                head = 2 * pj + h
                moved = pltpu.roll(blk, (head - _spare(h)) % 128, 1) if head != _spare(h) else blk
                dc = dc - jnp.where(lane == head, moved, 0.0)
        dlf = _scan_rows(dc, reverse=True) + carry_ref[0:1, :]
        carry_ref[...] = dlf[0:8, :]
        dz = jnp.where(lane < 8, dlf * (1.0 - jnp.exp(lf_ref[...])), 0.0)
        dpf_ref[...] = dz.astype(BF16)
        dfb_ref[...] += jnp.sum(dz, axis=0, keepdims=True)

    return pl.pallas_call(
        body, name="forget_bwd", grid=(n,),
        in_specs=[pl.BlockSpec((tm, N_PAIRS * 128), lambda i: (n - 1 - i, 0)),
                  pl.BlockSpec((tm, 128), lambda i: (n - 1 - i, 0)),
                  pl.BlockSpec((tm, 128), lambda i: (n - 1 - i, 0))],
        out_specs=[pl.BlockSpec((tm, 128), lambda i: (n - 1 - i, 0)), _full_spec((1, 128))],
        out_shape=[jax.ShapeDtypeStruct((s, 128), BF16), jax.ShapeDtypeStruct((1, 128), F32)],
        scratch_shapes=[pltpu.VMEM((8, 128), F32)],
        compiler_params=_params(("arbitrary",)),
    )(dcs, drs, lf)


def _pool_bwd(pu, pg, dmp, pw, ps, tm=512):
    s = pu.shape[0]
    hb = tm // HALO
    n = s // tm
    last_halo = s // HALO - 1

    def body(pu_ref, halo_ref, pg_ref, dmp_ref, pgn_ref, dmpn_ref, pw_ref, ps_ref,
             dpu_ref, dpg_ref, gpw_ref, gps_ref):
        i = pl.program_id(0)

        @pl.when(i == 0)
        def _():
            gpw_ref[...] = jnp.zeros_like(gpw_ref)
            gps_ref[...] = jnp.zeros_like(gps_ref)

        u = pu_ref[...].astype(F32)
        halo = jnp.where(i == 0, 0.0, halo_ref[...].astype(F32))
        d = _pool_delta(u, halo, i * tm)
        cnt = _window_counts(i * tm, tm)
        cnt_next = _window_counts((i + 1) * tm, HALO)
        e_parts, dd_parts = [], []
        for gi in range(len(POOL_WINDOWS)):
            cols = slice(gi * GROUP_DIM, (gi + 1) * GROUP_DIM)
            wg = pw_ref[gi]
            scale = ps_ref[:, cols]
            db = d[gi].astype(BF16)
            z = jnp.dot(db, wg, preferred_element_type=F32)
            gate = pg_ref[:, cols].astype(F32)
            sg = _sigmoid(gate)
            dm = dmp_ref[:, cols].astype(F32)
            dy = dm * (gate * sg)
            dpg_ref[:, cols] = (dm * (z * scale) * (sg * (1.0 + gate * (1.0 - sg)))).astype(BF16)
            gps_ref[:, cols] += jnp.sum(dy * z, axis=0, keepdims=True)
            dz = (dy * scale).astype(BF16)
            gpw_ref[gi] += lax.dot_general(db, dz, TN, preferred_element_type=F32)
            dd = lax.dot_general(dz, wg, NT, preferred_element_type=F32)
            gate_n = pgn_ref[:, cols].astype(F32)
            dz_n = (dmpn_ref[:, cols].astype(F32) * (gate_n * _sigmoid(gate_n)) * scale).astype(BF16)
            dd_n = lax.dot_general(dz_n, wg, NT, preferred_element_type=F32)
            dd_n = jnp.where(i == n - 1, 0.0, dd_n)
            dd_parts.append(dd)
            e_parts.append(jnp.concatenate([dd / cnt[gi], dd_n / cnt_next[gi]], axis=0))
        lead = _window_sums(jnp.concatenate(e_parts, axis=1), forward=True)
        for gi in range(len(POOL_WINDOWS)):
            cols = slice(gi * GROUP_DIM, (gi + 1) * GROUP_DIM)
            dpu_ref[:, cols] = (lead[gi][0:tm, :] - dd_parts[gi]).astype(BF16)

    act = jax.ShapeDtypeStruct((s, 512), BF16)
    prev_halo = pl.BlockSpec((HALO, 512), lambda i: (jnp.maximum(i * hb - 1, 0), 0))
    next_halo = pl.BlockSpec((HALO, 512), lambda i: (jnp.minimum((i + 1) * hb, last_halo), 0))
    return pl.pallas_call(
        body, name="pool_bwd", grid=(n,),
        in_specs=[_row_spec(tm, 512), prev_halo, _row_spec(tm, 512), _row_spec(tm, 512), next_halo, next_halo,
                  _full_spec((4, GROUP_DIM, GROUP_DIM)), _full_spec((1, 512))],
        out_specs=[_row_spec(tm, 512), _row_spec(tm, 512), _full_spec((4, GROUP_DIM, GROUP_DIM)),
                   _full_spec((1, 512))],
        out_shape=[act, act, jax.ShapeDtypeStruct((4, GROUP_DIM, GROUP_DIM), F32),
                   jax.ShapeDtypeStruct((1, 512), F32)],
        compiler_params=_params(("arbitrary",)),
    )(pu, pu, pg, dmp, pg, dmp, pw, ps)


def _inproj_bwd_x(dsec, w, x, g, dxo, tm=512):
    s = x.shape[0]

    def body(*refs):
        d_refs = refs[0:7]
        w_ref, x_ref, g_ref, dxo_ref, dxi_ref, h_ref, dg_ref = refs[7:]

        @pl.when(pl.program_id(0) == 0)
        def _():
            dg_ref[...] = jnp.zeros_like(dg_ref)

        dh = None
        for d_ref, (off, ncol) in zip(d_refs, SECTIONS):
            t = lax.dot_general(d_ref[...], w_ref[:, off:off + ncol], NT, preferred_element_type=F32)
            dh = t if dh is None else dh + t
        xf = x_ref[...]
        r = lax.rsqrt(jnp.mean(xf * xf, axis=-1, keepdims=True) + RMS_EPS)
        xh = xf * r
        gg = g_ref[...]
        h_ref[...] = (xh * gg).astype(BF16)
        u = dh * gg
        dxi_ref[...] = dxo_ref[...] + r * (u - xh * jnp.mean(xh * u, axis=-1, keepdims=True))
        dg_ref[...] += jnp.sum(dh * xh, axis=0, keepdims=True)

    return pl.pallas_call(
        body, name="inproj_bwd_x", grid=(s // tm,),
        in_specs=[_row_spec(tm, ncol) for _, ncol in SECTIONS]
                 + [_full_spec((D_MODEL, IN_COLS_PAD)), _row_spec(tm, D_MODEL), _full_spec((1, D_MODEL)),
                    _row_spec(tm, D_MODEL)],
        out_specs=[_row_spec(tm, D_MODEL), _row_spec(tm, D_MODEL), _full_spec((1, D_MODEL))],
        out_shape=[jax.ShapeDtypeStruct((s, D_MODEL), F32), jax.ShapeDtypeStruct((s, D_MODEL), BF16),
                   jax.ShapeDtypeStruct((1, D_MODEL), F32)],
        compiler_params=_params(("arbitrary",)),
    )(*dsec, w, x, g, dxo)


def _inproj_bwd_w(h, dsec, tm=512):
    s = h.shape[0]

    def body(*refs):
        h_ref = refs[0]
        d_refs = refs[1:8]
        gw_ref = refs[8]

        @pl.when(pl.program_id(0) == 0)
        def _():
            gw_ref[...] = jnp.zeros_like(gw_ref)

        hh = h_ref[...]
        for d_ref, (off, ncol) in zip(d_refs, SECTIONS):
            gw_ref[:, off:off + ncol] += lax.dot_general(hh, d_ref[...], TN, preferred_element_type=F32)

    return pl.pallas_call(
        body, name="inproj_bwd_w", grid=(s // tm,),
        in_specs=[_row_spec(tm, D_MODEL)] + [_row_spec(tm, ncol) for _, ncol in SECTIONS],
        out_specs=_full_spec((D_MODEL, IN_COLS_PAD)),
        out_shape=jax.ShapeDtypeStruct((D_MODEL, IN_COLS_PAD), F32),
        compiler_params=_params(("arbitrary",), vmem_mb=56),
    )(h, *dsec)


def _elementwise(fn, name, n_out, arrays, tm):
    rows, cols = arrays[0].shape
    n_in = len(arrays)

    def body(*refs):
        outs = fn(*[r[...] for r in refs[:n_in]])
        for r, val in zip(refs[n_in:], outs):
            r[...] = val

    spec = _row_spec(tm, cols)
    shape = jax.ShapeDtypeStruct((rows, cols), F32)
    return pl.pallas_call(
        body, name=name, grid=(rows // tm,),
        in_specs=[spec] * n_in, out_specs=[spec] * n_out, out_shape=[shape] * n_out,
        compiler_params=_params(("parallel",)),
    )(*arrays)


def _add2(a, b, name, tm):
    return _elementwise(lambda p, q: (p + q,), name, 1, [a, b], tm)[0]


def _chip_sum(place, g, r, name, tm):
    n_layers, _, rows, cols = g.shape
    half = rows // 2
    nb = half // tm

    def body(place_ref, g_ref, r_ref, o_ref):
        o_ref[...] = (g_ref[...] + r_ref[...]).astype(BF16)

    blk = (None, None, tm, cols)
    return pl.pallas_call(
        body, name=name,
        grid_spec=pltpu.PrefetchScalarGridSpec(
            num_scalar_prefetch=1, grid=(n_layers, N_CHIPS, nb),
            in_specs=[pl.BlockSpec(blk, lambda l, k, i, p: (l, k, p[1] * nb + i, 0)),
                      pl.BlockSpec(blk, lambda l, k, i, p: (l, k, i, 0))],
            out_specs=pl.BlockSpec(blk, lambda l, k, i, p: (l, k, i, 0))),
        out_shape=jax.ShapeDtypeStruct((n_layers, N_CHIPS, half, cols), BF16),
        compiler_params=_params(("parallel", "parallel", "parallel")),
    )(place, g, r)


def _mesh_sum(place, g, r, got, into, layer, name, tm):
    _, _, rows, cols = g.shape
    half = rows // 2
    nb = half // tm

    def body(place_ref, g_ref, r_ref, got_ref, into_ref, o_ref):
        own = g_ref[...] + r_ref[...]
        o_ref[...] = (own + got_ref[0].astype(F32)) + (got_ref[1].astype(F32) + got_ref[2].astype(F32))

    blk = (None, None, tm, cols)
    return pl.pallas_call(
        body, name=name,
        grid_spec=pltpu.PrefetchScalarGridSpec(
            num_scalar_prefetch=1, grid=(nb,),
            in_specs=[pl.BlockSpec(blk, lambda i, p: (0, p[0], p[1] * nb + i, 0)),
                      pl.BlockSpec(blk, lambda i, p: (0, p[0], i, 0)),
                      pl.BlockSpec((None, 3, tm, cols), lambda i, p: (0, 0, i, 0)),
                      ANY],
            out_specs=pl.BlockSpec((None, tm, cols), lambda i, p: (layer, p[1] * nb + i, 0))),
        out_shape=jax.ShapeDtypeStruct(into.shape, F32),
        input_output_aliases={4: 0},
        compiler_params=_params(("parallel",)),
    )(place, g, r, got, into)


def _mesh_sum_small(place, ps, got, tm):
    nb = SMALL_HALF // tm

    def body(place_ref, ps_ref, got_ref, o_ref):
        o_ref[...] = (ps_ref[...] + got_ref[0]) + (got_ref[1] + got_ref[2])

    return pl.pallas_call(
        body, name="mesh_sum_small",
        grid_spec=pltpu.PrefetchScalarGridSpec(
            num_scalar_prefetch=1, grid=(nb,),
            in_specs=[pl.BlockSpec((tm, 128), lambda i, p: (p[1] * nb + i, 0)),
                      pl.BlockSpec((3, tm, 128), lambda i, p: (0, i, 0))],
            out_specs=pl.BlockSpec((tm, 128), lambda i, p: (p[1] * nb + i, 0))),
        out_shape=jax.ShapeDtypeStruct((SMALL_ROWS, 128), F32),
        compiler_params=_params(("parallel",)),
    )(place, ps, got)


def _adamw(g, w, m, v, name, tm):
    def fn(g, w, m, v):
        m = ADAM_B1 * m + (1.0 - ADAM_B1) * g
        v = ADAM_B2 * v + (1.0 - ADAM_B2) * (g * g)
        m_hat = m / (1.0 - ADAM_B1 ** ADAM_STEP)
        v_hat = v / (1.0 - ADAM_B2 ** ADAM_STEP)
        delta = -ADAM_LR * (m_hat / (jnp.sqrt(v_hat) + ADAM_EPS) + ADAM_WD * w)
        return g, delta, m, v

    return _elementwise(fn, name, 4, [g, w, m, v], tm)


def _position():
    x, y, c = lax.axis_index("x"), lax.axis_index("y"), lax.axis_index("c")
    other_chips = [(x, 1 - y), (1 - x, y), (1 - x, 1 - y)]
    return x, y, c, other_chips


def _remote(src, dst, sems, k, to):
    send_sems, recv_sems = sems
    return pltpu.make_async_remote_copy(src_ref=src, dst_ref=dst, send_sem=send_sems.at[k],
                                        recv_sem=recv_sems.at[k], device_id=to, device_id_type=MESH)


def _comm_call(body, name, n_in, out_shape, n_remote, aliases=None):
    return pl.pallas_call(
        body, name=name, in_specs=[ANY] * n_in, out_specs=[ANY] * len(out_shape), out_shape=out_shape,
        scratch_shapes=[pltpu.SemaphoreType.DMA((n_remote,)), pltpu.SemaphoreType.DMA((n_remote,))],
        input_output_aliases=aliases or {},
    )


def _halves(ref_rows, c):
    half = ref_rows // 2
    return pl.ds(half * c, half), pl.ds(half * (1 - c), half)


def _gather_weights(wi, wo):
    def body(wi_ref, wo_ref, gi_ref, go_ref, send_sems, recv_sems):
        _gather_start((wi_ref, wo_ref), (gi_ref, go_ref), (send_sems, recv_sems))
        _gather_finish((wi_ref, wo_ref), (gi_ref, go_ref), (send_sems, recv_sems))

    return _comm_call(body, "gather_weights", 2, _gather_shapes(wi, wo), GATHER_SEMS)(wi, wo)


GATHER_SEMS = 12
SCATTER_SEMS = 9


def _gather_shapes(wi, wo):
    return [jax.ShapeDtypeStruct((a.shape[0], 3) + a.shape[1:], a.dtype) for a in (wi, wo)]


def _gather_sends(srcs, dsts, sems):
    x, y, c, chips = _position()
    out = []
    for k, (cx, cy) in enumerate(chips):
        for a, (src, dst) in enumerate(zip(srcs, dsts)):
            mine, _ = _halves(src.shape[1], c)
            out.append(_remote(src.at[:, mine], dst.at[:, k, mine], sems, len(srcs) * k + a, (cx, cy, c)))
    return out


def _gather_start(srcs, dsts, sems):
    for cp in _gather_sends(srcs, dsts, sems):
        cp.start()


def _gather_finish(srcs, dsts, sems):
    x, y, c, _ = _position()
    n = len(srcs)
    forwards = []
    for k in range(3):
        for a, dst in enumerate(dsts):
            mine, _ = _halves(dst.shape[2], c)
            blk = dst.at[:, k, mine]
            _remote(blk, blk, sems, n * k + a, (x, y, c)).wait_recv()
            forwards.append(_remote(blk, blk, sems, 3 * n + n * k + a, (x, y, 1 - c)))
            forwards[-1].start()
    for k in range(3):
        for a, dst in enumerate(dsts):
            _, other = _halves(dst.shape[2], c)
            blk = dst.at[:, k, other]
            _remote(blk, blk, sems, 3 * n + n * k + a, (x, y, c)).wait_recv()
    for cp in _gather_sends(srcs, dsts, sems) + forwards:
        cp.wait_send()


def _scatter_sends(big, small, sems):
    x, y, c, chips = _position()
    n = len(big) + (small is not None)
    out = []
    for k, (cx, cy) in enumerate(chips):
        ck = 2 * cx + cy
        for a, (src, dst) in enumerate(big):
            out.append(_remote(src.at[:, ck], dst.at[:, k], sems, n * k + a, (cx, cy, c)))
        if small is not None:
            half, _ = _halves(SMALL_ROWS, c)
            out.append(_remote(small[0].at[half], small[1].at[k], sems, n * k + n - 1, (cx, cy, c)))
    return out


def _scatter_start(big, small, sems):
    for cp in _scatter_sends(big, small, sems):
        cp.start()


def _scatter_finish(big, small, sems):
    x, y, c, _ = _position()
    n = len(big) + (small is not None)
    for k in range(3):
        landing = [dst.at[:, k] for _, dst in big] + ([small[1].at[k]] if small is not None else [])
        for a, blk in enumerate(landing):
            _remote(blk, blk, sems, n * k + a, (x, y, c)).wait_recv()
    for cp in _scatter_sends(big, small, sems):
        cp.wait_send()


def _exchange_sibling(gi, go, sm=None):
    n_big = 2

    def body(*refs):
        n = n_big + (sm is not None)
        srcs, dsts, sems = refs[:n], refs[n:2 * n], refs[2 * n:]
        x, y, c, _ = _position()
        cps = [_remote(src.at[:, :, _halves(src.shape[2], c)[1]], dst, sems, a, (x, y, 1 - c))
               for a, (src, dst) in enumerate(zip(srcs[:n_big], dsts[:n_big]))]
        if sm is not None:
            cps.append(_remote(srcs[-1], dsts[-1], sems, n_big, (x, y, 1 - c)))
        for cp in cps:
            cp.start()
        for cp in cps:
            cp.wait()

    arrays = [gi, go] + ([sm] if sm is not None else [])
    out_shape = [jax.ShapeDtypeStruct(a.shape[:2] + (a.shape[2] // 2, a.shape[3]), F32) for a in (gi, go)]
    if sm is not None:
        out_shape.append(jax.ShapeDtypeStruct(sm.shape, F32))
    name = "exchange_sibling" + ("_small" if sm is not None else "")
    return _comm_call(body, name, len(arrays), out_shape, 3)(*arrays)


def _scatter_shapes(pi, po):
    return [jax.ShapeDtypeStruct((a.shape[0], 3) + a.shape[2:], a.dtype) for a in (pi, po)]


def _scatter_chips(pi, po, ps):
    def body(pi_ref, po_ref, ps_ref, ri_ref, ro_ref, rs_ref, send_sems, recv_sems):
        big, small, sems = ((pi_ref, ri_ref), (po_ref, ro_ref)), (ps_ref, rs_ref), (send_sems, recv_sems)
        _scatter_start(big, small, sems)
        _scatter_finish(big, small, sems)

    out_shape = _scatter_shapes(pi, po) + [jax.ShapeDtypeStruct((3, SMALL_HALF, 128), F32)]
    return _comm_call(body, "scatter_chips", 3, out_shape, SCATTER_SEMS)(pi, po, ps)


def _join_sibling(fi, fo, fs):
    def body(fi_ref, fo_ref, fs_ref, oi_ref, oo_ref, os_ref, send_sems, recv_sems):
        x, y, c, _ = _position()
        sems = (send_sems, recv_sems)
        sent, got = [], []
        for a, (src, dst) in enumerate(((fi_ref, oi_ref), (fo_ref, oo_ref))):
            mine, other = _halves(src.shape[1], c)
            sent.append(_remote(src.at[:, mine], dst.at[:, mine], sems, a, (x, y, 1 - c)))
            got.append(_remote(dst.at[:, other], dst.at[:, other], sems, a, (x, y, c)))
        mine, other = _halves(SMALL_ROWS, c)
        sent.append(_remote(fs_ref.at[mine], os_ref.at[mine], sems, 2, (x, y, 1 - c)))
        got.append(_remote(os_ref.at[other], os_ref.at[other], sems, 2, (x, y, c)))
        for cp in sent:
            cp.start()
        for cp in got:
            cp.wait_recv()
        for cp in sent:
            cp.wait_send()

    out_shape = [jax.ShapeDtypeStruct(a.shape, F32) for a in (fi, fo, fs)]
    return _comm_call(body, "join_sibling", 3, out_shape, 3, aliases={0: 0, 1: 1, 2: 2})(fi, fo, fs)


def _pack_small(norm_g, final_g, pool_scale, forget_bias, pool_w):
    fb = jnp.pad(forget_bias.reshape(-1), (0, 128 - forget_bias.size))
    flat = jnp.concatenate([norm_g.reshape(-1), final_g.reshape(-1), pool_scale.reshape(-1), fb,
                            pool_w.reshape(-1)])
    return jnp.pad(flat, (0, SMALL_ROWS * 128 - flat.size)).reshape(SMALL_ROWS, 128)


def _unpack_small(packed):
    flat = packed.reshape(-1)
    sizes = (DEPTH * D_MODEL, D_MODEL, DEPTH * POOL_WIDTH, 128, DEPTH * 4 * GROUP_DIM * GROUP_DIM)
    offs = [0]
    for n in sizes:
        offs.append(offs[-1] + n)
    norm_g, final_g, pool_scale, fb, pool_w = [flat[offs[i]:offs[i + 1]] for i in range(5)]
    return (norm_g.reshape(DEPTH, D_MODEL), final_g, pool_scale.reshape(DEPTH, POOL_WIDTH),
            fb[:DEPTH * 8].reshape(DEPTH, 8), pool_w.reshape(DEPTH, 4, GROUP_DIM, GROUP_DIM))


def kernel(x, norm_g, w_in, forget_bias, pool_w, pool_scale, w_out, final_g, loss_target, m_norm_g, m_w_in, m_forget_bias, m_pool_w, m_pool_scale, m_w_out, m_final_g, v_norm_g, v_w_in, v_forget_bias, v_pool_w, v_pool_scale, v_w_out, v_final_g):
    x0 = x[0]
    tgt = loss_target[0]
    s = x0.shape[0]

    me = 2 * lax.axis_index("x") + lax.axis_index("y")
    place = jnp.stack([me, lax.axis_index("c")]).astype(jnp.int32)
    wi_b, wo_b = w_in.astype(BF16), w_out.astype(BF16)

    def chip_piece(own, got, chip):
        rel = jnp.bitwise_xor(chip, me)
        other = lax.dynamic_index_in_dim(got, jnp.maximum(rel - 1, 0), axis=1, keepdims=False)
        return jnp.where(rel == 0, own, other)[0]

    def layer_weights(l, gi, go):
        w_i = jnp.concatenate([chip_piece(wi_b[l:l + 1], gi, k) for k in range(N_CHIPS)], axis=-1)
        w_o = jnp.concatenate([chip_piece(wo_b[l:l + 1], go, k) for k in range(N_CHIPS)], axis=0)
        return jnp.pad(w_i, ((0, 0), (0, IN_COLS_PAD - IN_COLS))), w_o

    pool_w_b = pool_w.astype(BF16)
    fb_pad = jnp.pad(forget_bias, ((0, 0), (0, 128 - forget_bias.shape[1])))
    head_sel = (jnp.arange(ATTN_WIDTH)[None, :] // HEAD_DIM == jnp.arange(8)[:, None]).astype(BF16)

    saved, weights = [], []
    xl = x0
    gathered = _gather_weights(wi_b[0:1], wo_b[0:1])
    for l in range(DEPTH):
        w_i, w_o = layer_weights(l, *gathered)
        weights.append((w_i, w_o))
        pu, pg, q, k, v, ag, lf, c = _inproj_fwd(xl, norm_g[l][None], w_i, fb_pad[l][None])
        po = _pool_fwd(pu, pg, pool_w_b[l], pool_scale[l][None])
        if l + 1 < DEPTH:
            ma, o, lse, *gathered = _attn_fwd(q, k, v, c, ag, carry=(wi_b[l + 1:l + 2], wo_b[l + 1:l + 2]))
        else:
            ma, o, lse = _attn_fwd(q, k, v, c, ag)
        saved.append((xl, pu, pg, q, k, v, ag, lf, c, po, ma, o, lse))
        xl = _outproj_fwd(xl, po, ma, w_o)

    dx, g_final, loss_part = _loss_head(xl, tgt, final_g[None])
    loss = lax.psum(loss_part[0, 0], ("x", "y", "c"))

    g_norm, g_fb, g_pw, g_ps = [], [], [], []
    parts = {}
    pending = None
    for l in reversed(range(DEPTH)):
        xl, pu, pg, q, k, v, ag, lf, c, po, ma, o, lse = saved[l]
        w_i, w_o = weights[l]
        dmp, do, dag, dt, gwo = _outproj_bwd(dx, po, ma, ag, o, w_o, head_sel)
        if pending is None:
            dq, dk, dv, dcs, drs = _attn_bwd(q, k, v, do, c, lse, dt.reshape(N_PAIRS, 2, s))
        else:
            dq, dk, dv, dcs, drs, qi, qo = _attn_bwd(q, k, v, do, c, lse, dt.reshape(N_PAIRS, 2, s), carry=pending)
            parts[l + 1] += (qi, qo)
        drs = jnp.pad(drs[:, 0:2, :].reshape(8, s).T, ((0, 0), (0, 120)))
        dpf, dfb = _forget_bwd(dcs, drs, lf)
        dpu, dpg, gpw, gps = _pool_bwd(pu, pg, dmp, pool_w_b[l], pool_scale[l][None])
        dsec = (dpu, dpg, dq, dk, dv, dag, dpf)
        dx, h, gn = _inproj_bwd_x(dsec, w_i, xl, norm_g[l][None], dx)
        gwi = _inproj_bwd_w(h, dsec)
        g_norm.append(gn[0])
        g_fb.append(dfb[0, :8])
        g_pw.append(gpw)
        g_ps.append(gps[0])
        send_i = jnp.stack([gwi[:, k * SHARD_COLS:(k + 1) * SHARD_COLS] for k in range(N_CHIPS)])[None]
        send_o = gwo.reshape(1, N_CHIPS, SHARD_ROWS, D_MODEL)
        if l > 0:
            ri, ro = _exchange_sibling(send_i, send_o)
        else:
            g_norm, g_fb, g_pw, g_ps = [t[::-1] for t in (g_norm, g_fb, g_pw, g_ps)]
            send_s = _pack_small(jnp.stack(g_norm), g_final[0], jnp.stack(g_ps), jnp.stack(g_fb), jnp.stack(g_pw))
            ri, ro, rs = _exchange_sibling(send_i, send_o, send_s)
        pending = (_chip_sum(place, send_i, ri, "chip_sum_w_in", 256),
                   _chip_sum(place, send_o, ro, "chip_sum_w_out", SHARD_ROWS // 2))
        parts[l] = (send_i, ri, send_o, ro)
    ps_ = _add2(send_s, rs, "chip_sum_small", SMALL_ROWS // 4)
    qi, qo, qs = _scatter_chips(*pending, ps_)
    parts[0] += (qi, qo)
    fi = lax.empty((DEPTH, D_MODEL, SHARD_COLS), F32)
    fo = lax.empty((DEPTH, SHARD_ROWS, D_MODEL), F32)
    for l in range(DEPTH):
        send_i, ri, send_o, ro, qi, qo = parts[l]
        fi = _mesh_sum(place, send_i, ri, qi, fi, l, "mesh_sum_w_in", 256)
        fo = _mesh_sum(place, send_o, ro, qo, fo, l, "mesh_sum_w_out", SHARD_ROWS // 2)
    hs = _mesh_sum_small(place, ps_, qs, SMALL_HALF // 4)
    fi, fo, fs = _join_sibling(fi, fo, hs)

    flat_i = lambda a: a.reshape(-1, SHARD_COLS)
    flat_o = lambda a: a.reshape(-1, D_MODEL)
    out_i = _adamw(flat_i(fi), flat_i(w_in), flat_i(m_w_in), flat_i(v_w_in), "adamw_w_in", 256)
    out_o = _adamw(flat_o(fo), flat_o(w_out), flat_o(m_w_out), flat_o(v_w_out), "adamw_w_out", 256)
    out_s = _adamw(fs, _pack_small(norm_g, final_g, pool_scale, forget_bias, pool_w),
                   _pack_small(m_norm_g, m_final_g, m_pool_scale, m_forget_bias, m_pool_w),
                   _pack_small(v_norm_g, v_final_g, v_pool_scale, v_forget_bias, v_pool_w),
                   "adamw_small", SMALL_ROWS // 4)

    groups = []
    for oi, oo, os_ in zip(out_i, out_o, out_s):
        sn, sf, sp, sb, sw = _unpack_small(os_)
        groups.append((sn, oi.reshape(w_in.shape), sb, sw, sp, oo.reshape(w_out.shape), sf))
    return (loss, dx[None]) + tuple(t for grp in groups for t in grp)
```

```python
import jax
import jax.numpy as jnp
from jax import lax
from jax.experimental import pallas as pl
from jax.experimental.pallas import tpu as pltpu

F32 = jnp.float32
BF16 = jnp.bfloat16

D_MODEL = 1024
DEPTH = 4
POOL_WIDTH = 512
ATTN_WIDTH = 512
HEAD_DIM = 64
PAIR = 2 * HEAD_DIM
N_PAIRS = ATTN_WIDTH // PAIR
POOL_WINDOWS = (2, 4, 8, 16)
GROUP_DIM = 128
HALO = 16
IN_COLS = 3080
OFF_F = 3072
IN_COLS_PAD = 3200
SECTIONS = ((0, 512), (512, 512), (1024, 512), (1536, 512), (2048, 512), (2560, 512), (OFF_F, 128))
N_CHIPS = 4
SHARD_COLS = IN_COLS // N_CHIPS
SHARD_ROWS = D_MODEL // N_CHIPS
RMS_EPS = 1e-6
NEG = -1e30
Q_SCALE = 0.125

ADAM_LR = 0.001
ADAM_B1 = 0.9
ADAM_B2 = 0.999
ADAM_EPS = 1e-08
ADAM_WD = 0.01
ADAM_STEP = 10

SMALL_ROWS = 2112
SMALL_HALF = SMALL_ROWS // 2

NT = (((1,), (1,)), ((), ()))
TN = (((0,), (0,)), ((), ()))
MESH = pl.DeviceIdType.MESH
ANY = pl.BlockSpec(memory_space=pl.ANY)


def _params(semantics, vmem_mb=48):
    return pltpu.CompilerParams(dimension_semantics=semantics, vmem_limit_bytes=vmem_mb << 20)


def _row_spec(tm, cols):
    return pl.BlockSpec((tm, cols), lambda i: (i, 0))


def _full_spec(shape):
    return pl.BlockSpec(shape, lambda *_: (0,) * len(shape))


def _sigmoid(x):
    return jax.nn.sigmoid(x)


def _scan_rows(a, reverse=False):
    n = a.shape[0]
    row = lax.broadcasted_iota(jnp.int32, a.shape, 0)
    k = 1
    while k < n:
        if reverse:
            a = a + jnp.where(row < n - k, pltpu.roll(a, n - k, 0), 0.0)
        else:
            a = a + jnp.where(row >= k, pltpu.roll(a, k, 0), 0.0)
        k *= 2
    return a


def _inproj_fwd(x, g, w, fb, tm=512):
    s = x.shape[0]

    def body(x_ref, g_ref, w_ref, fb_ref, pu_ref, pg_ref, q_ref, k_ref, v_ref, ag_ref,
             lf_ref, c_ref, h_ref, carry_ref):
        @pl.when(pl.program_id(0) == 0)
        def _():
            carry_ref[...] = jnp.zeros_like(carry_ref)

        xf = x_ref[...]
        r = lax.rsqrt(jnp.mean(xf * xf, axis=-1, keepdims=True) + RMS_EPS)
        h = (xf * r * g_ref[...]).astype(BF16)
        h_ref[...] = h

        def proj(sec):
            off, n = SECTIONS[sec]
            return jnp.dot(h, w_ref[:, off:off + n], preferred_element_type=F32)

        pu_ref[...] = proj(0).astype(BF16)
        pg_ref[...] = proj(1).astype(BF16)
        q_ref[...] = (proj(2) * Q_SCALE).astype(BF16)
        k_ref[...] = proj(3).astype(BF16)
        v_ref[...] = proj(4).astype(BF16)
        ag_ref[...] = proj(5).astype(BF16)
        z = proj(6) + fb_ref[...]
        lf = jnp.minimum(z, 0.0) - jnp.log(1.0 + jnp.exp(-jnp.abs(z)))
        lf_ref[...] = lf
        c_ref[...] = _scan_rows(lf) + carry_ref[0:1, :]
        carry_ref[0:1, :] = c_ref[tm - 1:tm, :]

    act = jax.ShapeDtypeStruct((s, 512), BF16)
    return pl.pallas_call(
        body, name="inproj_fwd", grid=(s // tm,),
        in_specs=[_row_spec(tm, D_MODEL), _full_spec((1, D_MODEL)), _full_spec((D_MODEL, IN_COLS_PAD)),
                  _full_spec((1, 128))],
        out_specs=[_row_spec(tm, 512)] * 6 + [_row_spec(tm, 128), _row_spec(tm, 128), _row_spec(tm, D_MODEL)],
        out_shape=[act] * 6 + [jax.ShapeDtypeStruct((s, 128), F32), jax.ShapeDtypeStruct((s, 128), F32),
                               jax.ShapeDtypeStruct((s, D_MODEL), BF16)],
        scratch_shapes=[pltpu.VMEM((8, 128), F32)],
        compiler_params=_params(("arbitrary",)),
    )(x, g, w, fb)


def _window_sums(ext, forward):
    n = ext.shape[0]
    outs = []
    for gi, w in enumerate(POOL_WINDOWS):
        a = ext[:, gi * GROUP_DIM:(gi + 1) * GROUP_DIM]
        k = 1
        while k < w:
            a = a + pltpu.roll(a, (n - k) if forward else k, 0)
            k *= 2
        outs.append(a)
    return outs


def _window_counts(row0, tm):
    t = row0 + lax.broadcasted_iota(jnp.int32, (tm, GROUP_DIM), 0)
    return [jnp.minimum(t + 1, w).astype(F32) for w in POOL_WINDOWS]


def _pool_delta(u, halo, row0):
    tm = u.shape[0]
    sums = _window_sums(jnp.concatenate([halo, u], axis=0), forward=False)
    cnt = _window_counts(row0, tm)
    return [sums[gi][HALO:, :] / cnt[gi] - u[:, gi * GROUP_DIM:(gi + 1) * GROUP_DIM]
            for gi in range(len(POOL_WINDOWS))]


def _pool_fwd(pu, pg, pw, ps, tm=512):
    s = pu.shape[0]
    hb = tm // HALO

    def body(pu_ref, halo_ref, pg_ref, pw_ref, ps_ref, po_ref):
        i = pl.program_id(0)
        u = pu_ref[...].astype(F32)
        halo = jnp.where(i == 0, 0.0, halo_ref[...].astype(F32))
        d = _pool_delta(u, halo, i * tm)
        for gi in range(len(POOL_WINDOWS)):
            cols = slice(gi * GROUP_DIM, (gi + 1) * GROUP_DIM)
            z = jnp.dot(d[gi].astype(BF16), pw_ref[gi], preferred_element_type=F32)
            gate = pg_ref[:, cols].astype(F32)
            po_ref[:, cols] = (z * ps_ref[:, cols] * (gate * _sigmoid(gate))).astype(BF16)

    return pl.pallas_call(
        body, name="pool_fwd", grid=(s // tm,),
        in_specs=[_row_spec(tm, 512),
                  pl.BlockSpec((HALO, 512), lambda i: (jnp.maximum(i * hb - 1, 0), 0)),
                  _row_spec(tm, 512), _full_spec((4, GROUP_DIM, GROUP_DIM)), _full_spec((1, 512))],
        out_specs=_row_spec(tm, 512),
        out_shape=jax.ShapeDtypeStruct((s, 512), BF16),
        compiler_params=_params(("parallel",)),
    )(pu, pu, pg, pw, ps)


def _split3(x):
    hi = x.astype(BF16).astype(F32)
    r1 = x - hi
    lo = r1.astype(BF16).astype(F32)
    return hi, lo, (r1 - lo).astype(BF16).astype(F32)


def _own(idx, h):
    return (idx < HEAD_DIM) if h == 0 else (idx >= HEAD_DIM)


def _spare(h):
    return HEAD_DIM if h == 0 else 0


def _attn_fwd(q, k, v, c, ag, carry=None, tq=256, tk=512, chunk=512):
    s = q.shape[0]
    n_carry = 0 if carry is None else len(carry)

    def body(*refs):
        q_ref, k_ref, v_ref, c_ref, ag_ref = refs[:5]
        carry_src = refs[5:5 + n_carry]
        ma_ref, o_ref, lse_ref = refs[5 + n_carry:8 + n_carry]
        carry_dst = refs[8 + n_carry:8 + 2 * n_carry]
        kx_ref, vt_ref, acc0_ref, acc1_ref, sta_ref, stb_ref = refs[8 + 2 * n_carry:14 + 2 * n_carry]
        carry_sems = refs[14 + 2 * n_carry:]
        j = pl.program_id(0)
        i = pl.program_id(1)

        if n_carry:
            @pl.when((j == 0) & (i == 0))
            def _():
                _gather_start(carry_src, carry_dst, carry_sems)

        @pl.when(i == 0)
        def _():
            def prep(ch, carry):
                rows = pl.ds(pl.multiple_of(ch * chunk, chunk), chunk)
                kk = k_ref[rows, :].astype(F32)
                cc = c_ref[rows, :]
                vt = jnp.transpose(v_ref[rows, :].astype(F32))
                lane = lax.broadcasted_iota(jnp.int32, (chunk, PAIR), 1)
                sub = lax.broadcasted_iota(jnp.int32, (PAIR, chunk), 0)
                for h in range(2):
                    ccol = jnp.sum(jnp.where(lane == 2 * j + h, cc, 0.0), axis=1, keepdims=True)
                    kx = jnp.where(_own(lane, h), kk, 0.0)
                    for t, part in enumerate(_split3(-ccol)):
                        kx = jnp.where(lane == _spare(h) + t, part, kx)
                    kx_ref[h, rows, :] = kx.astype(BF16)
                    vt_ref[h, :, rows] = jnp.where(_own(sub, h), vt,
                                                   jnp.where(sub == _spare(h), 1.0, 0.0)).astype(BF16)
                return carry

            lax.fori_loop(0, s // chunk, prep, 0)

        qq = q_ref[...].astype(F32)
        lane_q = lax.broadcasted_iota(jnp.int32, (tq, PAIR), 1)
        qx = []
        for h in range(2):
            ones = (lane_q >= _spare(h)) & (lane_q < _spare(h) + 3)
            qx.append(jnp.transpose(jnp.where(_own(lane_q, h), qq, jnp.where(ones, 1.0, 0.0))).astype(BF16))
        accs = (acc0_ref, acc1_ref)
        for acc in accs:
            acc[...] = jnp.zeros_like(acc)

        def rows_of(kb):
            return pl.ds(pl.multiple_of(kb * tk, tk), tk)

        def scores(kb, dst):
            for h in range(2):
                dst[h] = jnp.dot(kx_ref[h, rows_of(kb), :], qx[h], preferred_element_type=F32)

        def consume(kb, src, m, masked):
            m_out, alpha, pv = [], [], []
            for h in range(2):
                sh = src[h]
                if masked:
                    kpos = kb * tk + lax.broadcasted_iota(jnp.int32, (tk, tq), 0)
                    qpos = i * tq + lax.broadcasted_iota(jnp.int32, (tk, tq), 1)
                    sh = jnp.where(qpos >= kpos, sh, NEG)
                m_new = jnp.maximum(m[h], jnp.max(sh, axis=0, keepdims=True))
                pt = jnp.exp(sh - m_new).astype(BF16)
                alpha.append(jnp.exp(m[h] - m_new))
                pv.append(jnp.dot(vt_ref[h, :, rows_of(kb)], pt, preferred_element_type=F32))
                m_out.append(m_new)
            for h in range(2):
                accs[h][...] = accs[h][...] * alpha[h] + pv[h]
            return tuple(m_out)

        n_full = (i * tq) // tk

        def two_blocks(t, m):
            kb = 2 * t
            scores(kb + 1, stb_ref)
            m = consume(kb, sta_ref, m, False)
            scores(kb + 2, sta_ref)
            return consume(kb + 1, stb_ref, m, False)

        def odd_tail(m):
            scores(n_full, stb_ref)
            m = consume(n_full - 1, sta_ref, m, False)
            return consume(n_full, stb_ref, m, True)

        def even_tail(m):
            return consume(n_full, sta_ref, m, True)

        scores(0, sta_ref)
        m0 = jnp.full((1, tq), NEG, F32)
        m = lax.fori_loop(0, n_full // 2, two_blocks, (m0, m0))
        m = lax.cond(lax.rem(n_full, 2) == 1, odd_tail, even_tail, m)

        sub_o = lax.broadcasted_iota(jnp.int32, (PAIR, tq), 0)
        den = [accs[h][_spare(h):_spare(h) + 1, :] for h in range(2)]
        ot = jnp.where(sub_o < HEAD_DIM, acc0_ref[...] / den[0], acc1_ref[...] / den[1])
        o = jnp.transpose(ot)
        o_ref[...] = o.astype(BF16)
        gate = ag_ref[...].astype(F32)
        ma_ref[...] = (o * (gate * _sigmoid(gate))).astype(BF16)
        sub8 = lax.broadcasted_iota(jnp.int32, (8, tq), 0)
        lse_ref[...] = jnp.where(sub8 == 0, m[0] + jnp.log(den[0]), m[1] + jnp.log(den[1]))

        if n_carry:
            @pl.when((j == N_PAIRS - 1) & (i == s // tq - 1))
            def _():
                _gather_finish(carry_src, carry_dst, carry_sems)

    carried = [] if carry is None else list(carry)
    carry_sem = [pltpu.SemaphoreType.DMA((GATHER_SEMS,))] * 2 if n_carry else []
    return pl.pallas_call(
        body, name="attn_fwd_gather" if n_carry else "attn_fwd", grid=(N_PAIRS, s // tq),
        in_specs=[pl.BlockSpec((tq, PAIR), lambda j, i: (i, j)),
                  pl.BlockSpec((s, PAIR), lambda j, i: (0, j)),
                  pl.BlockSpec((s, PAIR), lambda j, i: (0, j)),
                  pl.BlockSpec((s, 128), lambda j, i: (0, 0)),
                  pl.BlockSpec((tq, PAIR), lambda j, i: (i, j))] + [ANY] * n_carry,
        out_specs=[pl.BlockSpec((tq, PAIR), lambda j, i: (i, j)),
                   pl.BlockSpec((tq, PAIR), lambda j, i: (i, j)),
                   pl.BlockSpec((None, 8, tq), lambda j, i: (j, 0, i))] + [ANY] * n_carry,
        out_shape=[jax.ShapeDtypeStruct((s, ATTN_WIDTH), BF16), jax.ShapeDtypeStruct((s, ATTN_WIDTH), BF16),
                   jax.ShapeDtypeStruct((N_PAIRS, 8, s), F32)] + (_gather_shapes(*carried) if n_carry else []),
        scratch_shapes=[pltpu.VMEM((2, s, PAIR), BF16), pltpu.VMEM((2, PAIR, s), BF16),
                        pltpu.VMEM((PAIR, tq), F32), pltpu.VMEM((PAIR, tq), F32),
                        pltpu.VMEM((2, tk, tq), F32), pltpu.VMEM((2, tk, tq), F32)] + carry_sem,
        compiler_params=_params(("arbitrary", "arbitrary")),
    )(q, k, v, c, ag, *carried)


def _outproj_fwd(x, po, ma, wo, tm=512):
    s = x.shape[0]

    def body(x_ref, po_ref, ma_ref, wo_ref, xn_ref):
        xn_ref[...] = (x_ref[...]
                       + jnp.dot(po_ref[...], wo_ref[0:POOL_WIDTH, :], preferred_element_type=F32)
                       + jnp.dot(ma_ref[...], wo_ref[POOL_WIDTH:, :], preferred_element_type=F32))

    return pl.pallas_call(
        body, name="outproj_fwd", grid=(s // tm,),
        in_specs=[_row_spec(tm, D_MODEL), _row_spec(tm, 512), _row_spec(tm, 512),
                  _full_spec((D_MODEL, D_MODEL))],
        out_specs=_row_spec(tm, D_MODEL),
        out_shape=jax.ShapeDtypeStruct((s, D_MODEL), F32),
        compiler_params=_params(("parallel",)),
    )(x, po, ma, wo)


def _loss_head(x, tgt, g, tm=512):
    s = x.shape[0]

    def body(x_ref, t_ref, g_ref, dx_ref, dg_ref, loss_ref):
        @pl.when(pl.program_id(0) == 0)
        def _():
            dg_ref[...] = jnp.zeros_like(dg_ref)
            loss_ref[...] = jnp.zeros_like(loss_ref)

        xf = x_ref[...]
        r = lax.rsqrt(jnp.mean(xf * xf, axis=-1, keepdims=True) + RMS_EPS)
        xh = xf * r
        gg = g_ref[...]
        e = xh * gg - t_ref[...]
        loss_ref[...] += 0.5 * jnp.sum(jnp.mean(e * e, axis=-1, keepdims=True))
        dy = e * (1.0 / D_MODEL)
        u = dy * gg
        dx_ref[...] = r * (u - xh * jnp.mean(xh * u, axis=-1, keepdims=True))
        dg_ref[...] += jnp.sum(dy * xh, axis=0, keepdims=True)

    return pl.pallas_call(
        body, name="loss_head", grid=(s // tm,),
        in_specs=[_row_spec(tm, D_MODEL), _row_spec(tm, D_MODEL), _full_spec((1, D_MODEL))],
        out_specs=[_row_spec(tm, D_MODEL), _full_spec((1, D_MODEL)), _full_spec((8, 128))],
        out_shape=[jax.ShapeDtypeStruct((s, D_MODEL), F32), jax.ShapeDtypeStruct((1, D_MODEL), F32),
                   jax.ShapeDtypeStruct((8, 128), F32)],
        compiler_params=_params(("arbitrary",)),
    )(x, tgt, g)


def _outproj_bwd(dx, po, ma, ag, o, wo, head_sel, tm=512):
    s = dx.shape[0]

    def body(dx_ref, po_ref, ma_ref, ag_ref, o_ref, wo_ref, sel_ref, dmp_ref, do_ref, dag_ref, dt_ref, gwo_ref):
        @pl.when(pl.program_id(0) == 0)
        def _():
            gwo_ref[...] = jnp.zeros_like(gwo_ref)

        dxb = dx_ref[...].astype(BF16)
        dm = lax.dot_general(dxb, wo_ref[...], NT, preferred_element_type=F32)
        dmp_ref[...] = dm[:, 0:POOL_WIDTH].astype(BF16)
        dma = dm[:, POOL_WIDTH:]
        gate = ag_ref[...].astype(F32)
        of = o_ref[...].astype(F32)
        sg = _sigmoid(gate)
        do = dma * (gate * sg)
        do_ref[...] = do.astype(BF16)
        dag_ref[...] = (dma * of * (sg * (1.0 + gate * (1.0 - sg)))).astype(BF16)
        prod = do * of
        hi = prod.astype(BF16)
        lo = (prod - hi.astype(F32)).astype(BF16)
        dt_ref[...] = (lax.dot_general(sel_ref[...], hi, NT, preferred_element_type=F32)
                       + lax.dot_general(sel_ref[...], lo, NT, preferred_element_type=F32))
        gwo_ref[0:POOL_WIDTH, :] += lax.dot_general(po_ref[...], dxb, TN, preferred_element_type=F32)
        gwo_ref[POOL_WIDTH:, :] += lax.dot_general(ma_ref[...], dxb, TN, preferred_element_type=F32)

    act = jax.ShapeDtypeStruct((s, 512), BF16)
    return pl.pallas_call(
        body, name="outproj_bwd", grid=(s // tm,),
        in_specs=[_row_spec(tm, D_MODEL)] + [_row_spec(tm, 512)] * 4
                 + [_full_spec((D_MODEL, D_MODEL)), _full_spec((8, ATTN_WIDTH))],
        out_specs=[_row_spec(tm, 512)] * 3 + [pl.BlockSpec((8, tm), lambda i: (0, i)),
                                              _full_spec((D_MODEL, D_MODEL))],
        out_shape=[act, act, act, jax.ShapeDtypeStruct((8, s), F32),
                   jax.ShapeDtypeStruct((D_MODEL, D_MODEL), F32)],
        compiler_params=_params(("arbitrary",)),
    )(dx, po, ma, ag, o, wo, head_sel)


def _attn_bwd(q, k, v, do, c, lse, dt, carry=None, tq=256, chunk=512):
    s = q.shape[0]
    tk = 2 * tq
    nq = s // tq
    n_carry = 0 if carry is None else len(carry)

    def body(*refs):
        q_ref, k_ref, v_ref, do_ref, c_ref, lse_ref, dt_ref = refs[:7]
        carry_src = refs[7:7 + n_carry]
        dq_ref, dk_ref, dv_ref, dcs_ref, drs_ref = refs[7 + n_carry:12 + n_carry]
        carry_dst = refs[12 + n_carry:12 + 2 * n_carry]
        (qxt_ref, doxt_ref, qm_ref, dox_ref, dqt0_ref, dqt1_ref, dk0_ref, dk1_ref, dvx_ref,
         kx_ref, vx_ref, ktm_ref, sa_ref, sb_ref) = refs[12 + 2 * n_carry:26 + 2 * n_carry]
        carry_sems = refs[26 + 2 * n_carry:]
        carry_big = tuple(zip(carry_src, carry_dst))
        j = pl.program_id(0)
        b = pl.program_id(1)
        dqts = (dqt0_ref, dqt1_ref)
        dks = (dk0_ref, dk1_ref)

        if n_carry:
            @pl.when((j == 0) & (b == 0))
            def _():
                _scatter_start(carry_big, None, carry_sems)

        @pl.when(b == 0)
        def _():
            def prep(ch, carry):
                rows = pl.ds(pl.multiple_of(ch * chunk, chunk), chunk)
                qq = q_ref[rows, :].astype(F32)
                dd = do_ref[rows, :].astype(F32)
                qt = jnp.transpose(qq)
                ddt = jnp.transpose(dd)
                lane = lax.broadcasted_iota(jnp.int32, (chunk, PAIR), 1)
                sub = lax.broadcasted_iota(jnp.int32, (PAIR, chunk), 0)
                for h in range(2):
                    sp = _spare(h)
                    qm_ref[h, rows, :] = jnp.where(_own(lane, h), qq, jnp.where(lane == sp, 1.0, 0.0)).astype(BF16)
                    dox_ref[h, rows, :] = jnp.where(_own(lane, h), dd, 0.0).astype(BF16)
                    qx = jnp.where(_own(sub, h), qt, jnp.where((sub >= sp) & (sub < sp + 3), 1.0, 0.0))
                    for t, part in enumerate(_split3(-lse_ref[pl.ds(h, 1), rows])):
                        qx = jnp.where(sub == sp + 3 + t, part, qx)
                    qxt_ref[h, :, rows] = qx.astype(BF16)
                    dx = jnp.where(_own(sub, h), ddt, 0.0)
                    for t, part in enumerate(_split3(-dt_ref[pl.ds(h, 1), rows])):
                        dx = jnp.where(sub == sp + t, part, dx)
                    doxt_ref[h, :, rows] = dx.astype(BF16)
                return carry

            lax.fori_loop(0, s // chunk, prep, 0)
            for ref in dqts:
                ref[...] = jnp.zeros_like(ref)

        kk = k_ref[...].astype(F32)
        vv = v_ref[...].astype(F32)
        cc = c_ref[...]
        kt = jnp.transpose(kk)
        lane = lax.broadcasted_iota(jnp.int32, (tk, PAIR), 1)
        sub = lax.broadcasted_iota(jnp.int32, (PAIR, tk), 0)
        for h in range(2):
            sp = _spare(h)
            ccol = jnp.sum(jnp.where(lane == 2 * j + h, cc, 0.0), axis=1, keepdims=True)
            kx = jnp.where(_own(lane, h), kk, jnp.where((lane >= sp + 3) & (lane < sp + 6), 1.0, 0.0))
            for t, part in enumerate(_split3(-ccol)):
                kx = jnp.where(lane == sp + t, part, kx)
            kx_ref[h] = kx.astype(BF16)
            vx_ref[h] = jnp.where(_own(lane, h), vv, jnp.where((lane >= sp) & (lane < sp + 3), 1.0, 0.0)).astype(BF16)
            ktm_ref[h] = jnp.where(_own(sub, h), kt, jnp.where(sub == sp, 1.0, 0.0)).astype(BF16)
        for ref in dks:
            ref[...] = jnp.zeros_like(ref)
        dvx_ref[...] = jnp.zeros_like(dvx_ref)

        def cols_of(i):
            return pl.ds(pl.multiple_of(i * tq, tq), tq)

        def scores(i, dst):
            cols = cols_of(jnp.minimum(i, nq - 1))
            for h in range(2):
                dst[h] = jnp.dot(kx_ref[h], qxt_ref[h, :, cols], preferred_element_type=F32)
                dst[2 + h] = jnp.dot(vx_ref[h], doxt_ref[h, :, cols], preferred_element_type=F32)

        def consume(i, src, masked):
            cols = cols_of(i)
            for h in range(2):
                arg = src[h]
                if masked:
                    kpos = b * tk + lax.broadcasted_iota(jnp.int32, (tk, tq), 0)
                    qpos = i * tq + lax.broadcasted_iota(jnp.int32, (tk, tq), 1)
                    arg = jnp.where(qpos >= kpos, arg, NEG)
                pt = jnp.exp(arg)
                dst = (pt * src[2 + h]).astype(BF16)
                dvx_ref[...] += jnp.dot(pt.astype(BF16), dox_ref[h, cols, :], preferred_element_type=F32)
                dks[h][...] += jnp.dot(dst, qm_ref[h, cols, :], preferred_element_type=F32)
                dqts[h][:, cols] += jnp.dot(ktm_ref[h], dst, preferred_element_type=F32)

        i0 = 2 * b

        def two_blocks(t, masked):
            i = i0 + 2 * t
            scores(i + 1, sb_ref)
            consume(i, sa_ref, masked)
            scores(i + 2, sa_ref)
            consume(i + 1, sb_ref, masked)

        def loop_body(t, carry):
            two_blocks(t, False)
            return carry

        scores(i0, sa_ref)
        two_blocks(0, True)
        lax.fori_loop(1, (nq - i0) // 2, loop_body, 0)

        dv_ref[...] = dvx_ref[...].astype(BF16)
        dk_ref[...] = jnp.where(lane < HEAD_DIM, dk0_ref[...], dk1_ref[...]).astype(BF16)
        dcs_ref[...] = jnp.where(lane == _spare(0), dk0_ref[...], jnp.where(lane == _spare(1), dk1_ref[...], 0.0))

        @pl.when(b == s // tk - 1)
        def _():
            sub8 = lax.broadcasted_iota(jnp.int32, (8, s), 0)
            drs_ref[...] = jnp.where(sub8 == 0, dqt0_ref[_spare(0):_spare(0) + 1, :],
                                     jnp.where(sub8 == 1, dqt1_ref[_spare(1):_spare(1) + 1, :], 0.0))
            sub_c = lax.broadcasted_iota(jnp.int32, (PAIR, chunk), 0)
            for ch in range(s // chunk):
                rows = pl.ds(ch * chunk, chunk)
                both = jnp.where(sub_c < HEAD_DIM, dqt0_ref[:, rows], dqt1_ref[:, rows])
                dq_ref[rows, :] = (jnp.transpose(both) * Q_SCALE).astype(BF16)

        if n_carry:
            @pl.when((j == N_PAIRS - 1) & (b == s // tk - 1))
            def _():
                _scatter_finish(carry_big, None, carry_sems)

    act = jax.ShapeDtypeStruct((s, ATTN_WIDTH), BF16)
    pair_rows = pl.BlockSpec((s, PAIR), lambda j, b: (0, j))
    pair_blk = pl.BlockSpec((tk, PAIR), lambda j, b: (b, j))
    stat = pl.BlockSpec((None, 2, s), lambda j, b: (j, 0, 0))
    carried = [] if carry is None else list(carry)
    carry_sem = [pltpu.SemaphoreType.DMA((SCATTER_SEMS,))] * 2 if n_carry else []
    return pl.pallas_call(
        body, name="attn_bwd_scatter" if n_carry else "attn_bwd", grid=(N_PAIRS, s // tk),
        in_specs=[pair_rows, pair_blk, pair_blk, pair_rows,
                  pl.BlockSpec((tk, 128), lambda j, b: (b, 0)),
                  pl.BlockSpec((None, 8, s), lambda j, b: (j, 0, 0)), stat] + [ANY] * n_carry,
        out_specs=[pair_rows, pair_blk, pair_blk, pair_blk,
                   pl.BlockSpec((None, 8, s), lambda j, b: (j, 0, 0))] + [ANY] * n_carry,
        out_shape=[act, act, act, jax.ShapeDtypeStruct((s, N_PAIRS * 128), F32),
                   jax.ShapeDtypeStruct((N_PAIRS, 8, s), F32)] + (_scatter_shapes(*carried) if n_carry else []),
        scratch_shapes=[pltpu.VMEM((2, PAIR, s), BF16), pltpu.VMEM((2, PAIR, s), BF16),
                        pltpu.VMEM((2, s, PAIR), BF16), pltpu.VMEM((2, s, PAIR), BF16),
                        pltpu.VMEM((PAIR, s), F32), pltpu.VMEM((PAIR, s), F32),
                        pltpu.VMEM((tk, PAIR), F32), pltpu.VMEM((tk, PAIR), F32), pltpu.VMEM((tk, PAIR), F32),
                        pltpu.VMEM((2, tk, PAIR), BF16), pltpu.VMEM((2, tk, PAIR), BF16),
                        pltpu.VMEM((2, PAIR, tk), BF16),
                        pltpu.VMEM((4, tk, tq), F32), pltpu.VMEM((4, tk, tq), F32)] + carry_sem,
        compiler_params=_params(("arbitrary", "arbitrary")),
    )(q, k, v, do, c, lse, dt, *carried)


def _forget_bwd(dcs, drs, lf, tm=256):
    s = lf.shape[0]
    n = s // tm

    def body(dcs_ref, drs_ref, lf_ref, dpf_ref, dfb_ref, carry_ref):
        @pl.when(pl.program_id(0) == 0)
        def _():
            carry_ref[...] = jnp.zeros_like(carry_ref)
            dfb_ref[...] = jnp.zeros_like(dfb_ref)

        lane = lax.broadcasted_iota(jnp.int32, (tm, 128), 1)
        dc = drs_ref[...]
        for pj in range(N_PAIRS):
            blk = dcs_ref[:, pj * 128:(pj + 1) * 128]
            for h in range(2):
                head = 2 * pj + h
                moved = pltpu.roll(blk, (head - _spare(h)) % 128, 1) if head != _spare(h) else blk
                dc = dc - jnp.where(lane == head, moved, 0.0)
        dlf = _scan_rows(dc, reverse=True) + carry_ref[0:1, :]
        carry_ref[...] = dlf[0:8, :]
        dz = jnp.where(lane < 8, dlf * (1.0 - jnp.exp(lf_ref[...])), 0.0)
        dpf_ref[...] = dz.astype(BF16)
        dfb_ref[...] += jnp.sum(dz, axis=0, keepdims=True)

    return pl.pallas_call(
        body, name="forget_bwd", grid=(n,),
        in_specs=[pl.BlockSpec((tm, N_PAIRS * 128), lambda i: (n - 1 - i, 0)),
                  pl.BlockSpec((tm, 128), lambda i: (n - 1 - i, 0)),
                  pl.BlockSpec((tm, 128), lambda i: (n - 1 - i, 0))],
        out_specs=[pl.BlockSpec((tm, 128), lambda i: (n - 1 - i, 0)), _full_spec((1, 128))],
        out_shape=[jax.ShapeDtypeStruct((s, 128), BF16), jax.ShapeDtypeStruct((1, 128), F32)],
        scratch_shapes=[pltpu.VMEM((8, 128), F32)],
        compiler_params=_params(("arbitrary",)),
    )(dcs, drs, lf)


def _pool_bwd(pu, pg, dmp, pw, ps, tm=512):
    s = pu.shape[0]
    hb = tm // HALO
    n = s // tm
    last_halo = s // HALO - 1

    def body(pu_ref, halo_ref, pg_ref, dmp_ref, pgn_ref, dmpn_ref, pw_ref, ps_ref,
             dpu_ref, dpg_ref, gpw_ref, gps_ref):
        i = pl.program_id(0)

        @pl.when(i == 0)
        def _():
            gpw_ref[...] = jnp.zeros_like(gpw_ref)
            gps_ref[...] = jnp.zeros_like(gps_ref)

        u = pu_ref[...].astype(F32)
        halo = jnp.where(i == 0, 0.0, halo_ref[...].astype(F32))
        d = _pool_delta(u, halo, i * tm)
        cnt = _window_counts(i * tm, tm)
        cnt_next = _window_counts((i + 1) * tm, HALO)
        e_parts, dd_parts = [], []
        for gi in range(len(POOL_WINDOWS)):
            cols = slice(gi * GROUP_DIM, (gi + 1) * GROUP_DIM)
            wg = pw_ref[gi]
            scale = ps_ref[:, cols]
            db = d[gi].astype(BF16)
            z = jnp.dot(db, wg, preferred_element_type=F32)
            gate = pg_ref[:, cols].astype(F32)
            sg = _sigmoid(gate)
            dm = dmp_ref[:, cols].astype(F32)
            dy = dm * (gate * sg)
            dpg_ref[:, cols] = (dm * (z * scale) * (sg * (1.0 + gate * (1.0 - sg)))).astype(BF16)
            gps_ref[:, cols] += jnp.sum(dy * z, axis=0, keepdims=True)
            dz = (dy * scale).astype(BF16)
            gpw_ref[gi] += lax.dot_general(db, dz, TN, preferred_element_type=F32)
            dd = lax.dot_general(dz, wg, NT, preferred_element_type=F32)
            gate_n = pgn_ref[:, cols].astype(F32)
            dz_n = (dmpn_ref[:, cols].astype(F32) * (gate_n * _sigmoid(gate_n)) * scale).astype(BF16)
            dd_n = lax.dot_general(dz_n, wg, NT, preferred_element_type=F32)
            dd_n = jnp.where(i == n - 1, 0.0, dd_n)
            dd_parts.append(dd)
            e_parts.append(jnp.concatenate([dd / cnt[gi], dd_n / cnt_next[gi]], axis=0))
        lead = _window_sums(jnp.concatenate(e_parts, axis=1), forward=True)
        for gi in range(len(POOL_WINDOWS)):
            cols = slice(gi * GROUP_DIM, (gi + 1) * GROUP_DIM)
            dpu_ref[:, cols] = (lead[gi][0:tm, :] - dd_parts[gi]).astype(BF16)

    act = jax.ShapeDtypeStruct((s, 512), BF16)
    prev_halo = pl.BlockSpec((HALO, 512), lambda i: (jnp.maximum(i * hb - 1, 0), 0))
    next_halo = pl.BlockSpec((HALO, 512), lambda i: (jnp.minimum((i + 1) * hb, last_halo), 0))
    return pl.pallas_call(
        body, name="pool_bwd", grid=(n,),
        in_specs=[_row_spec(tm, 512), prev_halo, _row_spec(tm, 512), _row_spec(tm, 512), next_halo, next_halo,
                  _full_spec((4, GROUP_DIM, GROUP_DIM)), _full_spec((1, 512))],
        out_specs=[_row_spec(tm, 512), _row_spec(tm, 512), _full_spec((4, GROUP_DIM, GROUP_DIM)),
                   _full_spec((1, 512))],
        out_shape=[act, act, jax.ShapeDtypeStruct((4, GROUP_DIM, GROUP_DIM), F32),
                   jax.ShapeDtypeStruct((1, 512), F32)],
        compiler_params=_params(("arbitrary",)),
    )(pu, pu, pg, dmp, pg, dmp, pw, ps)


def _inproj_bwd_x(dsec, w, x, g, dxo, exchange=None, scatter=None, tm=512):
    s = x.shape[0]
    n_steps = s // tm
    carried = list(exchange or scatter or ())
    n_carry = len(carried)

    def body(*refs):
        d_refs = refs[0:7]
        w_ref, x_ref, g_ref, dxo_ref = refs[7:11]
        carry_src = refs[11:11 + n_carry]
        dxi_ref, dg_ref = refs[11 + n_carry:13 + n_carry]
        carry_dst = refs[13 + n_carry:13 + 2 * n_carry]
        sems = refs[13 + 2 * n_carry:]
        step = pl.program_id(0)

        def comm(phase):
            if exchange is not None:
                (_exchange_start if phase == 0 else _exchange_finish)(carry_src, carry_dst, None, sems)
            else:
                big = tuple(zip(carry_src[:2], carry_dst[:2]))
                (_scatter_start if phase == 0 else _scatter_finish)(big, (carry_src[2], carry_dst[2]), sems)

        @pl.when(step == 0)
        def _():
            dg_ref[...] = jnp.zeros_like(dg_ref)
            if n_carry:
                comm(0)

        dh = None
        for d_ref, (off, ncol) in zip(d_refs, SECTIONS):
            t = lax.dot_general(d_ref[...], w_ref[:, off:off + ncol], NT, preferred_element_type=F32)
            dh = t if dh is None else dh + t
        xf = x_ref[...]
        r = lax.rsqrt(jnp.mean(xf * xf, axis=-1, keepdims=True) + RMS_EPS)
        xh = xf * r
        u = dh * g_ref[...]
        dxi_ref[...] = dxo_ref[...] + r * (u - xh * jnp.mean(xh * u, axis=-1, keepdims=True))
        dg_ref[...] += jnp.sum(dh * xh, axis=0, keepdims=True)

        if n_carry:
            @pl.when(step == n_steps - 1)
            def _():
                comm(1)

    if exchange is not None:
        name, carry_shapes, n_sems = "inproj_bwd_x_exchange", _exchange_shapes(*exchange), 3
    elif scatter is not None:
        name, n_sems = "inproj_bwd_x_scatter", SCATTER_SEMS
        carry_shapes = _scatter_shapes(*scatter[:2]) + [jax.ShapeDtypeStruct((3, SMALL_HALF, 128), F32)]
    else:
        name, carry_shapes, n_sems = "inproj_bwd_x", [], 0
    return pl.pallas_call(
        body, name=name, grid=(n_steps,),
        in_specs=[_row_spec(tm, ncol) for _, ncol in SECTIONS]
                 + [_full_spec((D_MODEL, IN_COLS_PAD)), _row_spec(tm, D_MODEL), _full_spec((1, D_MODEL)),
                    _row_spec(tm, D_MODEL)] + [ANY] * n_carry,
        out_specs=[_row_spec(tm, D_MODEL), _full_spec((1, D_MODEL))] + [ANY] * n_carry,
        out_shape=[jax.ShapeDtypeStruct((s, D_MODEL), F32), jax.ShapeDtypeStruct((1, D_MODEL), F32)] + carry_shapes,
        scratch_shapes=[pltpu.SemaphoreType.DMA((n_sems,))] * 2 if n_carry else [],
        compiler_params=_params(("arbitrary",)),
    )(*dsec, w, x, g, dxo, *carried)


def _inproj_bwd_w(h, dsec, tm=512):
    s = h.shape[0]

    def body(*refs):
        h_ref = refs[0]
        d_refs = refs[1:8]
        gw_ref = refs[8]

        @pl.when(pl.program_id(0) == 0)
        def _():
            gw_ref[...] = jnp.zeros_like(gw_ref)

        hh = h_ref[...]
        for d_ref, (off, ncol) in zip(d_refs, SECTIONS):
            gw_ref[:, off:off + ncol] += lax.dot_general(hh, d_ref[...], TN, preferred_element_type=F32)

    return pl.pallas_call(
        body, name="inproj_bwd_w", grid=(s // tm,),
        in_specs=[_row_spec(tm, D_MODEL)] + [_row_spec(tm, ncol) for _, ncol in SECTIONS],
        out_specs=_full_spec((D_MODEL, IN_COLS_PAD)),
        out_shape=jax.ShapeDtypeStruct((D_MODEL, IN_COLS_PAD), F32),
        compiler_params=_params(("arbitrary",), vmem_mb=56),
    )(h, *dsec)


def _elementwise(fn, name, n_out, arrays, tm):
    rows, cols = arrays[0].shape
    n_in = len(arrays)

    def body(*refs):
        outs = fn(*[r[...] for r in refs[:n_in]])
        for r, val in zip(refs[n_in:], outs):
            r[...] = val

    spec = _row_spec(tm, cols)
    shape = jax.ShapeDtypeStruct((rows, cols), F32)
    return pl.pallas_call(
        body, name=name, grid=(rows // tm,),
        in_specs=[spec] * n_in, out_specs=[spec] * n_out, out_shape=[shape] * n_out,
        compiler_params=_params(("parallel",)),
    )(*arrays)


def _add2(a, b, name, tm):
    return _elementwise(lambda p, q: (p + q,), name, 1, [a, b], tm)[0]


def _chip_sum(place, g, r, name, tm):
    n_layers, _, rows, cols = g.shape
    half = rows // 2
    nb = half // tm

    def body(place_ref, g_ref, r_ref, o_ref):
        o_ref[...] = (g_ref[...] + r_ref[...]).astype(BF16)

    blk = (None, None, tm, cols)
    return pl.pallas_call(
        body, name=name,
        grid_spec=pltpu.PrefetchScalarGridSpec(
            num_scalar_prefetch=1, grid=(n_layers, N_CHIPS, nb),
            in_specs=[pl.BlockSpec(blk, lambda l, k, i, p: (l, k, p[1] * nb + i, 0)),
                      pl.BlockSpec(blk, lambda l, k, i, p: (l, k, i, 0))],
            out_specs=pl.BlockSpec(blk, lambda l, k, i, p: (l, k, i, 0))),
        out_shape=jax.ShapeDtypeStruct((n_layers, N_CHIPS, half, cols), BF16),
        compiler_params=_params(("parallel", "parallel", "parallel")),
    )(place, g, r)


def _mesh_sum(place, g, r, got, into, layer, name, tm):
    _, _, rows, cols = g.shape
    half = rows // 2
    nb = half // tm

    def body(place_ref, g_ref, r_ref, got_ref, into_ref, o_ref):
        own = g_ref[...] + r_ref[...]
        o_ref[...] = (own + got_ref[0].astype(F32)) + (got_ref[1].astype(F32) + got_ref[2].astype(F32))

    blk = (None, None, tm, cols)
    return pl.pallas_call(
        body, name=name,
        grid_spec=pltpu.PrefetchScalarGridSpec(
            num_scalar_prefetch=1, grid=(nb,),
            in_specs=[pl.BlockSpec(blk, lambda i, p: (0, p[0], p[1] * nb + i, 0)),
                      pl.BlockSpec(blk, lambda i, p: (0, p[0], i, 0)),
                      pl.BlockSpec((None, 3, tm, cols), lambda i, p: (0, 0, i, 0)),
                      ANY],
            out_specs=pl.BlockSpec((None, tm, cols), lambda i, p: (layer, p[1] * nb + i, 0))),
        out_shape=jax.ShapeDtypeStruct(into.shape, F32),
        input_output_aliases={4: 0},
        compiler_params=_params(("parallel",)),
    )(place, g, r, got, into)


def _mesh_sum_small(place, ps, got, tm):
    nb = SMALL_HALF // tm

    def body(place_ref, ps_ref, got_ref, o_ref):
        o_ref[...] = (ps_ref[...] + got_ref[0]) + (got_ref[1] + got_ref[2])

    return pl.pallas_call(
        body, name="mesh_sum_small",
        grid_spec=pltpu.PrefetchScalarGridSpec(
            num_scalar_prefetch=1, grid=(nb,),
            in_specs=[pl.BlockSpec((tm, 128), lambda i, p: (p[1] * nb + i, 0)),
                      pl.BlockSpec((3, tm, 128), lambda i, p: (0, i, 0))],
            out_specs=pl.BlockSpec((tm, 128), lambda i, p: (p[1] * nb + i, 0))),
        out_shape=jax.ShapeDtypeStruct((SMALL_ROWS, 128), F32),
        compiler_params=_params(("parallel",)),
    )(place, ps, got)


def _adamw(g, w, m, v, name, tm):
    def fn(g, w, m, v):
        m = ADAM_B1 * m + (1.0 - ADAM_B1) * g
        v = ADAM_B2 * v + (1.0 - ADAM_B2) * (g * g)
        m_hat = m / (1.0 - ADAM_B1 ** ADAM_STEP)
        v_hat = v / (1.0 - ADAM_B2 ** ADAM_STEP)
        delta = -ADAM_LR * (m_hat / (jnp.sqrt(v_hat) + ADAM_EPS) + ADAM_WD * w)
        return g, delta, m, v

    return _elementwise(fn, name, 4, [g, w, m, v], tm)


def _position():
    x, y, c = lax.axis_index("x"), lax.axis_index("y"), lax.axis_index("c")
    other_chips = [(x, 1 - y), (1 - x, y), (1 - x, 1 - y)]
    return x, y, c, other_chips


def _remote(src, dst, sems, k, to):
    send_sems, recv_sems = sems
    return pltpu.make_async_remote_copy(src_ref=src, dst_ref=dst, send_sem=send_sems.at[k],
                                        recv_sem=recv_sems.at[k], device_id=to, device_id_type=MESH)


def _comm_call(body, name, n_in, out_shape, n_remote, aliases=None):
    return pl.pallas_call(
        body, name=name, in_specs=[ANY] * n_in, out_specs=[ANY] * len(out_shape), out_shape=out_shape,
        scratch_shapes=[pltpu.SemaphoreType.DMA((n_remote,)), pltpu.SemaphoreType.DMA((n_remote,))],
        input_output_aliases=aliases or {},
    )


def _halves(ref_rows, c):
    half = ref_rows // 2
    return pl.ds(half * c, half), pl.ds(half * (1 - c), half)


def _gather_weights(wi, wo):
    def body(wi_ref, wo_ref, gi_ref, go_ref, send_sems, recv_sems):
        _gather_start((wi_ref, wo_ref), (gi_ref, go_ref), (send_sems, recv_sems))
        _gather_finish((wi_ref, wo_ref), (gi_ref, go_ref), (send_sems, recv_sems))

    return _comm_call(body, "gather_weights", 2, _gather_shapes(wi, wo), GATHER_SEMS)(wi, wo)


GATHER_SEMS = 12
SCATTER_SEMS = 9


def _gather_shapes(wi, wo):
    return [jax.ShapeDtypeStruct((a.shape[0], 3) + a.shape[1:], a.dtype) for a in (wi, wo)]


def _gather_sends(srcs, dsts, sems):
    x, y, c, chips = _position()
    out = []
    for k, (cx, cy) in enumerate(chips):
        for a, (src, dst) in enumerate(zip(srcs, dsts)):
            mine, _ = _halves(src.shape[1], c)
            out.append(_remote(src.at[:, mine], dst.at[:, k, mine], sems, len(srcs) * k + a, (cx, cy, c)))
    return out


def _gather_start(srcs, dsts, sems):
    for cp in _gather_sends(srcs, dsts, sems):
        cp.start()


def _gather_finish(srcs, dsts, sems):
    x, y, c, _ = _position()
    n = len(srcs)
    forwards = []
    for k in range(3):
        for a, dst in enumerate(dsts):
            mine, _ = _halves(dst.shape[2], c)
            blk = dst.at[:, k, mine]
            _remote(blk, blk, sems, n * k + a, (x, y, c)).wait_recv()
            forwards.append(_remote(blk, blk, sems, 3 * n + n * k + a, (x, y, 1 - c)))
            forwards[-1].start()
    for k in range(3):
        for a, dst in enumerate(dsts):
            _, other = _halves(dst.shape[2], c)
            blk = dst.at[:, k, other]
            _remote(blk, blk, sems, 3 * n + n * k + a, (x, y, c)).wait_recv()
    for cp in _gather_sends(srcs, dsts, sems) + forwards:
        cp.wait_send()


def _scatter_sends(big, small, sems):
    x, y, c, chips = _position()
    n = len(big) + (small is not None)
    out = []
    for k, (cx, cy) in enumerate(chips):
        ck = 2 * cx + cy
        for a, (src, dst) in enumerate(big):
            out.append(_remote(src.at[:, ck], dst.at[:, k], sems, n * k + a, (cx, cy, c)))
        if small is not None:
            half, _ = _halves(SMALL_ROWS, c)
            out.append(_remote(small[0].at[half], small[1].at[k], sems, n * k + n - 1, (cx, cy, c)))
    return out


def _scatter_start(big, small, sems):
    for cp in _scatter_sends(big, small, sems):
        cp.start()


def _scatter_finish(big, small, sems):
    x, y, c, _ = _position()
    n = len(big) + (small is not None)
    for k in range(3):
        landing = [dst.at[:, k] for _, dst in big] + ([small[1].at[k]] if small is not None else [])
        for a, blk in enumerate(landing):
            _remote(blk, blk, sems, n * k + a, (x, y, c)).wait_recv()
    for cp in _scatter_sends(big, small, sems):
        cp.wait_send()


def _exchange_shapes(gi, go):
    return [jax.ShapeDtypeStruct(a.shape[:2] + (a.shape[2] // 2, a.shape[3]), F32) for a in (gi, go)]


def _exchange_copies(srcs, dsts, small, sems):
    x, y, c, _ = _position()
    cps = [_remote(src.at[:, :, _halves(src.shape[2], c)[1]], dst, sems, a, (x, y, 1 - c))
           for a, (src, dst) in enumerate(zip(srcs, dsts))]
    if small is not None:
        cps.append(_remote(small[0], small[1], sems, len(srcs), (x, y, 1 - c)))
    return cps


def _exchange_start(srcs, dsts, small, sems):
    for cp in _exchange_copies(srcs, dsts, small, sems):
        cp.start()


def _exchange_finish(srcs, dsts, small, sems):
    for cp in _exchange_copies(srcs, dsts, small, sems):
        cp.wait()


def _exchange_sibling(gi, go, sm):
    def body(gi_ref, go_ref, sm_ref, ri_ref, ro_ref, rs_ref, send_sems, recv_sems):
        args = ((gi_ref, go_ref), (ri_ref, ro_ref), (sm_ref, rs_ref), (send_sems, recv_sems))
        _exchange_start(*args)
        _exchange_finish(*args)

    out_shape = _exchange_shapes(gi, go) + [jax.ShapeDtypeStruct(sm.shape, F32)]
    return _comm_call(body, "exchange_sibling", 3, out_shape, 3)(gi, go, sm)


def _scatter_shapes(pi, po):
    return [jax.ShapeDtypeStruct((a.shape[0], 3) + a.shape[2:], a.dtype) for a in (pi, po)]


def _scatter_chips(pi, po, ps):
    def body(pi_ref, po_ref, ps_ref, ri_ref, ro_ref, rs_ref, send_sems, recv_sems):
        big, small, sems = ((pi_ref, ri_ref), (po_ref, ro_ref)), (ps_ref, rs_ref), (send_sems, recv_sems)
        _scatter_start(big, small, sems)
        _scatter_finish(big, small, sems)

    out_shape = _scatter_shapes(pi, po) + [jax.ShapeDtypeStruct((3, SMALL_HALF, 128), F32)]
    return _comm_call(body, "scatter_chips", 3, out_shape, SCATTER_SEMS)(pi, po, ps)


def _join_sibling(fi, fo, fs):
    def body(fi_ref, fo_ref, fs_ref, oi_ref, oo_ref, os_ref, send_sems, recv_sems):
        x, y, c, _ = _position()
        sems = (send_sems, recv_sems)
        sent, got = [], []
        for a, (src, dst) in enumerate(((fi_ref, oi_ref), (fo_ref, oo_ref))):
            mine, other = _halves(src.shape[1], c)
            sent.append(_remote(src.at[:, mine], dst.at[:, mine], sems, a, (x, y, 1 - c)))
            got.append(_remote(dst.at[:, other], dst.at[:, other], sems, a, (x, y, c)))
        mine, other = _halves(SMALL_ROWS, c)
        sent.append(_remote(fs_ref.at[mine], os_ref.at[mine], sems, 2, (x, y, 1 - c)))
        got.append(_remote(os_ref.at[other], os_ref.at[other], sems, 2, (x, y, c)))
        for cp in sent:
            cp.start()
        for cp in got:
            cp.wait_recv()
        for cp in sent:
            cp.wait_send()

    out_shape = [jax.ShapeDtypeStruct(a.shape, F32) for a in (fi, fo, fs)]
    return _comm_call(body, "join_sibling", 3, out_shape, 3, aliases={0: 0, 1: 1, 2: 2})(fi, fo, fs)


SMALL_SIZES = (DEPTH * D_MODEL, D_MODEL, DEPTH * POOL_WIDTH, 128, DEPTH * 4 * GROUP_DIM * GROUP_DIM)
LOSS_SLOT = DEPTH * D_MODEL + D_MODEL + DEPTH * POOL_WIDTH + DEPTH * 8


def _pack_small(norm_g, final_g, pool_scale, forget_bias, pool_w, loss=None):
    fb = forget_bias.reshape(-1)
    if loss is not None:
        fb = jnp.concatenate([fb, loss.reshape(1)])
    fb = jnp.pad(fb, (0, 128 - fb.size))
    flat = jnp.concatenate([norm_g.reshape(-1), final_g.reshape(-1), pool_scale.reshape(-1), fb,
                            pool_w.reshape(-1)])
    return jnp.pad(flat, (0, SMALL_ROWS * 128 - flat.size)).reshape(SMALL_ROWS, 128)


def _unpack_small(packed):
    flat = packed.reshape(-1)
    offs = [0]
    for n in SMALL_SIZES:
        offs.append(offs[-1] + n)
    norm_g, final_g, pool_scale, fb, pool_w = [flat[offs[i]:offs[i + 1]] for i in range(5)]
    return (norm_g.reshape(DEPTH, D_MODEL), final_g, pool_scale.reshape(DEPTH, POOL_WIDTH),
            fb[:DEPTH * 8].reshape(DEPTH, 8), pool_w.reshape(DEPTH, 4, GROUP_DIM, GROUP_DIM))


def kernel(x, norm_g, w_in, forget_bias, pool_w, pool_scale, w_out, final_g, loss_target, m_norm_g, m_w_in, m_forget_bias, m_pool_w, m_pool_scale, m_w_out, m_final_g, v_norm_g, v_w_in, v_forget_bias, v_pool_w, v_pool_scale, v_w_out, v_final_g):
    x0 = x[0]
    tgt = loss_target[0]
    s = x0.shape[0]

    me = 2 * lax.axis_index("x") + lax.axis_index("y")
    place = jnp.stack([me, lax.axis_index("c")]).astype(jnp.int32)
    wi_b, wo_b = w_in.astype(BF16), w_out.astype(BF16)

    def chip_piece(own, got, chip):
        rel = jnp.bitwise_xor(chip, me)
        other = lax.dynamic_index_in_dim(got, jnp.maximum(rel - 1, 0), axis=1, keepdims=False)
        return jnp.where(rel == 0, own, other)[0]

    def layer_weights(l, gi, go):
        w_i = jnp.concatenate([chip_piece(wi_b[l:l + 1], gi, k) for k in range(N_CHIPS)], axis=-1)
        w_o = jnp.concatenate([chip_piece(wo_b[l:l + 1], go, k) for k in range(N_CHIPS)], axis=0)
        return jnp.pad(w_i, ((0, 0), (0, IN_COLS_PAD - IN_COLS))), w_o

    pool_w_b = pool_w.astype(BF16)
    fb_pad = jnp.pad(forget_bias, ((0, 0), (0, 128 - forget_bias.shape[1])))
    head_sel = (jnp.arange(ATTN_WIDTH)[None, :] // HEAD_DIM == jnp.arange(8)[:, None]).astype(BF16)

    saved, weights = [], []
    xl = x0
    gathered = _gather_weights(wi_b[0:1], wo_b[0:1])
    for l in range(DEPTH):
        w_i, w_o = layer_weights(l, *gathered)
        weights.append((w_i, w_o))
        pu, pg, q, k, v, ag, lf, c, h = _inproj_fwd(xl, norm_g[l][None], w_i, fb_pad[l][None])
        po = _pool_fwd(pu, pg, pool_w_b[l], pool_scale[l][None])
        if l + 1 < DEPTH:
            ma, o, lse, *gathered = _attn_fwd(q, k, v, c, ag, carry=(wi_b[l + 1:l + 2], wo_b[l + 1:l + 2]))
        else:
            ma, o, lse = _attn_fwd(q, k, v, c, ag)
        saved.append((xl, h, pu, pg, q, k, v, ag, lf, c, po, ma, o, lse))
        xl = _outproj_fwd(xl, po, ma, w_o)

    dx, g_final, loss_part = _loss_head(xl, tgt, final_g[None])

    g_norm, g_fb, g_pw, g_ps = [], [], [], []
    parts = {}
    pending = None
    for l in reversed(range(DEPTH)):
        xl, h, pu, pg, q, k, v, ag, lf, c, po, ma, o, lse = saved[l]
        w_i, w_o = weights[l]
        dmp, do, dag, dt, gwo = _outproj_bwd(dx, po, ma, ag, o, w_o, head_sel)
        if pending is None:
            dq, dk, dv, dcs, drs = _attn_bwd(q, k, v, do, c, lse, dt.reshape(N_PAIRS, 2, s))
        else:
            dq, dk, dv, dcs, drs, qi, qo = _attn_bwd(q, k, v, do, c, lse, dt.reshape(N_PAIRS, 2, s), carry=pending)
            parts[l + 1] += (qi, qo)
        drs = jnp.pad(drs[:, 0:2, :].reshape(8, s).T, ((0, 0), (0, 120)))
        dpf, dfb = _forget_bwd(dcs, drs, lf)
        dpu, dpg, gpw, gps = _pool_bwd(pu, pg, dmp, pool_w_b[l], pool_scale[l][None])
        dsec = (dpu, dpg, dq, dk, dv, dag, dpf)
        gwi = _inproj_bwd_w(h, dsec)
        g_fb.append(dfb[0, :8])
        g_pw.append(gpw)
        g_ps.append(gps[0])
        send_i = jnp.stack([gwi[:, k * SHARD_COLS:(k + 1) * SHARD_COLS] for k in range(N_CHIPS)])[None]
        send_o = gwo.reshape(1, N_CHIPS, SHARD_ROWS, D_MODEL)
        if l > 0:
            dx, gn, ri, ro = _inproj_bwd_x(dsec, w_i, xl, norm_g[l][None], dx, exchange=(send_i, send_o))
            g_norm.append(gn[0])
        else:
            dx, gn = _inproj_bwd_x(dsec, w_i, xl, norm_g[l][None], dx)
            g_norm.append(gn[0])
            g_norm, g_fb, g_pw, g_ps = [t[::-1] for t in (g_norm, g_fb, g_pw, g_ps)]
            send_s = _pack_small(jnp.stack(g_norm), g_final[0], jnp.stack(g_ps), jnp.stack(g_fb), jnp.stack(g_pw),
                                 loss=loss_part[0, 0])
            ri, ro, rs = _exchange_sibling(send_i, send_o, send_s)
        pending = (_chip_sum(place, send_i, ri, "chip_sum_w_in", 256),
                   _chip_sum(place, send_o, ro, "chip_sum_w_out", SHARD_ROWS // 2))
        parts[l] = (send_i, ri, send_o, ro)
    ps_ = _add2(send_s, rs, "chip_sum_small", SMALL_ROWS // 4)
    qi, qo, qs = _scatter_chips(*pending, ps_)
    parts[0] += (qi, qo)
    fi = lax.empty((DEPTH, D_MODEL, SHARD_COLS), F32)
    fo = lax.empty((DEPTH, SHARD_ROWS, D_MODEL), F32)
    for l in range(DEPTH):
        send_i, ri, send_o, ro, qi, qo = parts[l]
        fi = _mesh_sum(place, send_i, ri, qi, fi, l, "mesh_sum_w_in", 256)
        fo = _mesh_sum(place, send_o, ro, qo, fo, l, "mesh_sum_w_out", SHARD_ROWS // 2)
    hs = _mesh_sum_small(place, ps_, qs, SMALL_HALF // 4)
    fi, fo, fs = _join_sibling(fi, fo, hs)

    flat_i = lambda a: a.reshape(-1, SHARD_COLS)
    flat_o = lambda a: a.reshape(-1, D_MODEL)
    out_i = _adamw(flat_i(fi), flat_i(w_in), flat_i(m_w_in), flat_i(v_w_in), "adamw_w_in", 256)
    out_o = _adamw(flat_o(fo), flat_o(w_out), flat_o(m_w_out), flat_o(v_w_out), "adamw_w_out", 256)
    out_s = _adamw(fs, _pack_small(norm_g, final_g, pool_scale, forget_bias, pool_w),
                   _pack_small(m_norm_g, m_final_g, m_pool_scale, m_forget_bias, m_pool_w),
                   _pack_small(v_norm_g, v_final_g, v_pool_scale, v_forget_bias, v_pool_w),
                   "adamw_small", SMALL_ROWS // 4)

    loss = out_s[0].reshape(-1)[LOSS_SLOT]
    groups = []
    for oi, oo, os_ in zip(out_i, out_o, out_s):
        sn, sf, sp, sb, sw = _unpack_small(os_)
        groups.append((sn, oi.reshape(w_in.shape), sb, sw, sp, oo.reshape(w_out.shape), sf))
    return (loss, dx[None]) + tuple(t for grp in groups for t in grp)
```

```python
import jax
import jax.numpy as jnp
from jax import lax
from jax.experimental import pallas as pl
from jax.experimental.pallas import tpu as pltpu

F32 = jnp.float32
BF16 = jnp.bfloat16

D_MODEL = 1024
DEPTH = 4
POOL_WIDTH = 512
ATTN_WIDTH = 512
HEAD_DIM = 64
PAIR = 2 * HEAD_DIM
N_PAIRS = ATTN_WIDTH // PAIR
POOL_WINDOWS = (2, 4, 8, 16)
GROUP_DIM = 128
HALO = 16
IN_COLS = 3080
OFF_F = 3072
IN_COLS_PAD = 3200
SECTIONS = ((0, 512), (512, 512), (1024, 512), (1536, 512), (2048, 512), (2560, 512), (OFF_F, 128))
N_CHIPS = 4
SHARD_COLS = IN_COLS // N_CHIPS
SHARD_ROWS = D_MODEL // N_CHIPS
RMS_EPS = 1e-6
NEG = -1e30
Q_SCALE = 0.125

ADAM_LR = 0.001
ADAM_B1 = 0.9
ADAM_B2 = 0.999
ADAM_EPS = 1e-08
ADAM_WD = 0.01
ADAM_STEP = 10

SMALL_ROWS = 2112
SMALL_HALF = SMALL_ROWS // 2

NT = (((1,), (1,)), ((), ()))
TN = (((0,), (0,)), ((), ()))
MESH = pl.DeviceIdType.MESH
ANY = pl.BlockSpec(memory_space=pl.ANY)


def _params(semantics, vmem_mb=48):
    return pltpu.CompilerParams(dimension_semantics=semantics, vmem_limit_bytes=vmem_mb << 20)


def _row_spec(tm, cols):
    return pl.BlockSpec((tm, cols), lambda i: (i, 0))


def _full_spec(shape):
    return pl.BlockSpec(shape, lambda *_: (0,) * len(shape))


def _sigmoid(x):
    return jax.nn.sigmoid(x)


def _scan_rows(a, reverse=False):
    n = a.shape[0]
    row = lax.broadcasted_iota(jnp.int32, a.shape, 0)
    k = 1
    while k < n:
        if reverse:
            a = a + jnp.where(row < n - k, pltpu.roll(a, n - k, 0), 0.0)
        else:
            a = a + jnp.where(row >= k, pltpu.roll(a, k, 0), 0.0)
        k *= 2
    return a


def _inproj_fwd(x, g, w, fb, tm=512):
    s = x.shape[0]

    def body(x_ref, g_ref, w_ref, fb_ref, pu_ref, pg_ref, q_ref, k_ref, v_ref, ag_ref,
             lf_ref, c_ref, h_ref, carry_ref):
        @pl.when(pl.program_id(0) == 0)
        def _():
            carry_ref[...] = jnp.zeros_like(carry_ref)

        xf = x_ref[...]
        r = lax.rsqrt(jnp.mean(xf * xf, axis=-1, keepdims=True) + RMS_EPS)
        h = (xf * r * g_ref[...]).astype(BF16)
        h_ref[...] = h

        def proj(sec):
            off, n = SECTIONS[sec]
            return jnp.dot(h, w_ref[:, off:off + n], preferred_element_type=F32)

        pu_ref[...] = proj(0).astype(BF16)
        pg_ref[...] = proj(1).astype(BF16)
        q_ref[...] = (proj(2) * Q_SCALE).astype(BF16)
        k_ref[...] = proj(3).astype(BF16)
        v_ref[...] = proj(4).astype(BF16)
        ag_ref[...] = proj(5).astype(BF16)
        z = proj(6) + fb_ref[...]
        lf = jnp.minimum(z, 0.0) - jnp.log(1.0 + jnp.exp(-jnp.abs(z)))
        lf_ref[...] = lf
        c_ref[...] = _scan_rows(lf) + carry_ref[0:1, :]
        carry_ref[0:1, :] = c_ref[tm - 1:tm, :]

    act = jax.ShapeDtypeStruct((s, 512), BF16)
    return pl.pallas_call(
        body, name="inproj_fwd", grid=(s // tm,),
        in_specs=[_row_spec(tm, D_MODEL), _full_spec((1, D_MODEL)), _full_spec((D_MODEL, IN_COLS_PAD)),
                  _full_spec((1, 128))],
        out_specs=[_row_spec(tm, 512)] * 6 + [_row_spec(tm, 128), _row_spec(tm, 128), _row_spec(tm, D_MODEL)],
        out_shape=[act] * 6 + [jax.ShapeDtypeStruct((s, 128), F32), jax.ShapeDtypeStruct((s, 128), F32),
                               jax.ShapeDtypeStruct((s, D_MODEL), BF16)],
        scratch_shapes=[pltpu.VMEM((8, 128), F32)],
        compiler_params=_params(("arbitrary",)),
    )(x, g, w, fb)


def _window_sums(ext, forward):
    n = ext.shape[0]
    outs = []
    for gi, w in enumerate(POOL_WINDOWS):
        a = ext[:, gi * GROUP_DIM:(gi + 1) * GROUP_DIM]
        k = 1
        while k < w:
            a = a + pltpu.roll(a, (n - k) if forward else k, 0)
            k *= 2
        outs.append(a)
    return outs


def _window_counts(row0, tm):
    t = row0 + lax.broadcasted_iota(jnp.int32, (tm, GROUP_DIM), 0)
    return [jnp.minimum(t + 1, w).astype(F32) for w in POOL_WINDOWS]


def _pool_delta(u, halo, row0):
    tm = u.shape[0]
    sums = _window_sums(jnp.concatenate([halo, u], axis=0), forward=False)
    cnt = _window_counts(row0, tm)
    return [sums[gi][HALO:, :] / cnt[gi] - u[:, gi * GROUP_DIM:(gi + 1) * GROUP_DIM]
            for gi in range(len(POOL_WINDOWS))]


def _pool_fwd(pu, pg, pw, ps, tm=512):
    s = pu.shape[0]
    hb = tm // HALO

    def body(pu_ref, halo_ref, pg_ref, pw_ref, ps_ref, po_ref):
        i = pl.program_id(0)
        u = pu_ref[...].astype(F32)
        halo = jnp.where(i == 0, 0.0, halo_ref[...].astype(F32))
        d = _pool_delta(u, halo, i * tm)
        for gi in range(len(POOL_WINDOWS)):
            cols = slice(gi * GROUP_DIM, (gi + 1) * GROUP_DIM)
            z = jnp.dot(d[gi].astype(BF16), pw_ref[gi], preferred_element_type=F32)
            gate = pg_ref[:, cols].astype(F32)
            po_ref[:, cols] = (z * ps_ref[:, cols] * (gate * _sigmoid(gate))).astype(BF16)

    return pl.pallas_call(
        body, name="pool_fwd", grid=(s // tm,),
        in_specs=[_row_spec(tm, 512),
                  pl.BlockSpec((HALO, 512), lambda i: (jnp.maximum(i * hb - 1, 0), 0)),
                  _row_spec(tm, 512), _full_spec((4, GROUP_DIM, GROUP_DIM)), _full_spec((1, 512))],
        out_specs=_row_spec(tm, 512),
        out_shape=jax.ShapeDtypeStruct((s, 512), BF16),
        compiler_params=_params(("parallel",)),
    )(pu, pu, pg, pw, ps)


def _split3(x):
    hi = x.astype(BF16).astype(F32)
    r1 = x - hi
    lo = r1.astype(BF16).astype(F32)
    return hi, lo, (r1 - lo).astype(BF16).astype(F32)


def _own(idx, h):
    return (idx < HEAD_DIM) if h == 0 else (idx >= HEAD_DIM)


def _spare(h):
    return HEAD_DIM if h == 0 else 0


def _attn_fwd(q, k, v, c, ag, carry=None, tq=512, tk=512, chunk=512):
    s = q.shape[0]
    n_carry = 0 if carry is None else len(carry)

    def body(*refs):
        q_ref, k_ref, v_ref, c_ref, ag_ref = refs[:5]
        carry_src = refs[5:5 + n_carry]
        ma_ref, o_ref, lse_ref = refs[5 + n_carry:8 + n_carry]
        carry_dst = refs[8 + n_carry:8 + 2 * n_carry]
        kx_ref, vt_ref, acc0_ref, acc1_ref, sta_ref, stb_ref = refs[8 + 2 * n_carry:14 + 2 * n_carry]
        carry_sems = refs[14 + 2 * n_carry:]
        j = pl.program_id(0)
        i = pl.program_id(1)

        if n_carry:
            @pl.when((j == 0) & (i == 0))
            def _():
                _gather_start(carry_src, carry_dst, carry_sems[:2])
                for cp in _gather_own(carry_src, carry_dst, carry_sems[2]):
                    cp.start()

        @pl.when(i == 0)
        def _():
            def prep(ch, carry):
                rows = pl.ds(pl.multiple_of(ch * chunk, chunk), chunk)
                kk = k_ref[rows, :].astype(F32)
                cc = c_ref[rows, :]
                vt = jnp.transpose(v_ref[rows, :].astype(F32))
                lane = lax.broadcasted_iota(jnp.int32, (chunk, PAIR), 1)
                sub = lax.broadcasted_iota(jnp.int32, (PAIR, chunk), 0)
                for h in range(2):
                    ccol = jnp.sum(jnp.where(lane == 2 * j + h, cc, 0.0), axis=1, keepdims=True)
                    kx = jnp.where(_own(lane, h), kk, 0.0)
                    for t, part in enumerate(_split3(-ccol)):
                        kx = jnp.where(lane == _spare(h) + t, part, kx)
                    kx_ref[h, rows, :] = kx.astype(BF16)
                    vt_ref[h, :, rows] = jnp.where(_own(sub, h), vt,
                                                   jnp.where(sub == _spare(h), 1.0, 0.0)).astype(BF16)
                return carry

            lax.fori_loop(0, s // chunk, prep, 0)

        qq = q_ref[...].astype(F32)
        lane_q = lax.broadcasted_iota(jnp.int32, (tq, PAIR), 1)
        qx = []
        for h in range(2):
            ones = (lane_q >= _spare(h)) & (lane_q < _spare(h) + 3)
            qx.append(jnp.transpose(jnp.where(_own(lane_q, h), qq, jnp.where(ones, 1.0, 0.0))).astype(BF16))
        accs = (acc0_ref, acc1_ref)
        for acc in accs:
            acc[...] = jnp.zeros_like(acc)

        def rows_of(kb):
            return pl.ds(pl.multiple_of(kb * tk, tk), tk)

        def scores(kb, dst):
            for h in range(2):
                dst[h] = jnp.dot(kx_ref[h, rows_of(kb), :], qx[h], preferred_element_type=F32)

        def consume(kb, src, m, masked):
            m_out, alpha, pv = [], [], []
            for h in range(2):
                sh = src[h]
                if masked:
                    kpos = kb * tk + lax.broadcasted_iota(jnp.int32, (tk, tq), 0)
                    qpos = i * tq + lax.broadcasted_iota(jnp.int32, (tk, tq), 1)
                    sh = jnp.where(qpos >= kpos, sh, NEG)
                m_new = jnp.maximum(m[h], jnp.max(sh, axis=0, keepdims=True))
                pt = jnp.exp(sh - m_new).astype(BF16)
                alpha.append(jnp.exp(m[h] - m_new))
                pv.append(jnp.dot(vt_ref[h, :, rows_of(kb)], pt, preferred_element_type=F32))
                m_out.append(m_new)
            for h in range(2):
                accs[h][...] = accs[h][...] * alpha[h] + pv[h]
            return tuple(m_out)

        n_full = (i * tq) // tk

        def two_blocks(t, m):
            kb = 2 * t
            scores(kb + 1, stb_ref)
            m = consume(kb, sta_ref, m, False)
            scores(kb + 2, sta_ref)
            return consume(kb + 1, stb_ref, m, False)

        def odd_tail(m):
            scores(n_full, stb_ref)
            m = consume(n_full - 1, sta_ref, m, False)
            return consume(n_full, stb_ref, m, True)

        def even_tail(m):
            return consume(n_full, sta_ref, m, True)

        scores(0, sta_ref)
        m0 = jnp.full((1, tq), NEG, F32)
        m = lax.fori_loop(0, n_full // 2, two_blocks, (m0, m0))
        m = lax.cond(lax.rem(n_full, 2) == 1, odd_tail, even_tail, m)

        sub_o = lax.broadcasted_iota(jnp.int32, (PAIR, tq), 0)
        den = [accs[h][_spare(h):_spare(h) + 1, :] for h in range(2)]
        ot = jnp.where(sub_o < HEAD_DIM, acc0_ref[...] / den[0], acc1_ref[...] / den[1])
        o = jnp.transpose(ot)
        o_ref[...] = o.astype(BF16)
        gate = ag_ref[...].astype(F32)
        ma_ref[...] = (o * (gate * _sigmoid(gate))).astype(BF16)
        sub8 = lax.broadcasted_iota(jnp.int32, (8, tq), 0)
        lse_ref[...] = jnp.where(sub8 == 0, m[0] + jnp.log(den[0]), m[1] + jnp.log(den[1]))

        if n_carry:
            @pl.when((j == N_PAIRS - 1) & (i == s // tq - 1))
            def _():
                _gather_finish(carry_src, carry_dst, carry_sems[:2])
                for cp in _gather_own(carry_src, carry_dst, carry_sems[2]):
                    cp.wait()

    carried = [] if carry is None else list(carry)
    carry_sem = ([pltpu.SemaphoreType.DMA((GATHER_SEMS,))] * 2 + [pltpu.SemaphoreType.DMA((2,))]) if n_carry else []
    return pl.pallas_call(
        body, name="attn_fwd_gather" if n_carry else "attn_fwd", grid=(N_PAIRS, s // tq),
        in_specs=[pl.BlockSpec((tq, PAIR), lambda j, i: (i, j)),
                  pl.BlockSpec((s, PAIR), lambda j, i: (0, j)),
                  pl.BlockSpec((s, PAIR), lambda j, i: (0, j)),
                  pl.BlockSpec((s, 128), lambda j, i: (0, 0)),
                  pl.BlockSpec((tq, PAIR), lambda j, i: (i, j))] + [ANY] * n_carry,
        out_specs=[pl.BlockSpec((tq, PAIR), lambda j, i: (i, j)),
                   pl.BlockSpec((tq, PAIR), lambda j, i: (i, j)),
                   pl.BlockSpec((None, 8, tq), lambda j, i: (j, 0, i))] + [ANY] * n_carry,
        out_shape=[jax.ShapeDtypeStruct((s, ATTN_WIDTH), BF16), jax.ShapeDtypeStruct((s, ATTN_WIDTH), BF16),
                   jax.ShapeDtypeStruct((N_PAIRS, 8, s), F32)] + (_gather_shapes(*carried) if n_carry else []),
        scratch_shapes=[pltpu.VMEM((2, s, PAIR), BF16), pltpu.VMEM((2, PAIR, s), BF16),
                        pltpu.VMEM((PAIR, tq), F32), pltpu.VMEM((PAIR, tq), F32),
                        pltpu.VMEM((2, tk, tq), F32), pltpu.VMEM((2, tk, tq), F32)] + carry_sem,
        compiler_params=_params(("arbitrary", "arbitrary")),
    )(q, k, v, c, ag, *carried)


def _outproj_fwd(x, po, ma, wo, tm=512):
    s = x.shape[0]

    def body(x_ref, po_ref, ma_ref, wo_ref, xn_ref):
        xn_ref[...] = (x_ref[...]
                       + jnp.dot(po_ref[...], wo_ref[0:POOL_WIDTH, :], preferred_element_type=F32)
                       + jnp.dot(ma_ref[...], wo_ref[POOL_WIDTH:, :], preferred_element_type=F32))

    return pl.pallas_call(
        body, name="outproj_fwd", grid=(s // tm,),
        in_specs=[_row_spec(tm, D_MODEL), _row_spec(tm, 512), _row_spec(tm, 512),
                  _full_spec((D_MODEL, D_MODEL))],
        out_specs=_row_spec(tm, D_MODEL),
        out_shape=jax.ShapeDtypeStruct((s, D_MODEL), F32),
        compiler_params=_params(("parallel",)),
    )(x, po, ma, wo)


def _loss_head(x, tgt, g, tm=512):
    s = x.shape[0]

    def body(x_ref, t_ref, g_ref, dx_ref, dg_ref, loss_ref):
        @pl.when(pl.program_id(0) == 0)
        def _():
            dg_ref[...] = jnp.zeros_like(dg_ref)
            loss_ref[...] = jnp.zeros_like(loss_ref)

        xf = x_ref[...]
        r = lax.rsqrt(jnp.mean(xf * xf, axis=-1, keepdims=True) + RMS_EPS)
        xh = xf * r
        gg = g_ref[...]
        e = xh * gg - t_ref[...]
        loss_ref[...] += 0.5 * jnp.sum(jnp.mean(e * e, axis=-1, keepdims=True))
        dy = e * (1.0 / D_MODEL)
        u = dy * gg
        dx_ref[...] = r * (u - xh * jnp.mean(xh * u, axis=-1, keepdims=True))
        dg_ref[...] += jnp.sum(dy * xh, axis=0, keepdims=True)

    return pl.pallas_call(
        body, name="loss_head", grid=(s // tm,),
        in_specs=[_row_spec(tm, D_MODEL), _row_spec(tm, D_MODEL), _full_spec((1, D_MODEL))],
        out_specs=[_row_spec(tm, D_MODEL), _full_spec((1, D_MODEL)), _full_spec((8, 128))],
        out_shape=[jax.ShapeDtypeStruct((s, D_MODEL), F32), jax.ShapeDtypeStruct((1, D_MODEL), F32),
                   jax.ShapeDtypeStruct((8, 128), F32)],
        compiler_params=_params(("arbitrary",)),
    )(x, tgt, g)


def _outproj_bwd(dx, po, ma, ag, o, wo, head_sel, tm=512):
    s = dx.shape[0]

    def body(dx_ref, po_ref, ma_ref, ag_ref, o_ref, wo_ref, sel_ref, dmp_ref, do_ref, dag_ref, dt_ref, gwo_ref):
        @pl.when(pl.program_id(0) == 0)
        def _():
            gwo_ref[...] = jnp.zeros_like(gwo_ref)

        dxb = dx_ref[...].astype(BF16)
        dm = lax.dot_general(dxb, wo_ref[...], NT, preferred_element_type=F32)
        dmp_ref[...] = dm[:, 0:POOL_WIDTH].astype(BF16)
        dma = dm[:, POOL_WIDTH:]
        gate = ag_ref[...].astype(F32)
        of = o_ref[...].astype(F32)
        sg = _sigmoid(gate)
        do = dma * (gate * sg)
        do_ref[...] = do.astype(BF16)
        dag_ref[...] = (dma * of * (sg * (1.0 + gate * (1.0 - sg)))).astype(BF16)
        prod = do * of
        hi = prod.astype(BF16)
        lo = (prod - hi.astype(F32)).astype(BF16)
        dt_ref[...] = (lax.dot_general(sel_ref[...], hi, NT, preferred_element_type=F32)
                       + lax.dot_general(sel_ref[...], lo, NT, preferred_element_type=F32))
        gwo_ref[0:POOL_WIDTH, :] += lax.dot_general(po_ref[...], dxb, TN, preferred_element_type=F32)
        gwo_ref[POOL_WIDTH:, :] += lax.dot_general(ma_ref[...], dxb, TN, preferred_element_type=F32)

    act = jax.ShapeDtypeStruct((s, 512), BF16)
    return pl.pallas_call(
        body, name="outproj_bwd", grid=(s // tm,),
        in_specs=[_row_spec(tm, D_MODEL)] + [_row_spec(tm, 512)] * 4
                 + [_full_spec((D_MODEL, D_MODEL)), _full_spec((8, ATTN_WIDTH))],
        out_specs=[_row_spec(tm, 512)] * 3 + [pl.BlockSpec((8, tm), lambda i: (0, i)),
                                              _full_spec((D_MODEL, D_MODEL))],
        out_shape=[act, act, act, jax.ShapeDtypeStruct((8, s), F32),
                   jax.ShapeDtypeStruct((D_MODEL, D_MODEL), F32)],
        compiler_params=_params(("arbitrary",)),
    )(dx, po, ma, ag, o, wo, head_sel)


def _attn_bwd(q, k, v, do, c, lse, dt, carry=None, tq=256, chunk=512):
    s = q.shape[0]
    tk = 2 * tq
    nq = s // tq
    n_carry = 0 if carry is None else len(carry)

    def body(*refs):
        q_ref, k_ref, v_ref, do_ref, c_ref, lse_ref, dt_ref = refs[:7]
        carry_src = refs[7:7 + n_carry]
        dq_ref, dk_ref, dv_ref, dcs_ref, drs_ref = refs[7 + n_carry:12 + n_carry]
        carry_dst = refs[12 + n_carry:12 + 2 * n_carry]
        (qxt_ref, doxt_ref, qm_ref, dox_ref, dqt0_ref, dqt1_ref, dk0_ref, dk1_ref, dvx_ref,
         kx_ref, vx_ref, ktm_ref, sa_ref, sb_ref) = refs[12 + 2 * n_carry:26 + 2 * n_carry]
        carry_sems = refs[26 + 2 * n_carry:]
        carry_big = tuple(zip(carry_src, carry_dst))
        j = pl.program_id(0)
        b = pl.program_id(1)
        dqts = (dqt0_ref, dqt1_ref)
        dks = (dk0_ref, dk1_ref)

        if n_carry:
            @pl.when((j == 0) & (b == 0))
            def _():
                _scatter_start(carry_big, None, carry_sems)

        @pl.when(b == 0)
        def _():
            def prep(ch, carry):
                rows = pl.ds(pl.multiple_of(ch * chunk, chunk), chunk)
                qq = q_ref[rows, :].astype(F32)
                dd = do_ref[rows, :].astype(F32)
                qt = jnp.transpose(qq)
                ddt = jnp.transpose(dd)
                lane = lax.broadcasted_iota(jnp.int32, (chunk, PAIR), 1)
                sub = lax.broadcasted_iota(jnp.int32, (PAIR, chunk), 0)
                for h in range(2):
                    sp = _spare(h)
                    qm_ref[h, rows, :] = jnp.where(_own(lane, h), qq, jnp.where(lane == sp, 1.0, 0.0)).astype(BF16)
                    dox_ref[h, rows, :] = jnp.where(_own(lane, h), dd, 0.0).astype(BF16)
                    qx = jnp.where(_own(sub, h), qt, jnp.where((sub >= sp) & (sub < sp + 3), 1.0, 0.0))
                    for t, part in enumerate(_split3(-lse_ref[pl.ds(h, 1), rows])):
                        qx = jnp.where(sub == sp + 3 + t, part, qx)
                    qxt_ref[h, :, rows] = qx.astype(BF16)
                    dx = jnp.where(_own(sub, h), ddt, 0.0)
                    for t, part in enumerate(_split3(-dt_ref[pl.ds(h, 1), rows])):
                        dx = jnp.where(sub == sp + t, part, dx)
                    doxt_ref[h, :, rows] = dx.astype(BF16)
                return carry

            lax.fori_loop(0, s // chunk, prep, 0)
            for ref in dqts:
                ref[...] = jnp.zeros_like(ref)

        kk = k_ref[...].astype(F32)
        vv = v_ref[...].astype(F32)
        cc = c_ref[...]
        kt = jnp.transpose(kk)
        lane = lax.broadcasted_iota(jnp.int32, (tk, PAIR), 1)
        sub = lax.broadcasted_iota(jnp.int32, (PAIR, tk), 0)
        for h in range(2):
            sp = _spare(h)
            ccol = jnp.sum(jnp.where(lane == 2 * j + h, cc, 0.0), axis=1, keepdims=True)
            kx = jnp.where(_own(lane, h), kk, jnp.where((lane >= sp + 3) & (lane < sp + 6), 1.0, 0.0))
            for t, part in enumerate(_split3(-ccol)):
                kx = jnp.where(lane == sp + t, part, kx)
            kx_ref[h] = kx.astype(BF16)
            vx_ref[h] = jnp.where(_own(lane, h), vv, jnp.where((lane >= sp) & (lane < sp + 3), 1.0, 0.0)).astype(BF16)
            ktm_ref[h] = jnp.where(_own(sub, h), kt, jnp.where(sub == sp, 1.0, 0.0)).astype(BF16)
        for ref in dks:
            ref[...] = jnp.zeros_like(ref)
        dvx_ref[...] = jnp.zeros_like(dvx_ref)

        def cols_of(i):
            return pl.ds(pl.multiple_of(i * tq, tq), tq)

        def scores(i, dst):
            cols = cols_of(jnp.minimum(i, nq - 1))
            for h in range(2):
                dst[h] = jnp.dot(kx_ref[h], qxt_ref[h, :, cols], preferred_element_type=F32)
                dst[2 + h] = jnp.dot(vx_ref[h], doxt_ref[h, :, cols], preferred_element_type=F32)

        def consume(i, src, masked):
            cols = cols_of(i)
            for h in range(2):
                arg = src[h]
                if masked:
                    kpos = b * tk + lax.broadcasted_iota(jnp.int32, (tk, tq), 0)
                    qpos = i * tq + lax.broadcasted_iota(jnp.int32, (tk, tq), 1)
                    arg = jnp.where(qpos >= kpos, arg, NEG)
                pt = jnp.exp(arg)
                dst = (pt * src[2 + h]).astype(BF16)
                dvx_ref[...] += jnp.dot(pt.astype(BF16), dox_ref[h, cols, :], preferred_element_type=F32)
                dks[h][...] += jnp.dot(dst, qm_ref[h, cols, :], preferred_element_type=F32)
                dqts[h][:, cols] += jnp.dot(ktm_ref[h], dst, preferred_element_type=F32)

        i0 = 2 * b

        def two_blocks(t, masked):
            i = i0 + 2 * t
            scores(i + 1, sb_ref)
            consume(i, sa_ref, masked)
            scores(i + 2, sa_ref)
            consume(i + 1, sb_ref, masked)

        def loop_body(t, carry):
            two_blocks(t, False)
            return carry

        scores(i0, sa_ref)
        two_blocks(0, True)
        lax.fori_loop(1, (nq - i0) // 2, loop_body, 0)

        dv_ref[...] = dvx_ref[...].astype(BF16)
        dk_ref[...] = jnp.where(lane < HEAD_DIM, dk0_ref[...], dk1_ref[...]).astype(BF16)
        dcs_ref[...] = jnp.where(lane == _spare(0), dk0_ref[...], jnp.where(lane == _spare(1), dk1_ref[...], 0.0))

        @pl.when(b == s // tk - 1)
        def _():
            sub8 = lax.broadcasted_iota(jnp.int32, (8, s), 0)
            drs_ref[...] = jnp.where(sub8 == 0, dqt0_ref[_spare(0):_spare(0) + 1, :],
                                     jnp.where(sub8 == 1, dqt1_ref[_spare(1):_spare(1) + 1, :], 0.0))
            sub_c = lax.broadcasted_iota(jnp.int32, (PAIR, chunk), 0)
            for ch in range(s // chunk):
                rows = pl.ds(ch * chunk, chunk)
                both = jnp.where(sub_c < HEAD_DIM, dqt0_ref[:, rows], dqt1_ref[:, rows])
                dq_ref[rows, :] = (jnp.transpose(both) * Q_SCALE).astype(BF16)

        if n_carry:
            @pl.when((j == N_PAIRS - 1) & (b == s // tk - 1))
            def _():
                _scatter_finish(carry_big, None, carry_sems)

    act = jax.ShapeDtypeStruct((s, ATTN_WIDTH), BF16)
    pair_rows = pl.BlockSpec((s, PAIR), lambda j, b: (0, j))
    pair_blk = pl.BlockSpec((tk, PAIR), lambda j, b: (b, j))
    stat = pl.BlockSpec((None, 2, s), lambda j, b: (j, 0, 0))
    carried = [] if carry is None else list(carry)
    carry_sem = [pltpu.SemaphoreType.DMA((SCATTER_SEMS,))] * 2 if n_carry else []
    return pl.pallas_call(
        body, name="attn_bwd_scatter" if n_carry else "attn_bwd", grid=(N_PAIRS, s // tk),
        in_specs=[pair_rows, pair_blk, pair_blk, pair_rows,
                  pl.BlockSpec((tk, 128), lambda j, b: (b, 0)),
                  pl.BlockSpec((None, 8, s), lambda j, b: (j, 0, 0)), stat] + [ANY] * n_carry,
        out_specs=[pair_rows, pair_blk, pair_blk, pair_blk,
                   pl.BlockSpec((None, 8, s), lambda j, b: (j, 0, 0))] + [ANY] * n_carry,
        out_shape=[act, act, act, jax.ShapeDtypeStruct((s, N_PAIRS * 128), F32),
                   jax.ShapeDtypeStruct((N_PAIRS, 8, s), F32)] + (_scatter_shapes(*carried) if n_carry else []),
        scratch_shapes=[pltpu.VMEM((2, PAIR, s), BF16), pltpu.VMEM((2, PAIR, s), BF16),
                        pltpu.VMEM((2, s, PAIR), BF16), pltpu.VMEM((2, s, PAIR), BF16),
                        pltpu.VMEM((PAIR, s), F32), pltpu.VMEM((PAIR, s), F32),
                        pltpu.VMEM((tk, PAIR), F32), pltpu.VMEM((tk, PAIR), F32), pltpu.VMEM((tk, PAIR), F32),
                        pltpu.VMEM((2, tk, PAIR), BF16), pltpu.VMEM((2, tk, PAIR), BF16),
                        pltpu.VMEM((2, PAIR, tk), BF16),
                        pltpu.VMEM((4, tk, tq), F32), pltpu.VMEM((4, tk, tq), F32)] + carry_sem,
        compiler_params=_params(("arbitrary", "arbitrary")),
    )(q, k, v, do, c, lse, dt, *carried)


def _forget_bwd(dcs, drs, lf, tm=256):
    s = lf.shape[0]
    n = s // tm

    def body(dcs_ref, drs_ref, lf_ref, dpf_ref, dfb_ref, carry_ref):
        @pl.when(pl.program_id(0) == 0)
        def _():
            carry_ref[...] = jnp.zeros_like(carry_ref)
            dfb_ref[...] = jnp.zeros_like(dfb_ref)

        lane = lax.broadcasted_iota(jnp.int32, (tm, 128), 1)
        dc = drs_ref[...]
        for pj in range(N_PAIRS):
            blk = dcs_ref[:, pj * 128:(pj + 1) * 128]
            for h in range(2):
                head = 2 * pj + h
                moved = pltpu.roll(blk, (head - _spare(h)) % 128, 1) if head != _spare(h) else blk
                dc = dc - jnp.where(lane == head, moved, 0.0)
        dlf = _scan_rows(dc, reverse=True) + carry_ref[0:1, :]
        carry_ref[...] = dlf[0:8, :]
        dz = jnp.where(lane < 8, dlf * (1.0 - jnp.exp(lf_ref[...])), 0.0)
        dpf_ref[...] = dz.astype(BF16)
        dfb_ref[...] += jnp.sum(dz, axis=0, keepdims=True)

    return pl.pallas_call(
        body, name="forget_bwd", grid=(n,),
        in_specs=[pl.BlockSpec((tm, N_PAIRS * 128), lambda i: (n - 1 - i, 0)),
                  pl.BlockSpec((tm, 128), lambda i: (n - 1 - i, 0)),
                  pl.BlockSpec((tm, 128), lambda i: (n - 1 - i, 0))],
        out_specs=[pl.BlockSpec((tm, 128), lambda i: (n - 1 - i, 0)), _full_spec((1, 128))],
        out_shape=[jax.ShapeDtypeStruct((s, 128), BF16), jax.ShapeDtypeStruct((1, 128), F32)],
        scratch_shapes=[pltpu.VMEM((8, 128), F32)],
        compiler_params=_params(("arbitrary",)),
    )(dcs, drs, lf)


def _pool_bwd(pu, pg, dmp, pw, ps, tm=512):
    s = pu.shape[0]
    hb = tm // HALO
    n = s // tm
    last_halo = s // HALO - 1

    def body(pu_ref, halo_ref, pg_ref, dmp_ref, pgn_ref, dmpn_ref, pw_ref, ps_ref,
             dpu_ref, dpg_ref, gpw_ref, gps_ref):
        i = pl.program_id(0)

        @pl.when(i == 0)
        def _():
            gpw_ref[...] = jnp.zeros_like(gpw_ref)
            gps_ref[...] = jnp.zeros_like(gps_ref)

        u = pu_ref[...].astype(F32)
        halo = jnp.where(i == 0, 0.0, halo_ref[...].astype(F32))
        d = _pool_delta(u, halo, i * tm)
        cnt = _window_counts(i * tm, tm)
        cnt_next = _window_counts((i + 1) * tm, HALO)
        e_parts, dd_parts = [], []
        for gi in range(len(POOL_WINDOWS)):
            cols = slice(gi * GROUP_DIM, (gi + 1) * GROUP_DIM)
            wg = pw_ref[gi]
            scale = ps_ref[:, cols]
            db = d[gi].astype(BF16)
            z = jnp.dot(db, wg, preferred_element_type=F32)
            gate = pg_ref[:, cols].astype(F32)
            sg = _sigmoid(gate)
            dm = dmp_ref[:, cols].astype(F32)
            dy = dm * (gate * sg)
            dpg_ref[:, cols] = (dm * (z * scale) * (sg * (1.0 + gate * (1.0 - sg)))).astype(BF16)
            gps_ref[:, cols] += jnp.sum(dy * z, axis=0, keepdims=True)
            dz = (dy * scale).astype(BF16)
            gpw_ref[gi] += lax.dot_general(db, dz, TN, preferred_element_type=F32)
            dd = lax.dot_general(dz, wg, NT, preferred_element_type=F32)
            gate_n = pgn_ref[:, cols].astype(F32)
            dz_n = (dmpn_ref[:, cols].astype(F32) * (gate_n * _sigmoid(gate_n)) * scale).astype(BF16)
            dd_n = lax.dot_general(dz_n, wg, NT, preferred_element_type=F32)
            dd_n = jnp.where(i == n - 1, 0.0, dd_n)
            dd_parts.append(dd)
            e_parts.append(jnp.concatenate([dd / cnt[gi], dd_n / cnt_next[gi]], axis=0))
        lead = _window_sums(jnp.concatenate(e_parts, axis=1), forward=True)
        for gi in range(len(POOL_WINDOWS)):
            cols = slice(gi * GROUP_DIM, (gi + 1) * GROUP_DIM)
            dpu_ref[:, cols] = (lead[gi][0:tm, :] - dd_parts[gi]).astype(BF16)

    act = jax.ShapeDtypeStruct((s, 512), BF16)
    prev_halo = pl.BlockSpec((HALO, 512), lambda i: (jnp.maximum(i * hb - 1, 0), 0))
    next_halo = pl.BlockSpec((HALO, 512), lambda i: (jnp.minimum((i + 1) * hb, last_halo), 0))
    return pl.pallas_call(
        body, name="pool_bwd", grid=(n,),
        in_specs=[_row_spec(tm, 512), prev_halo, _row_spec(tm, 512), _row_spec(tm, 512), next_halo, next_halo,
                  _full_spec((4, GROUP_DIM, GROUP_DIM)), _full_spec((1, 512))],
        out_specs=[_row_spec(tm, 512), _row_spec(tm, 512), _full_spec((4, GROUP_DIM, GROUP_DIM)),
                   _full_spec((1, 512))],
        out_shape=[act, act, jax.ShapeDtypeStruct((4, GROUP_DIM, GROUP_DIM), F32),
                   jax.ShapeDtypeStruct((1, 512), F32)],
        compiler_params=_params(("arbitrary",)),
    )(pu, pu, pg, dmp, pg, dmp, pw, ps)


def _inproj_bwd_x(dsec, w, x, g, dxo, exchange=None, scatter=None, tm=512):
    s = x.shape[0]
    n_steps = s // tm
    carried = list(exchange or scatter or ())
    n_carry = len(carried)

    def body(*refs):
        d_refs = refs[0:7]
        w_ref, x_ref, g_ref, dxo_ref = refs[7:11]
        carry_src = refs[11:11 + n_carry]
        dxi_ref, dg_ref = refs[11 + n_carry:13 + n_carry]
        carry_dst = refs[13 + n_carry:13 + 2 * n_carry]
        sems = refs[13 + 2 * n_carry:]
        step = pl.program_id(0)

        def comm(phase):
            if exchange is not None:
                (_exchange_start if phase == 0 else _exchange_finish)(carry_src, carry_dst, None, sems)
            else:
                big = tuple(zip(carry_src[:2], carry_dst[:2]))
                (_scatter_start if phase == 0 else _scatter_finish)(big, (carry_src[2], carry_dst[2]), sems)

        @pl.when(step == 0)
        def _():
            dg_ref[...] = jnp.zeros_like(dg_ref)
            if n_carry:
                comm(0)

        dh = None
        for d_ref, (off, ncol) in zip(d_refs, SECTIONS):
            t = lax.dot_general(d_ref[...], w_ref[:, off:off + ncol], NT, preferred_element_type=F32)
            dh = t if dh is None else dh + t
        xf = x_ref[...]
        r = lax.rsqrt(jnp.mean(xf * xf, axis=-1, keepdims=True) + RMS_EPS)
        xh = xf * r
        u = dh * g_ref[...]
        dxi_ref[...] = dxo_ref[...] + r * (u - xh * jnp.mean(xh * u, axis=-1, keepdims=True))
        dg_ref[...] += jnp.sum(dh * xh, axis=0, keepdims=True)

        if n_carry:
            @pl.when(step == n_steps - 1)
            def _():
                comm(1)

    if exchange is not None:
        name, carry_shapes, n_sems = "inproj_bwd_x_exchange", _exchange_shapes(*exchange), 3
    elif scatter is not None:
        name, n_sems = "inproj_bwd_x_scatter", SCATTER_SEMS
        carry_shapes = _scatter_shapes(*scatter[:2]) + [jax.ShapeDtypeStruct((3, SMALL_HALF, 128), F32)]
    else:
        name, carry_shapes, n_sems = "inproj_bwd_x", [], 0
    return pl.pallas_call(
        body, name=name, grid=(n_steps,),
        in_specs=[_row_spec(tm, ncol) for _, ncol in SECTIONS]
                 + [_full_spec((D_MODEL, IN_COLS_PAD)), _row_spec(tm, D_MODEL), _full_spec((1, D_MODEL)),
                    _row_spec(tm, D_MODEL)] + [ANY] * n_carry,
        out_specs=[_row_spec(tm, D_MODEL), _full_spec((1, D_MODEL))] + [ANY] * n_carry,
        out_shape=[jax.ShapeDtypeStruct((s, D_MODEL), F32), jax.ShapeDtypeStruct((1, D_MODEL), F32)] + carry_shapes,
        scratch_shapes=[pltpu.SemaphoreType.DMA((n_sems,))] * 2 if n_carry else [],
        compiler_params=_params(("arbitrary",)),
    )(*dsec, w, x, g, dxo, *carried)


def _inproj_bwd_w(h, dsec, tm=512):
    s = h.shape[0]

    def body(*refs):
        h_ref = refs[0]
        d_refs = refs[1:8]
        gw_ref = refs[8]

        @pl.when(pl.program_id(0) == 0)
        def _():
            gw_ref[...] = jnp.zeros_like(gw_ref)

        hh = h_ref[...]
        for d_ref, (off, ncol) in zip(d_refs, SECTIONS):
            gw_ref[:, off:off + ncol] += lax.dot_general(hh, d_ref[...], TN, preferred_element_type=F32)

    return pl.pallas_call(
        body, name="inproj_bwd_w", grid=(s // tm,),
        in_specs=[_row_spec(tm, D_MODEL)] + [_row_spec(tm, ncol) for _, ncol in SECTIONS],
        out_specs=_full_spec((D_MODEL, IN_COLS_PAD)),
        out_shape=jax.ShapeDtypeStruct((D_MODEL, IN_COLS_PAD), F32),
        compiler_params=_params(("arbitrary",), vmem_mb=56),
    )(h, *dsec)


def _elementwise(fn, name, n_out, arrays, tm):
    shape = arrays[0].shape
    rest = shape[1:]
    n_in = len(arrays)

    def body(*refs):
        outs = fn(*[r[...] for r in refs[:n_in]])
        for r, val in zip(refs[n_in:], outs):
            r[...] = val

    spec = pl.BlockSpec((tm,) + rest, lambda i: (i,) + (0,) * len(rest))
    return pl.pallas_call(
        body, name=name, grid=(shape[0] // tm,),
        in_specs=[spec] * n_in, out_specs=[spec] * n_out, out_shape=[jax.ShapeDtypeStruct(shape, F32)] * n_out,
        compiler_params=_params(("parallel",)),
    )(*arrays)


def _add2(a, b, name, tm):
    return _elementwise(lambda p, q: (p + q,), name, 1, [a, b], tm)[0]


def _chip_sum(place, g, r, name, tm):
    n_layers, _, rows, cols = g.shape
    half = rows // 2
    nb = half // tm

    def body(place_ref, g_ref, r_ref, o_ref):
        o_ref[...] = (g_ref[...] + r_ref[...]).astype(BF16)

    blk = (None, None, tm, cols)
    return pl.pallas_call(
        body, name=name,
        grid_spec=pltpu.PrefetchScalarGridSpec(
            num_scalar_prefetch=1, grid=(n_layers, N_CHIPS, nb),
            in_specs=[pl.BlockSpec(blk, lambda l, k, i, p: (l, k, p[1] * nb + i, 0)),
                      pl.BlockSpec(blk, lambda l, k, i, p: (l, k, i, 0))],
            out_specs=pl.BlockSpec(blk, lambda l, k, i, p: (l, k, i, 0))),
        out_shape=jax.ShapeDtypeStruct((n_layers, N_CHIPS, half, cols), BF16),
        compiler_params=_params(("parallel", "parallel", "parallel")),
    )(place, g, r)


def _mesh_sum(place, g, r, got, into, layer, name, tm):
    _, _, rows, cols = g.shape
    half = rows // 2
    nb = half // tm

    def body(place_ref, g_ref, r_ref, got_ref, into_ref, o_ref):
        own = g_ref[...] + r_ref[...]
        o_ref[...] = (own + got_ref[0].astype(F32)) + (got_ref[1].astype(F32) + got_ref[2].astype(F32))

    blk = (None, None, tm, cols)
    return pl.pallas_call(
        body, name=name,
        grid_spec=pltpu.PrefetchScalarGridSpec(
            num_scalar_prefetch=1, grid=(nb,),
            in_specs=[pl.BlockSpec(blk, lambda i, p: (0, p[0], p[1] * nb + i, 0)),
                      pl.BlockSpec(blk, lambda i, p: (0, p[0], i, 0)),
                      pl.BlockSpec((None, 3, tm, cols), lambda i, p: (0, 0, i, 0)),
                      ANY],
            out_specs=pl.BlockSpec((None, tm, cols), lambda i, p: (layer, p[1] * nb + i, 0))),
        out_shape=jax.ShapeDtypeStruct(into.shape, F32),
        input_output_aliases={4: 0},
        compiler_params=_params(("parallel",)),
    )(place, g, r, got, into)


def _mesh_sum_small(place, ps, got, tm):
    nb = SMALL_HALF // tm

    def body(place_ref, ps_ref, got_ref, o_ref):
        o_ref[...] = (ps_ref[...] + got_ref[0]) + (got_ref[1] + got_ref[2])

    return pl.pallas_call(
        body, name="mesh_sum_small",
        grid_spec=pltpu.PrefetchScalarGridSpec(
            num_scalar_prefetch=1, grid=(nb,),
            in_specs=[pl.BlockSpec((tm, 128), lambda i, p: (p[1] * nb + i, 0)),
                      pl.BlockSpec((3, tm, 128), lambda i, p: (0, i, 0))],
            out_specs=pl.BlockSpec((tm, 128), lambda i, p: (p[1] * nb + i, 0))),
        out_shape=jax.ShapeDtypeStruct((SMALL_ROWS, 128), F32),
        compiler_params=_params(("parallel",)),
    )(place, ps, got)


def _adamw(g, w, m, v, name, tm):
    def fn(g, w, m, v):
        m = ADAM_B1 * m + (1.0 - ADAM_B1) * g
        v = ADAM_B2 * v + (1.0 - ADAM_B2) * (g * g)
        m_hat = m / (1.0 - ADAM_B1 ** ADAM_STEP)
        v_hat = v / (1.0 - ADAM_B2 ** ADAM_STEP)
        delta = -ADAM_LR * (m_hat / (jnp.sqrt(v_hat) + ADAM_EPS) + ADAM_WD * w)
        return g, delta, m, v

    return _elementwise(fn, name, 4, [g, w, m, v], tm)


def _position():
    x, y, c = lax.axis_index("x"), lax.axis_index("y"), lax.axis_index("c")
    other_chips = [(x, 1 - y), (1 - x, y), (1 - x, 1 - y)]
    return x, y, c, other_chips


def _remote(src, dst, sems, k, to):
    send_sems, recv_sems = sems
    return pltpu.make_async_remote_copy(src_ref=src, dst_ref=dst, send_sem=send_sems.at[k],
                                        recv_sem=recv_sems.at[k], device_id=to, device_id_type=MESH)


def _comm_call(body, name, n_in, out_shape, n_remote, aliases=None):
    return pl.pallas_call(
        body, name=name, in_specs=[ANY] * n_in, out_specs=[ANY] * len(out_shape), out_shape=out_shape,
        scratch_shapes=[pltpu.SemaphoreType.DMA((n_remote,)), pltpu.SemaphoreType.DMA((n_remote,))],
        input_output_aliases=aliases or {},
    )


def _halves(ref_rows, c):
    half = ref_rows // 2
    return pl.ds(half * c, half), pl.ds(half * (1 - c), half)


def _gather_weights(wi, wo):
    def body(wi_ref, wo_ref, gi_ref, go_ref, send_sems, recv_sems):
        _gather_start((wi_ref, wo_ref), (gi_ref, go_ref), (send_sems, recv_sems))
        _gather_finish((wi_ref, wo_ref), (gi_ref, go_ref), (send_sems, recv_sems))

    return _comm_call(body, "gather_weights", 2, _gather_shapes(wi, wo), GATHER_SEMS)(wi, wo)


GATHER_SEMS = 12
SCATTER_SEMS = 9


def _gather_shapes(wi, wo):
    return [jax.ShapeDtypeStruct((a.shape[0], N_CHIPS) + a.shape[1:], a.dtype) for a in (wi, wo)]


def _gather_sends(srcs, dsts, sems):
    x, y, c, chips = _position()
    out = []
    for k, (cx, cy) in enumerate(chips):
        for a, (src, dst) in enumerate(zip(srcs, dsts)):
            mine, _ = _halves(src.shape[1], c)
            out.append(_remote(src.at[:, mine], dst.at[:, k + 1, mine], sems, len(srcs) * k + a, (cx, cy, c)))
    return out


def _gather_own(srcs, dsts, local_sems):
    return [pltpu.make_async_copy(src, dst.at[:, 0], local_sems.at[a]) for a, (src, dst) in enumerate(zip(srcs, dsts))]


def _gather_start(srcs, dsts, sems):
    for cp in _gather_sends(srcs, dsts, sems):
        cp.start()


def _gather_finish(srcs, dsts, sems):
    x, y, c, _ = _position()
    n = len(srcs)
    forwards = []
    for k in range(3):
        for a, dst in enumerate(dsts):
            mine, _ = _halves(dst.shape[2], c)
            blk = dst.at[:, k + 1, mine]
            _remote(blk, blk, sems, n * k + a, (x, y, c)).wait_recv()
            forwards.append(_remote(blk, blk, sems, 3 * n + n * k + a, (x, y, 1 - c)))
            forwards[-1].start()
    for k in range(3):
        for a, dst in enumerate(dsts):
            _, other = _halves(dst.shape[2], c)
            blk = dst.at[:, k + 1, other]
            _remote(blk, blk, sems, 3 * n + n * k + a, (x, y, c)).wait_recv()
    for cp in _gather_sends(srcs, dsts, sems) + forwards:
        cp.wait_send()


def _scatter_sends(big, small, sems):
    x, y, c, chips = _position()
    n = len(big) + (small is not None)
    out = []
    for k, (cx, cy) in enumerate(chips):
        ck = 2 * cx + cy
        for a, (src, dst) in enumerate(big):
            out.append(_remote(src.at[:, ck], dst.at[:, k], sems, n * k + a, (cx, cy, c)))
        if small is not None:
            half, _ = _halves(SMALL_ROWS, c)
            out.append(_remote(small[0].at[half], small[1].at[k], sems, n * k + n - 1, (cx, cy, c)))
    return out


def _scatter_start(big, small, sems):
    for cp in _scatter_sends(big, small, sems):
        cp.start()


def _scatter_finish(big, small, sems):
    x, y, c, _ = _position()
    n = len(big) + (small is not None)
    for k in range(3):
        landing = [dst.at[:, k] for _, dst in big] + ([small[1].at[k]] if small is not None else [])
        for a, blk in enumerate(landing):
            _remote(blk, blk, sems, n * k + a, (x, y, c)).wait_recv()
    for cp in _scatter_sends(big, small, sems):
        cp.wait_send()


def _exchange_shapes(gi, go):
    return [jax.ShapeDtypeStruct(a.shape[:2] + (a.shape[2] // 2, a.shape[3]), F32) for a in (gi, go)]


def _exchange_copies(srcs, dsts, small, sems):
    x, y, c, _ = _position()
    cps = [_remote(src.at[:, :, _halves(src.shape[2], c)[1]], dst, sems, a, (x, y, 1 - c))
           for a, (src, dst) in enumerate(zip(srcs, dsts))]
    if small is not None:
        cps.append(_remote(small[0], small[1], sems, len(srcs), (x, y, 1 - c)))
    return cps


def _exchange_start(srcs, dsts, small, sems):
    for cp in _exchange_copies(srcs, dsts, small, sems):
        cp.start()


def _exchange_finish(srcs, dsts, small, sems):
    for cp in _exchange_copies(srcs, dsts, small, sems):
        cp.wait()


def _exchange_sibling(gi, go, sm):
    def body(gi_ref, go_ref, sm_ref, ri_ref, ro_ref, rs_ref, send_sems, recv_sems):
        args = ((gi_ref, go_ref), (ri_ref, ro_ref), (sm_ref, rs_ref), (send_sems, recv_sems))
        _exchange_start(*args)
        _exchange_finish(*args)

    out_shape = _exchange_shapes(gi, go) + [jax.ShapeDtypeStruct(sm.shape, F32)]
    return _comm_call(body, "exchange_sibling", 3, out_shape, 3)(gi, go, sm)


def _scatter_shapes(pi, po):
    return [jax.ShapeDtypeStruct((a.shape[0], 3) + a.shape[2:], a.dtype) for a in (pi, po)]


def _scatter_chips(pi, po, ps):
    def body(pi_ref, po_ref, ps_ref, ri_ref, ro_ref, rs_ref, send_sems, recv_sems):
        big, small, sems = ((pi_ref, ri_ref), (po_ref, ro_ref)), (ps_ref, rs_ref), (send_sems, recv_sems)
        _scatter_start(big, small, sems)
        _scatter_finish(big, small, sems)

    out_shape = _scatter_shapes(pi, po) + [jax.ShapeDtypeStruct((3, SMALL_HALF, 128), F32)]
    return _comm_call(body, "scatter_chips", 3, out_shape, SCATTER_SEMS)(pi, po, ps)


def _join_sibling(fi, fo, fs):
    def body(fi_ref, fo_ref, fs_ref, oi_ref, oo_ref, os_ref, send_sems, recv_sems):
        x, y, c, _ = _position()
        sems = (send_sems, recv_sems)
        sent, got = [], []
        for a, (src, dst) in enumerate(((fi_ref, oi_ref), (fo_ref, oo_ref))):
            mine, other = _halves(src.shape[1], c)
            sent.append(_remote(src.at[:, mine], dst.at[:, mine], sems, a, (x, y, 1 - c)))
            got.append(_remote(dst.at[:, other], dst.at[:, other], sems, a, (x, y, c)))
        mine, other = _halves(SMALL_ROWS, c)
        sent.append(_remote(fs_ref.at[mine], os_ref.at[mine], sems, 2, (x, y, 1 - c)))
        got.append(_remote(os_ref.at[other], os_ref.at[other], sems, 2, (x, y, c)))
        for cp in sent:
            cp.start()
        for cp in got:
            cp.wait_recv()
        for cp in sent:
            cp.wait_send()

    out_shape = [jax.ShapeDtypeStruct(a.shape, F32) for a in (fi, fo, fs)]
    return _comm_call(body, "join_sibling", 3, out_shape, 3, aliases={0: 0, 1: 1, 2: 2})(fi, fo, fs)


SMALL_SIZES = (DEPTH * D_MODEL, D_MODEL, DEPTH * POOL_WIDTH, 128, DEPTH * 4 * GROUP_DIM * GROUP_DIM)
LOSS_SLOT = DEPTH * D_MODEL + D_MODEL + DEPTH * POOL_WIDTH + DEPTH * 8


def _pack_small(norm_g, final_g, pool_scale, forget_bias, pool_w, loss=None):
    fb = forget_bias.reshape(-1)
    if loss is not None:
        fb = jnp.concatenate([fb, loss.reshape(1)])
    fb = jnp.pad(fb, (0, 128 - fb.size))
    flat = jnp.concatenate([norm_g.reshape(-1), final_g.reshape(-1), pool_scale.reshape(-1), fb,
                            pool_w.reshape(-1)])
    return jnp.pad(flat, (0, SMALL_ROWS * 128 - flat.size)).reshape(SMALL_ROWS, 128)


def _unpack_small(packed):
    flat = packed.reshape(-1)
    offs = [0]
    for n in SMALL_SIZES:
        offs.append(offs[-1] + n)
    norm_g, final_g, pool_scale, fb, pool_w = [flat[offs[i]:offs[i + 1]] for i in range(5)]
    return (norm_g.reshape(DEPTH, D_MODEL), final_g, pool_scale.reshape(DEPTH, POOL_WIDTH),
            fb[:DEPTH * 8].reshape(DEPTH, 8), pool_w.reshape(DEPTH, 4, GROUP_DIM, GROUP_DIM))


def kernel(x, norm_g, w_in, forget_bias, pool_w, pool_scale, w_out, final_g, loss_target, m_norm_g, m_w_in, m_forget_bias, m_pool_w, m_pool_scale, m_w_out, m_final_g, v_norm_g, v_w_in, v_forget_bias, v_pool_w, v_pool_scale, v_w_out, v_final_g):
    x0 = x[0]
    tgt = loss_target[0]
    s = x0.shape[0]

    me = 2 * lax.axis_index("x") + lax.axis_index("y")
    place = jnp.stack([me, lax.axis_index("c")]).astype(jnp.int32)
    wi_b, wo_b = w_in.astype(BF16), w_out.astype(BF16)

    def chip_piece(own, got, chip):
        rel = jnp.bitwise_xor(chip, me)
        if own is None:
            return lax.dynamic_index_in_dim(got, rel, axis=1, keepdims=False)[0]
        other = lax.dynamic_index_in_dim(got, jnp.maximum(rel, 1), axis=1, keepdims=False)
        return jnp.where(rel == 0, own, other)[0]

    def layer_weights(l, gi, go, own_gathered):
        own_i, own_o = (None, None) if own_gathered else (wi_b[l:l + 1], wo_b[l:l + 1])
        pad = jnp.zeros((D_MODEL, IN_COLS_PAD - IN_COLS), BF16)
        w_i = jnp.concatenate([chip_piece(own_i, gi, k) for k in range(N_CHIPS)] + [pad], axis=-1)
        w_o = jnp.concatenate([chip_piece(own_o, go, k) for k in range(N_CHIPS)], axis=0)
        return w_i, w_o

    pool_w_b = pool_w.astype(BF16)
    fb_pad = jnp.pad(forget_bias, ((0, 0), (0, 128 - forget_bias.shape[1])))
    head_sel = (jnp.arange(ATTN_WIDTH)[None, :] // HEAD_DIM == jnp.arange(8)[:, None]).astype(BF16)

    saved, weights = [], []
    xl = x0
    gathered = _gather_weights(wi_b[0:1], wo_b[0:1])
    for l in range(DEPTH):
        w_i, w_o = layer_weights(l, *gathered, own_gathered=l > 0)
        weights.append((w_i, w_o))
        pu, pg, q, k, v, ag, lf, c, h = _inproj_fwd(xl, norm_g[l][None], w_i, fb_pad[l][None])
        po = _pool_fwd(pu, pg, pool_w_b[l], pool_scale[l][None])
        if l + 1 < DEPTH:
            ma, o, lse, *gathered = _attn_fwd(q, k, v, c, ag, carry=(wi_b[l + 1:l + 2], wo_b[l + 1:l + 2]))
        else:
            ma, o, lse = _attn_fwd(q, k, v, c, ag)
        saved.append((xl, h, pu, pg, q, k, v, ag, lf, c, po, ma, o, lse))
        xl = _outproj_fwd(xl, po, ma, w_o)

    dx, g_final, loss_part = _loss_head(xl, tgt, final_g[None])

    g_norm, g_fb, g_pw, g_ps = [], [], [], []
    parts = {}
    pending = None
    for l in reversed(range(DEPTH)):
        xl, h, pu, pg, q, k, v, ag, lf, c, po, ma, o, lse = saved[l]
        w_i, w_o = weights[l]
        dmp, do, dag, dt, gwo = _outproj_bwd(dx, po, ma, ag, o, w_o, head_sel)
        if pending is None:
            dq, dk, dv, dcs, drs = _attn_bwd(q, k, v, do, c, lse, dt.reshape(N_PAIRS, 2, s))
        else:
            dq, dk, dv, dcs, drs, qi, qo = _attn_bwd(q, k, v, do, c, lse, dt.reshape(N_PAIRS, 2, s), carry=pending)
            parts[l + 1] += (qi, qo)
        drs = jnp.pad(drs[:, 0:2, :].reshape(8, s).T, ((0, 0), (0, 120)))
        dpf, dfb = _forget_bwd(dcs, drs, lf)
        dpu, dpg, gpw, gps = _pool_bwd(pu, pg, dmp, pool_w_b[l], pool_scale[l][None])
        dsec = (dpu, dpg, dq, dk, dv, dag, dpf)
        gwi = _inproj_bwd_w(h, dsec)
        g_fb.append(dfb[0, :8])
        g_pw.append(gpw)
        g_ps.append(gps[0])
        send_i = jnp.stack([gwi[:, k * SHARD_COLS:(k + 1) * SHARD_COLS] for k in range(N_CHIPS)])[None]
        send_o = gwo.reshape(1, N_CHIPS, SHARD_ROWS, D_MODEL)
        if l > 0:
            dx, gn, ri, ro = _inproj_bwd_x(dsec, w_i, xl, norm_g[l][None], dx, exchange=(send_i, send_o))
            g_norm.append(gn[0])
        else:
            dx, gn = _inproj_bwd_x(dsec, w_i, xl, norm_g[l][None], dx)
            g_norm.append(gn[0])
            g_norm, g_fb, g_pw, g_ps = [t[::-1] for t in (g_norm, g_fb, g_pw, g_ps)]
            send_s = _pack_small(jnp.stack(g_norm), g_final[0], jnp.stack(g_ps), jnp.stack(g_fb), jnp.stack(g_pw),
                                 loss=loss_part[0, 0])
            ri, ro, rs = _exchange_sibling(send_i, send_o, send_s)
        pending = (_chip_sum(place, send_i, ri, "chip_sum_w_in", 256),
                   _chip_sum(place, send_o, ro, "chip_sum_w_out", SHARD_ROWS // 2))
        parts[l] = (send_i, ri, send_o, ro)
    ps_ = _add2(send_s, rs, "chip_sum_small", SMALL_ROWS // 4)
    qi, qo, qs = _scatter_chips(*pending, ps_)
    parts[0] += (qi, qo)
    fi = lax.empty((DEPTH, D_MODEL, SHARD_COLS), F32)
    fo = lax.empty((DEPTH, SHARD_ROWS, D_MODEL), F32)
    for l in range(DEPTH):
        send_i, ri, send_o, ro, qi, qo = parts[l]
        fi = _mesh_sum(place, send_i, ri, qi, fi, l, "mesh_sum_w_in", 256)
        fo = _mesh_sum(place, send_o, ro, qo, fo, l, "mesh_sum_w_out", SHARD_ROWS // 2)
    hs = _mesh_sum_small(place, ps_, qs, SMALL_HALF // 4)
    fi, fo, fs = _join_sibling(fi, fo, hs)

    col_major = lambda a: jnp.transpose(a, (2, 0, 1))
    flat_o = lambda a: a.reshape(-1, D_MODEL)
    out_i = _adamw(col_major(fi), col_major(w_in), col_major(m_w_in), col_major(v_w_in), "adamw_w_in", 77)
    out_i = [jnp.transpose(t, (1, 2, 0)) for t in out_i]
    out_o = _adamw(flat_o(fo), flat_o(w_out), flat_o(m_w_out), flat_o(v_w_out), "adamw_w_out", 256)
    out_s = _adamw(fs, _pack_small(norm_g, final_g, pool_scale, forget_bias, pool_w),
                   _pack_small(m_norm_g, m_final_g, m_pool_scale, m_forget_bias, m_pool_w),
                   _pack_small(v_norm_g, v_final_g, v_pool_scale, v_forget_bias, v_pool_w),
                   "adamw_small", SMALL_ROWS // 4)

    loss = out_s[0].reshape(-1)[LOSS_SLOT]
    groups = []
    for oi, oo, os_ in zip(out_i, out_o, out_s):
        sn, sf, sp, sb, sw = _unpack_small(os_)
        groups.append((sn, oi.reshape(w_in.shape), sb, sw, sp, oo.reshape(w_out.shape), sf))
    return (loss, dx[None]) + tuple(t for grp in groups for t in grp)
```

```python
import jax
import jax.numpy as jnp
from jax import lax
from jax.experimental import pallas as pl
from jax.experimental.pallas import tpu as pltpu

F32 = jnp.float32
BF16 = jnp.bfloat16

D_MODEL = 1024
DEPTH = 4
POOL_WIDTH = 512
ATTN_WIDTH = 512
HEAD_DIM = 64
PAIR = 2 * HEAD_DIM
N_PAIRS = ATTN_WIDTH // PAIR
POOL_WINDOWS = (2, 4, 8, 16)
GROUP_DIM = 128
HALO = 16
IN_COLS = 3080
OFF_F = 3072
IN_COLS_PAD = 3200
SECTIONS = ((0, 512), (512, 512), (1024, 512), (1536, 512), (2048, 512), (2560, 512), (OFF_F, 128))
N_CHIPS = 4
SHARD_COLS = IN_COLS // N_CHIPS
SHARD_ROWS = D_MODEL // N_CHIPS
RMS_EPS = 1e-6
NEG = -1e30
Q_SCALE = 0.125

ADAM_LR = 0.001
ADAM_B1 = 0.9
ADAM_B2 = 0.999
ADAM_EPS = 1e-08
ADAM_WD = 0.01
ADAM_STEP = 10

SMALL_ROWS = 2112
SMALL_HALF = SMALL_ROWS // 2

NT = (((1,), (1,)), ((), ()))
TN = (((0,), (0,)), ((), ()))
MESH = pl.DeviceIdType.MESH
ANY = pl.BlockSpec(memory_space=pl.ANY)


def _params(semantics, vmem_mb=48):
    return pltpu.CompilerParams(dimension_semantics=semantics, vmem_limit_bytes=vmem_mb << 20)


def _row_spec(tm, cols):
    return pl.BlockSpec((tm, cols), lambda i: (i, 0))


def _full_spec(shape):
    return pl.BlockSpec(shape, lambda *_: (0,) * len(shape))


def _sigmoid(x):
    return jax.nn.sigmoid(x)


def _scan_rows(a, reverse=False):
    n = a.shape[0]
    row = lax.broadcasted_iota(jnp.int32, a.shape, 0)
    k = 1
    while k < n:
        if reverse:
            a = a + jnp.where(row < n - k, pltpu.roll(a, n - k, 0), 0.0)
        else:
            a = a + jnp.where(row >= k, pltpu.roll(a, k, 0), 0.0)
        k *= 2
    return a


def _inproj_fwd(x, g, w, fb, tm=512):
    s = x.shape[0]

    def body(x_ref, g_ref, w_ref, fb_ref, pu_ref, pg_ref, q_ref, k_ref, v_ref, ag_ref,
             lf_ref, c_ref, h_ref, carry_ref):
        @pl.when(pl.program_id(0) == 0)
        def _():
            carry_ref[...] = jnp.zeros_like(carry_ref)

        xf = x_ref[...]
        r = lax.rsqrt(jnp.mean(xf * xf, axis=-1, keepdims=True) + RMS_EPS)
        h = (xf * r * g_ref[...]).astype(BF16)
        h_ref[...] = h

        def proj(sec):
            off, n = SECTIONS[sec]
            return jnp.dot(h, w_ref[:, off:off + n], preferred_element_type=F32)

        pu_ref[...] = proj(0).astype(BF16)
        pg_ref[...] = proj(1).astype(BF16)
        q_ref[...] = (proj(2) * Q_SCALE).astype(BF16)
        k_ref[...] = proj(3).astype(BF16)
        v_ref[...] = proj(4).astype(BF16)
        ag_ref[...] = proj(5).astype(BF16)
        z = proj(6) + fb_ref[...]
        lf = jnp.minimum(z, 0.0) - jnp.log(1.0 + jnp.exp(-jnp.abs(z)))
        lf_ref[...] = lf
        c_ref[...] = _scan_rows(lf) + carry_ref[0:1, :]
        carry_ref[0:1, :] = c_ref[tm - 1:tm, :]

    act = jax.ShapeDtypeStruct((s, 512), BF16)
    return pl.pallas_call(
        body, name="inproj_fwd", grid=(s // tm,),
        in_specs=[_row_spec(tm, D_MODEL), _full_spec((1, D_MODEL)), _full_spec((D_MODEL, IN_COLS_PAD)),
                  _full_spec((1, 128))],
        out_specs=[_row_spec(tm, 512)] * 6 + [_row_spec(tm, 128), _row_spec(tm, 128), _row_spec(tm, D_MODEL)],
        out_shape=[act] * 6 + [jax.ShapeDtypeStruct((s, 128), F32), jax.ShapeDtypeStruct((s, 128), F32),
                               jax.ShapeDtypeStruct((s, D_MODEL), BF16)],
        scratch_shapes=[pltpu.VMEM((8, 128), F32)],
        compiler_params=_params(("arbitrary",)),
    )(x, g, w, fb)


def _window_sums(ext, forward):
    n = ext.shape[0]
    outs = []
    for gi, w in enumerate(POOL_WINDOWS):
        a = ext[:, gi * GROUP_DIM:(gi + 1) * GROUP_DIM]
        k = 1
        while k < w:
            a = a + pltpu.roll(a, (n - k) if forward else k, 0)
            k *= 2
        outs.append(a)
    return outs


def _window_counts(row0, tm):
    t = row0 + lax.broadcasted_iota(jnp.int32, (tm, GROUP_DIM), 0)
    return [jnp.minimum(t + 1, w).astype(F32) for w in POOL_WINDOWS]


def _pool_delta(u, halo, row0):
    tm = u.shape[0]
    sums = _window_sums(jnp.concatenate([halo, u], axis=0), forward=False)
    cnt = _window_counts(row0, tm)
    return [sums[gi][HALO:, :] / cnt[gi] - u[:, gi * GROUP_DIM:(gi + 1) * GROUP_DIM]
            for gi in range(len(POOL_WINDOWS))]


def _pool_fwd(pu, pg, pw, ps, tm=512):
    s = pu.shape[0]
    hb = tm // HALO

    def body(pu_ref, halo_ref, pg_ref, pw_ref, ps_ref, po_ref):
        i = pl.program_id(0)
        u = pu_ref[...].astype(F32)
        halo = jnp.where(i == 0, 0.0, halo_ref[...].astype(F32))
        d = _pool_delta(u, halo, i * tm)
        for gi in range(len(POOL_WINDOWS)):
            cols = slice(gi * GROUP_DIM, (gi + 1) * GROUP_DIM)
            z = jnp.dot(d[gi].astype(BF16), pw_ref[gi], preferred_element_type=F32)
            gate = pg_ref[:, cols].astype(F32)
            po_ref[:, cols] = (z * ps_ref[:, cols] * (gate * _sigmoid(gate))).astype(BF16)

    return pl.pallas_call(
        body, name="pool_fwd", grid=(s // tm,),
        in_specs=[_row_spec(tm, 512),
                  pl.BlockSpec((HALO, 512), lambda i: (jnp.maximum(i * hb - 1, 0), 0)),
                  _row_spec(tm, 512), _full_spec((4, GROUP_DIM, GROUP_DIM)), _full_spec((1, 512))],
        out_specs=_row_spec(tm, 512),
        out_shape=jax.ShapeDtypeStruct((s, 512), BF16),
        compiler_params=_params(("parallel",)),
    )(pu, pu, pg, pw, ps)


def _split3(x):
    hi = x.astype(BF16).astype(F32)
    r1 = x - hi
    lo = r1.astype(BF16).astype(F32)
    return hi, lo, (r1 - lo).astype(BF16).astype(F32)


def _own(idx, h):
    return (idx < HEAD_DIM) if h == 0 else (idx >= HEAD_DIM)


def _spare(h):
    return HEAD_DIM if h == 0 else 0


def _attn_fwd(q, k, v, c, ag, carry=None, tq=512, tk=512, chunk=512):
    s = q.shape[0]
    n_carry = 0 if carry is None else len(carry)

    def body(*refs):
        q_ref, k_ref, v_ref, c_ref, ag_ref = refs[:5]
        carry_src = refs[5:5 + n_carry]
        ma_ref, o_ref, lse_ref = refs[5 + n_carry:8 + n_carry]
        carry_dst = refs[8 + n_carry:8 + 2 * n_carry]
        kx_ref, vt_ref, acc0_ref, acc1_ref, sta_ref, stb_ref = refs[8 + 2 * n_carry:14 + 2 * n_carry]
        carry_sems = refs[14 + 2 * n_carry:]
        j = pl.program_id(0)
        i = pl.program_id(1)

        if n_carry:
            @pl.when((j == 0) & (i == 0))
            def _():
                _gather_start(carry_src, carry_dst, carry_sems[:2])
                for cp in _gather_own(carry_src, carry_dst, carry_sems[2]):
                    cp.start()

        @pl.when(i == 0)
        def _():
            def prep(ch, carry):
                rows = pl.ds(pl.multiple_of(ch * chunk, chunk), chunk)
                kk = k_ref[rows, :].astype(F32)
                cc = c_ref[rows, :]
                vt = jnp.transpose(v_ref[rows, :].astype(F32))
                lane = lax.broadcasted_iota(jnp.int32, (chunk, PAIR), 1)
                sub = lax.broadcasted_iota(jnp.int32, (PAIR, chunk), 0)
                for h in range(2):
                    ccol = jnp.sum(jnp.where(lane == 2 * j + h, cc, 0.0), axis=1, keepdims=True)
                    kx = jnp.where(_own(lane, h), kk, 0.0)
                    for t, part in enumerate(_split3(-ccol)):
                        kx = jnp.where(lane == _spare(h) + t, part, kx)
                    kx_ref[h, rows, :] = kx.astype(BF16)
                    vt_ref[h, :, rows] = jnp.where(_own(sub, h), vt,
                                                   jnp.where(sub == _spare(h), 1.0, 0.0)).astype(BF16)
                return carry

            lax.fori_loop(0, s // chunk, prep, 0)

        qq = q_ref[...].astype(F32)
        lane_q = lax.broadcasted_iota(jnp.int32, (tq, PAIR), 1)
        qx = []
        for h in range(2):
            ones = (lane_q >= _spare(h)) & (lane_q < _spare(h) + 3)
            qx.append(jnp.transpose(jnp.where(_own(lane_q, h), qq, jnp.where(ones, 1.0, 0.0))).astype(BF16))
        accs = (acc0_ref, acc1_ref)
        for acc in accs:
            acc[...] = jnp.zeros_like(acc)

        def rows_of(kb):
            return pl.ds(pl.multiple_of(kb * tk, tk), tk)

        def scores(kb, dst):
            for h in range(2):
                dst[h] = jnp.dot(kx_ref[h, rows_of(kb), :], qx[h], preferred_element_type=F32)

        def consume(kb, src, m, masked):
            m_out, alpha, pv = [], [], []
            for h in range(2):
                sh = src[h]
                if masked:
                    kpos = kb * tk + lax.broadcasted_iota(jnp.int32, (tk, tq), 0)
                    qpos = i * tq + lax.broadcasted_iota(jnp.int32, (tk, tq), 1)
                    sh = jnp.where(qpos >= kpos, sh, NEG)
                m_new = jnp.maximum(m[h], jnp.max(sh, axis=0, keepdims=True))
                pt = jnp.exp(sh - m_new).astype(BF16)
                alpha.append(jnp.exp(m[h] - m_new))
                pv.append(jnp.dot(vt_ref[h, :, rows_of(kb)], pt, preferred_element_type=F32))
                m_out.append(m_new)
            for h in range(2):
                accs[h][...] = accs[h][...] * alpha[h] + pv[h]
            return tuple(m_out)

        n_full = (i * tq) // tk

        def two_blocks(t, m):
            kb = 2 * t
            scores(kb + 1, stb_ref)
            m = consume(kb, sta_ref, m, False)
            scores(kb + 2, sta_ref)
            return consume(kb + 1, stb_ref, m, False)

        def odd_tail(m):
            scores(n_full, stb_ref)
            m = consume(n_full - 1, sta_ref, m, False)
            return consume(n_full, stb_ref, m, True)

        def even_tail(m):
            return consume(n_full, sta_ref, m, True)

        scores(0, sta_ref)
        m0 = jnp.full((1, tq), NEG, F32)
        m = lax.fori_loop(0, n_full // 2, two_blocks, (m0, m0))
        m = lax.cond(lax.rem(n_full, 2) == 1, odd_tail, even_tail, m)

        sub_o = lax.broadcasted_iota(jnp.int32, (PAIR, tq), 0)
        den = [accs[h][_spare(h):_spare(h) + 1, :] for h in range(2)]
        ot = jnp.where(sub_o < HEAD_DIM, acc0_ref[...] / den[0], acc1_ref[...] / den[1])
        o = jnp.transpose(ot)
        o_ref[...] = o.astype(BF16)
        gate = ag_ref[...].astype(F32)
        ma_ref[...] = (o * (gate * _sigmoid(gate))).astype(BF16)
        sub8 = lax.broadcasted_iota(jnp.int32, (8, tq), 0)
        lse_ref[...] = jnp.where(sub8 == 0, m[0] + jnp.log(den[0]), m[1] + jnp.log(den[1]))

        if n_carry:
            @pl.when((j == N_PAIRS - 1) & (i == s // tq - 1))
            def _():
                _gather_finish(carry_src, carry_dst, carry_sems[:2])
                for cp in _gather_own(carry_src, carry_dst, carry_sems[2]):
                    cp.wait()

    carried = [] if carry is None else list(carry)
    carry_sem = ([pltpu.SemaphoreType.DMA((GATHER_SEMS,))] * 2 + [pltpu.SemaphoreType.DMA((2,))]) if n_carry else []
    return pl.pallas_call(
        body, name="attn_fwd_gather" if n_carry else "attn_fwd", grid=(N_PAIRS, s // tq),
        in_specs=[pl.BlockSpec((tq, PAIR), lambda j, i: (i, j)),
                  pl.BlockSpec((s, PAIR), lambda j, i: (0, j)),
                  pl.BlockSpec((s, PAIR), lambda j, i: (0, j)),
                  pl.BlockSpec((s, 128), lambda j, i: (0, 0)),
                  pl.BlockSpec((tq, PAIR), lambda j, i: (i, j))] + [ANY] * n_carry,
        out_specs=[pl.BlockSpec((tq, PAIR), lambda j, i: (i, j)),
                   pl.BlockSpec((tq, PAIR), lambda j, i: (i, j)),
                   pl.BlockSpec((None, 8, tq), lambda j, i: (j, 0, i))] + [ANY] * n_carry,
        out_shape=[jax.ShapeDtypeStruct((s, ATTN_WIDTH), BF16), jax.ShapeDtypeStruct((s, ATTN_WIDTH), BF16),
                   jax.ShapeDtypeStruct((N_PAIRS, 8, s), F32)] + (_gather_shapes(*carried) if n_carry else []),
        scratch_shapes=[pltpu.VMEM((2, s, PAIR), BF16), pltpu.VMEM((2, PAIR, s), BF16),
                        pltpu.VMEM((PAIR, tq), F32), pltpu.VMEM((PAIR, tq), F32),
                        pltpu.VMEM((2, tk, tq), F32), pltpu.VMEM((2, tk, tq), F32)] + carry_sem,
        compiler_params=_params(("arbitrary", "arbitrary")),
    )(q, k, v, c, ag, *carried)


def _outproj_fwd(x, po, ma, wo, tm=512):
    s = x.shape[0]

    def body(x_ref, po_ref, ma_ref, wo_ref, xn_ref):
        xn_ref[...] = (x_ref[...]
                       + jnp.dot(po_ref[...], wo_ref[0:POOL_WIDTH, :], preferred_element_type=F32)
                       + jnp.dot(ma_ref[...], wo_ref[POOL_WIDTH:, :], preferred_element_type=F32))

    return pl.pallas_call(
        body, name="outproj_fwd", grid=(s // tm,),
        in_specs=[_row_spec(tm, D_MODEL), _row_spec(tm, 512), _row_spec(tm, 512),
                  _full_spec((D_MODEL, D_MODEL))],
        out_specs=_row_spec(tm, D_MODEL),
        out_shape=jax.ShapeDtypeStruct((s, D_MODEL), F32),
        compiler_params=_params(("parallel",)),
    )(x, po, ma, wo)


def _loss_head(x, tgt, g, tm=512):
    s = x.shape[0]

    def body(x_ref, t_ref, g_ref, dx_ref, dg_ref, loss_ref):
        @pl.when(pl.program_id(0) == 0)
        def _():
            dg_ref[...] = jnp.zeros_like(dg_ref)
            loss_ref[...] = jnp.zeros_like(loss_ref)

        xf = x_ref[...]
        r = lax.rsqrt(jnp.mean(xf * xf, axis=-1, keepdims=True) + RMS_EPS)
        xh = xf * r
        gg = g_ref[...]
        e = xh * gg - t_ref[...]
        loss_ref[...] += 0.5 * jnp.sum(jnp.mean(e * e, axis=-1, keepdims=True))
        dy = e * (1.0 / D_MODEL)
        u = dy * gg
        dx_ref[...] = r * (u - xh * jnp.mean(xh * u, axis=-1, keepdims=True))
        dg_ref[...] += jnp.sum(dy * xh, axis=0, keepdims=True)

    return pl.pallas_call(
        body, name="loss_head", grid=(s // tm,),
        in_specs=[_row_spec(tm, D_MODEL), _row_spec(tm, D_MODEL), _full_spec((1, D_MODEL))],
        out_specs=[_row_spec(tm, D_MODEL), _full_spec((1, D_MODEL)), _full_spec((8, 128))],
        out_shape=[jax.ShapeDtypeStruct((s, D_MODEL), F32), jax.ShapeDtypeStruct((1, D_MODEL), F32),
                   jax.ShapeDtypeStruct((8, 128), F32)],
        compiler_params=_params(("arbitrary",)),
    )(x, tgt, g)


def _outproj_bwd(dx, po, ma, ag, o, wo, head_sel, tm=512):
    s = dx.shape[0]

    def body(dx_ref, po_ref, ma_ref, ag_ref, o_ref, wo_ref, sel_ref, dmp_ref, do_ref, dag_ref, dt_ref, gwo_ref):
        @pl.when(pl.program_id(0) == 0)
        def _():
            gwo_ref[...] = jnp.zeros_like(gwo_ref)

        dxb = dx_ref[...].astype(BF16)
        dm = lax.dot_general(dxb, wo_ref[...], NT, preferred_element_type=F32)
        dmp_ref[...] = dm[:, 0:POOL_WIDTH].astype(BF16)
        dma = dm[:, POOL_WIDTH:]
        gate = ag_ref[...].astype(F32)
        of = o_ref[...].astype(F32)
        sg = _sigmoid(gate)
        do = dma * (gate * sg)
        do_ref[...] = do.astype(BF16)
        dag_ref[...] = (dma * of * (sg * (1.0 + gate * (1.0 - sg)))).astype(BF16)
        prod = do * of
        hi = prod.astype(BF16)
        lo = (prod - hi.astype(F32)).astype(BF16)
        dt_ref[...] = (lax.dot_general(sel_ref[...], hi, NT, preferred_element_type=F32)
                       + lax.dot_general(sel_ref[...], lo, NT, preferred_element_type=F32))
        gwo_ref[0:POOL_WIDTH, :] += lax.dot_general(po_ref[...], dxb, TN, preferred_element_type=F32)
        gwo_ref[POOL_WIDTH:, :] += lax.dot_general(ma_ref[...], dxb, TN, preferred_element_type=F32)

    act = jax.ShapeDtypeStruct((s, 512), BF16)
    return pl.pallas_call(
        body, name="outproj_bwd", grid=(s // tm,),
        in_specs=[_row_spec(tm, D_MODEL)] + [_row_spec(tm, 512)] * 4
                 + [_full_spec((D_MODEL, D_MODEL)), _full_spec((8, ATTN_WIDTH))],
        out_specs=[_row_spec(tm, 512)] * 3 + [pl.BlockSpec((8, tm), lambda i: (0, i)),
                                              _full_spec((D_MODEL, D_MODEL))],
        out_shape=[act, act, act, jax.ShapeDtypeStruct((8, s), F32),
                   jax.ShapeDtypeStruct((D_MODEL, D_MODEL), F32)],
        compiler_params=_params(("arbitrary",)),
    )(dx, po, ma, ag, o, wo, head_sel)


def _attn_bwd(q, k, v, do, c, lse, dt, carry=None, tq=256, chunk=512):
    s = q.shape[0]
    tk = 2 * tq
    nq = s // tq
    n_carry = 0 if carry is None else len(carry)

    def body(*refs):
        q_ref, k_ref, v_ref, do_ref, c_ref, lse_ref, dt_ref = refs[:7]
        carry_src = refs[7:7 + n_carry]
        dq_ref, dk_ref, dv_ref, dcs_ref, drs_ref = refs[7 + n_carry:12 + n_carry]
        carry_dst = refs[12 + n_carry:12 + 2 * n_carry]
        (qxt_ref, doxt_ref, qm_ref, dox_ref, dqt0_ref, dqt1_ref, dk0_ref, dk1_ref, dvx_ref,
         kx_ref, vx_ref, ktm_ref, sa_ref, sb_ref) = refs[12 + 2 * n_carry:26 + 2 * n_carry]
        carry_sems = refs[26 + 2 * n_carry:]
        carry_big = tuple(zip(carry_src, carry_dst))
        j = pl.program_id(0)
        b = pl.program_id(1)
        dqts = (dqt0_ref, dqt1_ref)
        dks = (dk0_ref, dk1_ref)

        if n_carry:
            @pl.when((j == 0) & (b == 0))
            def _():
                _scatter_start(carry_big, None, carry_sems)

        @pl.when(b == 0)
        def _():
            def prep(ch, carry):
                rows = pl.ds(pl.multiple_of(ch * chunk, chunk), chunk)
                qq = q_ref[rows, :].astype(F32)
                dd = do_ref[rows, :].astype(F32)
                qt = jnp.transpose(qq)
                ddt = jnp.transpose(dd)
                lane = lax.broadcasted_iota(jnp.int32, (chunk, PAIR), 1)
                sub = lax.broadcasted_iota(jnp.int32, (PAIR, chunk), 0)
                for h in range(2):
                    sp = _spare(h)
                    qm_ref[h, rows, :] = jnp.where(_own(lane, h), qq, jnp.where(lane == sp, 1.0, 0.0)).astype(BF16)
                    dox_ref[h, rows, :] = jnp.where(_own(lane, h), dd, 0.0).astype(BF16)
                    qx = jnp.where(_own(sub, h), qt, jnp.where((sub >= sp) & (sub < sp + 3), 1.0, 0.0))
                    for t, part in enumerate(_split3(-lse_ref[pl.ds(h, 1), rows])):
                        qx = jnp.where(sub == sp + 3 + t, part, qx)
                    qxt_ref[h, :, rows] = qx.astype(BF16)
                    dx = jnp.where(_own(sub, h), ddt, 0.0)
                    for t, part in enumerate(_split3(-dt_ref[pl.ds(h, 1), rows])):
                        dx = jnp.where(sub == sp + t, part, dx)
                    doxt_ref[h, :, rows] = dx.astype(BF16)
                return carry

            lax.fori_loop(0, s // chunk, prep, 0)
            for ref in dqts:
                ref[...] = jnp.zeros_like(ref)

        kk = k_ref[...].astype(F32)
        vv = v_ref[...].astype(F32)
        cc = c_ref[...]
        kt = jnp.transpose(kk)
        lane = lax.broadcasted_iota(jnp.int32, (tk, PAIR), 1)
        sub = lax.broadcasted_iota(jnp.int32, (PAIR, tk), 0)
        for h in range(2):
            sp = _spare(h)
            ccol = jnp.sum(jnp.where(lane == 2 * j + h, cc, 0.0), axis=1, keepdims=True)
            kx = jnp.where(_own(lane, h), kk, jnp.where((lane >= sp + 3) & (lane < sp + 6), 1.0, 0.0))
            for t, part in enumerate(_split3(-ccol)):
                kx = jnp.where(lane == sp + t, part, kx)
            kx_ref[h] = kx.astype(BF16)
            vx_ref[h] = jnp.where(_own(lane, h), vv, jnp.where((lane >= sp) & (lane < sp + 3), 1.0, 0.0)).astype(BF16)
            ktm_ref[h] = jnp.where(_own(sub, h), kt, jnp.where(sub == sp, 1.0, 0.0)).astype(BF16)
        for ref in dks:
            ref[...] = jnp.zeros_like(ref)
        dvx_ref[...] = jnp.zeros_like(dvx_ref)

        def cols_of(i):
            return pl.ds(pl.multiple_of(i * tq, tq), tq)

        def scores(i, dst):
            cols = cols_of(jnp.minimum(i, nq - 1))
            for h in range(2):
                dst[h] = jnp.dot(kx_ref[h], qxt_ref[h, :, cols], preferred_element_type=F32)
                dst[2 + h] = jnp.dot(vx_ref[h], doxt_ref[h, :, cols], preferred_element_type=F32)

        def consume(i, src, masked):
            cols = cols_of(i)
            for h in range(2):
                arg = src[h]
                if masked:
                    kpos = b * tk + lax.broadcasted_iota(jnp.int32, (tk, tq), 0)
                    qpos = i * tq + lax.broadcasted_iota(jnp.int32, (tk, tq), 1)
                    arg = jnp.where(qpos >= kpos, arg, NEG)
                pt = jnp.exp(arg)
                dst = (pt * src[2 + h]).astype(BF16)
                dvx_ref[...] += jnp.dot(pt.astype(BF16), dox_ref[h, cols, :], preferred_element_type=F32)
                dks[h][...] += jnp.dot(dst, qm_ref[h, cols, :], preferred_element_type=F32)
                dqts[h][:, cols] += jnp.dot(ktm_ref[h], dst, preferred_element_type=F32)

        i0 = 2 * b

        def two_blocks(t, masked):
            i = i0 + 2 * t
            scores(i + 1, sb_ref)
            consume(i, sa_ref, masked)
            scores(i + 2, sa_ref)
            consume(i + 1, sb_ref, masked)

        def loop_body(t, carry):
            two_blocks(t, False)
            return carry

        scores(i0, sa_ref)
        two_blocks(0, True)
        lax.fori_loop(1, (nq - i0) // 2, loop_body, 0)

        dv_ref[...] = dvx_ref[...].astype(BF16)
        dk_ref[...] = jnp.where(lane < HEAD_DIM, dk0_ref[...], dk1_ref[...]).astype(BF16)
        dcs_ref[...] = jnp.where(lane == _spare(0), dk0_ref[...], jnp.where(lane == _spare(1), dk1_ref[...], 0.0))

        @pl.when(b == s // tk - 1)
        def _():
            sub8 = lax.broadcasted_iota(jnp.int32, (8, s), 0)
            drs_ref[...] = jnp.where(sub8 == 0, dqt0_ref[_spare(0):_spare(0) + 1, :],
                                     jnp.where(sub8 == 1, dqt1_ref[_spare(1):_spare(1) + 1, :], 0.0))
            sub_c = lax.broadcasted_iota(jnp.int32, (PAIR, chunk), 0)
            for ch in range(s // chunk):
                rows = pl.ds(ch * chunk, chunk)
                both = jnp.where(sub_c < HEAD_DIM, dqt0_ref[:, rows], dqt1_ref[:, rows])
                dq_ref[rows, :] = (jnp.transpose(both) * Q_SCALE).astype(BF16)

        if n_carry:
            @pl.when((j == N_PAIRS - 1) & (b == s // tk - 1))
            def _():
                _scatter_finish(carry_big, None, carry_sems)

    act = jax.ShapeDtypeStruct((s, ATTN_WIDTH), BF16)
    pair_rows = pl.BlockSpec((s, PAIR), lambda j, b: (0, j))
    pair_blk = pl.BlockSpec((tk, PAIR), lambda j, b: (b, j))
    stat = pl.BlockSpec((None, 2, s), lambda j, b: (j, 0, 0))
    carried = [] if carry is None else list(carry)
    carry_sem = [pltpu.SemaphoreType.DMA((SCATTER_SEMS,))] * 2 if n_carry else []
    return pl.pallas_call(
        body, name="attn_bwd_scatter" if n_carry else "attn_bwd", grid=(N_PAIRS, s // tk),
        in_specs=[pair_rows, pair_blk, pair_blk, pair_rows,
                  pl.BlockSpec((tk, 128), lambda j, b: (b, 0)),
                  pl.BlockSpec((None, 8, s), lambda j, b: (j, 0, 0)), stat] + [ANY] * n_carry,
        out_specs=[pair_rows, pair_blk, pair_blk, pair_blk,
                   pl.BlockSpec((None, 8, s), lambda j, b: (j, 0, 0))] + [ANY] * n_carry,
        out_shape=[act, act, act, jax.ShapeDtypeStruct((s, N_PAIRS * 128), F32),
                   jax.ShapeDtypeStruct((N_PAIRS, 8, s), F32)] + (_scatter_shapes(*carried) if n_carry else []),
        scratch_shapes=[pltpu.VMEM((2, PAIR, s), BF16), pltpu.VMEM((2, PAIR, s), BF16),
                        pltpu.VMEM((2, s, PAIR), BF16), pltpu.VMEM((2, s, PAIR), BF16),
                        pltpu.VMEM((PAIR, s), F32), pltpu.VMEM((PAIR, s), F32),
                        pltpu.VMEM((tk, PAIR), F32), pltpu.VMEM((tk, PAIR), F32), pltpu.VMEM((tk, PAIR), F32),
                        pltpu.VMEM((2, tk, PAIR), BF16), pltpu.VMEM((2, tk, PAIR), BF16),
                        pltpu.VMEM((2, PAIR, tk), BF16),
                        pltpu.VMEM((4, tk, tq), F32), pltpu.VMEM((4, tk, tq), F32)] + carry_sem,
        compiler_params=_params(("arbitrary", "arbitrary")),
    )(q, k, v, do, c, lse, dt, *carried)


def _forget_bwd(dcs, drs, lf, tm=256):
    s = lf.shape[0]
    n = s // tm

    def body(dcs_ref, drs_ref, lf_ref, dpf_ref, dfb_ref, carry_ref):
        @pl.when(pl.program_id(0) == 0)
        def _():
            carry_ref[...] = jnp.zeros_like(carry_ref)
            dfb_ref[...] = jnp.zeros_like(dfb_ref)

        lane = lax.broadcasted_iota(jnp.int32, (tm, 128), 1)
        dc = drs_ref[...]
        for pj in range(N_PAIRS):
            blk = dcs_ref[:, pj * 128:(pj + 1) * 128]
            for h in range(2):
                head = 2 * pj + h
                moved = pltpu.roll(blk, (head - _spare(h)) % 128, 1) if head != _spare(h) else blk
                dc = dc - jnp.where(lane == head, moved, 0.0)
        dlf = _scan_rows(dc, reverse=True) + carry_ref[0:1, :]
        carry_ref[...] = dlf[0:8, :]
        dz = jnp.where(lane < 8, dlf * (1.0 - jnp.exp(lf_ref[...])), 0.0)
        dpf_ref[...] = dz.astype(BF16)
        dfb_ref[...] += jnp.sum(dz, axis=0, keepdims=True)

    return pl.pallas_call(
        body, name="forget_bwd", grid=(n,),
        in_specs=[pl.BlockSpec((tm, N_PAIRS * 128), lambda i: (n - 1 - i, 0)),
                  pl.BlockSpec((tm, 128), lambda i: (n - 1 - i, 0)),
                  pl.BlockSpec((tm, 128), lambda i: (n - 1 - i, 0))],
        out_specs=[pl.BlockSpec((tm, 128), lambda i: (n - 1 - i, 0)), _full_spec((1, 128))],
        out_shape=[jax.ShapeDtypeStruct((s, 128), BF16), jax.ShapeDtypeStruct((1, 128), F32)],
        scratch_shapes=[pltpu.VMEM((8, 128), F32)],
        compiler_params=_params(("arbitrary",)),
    )(dcs, drs, lf)


def _pool_bwd(pu, pg, dmp, pw, ps, tm=512):
    s = pu.shape[0]
    hb = tm // HALO
    n = s // tm
    last_halo = s // HALO - 1

    def body(pu_ref, halo_ref, pg_ref, dmp_ref, pgn_ref, dmpn_ref, pw_ref, ps_ref,
             dpu_ref, dpg_ref, gpw_ref, gps_ref):
        i = pl.program_id(0)

        @pl.when(i == 0)
        def _():
            gpw_ref[...] = jnp.zeros_like(gpw_ref)
            gps_ref[...] = jnp.zeros_like(gps_ref)

        u = pu_ref[...].astype(F32)
        halo = jnp.where(i == 0, 0.0, halo_ref[...].astype(F32))
        d = _pool_delta(u, halo, i * tm)
        cnt = _window_counts(i * tm, tm)
        cnt_next = _window_counts((i + 1) * tm, HALO)
        e_parts, dd_parts = [], []
        for gi in range(len(POOL_WINDOWS)):
            cols = slice(gi * GROUP_DIM, (gi + 1) * GROUP_DIM)
            wg = pw_ref[gi]
            scale = ps_ref[:, cols]
            db = d[gi].astype(BF16)
            z = jnp.dot(db, wg, preferred_element_type=F32)
            gate = pg_ref[:, cols].astype(F32)
            sg = _sigmoid(gate)
            dm = dmp_ref[:, cols].astype(F32)
            dy = dm * (gate * sg)
            dpg_ref[:, cols] = (dm * (z * scale) * (sg * (1.0 + gate * (1.0 - sg)))).astype(BF16)
            gps_ref[:, cols] += jnp.sum(dy * z, axis=0, keepdims=True)
            dz = (dy * scale).astype(BF16)
            gpw_ref[gi] += lax.dot_general(db, dz, TN, preferred_element_type=F32)
            dd = lax.dot_general(dz, wg, NT, preferred_element_type=F32)
            gate_n = pgn_ref[:, cols].astype(F32)
            dz_n = (dmpn_ref[:, cols].astype(F32) * (gate_n * _sigmoid(gate_n)) * scale).astype(BF16)
            dd_n = lax.dot_general(dz_n, wg, NT, preferred_element_type=F32)
            dd_n = jnp.where(i == n - 1, 0.0, dd_n)
            dd_parts.append(dd)
            e_parts.append(jnp.concatenate([dd / cnt[gi], dd_n / cnt_next[gi]], axis=0))
        lead = _window_sums(jnp.concatenate(e_parts, axis=1), forward=True)
        for gi in range(len(POOL_WINDOWS)):
            cols = slice(gi * GROUP_DIM, (gi + 1) * GROUP_DIM)
            dpu_ref[:, cols] = (lead[gi][0:tm, :] - dd_parts[gi]).astype(BF16)

    act = jax.ShapeDtypeStruct((s, 512), BF16)
    prev_halo = pl.BlockSpec((HALO, 512), lambda i: (jnp.maximum(i * hb - 1, 0), 0))
    next_halo = pl.BlockSpec((HALO, 512), lambda i: (jnp.minimum((i + 1) * hb, last_halo), 0))
    return pl.pallas_call(
        body, name="pool_bwd", grid=(n,),
        in_specs=[_row_spec(tm, 512), prev_halo, _row_spec(tm, 512), _row_spec(tm, 512), next_halo, next_halo,
                  _full_spec((4, GROUP_DIM, GROUP_DIM)), _full_spec((1, 512))],
        out_specs=[_row_spec(tm, 512), _row_spec(tm, 512), _full_spec((4, GROUP_DIM, GROUP_DIM)),
                   _full_spec((1, 512))],
        out_shape=[act, act, jax.ShapeDtypeStruct((4, GROUP_DIM, GROUP_DIM), F32),
                   jax.ShapeDtypeStruct((1, 512), F32)],
        compiler_params=_params(("arbitrary",)),
    )(pu, pu, pg, dmp, pg, dmp, pw, ps)


def _inproj_bwd_x(dsec, w, x, g, dxo, exchange=None, scatter=None, tm=512):
    s = x.shape[0]
    n_steps = s // tm
    carried = list(exchange or scatter or ())
    n_carry = len(carried)

    def body(*refs):
        d_refs = refs[0:7]
        w_ref, x_ref, g_ref, dxo_ref = refs[7:11]
        carry_src = refs[11:11 + n_carry]
        dxi_ref, dg_ref = refs[11 + n_carry:13 + n_carry]
        carry_dst = refs[13 + n_carry:13 + 2 * n_carry]
        sems = refs[13 + 2 * n_carry:]
        step = pl.program_id(0)

        def comm(phase):
            if exchange is not None:
                (_exchange_start if phase == 0 else _exchange_finish)(carry_src, carry_dst, None, sems)
            else:
                big = tuple(zip(carry_src[:2], carry_dst[:2]))
                (_scatter_start if phase == 0 else _scatter_finish)(big, (carry_src[2], carry_dst[2]), sems)

        @pl.when(step == 0)
        def _():
            dg_ref[...] = jnp.zeros_like(dg_ref)
            if n_carry:
                comm(0)

        dh = None
        for d_ref, (off, ncol) in zip(d_refs, SECTIONS):
            t = lax.dot_general(d_ref[...], w_ref[:, off:off + ncol], NT, preferred_element_type=F32)
            dh = t if dh is None else dh + t
        xf = x_ref[...]
        r = lax.rsqrt(jnp.mean(xf * xf, axis=-1, keepdims=True) + RMS_EPS)
        xh = xf * r
        u = dh * g_ref[...]
        dxi_ref[...] = dxo_ref[...] + r * (u - xh * jnp.mean(xh * u, axis=-1, keepdims=True))
        dg_ref[...] += jnp.sum(dh * xh, axis=0, keepdims=True)

        if n_carry:
            @pl.when(step == n_steps - 1)
            def _():
                comm(1)

    if exchange is not None:
        name, carry_shapes, n_sems = "inproj_bwd_x_exchange", _exchange_shapes(*exchange), 3
    elif scatter is not None:
        name, n_sems = "inproj_bwd_x_scatter", SCATTER_SEMS
        carry_shapes = _scatter_shapes(*scatter[:2]) + [jax.ShapeDtypeStruct((3, SMALL_HALF, 128), F32)]
    else:
        name, carry_shapes, n_sems = "inproj_bwd_x", [], 0
    return pl.pallas_call(
        body, name=name, grid=(n_steps,),
        in_specs=[_row_spec(tm, ncol) for _, ncol in SECTIONS]
                 + [_full_spec((D_MODEL, IN_COLS_PAD)), _row_spec(tm, D_MODEL), _full_spec((1, D_MODEL)),
                    _row_spec(tm, D_MODEL)] + [ANY] * n_carry,
        out_specs=[_row_spec(tm, D_MODEL), _full_spec((1, D_MODEL))] + [ANY] * n_carry,
        out_shape=[jax.ShapeDtypeStruct((s, D_MODEL), F32), jax.ShapeDtypeStruct((1, D_MODEL), F32)] + carry_shapes,
        scratch_shapes=[pltpu.SemaphoreType.DMA((n_sems,))] * 2 if n_carry else [],
        compiler_params=_params(("arbitrary",)),
    )(*dsec, w, x, g, dxo, *carried)


def _inproj_bwd_w(h, dsec, tm=512):
    s = h.shape[0]
    n_steps = s // tm

    def body(*refs):
        h_ref = refs[0]
        d_refs = refs[1:8]
        out_ref, gw_ref = refs[8:]
        step = pl.program_id(0)

        @pl.when(step == 0)
        def _():
            gw_ref[...] = jnp.zeros_like(gw_ref)

        hh = h_ref[...]
        for d_ref, (off, ncol) in zip(d_refs, SECTIONS):
            gw_ref[:, off:off + ncol] += lax.dot_general(hh, d_ref[...], TN, preferred_element_type=F32)

        @pl.when(step == n_steps - 1)
        def _():
            for k in range(N_CHIPS):
                out_ref[k] = gw_ref[:, k * SHARD_COLS:(k + 1) * SHARD_COLS]

    return pl.pallas_call(
        body, name="inproj_bwd_w", grid=(n_steps,),
        in_specs=[_row_spec(tm, D_MODEL)] + [_row_spec(tm, ncol) for _, ncol in SECTIONS],
        out_specs=_full_spec((N_CHIPS, D_MODEL, SHARD_COLS)),
        out_shape=jax.ShapeDtypeStruct((N_CHIPS, D_MODEL, SHARD_COLS), F32),
        scratch_shapes=[pltpu.VMEM((D_MODEL, IN_COLS_PAD), F32)],
        compiler_params=_params(("arbitrary",), vmem_mb=60),
    )(h, *dsec)


def _elementwise(fn, name, n_out, arrays, tm):
    shape = arrays[0].shape
    rest = shape[1:]
    n_in = len(arrays)

    def body(*refs):
        outs = fn(*[r[...] for r in refs[:n_in]])
        for r, val in zip(refs[n_in:], outs):
            r[...] = val

    spec = pl.BlockSpec((tm,) + rest, lambda i: (i,) + (0,) * len(rest))
    return pl.pallas_call(
        body, name=name, grid=(shape[0] // tm,),
        in_specs=[spec] * n_in, out_specs=[spec] * n_out, out_shape=[jax.ShapeDtypeStruct(shape, F32)] * n_out,
        compiler_params=_params(("parallel",)),
    )(*arrays)


def _add2(a, b, name, tm):
    return _elementwise(lambda p, q: (p + q,), name, 1, [a, b], tm)[0]


def _chip_sum(place, g, r, name, tm):
    n_layers, _, rows, cols = g.shape
    half = rows // 2
    nb = half // tm

    def body(place_ref, g_ref, r_ref, o_ref):
        o_ref[...] = (g_ref[...] + r_ref[...]).astype(BF16)

    blk = (None, None, tm, cols)
    return pl.pallas_call(
        body, name=name,
        grid_spec=pltpu.PrefetchScalarGridSpec(
            num_scalar_prefetch=1, grid=(n_layers, N_CHIPS, nb),
            in_specs=[pl.BlockSpec(blk, lambda l, k, i, p: (l, k, p[1] * nb + i, 0)),
                      pl.BlockSpec(blk, lambda l, k, i, p: (l, k, i, 0))],
            out_specs=pl.BlockSpec(blk, lambda l, k, i, p: (l, k, i, 0))),
        out_shape=jax.ShapeDtypeStruct((n_layers, N_CHIPS, half, cols), BF16),
        compiler_params=_params(("parallel", "parallel", "parallel")),
    )(place, g, r)


def _mesh_sum(place, g, r, got, into, layer, name, tm):
    _, _, rows, cols = g.shape
    half = rows // 2
    nb = half // tm

    def body(place_ref, g_ref, r_ref, got_ref, into_ref, o_ref):
        own = g_ref[...] + r_ref[...]
        o_ref[...] = (own + got_ref[0].astype(F32)) + (got_ref[1].astype(F32) + got_ref[2].astype(F32))

    blk = (None, None, tm, cols)
    return pl.pallas_call(
        body, name=name,
        grid_spec=pltpu.PrefetchScalarGridSpec(
            num_scalar_prefetch=1, grid=(nb,),
            in_specs=[pl.BlockSpec(blk, lambda i, p: (0, p[0], p[1] * nb + i, 0)),
                      pl.BlockSpec(blk, lambda i, p: (0, p[0], i, 0)),
                      pl.BlockSpec((None, 3, tm, cols), lambda i, p: (0, 0, i, 0)),
                      ANY],
            out_specs=pl.BlockSpec((None, tm, cols), lambda i, p: (layer, p[1] * nb + i, 0))),
        out_shape=jax.ShapeDtypeStruct(into.shape, F32),
        input_output_aliases={4: 0},
        compiler_params=_params(("parallel",)),
    )(place, g, r, got, into)


def _mesh_sum_small(place, ps, got, tm):
    nb = SMALL_HALF // tm

    def body(place_ref, ps_ref, got_ref, o_ref):
        o_ref[...] = (ps_ref[...] + got_ref[0]) + (got_ref[1] + got_ref[2])

    return pl.pallas_call(
        body, name="mesh_sum_small",
        grid_spec=pltpu.PrefetchScalarGridSpec(
            num_scalar_prefetch=1, grid=(nb,),
            in_specs=[pl.BlockSpec((tm, 128), lambda i, p: (p[1] * nb + i, 0)),
                      pl.BlockSpec((3, tm, 128), lambda i, p: (0, i, 0))],
            out_specs=pl.BlockSpec((tm, 128), lambda i, p: (p[1] * nb + i, 0))),
        out_shape=jax.ShapeDtypeStruct((SMALL_ROWS, 128), F32),
        compiler_params=_params(("parallel",)),
    )(place, ps, got)


def _adamw(g, w, m, v, name, tm):
    def fn(g, w, m, v):
        m = ADAM_B1 * m + (1.0 - ADAM_B1) * g
        v = ADAM_B2 * v + (1.0 - ADAM_B2) * (g * g)
        m_hat = m / (1.0 - ADAM_B1 ** ADAM_STEP)
        v_hat = v / (1.0 - ADAM_B2 ** ADAM_STEP)
        delta = -ADAM_LR * (m_hat / (jnp.sqrt(v_hat) + ADAM_EPS) + ADAM_WD * w)
        return g, delta, m, v

    return _elementwise(fn, name, 4, [g, w, m, v], tm)


def _position():
    x, y, c = lax.axis_index("x"), lax.axis_index("y"), lax.axis_index("c")
    other_chips = [(x, 1 - y), (1 - x, y), (1 - x, 1 - y)]
    return x, y, c, other_chips


def _remote(src, dst, sems, k, to):
    send_sems, recv_sems = sems
    return pltpu.make_async_remote_copy(src_ref=src, dst_ref=dst, send_sem=send_sems.at[k],
                                        recv_sem=recv_sems.at[k], device_id=to, device_id_type=MESH)


def _comm_call(body, name, n_in, out_shape, n_remote, aliases=None):
    return pl.pallas_call(
        body, name=name, in_specs=[ANY] * n_in, out_specs=[ANY] * len(out_shape), out_shape=out_shape,
        scratch_shapes=[pltpu.SemaphoreType.DMA((n_remote,)), pltpu.SemaphoreType.DMA((n_remote,))],
        input_output_aliases=aliases or {},
    )


def _halves(ref_rows, c):
    half = ref_rows // 2
    return pl.ds(half * c, half), pl.ds(half * (1 - c), half)


def _gather_weights(wi, wo):
    def body(wi_ref, wo_ref, gi_ref, go_ref, send_sems, recv_sems):
        _gather_start((wi_ref, wo_ref), (gi_ref, go_ref), (send_sems, recv_sems))
        _gather_finish((wi_ref, wo_ref), (gi_ref, go_ref), (send_sems, recv_sems))

    return _comm_call(body, "gather_weights", 2, _gather_shapes(wi, wo), GATHER_SEMS)(wi, wo)


GATHER_SEMS = 12
SCATTER_SEMS = 9


def _gather_shapes(wi, wo):
    return [jax.ShapeDtypeStruct((a.shape[0], N_CHIPS) + a.shape[1:], a.dtype) for a in (wi, wo)]


def _gather_sends(srcs, dsts, sems):
    x, y, c, chips = _position()
    out = []
    for k, (cx, cy) in enumerate(chips):
        for a, (src, dst) in enumerate(zip(srcs, dsts)):
            mine, _ = _halves(src.shape[1], c)
            out.append(_remote(src.at[:, mine], dst.at[:, k + 1, mine], sems, len(srcs) * k + a, (cx, cy, c)))
    return out


def _gather_own(srcs, dsts, local_sems):
    return [pltpu.make_async_copy(src, dst.at[:, 0], local_sems.at[a]) for a, (src, dst) in enumerate(zip(srcs, dsts))]


def _gather_start(srcs, dsts, sems):
    for cp in _gather_sends(srcs, dsts, sems):
        cp.start()


def _gather_finish(srcs, dsts, sems):
    x, y, c, _ = _position()
    n = len(srcs)
    forwards = []
    for k in range(3):
        for a, dst in enumerate(dsts):
            mine, _ = _halves(dst.shape[2], c)
            blk = dst.at[:, k + 1, mine]
            _remote(blk, blk, sems, n * k + a, (x, y, c)).wait_recv()
            forwards.append(_remote(blk, blk, sems, 3 * n + n * k + a, (x, y, 1 - c)))
            forwards[-1].start()
    for k in range(3):
        for a, dst in enumerate(dsts):
            _, other = _halves(dst.shape[2], c)
            blk = dst.at[:, k + 1, other]
            _remote(blk, blk, sems, 3 * n + n * k + a, (x, y, c)).wait_recv()
    for cp in _gather_sends(srcs, dsts, sems) + forwards:
        cp.wait_send()


def _scatter_sends(big, small, sems):
    x, y, c, chips = _position()
    n = len(big) + (small is not None)
    out = []
    for k, (cx, cy) in enumerate(chips):
        ck = 2 * cx + cy
        for a, (src, dst) in enumerate(big):
            out.append(_remote(src.at[:, ck], dst.at[:, k], sems, n * k + a, (cx, cy, c)))
        if small is not None:
            half, _ = _halves(SMALL_ROWS, c)
            out.append(_remote(small[0].at[half], small[1].at[k], sems, n * k + n - 1, (cx, cy, c)))
    return out


def _scatter_start(big, small, sems):
    for cp in _scatter_sends(big, small, sems):
        cp.start()


def _scatter_finish(big, small, sems):
    x, y, c, _ = _position()
    n = len(big) + (small is not None)
    for k in range(3):
        landing = [dst.at[:, k] for _, dst in big] + ([small[1].at[k]] if small is not None else [])
        for a, blk in enumerate(landing):
            _remote(blk, blk, sems, n * k + a, (x, y, c)).wait_recv()
    for cp in _scatter_sends(big, small, sems):
        cp.wait_send()


def _exchange_shapes(gi, go):
    return [jax.ShapeDtypeStruct(a.shape[:2] + (a.shape[2] // 2, a.shape[3]), F32) for a in (gi, go)]


def _exchange_copies(srcs, dsts, small, sems):
    x, y, c, _ = _position()
    cps = [_remote(src.at[:, :, _halves(src.shape[2], c)[1]], dst, sems, a, (x, y, 1 - c))
           for a, (src, dst) in enumerate(zip(srcs, dsts))]
    if small is not None:
        cps.append(_remote(small[0], small[1], sems, len(srcs), (x, y, 1 - c)))
    return cps


def _exchange_start(srcs, dsts, small, sems):
    for cp in _exchange_copies(srcs, dsts, small, sems):
        cp.start()


def _exchange_finish(srcs, dsts, small, sems):
    for cp in _exchange_copies(srcs, dsts, small, sems):
        cp.wait()


def _exchange_sibling(gi, go, sm):
    def body(gi_ref, go_ref, sm_ref, ri_ref, ro_ref, rs_ref, send_sems, recv_sems):
        args = ((gi_ref, go_ref), (ri_ref, ro_ref), (sm_ref, rs_ref), (send_sems, recv_sems))
        _exchange_start(*args)
        _exchange_finish(*args)

    out_shape = _exchange_shapes(gi, go) + [jax.ShapeDtypeStruct(sm.shape, F32)]
    return _comm_call(body, "exchange_sibling", 3, out_shape, 3)(gi, go, sm)


def _scatter_shapes(pi, po):
    return [jax.ShapeDtypeStruct((a.shape[0], 3) + a.shape[2:], a.dtype) for a in (pi, po)]


def _scatter_chips(pi, po, ps):
    def body(pi_ref, po_ref, ps_ref, ri_ref, ro_ref, rs_ref, send_sems, recv_sems):
        big, small, sems = ((pi_ref, ri_ref), (po_ref, ro_ref)), (ps_ref, rs_ref), (send_sems, recv_sems)
        _scatter_start(big, small, sems)
        _scatter_finish(big, small, sems)

    out_shape = _scatter_shapes(pi, po) + [jax.ShapeDtypeStruct((3, SMALL_HALF, 128), F32)]
    return _comm_call(body, "scatter_chips", 3, out_shape, SCATTER_SEMS)(pi, po, ps)


def _join_sibling(fi, fo, fs):
    def body(fi_ref, fo_ref, fs_ref, oi_ref, oo_ref, os_ref, send_sems, recv_sems):
        x, y, c, _ = _position()
        sems = (send_sems, recv_sems)
        sent, got = [], []
        for a, (src, dst) in enumerate(((fi_ref, oi_ref), (fo_ref, oo_ref))):
            mine, other = _halves(src.shape[1], c)
            sent.append(_remote(src.at[:, mine], dst.at[:, mine], sems, a, (x, y, 1 - c)))
            got.append(_remote(dst.at[:, other], dst.at[:, other], sems, a, (x, y, c)))
        mine, other = _halves(SMALL_ROWS, c)
        sent.append(_remote(fs_ref.at[mine], os_ref.at[mine], sems, 2, (x, y, 1 - c)))
        got.append(_remote(os_ref.at[other], os_ref.at[other], sems, 2, (x, y, c)))
        for cp in sent:
            cp.start()
        for cp in got:
            cp.wait_recv()
        for cp in sent:
            cp.wait_send()

    out_shape = [jax.ShapeDtypeStruct(a.shape, F32) for a in (fi, fo, fs)]
    return _comm_call(body, "join_sibling", 3, out_shape, 3, aliases={0: 0, 1: 1, 2: 2})(fi, fo, fs)


SMALL_SIZES = (DEPTH * D_MODEL, D_MODEL, DEPTH * POOL_WIDTH, 128, DEPTH * 4 * GROUP_DIM * GROUP_DIM)
LOSS_SLOT = DEPTH * D_MODEL + D_MODEL + DEPTH * POOL_WIDTH + DEPTH * 8


def _pack_small(norm_g, final_g, pool_scale, forget_bias, pool_w, loss=None):
    fb = forget_bias.reshape(-1)
    if loss is not None:
        fb = jnp.concatenate([fb, loss.reshape(1)])
    fb = jnp.pad(fb, (0, 128 - fb.size))
    flat = jnp.concatenate([norm_g.reshape(-1), final_g.reshape(-1), pool_scale.reshape(-1), fb,
                            pool_w.reshape(-1)])
    return jnp.pad(flat, (0, SMALL_ROWS * 128 - flat.size)).reshape(SMALL_ROWS, 128)


def _unpack_small(packed):
    flat = packed.reshape(-1)
    offs = [0]
    for n in SMALL_SIZES:
        offs.append(offs[-1] + n)
    norm_g, final_g, pool_scale, fb, pool_w = [flat[offs[i]:offs[i + 1]] for i in range(5)]
    return (norm_g.reshape(DEPTH, D_MODEL), final_g, pool_scale.reshape(DEPTH, POOL_WIDTH),
            fb[:DEPTH * 8].reshape(DEPTH, 8), pool_w.reshape(DEPTH, 4, GROUP_DIM, GROUP_DIM))


def kernel(x, norm_g, w_in, forget_bias, pool_w, pool_scale, w_out, final_g, loss_target, m_norm_g, m_w_in, m_forget_bias, m_pool_w, m_pool_scale, m_w_out, m_final_g, v_norm_g, v_w_in, v_forget_bias, v_pool_w, v_pool_scale, v_w_out, v_final_g):
    x0 = x[0]
    tgt = loss_target[0]
    s = x0.shape[0]

    me = 2 * lax.axis_index("x") + lax.axis_index("y")
    place = jnp.stack([me, lax.axis_index("c")]).astype(jnp.int32)
    wi_b, wo_b = w_in.astype(BF16), w_out.astype(BF16)

    def chip_piece(own, got, chip):
        rel = jnp.bitwise_xor(chip, me)
        if own is None:
            return lax.dynamic_index_in_dim(got, rel, axis=1, keepdims=False)[0]
        other = lax.dynamic_index_in_dim(got, jnp.maximum(rel, 1), axis=1, keepdims=False)
        return jnp.where(rel == 0, own, other)[0]

    def layer_weights(l, gi, go, own_gathered):
        own_i, own_o = (None, None) if own_gathered else (wi_b[l:l + 1], wo_b[l:l + 1])
        pad = jnp.zeros((D_MODEL, IN_COLS_PAD - IN_COLS), BF16)
        w_i = jnp.concatenate([chip_piece(own_i, gi, k) for k in range(N_CHIPS)] + [pad], axis=-1)
        w_o = jnp.concatenate([chip_piece(own_o, go, k) for k in range(N_CHIPS)], axis=0)
        return w_i, w_o

    pool_w_b = pool_w.astype(BF16)
    fb_pad = jnp.pad(forget_bias, ((0, 0), (0, 128 - forget_bias.shape[1])))
    head_sel = (jnp.arange(ATTN_WIDTH)[None, :] // HEAD_DIM == jnp.arange(8)[:, None]).astype(BF16)

    saved, weights = [], []
    xl = x0
    gathered = _gather_weights(wi_b[0:1], wo_b[0:1])
    for l in range(DEPTH):
        w_i, w_o = layer_weights(l, *gathered, own_gathered=l > 0)
        weights.append((w_i, w_o))
        pu, pg, q, k, v, ag, lf, c, h = _inproj_fwd(xl, norm_g[l][None], w_i, fb_pad[l][None])
        po = _pool_fwd(pu, pg, pool_w_b[l], pool_scale[l][None])
        if l + 1 < DEPTH:
            ma, o, lse, *gathered = _attn_fwd(q, k, v, c, ag, carry=(wi_b[l + 1:l + 2], wo_b[l + 1:l + 2]))
        else:
            ma, o, lse = _attn_fwd(q, k, v, c, ag)
        saved.append((xl, h, pu, pg, q, k, v, ag, lf, c, po, ma, o, lse))
        xl = _outproj_fwd(xl, po, ma, w_o)

    dx, g_final, loss_part = _loss_head(xl, tgt, final_g[None])

    g_norm, g_fb, g_pw, g_ps = [], [], [], []
    parts = {}
    pending = None
    for l in reversed(range(DEPTH)):
        xl, h, pu, pg, q, k, v, ag, lf, c, po, ma, o, lse = saved[l]
        w_i, w_o = weights[l]
        dmp, do, dag, dt, gwo = _outproj_bwd(dx, po, ma, ag, o, w_o, head_sel)
        if pending is None:
            dq, dk, dv, dcs, drs = _attn_bwd(q, k, v, do, c, lse, dt.reshape(N_PAIRS, 2, s))
        else:
            dq, dk, dv, dcs, drs, qi, qo = _attn_bwd(q, k, v, do, c, lse, dt.reshape(N_PAIRS, 2, s), carry=pending)
            parts[l + 1] += (qi, qo)
        drs = jnp.pad(drs[:, 0:2, :].reshape(8, s).T, ((0, 0), (0, 120)))
        dpf, dfb = _forget_bwd(dcs, drs, lf)
        dpu, dpg, gpw, gps = _pool_bwd(pu, pg, dmp, pool_w_b[l], pool_scale[l][None])
        dsec = (dpu, dpg, dq, dk, dv, dag, dpf)
        gwi = _inproj_bwd_w(h, dsec)
        g_fb.append(dfb[0, :8])
        g_pw.append(gpw)
        g_ps.append(gps[0])
        send_i = gwi[None]
        send_o = gwo.reshape(1, N_CHIPS, SHARD_ROWS, D_MODEL)
        if l > 0:
            dx, gn, ri, ro = _inproj_bwd_x(dsec, w_i, xl, norm_g[l][None], dx, exchange=(send_i, send_o))
            g_norm.append(gn[0])
        else:
            dx, gn = _inproj_bwd_x(dsec, w_i, xl, norm_g[l][None], dx)
            g_norm.append(gn[0])
            g_norm, g_fb, g_pw, g_ps = [t[::-1] for t in (g_norm, g_fb, g_pw, g_ps)]
            send_s = _pack_small(jnp.stack(g_norm), g_final[0], jnp.stack(g_ps), jnp.stack(g_fb), jnp.stack(g_pw),
                                 loss=loss_part[0, 0])
            ri, ro, rs = _exchange_sibling(send_i, send_o, send_s)
        pending = (_chip_sum(place, send_i, ri, "chip_sum_w_in", 256),
                   _chip_sum(place, send_o, ro, "chip_sum_w_out", SHARD_ROWS // 2))
        parts[l] = (send_i, ri, send_o, ro)
    ps_ = _add2(send_s, rs, "chip_sum_small", SMALL_ROWS // 4)
    qi, qo, qs = _scatter_chips(*pending, ps_)
    parts[0] += (qi, qo)
    fi = lax.empty((DEPTH, D_MODEL, SHARD_COLS), F32)
    fo = lax.empty((DEPTH, SHARD_ROWS, D_MODEL), F32)
    for l in range(DEPTH):
        send_i, ri, send_o, ro, qi, qo = parts[l]
        fi = _mesh_sum(place, send_i, ri, qi, fi, l, "mesh_sum_w_in", 256)
        fo = _mesh_sum(place, send_o, ro, qo, fo, l, "mesh_sum_w_out", SHARD_ROWS // 2)
    hs = _mesh_sum_small(place, ps_, qs, SMALL_HALF // 4)
    fi, fo, fs = _join_sibling(fi, fo, hs)

    col_major = lambda a: jnp.transpose(a, (2, 0, 1))
    flat_o = lambda a: a.reshape(-1, D_MODEL)
    out_i = _adamw(col_major(fi), col_major(w_in), col_major(m_w_in), col_major(v_w_in), "adamw_w_in", 77)
    out_i = [jnp.transpose(t, (1, 2, 0)) for t in out_i]
    out_o = _adamw(flat_o(fo), flat_o(w_out), flat_o(m_w_out), flat_o(v_w_out), "adamw_w_out", 256)
    out_s = _adamw(fs, _pack_small(norm_g, final_g, pool_scale, forget_bias, pool_w),
                   _pack_small(m_norm_g, m_final_g, m_pool_scale, m_forget_bias, m_pool_w),
                   _pack_small(v_norm_g, v_final_g, v_pool_scale, v_forget_bias, v_pool_w),
                   "adamw_small", SMALL_ROWS // 4)

    loss = out_s[0].reshape(-1)[LOSS_SLOT]
    groups = []
    for oi, oo, os_ in zip(out_i, out_o, out_s):
        sn, sf, sp, sb, sw = _unpack_small(os_)
        groups.append((sn, oi.reshape(w_in.shape), sb, sw, sp, oo.reshape(w_out.shape), sf))
    return (loss, dx[None]) + tuple(t for grp in groups for t in grp)
```

```python
import jax
import jax.numpy as jnp
from jax import lax
from jax.experimental import pallas as pl
from jax.experimental.pallas import tpu as pltpu

F32 = jnp.float32
BF16 = jnp.bfloat16

D_MODEL = 1024
DEPTH = 4
POOL_WIDTH = 512
ATTN_WIDTH = 512
HEAD_DIM = 64
PAIR = 2 * HEAD_DIM
N_PAIRS = ATTN_WIDTH // PAIR
POOL_WINDOWS = (2, 4, 8, 16)
GROUP_DIM = 128
HALO = 16
IN_COLS = 3080
OFF_F = 3072
IN_COLS_PAD = 3200
SECTIONS = ((0, 512), (512, 512), (1024, 512), (1536, 512), (2048, 512), (2560, 512), (OFF_F, 128))
N_CHIPS = 4
SHARD_COLS = IN_COLS // N_CHIPS
SHARD_ROWS = D_MODEL // N_CHIPS
RMS_EPS = 1e-6
NEG = -1e30
Q_SCALE = 0.125

ADAM_LR = 0.001
ADAM_B1 = 0.9
ADAM_B2 = 0.999
ADAM_EPS = 1e-08
ADAM_WD = 0.01
ADAM_STEP = 10

SMALL_ROWS = 2112
SMALL_HALF = SMALL_ROWS // 2

NT = (((1,), (1,)), ((), ()))
TN = (((0,), (0,)), ((), ()))
MESH = pl.DeviceIdType.MESH
ANY = pl.BlockSpec(memory_space=pl.ANY)


def _params(semantics, vmem_mb=48):
    return pltpu.CompilerParams(dimension_semantics=semantics, vmem_limit_bytes=vmem_mb << 20)


def _row_spec(tm, cols):
    return pl.BlockSpec((tm, cols), lambda i: (i, 0))


def _full_spec(shape):
    return pl.BlockSpec(shape, lambda *_: (0,) * len(shape))


def _sigmoid(x):
    return jax.nn.sigmoid(x)


def _scan_rows(a, reverse=False):
    n = a.shape[0]
    row = lax.broadcasted_iota(jnp.int32, a.shape, 0)
    k = 1
    while k < n:
        if reverse:
            a = a + jnp.where(row < n - k, pltpu.roll(a, n - k, 0), 0.0)
        else:
            a = a + jnp.where(row >= k, pltpu.roll(a, k, 0), 0.0)
        k *= 2
    return a


def _inproj_fwd(place, x, g, own, got, fb, tm=512):
    s = x.shape[0]

    def body(place_ref, x_ref, g_ref, fb_ref, *refs):
        own_ref = refs[0] if own is not None else None
        (got_ref, pu_ref, pg_ref, q_ref, k_ref, v_ref, ag_ref,
         lf_ref, c_ref, h_ref, w_ref, carry_ref, piece_ref, sem) = refs[own is not None:]

        @pl.when(pl.program_id(0) == 0)
        def _():
            carry_ref[...] = jnp.zeros_like(carry_ref)
            for chip in range(N_CHIPS):
                rel = jnp.bitwise_xor(chip, place_ref[0])

                def fetch(src):
                    cp = pltpu.make_async_copy(src, piece_ref, sem)
                    cp.start()
                    cp.wait()

                if own_ref is None:
                    fetch(got_ref.at[0, rel])
                else:
                    pl.when(rel == 0)(lambda: fetch(own_ref))
                    pl.when(rel != 0)(lambda: fetch(got_ref.at[0, rel]))
                w_ref[:, chip * SHARD_COLS:(chip + 1) * SHARD_COLS] = piece_ref[...]
            w_ref[:, IN_COLS:] = jnp.zeros((D_MODEL, IN_COLS_PAD - IN_COLS), BF16)

        xf = x_ref[...]
        r = lax.rsqrt(jnp.mean(xf * xf, axis=-1, keepdims=True) + RMS_EPS)
        h = (xf * r * g_ref[...]).astype(BF16)
        h_ref[...] = h

        def proj(sec):
            off, n = SECTIONS[sec]
            return jnp.dot(h, w_ref[:, off:off + n], preferred_element_type=F32)

        pu_ref[...] = proj(0).astype(BF16)
        pg_ref[...] = proj(1).astype(BF16)
        q_ref[...] = (proj(2) * Q_SCALE).astype(BF16)
        k_ref[...] = proj(3).astype(BF16)
        v_ref[...] = proj(4).astype(BF16)
        ag_ref[...] = proj(5).astype(BF16)
        z = proj(6) + fb_ref[...]
        lf = jnp.minimum(z, 0.0) - jnp.log(1.0 + jnp.exp(-jnp.abs(z)))
        lf_ref[...] = lf
        c_ref[...] = _scan_rows(lf) + carry_ref[0:1, :]
        carry_ref[0:1, :] = c_ref[tm - 1:tm, :]

    act = jax.ShapeDtypeStruct((s, 512), BF16)
    rows = lambda cols: pl.BlockSpec((tm, cols), lambda i, p: (i, 0))
    whole = lambda shape: pl.BlockSpec(shape, lambda i, p: (0,) * len(shape))
    shards = ([] if own is None else [own]) + [got]
    return pl.pallas_call(
        body, name="inproj_fwd" if own is None else "inproj_fwd_first",
        grid_spec=pltpu.PrefetchScalarGridSpec(
            num_scalar_prefetch=1, grid=(s // tm,),
            in_specs=[rows(D_MODEL), whole((1, D_MODEL)), whole((1, 128))] + [ANY] * len(shards),
            out_specs=[rows(512)] * 6 + [rows(128), rows(128), rows(D_MODEL), whole((D_MODEL, IN_COLS_PAD))],
            scratch_shapes=[pltpu.VMEM((8, 128), F32), pltpu.VMEM((D_MODEL, SHARD_COLS), BF16),
                            pltpu.SemaphoreType.DMA]),
        out_shape=[act] * 6 + [jax.ShapeDtypeStruct((s, 128), F32), jax.ShapeDtypeStruct((s, 128), F32),
                               jax.ShapeDtypeStruct((s, D_MODEL), BF16),
                               jax.ShapeDtypeStruct((D_MODEL, IN_COLS_PAD), BF16)],
        compiler_params=_params(("arbitrary",)),
    )(place, x, g, fb, *shards)


def _window_sums(ext, forward):
    n = ext.shape[0]
    outs = []
    for gi, w in enumerate(POOL_WINDOWS):
        a = ext[:, gi * GROUP_DIM:(gi + 1) * GROUP_DIM]
        k = 1
        while k < w:
            a = a + pltpu.roll(a, (n - k) if forward else k, 0)
            k *= 2
        outs.append(a)
    return outs


def _window_counts(row0, tm):
    t = row0 + lax.broadcasted_iota(jnp.int32, (tm, GROUP_DIM), 0)
    return [jnp.minimum(t + 1, w).astype(F32) for w in POOL_WINDOWS]


def _pool_delta(u, halo, row0):
    tm = u.shape[0]
    sums = _window_sums(jnp.concatenate([halo, u], axis=0), forward=False)
    cnt = _window_counts(row0, tm)
    return [sums[gi][HALO:, :] / cnt[gi] - u[:, gi * GROUP_DIM:(gi + 1) * GROUP_DIM]
            for gi in range(len(POOL_WINDOWS))]


def _pool_fwd(pu, pg, pw, ps, tm=512):
    s = pu.shape[0]
    hb = tm // HALO

    def body(pu_ref, halo_ref, pg_ref, pw_ref, ps_ref, po_ref):
        i = pl.program_id(0)
        u = pu_ref[...].astype(F32)
        halo = jnp.where(i == 0, 0.0, halo_ref[...].astype(F32))
        d = _pool_delta(u, halo, i * tm)
        for gi in range(len(POOL_WINDOWS)):
            cols = slice(gi * GROUP_DIM, (gi + 1) * GROUP_DIM)
            z = jnp.dot(d[gi].astype(BF16), pw_ref[gi], preferred_element_type=F32)
            gate = pg_ref[:, cols].astype(F32)
            po_ref[:, cols] = (z * ps_ref[:, cols] * (gate * _sigmoid(gate))).astype(BF16)

    return pl.pallas_call(
        body, name="pool_fwd", grid=(s // tm,),
        in_specs=[_row_spec(tm, 512),
                  pl.BlockSpec((HALO, 512), lambda i: (jnp.maximum(i * hb - 1, 0), 0)),
                  _row_spec(tm, 512), _full_spec((4, GROUP_DIM, GROUP_DIM)), _full_spec((1, 512))],
        out_specs=_row_spec(tm, 512),
        out_shape=jax.ShapeDtypeStruct((s, 512), BF16),
        compiler_params=_params(("parallel",)),
    )(pu, pu, pg, pw, ps)


def _split3(x):
    hi = x.astype(BF16).astype(F32)
    r1 = x - hi
    lo = r1.astype(BF16).astype(F32)
    return hi, lo, (r1 - lo).astype(BF16).astype(F32)


def _own(idx, h):
    return (idx < HEAD_DIM) if h == 0 else (idx >= HEAD_DIM)


def _spare(h):
    return HEAD_DIM if h == 0 else 0


def _attn_fwd(q, k, v, c, ag, carry=None, tq=512, tk=512, chunk=512):
    s = q.shape[0]
    n_carry = 0 if carry is None else len(carry)

    def body(*refs):
        q_ref, k_ref, v_ref, c_ref, ag_ref = refs[:5]
        carry_src = refs[5:5 + n_carry]
        ma_ref, o_ref, lse_ref = refs[5 + n_carry:8 + n_carry]
        carry_dst = refs[8 + n_carry:8 + 2 * n_carry]
        kx_ref, vt_ref, acc0_ref, acc1_ref, sta_ref, stb_ref = refs[8 + 2 * n_carry:14 + 2 * n_carry]
        carry_sems = refs[14 + 2 * n_carry:]
        j = pl.program_id(0)
        i = pl.program_id(1)

        if n_carry:
            @pl.when((j == 0) & (i == 0))
            def _():
                _gather_start(carry_src, carry_dst, carry_sems[:2])
                for cp in _gather_own(carry_src, carry_dst, carry_sems[2]):
                    cp.start()

        @pl.when(i == 0)
        def _():
            def prep(ch, carry):
                rows = pl.ds(pl.multiple_of(ch * chunk, chunk), chunk)
                kk = k_ref[rows, :].astype(F32)
                cc = c_ref[rows, :]
                vt = jnp.transpose(v_ref[rows, :].astype(F32))
                lane = lax.broadcasted_iota(jnp.int32, (chunk, PAIR), 1)
                sub = lax.broadcasted_iota(jnp.int32, (PAIR, chunk), 0)
                for h in range(2):
                    ccol = jnp.sum(jnp.where(lane == 2 * j + h, cc, 0.0), axis=1, keepdims=True)
                    kx = jnp.where(_own(lane, h), kk, 0.0)
                    for t, part in enumerate(_split3(-ccol)):
                        kx = jnp.where(lane == _spare(h) + t, part, kx)
                    kx_ref[h, rows, :] = kx.astype(BF16)
                    vt_ref[h, :, rows] = jnp.where(_own(sub, h), vt,
                                                   jnp.where(sub == _spare(h), 1.0, 0.0)).astype(BF16)
                return carry

            lax.fori_loop(0, s // chunk, prep, 0)

        qq = q_ref[...].astype(F32)
        lane_q = lax.broadcasted_iota(jnp.int32, (tq, PAIR), 1)
        qx = []
        for h in range(2):
            ones = (lane_q >= _spare(h)) & (lane_q < _spare(h) + 3)
            qx.append(jnp.transpose(jnp.where(_own(lane_q, h), qq, jnp.where(ones, 1.0, 0.0))).astype(BF16))
        accs = (acc0_ref, acc1_ref)
        for acc in accs:
            acc[...] = jnp.zeros_like(acc)

        def rows_of(kb):
            return pl.ds(pl.multiple_of(kb * tk, tk), tk)

        def scores(kb, dst):
            for h in range(2):
                dst[h] = jnp.dot(kx_ref[h, rows_of(kb), :], qx[h], preferred_element_type=F32)

        def consume(kb, src, m, masked):
            m_out, alpha, pv = [], [], []
            for h in range(2):
                sh = src[h]
                if masked:
                    kpos = kb * tk + lax.broadcasted_iota(jnp.int32, (tk, tq), 0)
                    qpos = i * tq + lax.broadcasted_iota(jnp.int32, (tk, tq), 1)
                    sh = jnp.where(qpos >= kpos, sh, NEG)
                m_new = jnp.maximum(m[h], jnp.max(sh, axis=0, keepdims=True))
                pt = jnp.exp(sh - m_new).astype(BF16)
                alpha.append(jnp.exp(m[h] - m_new))
                pv.append(jnp.dot(vt_ref[h, :, rows_of(kb)], pt, preferred_element_type=F32))
                m_out.append(m_new)
            for h in range(2):
                accs[h][...] = accs[h][...] * alpha[h] + pv[h]
            return tuple(m_out)

        n_full = (i * tq) // tk

        def two_blocks(t, m):
            kb = 2 * t
            scores(kb + 1, stb_ref)
            m = consume(kb, sta_ref, m, False)
            scores(kb + 2, sta_ref)
            return consume(kb + 1, stb_ref, m, False)

        def odd_tail(m):
            scores(n_full, stb_ref)
            m = consume(n_full - 1, sta_ref, m, False)
            return consume(n_full, stb_ref, m, True)

        def even_tail(m):
            return consume(n_full, sta_ref, m, True)

        scores(0, sta_ref)
        m0 = jnp.full((1, tq), NEG, F32)
        m = lax.fori_loop(0, n_full // 2, two_blocks, (m0, m0))
        m = lax.cond(lax.rem(n_full, 2) == 1, odd_tail, even_tail, m)

        sub_o = lax.broadcasted_iota(jnp.int32, (PAIR, tq), 0)
        den = [accs[h][_spare(h):_spare(h) + 1, :] for h in range(2)]
        ot = jnp.where(sub_o < HEAD_DIM, acc0_ref[...] / den[0], acc1_ref[...] / den[1])
        o = jnp.transpose(ot)
        o_ref[...] = o.astype(BF16)
        gate = ag_ref[...].astype(F32)
        ma_ref[...] = (o * (gate * _sigmoid(gate))).astype(BF16)
        sub8 = lax.broadcasted_iota(jnp.int32, (8, tq), 0)
        lse_ref[...] = jnp.where(sub8 == 0, m[0] + jnp.log(den[0]), m[1] + jnp.log(den[1]))

        if n_carry:
            @pl.when((j == N_PAIRS - 1) & (i == s // tq - 1))
            def _():
                _gather_finish(carry_src, carry_dst, carry_sems[:2])
                for cp in _gather_own(carry_src, carry_dst, carry_sems[2]):
                    cp.wait()

    carried = [] if carry is None else list(carry)
    carry_sem = ([pltpu.SemaphoreType.DMA((GATHER_SEMS,))] * 2 + [pltpu.SemaphoreType.DMA((2,))]) if n_carry else []
    return pl.pallas_call(
        body, name="attn_fwd_gather" if n_carry else "attn_fwd", grid=(N_PAIRS, s // tq),
        in_specs=[pl.BlockSpec((tq, PAIR), lambda j, i: (i, j)),
                  pl.BlockSpec((s, PAIR), lambda j, i: (0, j)),
                  pl.BlockSpec((s, PAIR), lambda j, i: (0, j)),
                  pl.BlockSpec((s, 128), lambda j, i: (0, 0)),
                  pl.BlockSpec((tq, PAIR), lambda j, i: (i, j))] + [ANY] * n_carry,
        out_specs=[pl.BlockSpec((tq, PAIR), lambda j, i: (i, j)),
                   pl.BlockSpec((tq, PAIR), lambda j, i: (i, j)),
                   pl.BlockSpec((None, 8, tq), lambda j, i: (j, 0, i))] + [ANY] * n_carry,
        out_shape=[jax.ShapeDtypeStruct((s, ATTN_WIDTH), BF16), jax.ShapeDtypeStruct((s, ATTN_WIDTH), BF16),
                   jax.ShapeDtypeStruct((N_PAIRS, 8, s), F32)] + (_gather_shapes(*carried) if n_carry else []),
        scratch_shapes=[pltpu.VMEM((2, s, PAIR), BF16), pltpu.VMEM((2, PAIR, s), BF16),
                        pltpu.VMEM((PAIR, tq), F32), pltpu.VMEM((PAIR, tq), F32),
                        pltpu.VMEM((2, tk, tq), F32), pltpu.VMEM((2, tk, tq), F32)] + carry_sem,
        compiler_params=_params(("arbitrary", "arbitrary")),
    )(q, k, v, c, ag, *carried)


def _outproj_fwd(x, po, ma, wo, tm=512):
    s = x.shape[0]

    def body(x_ref, po_ref, ma_ref, wo_ref, xn_ref):
        xn_ref[...] = (x_ref[...]
                       + jnp.dot(po_ref[...], wo_ref[0:POOL_WIDTH, :], preferred_element_type=F32)
                       + jnp.dot(ma_ref[...], wo_ref[POOL_WIDTH:, :], preferred_element_type=F32))

    return pl.pallas_call(
        body, name="outproj_fwd", grid=(s // tm,),
        in_specs=[_row_spec(tm, D_MODEL), _row_spec(tm, 512), _row_spec(tm, 512),
                  _full_spec((D_MODEL, D_MODEL))],
        out_specs=_row_spec(tm, D_MODEL),
        out_shape=jax.ShapeDtypeStruct((s, D_MODEL), F32),
        compiler_params=_params(("parallel",)),
    )(x, po, ma, wo)


def _loss_head(x, tgt, g, tm=512):
    s = x.shape[0]

    def body(x_ref, t_ref, g_ref, dx_ref, dg_ref, loss_ref):
        @pl.when(pl.program_id(0) == 0)
        def _():
            dg_ref[...] = jnp.zeros_like(dg_ref)
            loss_ref[...] = jnp.zeros_like(loss_ref)

        xf = x_ref[...]
        r = lax.rsqrt(jnp.mean(xf * xf, axis=-1, keepdims=True) + RMS_EPS)
        xh = xf * r
        gg = g_ref[...]
        e = xh * gg - t_ref[...]
        loss_ref[...] += 0.5 * jnp.sum(jnp.mean(e * e, axis=-1, keepdims=True))
        dy = e * (1.0 / D_MODEL)
        u = dy * gg
        dx_ref[...] = r * (u - xh * jnp.mean(xh * u, axis=-1, keepdims=True))
        dg_ref[...] += jnp.sum(dy * xh, axis=0, keepdims=True)

    return pl.pallas_call(
        body, name="loss_head", grid=(s // tm,),
        in_specs=[_row_spec(tm, D_MODEL), _row_spec(tm, D_MODEL), _full_spec((1, D_MODEL))],
        out_specs=[_row_spec(tm, D_MODEL), _full_spec((1, D_MODEL)), _full_spec((8, 128))],
        out_shape=[jax.ShapeDtypeStruct((s, D_MODEL), F32), jax.ShapeDtypeStruct((1, D_MODEL), F32),
                   jax.ShapeDtypeStruct((8, 128), F32)],
        compiler_params=_params(("arbitrary",)),
    )(x, tgt, g)


def _outproj_bwd(dx, po, ma, ag, o, wo, head_sel, tm=512):
    s = dx.shape[0]

    def body(dx_ref, po_ref, ma_ref, ag_ref, o_ref, wo_ref, sel_ref, dmp_ref, do_ref, dag_ref, dt_ref, gwo_ref):
        @pl.when(pl.program_id(0) == 0)
        def _():
            gwo_ref[...] = jnp.zeros_like(gwo_ref)

        dxb = dx_ref[...].astype(BF16)
        dm = lax.dot_general(dxb, wo_ref[...], NT, preferred_element_type=F32)
        dmp_ref[...] = dm[:, 0:POOL_WIDTH].astype(BF16)
        dma = dm[:, POOL_WIDTH:]
        gate = ag_ref[...].astype(F32)
        of = o_ref[...].astype(F32)
        sg = _sigmoid(gate)
        do = dma * (gate * sg)
        do_ref[...] = do.astype(BF16)
        dag_ref[...] = (dma * of * (sg * (1.0 + gate * (1.0 - sg)))).astype(BF16)
        prod = do * of
        hi = prod.astype(BF16)
        lo = (prod - hi.astype(F32)).astype(BF16)
        dt_ref[...] = (lax.dot_general(sel_ref[...], hi, NT, preferred_element_type=F32)
                       + lax.dot_general(sel_ref[...], lo, NT, preferred_element_type=F32))
        gwo_ref[0:POOL_WIDTH, :] += lax.dot_general(po_ref[...], dxb, TN, preferred_element_type=F32)
        gwo_ref[POOL_WIDTH:, :] += lax.dot_general(ma_ref[...], dxb, TN, preferred_element_type=F32)

    act = jax.ShapeDtypeStruct((s, 512), BF16)
    return pl.pallas_call(
        body, name="outproj_bwd", grid=(s // tm,),
        in_specs=[_row_spec(tm, D_MODEL)] + [_row_spec(tm, 512)] * 4
                 + [_full_spec((D_MODEL, D_MODEL)), _full_spec((8, ATTN_WIDTH))],
        out_specs=[_row_spec(tm, 512)] * 3 + [pl.BlockSpec((8, tm), lambda i: (0, i)),
                                              _full_spec((D_MODEL, D_MODEL))],
        out_shape=[act, act, act, jax.ShapeDtypeStruct((8, s), F32),
                   jax.ShapeDtypeStruct((D_MODEL, D_MODEL), F32)],
        compiler_params=_params(("arbitrary",)),
    )(dx, po, ma, ag, o, wo, head_sel)


def _attn_bwd(q, k, v, do, c, lse, dt, carry=None, tq=256, chunk=512):
    s = q.shape[0]
    tk = 2 * tq
    nq = s // tq
    n_carry = 0 if carry is None else len(carry)

    def body(*refs):
        q_ref, k_ref, v_ref, do_ref, c_ref, lse_ref, dt_ref = refs[:7]
        carry_src = refs[7:7 + n_carry]
        dq_ref, dk_ref, dv_ref, dcs_ref, drs_ref = refs[7 + n_carry:12 + n_carry]
        carry_dst = refs[12 + n_carry:12 + 2 * n_carry]
        (qxt_ref, doxt_ref, qm_ref, dox_ref, dqt0_ref, dqt1_ref, dk0_ref, dk1_ref, dvx_ref,
         kx_ref, vx_ref, ktm_ref, sa_ref, sb_ref) = refs[12 + 2 * n_carry:26 + 2 * n_carry]
        carry_sems = refs[26 + 2 * n_carry:]
        carry_big = tuple(zip(carry_src, carry_dst))
        j = pl.program_id(0)
        b = pl.program_id(1)
        dqts = (dqt0_ref, dqt1_ref)
        dks = (dk0_ref, dk1_ref)

        if n_carry:
            @pl.when((j == 0) & (b == 0))
            def _():
                _scatter_start(carry_big, None, carry_sems)

        @pl.when(b == 0)
        def _():
            def prep(ch, carry):
                rows = pl.ds(pl.multiple_of(ch * chunk, chunk), chunk)
                qq = q_ref[rows, :].astype(F32)
                dd = do_ref[rows, :].astype(F32)
                qt = jnp.transpose(qq)
                ddt = jnp.transpose(dd)
                lane = lax.broadcasted_iota(jnp.int32, (chunk, PAIR), 1)
                sub = lax.broadcasted_iota(jnp.int32, (PAIR, chunk), 0)
                for h in range(2):
                    sp = _spare(h)
                    qm_ref[h, rows, :] = jnp.where(_own(lane, h), qq, jnp.where(lane == sp, 1.0, 0.0)).astype(BF16)
                    dox_ref[h, rows, :] = jnp.where(_own(lane, h), dd, 0.0).astype(BF16)
                    qx = jnp.where(_own(sub, h), qt, jnp.where((sub >= sp) & (sub < sp + 3), 1.0, 0.0))
                    for t, part in enumerate(_split3(-lse_ref[pl.ds(h, 1), rows])):
                        qx = jnp.where(sub == sp + 3 + t, part, qx)
                    qxt_ref[h, :, rows] = qx.astype(BF16)
                    dx = jnp.where(_own(sub, h), ddt, 0.0)
                    for t, part in enumerate(_split3(-dt_ref[pl.ds(h, 1), rows])):
                        dx = jnp.where(sub == sp + t, part, dx)
                    doxt_ref[h, :, rows] = dx.astype(BF16)
                return carry

            lax.fori_loop(0, s // chunk, prep, 0)
            for ref in dqts:
                ref[...] = jnp.zeros_like(ref)

        kk = k_ref[...].astype(F32)
        vv = v_ref[...].astype(F32)
        cc = c_ref[...]
        kt = jnp.transpose(kk)
        lane = lax.broadcasted_iota(jnp.int32, (tk, PAIR), 1)
        sub = lax.broadcasted_iota(jnp.int32, (PAIR, tk), 0)
        for h in range(2):
            sp = _spare(h)
            ccol = jnp.sum(jnp.where(lane == 2 * j + h, cc, 0.0), axis=1, keepdims=True)
            kx = jnp.where(_own(lane, h), kk, jnp.where((lane >= sp + 3) & (lane < sp + 6), 1.0, 0.0))
            for t, part in enumerate(_split3(-ccol)):
                kx = jnp.where(lane == sp + t, part, kx)
            kx_ref[h] = kx.astype(BF16)
            vx_ref[h] = jnp.where(_own(lane, h), vv, jnp.where((lane >= sp) & (lane < sp + 3), 1.0, 0.0)).astype(BF16)
            ktm_ref[h] = jnp.where(_own(sub, h), kt, jnp.where(sub == sp, 1.0, 0.0)).astype(BF16)
        for ref in dks:
            ref[...] = jnp.zeros_like(ref)
        dvx_ref[...] = jnp.zeros_like(dvx_ref)

        def cols_of(i):
            return pl.ds(pl.multiple_of(i * tq, tq), tq)

        def scores(i, dst):
            cols = cols_of(jnp.minimum(i, nq - 1))
            for h in range(2):
                dst[h] = jnp.dot(kx_ref[h], qxt_ref[h, :, cols], preferred_element_type=F32)
                dst[2 + h] = jnp.dot(vx_ref[h], doxt_ref[h, :, cols], preferred_element_type=F32)

        def consume(i, src, masked):
            cols = cols_of(i)
            for h in range(2):
                arg = src[h]
                if masked:
                    kpos = b * tk + lax.broadcasted_iota(jnp.int32, (tk, tq), 0)
                    qpos = i * tq + lax.broadcasted_iota(jnp.int32, (tk, tq), 1)
                    arg = jnp.where(qpos >= kpos, arg, NEG)
                pt = jnp.exp(arg)
                dst = (pt * src[2 + h]).astype(BF16)
                dvx_ref[...] += jnp.dot(pt.astype(BF16), dox_ref[h, cols, :], preferred_element_type=F32)
                dks[h][...] += jnp.dot(dst, qm_ref[h, cols, :], preferred_element_type=F32)
                dqts[h][:, cols] += jnp.dot(ktm_ref[h], dst, preferred_element_type=F32)

        i0 = 2 * b

        def two_blocks(t, masked):
            i = i0 + 2 * t
            scores(i + 1, sb_ref)
            consume(i, sa_ref, masked)
            scores(i + 2, sa_ref)
            consume(i + 1, sb_ref, masked)

        def loop_body(t, carry):
            two_blocks(t, False)
            return carry

        scores(i0, sa_ref)
        two_blocks(0, True)
        lax.fori_loop(1, (nq - i0) // 2, loop_body, 0)

        dv_ref[...] = dvx_ref[...].astype(BF16)
        dk_ref[...] = jnp.where(lane < HEAD_DIM, dk0_ref[...], dk1_ref[...]).astype(BF16)
        dcs_ref[...] = jnp.where(lane == _spare(0), dk0_ref[...], jnp.where(lane == _spare(1), dk1_ref[...], 0.0))

        @pl.when(b == s // tk - 1)
        def _():
            sub8 = lax.broadcasted_iota(jnp.int32, (8, s), 0)
            drs_ref[...] = jnp.where(sub8 == 0, dqt0_ref[_spare(0):_spare(0) + 1, :],
                                     jnp.where(sub8 == 1, dqt1_ref[_spare(1):_spare(1) + 1, :], 0.0))
            sub_c = lax.broadcasted_iota(jnp.int32, (PAIR, chunk), 0)
            for ch in range(s // chunk):
                rows = pl.ds(ch * chunk, chunk)
                both = jnp.where(sub_c < HEAD_DIM, dqt0_ref[:, rows], dqt1_ref[:, rows])
                dq_ref[rows, :] = (jnp.transpose(both) * Q_SCALE).astype(BF16)

        if n_carry:
            @pl.when((j == N_PAIRS - 1) & (b == s // tk - 1))
            def _():
                _scatter_finish(carry_big, None, carry_sems)

    act = jax.ShapeDtypeStruct((s, ATTN_WIDTH), BF16)
    pair_rows = pl.BlockSpec((s, PAIR), lambda j, b: (0, j))
    pair_blk = pl.BlockSpec((tk, PAIR), lambda j, b: (b, j))
    stat = pl.BlockSpec((None, 2, s), lambda j, b: (j, 0, 0))
    carried = [] if carry is None else list(carry)
    carry_sem = [pltpu.SemaphoreType.DMA((SCATTER_SEMS,))] * 2 if n_carry else []
    return pl.pallas_call(
        body, name="attn_bwd_scatter" if n_carry else "attn_bwd", grid=(N_PAIRS, s // tk),
        in_specs=[pair_rows, pair_blk, pair_blk, pair_rows,
                  pl.BlockSpec((tk, 128), lambda j, b: (b, 0)),
                  pl.BlockSpec((None, 8, s), lambda j, b: (j, 0, 0)), stat] + [ANY] * n_carry,
        out_specs=[pair_rows, pair_blk, pair_blk, pair_blk,
                   pl.BlockSpec((None, 8, s), lambda j, b: (j, 0, 0))] + [ANY] * n_carry,
        out_shape=[act, act, act, jax.ShapeDtypeStruct((s, N_PAIRS * 128), F32),
                   jax.ShapeDtypeStruct((N_PAIRS, 8, s), F32)] + (_scatter_shapes(*carried) if n_carry else []),
        scratch_shapes=[pltpu.VMEM((2, PAIR, s), BF16), pltpu.VMEM((2, PAIR, s), BF16),
                        pltpu.VMEM((2, s, PAIR), BF16), pltpu.VMEM((2, s, PAIR), BF16),
                        pltpu.VMEM((PAIR, s), F32), pltpu.VMEM((PAIR, s), F32),
                        pltpu.VMEM((tk, PAIR), F32), pltpu.VMEM((tk, PAIR), F32), pltpu.VMEM((tk, PAIR), F32),
                        pltpu.VMEM((2, tk, PAIR), BF16), pltpu.VMEM((2, tk, PAIR), BF16),
                        pltpu.VMEM((2, PAIR, tk), BF16),
                        pltpu.VMEM((4, tk, tq), F32), pltpu.VMEM((4, tk, tq), F32)] + carry_sem,
        compiler_params=_params(("arbitrary", "arbitrary")),
    )(q, k, v, do, c, lse, dt, *carried)


def _forget_bwd(dcs, drs, lf, tm=512):
    s = lf.shape[0]
    n = s // tm

    def body(dcs_ref, drs_ref, lf_ref, dpf_ref, dfb_ref, carry_ref):
        @pl.when(pl.program_id(0) == 0)
        def _():
            carry_ref[...] = jnp.zeros_like(carry_ref)
            dfb_ref[...] = jnp.zeros_like(dfb_ref)

        lane = lax.broadcasted_iota(jnp.int32, (tm, 128), 1)
        dc = drs_ref[...]
        for pj in range(N_PAIRS):
            blk = dcs_ref[:, pj * 128:(pj + 1) * 128]
            for h in range(2):
                head = 2 * pj + h
                moved = pltpu.roll(blk, (head - _spare(h)) % 128, 1) if head != _spare(h) else blk
                dc = dc - jnp.where(lane == head, moved, 0.0)
        dlf = _scan_rows(dc, reverse=True) + carry_ref[0:1, :]
        carry_ref[...] = dlf[0:8, :]
        dz = jnp.where(lane < 8, dlf * (1.0 - jnp.exp(lf_ref[...])), 0.0)
        dpf_ref[...] = dz.astype(BF16)
        dfb_ref[...] += jnp.sum(dz, axis=0, keepdims=True)

    return pl.pallas_call(
        body, name="forget_bwd", grid=(n,),
        in_specs=[pl.BlockSpec((tm, N_PAIRS * 128), lambda i: (n - 1 - i, 0)),
                  pl.BlockSpec((tm, 128), lambda i: (n - 1 - i, 0)),
                  pl.BlockSpec((tm, 128), lambda i: (n - 1 - i, 0))],
        out_specs=[pl.BlockSpec((tm, 128), lambda i: (n - 1 - i, 0)), _full_spec((1, 128))],
        out_shape=[jax.ShapeDtypeStruct((s, 128), BF16), jax.ShapeDtypeStruct((1, 128), F32)],
        scratch_shapes=[pltpu.VMEM((8, 128), F32)],
        compiler_params=_params(("arbitrary",)),
    )(dcs, drs, lf)


def _pool_bwd(pu, pg, dmp, pw, ps, tm=512):
    s = pu.shape[0]
    hb = tm // HALO
    n = s // tm
    last_halo = s // HALO - 1

    def body(pu_ref, halo_ref, pg_ref, dmp_ref, pgn_ref, dmpn_ref, pw_ref, ps_ref,
             dpu_ref, dpg_ref, gpw_ref, gps_ref):
        i = pl.program_id(0)

        @pl.when(i == 0)
        def _():
            gpw_ref[...] = jnp.zeros_like(gpw_ref)
            gps_ref[...] = jnp.zeros_like(gps_ref)

        u = pu_ref[...].astype(F32)
        halo = jnp.where(i == 0, 0.0, halo_ref[...].astype(F32))
        d = _pool_delta(u, halo, i * tm)
        cnt = _window_counts(i * tm, tm)
        cnt_next = _window_counts((i + 1) * tm, HALO)
        e_parts, dd_parts = [], []
        for gi in range(len(POOL_WINDOWS)):
            cols = slice(gi * GROUP_DIM, (gi + 1) * GROUP_DIM)
            wg = pw_ref[gi]
            scale = ps_ref[:, cols]
            db = d[gi].astype(BF16)
            z = jnp.dot(db, wg, preferred_element_type=F32)
            gate = pg_ref[:, cols].astype(F32)
            sg = _sigmoid(gate)
            dm = dmp_ref[:, cols].astype(F32)
            dy = dm * (gate * sg)
            dpg_ref[:, cols] = (dm * (z * scale) * (sg * (1.0 + gate * (1.0 - sg)))).astype(BF16)
            gps_ref[:, cols] += jnp.sum(dy * z, axis=0, keepdims=True)
            dz = (dy * scale).astype(BF16)
            gpw_ref[gi] += lax.dot_general(db, dz, TN, preferred_element_type=F32)
            dd = lax.dot_general(dz, wg, NT, preferred_element_type=F32)
            gate_n = pgn_ref[:, cols].astype(F32)
            dz_n = (dmpn_ref[:, cols].astype(F32) * (gate_n * _sigmoid(gate_n)) * scale).astype(BF16)
            dd_n = lax.dot_general(dz_n, wg, NT, preferred_element_type=F32)
            dd_n = jnp.where(i == n - 1, 0.0, dd_n)
            dd_parts.append(dd)
            e_parts.append(jnp.concatenate([dd / cnt[gi], dd_n / cnt_next[gi]], axis=0))
        lead = _window_sums(jnp.concatenate(e_parts, axis=1), forward=True)
        for gi in range(len(POOL_WINDOWS)):
            cols = slice(gi * GROUP_DIM, (gi + 1) * GROUP_DIM)
            dpu_ref[:, cols] = (lead[gi][0:tm, :] - dd_parts[gi]).astype(BF16)

    act = jax.ShapeDtypeStruct((s, 512), BF16)
    prev_halo = pl.BlockSpec((HALO, 512), lambda i: (jnp.maximum(i * hb - 1, 0), 0))
    next_halo = pl.BlockSpec((HALO, 512), lambda i: (jnp.minimum((i + 1) * hb, last_halo), 0))
    return pl.pallas_call(
        body, name="pool_bwd", grid=(n,),
        in_specs=[_row_spec(tm, 512), prev_halo, _row_spec(tm, 512), _row_spec(tm, 512), next_halo, next_halo,
                  _full_spec((4, GROUP_DIM, GROUP_DIM)), _full_spec((1, 512))],
        out_specs=[_row_spec(tm, 512), _row_spec(tm, 512), _full_spec((4, GROUP_DIM, GROUP_DIM)),
                   _full_spec((1, 512))],
        out_shape=[act, act, jax.ShapeDtypeStruct((4, GROUP_DIM, GROUP_DIM), F32),
                   jax.ShapeDtypeStruct((1, 512), F32)],
        compiler_params=_params(("arbitrary",)),
    )(pu, pu, pg, dmp, pg, dmp, pw, ps)


def _inproj_bwd_x(dsec, w, x, g, dxo, exchange=None, scatter=None, tm=512):
    s = x.shape[0]
    n_steps = s // tm
    carried = list(exchange or scatter or ())
    n_carry = len(carried)

    def body(*refs):
        d_refs = refs[0:7]
        w_ref, x_ref, g_ref, dxo_ref = refs[7:11]
        carry_src = refs[11:11 + n_carry]
        dxi_ref, dg_ref = refs[11 + n_carry:13 + n_carry]
        carry_dst = refs[13 + n_carry:13 + 2 * n_carry]
        sems = refs[13 + 2 * n_carry:]
        step = pl.program_id(0)

        def comm(phase):
            if exchange is not None:
                (_exchange_start if phase == 0 else _exchange_finish)(carry_src, carry_dst, None, sems)
            else:
                big = tuple(zip(carry_src[:2], carry_dst[:2]))
                (_scatter_start if phase == 0 else _scatter_finish)(big, (carry_src[2], carry_dst[2]), sems)

        @pl.when(step == 0)
        def _():
            dg_ref[...] = jnp.zeros_like(dg_ref)
            if n_carry:
                comm(0)

        dh = None
        for d_ref, (off, ncol) in zip(d_refs, SECTIONS):
            t = lax.dot_general(d_ref[...], w_ref[:, off:off + ncol], NT, preferred_element_type=F32)
            dh = t if dh is None else dh + t
        xf = x_ref[...]
        r = lax.rsqrt(jnp.mean(xf * xf, axis=-1, keepdims=True) + RMS_EPS)
        xh = xf * r
        u = dh * g_ref[...]
        dxi_ref[...] = dxo_ref[...] + r * (u - xh * jnp.mean(xh * u, axis=-1, keepdims=True))
        dg_ref[...] += jnp.sum(dh * xh, axis=0, keepdims=True)

        if n_carry:
            @pl.when(step == n_steps - 1)
            def _():
                comm(1)

    if exchange is not None:
        name, carry_shapes, n_sems = "inproj_bwd_x_exchange", _exchange_shapes(*exchange), 3
    elif scatter is not None:
        name, n_sems = "inproj_bwd_x_scatter", SCATTER_SEMS
        carry_shapes = _scatter_shapes(*scatter[:2]) + [jax.ShapeDtypeStruct((3, SMALL_HALF, 128), F32)]
    else:
        name, carry_shapes, n_sems = "inproj_bwd_x", [], 0
    return pl.pallas_call(
        body, name=name, grid=(n_steps,),
        in_specs=[_row_spec(tm, ncol) for _, ncol in SECTIONS]
                 + [_full_spec((D_MODEL, IN_COLS_PAD)), _row_spec(tm, D_MODEL), _full_spec((1, D_MODEL)),
                    _row_spec(tm, D_MODEL)] + [ANY] * n_carry,
        out_specs=[_row_spec(tm, D_MODEL), _full_spec((1, D_MODEL))] + [ANY] * n_carry,
        out_shape=[jax.ShapeDtypeStruct((s, D_MODEL), F32), jax.ShapeDtypeStruct((1, D_MODEL), F32)] + carry_shapes,
        scratch_shapes=[pltpu.SemaphoreType.DMA((n_sems,))] * 2 if n_carry else [],
        compiler_params=_params(("arbitrary",)),
    )(*dsec, w, x, g, dxo, *carried)


def _inproj_bwd_w(h, dsec, tm=512):
    s = h.shape[0]
    n_steps = s // tm

    def body(*refs):
        h_ref = refs[0]
        d_refs = refs[1:8]
        out_ref, gw_ref = refs[8:]
        step = pl.program_id(0)

        @pl.when(step == 0)
        def _():
            gw_ref[...] = jnp.zeros_like(gw_ref)

        hh = h_ref[...]
        for d_ref, (off, ncol) in zip(d_refs, SECTIONS):
            gw_ref[:, off:off + ncol] += lax.dot_general(hh, d_ref[...], TN, preferred_element_type=F32)

        @pl.when(step == n_steps - 1)
        def _():
            for k in range(N_CHIPS):
                out_ref[k] = gw_ref[:, k * SHARD_COLS:(k + 1) * SHARD_COLS]

    return pl.pallas_call(
        body, name="inproj_bwd_w", grid=(n_steps,),
        in_specs=[_row_spec(tm, D_MODEL)] + [_row_spec(tm, ncol) for _, ncol in SECTIONS],
        out_specs=_full_spec((N_CHIPS, D_MODEL, SHARD_COLS)),
        out_shape=jax.ShapeDtypeStruct((N_CHIPS, D_MODEL, SHARD_COLS), F32),
        scratch_shapes=[pltpu.VMEM((D_MODEL, IN_COLS_PAD), F32)],
        compiler_params=_params(("arbitrary",), vmem_mb=60),
    )(h, *dsec)


def _elementwise(fn, name, n_out, arrays, tm):
    shape = arrays[0].shape
    rest = shape[1:]
    n_in = len(arrays)

    def body(*refs):
        outs = fn(*[r[...] for r in refs[:n_in]])
        for r, val in zip(refs[n_in:], outs):
            r[...] = val

    spec = pl.BlockSpec((tm,) + rest, lambda i: (i,) + (0,) * len(rest))
    return pl.pallas_call(
        body, name=name, grid=(shape[0] // tm,),
        in_specs=[spec] * n_in, out_specs=[spec] * n_out, out_shape=[jax.ShapeDtypeStruct(shape, F32)] * n_out,
        compiler_params=_params(("parallel",)),
    )(*arrays)


def _add2(a, b, name, tm):
    return _elementwise(lambda p, q: (p + q,), name, 1, [a, b], tm)[0]


def _chip_sum(place, g, r, name, tm):
    n_layers, _, rows, cols = g.shape
    half = rows // 2
    nb = half // tm

    def body(place_ref, g_ref, r_ref, o_ref):
        o_ref[...] = (g_ref[...] + r_ref[...]).astype(BF16)

    blk = (None, None, tm, cols)
    return pl.pallas_call(
        body, name=name,
        grid_spec=pltpu.PrefetchScalarGridSpec(
            num_scalar_prefetch=1, grid=(n_layers, N_CHIPS, nb),
            in_specs=[pl.BlockSpec(blk, lambda l, k, i, p: (l, k, p[1] * nb + i, 0)),
                      pl.BlockSpec(blk, lambda l, k, i, p: (l, k, i, 0))],
            out_specs=pl.BlockSpec(blk, lambda l, k, i, p: (l, k, i, 0))),
        out_shape=jax.ShapeDtypeStruct((n_layers, N_CHIPS, half, cols), BF16),
        compiler_params=_params(("parallel", "parallel", "parallel")),
    )(place, g, r)


def _mesh_sum(place, g, r, got, into, layer, name, tm):
    _, _, rows, cols = g.shape
    half = rows // 2
    nb = half // tm

    def body(place_ref, g_ref, r_ref, got_ref, into_ref, o_ref):
        own = g_ref[...] + r_ref[...]
        o_ref[...] = (own + got_ref[0].astype(F32)) + (got_ref[1].astype(F32) + got_ref[2].astype(F32))

    blk = (None, None, tm, cols)
    return pl.pallas_call(
        body, name=name,
        grid_spec=pltpu.PrefetchScalarGridSpec(
            num_scalar_prefetch=1, grid=(nb,),
            in_specs=[pl.BlockSpec(blk, lambda i, p: (0, p[0], p[1] * nb + i, 0)),
                      pl.BlockSpec(blk, lambda i, p: (0, p[0], i, 0)),
                      pl.BlockSpec((None, 3, tm, cols), lambda i, p: (0, 0, i, 0)),
                      ANY],
            out_specs=pl.BlockSpec((None, tm, cols), lambda i, p: (layer, p[1] * nb + i, 0))),
        out_shape=jax.ShapeDtypeStruct(into.shape, F32),
        input_output_aliases={4: 0},
        compiler_params=_params(("parallel",)),
    )(place, g, r, got, into)


def _mesh_sum_small(place, ps, got, tm):
    nb = SMALL_HALF // tm

    def body(place_ref, ps_ref, got_ref, o_ref):
        o_ref[...] = (ps_ref[...] + got_ref[0]) + (got_ref[1] + got_ref[2])

    return pl.pallas_call(
        body, name="mesh_sum_small",
        grid_spec=pltpu.PrefetchScalarGridSpec(
            num_scalar_prefetch=1, grid=(nb,),
            in_specs=[pl.BlockSpec((tm, 128), lambda i, p: (p[1] * nb + i, 0)),
                      pl.BlockSpec((3, tm, 128), lambda i, p: (0, i, 0))],
            out_specs=pl.BlockSpec((tm, 128), lambda i, p: (p[1] * nb + i, 0))),
        out_shape=jax.ShapeDtypeStruct((SMALL_ROWS, 128), F32),
        compiler_params=_params(("parallel",)),
    )(place, ps, got)


def _adamw(g, w, m, v, name, tm):
    def fn(g, w, m, v):
        m = ADAM_B1 * m + (1.0 - ADAM_B1) * g
        v = ADAM_B2 * v + (1.0 - ADAM_B2) * (g * g)
        m_hat = m / (1.0 - ADAM_B1 ** ADAM_STEP)
        v_hat = v / (1.0 - ADAM_B2 ** ADAM_STEP)
        delta = -ADAM_LR * (m_hat / (jnp.sqrt(v_hat) + ADAM_EPS) + ADAM_WD * w)
        return g, delta, m, v

    return _elementwise(fn, name, 4, [g, w, m, v], tm)


def _position():
    x, y, c = lax.axis_index("x"), lax.axis_index("y"), lax.axis_index("c")
    other_chips = [(x, 1 - y), (1 - x, y), (1 - x, 1 - y)]
    return x, y, c, other_chips


def _remote(src, dst, sems, k, to):
    send_sems, recv_sems = sems
    return pltpu.make_async_remote_copy(src_ref=src, dst_ref=dst, send_sem=send_sems.at[k],
                                        recv_sem=recv_sems.at[k], device_id=to, device_id_type=MESH)


def _comm_call(body, name, n_in, out_shape, n_remote, aliases=None):
    return pl.pallas_call(
        body, name=name, in_specs=[ANY] * n_in, out_specs=[ANY] * len(out_shape), out_shape=out_shape,
        scratch_shapes=[pltpu.SemaphoreType.DMA((n_remote,)), pltpu.SemaphoreType.DMA((n_remote,))],
        input_output_aliases=aliases or {},
    )


def _halves(ref_rows, c):
    half = ref_rows // 2
    return pl.ds(half * c, half), pl.ds(half * (1 - c), half)


def _gather_weights(wi, wo):
    def body(wi_ref, wo_ref, gi_ref, go_ref, send_sems, recv_sems):
        _gather_start((wi_ref, wo_ref), (gi_ref, go_ref), (send_sems, recv_sems))
        _gather_finish((wi_ref, wo_ref), (gi_ref, go_ref), (send_sems, recv_sems))

    return _comm_call(body, "gather_weights", 2, _gather_shapes(wi, wo), GATHER_SEMS)(wi, wo)


GATHER_SEMS = 12
SCATTER_SEMS = 9


def _gather_shapes(wi, wo):
    return [jax.ShapeDtypeStruct((a.shape[0], N_CHIPS) + a.shape[1:], a.dtype) for a in (wi, wo)]


def _gather_sends(srcs, dsts, sems):
    x, y, c, chips = _position()
    out = []
    for k, (cx, cy) in enumerate(chips):
        for a, (src, dst) in enumerate(zip(srcs, dsts)):
            mine, _ = _halves(src.shape[1], c)
            out.append(_remote(src.at[:, mine], dst.at[:, k + 1, mine], sems, len(srcs) * k + a, (cx, cy, c)))
    return out


def _gather_own(srcs, dsts, local_sems):
    return [pltpu.make_async_copy(src, dst.at[:, 0], local_sems.at[a]) for a, (src, dst) in enumerate(zip(srcs, dsts))]


def _gather_start(srcs, dsts, sems):
    for cp in _gather_sends(srcs, dsts, sems):
        cp.start()


def _gather_finish(srcs, dsts, sems):
    x, y, c, _ = _position()
    n = len(srcs)
    forwards = []
    for k in range(3):
        for a, dst in enumerate(dsts):
            mine, _ = _halves(dst.shape[2], c)
            blk = dst.at[:, k + 1, mine]
            _remote(blk, blk, sems, n * k + a, (x, y, c)).wait_recv()
            forwards.append(_remote(blk, blk, sems, 3 * n + n * k + a, (x, y, 1 - c)))
            forwards[-1].start()
    for k in range(3):
        for a, dst in enumerate(dsts):
            _, other = _halves(dst.shape[2], c)
            blk = dst.at[:, k + 1, other]
            _remote(blk, blk, sems, 3 * n + n * k + a, (x, y, c)).wait_recv()
    for cp in _gather_sends(srcs, dsts, sems) + forwards:
        cp.wait_send()


def _scatter_sends(big, small, sems):
    x, y, c, chips = _position()
    n = len(big) + (small is not None)
    out = []
    for k, (cx, cy) in enumerate(chips):
        ck = 2 * cx + cy
        for a, (src, dst) in enumerate(big):
            out.append(_remote(src.at[:, ck], dst.at[:, k], sems, n * k + a, (cx, cy, c)))
        if small is not None:
            half, _ = _halves(SMALL_ROWS, c)
            out.append(_remote(small[0].at[half], small[1].at[k], sems, n * k + n - 1, (cx, cy, c)))
    return out


def _scatter_start(big, small, sems):
    for cp in _scatter_sends(big, small, sems):
        cp.start()


def _scatter_finish(big, small, sems):
    x, y, c, _ = _position()
    n = len(big) + (small is not None)
    for k in range(3):
        landing = [dst.at[:, k] for _, dst in big] + ([small[1].at[k]] if small is not None else [])
        for a, blk in enumerate(landing):
            _remote(blk, blk, sems, n * k + a, (x, y, c)).wait_recv()
    for cp in _scatter_sends(big, small, sems):
        cp.wait_send()


def _exchange_shapes(gi, go):
    return [jax.ShapeDtypeStruct(a.shape[:2] + (a.shape[2] // 2, a.shape[3]), F32) for a in (gi, go)]


def _exchange_copies(srcs, dsts, small, sems):
    x, y, c, _ = _position()
    cps = [_remote(src.at[:, :, _halves(src.shape[2], c)[1]], dst, sems, a, (x, y, 1 - c))
           for a, (src, dst) in enumerate(zip(srcs, dsts))]
    if small is not None:
        cps.append(_remote(small[0], small[1], sems, len(srcs), (x, y, 1 - c)))
    return cps


def _exchange_start(srcs, dsts, small, sems):
    for cp in _exchange_copies(srcs, dsts, small, sems):
        cp.start()


def _exchange_finish(srcs, dsts, small, sems):
    for cp in _exchange_copies(srcs, dsts, small, sems):
        cp.wait()


def _exchange_sibling(gi, go, sm):
    def body(gi_ref, go_ref, sm_ref, ri_ref, ro_ref, rs_ref, send_sems, recv_sems):
        args = ((gi_ref, go_ref), (ri_ref, ro_ref), (sm_ref, rs_ref), (send_sems, recv_sems))
        _exchange_start(*args)
        _exchange_finish(*args)

    out_shape = _exchange_shapes(gi, go) + [jax.ShapeDtypeStruct(sm.shape, F32)]
    return _comm_call(body, "exchange_sibling", 3, out_shape, 3)(gi, go, sm)


def _scatter_shapes(pi, po):
    return [jax.ShapeDtypeStruct((a.shape[0], 3) + a.shape[2:], a.dtype) for a in (pi, po)]


def _scatter_chips(pi, po, ps):
    def body(pi_ref, po_ref, ps_ref, ri_ref, ro_ref, rs_ref, send_sems, recv_sems):
        big, small, sems = ((pi_ref, ri_ref), (po_ref, ro_ref)), (ps_ref, rs_ref), (send_sems, recv_sems)
        _scatter_start(big, small, sems)
        _scatter_finish(big, small, sems)

    out_shape = _scatter_shapes(pi, po) + [jax.ShapeDtypeStruct((3, SMALL_HALF, 128), F32)]
    return _comm_call(body, "scatter_chips", 3, out_shape, SCATTER_SEMS)(pi, po, ps)


def _join_sibling(fi, fo, fs):
    def body(fi_ref, fo_ref, fs_ref, oi_ref, oo_ref, os_ref, send_sems, recv_sems):
        x, y, c, _ = _position()
        sems = (send_sems, recv_sems)
        sent, got = [], []
        for a, (src, dst) in enumerate(((fi_ref, oi_ref), (fo_ref, oo_ref))):
            mine, other = _halves(src.shape[1], c)
            sent.append(_remote(src.at[:, mine], dst.at[:, mine], sems, a, (x, y, 1 - c)))
            got.append(_remote(dst.at[:, other], dst.at[:, other], sems, a, (x, y, c)))
        mine, other = _halves(SMALL_ROWS, c)
        sent.append(_remote(fs_ref.at[mine], os_ref.at[mine], sems, 2, (x, y, 1 - c)))
        got.append(_remote(os_ref.at[other], os_ref.at[other], sems, 2, (x, y, c)))
        for cp in sent:
            cp.start()
        for cp in got:
            cp.wait_recv()
        for cp in sent:
            cp.wait_send()

    out_shape = [jax.ShapeDtypeStruct(a.shape, F32) for a in (fi, fo, fs)]
    return _comm_call(body, "join_sibling", 3, out_shape, 3, aliases={0: 0, 1: 1, 2: 2})(fi, fo, fs)


SMALL_SIZES = (DEPTH * D_MODEL, D_MODEL, DEPTH * POOL_WIDTH, 128, DEPTH * 4 * GROUP_DIM * GROUP_DIM)
LOSS_SLOT = DEPTH * D_MODEL + D_MODEL + DEPTH * POOL_WIDTH + DEPTH * 8


def _pack_small(norm_g, final_g, pool_scale, forget_bias, pool_w, loss=None):
    fb = forget_bias.reshape(-1)
    if loss is not None:
        fb = jnp.concatenate([fb, loss.reshape(1)])
    fb = jnp.pad(fb, (0, 128 - fb.size))
    flat = jnp.concatenate([norm_g.reshape(-1), final_g.reshape(-1), pool_scale.reshape(-1), fb,
                            pool_w.reshape(-1)])
    return jnp.pad(flat, (0, SMALL_ROWS * 128 - flat.size)).reshape(SMALL_ROWS, 128)


def _unpack_small(packed):
    flat = packed.reshape(-1)
    offs = [0]
    for n in SMALL_SIZES:
        offs.append(offs[-1] + n)
    norm_g, final_g, pool_scale, fb, pool_w = [flat[offs[i]:offs[i + 1]] for i in range(5)]
    return (norm_g.reshape(DEPTH, D_MODEL), final_g, pool_scale.reshape(DEPTH, POOL_WIDTH),
            fb[:DEPTH * 8].reshape(DEPTH, 8), pool_w.reshape(DEPTH, 4, GROUP_DIM, GROUP_DIM))


def kernel(x, norm_g, w_in, forget_bias, pool_w, pool_scale, w_out, final_g, loss_target, m_norm_g, m_w_in, m_forget_bias, m_pool_w, m_pool_scale, m_w_out, m_final_g, v_norm_g, v_w_in, v_forget_bias, v_pool_w, v_pool_scale, v_w_out, v_final_g):
    x0 = x[0]
    tgt = loss_target[0]
    s = x0.shape[0]

    me = 2 * lax.axis_index("x") + lax.axis_index("y")
    place = jnp.stack([me, lax.axis_index("c")]).astype(jnp.int32)
    wi_b, wo_b = w_in.astype(BF16), w_out.astype(BF16)

    def chip_piece(own, got, chip):
        rel = jnp.bitwise_xor(chip, me)
        if own is None:
            return lax.dynamic_index_in_dim(got, rel, axis=1, keepdims=False)[0]
        other = lax.dynamic_index_in_dim(got, jnp.maximum(rel, 1), axis=1, keepdims=False)
        return jnp.where(rel == 0, own, other)[0]

    def whole_w_out(l, go, own_gathered):
        own_o = None if own_gathered else wo_b[l:l + 1]
        return jnp.concatenate([chip_piece(own_o, go, k) for k in range(N_CHIPS)], axis=0)

    pool_w_b = pool_w.astype(BF16)
    fb_pad = jnp.pad(forget_bias, ((0, 0), (0, 128 - forget_bias.shape[1])))
    head_sel = (jnp.arange(ATTN_WIDTH)[None, :] // HEAD_DIM == jnp.arange(8)[:, None]).astype(BF16)

    saved, weights = [], []
    xl = x0
    gathered = _gather_weights(wi_b[0:1], wo_b[0:1])
    for l in range(DEPTH):
        w_o = whole_w_out(l, gathered[1], own_gathered=l > 0)
        pu, pg, q, k, v, ag, lf, c, h, w_i = _inproj_fwd(place, xl, norm_g[l][None], None if l > 0 else wi_b[0],
                                                         gathered[0], fb_pad[l][None])
        weights.append((w_i, w_o))
        po = _pool_fwd(pu, pg, pool_w_b[l], pool_scale[l][None])
        if l + 1 < DEPTH:
            ma, o, lse, *gathered = _attn_fwd(q, k, v, c, ag, carry=(wi_b[l + 1:l + 2], wo_b[l + 1:l + 2]))
        else:
            ma, o, lse = _attn_fwd(q, k, v, c, ag)
        saved.append((xl, h, pu, pg, q, k, v, ag, lf, c, po, ma, o, lse))
        xl = _outproj_fwd(xl, po, ma, w_o)

    dx, g_final, loss_part = _loss_head(xl, tgt, final_g[None])

    g_norm, g_fb, g_pw, g_ps = [], [], [], []
    parts = {}
    pending = None
    for l in reversed(range(DEPTH)):
        xl, h, pu, pg, q, k, v, ag, lf, c, po, ma, o, lse = saved[l]
        w_i, w_o = weights[l]
        dmp, do, dag, dt, gwo = _outproj_bwd(dx, po, ma, ag, o, w_o, head_sel)
        if pending is None:
            dq, dk, dv, dcs, drs = _attn_bwd(q, k, v, do, c, lse, dt.reshape(N_PAIRS, 2, s))
        else:
            dq, dk, dv, dcs, drs, qi, qo = _attn_bwd(q, k, v, do, c, lse, dt.reshape(N_PAIRS, 2, s), carry=pending)
            parts[l + 1] += (qi, qo)
        drs = jnp.pad(drs[:, 0:2, :].reshape(8, s).T, ((0, 0), (0, 120)))
        dpf, dfb = _forget_bwd(dcs, drs, lf)
        dpu, dpg, gpw, gps = _pool_bwd(pu, pg, dmp, pool_w_b[l], pool_scale[l][None])
        dsec = (dpu, dpg, dq, dk, dv, dag, dpf)
        gwi = _inproj_bwd_w(h, dsec)
        g_fb.append(dfb[0, :8])
        g_pw.append(gpw)
        g_ps.append(gps[0])
        send_i = gwi[None]
        send_o = gwo.reshape(1, N_CHIPS, SHARD_ROWS, D_MODEL)
        if l > 0:
            dx, gn, ri, ro = _inproj_bwd_x(dsec, w_i, xl, norm_g[l][None], dx, exchange=(send_i, send_o))
            g_norm.append(gn[0])
        else:
            dx, gn = _inproj_bwd_x(dsec, w_i, xl, norm_g[l][None], dx)
            g_norm.append(gn[0])
            g_norm, g_fb, g_pw, g_ps = [t[::-1] for t in (g_norm, g_fb, g_pw, g_ps)]
            send_s = _pack_small(jnp.stack(g_norm), g_final[0], jnp.stack(g_ps), jnp.stack(g_fb), jnp.stack(g_pw),
                                 loss=loss_part[0, 0])
            ri, ro, rs = _exchange_sibling(send_i, send_o, send_s)
        pending = (_chip_sum(place, send_i, ri, "chip_sum_w_in", 256),
                   _chip_sum(place, send_o, ro, "chip_sum_w_out", SHARD_ROWS // 2))
        parts[l] = (send_i, ri, send_o, ro)
    ps_ = _add2(send_s, rs, "chip_sum_small", SMALL_ROWS // 4)
    qi, qo, qs = _scatter_chips(*pending, ps_)
    parts[0] += (qi, qo)
    fi = lax.empty((DEPTH, D_MODEL, SHARD_COLS), F32)
    fo = lax.empty((DEPTH, SHARD_ROWS, D_MODEL), F32)
    for l in range(DEPTH):
        send_i, ri, send_o, ro, qi, qo = parts[l]
        fi = _mesh_sum(place, send_i, ri, qi, fi, l, "mesh_sum_w_in", 256)
        fo = _mesh_sum(place, send_o, ro, qo, fo, l, "mesh_sum_w_out", SHARD_ROWS // 2)
    hs = _mesh_sum_small(place, ps_, qs, SMALL_HALF // 4)
    fi, fo, fs = _join_sibling(fi, fo, hs)

    col_major = lambda a: jnp.transpose(a, (2, 0, 1))
    flat_o = lambda a: a.reshape(-1, D_MODEL)
    out_i = _adamw(col_major(fi), col_major(w_in), col_major(m_w_in), col_major(v_w_in), "adamw_w_in", 77)
    out_i = [jnp.transpose(t, (1, 2, 0)) for t in out_i]
    out_o = _adamw(flat_o(fo), flat_o(w_out), flat_o(m_w_out), flat_o(v_w_out), "adamw_w_out", 256)
    out_s = _adamw(fs, _pack_small(norm_g, final_g, pool_scale, forget_bias, pool_w),
                   _pack_small(m_norm_g, m_final_g, m_pool_scale, m_forget_bias, m_pool_w),
                   _pack_small(v_norm_g, v_final_g, v_pool_scale, v_forget_bias, v_pool_w),
                   "adamw_small", SMALL_ROWS // 4)

    loss = out_s[0].reshape(-1)[LOSS_SLOT]
    groups = []
    for oi, oo, os_ in zip(out_i, out_o, out_s):
        sn, sf, sp, sb, sw = _unpack_small(os_)
        groups.append((sn, oi.reshape(w_in.shape), sb, sw, sp, oo.reshape(w_out.shape), sf))
    return (loss, dx[None]) + tuple(t for grp in groups for t in grp)
```

```python
import jax
import jax.numpy as jnp
from jax import lax
from jax.experimental import pallas as pl
from jax.experimental.pallas import tpu as pltpu

F32 = jnp.float32
BF16 = jnp.bfloat16

D_MODEL = 1024
DEPTH = 4
POOL_WIDTH = 512
ATTN_WIDTH = 512
HEAD_DIM = 64
PAIR = 2 * HEAD_DIM
N_PAIRS = ATTN_WIDTH // PAIR
POOL_WINDOWS = (2, 4, 8, 16)
GROUP_DIM = 128
HALO = 16
IN_COLS = 3080
OFF_F = 3072
IN_COLS_PAD = 3200
SECTIONS = ((0, 512), (512, 512), (1024, 512), (1536, 512), (2048, 512), (2560, 512), (OFF_F, 128))
N_CHIPS = 4
SHARD_COLS = IN_COLS // N_CHIPS
SHARD_ROWS = D_MODEL // N_CHIPS
RMS_EPS = 1e-6
NEG = -1e30
Q_SCALE = 0.125

ADAM_LR = 0.001
ADAM_B1 = 0.9
ADAM_B2 = 0.999
ADAM_EPS = 1e-08
ADAM_WD = 0.01
ADAM_STEP = 10

SMALL_ROWS = 2112
SMALL_HALF = SMALL_ROWS // 2

NT = (((1,), (1,)), ((), ()))
TN = (((0,), (0,)), ((), ()))
MESH = pl.DeviceIdType.MESH
ANY = pl.BlockSpec(memory_space=pl.ANY)


def _params(semantics, vmem_mb=48):
    return pltpu.CompilerParams(dimension_semantics=semantics, vmem_limit_bytes=vmem_mb << 20)


def _row_spec(tm, cols):
    return pl.BlockSpec((tm, cols), lambda i: (i, 0))


def _full_spec(shape):
    return pl.BlockSpec(shape, lambda *_: (0,) * len(shape))


def _sigmoid(x):
    return jax.nn.sigmoid(x)


def _scan_rows(a, reverse=False):
    n = a.shape[0]
    row = lax.broadcasted_iota(jnp.int32, a.shape, 0)
    k = 1
    while k < n:
        if reverse:
            a = a + jnp.where(row < n - k, pltpu.roll(a, n - k, 0), 0.0)
        else:
            a = a + jnp.where(row >= k, pltpu.roll(a, k, 0), 0.0)
        k *= 2
    return a


def _inproj_fwd(place, x, g, own, got, fb, tm=512):
    s = x.shape[0]

    def body(place_ref, x_ref, g_ref, fb_ref, *refs):
        own_ref = refs[0] if own is not None else None
        (got_ref, pu_ref, pg_ref, q_ref, k_ref, v_ref, ag_ref,
         lf_ref, c_ref, h_ref, w_ref, carry_ref, piece_ref, sem) = refs[own is not None:]

        @pl.when(pl.program_id(0) == 0)
        def _():
            carry_ref[...] = jnp.zeros_like(carry_ref)
            for chip in range(N_CHIPS):
                rel = jnp.bitwise_xor(chip, place_ref[0])

                def fetch(src):
                    cp = pltpu.make_async_copy(src, piece_ref, sem)
                    cp.start()
                    cp.wait()

                if own_ref is None:
                    fetch(got_ref.at[0, rel])
                else:
                    pl.when(rel == 0)(lambda: fetch(own_ref))
                    pl.when(rel != 0)(lambda: fetch(got_ref.at[0, rel]))
                w_ref[:, chip * SHARD_COLS:(chip + 1) * SHARD_COLS] = piece_ref[...]
            w_ref[:, IN_COLS:] = jnp.zeros((D_MODEL, IN_COLS_PAD - IN_COLS), BF16)

        xf = x_ref[...]
        r = lax.rsqrt(jnp.mean(xf * xf, axis=-1, keepdims=True) + RMS_EPS)
        h = (xf * r * g_ref[...]).astype(BF16)
        h_ref[...] = h

        def proj(sec):
            off, n = SECTIONS[sec]
            return jnp.dot(h, w_ref[:, off:off + n], preferred_element_type=F32)

        pu_ref[...] = proj(0).astype(BF16)
        pg_ref[...] = proj(1).astype(BF16)
        q_ref[...] = (proj(2) * Q_SCALE).astype(BF16)
        k_ref[...] = proj(3).astype(BF16)
        v_ref[...] = proj(4).astype(BF16)
        ag_ref[...] = proj(5).astype(BF16)
        z = proj(6) + fb_ref[...]
        lf = jnp.minimum(z, 0.0) - jnp.log(1.0 + jnp.exp(-jnp.abs(z)))
        lf_ref[...] = lf
        c_ref[...] = _scan_rows(lf) + carry_ref[0:1, :]
        carry_ref[0:1, :] = c_ref[tm - 1:tm, :]

    act = jax.ShapeDtypeStruct((s, 512), BF16)
    rows = lambda cols: pl.BlockSpec((tm, cols), lambda i, p: (i, 0))
    whole = lambda shape: pl.BlockSpec(shape, lambda i, p: (0,) * len(shape))
    shards = ([] if own is None else [own]) + [got]
    return pl.pallas_call(
        body, name="inproj_fwd" if own is None else "inproj_fwd_first",
        grid_spec=pltpu.PrefetchScalarGridSpec(
            num_scalar_prefetch=1, grid=(s // tm,),
            in_specs=[rows(D_MODEL), whole((1, D_MODEL)), whole((1, 128))] + [ANY] * len(shards),
            out_specs=[rows(512)] * 6 + [rows(128), rows(128), rows(D_MODEL), whole((D_MODEL, IN_COLS_PAD))],
            scratch_shapes=[pltpu.VMEM((8, 128), F32), pltpu.VMEM((D_MODEL, SHARD_COLS), BF16),
                            pltpu.SemaphoreType.DMA]),
        out_shape=[act] * 6 + [jax.ShapeDtypeStruct((s, 128), F32), jax.ShapeDtypeStruct((s, 128), F32),
                               jax.ShapeDtypeStruct((s, D_MODEL), BF16),
                               jax.ShapeDtypeStruct((D_MODEL, IN_COLS_PAD), BF16)],
        compiler_params=_params(("arbitrary",)),
    )(place, x, g, fb, *shards)


def _window_sums(ext, forward):
    n = ext.shape[0]
    outs = []
    for gi, w in enumerate(POOL_WINDOWS):
        a = ext[:, gi * GROUP_DIM:(gi + 1) * GROUP_DIM]
        k = 1
        while k < w:
            a = a + pltpu.roll(a, (n - k) if forward else k, 0)
            k *= 2
        outs.append(a)
    return outs


def _over_count(a, row0, gi):
    w = POOL_WINDOWS[gi]
    t = row0 + lax.broadcasted_iota(jnp.int32, (HALO, GROUP_DIM), 0)
    head = a[0:HALO, :] / jnp.minimum(t + 1, w).astype(F32)
    return jnp.concatenate([head, a[HALO:, :] * (1.0 / w)], axis=0)


def _pool_delta(u, halo, row0):
    sums = _window_sums(jnp.concatenate([halo, u], axis=0), forward=False)
    return [_over_count(sums[gi][HALO:, :], row0, gi) - u[:, gi * GROUP_DIM:(gi + 1) * GROUP_DIM]
            for gi in range(len(POOL_WINDOWS))]


def _pool_fwd(pu, pg, pw, ps, tm=512):
    s = pu.shape[0]
    hb = tm // HALO

    def body(pu_ref, halo_ref, pg_ref, pw_ref, ps_ref, po_ref):
        i = pl.program_id(0)
        u = pu_ref[...].astype(F32)
        halo = jnp.where(i == 0, 0.0, halo_ref[...].astype(F32))
        d = _pool_delta(u, halo, i * tm)
        for gi in range(len(POOL_WINDOWS)):
            cols = slice(gi * GROUP_DIM, (gi + 1) * GROUP_DIM)
            z = jnp.dot(d[gi].astype(BF16), pw_ref[gi], preferred_element_type=F32)
            gate = pg_ref[:, cols].astype(F32)
            po_ref[:, cols] = (z * ps_ref[:, cols] * (gate * _sigmoid(gate))).astype(BF16)

    return pl.pallas_call(
        body, name="pool_fwd", grid=(s // tm,),
        in_specs=[_row_spec(tm, 512),
                  pl.BlockSpec((HALO, 512), lambda i: (jnp.maximum(i * hb - 1, 0), 0)),
                  _row_spec(tm, 512), _full_spec((4, GROUP_DIM, GROUP_DIM)), _full_spec((1, 512))],
        out_specs=_row_spec(tm, 512),
        out_shape=jax.ShapeDtypeStruct((s, 512), BF16),
        compiler_params=_params(("parallel",)),
    )(pu, pu, pg, pw, ps)


def _split3(x):
    hi = x.astype(BF16).astype(F32)
    r1 = x - hi
    lo = r1.astype(BF16).astype(F32)
    return hi, lo, (r1 - lo).astype(BF16).astype(F32)


def _own(idx, h):
    return (idx < HEAD_DIM) if h == 0 else (idx >= HEAD_DIM)


def _spare(h):
    return HEAD_DIM if h == 0 else 0


def _attn_fwd(q, k, v, c, ag, carry=None, tq=512, tk=512, chunk=512):
    s = q.shape[0]
    n_carry = 0 if carry is None else len(carry)

    def body(*refs):
        q_ref, k_ref, v_ref, c_ref, ag_ref = refs[:5]
        carry_src = refs[5:5 + n_carry]
        ma_ref, o_ref, lse_ref = refs[5 + n_carry:8 + n_carry]
        carry_dst = refs[8 + n_carry:8 + 2 * n_carry]
        kx_ref, vt_ref, acc0_ref, acc1_ref, sta_ref, stb_ref = refs[8 + 2 * n_carry:14 + 2 * n_carry]
        carry_sems = refs[14 + 2 * n_carry:]
        j = pl.program_id(0)
        i = pl.program_id(1)

        if n_carry:
            @pl.when((j == 0) & (i == 0))
            def _():
                _gather_start(carry_src, carry_dst, carry_sems[:2])
                for cp in _gather_own(carry_src, carry_dst, carry_sems[2]):
                    cp.start()

        @pl.when(i == 0)
        def _():
            def prep(ch, carry):
                rows = pl.ds(pl.multiple_of(ch * chunk, chunk), chunk)
                kk = k_ref[rows, :].astype(F32)
                cc = c_ref[rows, :]
                vt = jnp.transpose(v_ref[rows, :].astype(F32))
                lane = lax.broadcasted_iota(jnp.int32, (chunk, PAIR), 1)
                sub = lax.broadcasted_iota(jnp.int32, (PAIR, chunk), 0)
                for h in range(2):
                    ccol = jnp.sum(jnp.where(lane == 2 * j + h, cc, 0.0), axis=1, keepdims=True)
                    kx = jnp.where(_own(lane, h), kk, 0.0)
                    for t, part in enumerate(_split3(-ccol)):
                        kx = jnp.where(lane == _spare(h) + t, part, kx)
                    kx_ref[h, rows, :] = kx.astype(BF16)
                    vt_ref[h, :, rows] = jnp.where(_own(sub, h), vt,
                                                   jnp.where(sub == _spare(h), 1.0, 0.0)).astype(BF16)
                return carry

            lax.fori_loop(0, s // chunk, prep, 0)

        qq = q_ref[...].astype(F32)
        lane_q = lax.broadcasted_iota(jnp.int32, (tq, PAIR), 1)
        qx = []
        for h in range(2):
            ones = (lane_q >= _spare(h)) & (lane_q < _spare(h) + 3)
            qx.append(jnp.transpose(jnp.where(_own(lane_q, h), qq, jnp.where(ones, 1.0, 0.0))).astype(BF16))
        accs = (acc0_ref, acc1_ref)
        for acc in accs:
            acc[...] = jnp.zeros_like(acc)

        def rows_of(kb):
            return pl.ds(pl.multiple_of(kb * tk, tk), tk)

        def scores(kb, dst):
            for h in range(2):
                dst[h] = jnp.dot(kx_ref[h, rows_of(kb), :], qx[h], preferred_element_type=F32)

        def consume(kb, src, m, masked):
            m_out, alpha, pv = [], [], []
            for h in range(2):
                sh = src[h]
                if masked:
                    kpos = kb * tk + lax.broadcasted_iota(jnp.int32, (tk, tq), 0)
                    qpos = i * tq + lax.broadcasted_iota(jnp.int32, (tk, tq), 1)
                    sh = jnp.where(qpos >= kpos, sh, NEG)
                m_new = jnp.maximum(m[h], jnp.max(sh, axis=0, keepdims=True))
                pt = jnp.exp(sh - m_new).astype(BF16)
                alpha.append(jnp.exp(m[h] - m_new))
                pv.append(jnp.dot(vt_ref[h, :, rows_of(kb)], pt, preferred_element_type=F32))
                m_out.append(m_new)
            for h in range(2):
                accs[h][...] = accs[h][...] * alpha[h] + pv[h]
            return tuple(m_out)

        n_full = (i * tq) // tk

        def two_blocks(t, m):
            kb = 2 * t
            scores(kb + 1, stb_ref)
            m = consume(kb, sta_ref, m, False)
            scores(kb + 2, sta_ref)
            return consume(kb + 1, stb_ref, m, False)

        def odd_tail(m):
            scores(n_full, stb_ref)
            m = consume(n_full - 1, sta_ref, m, False)
            return consume(n_full, stb_ref, m, True)

        def even_tail(m):
            return consume(n_full, sta_ref, m, True)

        scores(0, sta_ref)
        m0 = jnp.full((1, tq), NEG, F32)
        m = lax.fori_loop(0, n_full // 2, two_blocks, (m0, m0))
        m = lax.cond(lax.rem(n_full, 2) == 1, odd_tail, even_tail, m)

        sub_o = lax.broadcasted_iota(jnp.int32, (PAIR, tq), 0)
        den = [accs[h][_spare(h):_spare(h) + 1, :] for h in range(2)]
        ot = jnp.where(sub_o < HEAD_DIM, acc0_ref[...] / den[0], acc1_ref[...] / den[1])
        o = jnp.transpose(ot)
        o_ref[...] = o.astype(BF16)
        gate = ag_ref[...].astype(F32)
        ma_ref[...] = (o * (gate * _sigmoid(gate))).astype(BF16)
        sub8 = lax.broadcasted_iota(jnp.int32, (8, tq), 0)
        lse_ref[...] = jnp.where(sub8 == 0, m[0] + jnp.log(den[0]), m[1] + jnp.log(den[1]))

        if n_carry:
            @pl.when((j == N_PAIRS - 1) & (i == s // tq - 1))
            def _():
                _gather_finish(carry_src, carry_dst, carry_sems[:2])
                for cp in _gather_own(carry_src, carry_dst, carry_sems[2]):
                    cp.wait()

    carried = [] if carry is None else list(carry)
    carry_sem = ([pltpu.SemaphoreType.DMA((GATHER_SEMS,))] * 2 + [pltpu.SemaphoreType.DMA((2,))]) if n_carry else []
    return pl.pallas_call(
        body, name="attn_fwd_gather" if n_carry else "attn_fwd", grid=(N_PAIRS, s // tq),
        in_specs=[pl.BlockSpec((tq, PAIR), lambda j, i: (i, j)),
                  pl.BlockSpec((s, PAIR), lambda j, i: (0, j)),
                  pl.BlockSpec((s, PAIR), lambda j, i: (0, j)),
                  pl.BlockSpec((s, 128), lambda j, i: (0, 0)),
                  pl.BlockSpec((tq, PAIR), lambda j, i: (i, j))] + [ANY] * n_carry,
        out_specs=[pl.BlockSpec((tq, PAIR), lambda j, i: (i, j)),
                   pl.BlockSpec((tq, PAIR), lambda j, i: (i, j)),
                   pl.BlockSpec((None, 8, tq), lambda j, i: (j, 0, i))] + [ANY] * n_carry,
        out_shape=[jax.ShapeDtypeStruct((s, ATTN_WIDTH), BF16), jax.ShapeDtypeStruct((s, ATTN_WIDTH), BF16),
                   jax.ShapeDtypeStruct((N_PAIRS, 8, s), F32)] + (_gather_shapes(*carried) if n_carry else []),
        scratch_shapes=[pltpu.VMEM((2, s, PAIR), BF16), pltpu.VMEM((2, PAIR, s), BF16),
                        pltpu.VMEM((PAIR, tq), F32), pltpu.VMEM((PAIR, tq), F32),
                        pltpu.VMEM((2, tk, tq), F32), pltpu.VMEM((2, tk, tq), F32)] + carry_sem,
        compiler_params=_params(("arbitrary", "arbitrary")),
    )(q, k, v, c, ag, *carried)


def _outproj_fwd(x, po, ma, wo, tm=512):
    s = x.shape[0]

    def body(x_ref, po_ref, ma_ref, wo_ref, xn_ref):
        xn_ref[...] = (x_ref[...]
                       + jnp.dot(po_ref[...], wo_ref[0:POOL_WIDTH, :], preferred_element_type=F32)
                       + jnp.dot(ma_ref[...], wo_ref[POOL_WIDTH:, :], preferred_element_type=F32))

    return pl.pallas_call(
        body, name="outproj_fwd", grid=(s // tm,),
        in_specs=[_row_spec(tm, D_MODEL), _row_spec(tm, 512), _row_spec(tm, 512),
                  _full_spec((D_MODEL, D_MODEL))],
        out_specs=_row_spec(tm, D_MODEL),
        out_shape=jax.ShapeDtypeStruct((s, D_MODEL), F32),
        compiler_params=_params(("parallel",)),
    )(x, po, ma, wo)


def _loss_head(x, tgt, g, tm=512):
    s = x.shape[0]

    def body(x_ref, t_ref, g_ref, dx_ref, dg_ref, loss_ref):
        @pl.when(pl.program_id(0) == 0)
        def _():
            dg_ref[...] = jnp.zeros_like(dg_ref)
            loss_ref[...] = jnp.zeros_like(loss_ref)

        xf = x_ref[...]
        r = lax.rsqrt(jnp.mean(xf * xf, axis=-1, keepdims=True) + RMS_EPS)
        xh = xf * r
        gg = g_ref[...]
        e = xh * gg - t_ref[...]
        loss_ref[...] += 0.5 * jnp.sum(jnp.mean(e * e, axis=-1, keepdims=True))
        dy = e * (1.0 / D_MODEL)
        u = dy * gg
        dx_ref[...] = r * (u - xh * jnp.mean(xh * u, axis=-1, keepdims=True))
        dg_ref[...] += jnp.sum(dy * xh, axis=0, keepdims=True)

    return pl.pallas_call(
        body, name="loss_head", grid=(s // tm,),
        in_specs=[_row_spec(tm, D_MODEL), _row_spec(tm, D_MODEL), _full_spec((1, D_MODEL))],
        out_specs=[_row_spec(tm, D_MODEL), _full_spec((1, D_MODEL)), _full_spec((8, 128))],
        out_shape=[jax.ShapeDtypeStruct((s, D_MODEL), F32), jax.ShapeDtypeStruct((1, D_MODEL), F32),
                   jax.ShapeDtypeStruct((8, 128), F32)],
        compiler_params=_params(("arbitrary",)),
    )(x, tgt, g)


def _outproj_bwd(dx, po, ma, ag, o, wo, head_sel, tm=512):
    s = dx.shape[0]

    def body(dx_ref, po_ref, ma_ref, ag_ref, o_ref, wo_ref, sel_ref, dmp_ref, do_ref, dag_ref, dt_ref, gwo_ref):
        @pl.when(pl.program_id(0) == 0)
        def _():
            gwo_ref[...] = jnp.zeros_like(gwo_ref)

        dxb = dx_ref[...].astype(BF16)
        dm = lax.dot_general(dxb, wo_ref[...], NT, preferred_element_type=F32)
        dmp_ref[...] = dm[:, 0:POOL_WIDTH].astype(BF16)
        dma = dm[:, POOL_WIDTH:]
        gate = ag_ref[...].astype(F32)
        of = o_ref[...].astype(F32)
        sg = _sigmoid(gate)
        do = dma * (gate * sg)
        do_ref[...] = do.astype(BF16)
        dag_ref[...] = (dma * of * (sg * (1.0 + gate * (1.0 - sg)))).astype(BF16)
        prod = do * of
        hi = prod.astype(BF16)
        lo = (prod - hi.astype(F32)).astype(BF16)
        dt_ref[...] = (lax.dot_general(sel_ref[...], hi, NT, preferred_element_type=F32)
                       + lax.dot_general(sel_ref[...], lo, NT, preferred_element_type=F32))
        gwo_ref[0:POOL_WIDTH, :] += lax.dot_general(po_ref[...], dxb, TN, preferred_element_type=F32)
        gwo_ref[POOL_WIDTH:, :] += lax.dot_general(ma_ref[...], dxb, TN, preferred_element_type=F32)

    act = jax.ShapeDtypeStruct((s, 512), BF16)
    return pl.pallas_call(
        body, name="outproj_bwd", grid=(s // tm,),
        in_specs=[_row_spec(tm, D_MODEL)] + [_row_spec(tm, 512)] * 4
                 + [_full_spec((D_MODEL, D_MODEL)), _full_spec((8, ATTN_WIDTH))],
        out_specs=[_row_spec(tm, 512)] * 3 + [pl.BlockSpec((8, tm), lambda i: (0, i)),
                                              _full_spec((D_MODEL, D_MODEL))],
        out_shape=[act, act, act, jax.ShapeDtypeStruct((8, s), F32),
                   jax.ShapeDtypeStruct((D_MODEL, D_MODEL), F32)],
        compiler_params=_params(("arbitrary",)),
    )(dx, po, ma, ag, o, wo, head_sel)


def _attn_bwd(q, k, v, do, c, lse, dt, carry=None, tq=256, chunk=512):
    s = q.shape[0]
    tk = 2 * tq
    nq = s // tq
    n_carry = 0 if carry is None else len(carry)

    def body(*refs):
        q_ref, k_ref, v_ref, do_ref, c_ref, lse_ref, dt_ref = refs[:7]
        carry_src = refs[7:7 + n_carry]
        dq_ref, dk_ref, dv_ref, dcs_ref, drs_ref = refs[7 + n_carry:12 + n_carry]
        carry_dst = refs[12 + n_carry:12 + 2 * n_carry]
        (qxt_ref, doxt_ref, qm_ref, dox_ref, dqt0_ref, dqt1_ref, dk0_ref, dk1_ref, dvx_ref,
         kx_ref, vx_ref, ktm_ref, sa_ref, sb_ref) = refs[12 + 2 * n_carry:26 + 2 * n_carry]
        carry_sems = refs[26 + 2 * n_carry:]
        carry_big = tuple(zip(carry_src, carry_dst))
        j = pl.program_id(0)
        b = pl.program_id(1)
        dqts = (dqt0_ref, dqt1_ref)
        dks = (dk0_ref, dk1_ref)

        if n_carry:
            @pl.when((j == 0) & (b == 0))
            def _():
                _scatter_start(carry_big, None, carry_sems)

        @pl.when(b == 0)
        def _():
            def prep(ch, carry):
                rows = pl.ds(pl.multiple_of(ch * chunk, chunk), chunk)
                qq = q_ref[rows, :].astype(F32)
                dd = do_ref[rows, :].astype(F32)
                qt = jnp.transpose(qq)
                ddt = jnp.transpose(dd)
                lane = lax.broadcasted_iota(jnp.int32, (chunk, PAIR), 1)
                sub = lax.broadcasted_iota(jnp.int32, (PAIR, chunk), 0)
                for h in range(2):
                    sp = _spare(h)
                    qm_ref[h, rows, :] = jnp.where(_own(lane, h), qq, jnp.where(lane == sp, 1.0, 0.0)).astype(BF16)
                    dox_ref[h, rows, :] = jnp.where(_own(lane, h), dd, 0.0).astype(BF16)
                    qx = jnp.where(_own(sub, h), qt, jnp.where((sub >= sp) & (sub < sp + 3), 1.0, 0.0))
                    for t, part in enumerate(_split3(-lse_ref[pl.ds(h, 1), rows])):
                        qx = jnp.where(sub == sp + 3 + t, part, qx)
                    qxt_ref[h, :, rows] = qx.astype(BF16)
                    dx = jnp.where(_own(sub, h), ddt, 0.0)
                    for t, part in enumerate(_split3(-dt_ref[pl.ds(h, 1), rows])):
                        dx = jnp.where(sub == sp + t, part, dx)
                    doxt_ref[h, :, rows] = dx.astype(BF16)
                return carry

            lax.fori_loop(0, s // chunk, prep, 0)
            for ref in dqts:
                ref[...] = jnp.zeros_like(ref)

        kk = k_ref[...].astype(F32)
        vv = v_ref[...].astype(F32)
        cc = c_ref[...]
        kt = jnp.transpose(kk)
        lane = lax.broadcasted_iota(jnp.int32, (tk, PAIR), 1)
        sub = lax.broadcasted_iota(jnp.int32, (PAIR, tk), 0)
        for h in range(2):
            sp = _spare(h)
            ccol = jnp.sum(jnp.where(lane == 2 * j + h, cc, 0.0), axis=1, keepdims=True)
            kx = jnp.where(_own(lane, h), kk, jnp.where((lane >= sp + 3) & (lane < sp + 6), 1.0, 0.0))
            for t, part in enumerate(_split3(-ccol)):
                kx = jnp.where(lane == sp + t, part, kx)
            kx_ref[h] = kx.astype(BF16)
            vx_ref[h] = jnp.where(_own(lane, h), vv, jnp.where((lane >= sp) & (lane < sp + 3), 1.0, 0.0)).astype(BF16)
            ktm_ref[h] = jnp.where(_own(sub, h), kt, jnp.where(sub == sp, 1.0, 0.0)).astype(BF16)
        for ref in dks:
            ref[...] = jnp.zeros_like(ref)
        dvx_ref[...] = jnp.zeros_like(dvx_ref)

        def cols_of(i):
            return pl.ds(pl.multiple_of(i * tq, tq), tq)

        def scores(i, dst):
            cols = cols_of(jnp.minimum(i, nq - 1))
            for h in range(2):
                dst[h] = jnp.dot(kx_ref[h], qxt_ref[h, :, cols], preferred_element_type=F32)
                dst[2 + h] = jnp.dot(vx_ref[h], doxt_ref[h, :, cols], preferred_element_type=F32)

        def consume(i, src, masked):
            cols = cols_of(i)
            for h in range(2):
                arg = src[h]
                if masked:
                    kpos = b * tk + lax.broadcasted_iota(jnp.int32, (tk, tq), 0)
                    qpos = i * tq + lax.broadcasted_iota(jnp.int32, (tk, tq), 1)
                    arg = jnp.where(qpos >= kpos, arg, NEG)
                pt = jnp.exp(arg)
                dst = (pt * src[2 + h]).astype(BF16)
                dvx_ref[...] += jnp.dot(pt.astype(BF16), dox_ref[h, cols, :], preferred_element_type=F32)
                dks[h][...] += jnp.dot(dst, qm_ref[h, cols, :], preferred_element_type=F32)
                dqts[h][:, cols] += jnp.dot(ktm_ref[h], dst, preferred_element_type=F32)

        i0 = 2 * b

        def two_blocks(t, masked):
            i = i0 + 2 * t
            scores(i + 1, sb_ref)
            consume(i, sa_ref, masked)
            scores(i + 2, sa_ref)
            consume(i + 1, sb_ref, masked)

        def loop_body(t, carry):
            two_blocks(t, False)
            return carry

        scores(i0, sa_ref)
        two_blocks(0, True)
        lax.fori_loop(1, (nq - i0) // 2, loop_body, 0)

        dv_ref[...] = dvx_ref[...].astype(BF16)
        dk_ref[...] = jnp.where(lane < HEAD_DIM, dk0_ref[...], dk1_ref[...]).astype(BF16)
        dcs_ref[...] = jnp.where(lane == _spare(0), dk0_ref[...], jnp.where(lane == _spare(1), dk1_ref[...], 0.0))

        @pl.when(b == s // tk - 1)
        def _():
            sub8 = lax.broadcasted_iota(jnp.int32, (8, s), 0)
            drs_ref[...] = jnp.where(sub8 == 0, dqt0_ref[_spare(0):_spare(0) + 1, :],
                                     jnp.where(sub8 == 1, dqt1_ref[_spare(1):_spare(1) + 1, :], 0.0))
            sub_c = lax.broadcasted_iota(jnp.int32, (PAIR, chunk), 0)
            for ch in range(s // chunk):
                rows = pl.ds(ch * chunk, chunk)
                both = jnp.where(sub_c < HEAD_DIM, dqt0_ref[:, rows], dqt1_ref[:, rows])
                dq_ref[rows, :] = (jnp.transpose(both) * Q_SCALE).astype(BF16)

        if n_carry:
            @pl.when((j == N_PAIRS - 1) & (b == s // tk - 1))
            def _():
                _scatter_finish(carry_big, None, carry_sems)

    act = jax.ShapeDtypeStruct((s, ATTN_WIDTH), BF16)
    pair_rows = pl.BlockSpec((s, PAIR), lambda j, b: (0, j))
    pair_blk = pl.BlockSpec((tk, PAIR), lambda j, b: (b, j))
    stat = pl.BlockSpec((None, 2, s), lambda j, b: (j, 0, 0))
    carried = [] if carry is None else list(carry)
    carry_sem = [pltpu.SemaphoreType.DMA((SCATTER_SEMS,))] * 2 if n_carry else []
    return pl.pallas_call(
        body, name="attn_bwd_scatter" if n_carry else "attn_bwd", grid=(N_PAIRS, s // tk),
        in_specs=[pair_rows, pair_blk, pair_blk, pair_rows,
                  pl.BlockSpec((tk, 128), lambda j, b: (b, 0)),
                  pl.BlockSpec((None, 8, s), lambda j, b: (j, 0, 0)), stat] + [ANY] * n_carry,
        out_specs=[pair_rows, pair_blk, pair_blk, pair_blk,
                   pl.BlockSpec((None, 8, s), lambda j, b: (j, 0, 0))] + [ANY] * n_carry,
        out_shape=[act, act, act, jax.ShapeDtypeStruct((s, N_PAIRS * 128), F32),
                   jax.ShapeDtypeStruct((N_PAIRS, 8, s), F32)] + (_scatter_shapes(*carried) if n_carry else []),
        scratch_shapes=[pltpu.VMEM((2, PAIR, s), BF16), pltpu.VMEM((2, PAIR, s), BF16),
                        pltpu.VMEM((2, s, PAIR), BF16), pltpu.VMEM((2, s, PAIR), BF16),
                        pltpu.VMEM((PAIR, s), F32), pltpu.VMEM((PAIR, s), F32),
                        pltpu.VMEM((tk, PAIR), F32), pltpu.VMEM((tk, PAIR), F32), pltpu.VMEM((tk, PAIR), F32),
                        pltpu.VMEM((2, tk, PAIR), BF16), pltpu.VMEM((2, tk, PAIR), BF16),
                        pltpu.VMEM((2, PAIR, tk), BF16),
                        pltpu.VMEM((4, tk, tq), F32), pltpu.VMEM((4, tk, tq), F32)] + carry_sem,
        compiler_params=_params(("arbitrary", "arbitrary")),
    )(q, k, v, do, c, lse, dt, *carried)


def _forget_bwd(dcs, drs, lf, tm=512):
    s = lf.shape[0]
    n = s // tm

    def body(dcs_ref, drs_ref, lf_ref, dpf_ref, dfb_ref, carry_ref):
        @pl.when(pl.program_id(0) == 0)
        def _():
            carry_ref[...] = jnp.zeros_like(carry_ref)
            dfb_ref[...] = jnp.zeros_like(dfb_ref)

        lane = lax.broadcasted_iota(jnp.int32, (tm, 128), 1)
        dc = drs_ref[...]
        for pj in range(N_PAIRS):
            blk = dcs_ref[:, pj * 128:(pj + 1) * 128]
            for h in range(2):
                head = 2 * pj + h
                moved = pltpu.roll(blk, (head - _spare(h)) % 128, 1) if head != _spare(h) else blk
                dc = dc - jnp.where(lane == head, moved, 0.0)
        dlf = _scan_rows(dc, reverse=True) + carry_ref[0:1, :]
        carry_ref[...] = dlf[0:8, :]
        dz = jnp.where(lane < 8, dlf * (1.0 - jnp.exp(lf_ref[...])), 0.0)
        dpf_ref[...] = dz.astype(BF16)
        dfb_ref[...] += jnp.sum(dz, axis=0, keepdims=True)

    return pl.pallas_call(
        body, name="forget_bwd", grid=(n,),
        in_specs=[pl.BlockSpec((tm, N_PAIRS * 128), lambda i: (n - 1 - i, 0)),
                  pl.BlockSpec((tm, 128), lambda i: (n - 1 - i, 0)),
                  pl.BlockSpec((tm, 128), lambda i: (n - 1 - i, 0))],
        out_specs=[pl.BlockSpec((tm, 128), lambda i: (n - 1 - i, 0)), _full_spec((1, 128))],
        out_shape=[jax.ShapeDtypeStruct((s, 128), BF16), jax.ShapeDtypeStruct((1, 128), F32)],
        scratch_shapes=[pltpu.VMEM((8, 128), F32)],
        compiler_params=_params(("arbitrary",)),
    )(dcs, drs, lf)


def _pool_bwd(pu, pg, dmp, pw, ps, tm=512):
    s = pu.shape[0]
    hb = tm // HALO
    n = s // tm
    last_halo = s // HALO - 1

    def body(pu_ref, halo_ref, pg_ref, dmp_ref, pgn_ref, dmpn_ref, pw_ref, ps_ref,
             dpu_ref, dpg_ref, gpw_ref, gps_ref):
        i = pl.program_id(0)

        @pl.when(i == 0)
        def _():
            gpw_ref[...] = jnp.zeros_like(gpw_ref)
            gps_ref[...] = jnp.zeros_like(gps_ref)

        u = pu_ref[...].astype(F32)
        halo = jnp.where(i == 0, 0.0, halo_ref[...].astype(F32))
        d = _pool_delta(u, halo, i * tm)
        e_parts, dd_parts = [], []
        for gi in range(len(POOL_WINDOWS)):
            cols = slice(gi * GROUP_DIM, (gi + 1) * GROUP_DIM)
            wg = pw_ref[gi]
            scale = ps_ref[:, cols]
            db = d[gi].astype(BF16)
            z = jnp.dot(db, wg, preferred_element_type=F32)
            gate = pg_ref[:, cols].astype(F32)
            sg = _sigmoid(gate)
            dm = dmp_ref[:, cols].astype(F32)
            dy = dm * (gate * sg)
            dpg_ref[:, cols] = (dm * (z * scale) * (sg * (1.0 + gate * (1.0 - sg)))).astype(BF16)
            gps_ref[:, cols] += jnp.sum(dy * z, axis=0, keepdims=True)
            dz = (dy * scale).astype(BF16)
            gpw_ref[gi] += lax.dot_general(db, dz, TN, preferred_element_type=F32)
            dd = lax.dot_general(dz, wg, NT, preferred_element_type=F32)
            gate_n = pgn_ref[:, cols].astype(F32)
            dz_n = (dmpn_ref[:, cols].astype(F32) * (gate_n * _sigmoid(gate_n)) * scale).astype(BF16)
            dd_n = lax.dot_general(dz_n, wg, NT, preferred_element_type=F32)
            dd_n = jnp.where(i == n - 1, 0.0, dd_n)
            dd_parts.append(dd)
            e_parts.append(jnp.concatenate([_over_count(dd, i * tm, gi), dd_n * (1.0 / POOL_WINDOWS[gi])], axis=0))
        lead = _window_sums(jnp.concatenate(e_parts, axis=1), forward=True)
        for gi in range(len(POOL_WINDOWS)):
            cols = slice(gi * GROUP_DIM, (gi + 1) * GROUP_DIM)
            dpu_ref[:, cols] = (lead[gi][0:tm, :] - dd_parts[gi]).astype(BF16)

    act = jax.ShapeDtypeStruct((s, 512), BF16)
    prev_halo = pl.BlockSpec((HALO, 512), lambda i: (jnp.maximum(i * hb - 1, 0), 0))
    next_halo = pl.BlockSpec((HALO, 512), lambda i: (jnp.minimum((i + 1) * hb, last_halo), 0))
    return pl.pallas_call(
        body, name="pool_bwd", grid=(n,),
        in_specs=[_row_spec(tm, 512), prev_halo, _row_spec(tm, 512), _row_spec(tm, 512), next_halo, next_halo,
                  _full_spec((4, GROUP_DIM, GROUP_DIM)), _full_spec((1, 512))],
        out_specs=[_row_spec(tm, 512), _row_spec(tm, 512), _full_spec((4, GROUP_DIM, GROUP_DIM)),
                   _full_spec((1, 512))],
        out_shape=[act, act, jax.ShapeDtypeStruct((4, GROUP_DIM, GROUP_DIM), F32),
                   jax.ShapeDtypeStruct((1, 512), F32)],
        compiler_params=_params(("arbitrary",)),
    )(pu, pu, pg, dmp, pg, dmp, pw, ps)


def _inproj_bwd_x(dsec, w, x, g, dxo, exchange=None, scatter=None, tm=512):
    s = x.shape[0]
    n_steps = s // tm
    carried = list(exchange or scatter or ())
    n_carry = len(carried)

    def body(*refs):
        d_refs = refs[0:7]
        w_ref, x_ref, g_ref, dxo_ref = refs[7:11]
        carry_src = refs[11:11 + n_carry]
        dxi_ref, dg_ref = refs[11 + n_carry:13 + n_carry]
        carry_dst = refs[13 + n_carry:13 + 2 * n_carry]
        sems = refs[13 + 2 * n_carry:]
        step = pl.program_id(0)

        def comm(phase):
            if exchange is not None:
                (_exchange_start if phase == 0 else _exchange_finish)(carry_src, carry_dst, None, sems)
            else:
                big = tuple(zip(carry_src[:2], carry_dst[:2]))
                (_scatter_start if phase == 0 else _scatter_finish)(big, (carry_src[2], carry_dst[2]), sems)

        @pl.when(step == 0)
        def _():
            dg_ref[...] = jnp.zeros_like(dg_ref)
            if n_carry:
                comm(0)

        dh = None
        for d_ref, (off, ncol) in zip(d_refs, SECTIONS):
            t = lax.dot_general(d_ref[...], w_ref[:, off:off + ncol], NT, preferred_element_type=F32)
            dh = t if dh is None else dh + t
        xf = x_ref[...]
        r = lax.rsqrt(jnp.mean(xf * xf, axis=-1, keepdims=True) + RMS_EPS)
        xh = xf * r
        u = dh * g_ref[...]
        dxi_ref[...] = dxo_ref[...] + r * (u - xh * jnp.mean(xh * u, axis=-1, keepdims=True))
        dg_ref[...] += jnp.sum(dh * xh, axis=0, keepdims=True)

        if n_carry:
            @pl.when(step == n_steps - 1)
            def _():
                comm(1)

    if exchange is not None:
        name, carry_shapes, n_sems = "inproj_bwd_x_exchange", _exchange_shapes(*exchange), 3
    elif scatter is not None:
        name, n_sems = "inproj_bwd_x_scatter", SCATTER_SEMS
        carry_shapes = _scatter_shapes(*scatter[:2]) + [jax.ShapeDtypeStruct((3, SMALL_HALF, 128), F32)]
    else:
        name, carry_shapes, n_sems = "inproj_bwd_x", [], 0
    return pl.pallas_call(
        body, name=name, grid=(n_steps,),
        in_specs=[_row_spec(tm, ncol) for _, ncol in SECTIONS]
                 + [_full_spec((D_MODEL, IN_COLS_PAD)), _row_spec(tm, D_MODEL), _full_spec((1, D_MODEL)),
                    _row_spec(tm, D_MODEL)] + [ANY] * n_carry,
        out_specs=[_row_spec(tm, D_MODEL), _full_spec((1, D_MODEL))] + [ANY] * n_carry,
        out_shape=[jax.ShapeDtypeStruct((s, D_MODEL), F32), jax.ShapeDtypeStruct((1, D_MODEL), F32)] + carry_shapes,
        scratch_shapes=[pltpu.SemaphoreType.DMA((n_sems,))] * 2 if n_carry else [],
        compiler_params=_params(("arbitrary",)),
    )(*dsec, w, x, g, dxo, *carried)


def _inproj_bwd_w(h, dsec, tm=512):
    s = h.shape[0]
    n_steps = s // tm

    def body(*refs):
        h_ref = refs[0]
        d_refs = refs[1:8]
        out_ref, gw_ref = refs[8:]
        step = pl.program_id(0)

        @pl.when(step == 0)
        def _():
            gw_ref[...] = jnp.zeros_like(gw_ref)

        hh = h_ref[...]
        for d_ref, (off, ncol) in zip(d_refs, SECTIONS):
            gw_ref[:, off:off + ncol] += lax.dot_general(hh, d_ref[...], TN, preferred_element_type=F32)

        @pl.when(step == n_steps - 1)
        def _():
            for k in range(N_CHIPS):
                out_ref[k] = gw_ref[:, k * SHARD_COLS:(k + 1) * SHARD_COLS]

    return pl.pallas_call(
        body, name="inproj_bwd_w", grid=(n_steps,),
        in_specs=[_row_spec(tm, D_MODEL)] + [_row_spec(tm, ncol) for _, ncol in SECTIONS],
        out_specs=_full_spec((N_CHIPS, D_MODEL, SHARD_COLS)),
        out_shape=jax.ShapeDtypeStruct((N_CHIPS, D_MODEL, SHARD_COLS), F32),
        scratch_shapes=[pltpu.VMEM((D_MODEL, IN_COLS_PAD), F32)],
        compiler_params=_params(("arbitrary",), vmem_mb=60),
    )(h, *dsec)


def _elementwise(fn, name, n_out, arrays, tm):
    shape = arrays[0].shape
    rest = shape[1:]
    n_in = len(arrays)

    def body(*refs):
        outs = fn(*[r[...] for r in refs[:n_in]])
        for r, val in zip(refs[n_in:], outs):
            r[...] = val

    spec = pl.BlockSpec((tm,) + rest, lambda i: (i,) + (0,) * len(rest))
    return pl.pallas_call(
        body, name=name, grid=(shape[0] // tm,),
        in_specs=[spec] * n_in, out_specs=[spec] * n_out, out_shape=[jax.ShapeDtypeStruct(shape, F32)] * n_out,
        compiler_params=_params(("parallel",)),
    )(*arrays)


def _add2(a, b, name, tm):
    return _elementwise(lambda p, q: (p + q,), name, 1, [a, b], tm)[0]


def _chip_sum(place, g, r, name, tm):
    n_layers, _, rows, cols = g.shape
    half = rows // 2
    nb = half // tm

    def body(place_ref, g_ref, r_ref, o_ref):
        o_ref[...] = (g_ref[...] + r_ref[...]).astype(BF16)

    blk = (None, None, tm, cols)
    return pl.pallas_call(
        body, name=name,
        grid_spec=pltpu.PrefetchScalarGridSpec(
            num_scalar_prefetch=1, grid=(n_layers, N_CHIPS, nb),
            in_specs=[pl.BlockSpec(blk, lambda l, k, i, p: (l, k, p[1] * nb + i, 0)),
                      pl.BlockSpec(blk, lambda l, k, i, p: (l, k, i, 0))],
            out_specs=pl.BlockSpec(blk, lambda l, k, i, p: (l, k, i, 0))),
        out_shape=jax.ShapeDtypeStruct((n_layers, N_CHIPS, half, cols), BF16),
        compiler_params=_params(("parallel", "parallel", "parallel")),
    )(place, g, r)


def _mesh_sum(place, g, r, got, into, layer, name, tm):
    _, _, rows, cols = g.shape
    half = rows // 2
    nb = half // tm

    def body(place_ref, g_ref, r_ref, got_ref, into_ref, o_ref):
        own = g_ref[...] + r_ref[...]
        o_ref[...] = (own + got_ref[0].astype(F32)) + (got_ref[1].astype(F32) + got_ref[2].astype(F32))

    blk = (None, None, tm, cols)
    return pl.pallas_call(
        body, name=name,
        grid_spec=pltpu.PrefetchScalarGridSpec(
            num_scalar_prefetch=1, grid=(nb,),
            in_specs=[pl.BlockSpec(blk, lambda i, p: (0, p[0], p[1] * nb + i, 0)),
                      pl.BlockSpec(blk, lambda i, p: (0, p[0], i, 0)),
                      pl.BlockSpec((None, 3, tm, cols), lambda i, p: (0, 0, i, 0)),
                      ANY],
            out_specs=pl.BlockSpec((None, tm, cols), lambda i, p: (layer, p[1] * nb + i, 0))),
        out_shape=jax.ShapeDtypeStruct(into.shape, F32),
        input_output_aliases={4: 0},
        compiler_params=_params(("parallel",)),
    )(place, g, r, got, into)


def _mesh_sum_small(place, ps, got, tm):
    nb = SMALL_HALF // tm

    def body(place_ref, ps_ref, got_ref, o_ref):
        o_ref[...] = (ps_ref[...] + got_ref[0]) + (got_ref[1] + got_ref[2])

    return pl.pallas_call(
        body, name="mesh_sum_small",
        grid_spec=pltpu.PrefetchScalarGridSpec(
            num_scalar_prefetch=1, grid=(nb,),
            in_specs=[pl.BlockSpec((tm, 128), lambda i, p: (p[1] * nb + i, 0)),
                      pl.BlockSpec((3, tm, 128), lambda i, p: (0, i, 0))],
            out_specs=pl.BlockSpec((tm, 128), lambda i, p: (p[1] * nb + i, 0))),
        out_shape=jax.ShapeDtypeStruct((SMALL_ROWS, 128), F32),
        compiler_params=_params(("parallel",)),
    )(place, ps, got)


def _adamw(g, w, m, v, name, tm):
    def fn(g, w, m, v):
        m = ADAM_B1 * m + (1.0 - ADAM_B1) * g
        v = ADAM_B2 * v + (1.0 - ADAM_B2) * (g * g)
        m_hat = m / (1.0 - ADAM_B1 ** ADAM_STEP)
        v_hat = v / (1.0 - ADAM_B2 ** ADAM_STEP)
        delta = -ADAM_LR * (m_hat / (jnp.sqrt(v_hat) + ADAM_EPS) + ADAM_WD * w)
        return g, delta, m, v

    return _elementwise(fn, name, 4, [g, w, m, v], tm)


def _position():
    x, y, c = lax.axis_index("x"), lax.axis_index("y"), lax.axis_index("c")
    other_chips = [(x, 1 - y), (1 - x, y), (1 - x, 1 - y)]
    return x, y, c, other_chips


def _remote(src, dst, sems, k, to):
    send_sems, recv_sems = sems
    return pltpu.make_async_remote_copy(src_ref=src, dst_ref=dst, send_sem=send_sems.at[k],
                                        recv_sem=recv_sems.at[k], device_id=to, device_id_type=MESH)


def _comm_call(body, name, n_in, out_shape, n_remote, aliases=None):
    return pl.pallas_call(
        body, name=name, in_specs=[ANY] * n_in, out_specs=[ANY] * len(out_shape), out_shape=out_shape,
        scratch_shapes=[pltpu.SemaphoreType.DMA((n_remote,)), pltpu.SemaphoreType.DMA((n_remote,))],
        input_output_aliases=aliases or {},
    )


def _halves(ref_rows, c):
    half = ref_rows // 2
    return pl.ds(half * c, half), pl.ds(half * (1 - c), half)


def _gather_weights(wi, wo):
    def body(wi_ref, wo_ref, gi_ref, go_ref, send_sems, recv_sems):
        _gather_start((wi_ref, wo_ref), (gi_ref, go_ref), (send_sems, recv_sems))
        _gather_finish((wi_ref, wo_ref), (gi_ref, go_ref), (send_sems, recv_sems))

    return _comm_call(body, "gather_weights", 2, _gather_shapes(wi, wo), GATHER_SEMS)(wi, wo)


GATHER_SEMS = 12
SCATTER_SEMS = 9


def _gather_shapes(wi, wo):
    return [jax.ShapeDtypeStruct((a.shape[0], N_CHIPS) + a.shape[1:], a.dtype) for a in (wi, wo)]


def _gather_sends(srcs, dsts, sems):
    x, y, c, chips = _position()
    out = []
    for k, (cx, cy) in enumerate(chips):
        for a, (src, dst) in enumerate(zip(srcs, dsts)):
            mine, _ = _halves(src.shape[1], c)
            out.append(_remote(src.at[:, mine], dst.at[:, k + 1, mine], sems, len(srcs) * k + a, (cx, cy, c)))
    return out


def _gather_own(srcs, dsts, local_sems):
    return [pltpu.make_async_copy(src, dst.at[:, 0], local_sems.at[a]) for a, (src, dst) in enumerate(zip(srcs, dsts))]


def _gather_start(srcs, dsts, sems):
    for cp in _gather_sends(srcs, dsts, sems):
        cp.start()


def _gather_finish(srcs, dsts, sems):
    x, y, c, _ = _position()
    n = len(srcs)
    forwards = []
    for k in range(3):
        for a, dst in enumerate(dsts):
            mine, _ = _halves(dst.shape[2], c)
            blk = dst.at[:, k + 1, mine]
            _remote(blk, blk, sems, n * k + a, (x, y, c)).wait_recv()
            forwards.append(_remote(blk, blk, sems, 3 * n + n * k + a, (x, y, 1 - c)))
            forwards[-1].start()
    for k in range(3):
        for a, dst in enumerate(dsts):
            _, other = _halves(dst.shape[2], c)
            blk = dst.at[:, k + 1, other]
            _remote(blk, blk, sems, 3 * n + n * k + a, (x, y, c)).wait_recv()
    for cp in _gather_sends(srcs, dsts, sems) + forwards:
        cp.wait_send()


def _scatter_sends(big, small, sems):
    x, y, c, chips = _position()
    n = len(big) + (small is not None)
    out = []
    for k, (cx, cy) in enumerate(chips):
        ck = 2 * cx + cy
        for a, (src, dst) in enumerate(big):
            out.append(_remote(src.at[:, ck], dst.at[:, k], sems, n * k + a, (cx, cy, c)))
        if small is not None:
            half, _ = _halves(SMALL_ROWS, c)
            out.append(_remote(small[0].at[half], small[1].at[k], sems, n * k + n - 1, (cx, cy, c)))
    return out


def _scatter_start(big, small, sems):
    for cp in _scatter_sends(big, small, sems):
        cp.start()


def _scatter_finish(big, small, sems):
    x, y, c, _ = _position()
    n = len(big) + (small is not None)
    for k in range(3):
        landing = [dst.at[:, k] for _, dst in big] + ([small[1].at[k]] if small is not None else [])
        for a, blk in enumerate(landing):
            _remote(blk, blk, sems, n * k + a, (x, y, c)).wait_recv()
    for cp in _scatter_sends(big, small, sems):
        cp.wait_send()


def _exchange_shapes(gi, go):
    return [jax.ShapeDtypeStruct(a.shape[:2] + (a.shape[2] // 2, a.shape[3]), F32) for a in (gi, go)]


def _exchange_copies(srcs, dsts, small, sems):
    x, y, c, _ = _position()
    cps = [_remote(src.at[:, :, _halves(src.shape[2], c)[1]], dst, sems, a, (x, y, 1 - c))
           for a, (src, dst) in enumerate(zip(srcs, dsts))]
    if small is not None:
        cps.append(_remote(small[0], small[1], sems, len(srcs), (x, y, 1 - c)))
    return cps


def _exchange_start(srcs, dsts, small, sems):
    for cp in _exchange_copies(srcs, dsts, small, sems):
        cp.start()


def _exchange_finish(srcs, dsts, small, sems):
    for cp in _exchange_copies(srcs, dsts, small, sems):
        cp.wait()


def _exchange_sibling(gi, go, sm):
    def body(gi_ref, go_ref, sm_ref, ri_ref, ro_ref, rs_ref, send_sems, recv_sems):
        args = ((gi_ref, go_ref), (ri_ref, ro_ref), (sm_ref, rs_ref), (send_sems, recv_sems))
        _exchange_start(*args)
        _exchange_finish(*args)

    out_shape = _exchange_shapes(gi, go) + [jax.ShapeDtypeStruct(sm.shape, F32)]
    return _comm_call(body, "exchange_sibling", 3, out_shape, 3)(gi, go, sm)


def _scatter_shapes(pi, po):
    return [jax.ShapeDtypeStruct((a.shape[0], 3) + a.shape[2:], a.dtype) for a in (pi, po)]


def _join_sibling(fi, fo, fs, late):
    def body(fi_ref, fo_ref, fs_ref, late_ref, oi_ref, oo_ref, os_ref, sum_ref, send_sems, recv_sems,
             sib_ref, chip_ref, got_ref):
        x, y, c, chips = _position()
        sems = (send_sems, recv_sems)
        sent, got = [], []
        for a, (src, dst) in enumerate(((fi_ref, oi_ref), (fo_ref, oo_ref))):
            mine, other = _halves(src.shape[1], c)
            sent.append(_remote(src.at[:, mine], dst.at[:, mine], sems, a, (x, y, 1 - c)))
            got.append(_remote(dst.at[:, other], dst.at[:, other], sems, a, (x, y, c)))
        mine, other = _halves(SMALL_ROWS, c)
        sent.append(_remote(fs_ref.at[mine], os_ref.at[mine], sems, 2, (x, y, 1 - c)))
        got.append(_remote(os_ref.at[other], os_ref.at[other], sems, 2, (x, y, c)))
        for cp in sent:
            cp.start()

        to_sibling = _remote(late_ref, sib_ref, sems, 3, (x, y, 1 - c))
        to_sibling.start()
        to_sibling.wait()
        chip_ref[...] = late_ref[...] + sib_ref[...]
        to_chips = [_remote(chip_ref, got_ref.at[k], sems, 4 + k, (cx, cy, c)) for k, (cx, cy) in enumerate(chips)]
        for cp in to_chips:
            cp.start()
        for cp in to_chips:
            cp.wait()
        sum_ref[...] = (chip_ref[...] + got_ref[0]) + (got_ref[1] + got_ref[2])

        for cp in got:
            cp.wait_recv()
        for cp in sent:
            cp.wait_send()

    vmem = pl.BlockSpec(memory_space=pltpu.VMEM)
    return pl.pallas_call(
        body, name="join_sibling", in_specs=[ANY] * 3 + [vmem], out_specs=[ANY] * 3 + [vmem],
        out_shape=[jax.ShapeDtypeStruct(a.shape, F32) for a in (fi, fo, fs, late)],
        scratch_shapes=[pltpu.SemaphoreType.DMA((7,)), pltpu.SemaphoreType.DMA((7,)),
                        pltpu.VMEM(late.shape, F32), pltpu.VMEM(late.shape, F32), pltpu.VMEM((3,) + late.shape, F32)],
        input_output_aliases={0: 0, 1: 1, 2: 2},
    )(fi, fo, fs, late)


SMALL_SIZES = (DEPTH * D_MODEL, D_MODEL, DEPTH * POOL_WIDTH, 128, DEPTH * 4 * GROUP_DIM * GROUP_DIM)
LOSS_SLOT = DEPTH * D_MODEL + D_MODEL + DEPTH * POOL_WIDTH + DEPTH * 8


def _pack_small(norm_g, final_g, pool_scale, forget_bias, pool_w, loss=None):
    fb = forget_bias.reshape(-1)
    if loss is not None:
        fb = jnp.concatenate([fb, loss.reshape(1)])
    fb = jnp.pad(fb, (0, 128 - fb.size))
    flat = jnp.concatenate([norm_g.reshape(-1), final_g.reshape(-1), pool_scale.reshape(-1), fb,
                            pool_w.reshape(-1)])
    return jnp.pad(flat, (0, SMALL_ROWS * 128 - flat.size)).reshape(SMALL_ROWS, 128)


def _unpack_small(packed):
    flat = packed.reshape(-1)
    offs = [0]
    for n in SMALL_SIZES:
        offs.append(offs[-1] + n)
    norm_g, final_g, pool_scale, fb, pool_w = [flat[offs[i]:offs[i + 1]] for i in range(5)]
    return (norm_g.reshape(DEPTH, D_MODEL), final_g, pool_scale.reshape(DEPTH, POOL_WIDTH),
            fb[:DEPTH * 8].reshape(DEPTH, 8), pool_w.reshape(DEPTH, 4, GROUP_DIM, GROUP_DIM))


def kernel(x, norm_g, w_in, forget_bias, pool_w, pool_scale, w_out, final_g, loss_target, m_norm_g, m_w_in, m_forget_bias, m_pool_w, m_pool_scale, m_w_out, m_final_g, v_norm_g, v_w_in, v_forget_bias, v_pool_w, v_pool_scale, v_w_out, v_final_g):
    x0 = x[0]
    tgt = loss_target[0]
    s = x0.shape[0]

    me = 2 * lax.axis_index("x") + lax.axis_index("y")
    place = jnp.stack([me, lax.axis_index("c")]).astype(jnp.int32)
    wi_b, wo_b = w_in.astype(BF16), w_out.astype(BF16)

    def chip_piece(own, got, chip):
        rel = jnp.bitwise_xor(chip, me)
        if own is None:
            return lax.dynamic_index_in_dim(got, rel, axis=1, keepdims=False)[0]
        other = lax.dynamic_index_in_dim(got, jnp.maximum(rel, 1), axis=1, keepdims=False)
        return jnp.where(rel == 0, own, other)[0]

    def whole_w_out(l, go, own_gathered):
        own_o = None if own_gathered else wo_b[l:l + 1]
        return jnp.concatenate([chip_piece(own_o, go, k) for k in range(N_CHIPS)], axis=0)

    pool_w_b = pool_w.astype(BF16)
    fb_pad = jnp.pad(forget_bias, ((0, 0), (0, 128 - forget_bias.shape[1])))
    head_sel = (jnp.arange(ATTN_WIDTH)[None, :] // HEAD_DIM == jnp.arange(8)[:, None]).astype(BF16)

    saved, weights = [], []
    xl = x0
    gathered = _gather_weights(wi_b[0:1], wo_b[0:1])
    for l in range(DEPTH):
        w_o = whole_w_out(l, gathered[1], own_gathered=l > 0)
        pu, pg, q, k, v, ag, lf, c, h, w_i = _inproj_fwd(place, xl, norm_g[l][None], None if l > 0 else wi_b[0],
                                                         gathered[0], fb_pad[l][None])
        weights.append((w_i, w_o))
        po = _pool_fwd(pu, pg, pool_w_b[l], pool_scale[l][None])
        if l + 1 < DEPTH:
            ma, o, lse, *gathered = _attn_fwd(q, k, v, c, ag, carry=(wi_b[l + 1:l + 2], wo_b[l + 1:l + 2]))
        else:
            ma, o, lse = _attn_fwd(q, k, v, c, ag)
        saved.append((xl, h, pu, pg, q, k, v, ag, lf, c, po, ma, o, lse))
        xl = _outproj_fwd(xl, po, ma, w_o)

    dx, g_final, loss_part = _loss_head(xl, tgt, final_g[None])

    g_norm, g_fb, g_pw, g_ps = [], [], [], []
    parts = {}
    pending = None
    for l in reversed(range(DEPTH)):
        xl, h, pu, pg, q, k, v, ag, lf, c, po, ma, o, lse = saved[l]
        w_i, w_o = weights[l]
        dmp, do, dag, dt, gwo = _outproj_bwd(dx, po, ma, ag, o, w_o, head_sel)
        if pending is None:
            dq, dk, dv, dcs, drs = _attn_bwd(q, k, v, do, c, lse, dt.reshape(N_PAIRS, 2, s))
        else:
            dq, dk, dv, dcs, drs, qi, qo = _attn_bwd(q, k, v, do, c, lse, dt.reshape(N_PAIRS, 2, s), carry=pending)
            parts[l + 1] += (qi, qo)
        drs = jnp.pad(drs[:, 0:2, :].reshape(8, s).T, ((0, 0), (0, 120)))
        dpf, dfb = _forget_bwd(dcs, drs, lf)
        dpu, dpg, gpw, gps = _pool_bwd(pu, pg, dmp, pool_w_b[l], pool_scale[l][None])
        dsec = (dpu, dpg, dq, dk, dv, dag, dpf)
        gwi = _inproj_bwd_w(h, dsec)
        g_fb.append(dfb[0, :8])
        g_pw.append(gpw)
        g_ps.append(gps[0])
        send_i = gwi[None]
        send_o = gwo.reshape(1, N_CHIPS, SHARD_ROWS, D_MODEL)
        if l > 0:
            dx, gn, ri, ro = _inproj_bwd_x(dsec, w_i, xl, norm_g[l][None], dx, exchange=(send_i, send_o))
            g_norm.append(gn[0])
        else:
            g_norm.append(jnp.zeros((D_MODEL,), F32))
            g_norm, g_fb, g_pw, g_ps = [t[::-1] for t in (g_norm, g_fb, g_pw, g_ps)]
            send_s = _pack_small(jnp.stack(g_norm), g_final[0], jnp.stack(g_ps), jnp.stack(g_fb), jnp.stack(g_pw),
                                 loss=loss_part[0, 0])
            ri, ro, rs = _exchange_sibling(send_i, send_o, send_s)
        pending = (_chip_sum(place, send_i, ri, "chip_sum_w_in", 256),
                   _chip_sum(place, send_o, ro, "chip_sum_w_out", SHARD_ROWS // 2))
        parts[l] = (send_i, ri, send_o, ro)
    ps_ = _add2(send_s, rs, "chip_sum_small", SMALL_ROWS // 4)
    dx, gn0, qi, qo, qs = _inproj_bwd_x(dsec, w_i, xl, norm_g[0][None], dx, scatter=(*pending, ps_))
    parts[0] += (qi, qo)
    fi = lax.empty((DEPTH, D_MODEL, SHARD_COLS), F32)
    fo = lax.empty((DEPTH, SHARD_ROWS, D_MODEL), F32)
    for l in range(DEPTH):
        send_i, ri, send_o, ro, qi, qo = parts[l]
        fi = _mesh_sum(place, send_i, ri, qi, fi, l, "mesh_sum_w_in", 256)
        fo = _mesh_sum(place, send_o, ro, qo, fo, l, "mesh_sum_w_out", SHARD_ROWS // 2)
    hs = _mesh_sum_small(place, ps_, qs, SMALL_HALF // 4)
    fi, fo, fs, gn0 = _join_sibling(fi, fo, hs, gn0.reshape(8, 128))
    fs = jnp.concatenate([fs[0:8] + gn0, fs[8:]], axis=0)

    col_major = lambda a: jnp.transpose(a, (2, 0, 1))
    flat_o = lambda a: a.reshape(-1, D_MODEL)
    out_i = _adamw(col_major(fi), col_major(w_in), col_major(m_w_in), col_major(v_w_in), "adamw_w_in", 77)
    out_i = [jnp.transpose(t, (1, 2, 0)) for t in out_i]
    out_o = _adamw(flat_o(fo), flat_o(w_out), flat_o(m_w_out), flat_o(v_w_out), "adamw_w_out", 256)
    out_s = _adamw(fs, _pack_small(norm_g, final_g, pool_scale, forget_bias, pool_w),
                   _pack_small(m_norm_g, m_final_g, m_pool_scale, m_forget_bias, m_pool_w),
                   _pack_small(v_norm_g, v_final_g, v_pool_scale, v_forget_bias, v_pool_w),
                   "adamw_small", SMALL_ROWS // 4)

    loss = out_s[0].reshape(-1)[LOSS_SLOT]
    groups = []
    for oi, oo, os_ in zip(out_i, out_o, out_s):
        sn, sf, sp, sb, sw = _unpack_small(os_)
        groups.append((sn, oi.reshape(w_in.shape), sb, sw, sp, oo.reshape(w_out.shape), sf))
    return (loss, dx[None]) + tuple(t for grp in groups for t in grp)
```

```python
import jax
import jax.numpy as jnp
from jax import lax
from jax.experimental import pallas as pl
from jax.experimental.pallas import tpu as pltpu

F32 = jnp.float32
BF16 = jnp.bfloat16

D_MODEL = 1024
DEPTH = 4
POOL_WIDTH = 512
ATTN_WIDTH = 512
HEAD_DIM = 64
PAIR = 2 * HEAD_DIM
N_PAIRS = ATTN_WIDTH // PAIR
POOL_WINDOWS = (2, 4, 8, 16)
GROUP_DIM = 128
HALO = 16
IN_COLS = 3080
OFF_F = 3072
IN_COLS_PAD = 3200
SECTIONS = ((0, 512), (512, 512), (1024, 512), (1536, 512), (2048, 512), (2560, 512), (OFF_F, 128))
N_CHIPS = 4
SHARD_COLS = IN_COLS // N_CHIPS
SHARD_ROWS = D_MODEL // N_CHIPS
RMS_EPS = 1e-6
NEG = -1e30
Q_SCALE = 0.125

ADAM_LR = 0.001
ADAM_B1 = 0.9
ADAM_B2 = 0.999
ADAM_EPS = 1e-08
ADAM_WD = 0.01
ADAM_STEP = 10

SMALL_ROWS = 2112
SMALL_HALF = SMALL_ROWS // 2

NT = (((1,), (1,)), ((), ()))
TN = (((0,), (0,)), ((), ()))
MESH = pl.DeviceIdType.MESH
ANY = pl.BlockSpec(memory_space=pl.ANY)


def _params(semantics, vmem_mb=48):
    return pltpu.CompilerParams(dimension_semantics=semantics, vmem_limit_bytes=vmem_mb << 20)


def _row_spec(tm, cols):
    return pl.BlockSpec((tm, cols), lambda i: (i, 0))


def _full_spec(shape):
    return pl.BlockSpec(shape, lambda *_: (0,) * len(shape))


def _sigmoid(x):
    return jax.nn.sigmoid(x)


def _scan_rows(a, reverse=False):
    n = a.shape[0]
    row = lax.broadcasted_iota(jnp.int32, a.shape, 0)
    k = 1
    while k < n:
        if reverse:
            a = a + jnp.where(row < n - k, pltpu.roll(a, n - k, 0), 0.0)
        else:
            a = a + jnp.where(row >= k, pltpu.roll(a, k, 0), 0.0)
        k *= 2
    return a


def _inproj_fwd(place, x, g, own, got, fb, tm=512):
    s = x.shape[0]

    def body(place_ref, x_ref, g_ref, fb_ref, *refs):
        own_ref = refs[0] if own is not None else None
        (got_ref, pu_ref, pg_ref, q_ref, k_ref, v_ref, ag_ref,
         lf_ref, c_ref, h_ref, wt_ref, carry_ref, w_ref, piece_ref, sem) = refs[own is not None:]

        @pl.when(pl.program_id(0) == 0)
        def _():
            carry_ref[...] = jnp.zeros_like(carry_ref)
            for chip in range(N_CHIPS):
                rel = jnp.bitwise_xor(chip, place_ref[0])

                def fetch(src):
                    cp = pltpu.make_async_copy(src, piece_ref, sem)
                    cp.start()
                    cp.wait()

                if own_ref is None:
                    fetch(got_ref.at[0, rel])
                else:
                    pl.when(rel == 0)(lambda: fetch(own_ref))
                    pl.when(rel != 0)(lambda: fetch(got_ref.at[0, rel]))
                w_ref[:, chip * SHARD_COLS:(chip + 1) * SHARD_COLS] = piece_ref[...]
            w_ref[:, IN_COLS:] = jnp.zeros((D_MODEL, IN_COLS_PAD - IN_COLS), BF16)
            for t in range(IN_COLS_PAD // 128):
                cols = slice(t * 128, (t + 1) * 128)
                wt_ref[cols, :] = jnp.transpose(w_ref[:, cols].astype(F32)).astype(BF16)

        xf = x_ref[...]
        r = lax.rsqrt(jnp.mean(xf * xf, axis=-1, keepdims=True) + RMS_EPS)
        h = (xf * r * g_ref[...]).astype(BF16)
        h_ref[...] = h

        def proj(sec):
            off, n = SECTIONS[sec]
            return jnp.dot(h, w_ref[:, off:off + n], preferred_element_type=F32)

        pu_ref[...] = proj(0).astype(BF16)
        pg_ref[...] = proj(1).astype(BF16)
        q_ref[...] = (proj(2) * Q_SCALE).astype(BF16)
        k_ref[...] = proj(3).astype(BF16)
        v_ref[...] = proj(4).astype(BF16)
        ag_ref[...] = proj(5).astype(BF16)
        z = proj(6) + fb_ref[...]
        lf = jnp.minimum(z, 0.0) - jnp.log(1.0 + jnp.exp(-jnp.abs(z)))
        lf_ref[...] = lf
        c_ref[...] = _scan_rows(lf) + carry_ref[0:1, :]
        carry_ref[0:1, :] = c_ref[tm - 1:tm, :]

    act = jax.ShapeDtypeStruct((s, 512), BF16)
    rows = lambda cols: pl.BlockSpec((tm, cols), lambda i, p: (i, 0))
    whole = lambda shape: pl.BlockSpec(shape, lambda i, p: (0,) * len(shape))
    shards = ([] if own is None else [own]) + [got]
    return pl.pallas_call(
        body, name="inproj_fwd" if own is None else "inproj_fwd_first",
        grid_spec=pltpu.PrefetchScalarGridSpec(
            num_scalar_prefetch=1, grid=(s // tm,),
            in_specs=[rows(D_MODEL), whole((1, D_MODEL)), whole((1, 128))] + [ANY] * len(shards),
            out_specs=[rows(512)] * 6 + [rows(128), rows(128), rows(D_MODEL), whole((IN_COLS_PAD, D_MODEL))],
            scratch_shapes=[pltpu.VMEM((8, 128), F32), pltpu.VMEM((D_MODEL, IN_COLS_PAD), BF16),
                            pltpu.VMEM((D_MODEL, SHARD_COLS), BF16), pltpu.SemaphoreType.DMA]),
        out_shape=[act] * 6 + [jax.ShapeDtypeStruct((s, 128), F32), jax.ShapeDtypeStruct((s, 128), F32),
                               jax.ShapeDtypeStruct((s, D_MODEL), BF16),
                               jax.ShapeDtypeStruct((IN_COLS_PAD, D_MODEL), BF16)],
        compiler_params=_params(("arbitrary",), vmem_mb=56),
    )(place, x, g, fb, *shards)


def _window_sums(ext, forward):
    n = ext.shape[0]
    outs = []
    for gi, w in enumerate(POOL_WINDOWS):
        a = ext[:, gi * GROUP_DIM:(gi + 1) * GROUP_DIM]
        k = 1
        while k < w:
            a = a + pltpu.roll(a, (n - k) if forward else k, 0)
            k *= 2
        outs.append(a)
    return outs


def _over_count(a, row0, gi):
    w = POOL_WINDOWS[gi]
    t = row0 + lax.broadcasted_iota(jnp.int32, (HALO, GROUP_DIM), 0)
    head = a[0:HALO, :] / jnp.minimum(t + 1, w).astype(F32)
    return jnp.concatenate([head, a[HALO:, :] * (1.0 / w)], axis=0)


def _pool_delta(u, halo, row0):
    sums = _window_sums(jnp.concatenate([halo, u], axis=0), forward=False)
    return [_over_count(sums[gi][HALO:, :], row0, gi) - u[:, gi * GROUP_DIM:(gi + 1) * GROUP_DIM]
            for gi in range(len(POOL_WINDOWS))]


def _pool_fwd(pu, pg, pw, ps, tm=512):
    s = pu.shape[0]
    hb = tm // HALO

    def body(pu_ref, halo_ref, pg_ref, pw_ref, ps_ref, po_ref):
        i = pl.program_id(0)
        u = pu_ref[...].astype(F32)
        halo = jnp.where(i == 0, 0.0, halo_ref[...].astype(F32))
        d = _pool_delta(u, halo, i * tm)
        for gi in range(len(POOL_WINDOWS)):
            cols = slice(gi * GROUP_DIM, (gi + 1) * GROUP_DIM)
            z = jnp.dot(d[gi].astype(BF16), pw_ref[gi], preferred_element_type=F32)
            gate = pg_ref[:, cols].astype(F32)
            po_ref[:, cols] = (z * ps_ref[:, cols] * (gate * _sigmoid(gate))).astype(BF16)

    return pl.pallas_call(
        body, name="pool_fwd", grid=(s // tm,),
        in_specs=[_row_spec(tm, 512),
                  pl.BlockSpec((HALO, 512), lambda i: (jnp.maximum(i * hb - 1, 0), 0)),
                  _row_spec(tm, 512), _full_spec((4, GROUP_DIM, GROUP_DIM)), _full_spec((1, 512))],
        out_specs=_row_spec(tm, 512),
        out_shape=jax.ShapeDtypeStruct((s, 512), BF16),
        compiler_params=_params(("parallel",)),
    )(pu, pu, pg, pw, ps)


def _split3(x):
    hi = x.astype(BF16).astype(F32)
    r1 = x - hi
    lo = r1.astype(BF16).astype(F32)
    return hi, lo, (r1 - lo).astype(BF16).astype(F32)


def _own(idx, h):
    return (idx < HEAD_DIM) if h == 0 else (idx >= HEAD_DIM)


def _spare(h):
    return HEAD_DIM if h == 0 else 0


def _attn_fwd(q, k, v, c, ag, carry=None, tq=512, tk=512, chunk=512):
    s = q.shape[0]
    n_carry = 0 if carry is None else len(carry)

    def body(*refs):
        q_ref, k_ref, v_ref, c_ref, ag_ref = refs[:5]
        carry_src = refs[5:5 + n_carry]
        ma_ref, o_ref, lse_ref = refs[5 + n_carry:8 + n_carry]
        carry_dst = refs[8 + n_carry:8 + 2 * n_carry]
        kx_ref, vt_ref, acc0_ref, acc1_ref, sta_ref, stb_ref = refs[8 + 2 * n_carry:14 + 2 * n_carry]
        carry_sems = refs[14 + 2 * n_carry:]
        j = pl.program_id(0)
        i = pl.program_id(1)

        if n_carry:
            @pl.when((j == 0) & (i == 0))
            def _():
                _gather_start(carry_src, carry_dst, carry_sems[:2])
                for cp in _gather_own(carry_src, carry_dst, carry_sems[2]):
                    cp.start()

        @pl.when(i == 0)
        def _():
            def prep(ch, carry):
                rows = pl.ds(pl.multiple_of(ch * chunk, chunk), chunk)
                kk = k_ref[rows, :].astype(F32)
                cc = c_ref[rows, :]
                vt = jnp.transpose(v_ref[rows, :].astype(F32))
                lane = lax.broadcasted_iota(jnp.int32, (chunk, PAIR), 1)
                sub = lax.broadcasted_iota(jnp.int32, (PAIR, chunk), 0)
                for h in range(2):
                    ccol = jnp.sum(jnp.where(lane == 2 * j + h, cc, 0.0), axis=1, keepdims=True)
                    kx = jnp.where(_own(lane, h), kk, 0.0)
                    for t, part in enumerate(_split3(-ccol)):
                        kx = jnp.where(lane == _spare(h) + t, part, kx)
                    kx_ref[h, rows, :] = kx.astype(BF16)
                    vt_ref[h, :, rows] = jnp.where(_own(sub, h), vt,
                                                   jnp.where(sub == _spare(h), 1.0, 0.0)).astype(BF16)
                return carry

            lax.fori_loop(0, s // chunk, prep, 0)

        qq = q_ref[...].astype(F32)
        lane_q = lax.broadcasted_iota(jnp.int32, (tq, PAIR), 1)
        qx = []
        for h in range(2):
            ones = (lane_q >= _spare(h)) & (lane_q < _spare(h) + 3)
            qx.append(jnp.transpose(jnp.where(_own(lane_q, h), qq, jnp.where(ones, 1.0, 0.0))).astype(BF16))
        accs = (acc0_ref, acc1_ref)
        for acc in accs:
            acc[...] = jnp.zeros_like(acc)

        def rows_of(kb):
            return pl.ds(pl.multiple_of(kb * tk, tk), tk)

        def scores(kb, dst):
            for h in range(2):
                dst[h] = jnp.dot(kx_ref[h, rows_of(kb), :], qx[h], preferred_element_type=F32)

        def consume(kb, src, m, masked):
            m_out, alpha, pv = [], [], []
            for h in range(2):
                sh = src[h]
                if masked:
                    kpos = kb * tk + lax.broadcasted_iota(jnp.int32, (tk, tq), 0)
                    qpos = i * tq + lax.broadcasted_iota(jnp.int32, (tk, tq), 1)
                    sh = jnp.where(qpos >= kpos, sh, NEG)
                m_new = jnp.maximum(m[h], jnp.max(sh, axis=0, keepdims=True))
                pt = jnp.exp(sh - m_new).astype(BF16)
                alpha.append(jnp.exp(m[h] - m_new))
                pv.append(jnp.dot(vt_ref[h, :, rows_of(kb)], pt, preferred_element_type=F32))
                m_out.append(m_new)
            for h in range(2):
                accs[h][...] = accs[h][...] * alpha[h] + pv[h]
            return tuple(m_out)

        n_full = (i * tq) // tk

        def two_blocks(t, m):
            kb = 2 * t
            scores(kb + 1, stb_ref)
            m = consume(kb, sta_ref, m, False)
            scores(kb + 2, sta_ref)
            return consume(kb + 1, stb_ref, m, False)

        def odd_tail(m):
            scores(n_full, stb_ref)
            m = consume(n_full - 1, sta_ref, m, False)
            return consume(n_full, stb_ref, m, True)

        def even_tail(m):
            return consume(n_full, sta_ref, m, True)

        scores(0, sta_ref)
        m0 = jnp.full((1, tq), NEG, F32)
        m = lax.fori_loop(0, n_full // 2, two_blocks, (m0, m0))
        m = lax.cond(lax.rem(n_full, 2) == 1, odd_tail, even_tail, m)

        sub_o = lax.broadcasted_iota(jnp.int32, (PAIR, tq), 0)
        den = [accs[h][_spare(h):_spare(h) + 1, :] for h in range(2)]
        ot = jnp.where(sub_o < HEAD_DIM, acc0_ref[...] / den[0], acc1_ref[...] / den[1])
        o = jnp.transpose(ot)
        o_ref[...] = o.astype(BF16)
        gate = ag_ref[...].astype(F32)
        ma_ref[...] = (o * (gate * _sigmoid(gate))).astype(BF16)
        sub8 = lax.broadcasted_iota(jnp.int32, (8, tq), 0)
        lse_ref[...] = jnp.where(sub8 == 0, m[0] + jnp.log(den[0]), m[1] + jnp.log(den[1]))

        if n_carry:
            @pl.when((j == N_PAIRS - 1) & (i == s // tq - 1))
            def _():
                _gather_finish(carry_src, carry_dst, carry_sems[:2])
                for cp in _gather_own(carry_src, carry_dst, carry_sems[2]):
                    cp.wait()

    carried = [] if carry is None else list(carry)
    carry_sem = ([pltpu.SemaphoreType.DMA((GATHER_SEMS,))] * 2 + [pltpu.SemaphoreType.DMA((2,))]) if n_carry else []
    return pl.pallas_call(
        body, name="attn_fwd_gather" if n_carry else "attn_fwd", grid=(N_PAIRS, s // tq),
        in_specs=[pl.BlockSpec((tq, PAIR), lambda j, i: (i, j)),
                  pl.BlockSpec((s, PAIR), lambda j, i: (0, j)),
                  pl.BlockSpec((s, PAIR), lambda j, i: (0, j)),
                  pl.BlockSpec((s, 128), lambda j, i: (0, 0)),
                  pl.BlockSpec((tq, PAIR), lambda j, i: (i, j))] + [ANY] * n_carry,
        out_specs=[pl.BlockSpec((tq, PAIR), lambda j, i: (i, j)),
                   pl.BlockSpec((tq, PAIR), lambda j, i: (i, j)),
                   pl.BlockSpec((None, 8, tq), lambda j, i: (j, 0, i))] + [ANY] * n_carry,
        out_shape=[jax.ShapeDtypeStruct((s, ATTN_WIDTH), BF16), jax.ShapeDtypeStruct((s, ATTN_WIDTH), BF16),
                   jax.ShapeDtypeStruct((N_PAIRS, 8, s), F32)] + (_gather_shapes(*carried) if n_carry else []),
        scratch_shapes=[pltpu.VMEM((2, s, PAIR), BF16), pltpu.VMEM((2, PAIR, s), BF16),
                        pltpu.VMEM((PAIR, tq), F32), pltpu.VMEM((PAIR, tq), F32),
                        pltpu.VMEM((2, tk, tq), F32), pltpu.VMEM((2, tk, tq), F32)] + carry_sem,
        compiler_params=_params(("arbitrary", "arbitrary")),
    )(q, k, v, c, ag, *carried)


def _outproj_fwd(x, po, ma, wo, tm=512):
    s = x.shape[0]

    def body(x_ref, po_ref, ma_ref, wo_ref, xn_ref):
        xn_ref[...] = (x_ref[...]
                       + jnp.dot(po_ref[...], wo_ref[0:POOL_WIDTH, :], preferred_element_type=F32)
                       + jnp.dot(ma_ref[...], wo_ref[POOL_WIDTH:, :], preferred_element_type=F32))

    return pl.pallas_call(
        body, name="outproj_fwd", grid=(s // tm,),
        in_specs=[_row_spec(tm, D_MODEL), _row_spec(tm, 512), _row_spec(tm, 512),
                  _full_spec((D_MODEL, D_MODEL))],
        out_specs=_row_spec(tm, D_MODEL),
        out_shape=jax.ShapeDtypeStruct((s, D_MODEL), F32),
        compiler_params=_params(("parallel",)),
    )(x, po, ma, wo)


def _loss_head(x, po, ma, wo, tgt, g, tm=512):
    s = x.shape[0]

    def body(x_ref, po_ref, ma_ref, wo_ref, t_ref, g_ref, dx_ref, dg_ref, loss_ref):
        @pl.when(pl.program_id(0) == 0)
        def _():
            dg_ref[...] = jnp.zeros_like(dg_ref)
            loss_ref[...] = jnp.zeros_like(loss_ref)

        xf = (x_ref[...]
              + jnp.dot(po_ref[...], wo_ref[0:POOL_WIDTH, :], preferred_element_type=F32)
              + jnp.dot(ma_ref[...], wo_ref[POOL_WIDTH:, :], preferred_element_type=F32))
        r = lax.rsqrt(jnp.mean(xf * xf, axis=-1, keepdims=True) + RMS_EPS)
        xh = xf * r
        gg = g_ref[...]
        e = xh * gg - t_ref[...]
        loss_ref[...] += 0.5 * jnp.sum(jnp.mean(e * e, axis=-1, keepdims=True))
        dy = e * (1.0 / D_MODEL)
        u = dy * gg
        dx_ref[...] = r * (u - xh * jnp.mean(xh * u, axis=-1, keepdims=True))
        dg_ref[...] += jnp.sum(dy * xh, axis=0, keepdims=True)

    return pl.pallas_call(
        body, name="loss_head", grid=(s // tm,),
        in_specs=[_row_spec(tm, D_MODEL), _row_spec(tm, 512), _row_spec(tm, 512), _full_spec((D_MODEL, D_MODEL)),
                  _row_spec(tm, D_MODEL), _full_spec((1, D_MODEL))],
        out_specs=[_row_spec(tm, D_MODEL), _full_spec((1, D_MODEL)), _full_spec((8, 128))],
        out_shape=[jax.ShapeDtypeStruct((s, D_MODEL), F32), jax.ShapeDtypeStruct((1, D_MODEL), F32),
                   jax.ShapeDtypeStruct((8, 128), F32)],
        compiler_params=_params(("arbitrary",)),
    )(x, po, ma, wo, tgt, g)


def _outproj_bwd(dx, po, ma, ag, o, wo, head_sel, tm=512):
    s = dx.shape[0]

    def body(dx_ref, po_ref, ma_ref, ag_ref, o_ref, wo_ref, sel_ref, dmp_ref, do_ref, dag_ref, dt_ref, gwo_ref):
        @pl.when(pl.program_id(0) == 0)
        def _():
            gwo_ref[...] = jnp.zeros_like(gwo_ref)

        dxb = dx_ref[...].astype(BF16)
        dm = lax.dot_general(dxb, wo_ref[...], NT, preferred_element_type=F32)
        dmp_ref[...] = dm[:, 0:POOL_WIDTH].astype(BF16)
        dma = dm[:, POOL_WIDTH:]
        gate = ag_ref[...].astype(F32)
        of = o_ref[...].astype(F32)
        sg = _sigmoid(gate)
        do = dma * (gate * sg)
        do_ref[...] = do.astype(BF16)
        dag_ref[...] = (dma * of * (sg * (1.0 + gate * (1.0 - sg)))).astype(BF16)
        prod = do * of
        hi = prod.astype(BF16)
        lo = (prod - hi.astype(F32)).astype(BF16)
        dt_ref[...] = (lax.dot_general(sel_ref[...], hi, NT, preferred_element_type=F32)
                       + lax.dot_general(sel_ref[...], lo, NT, preferred_element_type=F32))
        gwo_ref[0:POOL_WIDTH, :] += lax.dot_general(po_ref[...], dxb, TN, preferred_element_type=F32)
        gwo_ref[POOL_WIDTH:, :] += lax.dot_general(ma_ref[...], dxb, TN, preferred_element_type=F32)

    act = jax.ShapeDtypeStruct((s, 512), BF16)
    return pl.pallas_call(
        body, name="outproj_bwd", grid=(s // tm,),
        in_specs=[_row_spec(tm, D_MODEL)] + [_row_spec(tm, 512)] * 4
                 + [_full_spec((D_MODEL, D_MODEL)), _full_spec((8, ATTN_WIDTH))],
        out_specs=[_row_spec(tm, 512)] * 3 + [pl.BlockSpec((8, tm), lambda i: (0, i)),
                                              _full_spec((D_MODEL, D_MODEL))],
        out_shape=[act, act, act, jax.ShapeDtypeStruct((8, s), F32),
                   jax.ShapeDtypeStruct((D_MODEL, D_MODEL), F32)],
        compiler_params=_params(("arbitrary",)),
    )(dx, po, ma, ag, o, wo, head_sel)


def _attn_bwd(q, k, v, do, c, lse, dt, carry=None, tq=256, chunk=512):
    s = q.shape[0]
    tk = 2 * tq
    nq = s // tq
    n_carry = 0 if carry is None else len(carry)

    def body(*refs):
        q_ref, k_ref, v_ref, do_ref, c_ref, lse_ref, dt_ref = refs[:7]
        carry_src = refs[7:7 + n_carry]
        dq_ref, dk_ref, dv_ref, dcs_ref, drs_ref = refs[7 + n_carry:12 + n_carry]
        carry_dst = refs[12 + n_carry:12 + 2 * n_carry]
        (qxt_ref, doxt_ref, qm_ref, dox_ref, dqt0_ref, dqt1_ref, dk0_ref, dk1_ref, dvx_ref,
         kx_ref, vx_ref, ktm_ref, sa_ref, sb_ref) = refs[12 + 2 * n_carry:26 + 2 * n_carry]
        carry_sems = refs[26 + 2 * n_carry:]
        carry_big = tuple(zip(carry_src, carry_dst))
        j = pl.program_id(0)
        b = pl.program_id(1)
        dqts = (dqt0_ref, dqt1_ref)
        dks = (dk0_ref, dk1_ref)

        if n_carry:
            @pl.when((j == 0) & (b == 0))
            def _():
                _scatter_start(carry_big, None, carry_sems)

        @pl.when(b == 0)
        def _():
            def prep(ch, carry):
                rows = pl.ds(pl.multiple_of(ch * chunk, chunk), chunk)
                qq = q_ref[rows, :].astype(F32)
                dd = do_ref[rows, :].astype(F32)
                qt = jnp.transpose(qq)
                ddt = jnp.transpose(dd)
                lane = lax.broadcasted_iota(jnp.int32, (chunk, PAIR), 1)
                sub = lax.broadcasted_iota(jnp.int32, (PAIR, chunk), 0)
                for h in range(2):
                    sp = _spare(h)
                    qm_ref[h, rows, :] = jnp.where(_own(lane, h), qq, jnp.where(lane == sp, 1.0, 0.0)).astype(BF16)
                    dox_ref[h, rows, :] = jnp.where(_own(lane, h), dd, 0.0).astype(BF16)
                    qx = jnp.where(_own(sub, h), qt, jnp.where((sub >= sp) & (sub < sp + 3), 1.0, 0.0))
                    for t, part in enumerate(_split3(-lse_ref[pl.ds(h, 1), rows])):
                        qx = jnp.where(sub == sp + 3 + t, part, qx)
                    qxt_ref[h, :, rows] = qx.astype(BF16)
                    dx = jnp.where(_own(sub, h), ddt, 0.0)
                    for t, part in enumerate(_split3(-dt_ref[pl.ds(h, 1), rows])):
                        dx = jnp.where(sub == sp + t, part, dx)
                    doxt_ref[h, :, rows] = dx.astype(BF16)
                return carry

            lax.fori_loop(0, s // chunk, prep, 0)
            for ref in dqts:
                ref[...] = jnp.zeros_like(ref)

        kk = k_ref[...].astype(F32)
        vv = v_ref[...].astype(F32)
        cc = c_ref[...]
        kt = jnp.transpose(kk)
        lane = lax.broadcasted_iota(jnp.int32, (tk, PAIR), 1)
        sub = lax.broadcasted_iota(jnp.int32, (PAIR, tk), 0)
        for h in range(2):
            sp = _spare(h)
            ccol = jnp.sum(jnp.where(lane == 2 * j + h, cc, 0.0), axis=1, keepdims=True)
            kx = jnp.where(_own(lane, h), kk, jnp.where((lane >= sp + 3) & (lane < sp + 6), 1.0, 0.0))
            for t, part in enumerate(_split3(-ccol)):
                kx = jnp.where(lane == sp + t, part, kx)
            kx_ref[h] = kx.astype(BF16)
            vx_ref[h] = jnp.where(_own(lane, h), vv, jnp.where((lane >= sp) & (lane < sp + 3), 1.0, 0.0)).astype(BF16)
            ktm_ref[h] = jnp.where(_own(sub, h), kt, jnp.where(sub == sp, 1.0, 0.0)).astype(BF16)
        for ref in dks:
            ref[...] = jnp.zeros_like(ref)
        dvx_ref[...] = jnp.zeros_like(dvx_ref)

        def cols_of(i):
            return pl.ds(pl.multiple_of(i * tq, tq), tq)

        def scores(i, dst):
            cols = cols_of(jnp.minimum(i, nq - 1))
            for h in range(2):
                dst[h] = jnp.dot(kx_ref[h], qxt_ref[h, :, cols], preferred_element_type=F32)
                dst[2 + h] = jnp.dot(vx_ref[h], doxt_ref[h, :, cols], preferred_element_type=F32)

        def consume(i, src, masked):
            cols = cols_of(i)
            for h in range(2):
                arg = src[h]
                if masked:
                    kpos = b * tk + lax.broadcasted_iota(jnp.int32, (tk, tq), 0)
                    qpos = i * tq + lax.broadcasted_iota(jnp.int32, (tk, tq), 1)
                    arg = jnp.where(qpos >= kpos, arg, NEG)
                pt = jnp.exp(arg)
                dst = (pt * src[2 + h]).astype(BF16)
                dvx_ref[...] += jnp.dot(pt.astype(BF16), dox_ref[h, cols, :], preferred_element_type=F32)
                dks[h][...] += jnp.dot(dst, qm_ref[h, cols, :], preferred_element_type=F32)
                dqts[h][:, cols] += jnp.dot(ktm_ref[h], dst, preferred_element_type=F32)

        i0 = 2 * b

        def two_blocks(t, masked):
            i = i0 + 2 * t
            scores(i + 1, sb_ref)
            consume(i, sa_ref, masked)
            scores(i + 2, sa_ref)
            consume(i + 1, sb_ref, masked)

        def loop_body(t, carry):
            two_blocks(t, False)
            return carry

        scores(i0, sa_ref)
        two_blocks(0, True)
        lax.fori_loop(1, (nq - i0) // 2, loop_body, 0)

        dv_ref[...] = dvx_ref[...].astype(BF16)
        dk_ref[...] = jnp.where(lane < HEAD_DIM, dk0_ref[...], dk1_ref[...]).astype(BF16)
        dcs_ref[...] = jnp.where(lane == _spare(0), dk0_ref[...], jnp.where(lane == _spare(1), dk1_ref[...], 0.0))

        @pl.when(b == s // tk - 1)
        def _():
            sub8 = lax.broadcasted_iota(jnp.int32, (8, s), 0)
            drs_ref[...] = jnp.where(sub8 == 0, dqt0_ref[_spare(0):_spare(0) + 1, :],
                                     jnp.where(sub8 == 1, dqt1_ref[_spare(1):_spare(1) + 1, :], 0.0))
            sub_c = lax.broadcasted_iota(jnp.int32, (PAIR, chunk), 0)
            for ch in range(s // chunk):
                rows = pl.ds(ch * chunk, chunk)
                both = jnp.where(sub_c < HEAD_DIM, dqt0_ref[:, rows], dqt1_ref[:, rows])
                dq_ref[rows, :] = (jnp.transpose(both) * Q_SCALE).astype(BF16)

        if n_carry:
            @pl.when((j == N_PAIRS - 1) & (b == s // tk - 1))
            def _():
                _scatter_finish(carry_big, None, carry_sems)

    act = jax.ShapeDtypeStruct((s, ATTN_WIDTH), BF16)
    pair_rows = pl.BlockSpec((s, PAIR), lambda j, b: (0, j))
    pair_blk = pl.BlockSpec((tk, PAIR), lambda j, b: (b, j))
    stat = pl.BlockSpec((None, 2, s), lambda j, b: (j, 0, 0))
    carried = [] if carry is None else list(carry)
    carry_sem = [pltpu.SemaphoreType.DMA((SCATTER_SEMS,))] * 2 if n_carry else []
    return pl.pallas_call(
        body, name="attn_bwd_scatter" if n_carry else "attn_bwd", grid=(N_PAIRS, s // tk),
        in_specs=[pair_rows, pair_blk, pair_blk, pair_rows,
                  pl.BlockSpec((tk, 128), lambda j, b: (b, 0)),
                  pl.BlockSpec((None, 8, s), lambda j, b: (j, 0, 0)), stat] + [ANY] * n_carry,
        out_specs=[pair_rows, pair_blk, pair_blk, pair_blk,
                   pl.BlockSpec((None, 8, s), lambda j, b: (j, 0, 0))] + [ANY] * n_carry,
        out_shape=[act, act, act, jax.ShapeDtypeStruct((s, N_PAIRS * 128), F32),
                   jax.ShapeDtypeStruct((N_PAIRS, 8, s), F32)] + (_scatter_shapes(*carried) if n_carry else []),
        scratch_shapes=[pltpu.VMEM((2, PAIR, s), BF16), pltpu.VMEM((2, PAIR, s), BF16),
                        pltpu.VMEM((2, s, PAIR), BF16), pltpu.VMEM((2, s, PAIR), BF16),
                        pltpu.VMEM((PAIR, s), F32), pltpu.VMEM((PAIR, s), F32),
                        pltpu.VMEM((tk, PAIR), F32), pltpu.VMEM((tk, PAIR), F32), pltpu.VMEM((tk, PAIR), F32),
                        pltpu.VMEM((2, tk, PAIR), BF16), pltpu.VMEM((2, tk, PAIR), BF16),
                        pltpu.VMEM((2, PAIR, tk), BF16),
                        pltpu.VMEM((4, tk, tq), F32), pltpu.VMEM((4, tk, tq), F32)] + carry_sem,
        compiler_params=_params(("arbitrary", "arbitrary")),
    )(q, k, v, do, c, lse, dt, *carried)


def _forget_bwd(dcs, drs, lf, tm=512):
    s = lf.shape[0]
    n = s // tm

    def body(dcs_ref, drs_ref, lf_ref, dpf_ref, dfb_ref, carry_ref):
        @pl.when(pl.program_id(0) == 0)
        def _():
            carry_ref[...] = jnp.zeros_like(carry_ref)
            dfb_ref[...] = jnp.zeros_like(dfb_ref)

        lane = lax.broadcasted_iota(jnp.int32, (tm, 128), 1)
        dc = drs_ref[...]
        for pj in range(N_PAIRS):
            blk = dcs_ref[:, pj * 128:(pj + 1) * 128]
            for h in range(2):
                head = 2 * pj + h
                moved = pltpu.roll(blk, (head - _spare(h)) % 128, 1) if head != _spare(h) else blk
                dc = dc - jnp.where(lane == head, moved, 0.0)
        dlf = _scan_rows(dc, reverse=True) + carry_ref[0:1, :]
        carry_ref[...] = dlf[0:8, :]
        dz = jnp.where(lane < 8, dlf * (1.0 - jnp.exp(lf_ref[...])), 0.0)
        dpf_ref[...] = dz.astype(BF16)
        dfb_ref[...] += jnp.sum(dz, axis=0, keepdims=True)

    return pl.pallas_call(
        body, name="forget_bwd", grid=(n,),
        in_specs=[pl.BlockSpec((tm, N_PAIRS * 128), lambda i: (n - 1 - i, 0)),
                  pl.BlockSpec((tm, 128), lambda i: (n - 1 - i, 0)),
                  pl.BlockSpec((tm, 128), lambda i: (n - 1 - i, 0))],
        out_specs=[pl.BlockSpec((tm, 128), lambda i: (n - 1 - i, 0)), _full_spec((1, 128))],
        out_shape=[jax.ShapeDtypeStruct((s, 128), BF16), jax.ShapeDtypeStruct((1, 128), F32)],
        scratch_shapes=[pltpu.VMEM((8, 128), F32)],
        compiler_params=_params(("arbitrary",)),
    )(dcs, drs, lf)


def _pool_bwd(pu, pg, dmp, pw, ps, tm=512):
    s = pu.shape[0]
    hb = tm // HALO
    n = s // tm
    last_halo = s // HALO - 1

    def body(pu_ref, halo_ref, pg_ref, dmp_ref, pgn_ref, dmpn_ref, pw_ref, ps_ref,
             dpu_ref, dpg_ref, gpw_ref, gps_ref):
        i = pl.program_id(0)

        @pl.when(i == 0)
        def _():
            gpw_ref[...] = jnp.zeros_like(gpw_ref)
            gps_ref[...] = jnp.zeros_like(gps_ref)

        u = pu_ref[...].astype(F32)
        halo = jnp.where(i == 0, 0.0, halo_ref[...].astype(F32))
        d = _pool_delta(u, halo, i * tm)
        e_parts, dd_parts = [], []
        for gi in range(len(POOL_WINDOWS)):
            cols = slice(gi * GROUP_DIM, (gi + 1) * GROUP_DIM)
            wg = pw_ref[gi]
            scale = ps_ref[:, cols]
            db = d[gi].astype(BF16)
            z = jnp.dot(db, wg, preferred_element_type=F32)
            gate = pg_ref[:, cols].astype(F32)
            sg = _sigmoid(gate)
            dm = dmp_ref[:, cols].astype(F32)
            dy = dm * (gate * sg)
            dpg_ref[:, cols] = (dm * (z * scale) * (sg * (1.0 + gate * (1.0 - sg)))).astype(BF16)
            gps_ref[:, cols] += jnp.sum(dy * z, axis=0, keepdims=True)
            dz = (dy * scale).astype(BF16)
            gpw_ref[gi] += lax.dot_general(db, dz, TN, preferred_element_type=F32)
            dd = lax.dot_general(dz, wg, NT, preferred_element_type=F32)
            gate_n = pgn_ref[:, cols].astype(F32)
            dz_n = (dmpn_ref[:, cols].astype(F32) * (gate_n * _sigmoid(gate_n)) * scale).astype(BF16)
            dd_n = lax.dot_general(dz_n, wg, NT, preferred_element_type=F32)
            dd_n = jnp.where(i == n - 1, 0.0, dd_n)
            dd_parts.append(dd)
            e_parts.append(jnp.concatenate([_over_count(dd, i * tm, gi), dd_n * (1.0 / POOL_WINDOWS[gi])], axis=0))
        lead = _window_sums(jnp.concatenate(e_parts, axis=1), forward=True)
        for gi in range(len(POOL_WINDOWS)):
            cols = slice(gi * GROUP_DIM, (gi + 1) * GROUP_DIM)
            dpu_ref[:, cols] = (lead[gi][0:tm, :] - dd_parts[gi]).astype(BF16)

    act = jax.ShapeDtypeStruct((s, 512), BF16)
    prev_halo = pl.BlockSpec((HALO, 512), lambda i: (jnp.maximum(i * hb - 1, 0), 0))
    next_halo = pl.BlockSpec((HALO, 512), lambda i: (jnp.minimum((i + 1) * hb, last_halo), 0))
    return pl.pallas_call(
        body, name="pool_bwd", grid=(n,),
        in_specs=[_row_spec(tm, 512), prev_halo, _row_spec(tm, 512), _row_spec(tm, 512), next_halo, next_halo,
                  _full_spec((4, GROUP_DIM, GROUP_DIM)), _full_spec((1, 512))],
        out_specs=[_row_spec(tm, 512), _row_spec(tm, 512), _full_spec((4, GROUP_DIM, GROUP_DIM)),
                   _full_spec((1, 512))],
        out_shape=[act, act, jax.ShapeDtypeStruct((4, GROUP_DIM, GROUP_DIM), F32),
                   jax.ShapeDtypeStruct((1, 512), F32)],
        compiler_params=_params(("arbitrary",)),
    )(pu, pu, pg, dmp, pg, dmp, pw, ps)


def _inproj_bwd_x(dsec, wt, x, g, dxo, exchange=None, scatter=None, tm=512):
    s = x.shape[0]
    n_steps = s // tm
    carried = list(exchange or scatter or ())
    n_carry = len(carried)

    def body(*refs):
        d_refs = refs[0:7]
        wt_ref, x_ref, g_ref, dxo_ref = refs[7:11]
        carry_src = refs[11:11 + n_carry]
        dxi_ref, dg_ref = refs[11 + n_carry:13 + n_carry]
        carry_dst = refs[13 + n_carry:13 + 2 * n_carry]
        sems = refs[13 + 2 * n_carry:]
        step = pl.program_id(0)

        def comm(phase):
            if exchange is not None:
                (_exchange_start if phase == 0 else _exchange_finish)(carry_src, carry_dst, None, sems)
            else:
                big = tuple(zip(carry_src[:2], carry_dst[:2]))
                (_scatter_start if phase == 0 else _scatter_finish)(big, (carry_src[2], carry_dst[2]), sems)

        @pl.when(step == 0)
        def _():
            dg_ref[...] = jnp.zeros_like(dg_ref)
            if n_carry:
                comm(0)

        dh = None
        for d_ref, (off, ncol) in zip(d_refs, SECTIONS):
            t = jnp.dot(d_ref[...], wt_ref[off:off + ncol, :], preferred_element_type=F32)
            dh = t if dh is None else dh + t
        xf = x_ref[...]
        r = lax.rsqrt(jnp.mean(xf * xf, axis=-1, keepdims=True) + RMS_EPS)
        xh = xf * r
        u = dh * g_ref[...]
        dxi_ref[...] = dxo_ref[...] + r * (u - xh * jnp.mean(xh * u, axis=-1, keepdims=True))
        dg_ref[...] += jnp.sum(dh * xh, axis=0, keepdims=True)

        if n_carry:
            @pl.when(step == n_steps - 1)
            def _():
                comm(1)

    if exchange is not None:
        name, carry_shapes, n_sems = "inproj_bwd_x_exchange", _exchange_shapes(*exchange), 3
    elif scatter is not None:
        name, n_sems = "inproj_bwd_x_scatter", SCATTER_SEMS
        carry_shapes = _scatter_shapes(*scatter[:2]) + [jax.ShapeDtypeStruct((3, SMALL_HALF, 128), F32)]
    else:
        name, carry_shapes, n_sems = "inproj_bwd_x", [], 0
    return pl.pallas_call(
        body, name=name, grid=(n_steps,),
        in_specs=[_row_spec(tm, ncol) for _, ncol in SECTIONS]
                 + [_full_spec((IN_COLS_PAD, D_MODEL)), _row_spec(tm, D_MODEL), _full_spec((1, D_MODEL)),
                    _row_spec(tm, D_MODEL)] + [ANY] * n_carry,
        out_specs=[_row_spec(tm, D_MODEL), _full_spec((1, D_MODEL))] + [ANY] * n_carry,
        out_shape=[jax.ShapeDtypeStruct((s, D_MODEL), F32), jax.ShapeDtypeStruct((1, D_MODEL), F32)] + carry_shapes,
        scratch_shapes=[pltpu.SemaphoreType.DMA((n_sems,))] * 2 if n_carry else [],
        compiler_params=_params(("arbitrary",)),
    )(*dsec, wt, x, g, dxo, *carried)


def _inproj_bwd_w(h, dsec, tm=512):
    s = h.shape[0]
    n_steps = s // tm

    def body(*refs):
        h_ref = refs[0]
        d_refs = refs[1:8]
        out_ref, gw_ref = refs[8:]
        step = pl.program_id(0)

        @pl.when(step == 0)
        def _():
            gw_ref[...] = jnp.zeros_like(gw_ref)

        hh = h_ref[...]
        for d_ref, (off, ncol) in zip(d_refs, SECTIONS):
            gw_ref[:, off:off + ncol] += lax.dot_general(hh, d_ref[...], TN, preferred_element_type=F32)

        @pl.when(step == n_steps - 1)
        def _():
            for k in range(N_CHIPS):
                out_ref[k] = gw_ref[:, k * SHARD_COLS:(k + 1) * SHARD_COLS]

    return pl.pallas_call(
        body, name="inproj_bwd_w", grid=(n_steps,),
        in_specs=[_row_spec(tm, D_MODEL)] + [_row_spec(tm, ncol) for _, ncol in SECTIONS],
        out_specs=_full_spec((N_CHIPS, D_MODEL, SHARD_COLS)),
        out_shape=jax.ShapeDtypeStruct((N_CHIPS, D_MODEL, SHARD_COLS), F32),
        scratch_shapes=[pltpu.VMEM((D_MODEL, IN_COLS_PAD), F32)],
        compiler_params=_params(("arbitrary",), vmem_mb=60),
    )(h, *dsec)


def _elementwise(fn, name, n_out, arrays, tm):
    shape = arrays[0].shape
    rest = shape[1:]
    n_in = len(arrays)

    def body(*refs):
        outs = fn(*[r[...] for r in refs[:n_in]])
        for r, val in zip(refs[n_in:], outs):
            r[...] = val

    spec = pl.BlockSpec((tm,) + rest, lambda i: (i,) + (0,) * len(rest))
    return pl.pallas_call(
        body, name=name, grid=(shape[0] // tm,),
        in_specs=[spec] * n_in, out_specs=[spec] * n_out, out_shape=[jax.ShapeDtypeStruct(shape, F32)] * n_out,
        compiler_params=_params(("parallel",)),
    )(*arrays)


def _add2(a, b, name, tm):
    return _elementwise(lambda p, q: (p + q,), name, 1, [a, b], tm)[0]


def _chip_sum(place, g, r, name, tm):
    n_layers, _, rows, cols = g.shape
    half = rows // 2
    nb = half // tm

    def body(place_ref, g_ref, r_ref, o_ref):
        o_ref[...] = (g_ref[...] + r_ref[...]).astype(BF16)

    blk = (None, None, tm, cols)
    return pl.pallas_call(
        body, name=name,
        grid_spec=pltpu.PrefetchScalarGridSpec(
            num_scalar_prefetch=1, grid=(n_layers, N_CHIPS, nb),
            in_specs=[pl.BlockSpec(blk, lambda l, k, i, p: (l, k, p[1] * nb + i, 0)),
                      pl.BlockSpec(blk, lambda l, k, i, p: (l, k, i, 0))],
            out_specs=pl.BlockSpec(blk, lambda l, k, i, p: (l, k, i, 0))),
        out_shape=jax.ShapeDtypeStruct((n_layers, N_CHIPS, half, cols), BF16),
        compiler_params=_params(("parallel", "parallel", "parallel")),
    )(place, g, r)


def _mesh_sum(place, g, r, got, into, layer, name, tm):
    _, _, rows, cols = g.shape
    half = rows // 2
    nb = half // tm

    def body(place_ref, g_ref, r_ref, got_ref, into_ref, o_ref):
        own = g_ref[...] + r_ref[...]
        o_ref[...] = (own + got_ref[0].astype(F32)) + (got_ref[1].astype(F32) + got_ref[2].astype(F32))

    blk = (None, None, tm, cols)
    return pl.pallas_call(
        body, name=name,
        grid_spec=pltpu.PrefetchScalarGridSpec(
            num_scalar_prefetch=1, grid=(nb,),
            in_specs=[pl.BlockSpec(blk, lambda i, p: (0, p[0], p[1] * nb + i, 0)),
                      pl.BlockSpec(blk, lambda i, p: (0, p[0], i, 0)),
                      pl.BlockSpec((None, 3, tm, cols), lambda i, p: (0, 0, i, 0)),
                      ANY],
            out_specs=pl.BlockSpec((None, tm, cols), lambda i, p: (layer, p[1] * nb + i, 0))),
        out_shape=jax.ShapeDtypeStruct(into.shape, F32),
        input_output_aliases={4: 0},
        compiler_params=_params(("parallel",)),
    )(place, g, r, got, into)


def _mesh_sum_small(place, ps, got, tm):
    nb = SMALL_HALF // tm

    def body(place_ref, ps_ref, got_ref, o_ref):
        o_ref[...] = (ps_ref[...] + got_ref[0]) + (got_ref[1] + got_ref[2])

    return pl.pallas_call(
        body, name="mesh_sum_small",
        grid_spec=pltpu.PrefetchScalarGridSpec(
            num_scalar_prefetch=1, grid=(nb,),
            in_specs=[pl.BlockSpec((tm, 128), lambda i, p: (p[1] * nb + i, 0)),
                      pl.BlockSpec((3, tm, 128), lambda i, p: (0, i, 0))],
            out_specs=pl.BlockSpec((tm, 128), lambda i, p: (p[1] * nb + i, 0))),
        out_shape=jax.ShapeDtypeStruct((SMALL_ROWS, 128), F32),
        compiler_params=_params(("parallel",)),
    )(place, ps, got)


def _adamw(g, w, m, v, name, tm):
    def fn(g, w, m, v):
        m = ADAM_B1 * m + (1.0 - ADAM_B1) * g
        v = ADAM_B2 * v + (1.0 - ADAM_B2) * (g * g)
        m_hat = m / (1.0 - ADAM_B1 ** ADAM_STEP)
        v_hat = v / (1.0 - ADAM_B2 ** ADAM_STEP)
        delta = -ADAM_LR * (m_hat / (jnp.sqrt(v_hat) + ADAM_EPS) + ADAM_WD * w)
        return g, delta, m, v

    return _elementwise(fn, name, 4, [g, w, m, v], tm)


def _position():
    x, y, c = lax.axis_index("x"), lax.axis_index("y"), lax.axis_index("c")
    other_chips = [(x, 1 - y), (1 - x, y), (1 - x, 1 - y)]
    return x, y, c, other_chips


def _remote(src, dst, sems, k, to):
    send_sems, recv_sems = sems
    return pltpu.make_async_remote_copy(src_ref=src, dst_ref=dst, send_sem=send_sems.at[k],
                                        recv_sem=recv_sems.at[k], device_id=to, device_id_type=MESH)


def _comm_call(body, name, n_in, out_shape, n_remote, aliases=None):
    return pl.pallas_call(
        body, name=name, in_specs=[ANY] * n_in, out_specs=[ANY] * len(out_shape), out_shape=out_shape,
        scratch_shapes=[pltpu.SemaphoreType.DMA((n_remote,)), pltpu.SemaphoreType.DMA((n_remote,))],
        input_output_aliases=aliases or {},
    )


def _halves(ref_rows, c):
    half = ref_rows // 2
    return pl.ds(half * c, half), pl.ds(half * (1 - c), half)


def _gather_weights(wi, wo):
    def body(wi_ref, wo_ref, gi_ref, go_ref, send_sems, recv_sems):
        _gather_start((wi_ref, wo_ref), (gi_ref, go_ref), (send_sems, recv_sems))
        _gather_finish((wi_ref, wo_ref), (gi_ref, go_ref), (send_sems, recv_sems))

    return _comm_call(body, "gather_weights", 2, _gather_shapes(wi, wo), GATHER_SEMS)(wi, wo)


GATHER_SEMS = 12
SCATTER_SEMS = 9


def _gather_shapes(wi, wo):
    return [jax.ShapeDtypeStruct((a.shape[0], N_CHIPS) + a.shape[1:], a.dtype) for a in (wi, wo)]


def _gather_sends(srcs, dsts, sems):
    x, y, c, chips = _position()
    out = []
    for k, (cx, cy) in enumerate(chips):
        for a, (src, dst) in enumerate(zip(srcs, dsts)):
            mine, _ = _halves(src.shape[1], c)
            out.append(_remote(src.at[:, mine], dst.at[:, k + 1, mine], sems, len(srcs) * k + a, (cx, cy, c)))
    return out


def _gather_own(srcs, dsts, local_sems):
    return [pltpu.make_async_copy(src, dst.at[:, 0], local_sems.at[a]) for a, (src, dst) in enumerate(zip(srcs, dsts))]


def _gather_start(srcs, dsts, sems):
    for cp in _gather_sends(srcs, dsts, sems):
        cp.start()


def _gather_finish(srcs, dsts, sems):
    x, y, c, _ = _position()
    n = len(srcs)
    forwards = []
    for k in range(3):
        for a, dst in enumerate(dsts):
            mine, _ = _halves(dst.shape[2], c)
            blk = dst.at[:, k + 1, mine]
            _remote(blk, blk, sems, n * k + a, (x, y, c)).wait_recv()
            forwards.append(_remote(blk, blk, sems, 3 * n + n * k + a, (x, y, 1 - c)))
            forwards[-1].start()
    for k in range(3):
        for a, dst in enumerate(dsts):
            _, other = _halves(dst.shape[2], c)
            blk = dst.at[:, k + 1, other]
            _remote(blk, blk, sems, 3 * n + n * k + a, (x, y, c)).wait_recv()
    for cp in _gather_sends(srcs, dsts, sems) + forwards:
        cp.wait_send()


def _scatter_sends(big, small, sems):
    x, y, c, chips = _position()
    n = len(big) + (small is not None)
    out = []
    for k, (cx, cy) in enumerate(chips):
        ck = 2 * cx + cy
        for a, (src, dst) in enumerate(big):
            out.append(_remote(src.at[:, ck], dst.at[:, k], sems, n * k + a, (cx, cy, c)))
        if small is not None:
            half, _ = _halves(SMALL_ROWS, c)
            out.append(_remote(small[0].at[half], small[1].at[k], sems, n * k + n - 1, (cx, cy, c)))
    return out


def _scatter_start(big, small, sems):
    for cp in _scatter_sends(big, small, sems):
        cp.start()


def _scatter_finish(big, small, sems):
    x, y, c, _ = _position()
    n = len(big) + (small is not None)
    for k in range(3):
        landing = [dst.at[:, k] for _, dst in big] + ([small[1].at[k]] if small is not None else [])
        for a, blk in enumerate(landing):
            _remote(blk, blk, sems, n * k + a, (x, y, c)).wait_recv()
    for cp in _scatter_sends(big, small, sems):
        cp.wait_send()


def _exchange_shapes(gi, go):
    return [jax.ShapeDtypeStruct(a.shape[:2] + (a.shape[2] // 2, a.shape[3]), F32) for a in (gi, go)]


def _exchange_copies(srcs, dsts, small, sems):
    x, y, c, _ = _position()
    cps = [_remote(src.at[:, :, _halves(src.shape[2], c)[1]], dst, sems, a, (x, y, 1 - c))
           for a, (src, dst) in enumerate(zip(srcs, dsts))]
    if small is not None:
        cps.append(_remote(small[0], small[1], sems, len(srcs), (x, y, 1 - c)))
    return cps


def _exchange_start(srcs, dsts, small, sems):
    for cp in _exchange_copies(srcs, dsts, small, sems):
        cp.start()


def _exchange_finish(srcs, dsts, small, sems):
    for cp in _exchange_copies(srcs, dsts, small, sems):
        cp.wait()


def _exchange_sibling(gi, go, sm):
    def body(gi_ref, go_ref, sm_ref, ri_ref, ro_ref, rs_ref, send_sems, recv_sems):
        args = ((gi_ref, go_ref), (ri_ref, ro_ref), (sm_ref, rs_ref), (send_sems, recv_sems))
        _exchange_start(*args)
        _exchange_finish(*args)

    out_shape = _exchange_shapes(gi, go) + [jax.ShapeDtypeStruct(sm.shape, F32)]
    return _comm_call(body, "exchange_sibling", 3, out_shape, 3)(gi, go, sm)


def _scatter_shapes(pi, po):
    return [jax.ShapeDtypeStruct((a.shape[0], 3) + a.shape[2:], a.dtype) for a in (pi, po)]


def _join_sibling(fi, fo, fs, late):
    def body(fi_ref, fo_ref, fs_ref, late_ref, oi_ref, oo_ref, os_ref, sum_ref, send_sems, recv_sems,
             sib_ref, chip_ref, got_ref):
        x, y, c, chips = _position()
        sems = (send_sems, recv_sems)
        sent, got = [], []
        for a, (src, dst) in enumerate(((fi_ref, oi_ref), (fo_ref, oo_ref))):
            mine, other = _halves(src.shape[1], c)
            sent.append(_remote(src.at[:, mine], dst.at[:, mine], sems, a, (x, y, 1 - c)))
            got.append(_remote(dst.at[:, other], dst.at[:, other], sems, a, (x, y, c)))
        mine, other = _halves(SMALL_ROWS, c)
        sent.append(_remote(fs_ref.at[mine], os_ref.at[mine], sems, 2, (x, y, 1 - c)))
        got.append(_remote(os_ref.at[other], os_ref.at[other], sems, 2, (x, y, c)))
        for cp in sent:
            cp.start()

        to_sibling = _remote(late_ref, sib_ref, sems, 3, (x, y, 1 - c))
        to_sibling.start()
        to_sibling.wait()
        chip_ref[...] = late_ref[...] + sib_ref[...]
        to_chips = [_remote(chip_ref, got_ref.at[k], sems, 4 + k, (cx, cy, c)) for k, (cx, cy) in enumerate(chips)]
        for cp in to_chips:
            cp.start()
        for cp in to_chips:
            cp.wait()
        sum_ref[...] = (chip_ref[...] + got_ref[0]) + (got_ref[1] + got_ref[2])

        for cp in got:
            cp.wait_recv()
        for cp in sent:
            cp.wait_send()

    vmem = pl.BlockSpec(memory_space=pltpu.VMEM)
    return pl.pallas_call(
        body, name="join_sibling", in_specs=[ANY] * 3 + [vmem], out_specs=[ANY] * 3 + [vmem],
        out_shape=[jax.ShapeDtypeStruct(a.shape, F32) for a in (fi, fo, fs, late)],
        scratch_shapes=[pltpu.SemaphoreType.DMA((7,)), pltpu.SemaphoreType.DMA((7,)),
                        pltpu.VMEM(late.shape, F32), pltpu.VMEM(late.shape, F32), pltpu.VMEM((3,) + late.shape, F32)],
        input_output_aliases={0: 0, 1: 1, 2: 2},
    )(fi, fo, fs, late)


SMALL_SIZES = (DEPTH * D_MODEL, D_MODEL, DEPTH * POOL_WIDTH, 128, DEPTH * 4 * GROUP_DIM * GROUP_DIM)
LOSS_SLOT = DEPTH * D_MODEL + D_MODEL + DEPTH * POOL_WIDTH + DEPTH * 8


def _pack_small(norm_g, final_g, pool_scale, forget_bias, pool_w, loss=None):
    fb = forget_bias.reshape(-1)
    if loss is not None:
        fb = jnp.concatenate([fb, loss.reshape(1)])
    fb = jnp.pad(fb, (0, 128 - fb.size))
    flat = jnp.concatenate([norm_g.reshape(-1), final_g.reshape(-1), pool_scale.reshape(-1), fb,
                            pool_w.reshape(-1)])
    return jnp.pad(flat, (0, SMALL_ROWS * 128 - flat.size)).reshape(SMALL_ROWS, 128)


def _unpack_small(packed):
    flat = packed.reshape(-1)
    offs = [0]
    for n in SMALL_SIZES:
        offs.append(offs[-1] + n)
    norm_g, final_g, pool_scale, fb, pool_w = [flat[offs[i]:offs[i + 1]] for i in range(5)]
    return (norm_g.reshape(DEPTH, D_MODEL), final_g, pool_scale.reshape(DEPTH, POOL_WIDTH),
            fb[:DEPTH * 8].reshape(DEPTH, 8), pool_w.reshape(DEPTH, 4, GROUP_DIM, GROUP_DIM))


def kernel(x, norm_g, w_in, forget_bias, pool_w, pool_scale, w_out, final_g, loss_target, m_norm_g, m_w_in, m_forget_bias, m_pool_w, m_pool_scale, m_w_out, m_final_g, v_norm_g, v_w_in, v_forget_bias, v_pool_w, v_pool_scale, v_w_out, v_final_g):
    x0 = x[0]
    tgt = loss_target[0]
    s = x0.shape[0]

    me = 2 * lax.axis_index("x") + lax.axis_index("y")
    place = jnp.stack([me, lax.axis_index("c")]).astype(jnp.int32)
    wi_b, wo_b = w_in.astype(BF16), w_out.astype(BF16)

    def chip_piece(own, got, chip):
        rel = jnp.bitwise_xor(chip, me)
        if own is None:
            return lax.dynamic_index_in_dim(got, rel, axis=1, keepdims=False)[0]
        other = lax.dynamic_index_in_dim(got, jnp.maximum(rel, 1), axis=1, keepdims=False)
        return jnp.where(rel == 0, own, other)[0]

    def whole_w_out(l, go, own_gathered):
        own_o = None if own_gathered else wo_b[l:l + 1]
        return jnp.concatenate([chip_piece(own_o, go, k) for k in range(N_CHIPS)], axis=0)

    pool_w_b = pool_w.astype(BF16)
    fb_pad = jnp.pad(forget_bias, ((0, 0), (0, 128 - forget_bias.shape[1])))
    head_sel = (jnp.arange(ATTN_WIDTH)[None, :] // HEAD_DIM == jnp.arange(8)[:, None]).astype(BF16)

    saved, weights = [], []
    xl = x0
    gathered = _gather_weights(wi_b[0:1], wo_b[0:1])
    for l in range(DEPTH):
        w_o = whole_w_out(l, gathered[1], own_gathered=l > 0)
        pu, pg, q, k, v, ag, lf, c, h, w_t = _inproj_fwd(place, xl, norm_g[l][None], None if l > 0 else wi_b[0],
                                                         gathered[0], fb_pad[l][None])
        weights.append((w_t, w_o))
        po = _pool_fwd(pu, pg, pool_w_b[l], pool_scale[l][None])
        if l + 1 < DEPTH:
            ma, o, lse, *gathered = _attn_fwd(q, k, v, c, ag, carry=(wi_b[l + 1:l + 2], wo_b[l + 1:l + 2]))
        else:
            ma, o, lse = _attn_fwd(q, k, v, c, ag)
        saved.append((xl, h, pu, pg, q, k, v, ag, lf, c, po, ma, o, lse))
        if l + 1 < DEPTH:
            xl = _outproj_fwd(xl, po, ma, w_o)
    dx, g_final, loss_part = _loss_head(xl, po, ma, w_o, tgt, final_g[None])

    g_norm, g_fb, g_pw, g_ps = [], [], [], []
    parts = {}
    pending = None
    for l in reversed(range(DEPTH)):
        xl, h, pu, pg, q, k, v, ag, lf, c, po, ma, o, lse = saved[l]
        w_t, w_o = weights[l]
        dmp, do, dag, dt, gwo = _outproj_bwd(dx, po, ma, ag, o, w_o, head_sel)
        if pending is None:
            dq, dk, dv, dcs, drs = _attn_bwd(q, k, v, do, c, lse, dt.reshape(N_PAIRS, 2, s))
        else:
            dq, dk, dv, dcs, drs, qi, qo = _attn_bwd(q, k, v, do, c, lse, dt.reshape(N_PAIRS, 2, s), carry=pending)
            parts[l + 1] += (qi, qo)
        drs = jnp.pad(drs[:, 0:2, :].reshape(8, s).T, ((0, 0), (0, 120)))
        dpf, dfb = _forget_bwd(dcs, drs, lf)
        dpu, dpg, gpw, gps = _pool_bwd(pu, pg, dmp, pool_w_b[l], pool_scale[l][None])
        dsec = (dpu, dpg, dq, dk, dv, dag, dpf)
        gwi = _inproj_bwd_w(h, dsec)
        g_fb.append(dfb[0, :8])
        g_pw.append(gpw)
        g_ps.append(gps[0])
        send_i = gwi[None]
        send_o = gwo.reshape(1, N_CHIPS, SHARD_ROWS, D_MODEL)
        if l > 0:
            dx, gn, ri, ro = _inproj_bwd_x(dsec, w_t, xl, norm_g[l][None], dx, exchange=(send_i, send_o))
            g_norm.append(gn[0])
        else:
            g_norm.append(jnp.zeros((D_MODEL,), F32))
            g_norm, g_fb, g_pw, g_ps = [t[::-1] for t in (g_norm, g_fb, g_pw, g_ps)]
            send_s = _pack_small(jnp.stack(g_norm), g_final[0], jnp.stack(g_ps), jnp.stack(g_fb), jnp.stack(g_pw),
                                 loss=loss_part[0, 0])
            ri, ro, rs = _exchange_sibling(send_i, send_o, send_s)
        pending = (_chip_sum(place, send_i, ri, "chip_sum_w_in", 256),
                   _chip_sum(place, send_o, ro, "chip_sum_w_out", SHARD_ROWS // 2))
        parts[l] = (send_i, ri, send_o, ro)
    ps_ = _add2(send_s, rs, "chip_sum_small", SMALL_ROWS // 4)
    dx, gn0, qi, qo, qs = _inproj_bwd_x(dsec, w_t, xl, norm_g[0][None], dx, scatter=(*pending, ps_))
    parts[0] += (qi, qo)
    fi = lax.empty((DEPTH, D_MODEL, SHARD_COLS), F32)
    fo = lax.empty((DEPTH, SHARD_ROWS, D_MODEL), F32)
    for l in range(DEPTH):
        send_i, ri, send_o, ro, qi, qo = parts[l]
        fi = _mesh_sum(place, send_i, ri, qi, fi, l, "mesh_sum_w_in", 256)
        fo = _mesh_sum(place, send_o, ro, qo, fo, l, "mesh_sum_w_out", SHARD_ROWS // 2)
    hs = _mesh_sum_small(place, ps_, qs, SMALL_HALF // 4)
    fi, fo, fs, gn0 = _join_sibling(fi, fo, hs, gn0.reshape(8, 128))
    fs = jnp.concatenate([fs[0:8] + gn0, fs[8:]], axis=0)

    col_major = lambda a: jnp.transpose(a, (2, 0, 1))
    flat_o = lambda a: a.reshape(-1, D_MODEL)
    out_i = _adamw(col_major(fi), col_major(w_in), col_major(m_w_in), col_major(v_w_in), "adamw_w_in", 77)
    out_i = [jnp.transpose(t, (1, 2, 0)) for t in out_i]
    out_o = _adamw(flat_o(fo), flat_o(w_out), flat_o(m_w_out), flat_o(v_w_out), "adamw_w_out", 256)
    out_s = _adamw(fs, _pack_small(norm_g, final_g, pool_scale, forget_bias, pool_w),
                   _pack_small(m_norm_g, m_final_g, m_pool_scale, m_forget_bias, m_pool_w),
                   _pack_small(v_norm_g, v_final_g, v_pool_scale, v_forget_bias, v_pool_w),
                   "adamw_small", SMALL_ROWS // 4)

    loss = out_s[0].reshape(-1)[LOSS_SLOT]
    groups = []
    for oi, oo, os_ in zip(out_i, out_o, out_s):
        sn, sf, sp, sb, sw = _unpack_small(os_)
        groups.append((sn, oi.reshape(w_in.shape), sb, sw, sp, oo.reshape(w_out.shape), sf))
    return (loss, dx[None]) + tuple(t for grp in groups for t in grp)
```

```python
import jax
import jax.numpy as jnp
from jax import lax
from jax.experimental import pallas as pl
from jax.experimental.pallas import tpu as pltpu

F32 = jnp.float32
BF16 = jnp.bfloat16

D_MODEL = 1024
DEPTH = 4
POOL_WIDTH = 512
ATTN_WIDTH = 512
HEAD_DIM = 64
PAIR = 2 * HEAD_DIM
N_PAIRS = ATTN_WIDTH // PAIR
POOL_WINDOWS = (2, 4, 8, 16)
GROUP_DIM = 128
HALO = 16
IN_COLS = 3080
OFF_F = 3072
IN_COLS_PAD = 3200
SECTIONS = ((0, 512), (512, 512), (1024, 512), (1536, 512), (2048, 512), (2560, 512), (OFF_F, 128))
N_CHIPS = 4
SHARD_COLS = IN_COLS // N_CHIPS
SHARD_ROWS = D_MODEL // N_CHIPS
RMS_EPS = 1e-6
NEG = -1e30
Q_SCALE = 0.125

ADAM_LR = 0.001
ADAM_B1 = 0.9
ADAM_B2 = 0.999
ADAM_EPS = 1e-08
ADAM_WD = 0.01
ADAM_STEP = 10

SMALL_ROWS = 2112
SMALL_HALF = SMALL_ROWS // 2

NT = (((1,), (1,)), ((), ()))
TN = (((0,), (0,)), ((), ()))
MESH = pl.DeviceIdType.MESH
ANY = pl.BlockSpec(memory_space=pl.ANY)


def _params(semantics, vmem_mb=48):
    return pltpu.CompilerParams(dimension_semantics=semantics, vmem_limit_bytes=vmem_mb << 20)


def _row_spec(tm, cols):
    return pl.BlockSpec((tm, cols), lambda i: (i, 0))


def _full_spec(shape):
    return pl.BlockSpec(shape, lambda *_: (0,) * len(shape))


def _sigmoid(x):
    return jax.nn.sigmoid(x)


def _scan_rows(a, reverse=False):
    n = a.shape[0]
    row = lax.broadcasted_iota(jnp.int32, a.shape, 0)
    k = 1
    while k < n:
        if reverse:
            a = a + jnp.where(row < n - k, pltpu.roll(a, n - k, 0), 0.0)
        else:
            a = a + jnp.where(row >= k, pltpu.roll(a, k, 0), 0.0)
        k *= 2
    return a


def _inproj_fwd(place, x, g, own, got, fb, pw, ps, tm=512):
    s = x.shape[0]

    def body(place_ref, x_ref, g_ref, fb_ref, pw_ref, ps_ref, *refs):
        own_ref = refs[0] if own is not None else None
        (got_ref, pu_ref, pg_ref, q_ref, k_ref, v_ref, ag_ref, lf_ref, c_ref, h_ref, po_ref, wt_ref,
         carry_ref, halo_ref, w_ref, piece_ref, sem) = refs[own is not None:]
        step = pl.program_id(0)

        @pl.when(step == 0)
        def _():
            carry_ref[...] = jnp.zeros_like(carry_ref)
            halo_ref[...] = jnp.zeros_like(halo_ref)
            for chip in range(N_CHIPS):
                rel = jnp.bitwise_xor(chip, place_ref[0])

                def fetch(src):
                    cp = pltpu.make_async_copy(src, piece_ref, sem)
                    cp.start()
                    cp.wait()

                if own_ref is None:
                    fetch(got_ref.at[0, rel])
                else:
                    pl.when(rel == 0)(lambda: fetch(own_ref))
                    pl.when(rel != 0)(lambda: fetch(got_ref.at[0, rel]))
                w_ref[:, chip * SHARD_COLS:(chip + 1) * SHARD_COLS] = piece_ref[...]
            w_ref[:, IN_COLS:] = jnp.zeros((D_MODEL, IN_COLS_PAD - IN_COLS), BF16)
            for t in range(IN_COLS_PAD // 128):
                cols = slice(t * 128, (t + 1) * 128)
                wt_ref[cols, :] = jnp.transpose(w_ref[:, cols].astype(F32)).astype(BF16)

        xf = x_ref[...]
        r = lax.rsqrt(jnp.mean(xf * xf, axis=-1, keepdims=True) + RMS_EPS)
        h = (xf * r * g_ref[...]).astype(BF16)
        h_ref[...] = h

        def proj(sec):
            off, n = SECTIONS[sec]
            return jnp.dot(h, w_ref[:, off:off + n], preferred_element_type=F32)

        pu_ref[...] = proj(0).astype(BF16)
        pg_ref[...] = proj(1).astype(BF16)
        u = pu_ref[...].astype(F32)
        d = _pool_delta(u, halo_ref[...], step * tm)
        halo_ref[...] = u[tm - HALO:, :]
        for gi in range(len(POOL_WINDOWS)):
            cols = slice(gi * GROUP_DIM, (gi + 1) * GROUP_DIM)
            zg = jnp.dot(d[gi].astype(BF16), pw_ref[gi], preferred_element_type=F32)
            gate = pg_ref[:, cols].astype(F32)
            po_ref[:, cols] = (zg * ps_ref[:, cols] * (gate * _sigmoid(gate))).astype(BF16)
        q_ref[...] = (proj(2) * Q_SCALE).astype(BF16)
        k_ref[...] = proj(3).astype(BF16)
        v_ref[...] = proj(4).astype(BF16)
        ag_ref[...] = proj(5).astype(BF16)
        z = proj(6) + fb_ref[...]
        lf = jnp.minimum(z, 0.0) - jnp.log(1.0 + jnp.exp(-jnp.abs(z)))
        lf_ref[...] = lf
        c_ref[...] = _scan_rows(lf) + carry_ref[0:1, :]
        carry_ref[0:1, :] = c_ref[tm - 1:tm, :]

    act = jax.ShapeDtypeStruct((s, 512), BF16)
    rows = lambda cols: pl.BlockSpec((tm, cols), lambda i, p: (i, 0))
    whole = lambda shape: pl.BlockSpec(shape, lambda i, p: (0,) * len(shape))
    shards = ([] if own is None else [own]) + [got]
    return pl.pallas_call(
        body, name="inproj_fwd" if own is None else "inproj_fwd_first",
        grid_spec=pltpu.PrefetchScalarGridSpec(
            num_scalar_prefetch=1, grid=(s // tm,),
            in_specs=[rows(D_MODEL), whole((1, D_MODEL)), whole((1, 128)), whole((4, GROUP_DIM, GROUP_DIM)),
                      whole((1, 512))] + [ANY] * len(shards),
            out_specs=[rows(512)] * 6 + [rows(128), rows(128), rows(D_MODEL), rows(512),
                                         whole((IN_COLS_PAD, D_MODEL))],
            scratch_shapes=[pltpu.VMEM((8, 128), F32), pltpu.VMEM((HALO, 512), F32),
                            pltpu.VMEM((D_MODEL, IN_COLS_PAD), BF16),
                            pltpu.VMEM((D_MODEL, SHARD_COLS), BF16), pltpu.SemaphoreType.DMA]),
        out_shape=[act] * 6 + [jax.ShapeDtypeStruct((s, 128), F32), jax.ShapeDtypeStruct((s, 128), F32),
                               jax.ShapeDtypeStruct((s, D_MODEL), BF16), act,
                               jax.ShapeDtypeStruct((IN_COLS_PAD, D_MODEL), BF16)],
        compiler_params=_params(("arbitrary",), vmem_mb=56),
    )(place, x, g, fb, pw, ps, *shards)


def _window_sums(ext, forward):
    n = ext.shape[0]
    outs = []
    for gi, w in enumerate(POOL_WINDOWS):
        a = ext[:, gi * GROUP_DIM:(gi + 1) * GROUP_DIM]
        k = 1
        while k < w:
            a = a + pltpu.roll(a, (n - k) if forward else k, 0)
            k *= 2
        outs.append(a)
    return outs


def _over_count(a, row0, gi):
    w = POOL_WINDOWS[gi]
    t = row0 + lax.broadcasted_iota(jnp.int32, (HALO, GROUP_DIM), 0)
    head = a[0:HALO, :] / jnp.minimum(t + 1, w).astype(F32)
    return jnp.concatenate([head, a[HALO:, :] * (1.0 / w)], axis=0)


def _pool_delta(u, halo, row0):
    sums = _window_sums(jnp.concatenate([halo, u], axis=0), forward=False)
    return [_over_count(sums[gi][HALO:, :], row0, gi) - u[:, gi * GROUP_DIM:(gi + 1) * GROUP_DIM]
            for gi in range(len(POOL_WINDOWS))]


def _split3(x):
    hi = x.astype(BF16).astype(F32)
    r1 = x - hi
    lo = r1.astype(BF16).astype(F32)
    return hi, lo, (r1 - lo).astype(BF16).astype(F32)


def _own(idx, h):
    return (idx < HEAD_DIM) if h == 0 else (idx >= HEAD_DIM)


def _spare(h):
    return HEAD_DIM if h == 0 else 0


def _attn_fwd(q, k, v, c, ag, carry=None, tq=512, tk=512, chunk=512):
    s = q.shape[0]
    n_carry = 0 if carry is None else len(carry)

    def body(*refs):
        q_ref, k_ref, v_ref, c_ref, ag_ref = refs[:5]
        carry_src = refs[5:5 + n_carry]
        ma_ref, o_ref, lse_ref = refs[5 + n_carry:8 + n_carry]
        carry_dst = refs[8 + n_carry:8 + 2 * n_carry]
        kx_ref, vt_ref, acc0_ref, acc1_ref, sta_ref, stb_ref = refs[8 + 2 * n_carry:14 + 2 * n_carry]
        carry_sems = refs[14 + 2 * n_carry:]
        j = pl.program_id(0)
        i = pl.program_id(1)

        if n_carry:
            @pl.when((j == 0) & (i == 0))
            def _():
                _gather_start(carry_src, carry_dst, carry_sems[:2])
                for cp in _gather_own(carry_src, carry_dst, carry_sems[2]):
                    cp.start()

        @pl.when(i == 0)
        def _():
            def prep(ch, carry):
                rows = pl.ds(pl.multiple_of(ch * chunk, chunk), chunk)
                kk = k_ref[rows, :].astype(F32)
                cc = c_ref[rows, :]
                vt = jnp.transpose(v_ref[rows, :].astype(F32))
                lane = lax.broadcasted_iota(jnp.int32, (chunk, PAIR), 1)
                sub = lax.broadcasted_iota(jnp.int32, (PAIR, chunk), 0)
                for h in range(2):
                    ccol = jnp.sum(jnp.where(lane == 2 * j + h, cc, 0.0), axis=1, keepdims=True)
                    kx = jnp.where(_own(lane, h), kk, 0.0)
                    for t, part in enumerate(_split3(-ccol)):
                        kx = jnp.where(lane == _spare(h) + t, part, kx)
                    kx_ref[h, rows, :] = kx.astype(BF16)
                    vt_ref[h, :, rows] = jnp.where(_own(sub, h), vt,
                                                   jnp.where(sub == _spare(h), 1.0, 0.0)).astype(BF16)
                return carry

            lax.fori_loop(0, s // chunk, prep, 0)

        qq = q_ref[...].astype(F32)
        lane_q = lax.broadcasted_iota(jnp.int32, (tq, PAIR), 1)
        qx = []
        for h in range(2):
            ones = (lane_q >= _spare(h)) & (lane_q < _spare(h) + 3)
            qx.append(jnp.transpose(jnp.where(_own(lane_q, h), qq, jnp.where(ones, 1.0, 0.0))).astype(BF16))
        accs = (acc0_ref, acc1_ref)
        for acc in accs:
            acc[...] = jnp.zeros_like(acc)

        def rows_of(kb):
            return pl.ds(pl.multiple_of(kb * tk, tk), tk)

        def scores(kb, dst):
            for h in range(2):
                dst[h] = jnp.dot(kx_ref[h, rows_of(kb), :], qx[h], preferred_element_type=F32)

        def consume(kb, src, m, masked):
            m_out, alpha, pv = [], [], []
            for h in range(2):
                sh = src[h]
                if masked:
                    kpos = kb * tk + lax.broadcasted_iota(jnp.int32, (tk, tq), 0)
                    qpos = i * tq + lax.broadcasted_iota(jnp.int32, (tk, tq), 1)
                    sh = jnp.where(qpos >= kpos, sh, NEG)
                m_new = jnp.maximum(m[h], jnp.max(sh, axis=0, keepdims=True))
                pt = jnp.exp(sh - m_new).astype(BF16)
                alpha.append(jnp.exp(m[h] - m_new))
                pv.append(jnp.dot(vt_ref[h, :, rows_of(kb)], pt, preferred_element_type=F32))
                m_out.append(m_new)
            for h in range(2):
                accs[h][...] = accs[h][...] * alpha[h] + pv[h]
            return tuple(m_out)

        n_full = (i * tq) // tk

        def two_blocks(t, m):
            kb = 2 * t
            scores(kb + 1, stb_ref)
            m = consume(kb, sta_ref, m, False)
            scores(kb + 2, sta_ref)
            return consume(kb + 1, stb_ref, m, False)

        def odd_tail(m):
            scores(n_full, stb_ref)
            m = consume(n_full - 1, sta_ref, m, False)
            return consume(n_full, stb_ref, m, True)

        def even_tail(m):
            return consume(n_full, sta_ref, m, True)

        scores(0, sta_ref)
        m0 = jnp.full((1, tq), NEG, F32)
        m = lax.fori_loop(0, n_full // 2, two_blocks, (m0, m0))
        m = lax.cond(lax.rem(n_full, 2) == 1, odd_tail, even_tail, m)

        sub_o = lax.broadcasted_iota(jnp.int32, (PAIR, tq), 0)
        den = [accs[h][_spare(h):_spare(h) + 1, :] for h in range(2)]
        ot = jnp.where(sub_o < HEAD_DIM, acc0_ref[...] / den[0], acc1_ref[...] / den[1])
        o = jnp.transpose(ot)
        o_ref[...] = o.astype(BF16)
        gate = ag_ref[...].astype(F32)
        ma_ref[...] = (o * (gate * _sigmoid(gate))).astype(BF16)
        sub8 = lax.broadcasted_iota(jnp.int32, (8, tq), 0)
        lse_ref[...] = jnp.where(sub8 == 0, m[0] + jnp.log(den[0]), m[1] + jnp.log(den[1]))

        if n_carry:
            @pl.when((j == N_PAIRS - 1) & (i == s // tq - 1))
            def _():
                _gather_finish(carry_src, carry_dst, carry_sems[:2])
                for cp in _gather_own(carry_src, carry_dst, carry_sems[2]):
                    cp.wait()

    carried = [] if carry is None else list(carry)
    carry_sem = ([pltpu.SemaphoreType.DMA((GATHER_SEMS,))] * 2 + [pltpu.SemaphoreType.DMA((2,))]) if n_carry else []
    return pl.pallas_call(
        body, name="attn_fwd_gather" if n_carry else "attn_fwd", grid=(N_PAIRS, s // tq),
        in_specs=[pl.BlockSpec((tq, PAIR), lambda j, i: (i, j)),
                  pl.BlockSpec((s, PAIR), lambda j, i: (0, j)),
                  pl.BlockSpec((s, PAIR), lambda j, i: (0, j)),
                  pl.BlockSpec((s, 128), lambda j, i: (0, 0)),
                  pl.BlockSpec((tq, PAIR), lambda j, i: (i, j))] + [ANY] * n_carry,
        out_specs=[pl.BlockSpec((tq, PAIR), lambda j, i: (i, j)),
                   pl.BlockSpec((tq, PAIR), lambda j, i: (i, j)),
                   pl.BlockSpec((None, 8, tq), lambda j, i: (j, 0, i))] + [ANY] * n_carry,
        out_shape=[jax.ShapeDtypeStruct((s, ATTN_WIDTH), BF16), jax.ShapeDtypeStruct((s, ATTN_WIDTH), BF16),
                   jax.ShapeDtypeStruct((N_PAIRS, 8, s), F32)] + (_gather_shapes(*carried) if n_carry else []),
        scratch_shapes=[pltpu.VMEM((2, s, PAIR), BF16), pltpu.VMEM((2, PAIR, s), BF16),
                        pltpu.VMEM((PAIR, tq), F32), pltpu.VMEM((PAIR, tq), F32),
                        pltpu.VMEM((2, tk, tq), F32), pltpu.VMEM((2, tk, tq), F32)] + carry_sem,
        compiler_params=_params(("arbitrary", "arbitrary")),
    )(q, k, v, c, ag, *carried)


def _outproj_fwd(x, po, ma, wo, tm=512):
    s = x.shape[0]

    def body(x_ref, po_ref, ma_ref, wo_ref, xn_ref):
        xn_ref[...] = (x_ref[...]
                       + jnp.dot(po_ref[...], wo_ref[0:POOL_WIDTH, :], preferred_element_type=F32)
                       + jnp.dot(ma_ref[...], wo_ref[POOL_WIDTH:, :], preferred_element_type=F32))

    return pl.pallas_call(
        body, name="outproj_fwd", grid=(s // tm,),
        in_specs=[_row_spec(tm, D_MODEL), _row_spec(tm, 512), _row_spec(tm, 512),
                  _full_spec((D_MODEL, D_MODEL))],
        out_specs=_row_spec(tm, D_MODEL),
        out_shape=jax.ShapeDtypeStruct((s, D_MODEL), F32),
        compiler_params=_params(("parallel",)),
    )(x, po, ma, wo)


def _loss_head(x, po, ma, wo, tgt, g, tm=512):
    s = x.shape[0]

    def body(x_ref, po_ref, ma_ref, wo_ref, t_ref, g_ref, dx_ref, dg_ref, loss_ref):
        @pl.when(pl.program_id(0) == 0)
        def _():
            dg_ref[...] = jnp.zeros_like(dg_ref)
            loss_ref[...] = jnp.zeros_like(loss_ref)

        xf = (x_ref[...]
              + jnp.dot(po_ref[...], wo_ref[0:POOL_WIDTH, :], preferred_element_type=F32)
              + jnp.dot(ma_ref[...], wo_ref[POOL_WIDTH:, :], preferred_element_type=F32))
        r = lax.rsqrt(jnp.mean(xf * xf, axis=-1, keepdims=True) + RMS_EPS)
        xh = xf * r
        gg = g_ref[...]
        e = xh * gg - t_ref[...]
        loss_ref[...] += 0.5 * jnp.sum(jnp.mean(e * e, axis=-1, keepdims=True))
        dy = e * (1.0 / D_MODEL)
        u = dy * gg
        dx_ref[...] = r * (u - xh * jnp.mean(xh * u, axis=-1, keepdims=True))
        dg_ref[...] += jnp.sum(dy * xh, axis=0, keepdims=True)

    return pl.pallas_call(
        body, name="loss_head", grid=(s // tm,),
        in_specs=[_row_spec(tm, D_MODEL), _row_spec(tm, 512), _row_spec(tm, 512), _full_spec((D_MODEL, D_MODEL)),
                  _row_spec(tm, D_MODEL), _full_spec((1, D_MODEL))],
        out_specs=[_row_spec(tm, D_MODEL), _full_spec((1, D_MODEL)), _full_spec((8, 128))],
        out_shape=[jax.ShapeDtypeStruct((s, D_MODEL), F32), jax.ShapeDtypeStruct((1, D_MODEL), F32),
                   jax.ShapeDtypeStruct((8, 128), F32)],
        compiler_params=_params(("arbitrary",)),
    )(x, po, ma, wo, tgt, g)


def _outproj_bwd(dx, po, ma, ag, o, wo, head_sel, tm=512):
    s = dx.shape[0]

    def body(dx_ref, po_ref, ma_ref, ag_ref, o_ref, wo_ref, sel_ref, dmp_ref, do_ref, dag_ref, dt_ref, gwo_ref):
        @pl.when(pl.program_id(0) == 0)
        def _():
            gwo_ref[...] = jnp.zeros_like(gwo_ref)

        dxb = dx_ref[...].astype(BF16)
        dm = lax.dot_general(dxb, wo_ref[...], NT, preferred_element_type=F32)
        dmp_ref[...] = dm[:, 0:POOL_WIDTH].astype(BF16)
        dma = dm[:, POOL_WIDTH:]
        gate = ag_ref[...].astype(F32)
        of = o_ref[...].astype(F32)
        sg = _sigmoid(gate)
        do = dma * (gate * sg)
        do_ref[...] = do.astype(BF16)
        dag_ref[...] = (dma * of * (sg * (1.0 + gate * (1.0 - sg)))).astype(BF16)
        prod = do * of
        hi = prod.astype(BF16)
        lo = (prod - hi.astype(F32)).astype(BF16)
        dt_ref[...] = (lax.dot_general(sel_ref[...], hi, NT, preferred_element_type=F32)
                       + lax.dot_general(sel_ref[...], lo, NT, preferred_element_type=F32))
        gwo_ref[0:POOL_WIDTH, :] += lax.dot_general(po_ref[...], dxb, TN, preferred_element_type=F32)
        gwo_ref[POOL_WIDTH:, :] += lax.dot_general(ma_ref[...], dxb, TN, preferred_element_type=F32)

    act = jax.ShapeDtypeStruct((s, 512), BF16)
    return pl.pallas_call(
        body, name="outproj_bwd", grid=(s // tm,),
        in_specs=[_row_spec(tm, D_MODEL)] + [_row_spec(tm, 512)] * 4
                 + [_full_spec((D_MODEL, D_MODEL)), _full_spec((8, ATTN_WIDTH))],
        out_specs=[_row_spec(tm, 512)] * 3 + [pl.BlockSpec((8, tm), lambda i: (0, i)),
                                              _full_spec((D_MODEL, D_MODEL))],
        out_shape=[act, act, act, jax.ShapeDtypeStruct((8, s), F32),
                   jax.ShapeDtypeStruct((D_MODEL, D_MODEL), F32)],
        compiler_params=_params(("arbitrary",)),
    )(dx, po, ma, ag, o, wo, head_sel)


def _attn_bwd(q, k, v, do, c, lse, dt, carry=None, tq=256, chunk=512):
    s = q.shape[0]
    tk = 2 * tq
    nq = s // tq
    n_carry = 0 if carry is None else len(carry)

    def body(*refs):
        q_ref, k_ref, v_ref, do_ref, c_ref, lse_ref, dt_ref = refs[:7]
        carry_src = refs[7:7 + n_carry]
        dq_ref, dk_ref, dv_ref, dcs_ref, drs_ref = refs[7 + n_carry:12 + n_carry]
        carry_dst = refs[12 + n_carry:12 + 2 * n_carry]
        (qxt_ref, doxt_ref, qm_ref, dox_ref, dqt0_ref, dqt1_ref, dk0_ref, dk1_ref, dvx_ref,
         kx_ref, vx_ref, ktm_ref, sa_ref, sb_ref) = refs[12 + 2 * n_carry:26 + 2 * n_carry]
        carry_sems = refs[26 + 2 * n_carry:]
        carry_big = tuple(zip(carry_src, carry_dst))
        j = pl.program_id(0)
        b = pl.program_id(1)
        dqts = (dqt0_ref, dqt1_ref)
        dks = (dk0_ref, dk1_ref)

        if n_carry:
            @pl.when((j == 0) & (b == 0))
            def _():
                _scatter_start(carry_big, None, carry_sems)

        @pl.when(b == 0)
        def _():
            def prep(ch, carry):
                rows = pl.ds(pl.multiple_of(ch * chunk, chunk), chunk)
                qq = q_ref[rows, :].astype(F32)
                dd = do_ref[rows, :].astype(F32)
                qt = jnp.transpose(qq)
                ddt = jnp.transpose(dd)
                lane = lax.broadcasted_iota(jnp.int32, (chunk, PAIR), 1)
                sub = lax.broadcasted_iota(jnp.int32, (PAIR, chunk), 0)
                for h in range(2):
                    sp = _spare(h)
                    qm_ref[h, rows, :] = jnp.where(_own(lane, h), qq, jnp.where(lane == sp, 1.0, 0.0)).astype(BF16)
                    dox_ref[h, rows, :] = jnp.where(_own(lane, h), dd, 0.0).astype(BF16)
                    qx = jnp.where(_own(sub, h), qt, jnp.where((sub >= sp) & (sub < sp + 3), 1.0, 0.0))
                    for t, part in enumerate(_split3(-lse_ref[pl.ds(h, 1), rows])):
                        qx = jnp.where(sub == sp + 3 + t, part, qx)
                    qxt_ref[h, :, rows] = qx.astype(BF16)
                    dx = jnp.where(_own(sub, h), ddt, 0.0)
                    for t, part in enumerate(_split3(-dt_ref[pl.ds(h, 1), rows])):
                        dx = jnp.where(sub == sp + t, part, dx)
                    doxt_ref[h, :, rows] = dx.astype(BF16)
                return carry

            lax.fori_loop(0, s // chunk, prep, 0)
            for ref in dqts:
                ref[...] = jnp.zeros_like(ref)

        kk = k_ref[...].astype(F32)
        vv = v_ref[...].astype(F32)
        cc = c_ref[...]
        kt = jnp.transpose(kk)
        lane = lax.broadcasted_iota(jnp.int32, (tk, PAIR), 1)
        sub = lax.broadcasted_iota(jnp.int32, (PAIR, tk), 0)
        for h in range(2):
            sp = _spare(h)
            ccol = jnp.sum(jnp.where(lane == 2 * j + h, cc, 0.0), axis=1, keepdims=True)
            kx = jnp.where(_own(lane, h), kk, jnp.where((lane >= sp + 3) & (lane < sp + 6), 1.0, 0.0))
            for t, part in enumerate(_split3(-ccol)):
                kx = jnp.where(lane == sp + t, part, kx)
            kx_ref[h] = kx.astype(BF16)
            vx_ref[h] = jnp.where(_own(lane, h), vv, jnp.where((lane >= sp) & (lane < sp + 3), 1.0, 0.0)).astype(BF16)
            ktm_ref[h] = jnp.where(_own(sub, h), kt, jnp.where(sub == sp, 1.0, 0.0)).astype(BF16)
        for ref in dks:
            ref[...] = jnp.zeros_like(ref)
        dvx_ref[...] = jnp.zeros_like(dvx_ref)

        def cols_of(i):
            return pl.ds(pl.multiple_of(i * tq, tq), tq)

        def scores(i, dst):
            cols = cols_of(jnp.minimum(i, nq - 1))
            for h in range(2):
                dst[h] = jnp.dot(kx_ref[h], qxt_ref[h, :, cols], preferred_element_type=F32)
                dst[2 + h] = jnp.dot(vx_ref[h], doxt_ref[h, :, cols], preferred_element_type=F32)

        def consume(i, src, masked):
            cols = cols_of(i)
            for h in range(2):
                arg = src[h]
                if masked:
                    kpos = b * tk + lax.broadcasted_iota(jnp.int32, (tk, tq), 0)
                    qpos = i * tq + lax.broadcasted_iota(jnp.int32, (tk, tq), 1)
                    arg = jnp.where(qpos >= kpos, arg, NEG)
                pt = jnp.exp(arg)
                dst = (pt * src[2 + h]).astype(BF16)
                dvx_ref[...] += jnp.dot(pt.astype(BF16), dox_ref[h, cols, :], preferred_element_type=F32)
                dks[h][...] += jnp.dot(dst, qm_ref[h, cols, :], preferred_element_type=F32)
                dqts[h][:, cols] += jnp.dot(ktm_ref[h], dst, preferred_element_type=F32)

        i0 = 2 * b

        def two_blocks(t, masked):
            i = i0 + 2 * t
            scores(i + 1, sb_ref)
            consume(i, sa_ref, masked)
            scores(i + 2, sa_ref)
            consume(i + 1, sb_ref, masked)

        def loop_body(t, carry):
            two_blocks(t, False)
            return carry

        scores(i0, sa_ref)
        two_blocks(0, True)
        lax.fori_loop(1, (nq - i0) // 2, loop_body, 0)

        dv_ref[...] = dvx_ref[...].astype(BF16)
        dk_ref[...] = jnp.where(lane < HEAD_DIM, dk0_ref[...], dk1_ref[...]).astype(BF16)
        dcs_ref[...] = jnp.where(lane == _spare(0), dk0_ref[...], jnp.where(lane == _spare(1), dk1_ref[...], 0.0))

        @pl.when(b == s // tk - 1)
        def _():
            sub8 = lax.broadcasted_iota(jnp.int32, (8, s), 0)
            drs_ref[...] = jnp.where(sub8 == 0, dqt0_ref[_spare(0):_spare(0) + 1, :],
                                     jnp.where(sub8 == 1, dqt1_ref[_spare(1):_spare(1) + 1, :], 0.0))
            sub_c = lax.broadcasted_iota(jnp.int32, (PAIR, chunk), 0)
            for ch in range(s // chunk):
                rows = pl.ds(ch * chunk, chunk)
                both = jnp.where(sub_c < HEAD_DIM, dqt0_ref[:, rows], dqt1_ref[:, rows])
                dq_ref[rows, :] = (jnp.transpose(both) * Q_SCALE).astype(BF16)

        if n_carry:
            @pl.when((j == N_PAIRS - 1) & (b == s // tk - 1))
            def _():
                _scatter_finish(carry_big, None, carry_sems)

    act = jax.ShapeDtypeStruct((s, ATTN_WIDTH), BF16)
    pair_rows = pl.BlockSpec((s, PAIR), lambda j, b: (0, j))
    pair_blk = pl.BlockSpec((tk, PAIR), lambda j, b: (b, j))
    stat = pl.BlockSpec((None, 2, s), lambda j, b: (j, 0, 0))
    carried = [] if carry is None else list(carry)
    carry_sem = [pltpu.SemaphoreType.DMA((SCATTER_SEMS,))] * 2 if n_carry else []
    return pl.pallas_call(
        body, name="attn_bwd_scatter" if n_carry else "attn_bwd", grid=(N_PAIRS, s // tk),
        in_specs=[pair_rows, pair_blk, pair_blk, pair_rows,
                  pl.BlockSpec((tk, 128), lambda j, b: (b, 0)),
                  pl.BlockSpec((None, 8, s), lambda j, b: (j, 0, 0)), stat] + [ANY] * n_carry,
        out_specs=[pair_rows, pair_blk, pair_blk, pair_blk,
                   pl.BlockSpec((None, 8, s), lambda j, b: (j, 0, 0))] + [ANY] * n_carry,
        out_shape=[act, act, act, jax.ShapeDtypeStruct((s, N_PAIRS * 128), F32),
                   jax.ShapeDtypeStruct((N_PAIRS, 8, s), F32)] + (_scatter_shapes(*carried) if n_carry else []),
        scratch_shapes=[pltpu.VMEM((2, PAIR, s), BF16), pltpu.VMEM((2, PAIR, s), BF16),
                        pltpu.VMEM((2, s, PAIR), BF16), pltpu.VMEM((2, s, PAIR), BF16),
                        pltpu.VMEM((PAIR, s), F32), pltpu.VMEM((PAIR, s), F32),
                        pltpu.VMEM((tk, PAIR), F32), pltpu.VMEM((tk, PAIR), F32), pltpu.VMEM((tk, PAIR), F32),
                        pltpu.VMEM((2, tk, PAIR), BF16), pltpu.VMEM((2, tk, PAIR), BF16),
                        pltpu.VMEM((2, PAIR, tk), BF16),
                        pltpu.VMEM((4, tk, tq), F32), pltpu.VMEM((4, tk, tq), F32)] + carry_sem,
        compiler_params=_params(("arbitrary", "arbitrary")),
    )(q, k, v, do, c, lse, dt, *carried)


def _forget_bwd(dcs, drs, lf, tm=512):
    s = lf.shape[0]
    n = s // tm

    def body(dcs_ref, drs_ref, lf_ref, dpf_ref, dfb_ref, carry_ref):
        @pl.when(pl.program_id(0) == 0)
        def _():
            carry_ref[...] = jnp.zeros_like(carry_ref)
            dfb_ref[...] = jnp.zeros_like(dfb_ref)

        lane = lax.broadcasted_iota(jnp.int32, (tm, 128), 1)
        dc = drs_ref[...]
        for pj in range(N_PAIRS):
            blk = dcs_ref[:, pj * 128:(pj + 1) * 128]
            for h in range(2):
                head = 2 * pj + h
                moved = pltpu.roll(blk, (head - _spare(h)) % 128, 1) if head != _spare(h) else blk
                dc = dc - jnp.where(lane == head, moved, 0.0)
        dlf = _scan_rows(dc, reverse=True) + carry_ref[0:1, :]
        carry_ref[...] = dlf[0:8, :]
        dz = jnp.where(lane < 8, dlf * (1.0 - jnp.exp(lf_ref[...])), 0.0)
        dpf_ref[...] = dz.astype(BF16)
        dfb_ref[...] += jnp.sum(dz, axis=0, keepdims=True)

    return pl.pallas_call(
        body, name="forget_bwd", grid=(n,),
        in_specs=[pl.BlockSpec((tm, N_PAIRS * 128), lambda i: (n - 1 - i, 0)),
                  pl.BlockSpec((tm, 128), lambda i: (n - 1 - i, 0)),
                  pl.BlockSpec((tm, 128), lambda i: (n - 1 - i, 0))],
        out_specs=[pl.BlockSpec((tm, 128), lambda i: (n - 1 - i, 0)), _full_spec((1, 128))],
        out_shape=[jax.ShapeDtypeStruct((s, 128), BF16), jax.ShapeDtypeStruct((1, 128), F32)],
        scratch_shapes=[pltpu.VMEM((8, 128), F32)],
        compiler_params=_params(("arbitrary",)),
    )(dcs, drs, lf)


def _pool_bwd(pu, pg, dmp, pw, ps, tm=512):
    s = pu.shape[0]
    hb = tm // HALO
    n = s // tm
    last_halo = s // HALO - 1

    def body(pu_ref, halo_ref, pg_ref, dmp_ref, pgn_ref, dmpn_ref, pw_ref, ps_ref,
             dpu_ref, dpg_ref, gpw_ref, gps_ref):
        i = pl.program_id(0)

        @pl.when(i == 0)
        def _():
            gpw_ref[...] = jnp.zeros_like(gpw_ref)
            gps_ref[...] = jnp.zeros_like(gps_ref)

        u = pu_ref[...].astype(F32)
        halo = jnp.where(i == 0, 0.0, halo_ref[...].astype(F32))
        d = _pool_delta(u, halo, i * tm)
        e_parts, dd_parts = [], []
        for gi in range(len(POOL_WINDOWS)):
            cols = slice(gi * GROUP_DIM, (gi + 1) * GROUP_DIM)
            wg = pw_ref[gi]
            scale = ps_ref[:, cols]
            db = d[gi].astype(BF16)
            z = jnp.dot(db, wg, preferred_element_type=F32)
            gate = pg_ref[:, cols].astype(F32)
            sg = _sigmoid(gate)
            dm = dmp_ref[:, cols].astype(F32)
            dy = dm * (gate * sg)
            dpg_ref[:, cols] = (dm * (z * scale) * (sg * (1.0 + gate * (1.0 - sg)))).astype(BF16)
            gps_ref[:, cols] += jnp.sum(dy * z, axis=0, keepdims=True)
            dz = (dy * scale).astype(BF16)
            gpw_ref[gi] += lax.dot_general(db, dz, TN, preferred_element_type=F32)
            dd = lax.dot_general(dz, wg, NT, preferred_element_type=F32)
            gate_n = pgn_ref[:, cols].astype(F32)
            dz_n = (dmpn_ref[:, cols].astype(F32) * (gate_n * _sigmoid(gate_n)) * scale).astype(BF16)
            dd_n = lax.dot_general(dz_n, wg, NT, preferred_element_type=F32)
            dd_n = jnp.where(i == n - 1, 0.0, dd_n)
            dd_parts.append(dd)
            e_parts.append(jnp.concatenate([_over_count(dd, i * tm, gi), dd_n * (1.0 / POOL_WINDOWS[gi])], axis=0))
        lead = _window_sums(jnp.concatenate(e_parts, axis=1), forward=True)
        for gi in range(len(POOL_WINDOWS)):
            cols = slice(gi * GROUP_DIM, (gi + 1) * GROUP_DIM)
            dpu_ref[:, cols] = (lead[gi][0:tm, :] - dd_parts[gi]).astype(BF16)

    act = jax.ShapeDtypeStruct((s, 512), BF16)
    prev_halo = pl.BlockSpec((HALO, 512), lambda i: (jnp.maximum(i * hb - 1, 0), 0))
    next_halo = pl.BlockSpec((HALO, 512), lambda i: (jnp.minimum((i + 1) * hb, last_halo), 0))
    return pl.pallas_call(
        body, name="pool_bwd", grid=(n,),
        in_specs=[_row_spec(tm, 512), prev_halo, _row_spec(tm, 512), _row_spec(tm, 512), next_halo, next_halo,
                  _full_spec((4, GROUP_DIM, GROUP_DIM)), _full_spec((1, 512))],
        out_specs=[_row_spec(tm, 512), _row_spec(tm, 512), _full_spec((4, GROUP_DIM, GROUP_DIM)),
                   _full_spec((1, 512))],
        out_shape=[act, act, jax.ShapeDtypeStruct((4, GROUP_DIM, GROUP_DIM), F32),
                   jax.ShapeDtypeStruct((1, 512), F32)],
        compiler_params=_params(("arbitrary",)),
    )(pu, pu, pg, dmp, pg, dmp, pw, ps)


def _inproj_bwd_x(dsec, wt, x, g, dxo, exchange=None, scatter=None, tm=512):
    s = x.shape[0]
    n_steps = s // tm
    carried = list(exchange or scatter or ())
    n_carry = len(carried)

    def body(*refs):
        d_refs = refs[0:7]
        wt_ref, x_ref, g_ref, dxo_ref = refs[7:11]
        carry_src = refs[11:11 + n_carry]
        dxi_ref, dg_ref = refs[11 + n_carry:13 + n_carry]
        carry_dst = refs[13 + n_carry:13 + 2 * n_carry]
        sems = refs[13 + 2 * n_carry:]
        step = pl.program_id(0)

        def comm(phase):
            if exchange is not None:
                (_exchange_start if phase == 0 else _exchange_finish)(carry_src, carry_dst, None, sems)
            else:
                big = tuple(zip(carry_src[:2], carry_dst[:2]))
                (_scatter_start if phase == 0 else _scatter_finish)(big, (carry_src[2], carry_dst[2]), sems)

        @pl.when(step == 0)
        def _():
            dg_ref[...] = jnp.zeros_like(dg_ref)
            if n_carry:
                comm(0)

        dh = None
        for d_ref, (off, ncol) in zip(d_refs, SECTIONS):
            t = jnp.dot(d_ref[...], wt_ref[off:off + ncol, :], preferred_element_type=F32)
            dh = t if dh is None else dh + t
        xf = x_ref[...]
        r = lax.rsqrt(jnp.mean(xf * xf, axis=-1, keepdims=True) + RMS_EPS)
        xh = xf * r
        u = dh * g_ref[...]
        dxi_ref[...] = dxo_ref[...] + r * (u - xh * jnp.mean(xh * u, axis=-1, keepdims=True))
        dg_ref[...] += jnp.sum(dh * xh, axis=0, keepdims=True)

        if n_carry:
            @pl.when(step == n_steps - 1)
            def _():
                comm(1)

    if exchange is not None:
        name, carry_shapes, n_sems = "inproj_bwd_x_exchange", _exchange_shapes(*exchange), 3
    elif scatter is not None:
        name, n_sems = "inproj_bwd_x_scatter", SCATTER_SEMS
        carry_shapes = _scatter_shapes(*scatter[:2]) + [jax.ShapeDtypeStruct((3, SMALL_HALF, 128), F32)]
    else:
        name, carry_shapes, n_sems = "inproj_bwd_x", [], 0
    return pl.pallas_call(
        body, name=name, grid=(n_steps,),
        in_specs=[_row_spec(tm, ncol) for _, ncol in SECTIONS]
                 + [_full_spec((IN_COLS_PAD, D_MODEL)), _row_spec(tm, D_MODEL), _full_spec((1, D_MODEL)),
                    _row_spec(tm, D_MODEL)] + [ANY] * n_carry,
        out_specs=[_row_spec(tm, D_MODEL), _full_spec((1, D_MODEL))] + [ANY] * n_carry,
        out_shape=[jax.ShapeDtypeStruct((s, D_MODEL), F32), jax.ShapeDtypeStruct((1, D_MODEL), F32)] + carry_shapes,
        scratch_shapes=[pltpu.SemaphoreType.DMA((n_sems,))] * 2 if n_carry else [],
        compiler_params=_params(("arbitrary",)),
    )(*dsec, wt, x, g, dxo, *carried)


def _inproj_bwd_w(h, dsec, tm=512):
    s = h.shape[0]
    n_steps = s // tm

    def body(*refs):
        h_ref = refs[0]
        d_refs = refs[1:8]
        out_ref, gw_ref = refs[8:]
        step = pl.program_id(0)

        @pl.when(step == 0)
        def _():
            gw_ref[...] = jnp.zeros_like(gw_ref)

        hh = h_ref[...]
        for d_ref, (off, ncol) in zip(d_refs, SECTIONS):
            gw_ref[:, off:off + ncol] += lax.dot_general(hh, d_ref[...], TN, preferred_element_type=F32)

        @pl.when(step == n_steps - 1)
        def _():
            for k in range(N_CHIPS):
                out_ref[k] = gw_ref[:, k * SHARD_COLS:(k + 1) * SHARD_COLS]

    return pl.pallas_call(
        body, name="inproj_bwd_w", grid=(n_steps,),
        in_specs=[_row_spec(tm, D_MODEL)] + [_row_spec(tm, ncol) for _, ncol in SECTIONS],
        out_specs=_full_spec((N_CHIPS, D_MODEL, SHARD_COLS)),
        out_shape=jax.ShapeDtypeStruct((N_CHIPS, D_MODEL, SHARD_COLS), F32),
        scratch_shapes=[pltpu.VMEM((D_MODEL, IN_COLS_PAD), F32)],
        compiler_params=_params(("arbitrary",), vmem_mb=60),
    )(h, *dsec)


def _elementwise(fn, name, n_out, arrays, tm):
    shape = arrays[0].shape
    rest = shape[1:]
    n_in = len(arrays)

    def body(*refs):
        outs = fn(*[r[...] for r in refs[:n_in]])
        for r, val in zip(refs[n_in:], outs):
            r[...] = val

    spec = pl.BlockSpec((tm,) + rest, lambda i: (i,) + (0,) * len(rest))
    return pl.pallas_call(
        body, name=name, grid=(shape[0] // tm,),
        in_specs=[spec] * n_in, out_specs=[spec] * n_out, out_shape=[jax.ShapeDtypeStruct(shape, F32)] * n_out,
        compiler_params=_params(("parallel",)),
    )(*arrays)


def _add2(a, b, name, tm):
    return _elementwise(lambda p, q: (p + q,), name, 1, [a, b], tm)[0]


def _chip_sum(place, g, r, name, tm):
    n_layers, _, rows, cols = g.shape
    half = rows // 2
    nb = half // tm

    def body(place_ref, g_ref, r_ref, o_ref):
        o_ref[...] = (g_ref[...] + r_ref[...]).astype(BF16)

    blk = (None, None, tm, cols)
    return pl.pallas_call(
        body, name=name,
        grid_spec=pltpu.PrefetchScalarGridSpec(
            num_scalar_prefetch=1, grid=(n_layers, N_CHIPS, nb),
            in_specs=[pl.BlockSpec(blk, lambda l, k, i, p: (l, k, p[1] * nb + i, 0)),
                      pl.BlockSpec(blk, lambda l, k, i, p: (l, k, i, 0))],
            out_specs=pl.BlockSpec(blk, lambda l, k, i, p: (l, k, i, 0))),
        out_shape=jax.ShapeDtypeStruct((n_layers, N_CHIPS, half, cols), BF16),
        compiler_params=_params(("parallel", "parallel", "parallel")),
    )(place, g, r)


def _mesh_sum(place, g, r, got, into, layer, name, tm):
    _, _, rows, cols = g.shape
    half = rows // 2
    nb = half // tm

    def body(place_ref, g_ref, r_ref, got_ref, into_ref, o_ref):
        own = g_ref[...] + r_ref[...]
        o_ref[...] = (own + got_ref[0].astype(F32)) + (got_ref[1].astype(F32) + got_ref[2].astype(F32))

    blk = (None, None, tm, cols)
    return pl.pallas_call(
        body, name=name,
        grid_spec=pltpu.PrefetchScalarGridSpec(
            num_scalar_prefetch=1, grid=(nb,),
            in_specs=[pl.BlockSpec(blk, lambda i, p: (0, p[0], p[1] * nb + i, 0)),
                      pl.BlockSpec(blk, lambda i, p: (0, p[0], i, 0)),
                      pl.BlockSpec((None, 3, tm, cols), lambda i, p: (0, 0, i, 0)),
                      ANY],
            out_specs=pl.BlockSpec((None, tm, cols), lambda i, p: (layer, p[1] * nb + i, 0))),
        out_shape=jax.ShapeDtypeStruct(into.shape, F32),
        input_output_aliases={4: 0},
        compiler_params=_params(("parallel",)),
    )(place, g, r, got, into)


def _mesh_sum_small(place, ps, got, tm):
    nb = SMALL_HALF // tm

    def body(place_ref, ps_ref, got_ref, o_ref):
        o_ref[...] = (ps_ref[...] + got_ref[0]) + (got_ref[1] + got_ref[2])

    return pl.pallas_call(
        body, name="mesh_sum_small",
        grid_spec=pltpu.PrefetchScalarGridSpec(
            num_scalar_prefetch=1, grid=(nb,),
            in_specs=[pl.BlockSpec((tm, 128), lambda i, p: (p[1] * nb + i, 0)),
                      pl.BlockSpec((3, tm, 128), lambda i, p: (0, i, 0))],
            out_specs=pl.BlockSpec((tm, 128), lambda i, p: (p[1] * nb + i, 0))),
        out_shape=jax.ShapeDtypeStruct((SMALL_ROWS, 128), F32),
        compiler_params=_params(("parallel",)),
    )(place, ps, got)


def _adamw(g, w, m, v, name, tm):
    def fn(g, w, m, v):
        m = ADAM_B1 * m + (1.0 - ADAM_B1) * g
        v = ADAM_B2 * v + (1.0 - ADAM_B2) * (g * g)
        m_hat = m / (1.0 - ADAM_B1 ** ADAM_STEP)
        v_hat = v / (1.0 - ADAM_B2 ** ADAM_STEP)
        delta = -ADAM_LR * (m_hat / (jnp.sqrt(v_hat) + ADAM_EPS) + ADAM_WD * w)
        return g, delta, m, v

    return _elementwise(fn, name, 4, [g, w, m, v], tm)


def _position():
    x, y, c = lax.axis_index("x"), lax.axis_index("y"), lax.axis_index("c")
    other_chips = [(x, 1 - y), (1 - x, y), (1 - x, 1 - y)]
    return x, y, c, other_chips


def _remote(src, dst, sems, k, to):
    send_sems, recv_sems = sems
    return pltpu.make_async_remote_copy(src_ref=src, dst_ref=dst, send_sem=send_sems.at[k],
                                        recv_sem=recv_sems.at[k], device_id=to, device_id_type=MESH)


def _comm_call(body, name, n_in, out_shape, n_remote, aliases=None):
    return pl.pallas_call(
        body, name=name, in_specs=[ANY] * n_in, out_specs=[ANY] * len(out_shape), out_shape=out_shape,
        scratch_shapes=[pltpu.SemaphoreType.DMA((n_remote,)), pltpu.SemaphoreType.DMA((n_remote,))],
        input_output_aliases=aliases or {},
    )


def _halves(ref_rows, c):
    half = ref_rows // 2
    return pl.ds(half * c, half), pl.ds(half * (1 - c), half)


def _gather_weights(wi, wo):
    def body(wi_ref, wo_ref, gi_ref, go_ref, send_sems, recv_sems):
        _gather_start((wi_ref, wo_ref), (gi_ref, go_ref), (send_sems, recv_sems))
        _gather_finish((wi_ref, wo_ref), (gi_ref, go_ref), (send_sems, recv_sems))

    return _comm_call(body, "gather_weights", 2, _gather_shapes(wi, wo), GATHER_SEMS)(wi, wo)


GATHER_SEMS = 12
SCATTER_SEMS = 9


def _gather_shapes(wi, wo):
    return [jax.ShapeDtypeStruct((a.shape[0], N_CHIPS) + a.shape[1:], a.dtype) for a in (wi, wo)]


def _gather_sends(srcs, dsts, sems):
    x, y, c, chips = _position()
    out = []
    for k, (cx, cy) in enumerate(chips):
        for a, (src, dst) in enumerate(zip(srcs, dsts)):
            mine, _ = _halves(src.shape[1], c)
            out.append(_remote(src.at[:, mine], dst.at[:, k + 1, mine], sems, len(srcs) * k + a, (cx, cy, c)))
    return out


def _gather_own(srcs, dsts, local_sems):
    return [pltpu.make_async_copy(src, dst.at[:, 0], local_sems.at[a]) for a, (src, dst) in enumerate(zip(srcs, dsts))]


def _gather_start(srcs, dsts, sems):
    for cp in _gather_sends(srcs, dsts, sems):
        cp.start()


def _gather_finish(srcs, dsts, sems):
    x, y, c, _ = _position()
    n = len(srcs)
    forwards = []
    for k in range(3):
        for a, dst in enumerate(dsts):
            mine, _ = _halves(dst.shape[2], c)
            blk = dst.at[:, k + 1, mine]
            _remote(blk, blk, sems, n * k + a, (x, y, c)).wait_recv()
            forwards.append(_remote(blk, blk, sems, 3 * n + n * k + a, (x, y, 1 - c)))
            forwards[-1].start()
    for k in range(3):
        for a, dst in enumerate(dsts):
            _, other = _halves(dst.shape[2], c)
            blk = dst.at[:, k + 1, other]
            _remote(blk, blk, sems, 3 * n + n * k + a, (x, y, c)).wait_recv()
    for cp in _gather_sends(srcs, dsts, sems) + forwards:
        cp.wait_send()


def _scatter_sends(big, small, sems):
    x, y, c, chips = _position()
    n = len(big) + (small is not None)
    out = []
    for k, (cx, cy) in enumerate(chips):
        ck = 2 * cx + cy
        for a, (src, dst) in enumerate(big):
            out.append(_remote(src.at[:, ck], dst.at[:, k], sems, n * k + a, (cx, cy, c)))
        if small is not None:
            half, _ = _halves(SMALL_ROWS, c)
            out.append(_remote(small[0].at[half], small[1].at[k], sems, n * k + n - 1, (cx, cy, c)))
    return out


def _scatter_start(big, small, sems):
    for cp in _scatter_sends(big, small, sems):
        cp.start()


def _scatter_finish(big, small, sems):
    x, y, c, _ = _position()
    n = len(big) + (small is not None)
    for k in range(3):
        landing = [dst.at[:, k] for _, dst in big] + ([small[1].at[k]] if small is not None else [])
        for a, blk in enumerate(landing):
            _remote(blk, blk, sems, n * k + a, (x, y, c)).wait_recv()
    for cp in _scatter_sends(big, small, sems):
        cp.wait_send()


def _exchange_shapes(gi, go):
    return [jax.ShapeDtypeStruct(a.shape[:2] + (a.shape[2] // 2, a.shape[3]), F32) for a in (gi, go)]


def _exchange_copies(srcs, dsts, small, sems):
    x, y, c, _ = _position()
    cps = [_remote(src.at[:, :, _halves(src.shape[2], c)[1]], dst, sems, a, (x, y, 1 - c))
           for a, (src, dst) in enumerate(zip(srcs, dsts))]
    if small is not None:
        cps.append(_remote(small[0], small[1], sems, len(srcs), (x, y, 1 - c)))
    return cps


def _exchange_start(srcs, dsts, small, sems):
    for cp in _exchange_copies(srcs, dsts, small, sems):
        cp.start()


def _exchange_finish(srcs, dsts, small, sems):
    for cp in _exchange_copies(srcs, dsts, small, sems):
        cp.wait()


def _exchange_sibling(gi, go, sm):
    def body(gi_ref, go_ref, sm_ref, ri_ref, ro_ref, rs_ref, send_sems, recv_sems):
        args = ((gi_ref, go_ref), (ri_ref, ro_ref), (sm_ref, rs_ref), (send_sems, recv_sems))
        _exchange_start(*args)
        _exchange_finish(*args)

    out_shape = _exchange_shapes(gi, go) + [jax.ShapeDtypeStruct(sm.shape, F32)]
    return _comm_call(body, "exchange_sibling", 3, out_shape, 3)(gi, go, sm)


def _scatter_shapes(pi, po):
    return [jax.ShapeDtypeStruct((a.shape[0], 3) + a.shape[2:], a.dtype) for a in (pi, po)]


def _join_sibling(fi, fo, fs, late):
    def body(fi_ref, fo_ref, fs_ref, late_ref, oi_ref, oo_ref, os_ref, sum_ref, send_sems, recv_sems,
             sib_ref, chip_ref, got_ref):
        x, y, c, chips = _position()
        sems = (send_sems, recv_sems)
        sent, got = [], []
        for a, (src, dst) in enumerate(((fi_ref, oi_ref), (fo_ref, oo_ref))):
            mine, other = _halves(src.shape[1], c)
            sent.append(_remote(src.at[:, mine], dst.at[:, mine], sems, a, (x, y, 1 - c)))
            got.append(_remote(dst.at[:, other], dst.at[:, other], sems, a, (x, y, c)))
        mine, other = _halves(SMALL_ROWS, c)
        sent.append(_remote(fs_ref.at[mine], os_ref.at[mine], sems, 2, (x, y, 1 - c)))
        got.append(_remote(os_ref.at[other], os_ref.at[other], sems, 2, (x, y, c)))
        for cp in sent:
            cp.start()

        to_sibling = _remote(late_ref, sib_ref, sems, 3, (x, y, 1 - c))
        to_sibling.start()
        to_sibling.wait()
        chip_ref[...] = late_ref[...] + sib_ref[...]
        to_chips = [_remote(chip_ref, got_ref.at[k], sems, 4 + k, (cx, cy, c)) for k, (cx, cy) in enumerate(chips)]
        for cp in to_chips:
            cp.start()
        for cp in to_chips:
            cp.wait()
        sum_ref[...] = (chip_ref[...] + got_ref[0]) + (got_ref[1] + got_ref[2])

        for cp in got:
            cp.wait_recv()
        for cp in sent:
            cp.wait_send()

    vmem = pl.BlockSpec(memory_space=pltpu.VMEM)
    return pl.pallas_call(
        body, name="join_sibling", in_specs=[ANY] * 3 + [vmem], out_specs=[ANY] * 3 + [vmem],
        out_shape=[jax.ShapeDtypeStruct(a.shape, F32) for a in (fi, fo, fs, late)],
        scratch_shapes=[pltpu.SemaphoreType.DMA((7,)), pltpu.SemaphoreType.DMA((7,)),
                        pltpu.VMEM(late.shape, F32), pltpu.VMEM(late.shape, F32), pltpu.VMEM((3,) + late.shape, F32)],
        input_output_aliases={0: 0, 1: 1, 2: 2},
    )(fi, fo, fs, late)


SMALL_SIZES = (DEPTH * D_MODEL, D_MODEL, DEPTH * POOL_WIDTH, 128, DEPTH * 4 * GROUP_DIM * GROUP_DIM)
LOSS_SLOT = DEPTH * D_MODEL + D_MODEL + DEPTH * POOL_WIDTH + DEPTH * 8


def _pack_small(norm_g, final_g, pool_scale, forget_bias, pool_w, loss=None):
    fb = forget_bias.reshape(-1)
    if loss is not None:
        fb = jnp.concatenate([fb, loss.reshape(1)])
    fb = jnp.pad(fb, (0, 128 - fb.size))
    flat = jnp.concatenate([norm_g.reshape(-1), final_g.reshape(-1), pool_scale.reshape(-1), fb,
                            pool_w.reshape(-1)])
    return jnp.pad(flat, (0, SMALL_ROWS * 128 - flat.size)).reshape(SMALL_ROWS, 128)


def _unpack_small(packed):
    flat = packed.reshape(-1)
    offs = [0]
    for n in SMALL_SIZES:
        offs.append(offs[-1] + n)
    norm_g, final_g, pool_scale, fb, pool_w = [flat[offs[i]:offs[i + 1]] for i in range(5)]
    return (norm_g.reshape(DEPTH, D_MODEL), final_g, pool_scale.reshape(DEPTH, POOL_WIDTH),
            fb[:DEPTH * 8].reshape(DEPTH, 8), pool_w.reshape(DEPTH, 4, GROUP_DIM, GROUP_DIM))


def kernel(x, norm_g, w_in, forget_bias, pool_w, pool_scale, w_out, final_g, loss_target, m_norm_g, m_w_in, m_forget_bias, m_pool_w, m_pool_scale, m_w_out, m_final_g, v_norm_g, v_w_in, v_forget_bias, v_pool_w, v_pool_scale, v_w_out, v_final_g):
    x0 = x[0]
    tgt = loss_target[0]
    s = x0.shape[0]

    me = 2 * lax.axis_index("x") + lax.axis_index("y")
    place = jnp.stack([me, lax.axis_index("c")]).astype(jnp.int32)
    wi_b, wo_b = w_in.astype(BF16), w_out.astype(BF16)

    def chip_piece(own, got, chip):
        rel = jnp.bitwise_xor(chip, me)
        if own is None:
            return lax.dynamic_index_in_dim(got, rel, axis=1, keepdims=False)[0]
        other = lax.dynamic_index_in_dim(got, jnp.maximum(rel, 1), axis=1, keepdims=False)
        return jnp.where(rel == 0, own, other)[0]

    def whole_w_out(l, go, own_gathered):
        own_o = None if own_gathered else wo_b[l:l + 1]
        return jnp.concatenate([chip_piece(own_o, go, k) for k in range(N_CHIPS)], axis=0)

    pool_w_b = pool_w.astype(BF16)
    fb_pad = jnp.pad(forget_bias, ((0, 0), (0, 128 - forget_bias.shape[1])))
    head_sel = (jnp.arange(ATTN_WIDTH)[None, :] // HEAD_DIM == jnp.arange(8)[:, None]).astype(BF16)

    saved, weights = [], []
    xl = x0
    gathered = _gather_weights(wi_b[0:1], wo_b[0:1])
    for l in range(DEPTH):
        w_o = whole_w_out(l, gathered[1], own_gathered=l > 0)
        pu, pg, q, k, v, ag, lf, c, h, po, w_t = _inproj_fwd(
            place, xl, norm_g[l][None], None if l > 0 else wi_b[0], gathered[0], fb_pad[l][None],
            pool_w_b[l], pool_scale[l][None])
        weights.append((w_t, w_o))
        if l + 1 < DEPTH:
            ma, o, lse, *gathered = _attn_fwd(q, k, v, c, ag, carry=(wi_b[l + 1:l + 2], wo_b[l + 1:l + 2]))
        else:
            ma, o, lse = _attn_fwd(q, k, v, c, ag)
        saved.append((xl, h, pu, pg, q, k, v, ag, lf, c, po, ma, o, lse))
        if l + 1 < DEPTH:
            xl = _outproj_fwd(xl, po, ma, w_o)
    dx, g_final, loss_part = _loss_head(xl, po, ma, w_o, tgt, final_g[None])

    g_norm, g_fb, g_pw, g_ps = [], [], [], []
    parts = {}
    pending = None
    for l in reversed(range(DEPTH)):
        xl, h, pu, pg, q, k, v, ag, lf, c, po, ma, o, lse = saved[l]
        w_t, w_o = weights[l]
        dmp, do, dag, dt, gwo = _outproj_bwd(dx, po, ma, ag, o, w_o, head_sel)
        if pending is None:
            dq, dk, dv, dcs, drs = _attn_bwd(q, k, v, do, c, lse, dt.reshape(N_PAIRS, 2, s))
        else:
            dq, dk, dv, dcs, drs, qi, qo = _attn_bwd(q, k, v, do, c, lse, dt.reshape(N_PAIRS, 2, s), carry=pending)
            parts[l + 1] += (qi, qo)
        drs = jnp.pad(drs[:, 0:2, :].reshape(8, s).T, ((0, 0), (0, 120)))
        dpf, dfb = _forget_bwd(dcs, drs, lf)
        dpu, dpg, gpw, gps = _pool_bwd(pu, pg, dmp, pool_w_b[l], pool_scale[l][None])
        dsec = (dpu, dpg, dq, dk, dv, dag, dpf)
        gwi = _inproj_bwd_w(h, dsec)
        g_fb.append(dfb[0, :8])
        g_pw.append(gpw)
        g_ps.append(gps[0])
        send_i = gwi[None]
        send_o = gwo.reshape(1, N_CHIPS, SHARD_ROWS, D_MODEL)
        if l > 0:
            dx, gn, ri, ro = _inproj_bwd_x(dsec, w_t, xl, norm_g[l][None], dx, exchange=(send_i, send_o))
            g_norm.append(gn[0])
        else:
            g_norm.append(jnp.zeros((D_MODEL,), F32))
            g_norm, g_fb, g_pw, g_ps = [t[::-1] for t in (g_norm, g_fb, g_pw, g_ps)]
            send_s = _pack_small(jnp.stack(g_norm), g_final[0], jnp.stack(g_ps), jnp.stack(g_fb), jnp.stack(g_pw),
                                 loss=loss_part[0, 0])
            ri, ro, rs = _exchange_sibling(send_i, send_o, send_s)
        pending = (_chip_sum(place, send_i, ri, "chip_sum_w_in", 256),
                   _chip_sum(place, send_o, ro, "chip_sum_w_out", SHARD_ROWS // 2))
        parts[l] = (send_i, ri, send_o, ro)
    ps_ = _add2(send_s, rs, "chip_sum_small", SMALL_ROWS // 4)
    dx, gn0, qi, qo, qs = _inproj_bwd_x(dsec, w_t, xl, norm_g[0][None], dx, scatter=(*pending, ps_))
    parts[0] += (qi, qo)
    fi = lax.empty((DEPTH, D_MODEL, SHARD_COLS), F32)
    fo = lax.empty((DEPTH, SHARD_ROWS, D_MODEL), F32)
    for l in range(DEPTH):
        send_i, ri, send_o, ro, qi, qo = parts[l]
        fi = _mesh_sum(place, send_i, ri, qi, fi, l, "mesh_sum_w_in", 256)
        fo = _mesh_sum(place, send_o, ro, qo, fo, l, "mesh_sum_w_out", SHARD_ROWS // 2)
    hs = _mesh_sum_small(place, ps_, qs, SMALL_HALF // 4)
    fi, fo, fs, gn0 = _join_sibling(fi, fo, hs, gn0.reshape(8, 128))
    fs = jnp.concatenate([fs[0:8] + gn0, fs[8:]], axis=0)

    col_major = lambda a: jnp.transpose(a, (2, 0, 1))
    flat_o = lambda a: a.reshape(-1, D_MODEL)
    out_i = _adamw(col_major(fi), col_major(w_in), col_major(m_w_in), col_major(v_w_in), "adamw_w_in", 77)
    out_i = [jnp.transpose(t, (1, 2, 0)) for t in out_i]
    out_o = _adamw(flat_o(fo), flat_o(w_out), flat_o(m_w_out), flat_o(v_w_out), "adamw_w_out", 256)
    out_s = _adamw(fs, _pack_small(norm_g, final_g, pool_scale, forget_bias, pool_w),
                   _pack_small(m_norm_g, m_final_g, m_pool_scale, m_forget_bias, m_pool_w),
                   _pack_small(v_norm_g, v_final_g, v_pool_scale, v_forget_bias, v_pool_w),
                   "adamw_small", SMALL_ROWS // 4)

    loss = out_s[0].reshape(-1)[LOSS_SLOT]
    groups = []
    for oi, oo, os_ in zip(out_i, out_o, out_s):
        sn, sf, sp, sb, sw = _unpack_small(os_)
        groups.append((sn, oi.reshape(w_in.shape), sb, sw, sp, oo.reshape(w_out.shape), sf))
    return (loss, dx[None]) + tuple(t for grp in groups for t in grp)
```

```python
import jax
import jax.numpy as jnp
from jax import lax
from jax.experimental import pallas as pl
from jax.experimental.pallas import tpu as pltpu

F32 = jnp.float32
BF16 = jnp.bfloat16

D_MODEL = 1024
DEPTH = 4
POOL_WIDTH = 512
ATTN_WIDTH = 512
HEAD_DIM = 64
PAIR = 2 * HEAD_DIM
N_PAIRS = ATTN_WIDTH // PAIR
POOL_WINDOWS = (2, 4, 8, 16)
GROUP_DIM = 128
HALO = 16
IN_COLS = 3080
OFF_F = 3072
IN_COLS_PAD = 3200
SECTIONS = ((0, 512), (512, 512), (1024, 512), (1536, 512), (2048, 512), (2560, 512), (OFF_F, 128))
N_CHIPS = 4
SHARD_COLS = IN_COLS // N_CHIPS
SHARD_ROWS = D_MODEL // N_CHIPS
RMS_EPS = 1e-6
NEG = -1e30
Q_SCALE = 0.125

ADAM_LR = 0.001
ADAM_B1 = 0.9
ADAM_B2 = 0.999
ADAM_EPS = 1e-08
ADAM_WD = 0.01
ADAM_STEP = 10

SMALL_ROWS = 2112
SMALL_HALF = SMALL_ROWS // 2

NT = (((1,), (1,)), ((), ()))
TN = (((0,), (0,)), ((), ()))
MESH = pl.DeviceIdType.MESH
ANY = pl.BlockSpec(memory_space=pl.ANY)


def _params(semantics, vmem_mb=48):
    return pltpu.CompilerParams(dimension_semantics=semantics, vmem_limit_bytes=vmem_mb << 20)


def _row_spec(tm, cols):
    return pl.BlockSpec((tm, cols), lambda i: (i, 0))


def _full_spec(shape):
    return pl.BlockSpec(shape, lambda *_: (0,) * len(shape))


def _sigmoid(x):
    return jax.nn.sigmoid(x)


def _scan_rows(a, reverse=False):
    n = a.shape[0]
    row = lax.broadcasted_iota(jnp.int32, a.shape, 0)
    k = 1
    while k < n:
        if reverse:
            a = a + jnp.where(row < n - k, pltpu.roll(a, n - k, 0), 0.0)
        else:
            a = a + jnp.where(row >= k, pltpu.roll(a, k, 0), 0.0)
        k *= 2
    return a


def _inproj_fwd(place, x, g, own, got, fb, pw, ps, tm=512):
    s = x.shape[0]

    def body(place_ref, x_ref, g_ref, fb_ref, pw_ref, ps_ref, *refs):
        own_ref = refs[0] if own is not None else None
        (got_ref, pu_ref, pg_ref, q_ref, k_ref, v_ref, ag_ref, lf_ref, c_ref, h_ref, po_ref, wt_ref,
         carry_ref, halo_ref, w_ref, piece_ref, sem) = refs[own is not None:]
        step = pl.program_id(0)

        @pl.when(step == 0)
        def _():
            carry_ref[...] = jnp.zeros_like(carry_ref)
            halo_ref[...] = jnp.zeros_like(halo_ref)
            for chip in range(N_CHIPS):
                rel = jnp.bitwise_xor(chip, place_ref[0])

                def fetch(src):
                    cp = pltpu.make_async_copy(src, piece_ref, sem)
                    cp.start()
                    cp.wait()

                if own_ref is None:
                    fetch(got_ref.at[0, rel])
                else:
                    pl.when(rel == 0)(lambda: fetch(own_ref))
                    pl.when(rel != 0)(lambda: fetch(got_ref.at[0, rel]))
                w_ref[:, chip * SHARD_COLS:(chip + 1) * SHARD_COLS] = piece_ref[...]
            w_ref[:, IN_COLS:] = jnp.zeros((D_MODEL, IN_COLS_PAD - IN_COLS), BF16)
            for t in range(IN_COLS_PAD // 128):
                cols = slice(t * 128, (t + 1) * 128)
                wt_ref[cols, :] = jnp.transpose(w_ref[:, cols].astype(F32)).astype(BF16)

        xf = x_ref[...]
        r = lax.rsqrt(jnp.mean(xf * xf, axis=-1, keepdims=True) + RMS_EPS)
        h = (xf * r * g_ref[...]).astype(BF16)
        h_ref[...] = h

        def proj(sec):
            off, n = SECTIONS[sec]
            return jnp.dot(h, w_ref[:, off:off + n], preferred_element_type=F32)

        pu_ref[...] = proj(0).astype(BF16)
        pg_ref[...] = proj(1).astype(BF16)
        u = pu_ref[...].astype(F32)
        d = _pool_delta(u, halo_ref[...], step * tm)
        halo_ref[...] = u[tm - HALO:, :]
        for gi in range(len(POOL_WINDOWS)):
            cols = slice(gi * GROUP_DIM, (gi + 1) * GROUP_DIM)
            zg = jnp.dot(d[gi].astype(BF16), pw_ref[gi], preferred_element_type=F32)
            gate = pg_ref[:, cols].astype(F32)
            po_ref[:, cols] = (zg * ps_ref[:, cols] * (gate * _sigmoid(gate))).astype(BF16)
        q_ref[...] = (proj(2) * Q_SCALE).astype(BF16)
        k_ref[...] = proj(3).astype(BF16)
        v_ref[...] = proj(4).astype(BF16)
        ag_ref[...] = proj(5).astype(BF16)
        z = proj(6) + fb_ref[...]
        lf = jnp.minimum(z, 0.0) - jnp.log(1.0 + jnp.exp(-jnp.abs(z)))
        lf_ref[...] = lf
        c_ref[...] = _scan_rows(lf) + carry_ref[0:1, :]
        carry_ref[0:1, :] = c_ref[tm - 1:tm, :]

    act = jax.ShapeDtypeStruct((s, 512), BF16)
    rows = lambda cols: pl.BlockSpec((tm, cols), lambda i, p: (i, 0))
    whole = lambda shape: pl.BlockSpec(shape, lambda i, p: (0,) * len(shape))
    shards = ([] if own is None else [own]) + [got]
    return pl.pallas_call(
        body, name="inproj_fwd" if own is None else "inproj_fwd_first",
        grid_spec=pltpu.PrefetchScalarGridSpec(
            num_scalar_prefetch=1, grid=(s // tm,),
            in_specs=[rows(D_MODEL), whole((1, D_MODEL)), whole((1, 128)), whole((4, GROUP_DIM, GROUP_DIM)),
                      whole((1, 512))] + [ANY] * len(shards),
            out_specs=[rows(512)] * 6 + [rows(128), rows(128), rows(D_MODEL), rows(512),
                                         whole((IN_COLS_PAD, D_MODEL))],
            scratch_shapes=[pltpu.VMEM((8, 128), F32), pltpu.VMEM((HALO, 512), F32),
                            pltpu.VMEM((D_MODEL, IN_COLS_PAD), BF16),
                            pltpu.VMEM((D_MODEL, SHARD_COLS), BF16), pltpu.SemaphoreType.DMA]),
        out_shape=[act] * 6 + [jax.ShapeDtypeStruct((s, 128), F32), jax.ShapeDtypeStruct((s, 128), F32),
                               jax.ShapeDtypeStruct((s, D_MODEL), BF16), act,
                               jax.ShapeDtypeStruct((IN_COLS_PAD, D_MODEL), BF16)],
        compiler_params=_params(("arbitrary",), vmem_mb=56),
    )(place, x, g, fb, pw, ps, *shards)


def _window_sums(ext, forward):
    n = ext.shape[0]
    outs = []
    for gi, w in enumerate(POOL_WINDOWS):
        a = ext[:, gi * GROUP_DIM:(gi + 1) * GROUP_DIM]
        k = 1
        while k < w:
            a = a + pltpu.roll(a, (n - k) if forward else k, 0)
            k *= 2
        outs.append(a)
    return outs


def _over_count(a, row0, gi):
    w = POOL_WINDOWS[gi]
    t = row0 + lax.broadcasted_iota(jnp.int32, (HALO, GROUP_DIM), 0)
    head = a[0:HALO, :] / jnp.minimum(t + 1, w).astype(F32)
    return jnp.concatenate([head, a[HALO:, :] * (1.0 / w)], axis=0)


def _pool_delta(u, halo, row0):
    sums = _window_sums(jnp.concatenate([halo, u], axis=0), forward=False)
    return [_over_count(sums[gi][HALO:, :], row0, gi) - u[:, gi * GROUP_DIM:(gi + 1) * GROUP_DIM]
            for gi in range(len(POOL_WINDOWS))]


def _split3(x):
    hi = x.astype(BF16).astype(F32)
    r1 = x - hi
    lo = r1.astype(BF16).astype(F32)
    return hi, lo, (r1 - lo).astype(BF16).astype(F32)


def _own(idx, h):
    return (idx < HEAD_DIM) if h == 0 else (idx >= HEAD_DIM)


def _spare(h):
    return HEAD_DIM if h == 0 else 0


def _attn_fwd(q, k, v, c, ag, carry=None, tq=512, tk=512, chunk=512):
    s = q.shape[0]
    n_carry = 0 if carry is None else len(carry)

    def body(*refs):
        q_ref, k_ref, v_ref, c_ref, ag_ref = refs[:5]
        carry_src = refs[5:5 + n_carry]
        ma_ref, o_ref, lse_ref = refs[5 + n_carry:8 + n_carry]
        carry_dst = refs[8 + n_carry:8 + 2 * n_carry]
        kx_ref, vt_ref, acc0_ref, acc1_ref, sta_ref, stb_ref = refs[8 + 2 * n_carry:14 + 2 * n_carry]
        carry_sems = refs[14 + 2 * n_carry:]
        j = pl.program_id(0)
        i = pl.program_id(1)

        if n_carry:
            @pl.when((j == 0) & (i == 0))
            def _():
                _gather_start(carry_src, carry_dst, carry_sems[:2])
                for cp in _gather_own(carry_src, carry_dst, carry_sems[2]):
                    cp.start()

        @pl.when(i == 0)
        def _():
            def prep(ch, carry):
                rows = pl.ds(pl.multiple_of(ch * chunk, chunk), chunk)
                kk = k_ref[rows, :].astype(F32)
                cc = c_ref[rows, :]
                vt = jnp.transpose(v_ref[rows, :].astype(F32))
                lane = lax.broadcasted_iota(jnp.int32, (chunk, PAIR), 1)
                sub = lax.broadcasted_iota(jnp.int32, (PAIR, chunk), 0)
                for h in range(2):
                    ccol = jnp.sum(jnp.where(lane == 2 * j + h, cc, 0.0), axis=1, keepdims=True)
                    kx = jnp.where(_own(lane, h), kk, 0.0)
                    for t, part in enumerate(_split3(-ccol)):
                        kx = jnp.where(lane == _spare(h) + t, part, kx)
                    kx_ref[h, rows, :] = kx.astype(BF16)
                    vt_ref[h, :, rows] = jnp.where(_own(sub, h), vt,
                                                   jnp.where(sub == _spare(h), 1.0, 0.0)).astype(BF16)
                return carry

            lax.fori_loop(0, s // chunk, prep, 0)

        qq = q_ref[...].astype(F32)
        lane_q = lax.broadcasted_iota(jnp.int32, (tq, PAIR), 1)
        qx = []
        for h in range(2):
            ones = (lane_q >= _spare(h)) & (lane_q < _spare(h) + 3)
            qx.append(jnp.transpose(jnp.where(_own(lane_q, h), qq, jnp.where(ones, 1.0, 0.0))).astype(BF16))
        accs = (acc0_ref, acc1_ref)
        for acc in accs:
            acc[...] = jnp.zeros_like(acc)

        def rows_of(kb):
            return pl.ds(pl.multiple_of(kb * tk, tk), tk)

        def scores(kb, dst):
            for h in range(2):
                dst[h] = jnp.dot(kx_ref[h, rows_of(kb), :], qx[h], preferred_element_type=F32)

        def consume(kb, src, m, masked):
            m_out, alpha, pv = [], [], []
            for h in range(2):
                sh = src[h]
                if masked:
                    kpos = kb * tk + lax.broadcasted_iota(jnp.int32, (tk, tq), 0)
                    qpos = i * tq + lax.broadcasted_iota(jnp.int32, (tk, tq), 1)
                    sh = jnp.where(qpos >= kpos, sh, NEG)
                m_new = jnp.maximum(m[h], jnp.max(sh, axis=0, keepdims=True))
                pt = jnp.exp(sh - m_new).astype(BF16)
                alpha.append(jnp.exp(m[h] - m_new))
                pv.append(jnp.dot(vt_ref[h, :, rows_of(kb)], pt, preferred_element_type=F32))
                m_out.append(m_new)
            for h in range(2):
                accs[h][...] = accs[h][...] * alpha[h] + pv[h]
            return tuple(m_out)

        n_full = (i * tq) // tk

        def two_blocks(t, m):
            kb = 2 * t
            scores(kb + 1, stb_ref)
            m = consume(kb, sta_ref, m, False)
            scores(kb + 2, sta_ref)
            return consume(kb + 1, stb_ref, m, False)

        def odd_tail(m):
            scores(n_full, stb_ref)
            m = consume(n_full - 1, sta_ref, m, False)
            return consume(n_full, stb_ref, m, True)

        def even_tail(m):
            return consume(n_full, sta_ref, m, True)

        scores(0, sta_ref)
        m0 = jnp.full((1, tq), NEG, F32)
        m = lax.fori_loop(0, n_full // 2, two_blocks, (m0, m0))
        m = lax.cond(lax.rem(n_full, 2) == 1, odd_tail, even_tail, m)

        sub_o = lax.broadcasted_iota(jnp.int32, (PAIR, tq), 0)
        den = [accs[h][_spare(h):_spare(h) + 1, :] for h in range(2)]
        ot = jnp.where(sub_o < HEAD_DIM, acc0_ref[...] / den[0], acc1_ref[...] / den[1])
        o = jnp.transpose(ot)
        o_ref[...] = o.astype(BF16)
        gate = ag_ref[...].astype(F32)
        ma_ref[...] = (o * (gate * _sigmoid(gate))).astype(BF16)
        sub8 = lax.broadcasted_iota(jnp.int32, (8, tq), 0)
        lse_ref[...] = jnp.where(sub8 == 0, m[0] + jnp.log(den[0]), m[1] + jnp.log(den[1]))

        if n_carry:
            @pl.when((j == N_PAIRS - 1) & (i == s // tq - 1))
            def _():
                _gather_finish(carry_src, carry_dst, carry_sems[:2])
                for cp in _gather_own(carry_src, carry_dst, carry_sems[2]):
                    cp.wait()

    carried = [] if carry is None else list(carry)
    carry_sem = ([pltpu.SemaphoreType.DMA((GATHER_SEMS * n_carry,))] * 2
                 + [pltpu.SemaphoreType.DMA((n_carry,))]) if n_carry else []
    return pl.pallas_call(
        body, name="attn_fwd_gather" if n_carry else "attn_fwd", grid=(N_PAIRS, s // tq),
        in_specs=[pl.BlockSpec((tq, PAIR), lambda j, i: (i, j)),
                  pl.BlockSpec((s, PAIR), lambda j, i: (0, j)),
                  pl.BlockSpec((s, PAIR), lambda j, i: (0, j)),
                  pl.BlockSpec((s, 128), lambda j, i: (0, 0)),
                  pl.BlockSpec((tq, PAIR), lambda j, i: (i, j))] + [ANY] * n_carry,
        out_specs=[pl.BlockSpec((tq, PAIR), lambda j, i: (i, j)),
                   pl.BlockSpec((tq, PAIR), lambda j, i: (i, j)),
                   pl.BlockSpec((None, 8, tq), lambda j, i: (j, 0, i))] + [ANY] * n_carry,
        out_shape=[jax.ShapeDtypeStruct((s, ATTN_WIDTH), BF16), jax.ShapeDtypeStruct((s, ATTN_WIDTH), BF16),
                   jax.ShapeDtypeStruct((N_PAIRS, 8, s), F32)] + (_gather_shapes(*carried) if n_carry else []),
        scratch_shapes=[pltpu.VMEM((2, s, PAIR), BF16), pltpu.VMEM((2, PAIR, s), BF16),
                        pltpu.VMEM((PAIR, tq), F32), pltpu.VMEM((PAIR, tq), F32),
                        pltpu.VMEM((2, tk, tq), F32), pltpu.VMEM((2, tk, tq), F32)] + carry_sem,
        compiler_params=_params(("arbitrary", "arbitrary")),
    )(q, k, v, c, ag, *carried)


def _outproj_fwd(x, po, ma, wo, tm=512):
    s = x.shape[0]

    def body(x_ref, po_ref, ma_ref, wo_ref, xn_ref):
        xn_ref[...] = (x_ref[...]
                       + jnp.dot(po_ref[...], wo_ref[0:POOL_WIDTH, :], preferred_element_type=F32)
                       + jnp.dot(ma_ref[...], wo_ref[POOL_WIDTH:, :], preferred_element_type=F32))

    return pl.pallas_call(
        body, name="outproj_fwd", grid=(s // tm,),
        in_specs=[_row_spec(tm, D_MODEL), _row_spec(tm, 512), _row_spec(tm, 512),
                  _full_spec((D_MODEL, D_MODEL))],
        out_specs=_row_spec(tm, D_MODEL),
        out_shape=jax.ShapeDtypeStruct((s, D_MODEL), F32),
        compiler_params=_params(("parallel",)),
    )(x, po, ma, wo)


def _loss_head(x, po, ma, wo, tgt, g, tm=512):
    s = x.shape[0]

    def body(x_ref, po_ref, ma_ref, wo_ref, t_ref, g_ref, dx_ref, dg_ref, loss_ref):
        @pl.when(pl.program_id(0) == 0)
        def _():
            dg_ref[...] = jnp.zeros_like(dg_ref)
            loss_ref[...] = jnp.zeros_like(loss_ref)

        xf = (x_ref[...]
              + jnp.dot(po_ref[...], wo_ref[0:POOL_WIDTH, :], preferred_element_type=F32)
              + jnp.dot(ma_ref[...], wo_ref[POOL_WIDTH:, :], preferred_element_type=F32))
        r = lax.rsqrt(jnp.mean(xf * xf, axis=-1, keepdims=True) + RMS_EPS)
        xh = xf * r
        gg = g_ref[...]
        e = xh * gg - t_ref[...]
        loss_ref[...] += 0.5 * jnp.sum(jnp.mean(e * e, axis=-1, keepdims=True))
        dy = e * (1.0 / D_MODEL)
        u = dy * gg
        dx_ref[...] = r * (u - xh * jnp.mean(xh * u, axis=-1, keepdims=True))
        dg_ref[...] += jnp.sum(dy * xh, axis=0, keepdims=True)

    return pl.pallas_call(
        body, name="loss_head", grid=(s // tm,),
        in_specs=[_row_spec(tm, D_MODEL), _row_spec(tm, 512), _row_spec(tm, 512), _full_spec((D_MODEL, D_MODEL)),
                  _row_spec(tm, D_MODEL), _full_spec((1, D_MODEL))],
        out_specs=[_row_spec(tm, D_MODEL), _full_spec((1, D_MODEL)), _full_spec((8, 128))],
        out_shape=[jax.ShapeDtypeStruct((s, D_MODEL), F32), jax.ShapeDtypeStruct((1, D_MODEL), F32),
                   jax.ShapeDtypeStruct((8, 128), F32)],
        compiler_params=_params(("arbitrary",)),
    )(x, po, ma, wo, tgt, g)


def _outproj_bwd(dx, po, ma, ag, o, wo, head_sel, tm=512):
    s = dx.shape[0]

    def body(dx_ref, po_ref, ma_ref, ag_ref, o_ref, wo_ref, sel_ref, dmp_ref, do_ref, dag_ref, dt_ref, gwo_ref):
        @pl.when(pl.program_id(0) == 0)
        def _():
            gwo_ref[...] = jnp.zeros_like(gwo_ref)

        dxb = dx_ref[...].astype(BF16)
        dm = lax.dot_general(dxb, wo_ref[...], NT, preferred_element_type=F32)
        dmp_ref[...] = dm[:, 0:POOL_WIDTH].astype(BF16)
        dma = dm[:, POOL_WIDTH:]
        gate = ag_ref[...].astype(F32)
        of = o_ref[...].astype(F32)
        sg = _sigmoid(gate)
        do = dma * (gate * sg)
        do_ref[...] = do.astype(BF16)
        dag_ref[...] = (dma * of * (sg * (1.0 + gate * (1.0 - sg)))).astype(BF16)
        prod = do * of
        hi = prod.astype(BF16)
        lo = (prod - hi.astype(F32)).astype(BF16)
        dt_ref[...] = (lax.dot_general(sel_ref[...], hi, NT, preferred_element_type=F32)
                       + lax.dot_general(sel_ref[...], lo, NT, preferred_element_type=F32))
        gwo_ref[0:POOL_WIDTH, :] += lax.dot_general(po_ref[...], dxb, TN, preferred_element_type=F32)
        gwo_ref[POOL_WIDTH:, :] += lax.dot_general(ma_ref[...], dxb, TN, preferred_element_type=F32)

    act = jax.ShapeDtypeStruct((s, 512), BF16)
    return pl.pallas_call(
        body, name="outproj_bwd", grid=(s // tm,),
        in_specs=[_row_spec(tm, D_MODEL)] + [_row_spec(tm, 512)] * 4
                 + [_full_spec((D_MODEL, D_MODEL)), _full_spec((8, ATTN_WIDTH))],
        out_specs=[_row_spec(tm, 512)] * 3 + [pl.BlockSpec((8, tm), lambda i: (0, i)),
                                              _full_spec((D_MODEL, D_MODEL))],
        out_shape=[act, act, act, jax.ShapeDtypeStruct((8, s), F32),
                   jax.ShapeDtypeStruct((D_MODEL, D_MODEL), F32)],
        compiler_params=_params(("arbitrary",)),
    )(dx, po, ma, ag, o, wo, head_sel)


def _attn_bwd(q, k, v, do, c, lse, dt, carry=None, tq=256, chunk=512):
    s = q.shape[0]
    tk = 2 * tq
    nq = s // tq
    n_carry = 0 if carry is None else len(carry)

    def body(*refs):
        q_ref, k_ref, v_ref, do_ref, c_ref, lse_ref, dt_ref = refs[:7]
        carry_src = refs[7:7 + n_carry]
        dq_ref, dk_ref, dv_ref, dcs_ref, drs_ref = refs[7 + n_carry:12 + n_carry]
        carry_dst = refs[12 + n_carry:12 + 2 * n_carry]
        (qxt_ref, doxt_ref, qm_ref, dox_ref, dqt0_ref, dqt1_ref, dk0_ref, dk1_ref, dvx_ref,
         kx_ref, vx_ref, ktm_ref, sa_ref, sb_ref) = refs[12 + 2 * n_carry:26 + 2 * n_carry]
        carry_sems = refs[26 + 2 * n_carry:]
        carry_big = tuple(zip(carry_src, carry_dst))
        j = pl.program_id(0)
        b = pl.program_id(1)
        dqts = (dqt0_ref, dqt1_ref)
        dks = (dk0_ref, dk1_ref)

        if n_carry:
            @pl.when((j == 0) & (b == 0))
            def _():
                _scatter_start(carry_big, None, carry_sems)

        @pl.when(b == 0)
        def _():
            def prep(ch, carry):
                rows = pl.ds(pl.multiple_of(ch * chunk, chunk), chunk)
                qq = q_ref[rows, :].astype(F32)
                dd = do_ref[rows, :].astype(F32)
                qt = jnp.transpose(qq)
                ddt = jnp.transpose(dd)
                lane = lax.broadcasted_iota(jnp.int32, (chunk, PAIR), 1)
                sub = lax.broadcasted_iota(jnp.int32, (PAIR, chunk), 0)
                for h in range(2):
                    sp = _spare(h)
                    qm_ref[h, rows, :] = jnp.where(_own(lane, h), qq, jnp.where(lane == sp, 1.0, 0.0)).astype(BF16)
                    dox_ref[h, rows, :] = jnp.where(_own(lane, h), dd, 0.0).astype(BF16)
                    qx = jnp.where(_own(sub, h), qt, jnp.where((sub >= sp) & (sub < sp + 3), 1.0, 0.0))
                    for t, part in enumerate(_split3(-lse_ref[pl.ds(h, 1), rows])):
                        qx = jnp.where(sub == sp + 3 + t, part, qx)
                    qxt_ref[h, :, rows] = qx.astype(BF16)
                    dx = jnp.where(_own(sub, h), ddt, 0.0)
                    for t, part in enumerate(_split3(-dt_ref[pl.ds(h, 1), rows])):
                        dx = jnp.where(sub == sp + t, part, dx)
                    doxt_ref[h, :, rows] = dx.astype(BF16)
                return carry

            lax.fori_loop(0, s // chunk, prep, 0)
            for ref in dqts:
                ref[...] = jnp.zeros_like(ref)

        kk = k_ref[...].astype(F32)
        vv = v_ref[...].astype(F32)
        cc = c_ref[...]
        kt = jnp.transpose(kk)
        lane = lax.broadcasted_iota(jnp.int32, (tk, PAIR), 1)
        sub = lax.broadcasted_iota(jnp.int32, (PAIR, tk), 0)
        for h in range(2):
            sp = _spare(h)
            ccol = jnp.sum(jnp.where(lane == 2 * j + h, cc, 0.0), axis=1, keepdims=True)
            kx = jnp.where(_own(lane, h), kk, jnp.where((lane >= sp + 3) & (lane < sp + 6), 1.0, 0.0))
            for t, part in enumerate(_split3(-ccol)):
                kx = jnp.where(lane == sp + t, part, kx)
            kx_ref[h] = kx.astype(BF16)
            vx_ref[h] = jnp.where(_own(lane, h), vv, jnp.where((lane >= sp) & (lane < sp + 3), 1.0, 0.0)).astype(BF16)
            ktm_ref[h] = jnp.where(_own(sub, h), kt, jnp.where(sub == sp, 1.0, 0.0)).astype(BF16)
        for ref in dks:
            ref[...] = jnp.zeros_like(ref)
        dvx_ref[...] = jnp.zeros_like(dvx_ref)

        def cols_of(i):
            return pl.ds(pl.multiple_of(i * tq, tq), tq)

        def scores(i, dst):
            cols = cols_of(jnp.minimum(i, nq - 1))
            for h in range(2):
                dst[h] = jnp.dot(kx_ref[h], qxt_ref[h, :, cols], preferred_element_type=F32)
                dst[2 + h] = jnp.dot(vx_ref[h], doxt_ref[h, :, cols], preferred_element_type=F32)

        def consume(i, src, masked):
            cols = cols_of(i)
            for h in range(2):
                arg = src[h]
                if masked:
                    kpos = b * tk + lax.broadcasted_iota(jnp.int32, (tk, tq), 0)
                    qpos = i * tq + lax.broadcasted_iota(jnp.int32, (tk, tq), 1)
                    arg = jnp.where(qpos >= kpos, arg, NEG)
                pt = jnp.exp(arg)
                dst = (pt * src[2 + h]).astype(BF16)
                dvx_ref[...] += jnp.dot(pt.astype(BF16), dox_ref[h, cols, :], preferred_element_type=F32)
                dks[h][...] += jnp.dot(dst, qm_ref[h, cols, :], preferred_element_type=F32)
                dqts[h][:, cols] += jnp.dot(ktm_ref[h], dst, preferred_element_type=F32)

        i0 = 2 * b

        def two_blocks(t, masked):
            i = i0 + 2 * t
            scores(i + 1, sb_ref)
            consume(i, sa_ref, masked)
            scores(i + 2, sa_ref)
            consume(i + 1, sb_ref, masked)

        def four_blocks(u, carry):
            two_blocks(1 + 2 * u, False)
            two_blocks(2 + 2 * u, False)
            return carry

        scores(i0, sa_ref)
        two_blocks(0, True)
        n_open = (nq - i0) // 2 - 1
        lax.fori_loop(0, n_open // 2, four_blocks, 0)

        @pl.when(lax.rem(n_open, 2) == 1)
        def _():
            two_blocks(n_open, False)

        dv_ref[...] = dvx_ref[...].astype(BF16)
        dk_ref[...] = jnp.where(lane < HEAD_DIM, dk0_ref[...], dk1_ref[...]).astype(BF16)
        dcs_ref[...] = jnp.where(lane == _spare(0), dk0_ref[...], jnp.where(lane == _spare(1), dk1_ref[...], 0.0))

        @pl.when(b == s // tk - 1)
        def _():
            sub8 = lax.broadcasted_iota(jnp.int32, (8, s), 0)
            drs_ref[...] = jnp.where(sub8 == 0, dqt0_ref[_spare(0):_spare(0) + 1, :],
                                     jnp.where(sub8 == 1, dqt1_ref[_spare(1):_spare(1) + 1, :], 0.0))
            sub_c = lax.broadcasted_iota(jnp.int32, (PAIR, chunk), 0)
            for ch in range(s // chunk):
                rows = pl.ds(ch * chunk, chunk)
                both = jnp.where(sub_c < HEAD_DIM, dqt0_ref[:, rows], dqt1_ref[:, rows])
                dq_ref[rows, :] = (jnp.transpose(both) * Q_SCALE).astype(BF16)

        if n_carry:
            @pl.when((j == N_PAIRS - 1) & (b == s // tk - 1))
            def _():
                _scatter_finish(carry_big, None, carry_sems)

    act = jax.ShapeDtypeStruct((s, ATTN_WIDTH), BF16)
    pair_rows = pl.BlockSpec((s, PAIR), lambda j, b: (0, j))
    pair_blk = pl.BlockSpec((tk, PAIR), lambda j, b: (b, j))
    stat = pl.BlockSpec((None, 2, s), lambda j, b: (j, 0, 0))
    carried = [] if carry is None else list(carry)
    carry_sem = [pltpu.SemaphoreType.DMA((SCATTER_SEMS,))] * 2 if n_carry else []
    return pl.pallas_call(
        body, name="attn_bwd_scatter" if n_carry else "attn_bwd", grid=(N_PAIRS, s // tk),
        in_specs=[pair_rows, pair_blk, pair_blk, pair_rows,
                  pl.BlockSpec((tk, 128), lambda j, b: (b, 0)),
                  pl.BlockSpec((None, 8, s), lambda j, b: (j, 0, 0)), stat] + [ANY] * n_carry,
        out_specs=[pair_rows, pair_blk, pair_blk, pair_blk,
                   pl.BlockSpec((None, 8, s), lambda j, b: (j, 0, 0))] + [ANY] * n_carry,
        out_shape=[act, act, act, jax.ShapeDtypeStruct((s, N_PAIRS * 128), F32),
                   jax.ShapeDtypeStruct((N_PAIRS, 8, s), F32)] + (_scatter_shapes(*carried) if n_carry else []),
        scratch_shapes=[pltpu.VMEM((2, PAIR, s), BF16), pltpu.VMEM((2, PAIR, s), BF16),
                        pltpu.VMEM((2, s, PAIR), BF16), pltpu.VMEM((2, s, PAIR), BF16),
                        pltpu.VMEM((PAIR, s), F32), pltpu.VMEM((PAIR, s), F32),
                        pltpu.VMEM((tk, PAIR), F32), pltpu.VMEM((tk, PAIR), F32), pltpu.VMEM((tk, PAIR), F32),
                        pltpu.VMEM((2, tk, PAIR), BF16), pltpu.VMEM((2, tk, PAIR), BF16),
                        pltpu.VMEM((2, PAIR, tk), BF16),
                        pltpu.VMEM((4, tk, tq), F32), pltpu.VMEM((4, tk, tq), F32)] + carry_sem,
        compiler_params=_params(("arbitrary", "arbitrary")),
    )(q, k, v, do, c, lse, dt, *carried)


def _forget_bwd(dcs, drs, lf, tm=512):
    s = lf.shape[0]
    n = s // tm

    def body(dcs_ref, drs_ref, lf_ref, dpf_ref, dfb_ref, carry_ref):
        @pl.when(pl.program_id(0) == 0)
        def _():
            carry_ref[...] = jnp.zeros_like(carry_ref)
            dfb_ref[...] = jnp.zeros_like(dfb_ref)

        lane = lax.broadcasted_iota(jnp.int32, (tm, 128), 1)
        dc = drs_ref[...]
        for pj in range(N_PAIRS):
            blk = dcs_ref[:, pj * 128:(pj + 1) * 128]
            for h in range(2):
                head = 2 * pj + h
                moved = pltpu.roll(blk, (head - _spare(h)) % 128, 1) if head != _spare(h) else blk
                dc = dc - jnp.where(lane == head, moved, 0.0)
        dlf = _scan_rows(dc, reverse=True) + carry_ref[0:1, :]
        carry_ref[...] = dlf[0:8, :]
        dz = jnp.where(lane < 8, dlf * (1.0 - jnp.exp(lf_ref[...])), 0.0)
        dpf_ref[...] = dz.astype(BF16)
        dfb_ref[...] += jnp.sum(dz, axis=0, keepdims=True)

    return pl.pallas_call(
        body, name="forget_bwd", grid=(n,),
        in_specs=[pl.BlockSpec((tm, N_PAIRS * 128), lambda i: (n - 1 - i, 0)),
                  pl.BlockSpec((tm, 128), lambda i: (n - 1 - i, 0)),
                  pl.BlockSpec((tm, 128), lambda i: (n - 1 - i, 0))],
        out_specs=[pl.BlockSpec((tm, 128), lambda i: (n - 1 - i, 0)), _full_spec((1, 128))],
        out_shape=[jax.ShapeDtypeStruct((s, 128), BF16), jax.ShapeDtypeStruct((1, 128), F32)],
        scratch_shapes=[pltpu.VMEM((8, 128), F32)],
        compiler_params=_params(("arbitrary",)),
    )(dcs, drs, lf)


def _pool_bwd(pu, pg, dmp, pw, ps, tm=512):
    s = pu.shape[0]
    hb = tm // HALO
    n = s // tm
    last_halo = s // HALO - 1

    def body(pu_ref, halo_ref, pg_ref, dmp_ref, pgn_ref, dmpn_ref, pw_ref, ps_ref,
             dpu_ref, dpg_ref, gpw_ref, gps_ref):
        i = pl.program_id(0)

        @pl.when(i == 0)
        def _():
            gpw_ref[...] = jnp.zeros_like(gpw_ref)
            gps_ref[...] = jnp.zeros_like(gps_ref)

        u = pu_ref[...].astype(F32)
        halo = jnp.where(i == 0, 0.0, halo_ref[...].astype(F32))
        d = _pool_delta(u, halo, i * tm)
        e_parts, dd_parts = [], []
        for gi in range(len(POOL_WINDOWS)):
            cols = slice(gi * GROUP_DIM, (gi + 1) * GROUP_DIM)
            wg = pw_ref[gi]
            scale = ps_ref[:, cols]
            db = d[gi].astype(BF16)
            z = jnp.dot(db, wg, preferred_element_type=F32)
            gate = pg_ref[:, cols].astype(F32)
            sg = _sigmoid(gate)
            dm = dmp_ref[:, cols].astype(F32)
            dy = dm * (gate * sg)
            dpg_ref[:, cols] = (dm * (z * scale) * (sg * (1.0 + gate * (1.0 - sg)))).astype(BF16)
            gps_ref[:, cols] += jnp.sum(dy * z, axis=0, keepdims=True)
            dz = (dy * scale).astype(BF16)
            gpw_ref[gi] += lax.dot_general(db, dz, TN, preferred_element_type=F32)
            dd = lax.dot_general(dz, wg, NT, preferred_element_type=F32)
            gate_n = pgn_ref[:, cols].astype(F32)
            dz_n = (dmpn_ref[:, cols].astype(F32) * (gate_n * _sigmoid(gate_n)) * scale).astype(BF16)
            dd_n = lax.dot_general(dz_n, wg, NT, preferred_element_type=F32)
            dd_n = jnp.where(i == n - 1, 0.0, dd_n)
            dd_parts.append(dd)
            e_parts.append(jnp.concatenate([_over_count(dd, i * tm, gi), dd_n * (1.0 / POOL_WINDOWS[gi])], axis=0))
        lead = _window_sums(jnp.concatenate(e_parts, axis=1), forward=True)
        for gi in range(len(POOL_WINDOWS)):
            cols = slice(gi * GROUP_DIM, (gi + 1) * GROUP_DIM)
            dpu_ref[:, cols] = (lead[gi][0:tm, :] - dd_parts[gi]).astype(BF16)

    act = jax.ShapeDtypeStruct((s, 512), BF16)
    prev_halo = pl.BlockSpec((HALO, 512), lambda i: (jnp.maximum(i * hb - 1, 0), 0))
    next_halo = pl.BlockSpec((HALO, 512), lambda i: (jnp.minimum((i + 1) * hb, last_halo), 0))
    return pl.pallas_call(
        body, name="pool_bwd", grid=(n,),
        in_specs=[_row_spec(tm, 512), prev_halo, _row_spec(tm, 512), _row_spec(tm, 512), next_halo, next_halo,
                  _full_spec((4, GROUP_DIM, GROUP_DIM)), _full_spec((1, 512))],
        out_specs=[_row_spec(tm, 512), _row_spec(tm, 512), _full_spec((4, GROUP_DIM, GROUP_DIM)),
                   _full_spec((1, 512))],
        out_shape=[act, act, jax.ShapeDtypeStruct((4, GROUP_DIM, GROUP_DIM), F32),
                   jax.ShapeDtypeStruct((1, 512), F32)],
        compiler_params=_params(("arbitrary",)),
    )(pu, pu, pg, dmp, pg, dmp, pw, ps)


def _inproj_bwd_x(dsec, wt, x, g, dxo, exchange=None, scatter=None, tm=512):
    s = x.shape[0]
    n_steps = s // tm
    carried = list(exchange or scatter or ())
    n_carry = len(carried)

    def body(*refs):
        d_refs = refs[0:7]
        wt_ref, x_ref, g_ref, dxo_ref = refs[7:11]
        carry_src = refs[11:11 + n_carry]
        dxi_ref, dg_ref = refs[11 + n_carry:13 + n_carry]
        carry_dst = refs[13 + n_carry:13 + 2 * n_carry]
        sems = refs[13 + 2 * n_carry:]
        step = pl.program_id(0)

        def comm(phase):
            if exchange is not None:
                (_exchange_start if phase == 0 else _exchange_finish)(carry_src, carry_dst, None, sems)
            else:
                big = tuple(zip(carry_src[:2], carry_dst[:2]))
                (_scatter_start if phase == 0 else _scatter_finish)(big, (carry_src[2], carry_dst[2]), sems)

        @pl.when(step == 0)
        def _():
            dg_ref[...] = jnp.zeros_like(dg_ref)
            if n_carry:
                comm(0)

        dh = None
        for d_ref, (off, ncol) in zip(d_refs, SECTIONS):
            t = jnp.dot(d_ref[...], wt_ref[off:off + ncol, :], preferred_element_type=F32)
            dh = t if dh is None else dh + t
        xf = x_ref[...]
        r = lax.rsqrt(jnp.mean(xf * xf, axis=-1, keepdims=True) + RMS_EPS)
        xh = xf * r
        u = dh * g_ref[...]
        dxi_ref[...] = dxo_ref[...] + r * (u - xh * jnp.mean(xh * u, axis=-1, keepdims=True))
        dg_ref[...] += jnp.sum(dh * xh, axis=0, keepdims=True)

        if n_carry:
            @pl.when(step == n_steps - 1)
            def _():
                comm(1)

    if exchange is not None:
        name, carry_shapes, n_sems = "inproj_bwd_x_exchange", _exchange_shapes(*exchange), 3
    elif scatter is not None:
        name, n_sems = "inproj_bwd_x_scatter", SCATTER_SEMS
        carry_shapes = _scatter_shapes(*scatter[:2]) + [jax.ShapeDtypeStruct((3, SMALL_HALF, 128), F32)]
    else:
        name, carry_shapes, n_sems = "inproj_bwd_x", [], 0
    return pl.pallas_call(
        body, name=name, grid=(n_steps,),
        in_specs=[_row_spec(tm, ncol) for _, ncol in SECTIONS]
                 + [_full_spec((IN_COLS_PAD, D_MODEL)), _row_spec(tm, D_MODEL), _full_spec((1, D_MODEL)),
                    _row_spec(tm, D_MODEL)] + [ANY] * n_carry,
        out_specs=[_row_spec(tm, D_MODEL), _full_spec((1, D_MODEL))] + [ANY] * n_carry,
        out_shape=[jax.ShapeDtypeStruct((s, D_MODEL), F32), jax.ShapeDtypeStruct((1, D_MODEL), F32)] + carry_shapes,
        scratch_shapes=[pltpu.SemaphoreType.DMA((n_sems,))] * 2 if n_carry else [],
        compiler_params=_params(("arbitrary",)),
    )(*dsec, wt, x, g, dxo, *carried)


def _inproj_bwd_w(h, dsec, tm=512):
    s = h.shape[0]
    n_steps = s // tm

    def body(*refs):
        h_ref = refs[0]
        d_refs = refs[1:8]
        out_ref, gw_ref = refs[8:]
        step = pl.program_id(0)

        @pl.when(step == 0)
        def _():
            gw_ref[...] = jnp.zeros_like(gw_ref)

        hh = h_ref[...]
        for d_ref, (off, ncol) in zip(d_refs, SECTIONS):
            gw_ref[:, off:off + ncol] += lax.dot_general(hh, d_ref[...], TN, preferred_element_type=F32)

        @pl.when(step == n_steps - 1)
        def _():
            for k in range(N_CHIPS):
                out_ref[k] = gw_ref[:, k * SHARD_COLS:(k + 1) * SHARD_COLS]

    return pl.pallas_call(
        body, name="inproj_bwd_w", grid=(n_steps,),
        in_specs=[_row_spec(tm, D_MODEL)] + [_row_spec(tm, ncol) for _, ncol in SECTIONS],
        out_specs=_full_spec((N_CHIPS, D_MODEL, SHARD_COLS)),
        out_shape=jax.ShapeDtypeStruct((N_CHIPS, D_MODEL, SHARD_COLS), F32),
        scratch_shapes=[pltpu.VMEM((D_MODEL, IN_COLS_PAD), F32)],
        compiler_params=_params(("arbitrary",), vmem_mb=60),
    )(h, *dsec)


def _elementwise(fn, name, n_out, arrays, tm):
    shape = arrays[0].shape
    rest = shape[1:]
    n_in = len(arrays)

    def body(*refs):
        outs = fn(*[r[...] for r in refs[:n_in]])
        for r, val in zip(refs[n_in:], outs):
            r[...] = val

    spec = pl.BlockSpec((tm,) + rest, lambda i: (i,) + (0,) * len(rest))
    return pl.pallas_call(
        body, name=name, grid=(shape[0] // tm,),
        in_specs=[spec] * n_in, out_specs=[spec] * n_out, out_shape=[jax.ShapeDtypeStruct(shape, F32)] * n_out,
        compiler_params=_params(("parallel",)),
    )(*arrays)


def _add2(a, b, name, tm):
    return _elementwise(lambda p, q: (p + q,), name, 1, [a, b], tm)[0]


def _chip_sum(place, g, r, name, tm):
    n_layers, _, rows, cols = g.shape
    half = rows // 2
    nb = half // tm

    def body(place_ref, g_ref, r_ref, o_ref):
        o_ref[...] = (g_ref[...] + r_ref[...]).astype(BF16)

    blk = (None, None, tm, cols)
    return pl.pallas_call(
        body, name=name,
        grid_spec=pltpu.PrefetchScalarGridSpec(
            num_scalar_prefetch=1, grid=(n_layers, N_CHIPS, nb),
            in_specs=[pl.BlockSpec(blk, lambda l, k, i, p: (l, k, p[1] * nb + i, 0)),
                      pl.BlockSpec(blk, lambda l, k, i, p: (l, k, i, 0))],
            out_specs=pl.BlockSpec(blk, lambda l, k, i, p: (l, k, i, 0))),
        out_shape=jax.ShapeDtypeStruct((n_layers, N_CHIPS, half, cols), BF16),
        compiler_params=_params(("parallel", "parallel", "parallel")),
    )(place, g, r)


def _mesh_sum(place, g, r, got, into, layer, name, tm):
    _, _, rows, cols = g.shape
    half = rows // 2
    nb = half // tm

    def body(place_ref, g_ref, r_ref, got_ref, into_ref, o_ref):
        own = g_ref[...] + r_ref[...]
        o_ref[...] = (own + got_ref[0].astype(F32)) + (got_ref[1].astype(F32) + got_ref[2].astype(F32))

    blk = (None, None, tm, cols)
    return pl.pallas_call(
        body, name=name,
        grid_spec=pltpu.PrefetchScalarGridSpec(
            num_scalar_prefetch=1, grid=(nb,),
            in_specs=[pl.BlockSpec(blk, lambda i, p: (0, p[0], p[1] * nb + i, 0)),
                      pl.BlockSpec(blk, lambda i, p: (0, p[0], i, 0)),
                      pl.BlockSpec((None, 3, tm, cols), lambda i, p: (0, 0, i, 0)),
                      ANY],
            out_specs=pl.BlockSpec((None, tm, cols), lambda i, p: (layer, p[1] * nb + i, 0))),
        out_shape=jax.ShapeDtypeStruct(into.shape, F32),
        input_output_aliases={4: 0},
        compiler_params=_params(("parallel",)),
    )(place, g, r, got, into)


def _mesh_sum_small(place, ps, got, tm):
    nb = SMALL_HALF // tm

    def body(place_ref, ps_ref, got_ref, o_ref):
        o_ref[...] = (ps_ref[...] + got_ref[0]) + (got_ref[1] + got_ref[2])

    return pl.pallas_call(
        body, name="mesh_sum_small",
        grid_spec=pltpu.PrefetchScalarGridSpec(
            num_scalar_prefetch=1, grid=(nb,),
            in_specs=[pl.BlockSpec((tm, 128), lambda i, p: (p[1] * nb + i, 0)),
                      pl.BlockSpec((3, tm, 128), lambda i, p: (0, i, 0))],
            out_specs=pl.BlockSpec((tm, 128), lambda i, p: (p[1] * nb + i, 0))),
        out_shape=jax.ShapeDtypeStruct((SMALL_ROWS, 128), F32),
        compiler_params=_params(("parallel",)),
    )(place, ps, got)


def _adamw(g, w, m, v, name, tm):
    def fn(g, w, m, v):
        m = ADAM_B1 * m + (1.0 - ADAM_B1) * g
        v = ADAM_B2 * v + (1.0 - ADAM_B2) * (g * g)
        m_hat = m / (1.0 - ADAM_B1 ** ADAM_STEP)
        v_hat = v / (1.0 - ADAM_B2 ** ADAM_STEP)
        delta = -ADAM_LR * (m_hat / (jnp.sqrt(v_hat) + ADAM_EPS) + ADAM_WD * w)
        return g, delta, m, v

    return _elementwise(fn, name, 4, [g, w, m, v], tm)


def _position():
    x, y, c = lax.axis_index("x"), lax.axis_index("y"), lax.axis_index("c")
    other_chips = [(x, 1 - y), (1 - x, y), (1 - x, 1 - y)]
    return x, y, c, other_chips


def _remote(src, dst, sems, k, to):
    send_sems, recv_sems = sems
    return pltpu.make_async_remote_copy(src_ref=src, dst_ref=dst, send_sem=send_sems.at[k],
                                        recv_sem=recv_sems.at[k], device_id=to, device_id_type=MESH)


def _comm_call(body, name, n_in, out_shape, n_remote, aliases=None):
    return pl.pallas_call(
        body, name=name, in_specs=[ANY] * n_in, out_specs=[ANY] * len(out_shape), out_shape=out_shape,
        scratch_shapes=[pltpu.SemaphoreType.DMA((n_remote,)), pltpu.SemaphoreType.DMA((n_remote,))],
        input_output_aliases=aliases or {},
    )


def _halves(ref_rows, c):
    half = ref_rows // 2
    return pl.ds(half * c, half), pl.ds(half * (1 - c), half)


def _gather_weights(wi):
    def body(wi_ref, gi_ref, send_sems, recv_sems):
        _gather_start((wi_ref,), (gi_ref,), (send_sems, recv_sems))
        _gather_finish((wi_ref,), (gi_ref,), (send_sems, recv_sems))

    return _comm_call(body, "gather_weights", 1, _gather_shapes(wi), GATHER_SEMS)(wi)[0]


GATHER_SEMS = 6
SCATTER_SEMS = 9


def _gather_shapes(*arrays):
    return [jax.ShapeDtypeStruct((a.shape[0], N_CHIPS) + a.shape[1:], a.dtype) for a in arrays]


def _gather_sends(srcs, dsts, sems):
    x, y, c, chips = _position()
    out = []
    for k, (cx, cy) in enumerate(chips):
        for a, (src, dst) in enumerate(zip(srcs, dsts)):
            mine, _ = _halves(src.shape[1], c)
            out.append(_remote(src.at[:, mine], dst.at[:, k + 1, mine], sems, len(srcs) * k + a, (cx, cy, c)))
    return out


def _gather_own(srcs, dsts, local_sems):
    return [pltpu.make_async_copy(src, dst.at[:, 0], local_sems.at[a]) for a, (src, dst) in enumerate(zip(srcs, dsts))]


def _gather_start(srcs, dsts, sems):
    for cp in _gather_sends(srcs, dsts, sems):
        cp.start()


def _gather_finish(srcs, dsts, sems):
    x, y, c, _ = _position()
    n = len(srcs)
    forwards = []
    for k in range(3):
        for a, dst in enumerate(dsts):
            mine, _ = _halves(dst.shape[2], c)
            blk = dst.at[:, k + 1, mine]
            _remote(blk, blk, sems, n * k + a, (x, y, c)).wait_recv()
            forwards.append(_remote(blk, blk, sems, 3 * n + n * k + a, (x, y, 1 - c)))
            forwards[-1].start()
    for k in range(3):
        for a, dst in enumerate(dsts):
            _, other = _halves(dst.shape[2], c)
            blk = dst.at[:, k + 1, other]
            _remote(blk, blk, sems, 3 * n + n * k + a, (x, y, c)).wait_recv()
    for cp in _gather_sends(srcs, dsts, sems) + forwards:
        cp.wait_send()


def _scatter_sends(big, small, sems):
    x, y, c, chips = _position()
    n = len(big) + (small is not None)
    out = []
    for k, (cx, cy) in enumerate(chips):
        ck = 2 * cx + cy
        for a, (src, dst) in enumerate(big):
            out.append(_remote(src.at[:, ck], dst.at[:, k], sems, n * k + a, (cx, cy, c)))
        if small is not None:
            half, _ = _halves(SMALL_ROWS, c)
            out.append(_remote(small[0].at[half], small[1].at[k], sems, n * k + n - 1, (cx, cy, c)))
    return out


def _scatter_start(big, small, sems):
    for cp in _scatter_sends(big, small, sems):
        cp.start()


def _scatter_finish(big, small, sems):
    x, y, c, _ = _position()
    n = len(big) + (small is not None)
    for k in range(3):
        landing = [dst.at[:, k] for _, dst in big] + ([small[1].at[k]] if small is not None else [])
        for a, blk in enumerate(landing):
            _remote(blk, blk, sems, n * k + a, (x, y, c)).wait_recv()
    for cp in _scatter_sends(big, small, sems):
        cp.wait_send()


def _exchange_shapes(gi, go):
    return [jax.ShapeDtypeStruct(a.shape[:2] + (a.shape[2] // 2, a.shape[3]), F32) for a in (gi, go)]


def _exchange_copies(srcs, dsts, small, sems):
    x, y, c, _ = _position()
    cps = [_remote(src.at[:, :, _halves(src.shape[2], c)[1]], dst, sems, a, (x, y, 1 - c))
           for a, (src, dst) in enumerate(zip(srcs, dsts))]
    if small is not None:
        cps.append(_remote(small[0], small[1], sems, len(srcs), (x, y, 1 - c)))
    return cps


def _exchange_start(srcs, dsts, small, sems):
    for cp in _exchange_copies(srcs, dsts, small, sems):
        cp.start()


def _exchange_finish(srcs, dsts, small, sems):
    for cp in _exchange_copies(srcs, dsts, small, sems):
        cp.wait()


def _exchange_sibling(gi, go, sm):
    def body(gi_ref, go_ref, sm_ref, ri_ref, ro_ref, rs_ref, send_sems, recv_sems):
        args = ((gi_ref, go_ref), (ri_ref, ro_ref), (sm_ref, rs_ref), (send_sems, recv_sems))
        _exchange_start(*args)
        _exchange_finish(*args)

    out_shape = _exchange_shapes(gi, go) + [jax.ShapeDtypeStruct(sm.shape, F32)]
    return _comm_call(body, "exchange_sibling", 3, out_shape, 3)(gi, go, sm)


def _scatter_shapes(pi, po):
    return [jax.ShapeDtypeStruct((a.shape[0], 3) + a.shape[2:], a.dtype) for a in (pi, po)]


def _join_sibling(fi, fo, fs, late):
    def body(fi_ref, fo_ref, fs_ref, late_ref, oi_ref, oo_ref, os_ref, sum_ref, send_sems, recv_sems,
             sib_ref, chip_ref, got_ref):
        x, y, c, chips = _position()
        sems = (send_sems, recv_sems)
        sent, got = [], []
        for a, (src, dst) in enumerate(((fi_ref, oi_ref), (fo_ref, oo_ref))):
            mine, other = _halves(src.shape[1], c)
            sent.append(_remote(src.at[:, mine], dst.at[:, mine], sems, a, (x, y, 1 - c)))
            got.append(_remote(dst.at[:, other], dst.at[:, other], sems, a, (x, y, c)))
        mine, other = _halves(SMALL_ROWS, c)
        sent.append(_remote(fs_ref.at[mine], os_ref.at[mine], sems, 2, (x, y, 1 - c)))
        got.append(_remote(os_ref.at[other], os_ref.at[other], sems, 2, (x, y, c)))
        for cp in sent:
            cp.start()

        to_sibling = _remote(late_ref, sib_ref, sems, 3, (x, y, 1 - c))
        to_sibling.start()
        to_sibling.wait()
        chip_ref[...] = late_ref[...] + sib_ref[...]
        to_chips = [_remote(chip_ref, got_ref.at[k], sems, 4 + k, (cx, cy, c)) for k, (cx, cy) in enumerate(chips)]
        for cp in to_chips:
            cp.start()
        for cp in to_chips:
            cp.wait()
        sum_ref[...] = (chip_ref[...] + got_ref[0]) + (got_ref[1] + got_ref[2])

        for cp in got:
            cp.wait_recv()
        for cp in sent:
            cp.wait_send()

    vmem = pl.BlockSpec(memory_space=pltpu.VMEM)
    return pl.pallas_call(
        body, name="join_sibling", in_specs=[ANY] * 3 + [vmem], out_specs=[ANY] * 3 + [vmem],
        out_shape=[jax.ShapeDtypeStruct(a.shape, F32) for a in (fi, fo, fs, late)],
        scratch_shapes=[pltpu.SemaphoreType.DMA((7,)), pltpu.SemaphoreType.DMA((7,)),
                        pltpu.VMEM(late.shape, F32), pltpu.VMEM(late.shape, F32), pltpu.VMEM((3,) + late.shape, F32)],
        input_output_aliases={0: 0, 1: 1, 2: 2},
    )(fi, fo, fs, late)


SMALL_SIZES = (DEPTH * D_MODEL, D_MODEL, DEPTH * POOL_WIDTH, 128, DEPTH * 4 * GROUP_DIM * GROUP_DIM)
LOSS_SLOT = DEPTH * D_MODEL + D_MODEL + DEPTH * POOL_WIDTH + DEPTH * 8


def _pack_small(norm_g, final_g, pool_scale, forget_bias, pool_w, loss=None):
    fb = forget_bias.reshape(-1)
    if loss is not None:
        fb = jnp.concatenate([fb, loss.reshape(1)])
    fb = jnp.pad(fb, (0, 128 - fb.size))
    flat = jnp.concatenate([norm_g.reshape(-1), final_g.reshape(-1), pool_scale.reshape(-1), fb,
                            pool_w.reshape(-1)])
    return jnp.pad(flat, (0, SMALL_ROWS * 128 - flat.size)).reshape(SMALL_ROWS, 128)


def _unpack_small(packed):
    flat = packed.reshape(-1)
    offs = [0]
    for n in SMALL_SIZES:
        offs.append(offs[-1] + n)
    norm_g, final_g, pool_scale, fb, pool_w = [flat[offs[i]:offs[i + 1]] for i in range(5)]
    return (norm_g.reshape(DEPTH, D_MODEL), final_g, pool_scale.reshape(DEPTH, POOL_WIDTH),
            fb[:DEPTH * 8].reshape(DEPTH, 8), pool_w.reshape(DEPTH, 4, GROUP_DIM, GROUP_DIM))


def kernel(x, norm_g, w_in, forget_bias, pool_w, pool_scale, w_out, final_g, loss_target, m_norm_g, m_w_in, m_forget_bias, m_pool_w, m_pool_scale, m_w_out, m_final_g, v_norm_g, v_w_in, v_forget_bias, v_pool_w, v_pool_scale, v_w_out, v_final_g):
    x0 = x[0]
    tgt = loss_target[0]
    s = x0.shape[0]

    me = 2 * lax.axis_index("x") + lax.axis_index("y")
    place = jnp.stack([me, lax.axis_index("c")]).astype(jnp.int32)
    wi_b, wo_b = w_in.astype(BF16), w_out.astype(BF16)

    def whole_w_out(got):
        return jnp.concatenate([lax.dynamic_index_in_dim(got, jnp.bitwise_xor(k, me), axis=1, keepdims=False)[0]
                                for k in range(N_CHIPS)], axis=0)

    pool_w_b = pool_w.astype(BF16)
    fb_pad = jnp.pad(forget_bias, ((0, 0), (0, 128 - forget_bias.shape[1])))
    head_sel = (jnp.arange(ATTN_WIDTH)[None, :] // HEAD_DIM == jnp.arange(8)[:, None]).astype(BF16)

    saved, weights = [], []
    xl = x0
    got_i = _gather_weights(wi_b[0:1])
    for l in range(DEPTH):
        pu, pg, q, k, v, ag, lf, c, h, po, w_t = _inproj_fwd(
            place, xl, norm_g[l][None], None if l > 0 else wi_b[0], got_i, fb_pad[l][None],
            pool_w_b[l], pool_scale[l][None])
        if l + 1 < DEPTH:
            ma, o, lse, got_o, got_i = _attn_fwd(q, k, v, c, ag, carry=(wo_b[l:l + 1], wi_b[l + 1:l + 2]))
        else:
            ma, o, lse, got_o = _attn_fwd(q, k, v, c, ag, carry=(wo_b[l:l + 1],))
        w_o = whole_w_out(got_o)
        weights.append((w_t, w_o))
        saved.append((xl, h, pu, pg, q, k, v, ag, lf, c, po, ma, o, lse))
        if l + 1 < DEPTH:
            xl = _outproj_fwd(xl, po, ma, w_o)
    dx, g_final, loss_part = _loss_head(xl, po, ma, w_o, tgt, final_g[None])

    g_norm, g_fb, g_pw, g_ps = [], [], [], []
    parts = {}
    pending = None
    for l in reversed(range(DEPTH)):
        xl, h, pu, pg, q, k, v, ag, lf, c, po, ma, o, lse = saved[l]
        w_t, w_o = weights[l]
        dmp, do, dag, dt, gwo = _outproj_bwd(dx, po, ma, ag, o, w_o, head_sel)
        if pending is None:
            dq, dk, dv, dcs, drs = _attn_bwd(q, k, v, do, c, lse, dt.reshape(N_PAIRS, 2, s))
        else:
            dq, dk, dv, dcs, drs, qi, qo = _attn_bwd(q, k, v, do, c, lse, dt.reshape(N_PAIRS, 2, s), carry=pending)
            parts[l + 1] += (qi, qo)
        drs = jnp.pad(drs[:, 0:2, :].reshape(8, s).T, ((0, 0), (0, 120)))
        dpf, dfb = _forget_bwd(dcs, drs, lf)
        dpu, dpg, gpw, gps = _pool_bwd(pu, pg, dmp, pool_w_b[l], pool_scale[l][None])
        dsec = (dpu, dpg, dq, dk, dv, dag, dpf)
        gwi = _inproj_bwd_w(h, dsec)
        g_fb.append(dfb[0, :8])
        g_pw.append(gpw)
        g_ps.append(gps[0])
        send_i = gwi[None]
        send_o = gwo.reshape(1, N_CHIPS, SHARD_ROWS, D_MODEL)
        if l > 0:
            dx, gn, ri, ro = _inproj_bwd_x(dsec, w_t, xl, norm_g[l][None], dx, exchange=(send_i, send_o))
            g_norm.append(gn[0])
        else:
            g_norm.append(jnp.zeros((D_MODEL,), F32))
            g_norm, g_fb, g_pw, g_ps = [t[::-1] for t in (g_norm, g_fb, g_pw, g_ps)]
            send_s = _pack_small(jnp.stack(g_norm), g_final[0], jnp.stack(g_ps), jnp.stack(g_fb), jnp.stack(g_pw),
                                 loss=loss_part[0, 0])
            ri, ro, rs = _exchange_sibling(send_i, send_o, send_s)
        pending = (_chip_sum(place, send_i, ri, "chip_sum_w_in", 256),
                   _chip_sum(place, send_o, ro, "chip_sum_w_out", SHARD_ROWS // 2))
        parts[l] = (send_i, ri, send_o, ro)
    ps_ = _add2(send_s, rs, "chip_sum_small", SMALL_ROWS // 4)
    dx, gn0, qi, qo, qs = _inproj_bwd_x(dsec, w_t, xl, norm_g[0][None], dx, scatter=(*pending, ps_))
    parts[0] += (qi, qo)
    fi = lax.empty((DEPTH, D_MODEL, SHARD_COLS), F32)
    fo = lax.empty((DEPTH, SHARD_ROWS, D_MODEL), F32)
    for l in range(DEPTH):
        send_i, ri, send_o, ro, qi, qo = parts[l]
        fi = _mesh_sum(place, send_i, ri, qi, fi, l, "mesh_sum_w_in", 256)
        fo = _mesh_sum(place, send_o, ro, qo, fo, l, "mesh_sum_w_out", SHARD_ROWS // 2)
    hs = _mesh_sum_small(place, ps_, qs, SMALL_HALF // 4)
    fi, fo, fs, gn0 = _join_sibling(fi, fo, hs, gn0.reshape(8, 128))
    fs = jnp.concatenate([fs[0:8] + gn0, fs[8:]], axis=0)

    col_major = lambda a: jnp.transpose(a, (2, 0, 1))
    flat_o = lambda a: a.reshape(-1, D_MODEL)
    out_i = _adamw(col_major(fi), col_major(w_in), col_major(m_w_in), col_major(v_w_in), "adamw_w_in", 77)
    out_i = [jnp.transpose(t, (1, 2, 0)) for t in out_i]
    out_o = _adamw(flat_o(fo), flat_o(w_out), flat_o(m_w_out), flat_o(v_w_out), "adamw_w_out", 256)
    out_s = _adamw(fs, _pack_small(norm_g, final_g, pool_scale, forget_bias, pool_w),
                   _pack_small(m_norm_g, m_final_g, m_pool_scale, m_forget_bias, m_pool_w),
                   _pack_small(v_norm_g, v_final_g, v_pool_scale, v_forget_bias, v_pool_w),
                   "adamw_small", SMALL_ROWS // 4)

    loss = out_s[0].reshape(-1)[LOSS_SLOT]
    groups = []
    for oi, oo, os_ in zip(out_i, out_o, out_s):
        sn, sf, sp, sb, sw = _unpack_small(os_)
        groups.append((sn, oi.reshape(w_in.shape), sb, sw, sp, oo.reshape(w_out.shape), sf))
    return (loss, dx[None]) + tuple(t for grp in groups for t in grp)
```

```python
import jax
import jax.numpy as jnp
from jax import lax
from jax.experimental import pallas as pl
from jax.experimental.pallas import tpu as pltpu

F32 = jnp.float32
BF16 = jnp.bfloat16

D_MODEL = 1024
DEPTH = 4
POOL_WIDTH = 512
ATTN_WIDTH = 512
HEAD_DIM = 64
PAIR = 2 * HEAD_DIM
N_PAIRS = ATTN_WIDTH // PAIR
POOL_WINDOWS = (2, 4, 8, 16)
GROUP_DIM = 128
HALO = 16
IN_COLS = 3080
OFF_F = 3072
IN_COLS_PAD = 3200
SECTIONS = ((0, 512), (512, 512), (1024, 512), (1536, 512), (2048, 512), (2560, 512), (OFF_F, 128))
N_CHIPS = 4
SHARD_COLS = IN_COLS // N_CHIPS
SHARD_ROWS = D_MODEL // N_CHIPS
RMS_EPS = 1e-6
NEG = -1e30
Q_SCALE = 0.125

ADAM_LR = 0.001
ADAM_B1 = 0.9
ADAM_B2 = 0.999
ADAM_EPS = 1e-08
ADAM_WD = 0.01
ADAM_STEP = 10

SMALL_ROWS = 2112
SMALL_HALF = SMALL_ROWS // 2

NT = (((1,), (1,)), ((), ()))
TN = (((0,), (0,)), ((), ()))
MESH = pl.DeviceIdType.MESH
ANY = pl.BlockSpec(memory_space=pl.ANY)


def _params(semantics, vmem_mb=48):
    return pltpu.CompilerParams(dimension_semantics=semantics, vmem_limit_bytes=vmem_mb << 20)


def _row_spec(tm, cols):
    return pl.BlockSpec((tm, cols), lambda i: (i, 0))


def _full_spec(shape):
    return pl.BlockSpec(shape, lambda *_: (0,) * len(shape))


def _sigmoid(x):
    return jax.nn.sigmoid(x)


def _scan_rows(a, reverse=False):
    n = a.shape[0]
    row = lax.broadcasted_iota(jnp.int32, a.shape, 0)
    k = 1
    while k < n:
        if reverse:
            a = a + jnp.where(row < n - k, pltpu.roll(a, n - k, 0), 0.0)
        else:
            a = a + jnp.where(row >= k, pltpu.roll(a, k, 0), 0.0)
        k *= 2
    return a


def _inproj_fwd(place, x, g, own, got, fb, pw, ps, tm=512):
    s = x.shape[0]

    def body(place_ref, x_ref, g_ref, fb_ref, pw_ref, ps_ref, *refs):
        own_ref = refs[0] if own is not None else None
        (got_ref, pu_ref, pg_ref, q_ref, k_ref, v_ref, ag_ref, lf_ref, c_ref, h_ref, po_ref, wt_ref,
         carry_ref, halo_ref, w_ref, piece_ref, sem) = refs[own is not None:]
        step = pl.program_id(0)

        @pl.when(step == 0)
        def _():
            carry_ref[...] = jnp.zeros_like(carry_ref)
            halo_ref[...] = jnp.zeros_like(halo_ref)
            for chip in range(N_CHIPS):
                rel = jnp.bitwise_xor(chip, place_ref[0])

                def fetch(src):
                    cp = pltpu.make_async_copy(src, piece_ref, sem)
                    cp.start()
                    cp.wait()

                if own_ref is None:
                    fetch(got_ref.at[0, rel])
                else:
                    pl.when(rel == 0)(lambda: fetch(own_ref))
                    pl.when(rel != 0)(lambda: fetch(got_ref.at[0, rel]))
                w_ref[:, chip * SHARD_COLS:(chip + 1) * SHARD_COLS] = piece_ref[...]
            w_ref[:, IN_COLS:] = jnp.zeros((D_MODEL, IN_COLS_PAD - IN_COLS), BF16)
            for t in range(IN_COLS_PAD // 128):
                cols = slice(t * 128, (t + 1) * 128)
                wt_ref[cols, :] = jnp.transpose(w_ref[:, cols].astype(F32)).astype(BF16)

        xf = x_ref[...]
        r = lax.rsqrt(jnp.mean(xf * xf, axis=-1, keepdims=True) + RMS_EPS)
        h = (xf * r * g_ref[...]).astype(BF16)
        h_ref[...] = h

        def proj(sec):
            off, n = SECTIONS[sec]
            return jnp.dot(h, w_ref[:, off:off + n], preferred_element_type=F32)

        pu_ref[...] = proj(0).astype(BF16)
        pg_ref[...] = proj(1).astype(BF16)
        u = pu_ref[...].astype(F32)
        d = _pool_delta(u, halo_ref[...], step * tm)
        halo_ref[...] = u[tm - HALO:, :]
        for gi in range(len(POOL_WINDOWS)):
            cols = slice(gi * GROUP_DIM, (gi + 1) * GROUP_DIM)
            zg = jnp.dot(d[gi].astype(BF16), pw_ref[gi], preferred_element_type=F32)
            gate = pg_ref[:, cols].astype(F32)
            po_ref[:, cols] = (zg * ps_ref[:, cols] * (gate * _sigmoid(gate))).astype(BF16)
        q_ref[...] = (proj(2) * Q_SCALE).astype(BF16)
        k_ref[...] = proj(3).astype(BF16)
        v_ref[...] = proj(4).astype(BF16)
        ag_ref[...] = proj(5).astype(BF16)
        z = proj(6) + fb_ref[...]
        lf = jnp.minimum(z, 0.0) - jnp.log(1.0 + jnp.exp(-jnp.abs(z)))
        lf_ref[...] = lf
        c_ref[...] = _scan_rows(lf) + carry_ref[0:1, :]
        carry_ref[0:1, :] = c_ref[tm - 1:tm, :]

    act = jax.ShapeDtypeStruct((s, 512), BF16)
    rows = lambda cols: pl.BlockSpec((tm, cols), lambda i, p: (i, 0))
    whole = lambda shape: pl.BlockSpec(shape, lambda i, p: (0,) * len(shape))
    shards = ([] if own is None else [own]) + [got]
    return pl.pallas_call(
        body, name="inproj_fwd" if own is None else "inproj_fwd_first",
        grid_spec=pltpu.PrefetchScalarGridSpec(
            num_scalar_prefetch=1, grid=(s // tm,),
            in_specs=[rows(D_MODEL), whole((1, D_MODEL)), whole((1, 128)), whole((4, GROUP_DIM, GROUP_DIM)),
                      whole((1, 512))] + [ANY] * len(shards),
            out_specs=[rows(512)] * 6 + [rows(128), rows(128), rows(D_MODEL), rows(512),
                                         whole((IN_COLS_PAD, D_MODEL))],
            scratch_shapes=[pltpu.VMEM((8, 128), F32), pltpu.VMEM((HALO, 512), F32),
                            pltpu.VMEM((D_MODEL, IN_COLS_PAD), BF16),
                            pltpu.VMEM((D_MODEL, SHARD_COLS), BF16), pltpu.SemaphoreType.DMA]),
        out_shape=[act] * 6 + [jax.ShapeDtypeStruct((s, 128), F32), jax.ShapeDtypeStruct((s, 128), F32),
                               jax.ShapeDtypeStruct((s, D_MODEL), BF16), act,
                               jax.ShapeDtypeStruct((IN_COLS_PAD, D_MODEL), BF16)],
        compiler_params=_params(("arbitrary",), vmem_mb=56),
    )(place, x, g, fb, pw, ps, *shards)


def _window_sums(ext, forward):
    n = ext.shape[0]
    outs = []
    for gi, w in enumerate(POOL_WINDOWS):
        a = ext[:, gi * GROUP_DIM:(gi + 1) * GROUP_DIM]
        k = 1
        while k < w:
            a = a + pltpu.roll(a, (n - k) if forward else k, 0)
            k *= 2
        outs.append(a)
    return outs


def _over_count(a, row0, gi):
    w = POOL_WINDOWS[gi]
    t = row0 + lax.broadcasted_iota(jnp.int32, (HALO, GROUP_DIM), 0)
    head = a[0:HALO, :] / jnp.minimum(t + 1, w).astype(F32)
    return jnp.concatenate([head, a[HALO:, :] * (1.0 / w)], axis=0)


def _pool_delta(u, halo, row0):
    sums = _window_sums(jnp.concatenate([halo, u], axis=0), forward=False)
    return [_over_count(sums[gi][HALO:, :], row0, gi) - u[:, gi * GROUP_DIM:(gi + 1) * GROUP_DIM]
            for gi in range(len(POOL_WINDOWS))]


def _split3(x):
    hi = x.astype(BF16).astype(F32)
    r1 = x - hi
    lo = r1.astype(BF16).astype(F32)
    return hi, lo, (r1 - lo).astype(BF16).astype(F32)


def _own(idx, h):
    return (idx < HEAD_DIM) if h == 0 else (idx >= HEAD_DIM)


def _spare(h):
    return HEAD_DIM if h == 0 else 0


def _attn_fwd(q, k, v, c, ag, carry=None, tq=512, tk=512, chunk=512):
    s = q.shape[0]
    n_carry = 0 if carry is None else len(carry)

    def body(*refs):
        q_ref, k_ref, v_ref, c_ref, ag_ref = refs[:5]
        carry_src = refs[5:5 + n_carry]
        ma_ref, o_ref, lse_ref = refs[5 + n_carry:8 + n_carry]
        carry_dst = refs[8 + n_carry:8 + 2 * n_carry]
        kx_ref, vt_ref, acc0_ref, acc1_ref, sta_ref, stb_ref = refs[8 + 2 * n_carry:14 + 2 * n_carry]
        carry_sems = refs[14 + 2 * n_carry:]
        j = pl.program_id(0)
        i = pl.program_id(1)

        if n_carry:
            @pl.when((j == 0) & (i == 0))
            def _():
                _gather_start(carry_src, carry_dst, carry_sems[:2])
                for cp in _gather_own(carry_src, carry_dst, carry_sems[2]):
                    cp.start()

        @pl.when(i == 0)
        def _():
            def prep(ch, carry):
                rows = pl.ds(pl.multiple_of(ch * chunk, chunk), chunk)
                kk = k_ref[rows, :].astype(F32)
                cc = c_ref[rows, :]
                vt = jnp.transpose(v_ref[rows, :].astype(F32))
                lane = lax.broadcasted_iota(jnp.int32, (chunk, PAIR), 1)
                sub = lax.broadcasted_iota(jnp.int32, (PAIR, chunk), 0)
                for h in range(2):
                    ccol = jnp.sum(jnp.where(lane == 2 * j + h, cc, 0.0), axis=1, keepdims=True)
                    kx = jnp.where(_own(lane, h), kk, 0.0)
                    for t, part in enumerate(_split3(-ccol)):
                        kx = jnp.where(lane == _spare(h) + t, part, kx)
                    kx_ref[h, rows, :] = kx.astype(BF16)
                    vt_ref[h, :, rows] = jnp.where(_own(sub, h), vt,
                                                   jnp.where(sub == _spare(h), 1.0, 0.0)).astype(BF16)
                return carry

            lax.fori_loop(0, s // chunk, prep, 0)

        qq = q_ref[...].astype(F32)
        lane_q = lax.broadcasted_iota(jnp.int32, (tq, PAIR), 1)
        qx = []
        for h in range(2):
            ones = (lane_q >= _spare(h)) & (lane_q < _spare(h) + 3)
            qx.append(jnp.transpose(jnp.where(_own(lane_q, h), qq, jnp.where(ones, 1.0, 0.0))).astype(BF16))
        accs = (acc0_ref, acc1_ref)
        for acc in accs:
            acc[...] = jnp.zeros_like(acc)

        def rows_of(kb):
            return pl.ds(pl.multiple_of(kb * tk, tk), tk)

        def scores(kb, dst):
            for h in range(2):
                dst[h] = jnp.dot(kx_ref[h, rows_of(kb), :], qx[h], preferred_element_type=F32)

        def consume(kb, src, m, masked):
            m_out, alpha, pv = [], [], []
            for h in range(2):
                sh = src[h]
                if masked:
                    kpos = kb * tk + lax.broadcasted_iota(jnp.int32, (tk, tq), 0)
                    qpos = i * tq + lax.broadcasted_iota(jnp.int32, (tk, tq), 1)
                    sh = jnp.where(qpos >= kpos, sh, NEG)
                m_new = jnp.maximum(m[h], jnp.max(sh, axis=0, keepdims=True))
                pt = jnp.exp(sh - m_new).astype(BF16)
                alpha.append(jnp.exp(m[h] - m_new))
                pv.append(jnp.dot(vt_ref[h, :, rows_of(kb)], pt, preferred_element_type=F32))
                m_out.append(m_new)
            for h in range(2):
                accs[h][...] = accs[h][...] * alpha[h] + pv[h]
            return tuple(m_out)

        n_full = (i * tq) // tk

        def two_blocks(t, m):
            kb = 2 * t
            scores(kb + 1, stb_ref)
            m = consume(kb, sta_ref, m, False)
            scores(kb + 2, sta_ref)
            return consume(kb + 1, stb_ref, m, False)

        def odd_tail(m):
            scores(n_full, stb_ref)
            m = consume(n_full - 1, sta_ref, m, False)
            return consume(n_full, stb_ref, m, True)

        def even_tail(m):
            return consume(n_full, sta_ref, m, True)

        scores(0, sta_ref)
        m0 = jnp.full((1, tq), NEG, F32)
        n_two = n_full // 2
        m = lax.fori_loop(0, n_two // 2, lambda u, m: two_blocks(2 * u + 1, two_blocks(2 * u, m)), (m0, m0))
        m = lax.cond(lax.rem(n_two, 2) == 1, lambda m: two_blocks(n_two - 1, m), lambda m: m, m)
        m = lax.cond(lax.rem(n_full, 2) == 1, odd_tail, even_tail, m)

        sub_o = lax.broadcasted_iota(jnp.int32, (PAIR, tq), 0)
        den = [accs[h][_spare(h):_spare(h) + 1, :] for h in range(2)]
        ot = jnp.where(sub_o < HEAD_DIM, acc0_ref[...] / den[0], acc1_ref[...] / den[1])
        o = jnp.transpose(ot)
        o_ref[...] = o.astype(BF16)
        gate = ag_ref[...].astype(F32)
        ma_ref[...] = (o * (gate * _sigmoid(gate))).astype(BF16)
        sub8 = lax.broadcasted_iota(jnp.int32, (8, tq), 0)
        lse_ref[...] = jnp.where(sub8 == 0, m[0] + jnp.log(den[0]), m[1] + jnp.log(den[1]))

        if n_carry:
            @pl.when((j == N_PAIRS - 1) & (i == s // tq - 1))
            def _():
                _gather_finish(carry_src, carry_dst, carry_sems[:2])
                for cp in _gather_own(carry_src, carry_dst, carry_sems[2]):
                    cp.wait()

    carried = [] if carry is None else list(carry)
    carry_sem = ([pltpu.SemaphoreType.DMA((GATHER_SEMS * n_carry,))] * 2
                 + [pltpu.SemaphoreType.DMA((n_carry,))]) if n_carry else []
    return pl.pallas_call(
        body, name="attn_fwd_gather" if n_carry else "attn_fwd", grid=(N_PAIRS, s // tq),
        in_specs=[pl.BlockSpec((tq, PAIR), lambda j, i: (i, j)),
                  pl.BlockSpec((s, PAIR), lambda j, i: (0, j)),
                  pl.BlockSpec((s, PAIR), lambda j, i: (0, j)),
                  pl.BlockSpec((s, 128), lambda j, i: (0, 0)),
                  pl.BlockSpec((tq, PAIR), lambda j, i: (i, j))] + [ANY] * n_carry,
        out_specs=[pl.BlockSpec((tq, PAIR), lambda j, i: (i, j)),
                   pl.BlockSpec((tq, PAIR), lambda j, i: (i, j)),
                   pl.BlockSpec((None, 8, tq), lambda j, i: (j, 0, i))] + [ANY] * n_carry,
        out_shape=[jax.ShapeDtypeStruct((s, ATTN_WIDTH), BF16), jax.ShapeDtypeStruct((s, ATTN_WIDTH), BF16),
                   jax.ShapeDtypeStruct((N_PAIRS, 8, s), F32)] + (_gather_shapes(*carried) if n_carry else []),
        scratch_shapes=[pltpu.VMEM((2, s, PAIR), BF16), pltpu.VMEM((2, PAIR, s), BF16),
                        pltpu.VMEM((PAIR, tq), F32), pltpu.VMEM((PAIR, tq), F32),
                        pltpu.VMEM((2, tk, tq), F32), pltpu.VMEM((2, tk, tq), F32)] + carry_sem,
        compiler_params=_params(("arbitrary", "arbitrary")),
    )(q, k, v, c, ag, *carried)


def _outproj_fwd(x, po, ma, wo, tm=512):
    s = x.shape[0]

    def body(x_ref, po_ref, ma_ref, wo_ref, xn_ref):
        xn_ref[...] = (x_ref[...]
                       + jnp.dot(po_ref[...], wo_ref[0:POOL_WIDTH, :], preferred_element_type=F32)
                       + jnp.dot(ma_ref[...], wo_ref[POOL_WIDTH:, :], preferred_element_type=F32))

    return pl.pallas_call(
        body, name="outproj_fwd", grid=(s // tm,),
        in_specs=[_row_spec(tm, D_MODEL), _row_spec(tm, 512), _row_spec(tm, 512),
                  _full_spec((D_MODEL, D_MODEL))],
        out_specs=_row_spec(tm, D_MODEL),
        out_shape=jax.ShapeDtypeStruct((s, D_MODEL), F32),
        compiler_params=_params(("parallel",)),
    )(x, po, ma, wo)


def _loss_head(x, po, ma, wo, tgt, g, tm=512):
    s = x.shape[0]

    def body(x_ref, po_ref, ma_ref, wo_ref, t_ref, g_ref, dx_ref, dg_ref, loss_ref):
        @pl.when(pl.program_id(0) == 0)
        def _():
            dg_ref[...] = jnp.zeros_like(dg_ref)
            loss_ref[...] = jnp.zeros_like(loss_ref)

        xf = (x_ref[...]
              + jnp.dot(po_ref[...], wo_ref[0:POOL_WIDTH, :], preferred_element_type=F32)
              + jnp.dot(ma_ref[...], wo_ref[POOL_WIDTH:, :], preferred_element_type=F32))
        r = lax.rsqrt(jnp.mean(xf * xf, axis=-1, keepdims=True) + RMS_EPS)
        xh = xf * r
        gg = g_ref[...]
        e = xh * gg - t_ref[...]
        loss_ref[...] += 0.5 * jnp.sum(jnp.mean(e * e, axis=-1, keepdims=True))
        dy = e * (1.0 / D_MODEL)
        u = dy * gg
        dx_ref[...] = r * (u - xh * jnp.mean(xh * u, axis=-1, keepdims=True))
        dg_ref[...] += jnp.sum(dy * xh, axis=0, keepdims=True)

    return pl.pallas_call(
        body, name="loss_head", grid=(s // tm,),
        in_specs=[_row_spec(tm, D_MODEL), _row_spec(tm, 512), _row_spec(tm, 512), _full_spec((D_MODEL, D_MODEL)),
                  _row_spec(tm, D_MODEL), _full_spec((1, D_MODEL))],
        out_specs=[_row_spec(tm, D_MODEL), _full_spec((1, D_MODEL)), _full_spec((8, 128))],
        out_shape=[jax.ShapeDtypeStruct((s, D_MODEL), F32), jax.ShapeDtypeStruct((1, D_MODEL), F32),
                   jax.ShapeDtypeStruct((8, 128), F32)],
        compiler_params=_params(("arbitrary",)),
    )(x, po, ma, wo, tgt, g)


def _outproj_bwd(dx, po, ma, ag, o, wo, head_sel, tm=512):
    s = dx.shape[0]

    def body(dx_ref, po_ref, ma_ref, ag_ref, o_ref, wo_ref, sel_ref, dmp_ref, do_ref, dag_ref, dt_ref, gwo_ref):
        @pl.when(pl.program_id(0) == 0)
        def _():
            gwo_ref[...] = jnp.zeros_like(gwo_ref)

        dxb = dx_ref[...].astype(BF16)
        dm = lax.dot_general(dxb, wo_ref[...], NT, preferred_element_type=F32)
        dmp_ref[...] = dm[:, 0:POOL_WIDTH].astype(BF16)
        dma = dm[:, POOL_WIDTH:]
        gate = ag_ref[...].astype(F32)
        of = o_ref[...].astype(F32)
        sg = _sigmoid(gate)
        do = dma * (gate * sg)
        do_ref[...] = do.astype(BF16)
        dag_ref[...] = (dma * of * (sg * (1.0 + gate * (1.0 - sg)))).astype(BF16)
        prod = do * of
        hi = prod.astype(BF16)
        lo = (prod - hi.astype(F32)).astype(BF16)
        dt_ref[...] = (lax.dot_general(sel_ref[...], hi, NT, preferred_element_type=F32)
                       + lax.dot_general(sel_ref[...], lo, NT, preferred_element_type=F32))
        gwo_ref[0:POOL_WIDTH, :] += lax.dot_general(po_ref[...], dxb, TN, preferred_element_type=F32)
        gwo_ref[POOL_WIDTH:, :] += lax.dot_general(ma_ref[...], dxb, TN, preferred_element_type=F32)

    act = jax.ShapeDtypeStruct((s, 512), BF16)
    return pl.pallas_call(
        body, name="outproj_bwd", grid=(s // tm,),
        in_specs=[_row_spec(tm, D_MODEL)] + [_row_spec(tm, 512)] * 4
                 + [_full_spec((D_MODEL, D_MODEL)), _full_spec((8, ATTN_WIDTH))],
        out_specs=[_row_spec(tm, 512)] * 3 + [pl.BlockSpec((8, tm), lambda i: (0, i)),
                                              _full_spec((D_MODEL, D_MODEL))],
        out_shape=[act, act, act, jax.ShapeDtypeStruct((8, s), F32),
                   jax.ShapeDtypeStruct((D_MODEL, D_MODEL), F32)],
        compiler_params=_params(("arbitrary",)),
    )(dx, po, ma, ag, o, wo, head_sel)


def _attn_bwd(q, k, v, do, c, lse, dt, carry=None, tq=256, chunk=512):
    s = q.shape[0]
    tk = 2 * tq
    nq = s // tq
    n_carry = 0 if carry is None else len(carry)

    def body(*refs):
        q_ref, k_ref, v_ref, do_ref, c_ref, lse_ref, dt_ref = refs[:7]
        carry_src = refs[7:7 + n_carry]
        dq_ref, dk_ref, dv_ref, dcs_ref, drs_ref = refs[7 + n_carry:12 + n_carry]
        carry_dst = refs[12 + n_carry:12 + 2 * n_carry]
        (qxt_ref, doxt_ref, qm_ref, dox_ref, dqt0_ref, dqt1_ref, dk0_ref, dk1_ref, dvx_ref,
         kx_ref, vx_ref, ktm_ref, sa_ref, sb_ref) = refs[12 + 2 * n_carry:26 + 2 * n_carry]
        carry_sems = refs[26 + 2 * n_carry:]
        carry_big = tuple(zip(carry_src, carry_dst))
        j = pl.program_id(0)
        b = pl.program_id(1)
        dqts = (dqt0_ref, dqt1_ref)
        dks = (dk0_ref, dk1_ref)

        if n_carry:
            @pl.when((j == 0) & (b == 0))
            def _():
                _scatter_start(carry_big, None, carry_sems)

        @pl.when(b == 0)
        def _():
            def prep(ch, carry):
                rows = pl.ds(pl.multiple_of(ch * chunk, chunk), chunk)
                qq = q_ref[rows, :].astype(F32)
                dd = do_ref[rows, :].astype(F32)
                qt = jnp.transpose(qq)
                ddt = jnp.transpose(dd)
                lane = lax.broadcasted_iota(jnp.int32, (chunk, PAIR), 1)
                sub = lax.broadcasted_iota(jnp.int32, (PAIR, chunk), 0)
                for h in range(2):
                    sp = _spare(h)
                    qm_ref[h, rows, :] = jnp.where(_own(lane, h), qq, jnp.where(lane == sp, 1.0, 0.0)).astype(BF16)
                    dox_ref[h, rows, :] = jnp.where(_own(lane, h), dd, 0.0).astype(BF16)
                    qx = jnp.where(_own(sub, h), qt, jnp.where((sub >= sp) & (sub < sp + 3), 1.0, 0.0))
                    for t, part in enumerate(_split3(-lse_ref[pl.ds(h, 1), rows])):
                        qx = jnp.where(sub == sp + 3 + t, part, qx)
                    qxt_ref[h, :, rows] = qx.astype(BF16)
                    dx = jnp.where(_own(sub, h), ddt, 0.0)
                    for t, part in enumerate(_split3(-dt_ref[pl.ds(h, 1), rows])):
                        dx = jnp.where(sub == sp + t, part, dx)
                    doxt_ref[h, :, rows] = dx.astype(BF16)
                return carry

            lax.fori_loop(0, s // chunk, prep, 0)
            for ref in dqts:
                ref[...] = jnp.zeros_like(ref)

        kk = k_ref[...].astype(F32)
        vv = v_ref[...].astype(F32)
        cc = c_ref[...]
        kt = jnp.transpose(kk)
        lane = lax.broadcasted_iota(jnp.int32, (tk, PAIR), 1)
        sub = lax.broadcasted_iota(jnp.int32, (PAIR, tk), 0)
        for h in range(2):
            sp = _spare(h)
            ccol = jnp.sum(jnp.where(lane == 2 * j + h, cc, 0.0), axis=1, keepdims=True)
            kx = jnp.where(_own(lane, h), kk, jnp.where((lane >= sp + 3) & (lane < sp + 6), 1.0, 0.0))
            for t, part in enumerate(_split3(-ccol)):
                kx = jnp.where(lane == sp + t, part, kx)
            kx_ref[h] = kx.astype(BF16)
            vx_ref[h] = jnp.where(_own(lane, h), vv, jnp.where((lane >= sp) & (lane < sp + 3), 1.0, 0.0)).astype(BF16)
            ktm_ref[h] = jnp.where(_own(sub, h), kt, jnp.where(sub == sp, 1.0, 0.0)).astype(BF16)
        for ref in dks:
            ref[...] = jnp.zeros_like(ref)
        dvx_ref[...] = jnp.zeros_like(dvx_ref)

        def cols_of(i):
            return pl.ds(pl.multiple_of(i * tq, tq), tq)

        def scores(i, dst):
            cols = cols_of(jnp.minimum(i, nq - 1))
            for h in range(2):
                dst[h] = jnp.dot(kx_ref[h], qxt_ref[h, :, cols], preferred_element_type=F32)
                dst[2 + h] = jnp.dot(vx_ref[h], doxt_ref[h, :, cols], preferred_element_type=F32)

        def consume(i, src, masked):
            cols = cols_of(i)
            for h in range(2):
                arg = src[h]
                if masked:
                    kpos = b * tk + lax.broadcasted_iota(jnp.int32, (tk, tq), 0)
                    qpos = i * tq + lax.broadcasted_iota(jnp.int32, (tk, tq), 1)
                    arg = jnp.where(qpos >= kpos, arg, NEG)
                pt = jnp.exp(arg)
                dst = (pt * src[2 + h]).astype(BF16)
                dvx_ref[...] += jnp.dot(pt.astype(BF16), dox_ref[h, cols, :], preferred_element_type=F32)
                dks[h][...] += jnp.dot(dst, qm_ref[h, cols, :], preferred_element_type=F32)
                dqts[h][:, cols] += jnp.dot(ktm_ref[h], dst, preferred_element_type=F32)

        i0 = 2 * b

        def two_blocks(t, masked):
            i = i0 + 2 * t
            scores(i + 1, sb_ref)
            consume(i, sa_ref, masked)
            scores(i + 2, sa_ref)
            consume(i + 1, sb_ref, masked)

        def four_blocks(u, carry):
            two_blocks(1 + 2 * u, False)
            two_blocks(2 + 2 * u, False)
            return carry

        scores(i0, sa_ref)
        two_blocks(0, True)
        n_open = (nq - i0) // 2 - 1
        lax.fori_loop(0, n_open // 2, four_blocks, 0)

        @pl.when(lax.rem(n_open, 2) == 1)
        def _():
            two_blocks(n_open, False)

        dv_ref[...] = dvx_ref[...].astype(BF16)
        dk_ref[...] = jnp.where(lane < HEAD_DIM, dk0_ref[...], dk1_ref[...]).astype(BF16)
        dcs_ref[...] = jnp.where(lane == _spare(0), dk0_ref[...], jnp.where(lane == _spare(1), dk1_ref[...], 0.0))

        @pl.when(b == s // tk - 1)
        def _():
            sub8 = lax.broadcasted_iota(jnp.int32, (8, s), 0)
            drs_ref[...] = jnp.where(sub8 == 0, dqt0_ref[_spare(0):_spare(0) + 1, :],
                                     jnp.where(sub8 == 1, dqt1_ref[_spare(1):_spare(1) + 1, :], 0.0))
            sub_c = lax.broadcasted_iota(jnp.int32, (PAIR, chunk), 0)
            for ch in range(s // chunk):
                rows = pl.ds(ch * chunk, chunk)
                both = jnp.where(sub_c < HEAD_DIM, dqt0_ref[:, rows], dqt1_ref[:, rows])
                dq_ref[rows, :] = (jnp.transpose(both) * Q_SCALE).astype(BF16)

        if n_carry:
            @pl.when((j == N_PAIRS - 1) & (b == s // tk - 1))
            def _():
                _scatter_finish(carry_big, None, carry_sems)

    act = jax.ShapeDtypeStruct((s, ATTN_WIDTH), BF16)
    pair_rows = pl.BlockSpec((s, PAIR), lambda j, b: (0, j))
    pair_blk = pl.BlockSpec((tk, PAIR), lambda j, b: (b, j))
    stat = pl.BlockSpec((None, 2, s), lambda j, b: (j, 0, 0))
    carried = [] if carry is None else list(carry)
    carry_sem = [pltpu.SemaphoreType.DMA((SCATTER_SEMS,))] * 2 if n_carry else []
    return pl.pallas_call(
        body, name="attn_bwd_scatter" if n_carry else "attn_bwd", grid=(N_PAIRS, s // tk),
        in_specs=[pair_rows, pair_blk, pair_blk, pair_rows,
                  pl.BlockSpec((tk, 128), lambda j, b: (b, 0)),
                  pl.BlockSpec((None, 8, s), lambda j, b: (j, 0, 0)), stat] + [ANY] * n_carry,
        out_specs=[pair_rows, pair_blk, pair_blk, pair_blk,
                   pl.BlockSpec((None, 8, s), lambda j, b: (j, 0, 0))] + [ANY] * n_carry,
        out_shape=[act, act, act, jax.ShapeDtypeStruct((s, N_PAIRS * 128), F32),
                   jax.ShapeDtypeStruct((N_PAIRS, 8, s), F32)] + (_scatter_shapes(*carried) if n_carry else []),
        scratch_shapes=[pltpu.VMEM((2, PAIR, s), BF16), pltpu.VMEM((2, PAIR, s), BF16),
                        pltpu.VMEM((2, s, PAIR), BF16), pltpu.VMEM((2, s, PAIR), BF16),
                        pltpu.VMEM((PAIR, s), F32), pltpu.VMEM((PAIR, s), F32),
                        pltpu.VMEM((tk, PAIR), F32), pltpu.VMEM((tk, PAIR), F32), pltpu.VMEM((tk, PAIR), F32),
                        pltpu.VMEM((2, tk, PAIR), BF16), pltpu.VMEM((2, tk, PAIR), BF16),
                        pltpu.VMEM((2, PAIR, tk), BF16),
                        pltpu.VMEM((4, tk, tq), F32), pltpu.VMEM((4, tk, tq), F32)] + carry_sem,
        compiler_params=_params(("arbitrary", "arbitrary")),
    )(q, k, v, do, c, lse, dt, *carried)


def _forget_bwd(dcs, drs, lf, tm=512):
    s = lf.shape[0]
    n = s // tm

    def body(dcs_ref, drs_ref, lf_ref, dpf_ref, dfb_ref, carry_ref):
        @pl.when(pl.program_id(0) == 0)
        def _():
            carry_ref[...] = jnp.zeros_like(carry_ref)
            dfb_ref[...] = jnp.zeros_like(dfb_ref)

        lane = lax.broadcasted_iota(jnp.int32, (tm, 128), 1)
        dc = drs_ref[...]
        for pj in range(N_PAIRS):
            blk = dcs_ref[:, pj * 128:(pj + 1) * 128]
            for h in range(2):
                head = 2 * pj + h
                moved = pltpu.roll(blk, (head - _spare(h)) % 128, 1) if head != _spare(h) else blk
                dc = dc - jnp.where(lane == head, moved, 0.0)
        dlf = _scan_rows(dc, reverse=True) + carry_ref[0:1, :]
        carry_ref[...] = dlf[0:8, :]
        dz = jnp.where(lane < 8, dlf * (1.0 - jnp.exp(lf_ref[...])), 0.0)
        dpf_ref[...] = dz.astype(BF16)
        dfb_ref[...] += jnp.sum(dz, axis=0, keepdims=True)

    return pl.pallas_call(
        body, name="forget_bwd", grid=(n,),
        in_specs=[pl.BlockSpec((tm, N_PAIRS * 128), lambda i: (n - 1 - i, 0)),
                  pl.BlockSpec((tm, 128), lambda i: (n - 1 - i, 0)),
                  pl.BlockSpec((tm, 128), lambda i: (n - 1 - i, 0))],
        out_specs=[pl.BlockSpec((tm, 128), lambda i: (n - 1 - i, 0)), _full_spec((1, 128))],
        out_shape=[jax.ShapeDtypeStruct((s, 128), BF16), jax.ShapeDtypeStruct((1, 128), F32)],
        scratch_shapes=[pltpu.VMEM((8, 128), F32)],
        compiler_params=_params(("arbitrary",)),
    )(dcs, drs, lf)


def _pool_bwd(pu, pg, dmp, pw, ps, tm=512):
    s = pu.shape[0]
    hb = tm // HALO
    n = s // tm
    last_halo = s // HALO - 1

    def body(pu_ref, halo_ref, pg_ref, dmp_ref, pgn_ref, dmpn_ref, pw_ref, ps_ref,
             dpu_ref, dpg_ref, gpw_ref, gps_ref):
        i = pl.program_id(0)

        @pl.when(i == 0)
        def _():
            gpw_ref[...] = jnp.zeros_like(gpw_ref)
            gps_ref[...] = jnp.zeros_like(gps_ref)

        u = pu_ref[...].astype(F32)
        halo = jnp.where(i == 0, 0.0, halo_ref[...].astype(F32))
        d = _pool_delta(u, halo, i * tm)
        e_parts, dd_parts = [], []
        for gi in range(len(POOL_WINDOWS)):
            cols = slice(gi * GROUP_DIM, (gi + 1) * GROUP_DIM)
            wg = pw_ref[gi]
            scale = ps_ref[:, cols]
            db = d[gi].astype(BF16)
            z = jnp.dot(db, wg, preferred_element_type=F32)
            gate = pg_ref[:, cols].astype(F32)
            sg = _sigmoid(gate)
            dm = dmp_ref[:, cols].astype(F32)
            dy = dm * (gate * sg)
            dpg_ref[:, cols] = (dm * (z * scale) * (sg * (1.0 + gate * (1.0 - sg)))).astype(BF16)
            gps_ref[:, cols] += jnp.sum(dy * z, axis=0, keepdims=True)
            dz = (dy * scale).astype(BF16)
            gpw_ref[gi] += lax.dot_general(db, dz, TN, preferred_element_type=F32)
            dd = lax.dot_general(dz, wg, NT, preferred_element_type=F32)
            gate_n = pgn_ref[:, cols].astype(F32)
            dz_n = (dmpn_ref[:, cols].astype(F32) * (gate_n * _sigmoid(gate_n)) * scale).astype(BF16)
            dd_n = lax.dot_general(dz_n, wg, NT, preferred_element_type=F32)
            dd_n = jnp.where(i == n - 1, 0.0, dd_n)
            dd_parts.append(dd)
            e_parts.append(jnp.concatenate([_over_count(dd, i * tm, gi), dd_n * (1.0 / POOL_WINDOWS[gi])], axis=0))
        lead = _window_sums(jnp.concatenate(e_parts, axis=1), forward=True)
        for gi in range(len(POOL_WINDOWS)):
            cols = slice(gi * GROUP_DIM, (gi + 1) * GROUP_DIM)
            dpu_ref[:, cols] = (lead[gi][0:tm, :] - dd_parts[gi]).astype(BF16)

    act = jax.ShapeDtypeStruct((s, 512), BF16)
    prev_halo = pl.BlockSpec((HALO, 512), lambda i: (jnp.maximum(i * hb - 1, 0), 0))
    next_halo = pl.BlockSpec((HALO, 512), lambda i: (jnp.minimum((i + 1) * hb, last_halo), 0))
    return pl.pallas_call(
        body, name="pool_bwd", grid=(n,),
        in_specs=[_row_spec(tm, 512), prev_halo, _row_spec(tm, 512), _row_spec(tm, 512), next_halo, next_halo,
                  _full_spec((4, GROUP_DIM, GROUP_DIM)), _full_spec((1, 512))],
        out_specs=[_row_spec(tm, 512), _row_spec(tm, 512), _full_spec((4, GROUP_DIM, GROUP_DIM)),
                   _full_spec((1, 512))],
        out_shape=[act, act, jax.ShapeDtypeStruct((4, GROUP_DIM, GROUP_DIM), F32),
                   jax.ShapeDtypeStruct((1, 512), F32)],
        compiler_params=_params(("arbitrary",)),
    )(pu, pu, pg, dmp, pg, dmp, pw, ps)


def _inproj_bwd_x(dsec, wt, x, g, dxo, exchange=None, scatter=None, tm=512):
    s = x.shape[0]
    n_steps = s // tm
    carried = list(exchange or scatter or ())
    n_carry = len(carried)

    def body(*refs):
        d_refs = refs[0:7]
        wt_ref, x_ref, g_ref, dxo_ref = refs[7:11]
        carry_src = refs[11:11 + n_carry]
        dxi_ref, dg_ref = refs[11 + n_carry:13 + n_carry]
        carry_dst = refs[13 + n_carry:13 + 2 * n_carry]
        sems = refs[13 + 2 * n_carry:]
        step = pl.program_id(0)

        def comm(phase):
            if exchange is not None:
                (_exchange_start if phase == 0 else _exchange_finish)(carry_src, carry_dst, None, sems)
            else:
                big = tuple(zip(carry_src[:2], carry_dst[:2]))
                (_scatter_start if phase == 0 else _scatter_finish)(big, (carry_src[2], carry_dst[2]), sems)

        @pl.when(step == 0)
        def _():
            dg_ref[...] = jnp.zeros_like(dg_ref)
            if n_carry:
                comm(0)

        dh = None
        for d_ref, (off, ncol) in zip(d_refs, SECTIONS):
            t = jnp.dot(d_ref[...], wt_ref[off:off + ncol, :], preferred_element_type=F32)
            dh = t if dh is None else dh + t
        xf = x_ref[...]
        r = lax.rsqrt(jnp.mean(xf * xf, axis=-1, keepdims=True) + RMS_EPS)
        xh = xf * r
        u = dh * g_ref[...]
        dxi_ref[...] = dxo_ref[...] + r * (u - xh * jnp.mean(xh * u, axis=-1, keepdims=True))
        dg_ref[...] += jnp.sum(dh * xh, axis=0, keepdims=True)

        if n_carry:
            @pl.when(step == n_steps - 1)
            def _():
                comm(1)

    if exchange is not None:
        name, carry_shapes, n_sems = "inproj_bwd_x_exchange", _exchange_shapes(*exchange), 3
    elif scatter is not None:
        name, n_sems = "inproj_bwd_x_scatter", SCATTER_SEMS
        carry_shapes = _scatter_shapes(*scatter[:2]) + [jax.ShapeDtypeStruct((3, SMALL_HALF, 128), F32)]
    else:
        name, carry_shapes, n_sems = "inproj_bwd_x", [], 0
    return pl.pallas_call(
        body, name=name, grid=(n_steps,),
        in_specs=[_row_spec(tm, ncol) for _, ncol in SECTIONS]
                 + [_full_spec((IN_COLS_PAD, D_MODEL)), _row_spec(tm, D_MODEL), _full_spec((1, D_MODEL)),
                    _row_spec(tm, D_MODEL)] + [ANY] * n_carry,
        out_specs=[_row_spec(tm, D_MODEL), _full_spec((1, D_MODEL))] + [ANY] * n_carry,
        out_shape=[jax.ShapeDtypeStruct((s, D_MODEL), F32), jax.ShapeDtypeStruct((1, D_MODEL), F32)] + carry_shapes,
        scratch_shapes=[pltpu.SemaphoreType.DMA((n_sems,))] * 2 if n_carry else [],
        compiler_params=_params(("arbitrary",)),
    )(*dsec, wt, x, g, dxo, *carried)


def _inproj_bwd_w(h, dsec, tm=512):
    s = h.shape[0]
    n_steps = s // tm

    def body(*refs):
        h_ref = refs[0]
        d_refs = refs[1:8]
        out_ref, gw_ref = refs[8:]
        step = pl.program_id(0)

        @pl.when(step == 0)
        def _():
            gw_ref[...] = jnp.zeros_like(gw_ref)

        hh = h_ref[...]
        for d_ref, (off, ncol) in zip(d_refs, SECTIONS):
            gw_ref[:, off:off + ncol] += lax.dot_general(hh, d_ref[...], TN, preferred_element_type=F32)

        @pl.when(step == n_steps - 1)
        def _():
            for k in range(N_CHIPS):
                out_ref[k] = gw_ref[:, k * SHARD_COLS:(k + 1) * SHARD_COLS]

    return pl.pallas_call(
        body, name="inproj_bwd_w", grid=(n_steps,),
        in_specs=[_row_spec(tm, D_MODEL)] + [_row_spec(tm, ncol) for _, ncol in SECTIONS],
        out_specs=_full_spec((N_CHIPS, D_MODEL, SHARD_COLS)),
        out_shape=jax.ShapeDtypeStruct((N_CHIPS, D_MODEL, SHARD_COLS), F32),
        scratch_shapes=[pltpu.VMEM((D_MODEL, IN_COLS_PAD), F32)],
        compiler_params=_params(("arbitrary",), vmem_mb=60),
    )(h, *dsec)


def _elementwise(fn, name, n_out, arrays, tm):
    shape = arrays[0].shape
    rest = shape[1:]
    n_in = len(arrays)

    def body(*refs):
        outs = fn(*[r[...] for r in refs[:n_in]])
        for r, val in zip(refs[n_in:], outs):
            r[...] = val

    spec = pl.BlockSpec((tm,) + rest, lambda i: (i,) + (0,) * len(rest))
    return pl.pallas_call(
        body, name=name, grid=(shape[0] // tm,),
        in_specs=[spec] * n_in, out_specs=[spec] * n_out, out_shape=[jax.ShapeDtypeStruct(shape, F32)] * n_out,
        compiler_params=_params(("parallel",)),
    )(*arrays)


def _add2(a, b, name, tm):
    return _elementwise(lambda p, q: (p + q,), name, 1, [a, b], tm)[0]


def _chip_sum(place, g, r, name, tm):
    n_layers, _, rows, cols = g.shape
    half = rows // 2
    nb = half // tm

    def body(place_ref, g_ref, r_ref, o_ref):
        o_ref[...] = (g_ref[...] + r_ref[...]).astype(BF16)

    blk = (None, None, tm, cols)
    return pl.pallas_call(
        body, name=name,
        grid_spec=pltpu.PrefetchScalarGridSpec(
            num_scalar_prefetch=1, grid=(n_layers, N_CHIPS, nb),
            in_specs=[pl.BlockSpec(blk, lambda l, k, i, p: (l, k, p[1] * nb + i, 0)),
                      pl.BlockSpec(blk, lambda l, k, i, p: (l, k, i, 0))],
            out_specs=pl.BlockSpec(blk, lambda l, k, i, p: (l, k, i, 0))),
        out_shape=jax.ShapeDtypeStruct((n_layers, N_CHIPS, half, cols), BF16),
        compiler_params=_params(("parallel", "parallel", "parallel")),
    )(place, g, r)


def _mesh_sum(place, g, r, got, into, layer, name, tm):
    _, _, rows, cols = g.shape
    half = rows // 2
    nb = half // tm

    def body(place_ref, g_ref, r_ref, got_ref, into_ref, o_ref):
        own = g_ref[...] + r_ref[...]
        o_ref[...] = (own + got_ref[0].astype(F32)) + (got_ref[1].astype(F32) + got_ref[2].astype(F32))

    blk = (None, None, tm, cols)
    return pl.pallas_call(
        body, name=name,
        grid_spec=pltpu.PrefetchScalarGridSpec(
            num_scalar_prefetch=1, grid=(nb,),
            in_specs=[pl.BlockSpec(blk, lambda i, p: (0, p[0], p[1] * nb + i, 0)),
                      pl.BlockSpec(blk, lambda i, p: (0, p[0], i, 0)),
                      pl.BlockSpec((None, 3, tm, cols), lambda i, p: (0, 0, i, 0)),
                      ANY],
            out_specs=pl.BlockSpec((None, tm, cols), lambda i, p: (layer, p[1] * nb + i, 0))),
        out_shape=jax.ShapeDtypeStruct(into.shape, F32),
        input_output_aliases={4: 0},
        compiler_params=_params(("parallel",)),
    )(place, g, r, got, into)


def _mesh_sum_small(place, ps, got, tm):
    nb = SMALL_HALF // tm

    def body(place_ref, ps_ref, got_ref, o_ref):
        o_ref[...] = (ps_ref[...] + got_ref[0]) + (got_ref[1] + got_ref[2])

    return pl.pallas_call(
        body, name="mesh_sum_small",
        grid_spec=pltpu.PrefetchScalarGridSpec(
            num_scalar_prefetch=1, grid=(nb,),
            in_specs=[pl.BlockSpec((tm, 128), lambda i, p: (p[1] * nb + i, 0)),
                      pl.BlockSpec((3, tm, 128), lambda i, p: (0, i, 0))],
            out_specs=pl.BlockSpec((tm, 128), lambda i, p: (p[1] * nb + i, 0))),
        out_shape=jax.ShapeDtypeStruct((SMALL_ROWS, 128), F32),
        compiler_params=_params(("parallel",)),
    )(place, ps, got)


def _adamw(g, w, m, v, name, tm):
    def fn(g, w, m, v):
        m = ADAM_B1 * m + (1.0 - ADAM_B1) * g
        v = ADAM_B2 * v + (1.0 - ADAM_B2) * (g * g)
        m_hat = m / (1.0 - ADAM_B1 ** ADAM_STEP)
        v_hat = v / (1.0 - ADAM_B2 ** ADAM_STEP)
        delta = -ADAM_LR * (m_hat / (jnp.sqrt(v_hat) + ADAM_EPS) + ADAM_WD * w)
        return g, delta, m, v

    return _elementwise(fn, name, 4, [g, w, m, v], tm)


def _position():
    x, y, c = lax.axis_index("x"), lax.axis_index("y"), lax.axis_index("c")
    other_chips = [(x, 1 - y), (1 - x, y), (1 - x, 1 - y)]
    return x, y, c, other_chips


def _remote(src, dst, sems, k, to):
    send_sems, recv_sems = sems
    return pltpu.make_async_remote_copy(src_ref=src, dst_ref=dst, send_sem=send_sems.at[k],
                                        recv_sem=recv_sems.at[k], device_id=to, device_id_type=MESH)


def _comm_call(body, name, n_in, out_shape, n_remote):
    return pl.pallas_call(
        body, name=name, in_specs=[ANY] * n_in, out_specs=[ANY] * len(out_shape), out_shape=out_shape,
        scratch_shapes=[pltpu.SemaphoreType.DMA((n_remote,)), pltpu.SemaphoreType.DMA((n_remote,))],
    )


def _halves(ref_rows, c):
    half = ref_rows // 2
    return pl.ds(half * c, half), pl.ds(half * (1 - c), half)


def _gather_weights(wi):
    def body(wi_ref, gi_ref, send_sems, recv_sems):
        _gather_start((wi_ref,), (gi_ref,), (send_sems, recv_sems))
        _gather_finish((wi_ref,), (gi_ref,), (send_sems, recv_sems))

    return _comm_call(body, "gather_weights", 1, _gather_shapes(wi), GATHER_SEMS)(wi)[0]


GATHER_SEMS = 6
SCATTER_SEMS = 9


def _gather_shapes(*arrays):
    return [jax.ShapeDtypeStruct((a.shape[0], N_CHIPS) + a.shape[1:], a.dtype) for a in arrays]


def _gather_sends(srcs, dsts, sems):
    x, y, c, chips = _position()
    out = []
    for k, (cx, cy) in enumerate(chips):
        for a, (src, dst) in enumerate(zip(srcs, dsts)):
            mine, _ = _halves(src.shape[1], c)
            out.append(_remote(src.at[:, mine], dst.at[:, k + 1, mine], sems, len(srcs) * k + a, (cx, cy, c)))
    return out


def _gather_own(srcs, dsts, local_sems):
    return [pltpu.make_async_copy(src, dst.at[:, 0], local_sems.at[a]) for a, (src, dst) in enumerate(zip(srcs, dsts))]


def _gather_start(srcs, dsts, sems):
    for cp in _gather_sends(srcs, dsts, sems):
        cp.start()


def _gather_finish(srcs, dsts, sems):
    x, y, c, _ = _position()
    n = len(srcs)
    forwards = []
    for k in range(3):
        for a, dst in enumerate(dsts):
            mine, _ = _halves(dst.shape[2], c)
            blk = dst.at[:, k + 1, mine]
            _remote(blk, blk, sems, n * k + a, (x, y, c)).wait_recv()
            forwards.append(_remote(blk, blk, sems, 3 * n + n * k + a, (x, y, 1 - c)))
            forwards[-1].start()
    for k in range(3):
        for a, dst in enumerate(dsts):
            _, other = _halves(dst.shape[2], c)
            blk = dst.at[:, k + 1, other]
            _remote(blk, blk, sems, 3 * n + n * k + a, (x, y, c)).wait_recv()
    for cp in _gather_sends(srcs, dsts, sems) + forwards:
        cp.wait_send()


def _scatter_sends(big, small, sems):
    x, y, c, chips = _position()
    n = len(big) + (small is not None)
    out = []
    for k, (cx, cy) in enumerate(chips):
        ck = 2 * cx + cy
        for a, (src, dst) in enumerate(big):
            out.append(_remote(src.at[:, ck], dst.at[:, k], sems, n * k + a, (cx, cy, c)))
        if small is not None:
            half, _ = _halves(SMALL_ROWS, c)
            out.append(_remote(small[0].at[half], small[1].at[k], sems, n * k + n - 1, (cx, cy, c)))
    return out


def _scatter_start(big, small, sems):
    for cp in _scatter_sends(big, small, sems):
        cp.start()


def _scatter_finish(big, small, sems):
    x, y, c, _ = _position()
    n = len(big) + (small is not None)
    for k in range(3):
        landing = [dst.at[:, k] for _, dst in big] + ([small[1].at[k]] if small is not None else [])
        for a, blk in enumerate(landing):
            _remote(blk, blk, sems, n * k + a, (x, y, c)).wait_recv()
    for cp in _scatter_sends(big, small, sems):
        cp.wait_send()


def _exchange_shapes(gi, go):
    return [jax.ShapeDtypeStruct(a.shape[:2] + (a.shape[2] // 2, a.shape[3]), F32) for a in (gi, go)]


def _exchange_copies(srcs, dsts, small, sems):
    x, y, c, _ = _position()
    cps = [_remote(src.at[:, :, _halves(src.shape[2], c)[1]], dst, sems, a, (x, y, 1 - c))
           for a, (src, dst) in enumerate(zip(srcs, dsts))]
    if small is not None:
        cps.append(_remote(small[0], small[1], sems, len(srcs), (x, y, 1 - c)))
    return cps


def _exchange_start(srcs, dsts, small, sems):
    for cp in _exchange_copies(srcs, dsts, small, sems):
        cp.start()


def _exchange_finish(srcs, dsts, small, sems):
    for cp in _exchange_copies(srcs, dsts, small, sems):
        cp.wait()


def _exchange_sibling(gi, go, sm):
    def body(gi_ref, go_ref, sm_ref, ri_ref, ro_ref, rs_ref, send_sems, recv_sems):
        args = ((gi_ref, go_ref), (ri_ref, ro_ref), (sm_ref, rs_ref), (send_sems, recv_sems))
        _exchange_start(*args)
        _exchange_finish(*args)

    out_shape = _exchange_shapes(gi, go) + [jax.ShapeDtypeStruct(sm.shape, F32)]
    return _comm_call(body, "exchange_sibling", 3, out_shape, 3)(gi, go, sm)


def _scatter_shapes(pi, po):
    return [jax.ShapeDtypeStruct((a.shape[0], 3) + a.shape[2:], a.dtype) for a in (pi, po)]


def _join_sibling(fi, fo, fs, late):
    def body(fi_ref, fo_ref, fs_ref, late_ref, oi_ref, oo_ref, os_ref, sum_ref, send_sems, recv_sems,
             sib_ref, chip_ref, got_ref):
        x, y, c, chips = _position()
        sems = (send_sems, recv_sems)
        sent, got = [], []
        for a, (src, dst) in enumerate(((fi_ref, oi_ref), (fo_ref, oo_ref))):
            mine, other = _halves(src.shape[1], c)
            sent.append(_remote(src.at[:, mine], dst.at[:, mine], sems, a, (x, y, 1 - c)))
            got.append(_remote(dst.at[:, other], dst.at[:, other], sems, a, (x, y, c)))
        mine, other = _halves(SMALL_ROWS, c)
        sent.append(_remote(fs_ref.at[mine], os_ref.at[mine], sems, 2, (x, y, 1 - c)))
        got.append(_remote(os_ref.at[other], os_ref.at[other], sems, 2, (x, y, c)))
        for cp in sent:
            cp.start()

        to_sibling = _remote(late_ref, sib_ref, sems, 3, (x, y, 1 - c))
        to_sibling.start()
        to_sibling.wait()
        chip_ref[...] = late_ref[...] + sib_ref[...]
        to_chips = [_remote(chip_ref, got_ref.at[k], sems, 4 + k, (cx, cy, c)) for k, (cx, cy) in enumerate(chips)]
        for cp in to_chips:
            cp.start()
        for cp in to_chips:
            cp.wait()
        sum_ref[...] = (chip_ref[...] + got_ref[0]) + (got_ref[1] + got_ref[2])

        for cp in got:
            cp.wait_recv()
        for cp in sent:
            cp.wait_send()

    vmem = pl.BlockSpec(memory_space=pltpu.VMEM)
    return pl.pallas_call(
        body, name="join_sibling", in_specs=[ANY] * 3 + [vmem], out_specs=[ANY] * 3 + [vmem],
        out_shape=[jax.ShapeDtypeStruct(a.shape, F32) for a in (fi, fo, fs, late)],
        scratch_shapes=[pltpu.SemaphoreType.DMA((7,)), pltpu.SemaphoreType.DMA((7,)),
                        pltpu.VMEM(late.shape, F32), pltpu.VMEM(late.shape, F32), pltpu.VMEM((3,) + late.shape, F32)],
        input_output_aliases={0: 0, 1: 1, 2: 2},
    )(fi, fo, fs, late)


SMALL_SIZES = (DEPTH * D_MODEL, D_MODEL, DEPTH * POOL_WIDTH, 128, DEPTH * 4 * GROUP_DIM * GROUP_DIM)
LOSS_SLOT = DEPTH * D_MODEL + D_MODEL + DEPTH * POOL_WIDTH + DEPTH * 8


def _pack_small(norm_g, final_g, pool_scale, forget_bias, pool_w, loss=None):
    fb = forget_bias.reshape(-1)
    if loss is not None:
        fb = jnp.concatenate([fb, loss.reshape(1)])
    fb = jnp.pad(fb, (0, 128 - fb.size))
    flat = jnp.concatenate([norm_g.reshape(-1), final_g.reshape(-1), pool_scale.reshape(-1), fb,
                            pool_w.reshape(-1)])
    return jnp.pad(flat, (0, SMALL_ROWS * 128 - flat.size)).reshape(SMALL_ROWS, 128)


def _unpack_small(packed):
    flat = packed.reshape(-1)
    offs = [0]
    for n in SMALL_SIZES:
        offs.append(offs[-1] + n)
    norm_g, final_g, pool_scale, fb, pool_w = [flat[offs[i]:offs[i + 1]] for i in range(5)]
    return (norm_g.reshape(DEPTH, D_MODEL), final_g, pool_scale.reshape(DEPTH, POOL_WIDTH),
            fb[:DEPTH * 8].reshape(DEPTH, 8), pool_w.reshape(DEPTH, 4, GROUP_DIM, GROUP_DIM))


def kernel(x, norm_g, w_in, forget_bias, pool_w, pool_scale, w_out, final_g, loss_target, m_norm_g, m_w_in, m_forget_bias, m_pool_w, m_pool_scale, m_w_out, m_final_g, v_norm_g, v_w_in, v_forget_bias, v_pool_w, v_pool_scale, v_w_out, v_final_g):
    x0 = x[0]
    tgt = loss_target[0]
    s = x0.shape[0]

    me = 2 * lax.axis_index("x") + lax.axis_index("y")
    place = jnp.stack([me, lax.axis_index("c")]).astype(jnp.int32)
    wi_b, wo_b = w_in.astype(BF16), w_out.astype(BF16)

    def whole_w_out(got):
        return jnp.concatenate([lax.dynamic_index_in_dim(got, jnp.bitwise_xor(k, me), axis=1, keepdims=False)[0]
                                for k in range(N_CHIPS)], axis=0)

    pool_w_b = pool_w.astype(BF16)
    fb_pad = jnp.pad(forget_bias, ((0, 0), (0, 128 - forget_bias.shape[1])))
    head_sel = (jnp.arange(ATTN_WIDTH)[None, :] // HEAD_DIM == jnp.arange(8)[:, None]).astype(BF16)

    saved, weights = [], []
    xl = x0
    got_i = _gather_weights(wi_b[0:1])
    for l in range(DEPTH):
        pu, pg, q, k, v, ag, lf, c, h, po, w_t = _inproj_fwd(
            place, xl, norm_g[l][None], None if l > 0 else wi_b[0], got_i, fb_pad[l][None],
            pool_w_b[l], pool_scale[l][None])
        if l + 1 < DEPTH:
            ma, o, lse, got_o, got_i = _attn_fwd(q, k, v, c, ag, carry=(wo_b[l:l + 1], wi_b[l + 1:l + 2]))
        else:
            ma, o, lse, got_o = _attn_fwd(q, k, v, c, ag, carry=(wo_b[l:l + 1],))
        w_o = whole_w_out(got_o)
        weights.append((w_t, w_o))
        saved.append((xl, h, pu, pg, q, k, v, ag, lf, c, po, ma, o, lse))
        if l + 1 < DEPTH:
            xl = _outproj_fwd(xl, po, ma, w_o)
    dx, g_final, loss_part = _loss_head(xl, po, ma, w_o, tgt, final_g[None])

    g_norm, g_fb, g_pw, g_ps = [], [], [], []
    parts = {}
    pending = None
    for l in reversed(range(DEPTH)):
        xl, h, pu, pg, q, k, v, ag, lf, c, po, ma, o, lse = saved[l]
        w_t, w_o = weights[l]
        dmp, do, dag, dt, gwo = _outproj_bwd(dx, po, ma, ag, o, w_o, head_sel)
        if pending is None:
            dq, dk, dv, dcs, drs = _attn_bwd(q, k, v, do, c, lse, dt.reshape(N_PAIRS, 2, s))
        else:
            dq, dk, dv, dcs, drs, qi, qo = _attn_bwd(q, k, v, do, c, lse, dt.reshape(N_PAIRS, 2, s), carry=pending)
            parts[l + 1] += (qi, qo)
        drs = jnp.pad(drs[:, 0:2, :].reshape(8, s).T, ((0, 0), (0, 120)))
        dpf, dfb = _forget_bwd(dcs, drs, lf)
        dpu, dpg, gpw, gps = _pool_bwd(pu, pg, dmp, pool_w_b[l], pool_scale[l][None])
        dsec = (dpu, dpg, dq, dk, dv, dag, dpf)
        gwi = _inproj_bwd_w(h, dsec)
        g_fb.append(dfb[0, :8])
        g_pw.append(gpw)
        g_ps.append(gps[0])
        send_i = gwi[None]
        send_o = gwo.reshape(1, N_CHIPS, SHARD_ROWS, D_MODEL)
        if l > 0:
            dx, gn, ri, ro = _inproj_bwd_x(dsec, w_t, xl, norm_g[l][None], dx, exchange=(send_i, send_o))
            g_norm.append(gn[0])
        else:
            g_norm.append(jnp.zeros((D_MODEL,), F32))
            g_norm, g_fb, g_pw, g_ps = [t[::-1] for t in (g_norm, g_fb, g_pw, g_ps)]
            send_s = _pack_small(jnp.stack(g_norm), g_final[0], jnp.stack(g_ps), jnp.stack(g_fb), jnp.stack(g_pw),
                                 loss=loss_part[0, 0])
            ri, ro, rs = _exchange_sibling(send_i, send_o, send_s)
        pending = (_chip_sum(place, send_i, ri, "chip_sum_w_in", 256),
                   _chip_sum(place, send_o, ro, "chip_sum_w_out", SHARD_ROWS // 2))
        parts[l] = (send_i, ri, send_o, ro)
    ps_ = _add2(send_s, rs, "chip_sum_small", SMALL_ROWS // 4)
    dx, gn0, qi, qo, qs = _inproj_bwd_x(dsec, w_t, xl, norm_g[0][None], dx, scatter=(*pending, ps_))
    parts[0] += (qi, qo)
    fi = lax.empty((DEPTH, D_MODEL, SHARD_COLS), F32)
    fo = lax.empty((DEPTH, SHARD_ROWS, D_MODEL), F32)
    for l in range(DEPTH):
        send_i, ri, send_o, ro, qi, qo = parts[l]
        fi = _mesh_sum(place, send_i, ri, qi, fi, l, "mesh_sum_w_in", 256)
        fo = _mesh_sum(place, send_o, ro, qo, fo, l, "mesh_sum_w_out", SHARD_ROWS // 2)
    hs = _mesh_sum_small(place, ps_, qs, SMALL_HALF // 4)
    fi, fo, fs, gn0 = _join_sibling(fi, fo, hs, gn0.reshape(8, 128))
    fs = jnp.concatenate([fs[0:8] + gn0, fs[8:]], axis=0)

    col_major = lambda a: jnp.transpose(a, (2, 0, 1))
    flat_o = lambda a: a.reshape(-1, D_MODEL)
    out_i = _adamw(col_major(fi), col_major(w_in), col_major(m_w_in), col_major(v_w_in), "adamw_w_in", 77)
    out_i = [jnp.transpose(t, (1, 2, 0)) for t in out_i]
    out_o = _adamw(flat_o(fo), flat_o(w_out), flat_o(m_w_out), flat_o(v_w_out), "adamw_w_out", 256)
    out_s = _adamw(fs, _pack_small(norm_g, final_g, pool_scale, forget_bias, pool_w),
                   _pack_small(m_norm_g, m_final_g, m_pool_scale, m_forget_bias, m_pool_w),
                   _pack_small(v_norm_g, v_final_g, v_pool_scale, v_forget_bias, v_pool_w),
                   "adamw_small", SMALL_ROWS // 4)

    loss = out_s[0].reshape(-1)[LOSS_SLOT]
    groups = []
    for oi, oo, os_ in zip(out_i, out_o, out_s):
        sn, sf, sp, sb, sw = _unpack_small(os_)
        groups.append((sn, oi.reshape(w_in.shape), sb, sw, sp, oo.reshape(w_out.shape), sf))
    return (loss, dx[None]) + tuple(t for grp in groups for t in grp)
```

```python
import jax
import jax.numpy as jnp
from jax import lax
from jax.experimental import pallas as pl
from jax.experimental.pallas import tpu as pltpu

F32 = jnp.float32
BF16 = jnp.bfloat16

D_MODEL = 1024
DEPTH = 4
POOL_WIDTH = 512
ATTN_WIDTH = 512
HEAD_DIM = 64
PAIR = 2 * HEAD_DIM
N_PAIRS = ATTN_WIDTH // PAIR
POOL_WINDOWS = (2, 4, 8, 16)
GROUP_DIM = 128
HALO = 16
IN_COLS = 3080
OFF_F = 3072
IN_COLS_PAD = 3200
SECTIONS = ((0, 512), (512, 512), (1024, 512), (1536, 512), (2048, 512), (2560, 512), (OFF_F, 128))
N_CHIPS = 4
SHARD_COLS = IN_COLS // N_CHIPS
SHARD_ROWS = D_MODEL // N_CHIPS
RMS_EPS = 1e-6
NEG = -1e30
Q_SCALE = 0.125

ADAM_LR = 0.001
ADAM_B1 = 0.9
ADAM_B2 = 0.999
ADAM_EPS = 1e-08
ADAM_WD = 0.01
ADAM_STEP = 10

SMALL_ROWS = 2112
SMALL_HALF = SMALL_ROWS // 2

NT = (((1,), (1,)), ((), ()))
TN = (((0,), (0,)), ((), ()))
MESH = pl.DeviceIdType.MESH
ANY = pl.BlockSpec(memory_space=pl.ANY)


def _params(semantics, vmem_mb=48):
    return pltpu.CompilerParams(dimension_semantics=semantics, vmem_limit_bytes=vmem_mb << 20)


def _row_spec(tm, cols):
    return pl.BlockSpec((tm, cols), lambda i: (i, 0))


def _full_spec(shape):
    return pl.BlockSpec(shape, lambda *_: (0,) * len(shape))


def _sigmoid(x):
    return jax.nn.sigmoid(x)


def _scan_rows(a, reverse=False):
    n = a.shape[0]
    row = lax.broadcasted_iota(jnp.int32, a.shape, 0)
    k = 1
    while k < n:
        if reverse:
            a = a + jnp.where(row < n - k, pltpu.roll(a, n - k, 0), 0.0)
        else:
            a = a + jnp.where(row >= k, pltpu.roll(a, k, 0), 0.0)
        k *= 2
    return a


def _inproj_fwd(place, x, g, own, got, fb, pw, ps, tm=512):
    s = x.shape[0]

    def body(place_ref, x_ref, g_ref, fb_ref, pw_ref, ps_ref, *refs):
        own_ref = refs[0] if own is not None else None
        (got_ref, pu_ref, pg_ref, q_ref, k_ref, v_ref, ag_ref, lf_ref, c_ref, h_ref, po_ref, wt_ref,
         carry_ref, halo_ref, w_ref, piece_ref, sem) = refs[own is not None:]
        step = pl.program_id(0)

        @pl.when(step == 0)
        def _():
            carry_ref[...] = jnp.zeros_like(carry_ref)
            halo_ref[...] = jnp.zeros_like(halo_ref)
            for chip in range(N_CHIPS):
                rel = jnp.bitwise_xor(chip, place_ref[0])

                def fetch(src):
                    cp = pltpu.make_async_copy(src, piece_ref, sem)
                    cp.start()
                    cp.wait()

                if own_ref is None:
                    fetch(got_ref.at[0, rel])
                else:
                    pl.when(rel == 0)(lambda: fetch(own_ref))
                    pl.when(rel != 0)(lambda: fetch(got_ref.at[0, rel]))
                w_ref[:, chip * SHARD_COLS:(chip + 1) * SHARD_COLS] = piece_ref[...]
            w_ref[:, IN_COLS:] = jnp.zeros((D_MODEL, IN_COLS_PAD - IN_COLS), BF16)
            for t in range(IN_COLS_PAD // 128):
                cols = slice(t * 128, (t + 1) * 128)
                wt_ref[cols, :] = jnp.transpose(w_ref[:, cols].astype(F32)).astype(BF16)

        xf = x_ref[...]
        r = lax.rsqrt(jnp.mean(xf * xf, axis=-1, keepdims=True) + RMS_EPS)
        h = (xf * r * g_ref[...]).astype(BF16)
        h_ref[...] = h

        def proj(sec):
            off, n = SECTIONS[sec]
            return jnp.dot(h, w_ref[:, off:off + n], preferred_element_type=F32)

        pu_ref[...] = proj(0).astype(BF16)
        pg_ref[...] = proj(1).astype(BF16)
        u = pu_ref[...].astype(F32)
        d = _pool_delta(u, halo_ref[...], step * tm)
        halo_ref[...] = u[tm - HALO:, :]
        for gi in range(len(POOL_WINDOWS)):
            cols = slice(gi * GROUP_DIM, (gi + 1) * GROUP_DIM)
            zg = jnp.dot(d[gi].astype(BF16), pw_ref[gi], preferred_element_type=F32)
            gate = pg_ref[:, cols].astype(F32)
            po_ref[:, cols] = (zg * ps_ref[:, cols] * (gate * _sigmoid(gate))).astype(BF16)
        q_ref[...] = (proj(2) * Q_SCALE).astype(BF16)
        k_ref[...] = proj(3).astype(BF16)
        v_ref[...] = proj(4).astype(BF16)
        ag_ref[...] = proj(5).astype(BF16)
        z = proj(6) + fb_ref[...]
        lf = jnp.minimum(z, 0.0) - jnp.log(1.0 + jnp.exp(-jnp.abs(z)))
        lf_ref[...] = lf
        c_ref[...] = _scan_rows(lf) + carry_ref[0:1, :]
        carry_ref[0:1, :] = c_ref[tm - 1:tm, :]

    act = jax.ShapeDtypeStruct((s, 512), BF16)
    rows = lambda cols: pl.BlockSpec((tm, cols), lambda i, p: (i, 0))
    whole = lambda shape: pl.BlockSpec(shape, lambda i, p: (0,) * len(shape))
    shards = ([] if own is None else [own]) + [got]
    return pl.pallas_call(
        body, name="inproj_fwd" if own is None else "inproj_fwd_first",
        grid_spec=pltpu.PrefetchScalarGridSpec(
            num_scalar_prefetch=1, grid=(s // tm,),
            in_specs=[rows(D_MODEL), whole((1, D_MODEL)), whole((1, 128)), whole((4, GROUP_DIM, GROUP_DIM)),
                      whole((1, 512))] + [ANY] * len(shards),
            out_specs=[rows(512)] * 6 + [rows(128), rows(128), rows(D_MODEL), rows(512),
                                         whole((IN_COLS_PAD, D_MODEL))],
            scratch_shapes=[pltpu.VMEM((8, 128), F32), pltpu.VMEM((HALO, 512), F32),
                            pltpu.VMEM((D_MODEL, IN_COLS_PAD), BF16),
                            pltpu.VMEM((D_MODEL, SHARD_COLS), BF16), pltpu.SemaphoreType.DMA]),
        out_shape=[act] * 6 + [jax.ShapeDtypeStruct((s, 128), F32), jax.ShapeDtypeStruct((s, 128), F32),
                               jax.ShapeDtypeStruct((s, D_MODEL), BF16), act,
                               jax.ShapeDtypeStruct((IN_COLS_PAD, D_MODEL), BF16)],
        compiler_params=_params(("arbitrary",), vmem_mb=56),
    )(place, x, g, fb, pw, ps, *shards)


def _window_sums(ext, forward):
    n = ext.shape[0]
    outs = []
    for gi, w in enumerate(POOL_WINDOWS):
        a = ext[:, gi * GROUP_DIM:(gi + 1) * GROUP_DIM]
        k = 1
        while k < w:
            a = a + pltpu.roll(a, (n - k) if forward else k, 0)
            k *= 2
        outs.append(a)
    return outs


def _over_count(a, row0, gi):
    w = POOL_WINDOWS[gi]
    t = row0 + lax.broadcasted_iota(jnp.int32, (HALO, GROUP_DIM), 0)
    head = a[0:HALO, :] / jnp.minimum(t + 1, w).astype(F32)
    return jnp.concatenate([head, a[HALO:, :] * (1.0 / w)], axis=0)


def _pool_delta(u, halo, row0):
    sums = _window_sums(jnp.concatenate([halo, u], axis=0), forward=False)
    return [_over_count(sums[gi][HALO:, :], row0, gi) - u[:, gi * GROUP_DIM:(gi + 1) * GROUP_DIM]
            for gi in range(len(POOL_WINDOWS))]


def _split3(x):
    hi = x.astype(BF16).astype(F32)
    r1 = x - hi
    lo = r1.astype(BF16).astype(F32)
    return hi, lo, (r1 - lo).astype(BF16).astype(F32)


def _own(idx, h):
    return (idx < HEAD_DIM) if h == 0 else (idx >= HEAD_DIM)


def _spare(h):
    return HEAD_DIM if h == 0 else 0


def _attn_fwd(q, k, v, c, ag, carry=None, tq=512, tk=512, chunk=512):
    s = q.shape[0]
    n_carry = 0 if carry is None else len(carry)

    def body(*refs):
        q_ref, k_ref, v_ref, c_ref, ag_ref = refs[:5]
        carry_src = refs[5:5 + n_carry]
        ma_ref, o_ref, lse_ref = refs[5 + n_carry:8 + n_carry]
        carry_dst = refs[8 + n_carry:8 + 2 * n_carry]
        kx_ref, vt_ref, acc0_ref, acc1_ref, sta_ref, stb_ref = refs[8 + 2 * n_carry:14 + 2 * n_carry]
        carry_sems = refs[14 + 2 * n_carry:]
        j = pl.program_id(0)
        i = pl.program_id(1)

        if n_carry:
            @pl.when((j == 0) & (i == 0))
            def _():
                _gather_start(carry_src, carry_dst, carry_sems[:2])
                for cp in _gather_own(carry_src, carry_dst, carry_sems[2]):
                    cp.start()

        @pl.when(i == 0)
        def _():
            def prep(ch, carry):
                rows = pl.ds(pl.multiple_of(ch * chunk, chunk), chunk)
                kk = k_ref[rows, :].astype(F32)
                cc = c_ref[rows, :]
                vt = jnp.transpose(v_ref[rows, :].astype(F32))
                lane = lax.broadcasted_iota(jnp.int32, (chunk, PAIR), 1)
                sub = lax.broadcasted_iota(jnp.int32, (PAIR, chunk), 0)
                for h in range(2):
                    ccol = jnp.sum(jnp.where(lane == 2 * j + h, cc, 0.0), axis=1, keepdims=True)
                    kx = jnp.where(_own(lane, h), kk, 0.0)
                    for t, part in enumerate(_split3(-ccol)):
                        kx = jnp.where(lane == _spare(h) + t, part, kx)
                    kx_ref[h, rows, :] = kx.astype(BF16)
                    vt_ref[h, :, rows] = jnp.where(_own(sub, h), vt,
                                                   jnp.where(sub == _spare(h), 1.0, 0.0)).astype(BF16)
                return carry

            lax.fori_loop(0, s // chunk, prep, 0)

        def queries(blk):
            qq = q_ref[pl.ds(pl.multiple_of(blk * tq, tq), tq), :].astype(F32)
            lane_q = lax.broadcasted_iota(jnp.int32, (tq, PAIR), 1)
            out = []
            for h in range(2):
                ones = (lane_q >= _spare(h)) & (lane_q < _spare(h) + 3)
                out.append(jnp.transpose(jnp.where(_own(lane_q, h), qq, jnp.where(ones, 1.0, 0.0))).astype(BF16))
            return out

        qx = queries(i)
        accs = (acc0_ref, acc1_ref)
        for acc in accs:
            acc[...] = jnp.zeros_like(acc)

        def rows_of(kb):
            return pl.ds(pl.multiple_of(kb * tk, tk), tk)

        def scores(kb, dst, qt=qx):
            for h in range(2):
                dst[h] = jnp.dot(kx_ref[h, rows_of(kb), :], qt[h], preferred_element_type=F32)

        def consume(kb, src, m, masked):
            m_out, alpha, pv = [], [], []
            for h in range(2):
                sh = src[h]
                if masked:
                    kpos = kb * tk + lax.broadcasted_iota(jnp.int32, (tk, tq), 0)
                    qpos = i * tq + lax.broadcasted_iota(jnp.int32, (tk, tq), 1)
                    sh = jnp.where(qpos >= kpos, sh, NEG)
                m_new = jnp.maximum(m[h], jnp.max(sh, axis=0, keepdims=True))
                pt = jnp.exp(sh - m_new).astype(BF16)
                alpha.append(jnp.exp(m[h] - m_new))
                pv.append(jnp.dot(vt_ref[h, :, rows_of(kb)], pt, preferred_element_type=F32))
                m_out.append(m_new)
            for h in range(2):
                accs[h][...] = accs[h][...] * alpha[h] + pv[h]
            return tuple(m_out)

        n_full = (i * tq) // tk

        def two_blocks(t, m):
            kb = 2 * t
            scores(kb + 1, stb_ref)
            m = consume(kb, sta_ref, m, False)
            scores(kb + 2, sta_ref)
            return consume(kb + 1, stb_ref, m, False)

        def odd_tail(m):
            scores(n_full, stb_ref)
            m = consume(n_full - 1, sta_ref, m, False)
            return consume(n_full, stb_ref, m, True)

        def even_tail(m):
            return consume(n_full, sta_ref, m, True)

        pl.when(i == 0)(lambda: scores(0, sta_ref))
        m0 = jnp.full((1, tq), NEG, F32)
        n_two = n_full // 2
        m = lax.fori_loop(0, n_two // 2, lambda u, m: two_blocks(2 * u + 1, two_blocks(2 * u, m)), (m0, m0))
        m = lax.cond(lax.rem(n_two, 2) == 1, lambda m: two_blocks(n_two - 1, m), lambda m: m, m)
        m = lax.cond(lax.rem(n_full, 2) == 1, odd_tail, even_tail, m)

        scores(0, sta_ref, queries(jnp.minimum(i + 1, s // tq - 1)))

        sub_o = lax.broadcasted_iota(jnp.int32, (PAIR, tq), 0)
        den = [accs[h][_spare(h):_spare(h) + 1, :] for h in range(2)]
        ot = jnp.where(sub_o < HEAD_DIM, acc0_ref[...] / den[0], acc1_ref[...] / den[1])
        o = jnp.transpose(ot)
        o_ref[...] = o.astype(BF16)
        gate = ag_ref[...].astype(F32)
        ma_ref[...] = (o * (gate * _sigmoid(gate))).astype(BF16)
        sub8 = lax.broadcasted_iota(jnp.int32, (8, tq), 0)
        lse_ref[...] = jnp.where(sub8 == 0, m[0] + jnp.log(den[0]), m[1] + jnp.log(den[1]))

        if n_carry:
            @pl.when((j == N_PAIRS - 1) & (i == s // tq - 1))
            def _():
                _gather_finish(carry_src, carry_dst, carry_sems[:2])
                for cp in _gather_own(carry_src, carry_dst, carry_sems[2]):
                    cp.wait()

    carried = [] if carry is None else list(carry)
    carry_sem = ([pltpu.SemaphoreType.DMA((GATHER_SEMS * n_carry,))] * 2
                 + [pltpu.SemaphoreType.DMA((n_carry,))]) if n_carry else []
    return pl.pallas_call(
        body, name="attn_fwd_gather" if n_carry else "attn_fwd", grid=(N_PAIRS, s // tq),
        in_specs=[pl.BlockSpec((s, PAIR), lambda j, i: (0, j)),
                  pl.BlockSpec((s, PAIR), lambda j, i: (0, j)),
                  pl.BlockSpec((s, PAIR), lambda j, i: (0, j)),
                  pl.BlockSpec((s, 128), lambda j, i: (0, 0)),
                  pl.BlockSpec((tq, PAIR), lambda j, i: (i, j))] + [ANY] * n_carry,
        out_specs=[pl.BlockSpec((tq, PAIR), lambda j, i: (i, j)),
                   pl.BlockSpec((tq, PAIR), lambda j, i: (i, j)),
                   pl.BlockSpec((None, 8, tq), lambda j, i: (j, 0, i))] + [ANY] * n_carry,
        out_shape=[jax.ShapeDtypeStruct((s, ATTN_WIDTH), BF16), jax.ShapeDtypeStruct((s, ATTN_WIDTH), BF16),
                   jax.ShapeDtypeStruct((N_PAIRS, 8, s), F32)] + (_gather_shapes(*carried) if n_carry else []),
        scratch_shapes=[pltpu.VMEM((2, s, PAIR), BF16), pltpu.VMEM((2, PAIR, s), BF16),
                        pltpu.VMEM((PAIR, tq), F32), pltpu.VMEM((PAIR, tq), F32),
                        pltpu.VMEM((2, tk, tq), F32), pltpu.VMEM((2, tk, tq), F32)] + carry_sem,
        compiler_params=_params(("arbitrary", "arbitrary")),
    )(q, k, v, c, ag, *carried)


def _outproj_fwd(x, po, ma, wo, tm=512):
    s = x.shape[0]

    def body(x_ref, po_ref, ma_ref, wo_ref, xn_ref):
        xn_ref[...] = (x_ref[...]
                       + jnp.dot(po_ref[...], wo_ref[0:POOL_WIDTH, :], preferred_element_type=F32)
                       + jnp.dot(ma_ref[...], wo_ref[POOL_WIDTH:, :], preferred_element_type=F32))

    return pl.pallas_call(
        body, name="outproj_fwd", grid=(s // tm,),
        in_specs=[_row_spec(tm, D_MODEL), _row_spec(tm, 512), _row_spec(tm, 512),
                  _full_spec((D_MODEL, D_MODEL))],
        out_specs=_row_spec(tm, D_MODEL),
        out_shape=jax.ShapeDtypeStruct((s, D_MODEL), F32),
        compiler_params=_params(("parallel",)),
    )(x, po, ma, wo)


def _loss_head(x, po, ma, wo, tgt, g, tm=512):
    s = x.shape[0]

    def body(x_ref, po_ref, ma_ref, wo_ref, t_ref, g_ref, dx_ref, dg_ref, loss_ref):
        @pl.when(pl.program_id(0) == 0)
        def _():
            dg_ref[...] = jnp.zeros_like(dg_ref)
            loss_ref[...] = jnp.zeros_like(loss_ref)

        xf = (x_ref[...]
              + jnp.dot(po_ref[...], wo_ref[0:POOL_WIDTH, :], preferred_element_type=F32)
              + jnp.dot(ma_ref[...], wo_ref[POOL_WIDTH:, :], preferred_element_type=F32))
        r = lax.rsqrt(jnp.mean(xf * xf, axis=-1, keepdims=True) + RMS_EPS)
        xh = xf * r
        gg = g_ref[...]
        e = xh * gg - t_ref[...]
        loss_ref[...] += 0.5 * jnp.sum(jnp.mean(e * e, axis=-1, keepdims=True))
        dy = e * (1.0 / D_MODEL)
        u = dy * gg
        dx_ref[...] = r * (u - xh * jnp.mean(xh * u, axis=-1, keepdims=True))
        dg_ref[...] += jnp.sum(dy * xh, axis=0, keepdims=True)

    return pl.pallas_call(
        body, name="loss_head", grid=(s // tm,),
        in_specs=[_row_spec(tm, D_MODEL), _row_spec(tm, 512), _row_spec(tm, 512), _full_spec((D_MODEL, D_MODEL)),
                  _row_spec(tm, D_MODEL), _full_spec((1, D_MODEL))],
        out_specs=[_row_spec(tm, D_MODEL), _full_spec((1, D_MODEL)), _full_spec((8, 128))],
        out_shape=[jax.ShapeDtypeStruct((s, D_MODEL), F32), jax.ShapeDtypeStruct((1, D_MODEL), F32),
                   jax.ShapeDtypeStruct((8, 128), F32)],
        compiler_params=_params(("arbitrary",)),
    )(x, po, ma, wo, tgt, g)


def _outproj_bwd(dx, po, ma, ag, o, wo, head_sel, tm=512):
    s = dx.shape[0]

    def body(dx_ref, po_ref, ma_ref, ag_ref, o_ref, wo_ref, sel_ref, dmp_ref, do_ref, dag_ref, dt_ref, gwo_ref):
        @pl.when(pl.program_id(0) == 0)
        def _():
            gwo_ref[...] = jnp.zeros_like(gwo_ref)

        dxb = dx_ref[...].astype(BF16)
        dm = lax.dot_general(dxb, wo_ref[...], NT, preferred_element_type=F32)
        dmp_ref[...] = dm[:, 0:POOL_WIDTH].astype(BF16)
        dma = dm[:, POOL_WIDTH:]
        gate = ag_ref[...].astype(F32)
        of = o_ref[...].astype(F32)
        sg = _sigmoid(gate)
        do = dma * (gate * sg)
        do_ref[...] = do.astype(BF16)
        dag_ref[...] = (dma * of * (sg * (1.0 + gate * (1.0 - sg)))).astype(BF16)
        prod = do * of
        hi = prod.astype(BF16)
        lo = (prod - hi.astype(F32)).astype(BF16)
        dt_ref[...] = (lax.dot_general(sel_ref[...], hi, NT, preferred_element_type=F32)
                       + lax.dot_general(sel_ref[...], lo, NT, preferred_element_type=F32))
        gwo_ref[0:POOL_WIDTH, :] += lax.dot_general(po_ref[...], dxb, TN, preferred_element_type=F32)
        gwo_ref[POOL_WIDTH:, :] += lax.dot_general(ma_ref[...], dxb, TN, preferred_element_type=F32)

    act = jax.ShapeDtypeStruct((s, 512), BF16)
    return pl.pallas_call(
        body, name="outproj_bwd", grid=(s // tm,),
        in_specs=[_row_spec(tm, D_MODEL)] + [_row_spec(tm, 512)] * 4
                 + [_full_spec((D_MODEL, D_MODEL)), _full_spec((8, ATTN_WIDTH))],
        out_specs=[_row_spec(tm, 512)] * 3 + [pl.BlockSpec((8, tm), lambda i: (0, i)),
                                              _full_spec((D_MODEL, D_MODEL))],
        out_shape=[act, act, act, jax.ShapeDtypeStruct((8, s), F32),
                   jax.ShapeDtypeStruct((D_MODEL, D_MODEL), F32)],
        compiler_params=_params(("arbitrary",)),
    )(dx, po, ma, ag, o, wo, head_sel)


def _attn_bwd(q, k, v, do, c, lse, dt, carry=None, tq=256, chunk=512):
    s = q.shape[0]
    tk = 2 * tq
    nq = s // tq
    n_carry = 0 if carry is None else len(carry)

    def body(*refs):
        q_ref, k_ref, v_ref, do_ref, c_ref, lse_ref, dt_ref = refs[:7]
        carry_src = refs[7:7 + n_carry]
        dq_ref, dk_ref, dv_ref, dcs_ref, drs_ref = refs[7 + n_carry:12 + n_carry]
        carry_dst = refs[12 + n_carry:12 + 2 * n_carry]
        (qxt_ref, doxt_ref, qm_ref, dox_ref, dqt0_ref, dqt1_ref, dk0_ref, dk1_ref, dvx_ref,
         kx_ref, vx_ref, ktm_ref, sa_ref, sb_ref) = refs[12 + 2 * n_carry:26 + 2 * n_carry]
        carry_sems = refs[26 + 2 * n_carry:]
        carry_big = tuple(zip(carry_src, carry_dst))
        j = pl.program_id(0)
        b = pl.program_id(1)
        dqts = (dqt0_ref, dqt1_ref)
        dks = (dk0_ref, dk1_ref)

        if n_carry:
            @pl.when((j == 0) & (b == 0))
            def _():
                _scatter_start(carry_big, None, carry_sems)

        @pl.when(b == 0)
        def _():
            def prep(ch, carry):
                rows = pl.ds(pl.multiple_of(ch * chunk, chunk), chunk)
                qq = q_ref[rows, :].astype(F32)
                dd = do_ref[rows, :].astype(F32)
                qt = jnp.transpose(qq)
                ddt = jnp.transpose(dd)
                lane = lax.broadcasted_iota(jnp.int32, (chunk, PAIR), 1)
                sub = lax.broadcasted_iota(jnp.int32, (PAIR, chunk), 0)
                for h in range(2):
                    sp = _spare(h)
                    qm_ref[h, rows, :] = jnp.where(_own(lane, h), qq, jnp.where(lane == sp, 1.0, 0.0)).astype(BF16)
                    dox_ref[h, rows, :] = jnp.where(_own(lane, h), dd, 0.0).astype(BF16)
                    qx = jnp.where(_own(sub, h), qt, jnp.where((sub >= sp) & (sub < sp + 3), 1.0, 0.0))
                    for t, part in enumerate(_split3(-lse_ref[pl.ds(h, 1), rows])):
                        qx = jnp.where(sub == sp + 3 + t, part, qx)
                    qxt_ref[h, :, rows] = qx.astype(BF16)
                    dx = jnp.where(_own(sub, h), ddt, 0.0)
                    for t, part in enumerate(_split3(-dt_ref[pl.ds(h, 1), rows])):
                        dx = jnp.where(sub == sp + t, part, dx)
                    doxt_ref[h, :, rows] = dx.astype(BF16)
                return carry

            lax.fori_loop(0, s // chunk, prep, 0)
            for ref in dqts:
                ref[...] = jnp.zeros_like(ref)

        kk = k_ref[...].astype(F32)
        vv = v_ref[...].astype(F32)
        cc = c_ref[...]
        kt = jnp.transpose(kk)
        lane = lax.broadcasted_iota(jnp.int32, (tk, PAIR), 1)
        sub = lax.broadcasted_iota(jnp.int32, (PAIR, tk), 0)
        for h in range(2):
            sp = _spare(h)
            ccol = jnp.sum(jnp.where(lane == 2 * j + h, cc, 0.0), axis=1, keepdims=True)
            kx = jnp.where(_own(lane, h), kk, jnp.where((lane >= sp + 3) & (lane < sp + 6), 1.0, 0.0))
            for t, part in enumerate(_split3(-ccol)):
                kx = jnp.where(lane == sp + t, part, kx)
            kx_ref[h] = kx.astype(BF16)
            vx_ref[h] = jnp.where(_own(lane, h), vv, jnp.where((lane >= sp) & (lane < sp + 3), 1.0, 0.0)).astype(BF16)
            ktm_ref[h] = jnp.where(_own(sub, h), kt, jnp.where(sub == sp, 1.0, 0.0)).astype(BF16)
        for ref in dks:
            ref[...] = jnp.zeros_like(ref)
        dvx_ref[...] = jnp.zeros_like(dvx_ref)

        def cols_of(i):
            return pl.ds(pl.multiple_of(i * tq, tq), tq)

        def scores(i, dst):
            cols = cols_of(jnp.minimum(i, nq - 1))
            for h in range(2):
                dst[h] = jnp.dot(kx_ref[h], qxt_ref[h, :, cols], preferred_element_type=F32)
                dst[2 + h] = jnp.dot(vx_ref[h], doxt_ref[h, :, cols], preferred_element_type=F32)

        def consume(i, src, masked):
            cols = cols_of(i)
            for h in range(2):
                arg = src[h]
                if masked:
                    kpos = b * tk + lax.broadcasted_iota(jnp.int32, (tk, tq), 0)
                    qpos = i * tq + lax.broadcasted_iota(jnp.int32, (tk, tq), 1)
                    arg = jnp.where(qpos >= kpos, arg, NEG)
                pt = jnp.exp(arg)
                dst = (pt * src[2 + h]).astype(BF16)
                dvx_ref[...] += jnp.dot(pt.astype(BF16), dox_ref[h, cols, :], preferred_element_type=F32)
                dks[h][...] += jnp.dot(dst, qm_ref[h, cols, :], preferred_element_type=F32)
                dqts[h][:, cols] += jnp.dot(ktm_ref[h], dst, preferred_element_type=F32)

        i0 = 2 * b

        def two_blocks(t, masked):
            i = i0 + 2 * t
            scores(i + 1, sb_ref)
            consume(i, sa_ref, masked)
            scores(i + 2, sa_ref)
            consume(i + 1, sb_ref, masked)

        def four_blocks(u, carry):
            two_blocks(1 + 2 * u, False)
            two_blocks(2 + 2 * u, False)
            return carry

        scores(i0, sa_ref)
        two_blocks(0, True)
        n_open = (nq - i0) // 2 - 1
        lax.fori_loop(0, n_open // 2, four_blocks, 0)

        @pl.when(lax.rem(n_open, 2) == 1)
        def _():
            two_blocks(n_open, False)

        dv_ref[...] = dvx_ref[...].astype(BF16)
        dk_ref[...] = jnp.where(lane < HEAD_DIM, dk0_ref[...], dk1_ref[...]).astype(BF16)
        dcs_ref[...] = jnp.where(lane == _spare(0), dk0_ref[...], jnp.where(lane == _spare(1), dk1_ref[...], 0.0))

        @pl.when(b == s // tk - 1)
        def _():
            sub8 = lax.broadcasted_iota(jnp.int32, (8, s), 0)
            drs_ref[...] = jnp.where(sub8 == 0, dqt0_ref[_spare(0):_spare(0) + 1, :],
                                     jnp.where(sub8 == 1, dqt1_ref[_spare(1):_spare(1) + 1, :], 0.0))
            sub_c = lax.broadcasted_iota(jnp.int32, (PAIR, chunk), 0)
            for ch in range(s // chunk):
                rows = pl.ds(ch * chunk, chunk)
                both = jnp.where(sub_c < HEAD_DIM, dqt0_ref[:, rows], dqt1_ref[:, rows])
                dq_ref[rows, :] = (jnp.transpose(both) * Q_SCALE).astype(BF16)

        if n_carry:
            @pl.when((j == N_PAIRS - 1) & (b == s // tk - 1))
            def _():
                _scatter_finish(carry_big, None, carry_sems)

    act = jax.ShapeDtypeStruct((s, ATTN_WIDTH), BF16)
    pair_rows = pl.BlockSpec((s, PAIR), lambda j, b: (0, j))
    pair_blk = pl.BlockSpec((tk, PAIR), lambda j, b: (b, j))
    stat = pl.BlockSpec((None, 2, s), lambda j, b: (j, 0, 0))
    carried = [] if carry is None else list(carry)
    carry_sem = [pltpu.SemaphoreType.DMA((SCATTER_SEMS,))] * 2 if n_carry else []
    return pl.pallas_call(
        body, name="attn_bwd_scatter" if n_carry else "attn_bwd", grid=(N_PAIRS, s // tk),
        in_specs=[pair_rows, pair_blk, pair_blk, pair_rows,
                  pl.BlockSpec((tk, 128), lambda j, b: (b, 0)),
                  pl.BlockSpec((None, 8, s), lambda j, b: (j, 0, 0)), stat] + [ANY] * n_carry,
        out_specs=[pair_rows, pair_blk, pair_blk, pair_blk,
                   pl.BlockSpec((None, 8, s), lambda j, b: (j, 0, 0))] + [ANY] * n_carry,
        out_shape=[act, act, act, jax.ShapeDtypeStruct((s, N_PAIRS * 128), F32),
                   jax.ShapeDtypeStruct((N_PAIRS, 8, s), F32)] + (_scatter_shapes(*carried) if n_carry else []),
        scratch_shapes=[pltpu.VMEM((2, PAIR, s), BF16), pltpu.VMEM((2, PAIR, s), BF16),
                        pltpu.VMEM((2, s, PAIR), BF16), pltpu.VMEM((2, s, PAIR), BF16),
                        pltpu.VMEM((PAIR, s), F32), pltpu.VMEM((PAIR, s), F32),
                        pltpu.VMEM((tk, PAIR), F32), pltpu.VMEM((tk, PAIR), F32), pltpu.VMEM((tk, PAIR), F32),
                        pltpu.VMEM((2, tk, PAIR), BF16), pltpu.VMEM((2, tk, PAIR), BF16),
                        pltpu.VMEM((2, PAIR, tk), BF16),
                        pltpu.VMEM((4, tk, tq), F32), pltpu.VMEM((4, tk, tq), F32)] + carry_sem,
        compiler_params=_params(("arbitrary", "arbitrary")),
    )(q, k, v, do, c, lse, dt, *carried)


def _forget_bwd(dcs, drs, lf, tm=512):
    s = lf.shape[0]
    n = s // tm

    def body(dcs_ref, drs_ref, lf_ref, dpf_ref, dfb_ref, carry_ref):
        @pl.when(pl.program_id(0) == 0)
        def _():
            carry_ref[...] = jnp.zeros_like(carry_ref)
            dfb_ref[...] = jnp.zeros_like(dfb_ref)

        lane = lax.broadcasted_iota(jnp.int32, (tm, 128), 1)
        dc = drs_ref[...]
        for pj in range(N_PAIRS):
            blk = dcs_ref[:, pj * 128:(pj + 1) * 128]
            for h in range(2):
                head = 2 * pj + h
                moved = pltpu.roll(blk, (head - _spare(h)) % 128, 1) if head != _spare(h) else blk
                dc = dc - jnp.where(lane == head, moved, 0.0)
        dlf = _scan_rows(dc, reverse=True) + carry_ref[0:1, :]
        carry_ref[...] = dlf[0:8, :]
        dz = jnp.where(lane < 8, dlf * (1.0 - jnp.exp(lf_ref[...])), 0.0)
        dpf_ref[...] = dz.astype(BF16)
        dfb_ref[...] += jnp.sum(dz, axis=0, keepdims=True)

    return pl.pallas_call(
        body, name="forget_bwd", grid=(n,),
        in_specs=[pl.BlockSpec((tm, N_PAIRS * 128), lambda i: (n - 1 - i, 0)),
                  pl.BlockSpec((tm, 128), lambda i: (n - 1 - i, 0)),
                  pl.BlockSpec((tm, 128), lambda i: (n - 1 - i, 0))],
        out_specs=[pl.BlockSpec((tm, 128), lambda i: (n - 1 - i, 0)), _full_spec((1, 128))],
        out_shape=[jax.ShapeDtypeStruct((s, 128), BF16), jax.ShapeDtypeStruct((1, 128), F32)],
        scratch_shapes=[pltpu.VMEM((8, 128), F32)],
        compiler_params=_params(("arbitrary",)),
    )(dcs, drs, lf)


def _pool_bwd(pu, pg, dmp, pw, ps, tm=512):
    s = pu.shape[0]
    hb = tm // HALO
    n = s // tm
    last_halo = s // HALO - 1

    def body(pu_ref, halo_ref, pg_ref, dmp_ref, pgn_ref, dmpn_ref, pw_ref, ps_ref,
             dpu_ref, dpg_ref, gpw_ref, gps_ref):
        i = pl.program_id(0)

        @pl.when(i == 0)
        def _():
            gpw_ref[...] = jnp.zeros_like(gpw_ref)
            gps_ref[...] = jnp.zeros_like(gps_ref)

        u = pu_ref[...].astype(F32)
        halo = jnp.where(i == 0, 0.0, halo_ref[...].astype(F32))
        d = _pool_delta(u, halo, i * tm)
        e_parts, dd_parts = [], []
        for gi in range(len(POOL_WINDOWS)):
            cols = slice(gi * GROUP_DIM, (gi + 1) * GROUP_DIM)
            wg = pw_ref[gi]
            scale = ps_ref[:, cols]
            db = d[gi].astype(BF16)
            z = jnp.dot(db, wg, preferred_element_type=F32)
            gate = pg_ref[:, cols].astype(F32)
            sg = _sigmoid(gate)
            dm = dmp_ref[:, cols].astype(F32)
            dy = dm * (gate * sg)
            dpg_ref[:, cols] = (dm * (z * scale) * (sg * (1.0 + gate * (1.0 - sg)))).astype(BF16)
            gps_ref[:, cols] += jnp.sum(dy * z, axis=0, keepdims=True)
            dz = (dy * scale).astype(BF16)
            gpw_ref[gi] += lax.dot_general(db, dz, TN, preferred_element_type=F32)
            dd = lax.dot_general(dz, wg, NT, preferred_element_type=F32)
            gate_n = pgn_ref[:, cols].astype(F32)
            dz_n = (dmpn_ref[:, cols].astype(F32) * (gate_n * _sigmoid(gate_n)) * scale).astype(BF16)
            dd_n = lax.dot_general(dz_n, wg, NT, preferred_element_type=F32)
            dd_n = jnp.where(i == n - 1, 0.0, dd_n)
            dd_parts.append(dd)
            e_parts.append(jnp.concatenate([_over_count(dd, i * tm, gi), dd_n * (1.0 / POOL_WINDOWS[gi])], axis=0))
        lead = _window_sums(jnp.concatenate(e_parts, axis=1), forward=True)
        for gi in range(len(POOL_WINDOWS)):
            cols = slice(gi * GROUP_DIM, (gi + 1) * GROUP_DIM)
            dpu_ref[:, cols] = (lead[gi][0:tm, :] - dd_parts[gi]).astype(BF16)

    act = jax.ShapeDtypeStruct((s, 512), BF16)
    prev_halo = pl.BlockSpec((HALO, 512), lambda i: (jnp.maximum(i * hb - 1, 0), 0))
    next_halo = pl.BlockSpec((HALO, 512), lambda i: (jnp.minimum((i + 1) * hb, last_halo), 0))
    return pl.pallas_call(
        body, name="pool_bwd", grid=(n,),
        in_specs=[_row_spec(tm, 512), prev_halo, _row_spec(tm, 512), _row_spec(tm, 512), next_halo, next_halo,
                  _full_spec((4, GROUP_DIM, GROUP_DIM)), _full_spec((1, 512))],
        out_specs=[_row_spec(tm, 512), _row_spec(tm, 512), _full_spec((4, GROUP_DIM, GROUP_DIM)),
                   _full_spec((1, 512))],
        out_shape=[act, act, jax.ShapeDtypeStruct((4, GROUP_DIM, GROUP_DIM), F32),
                   jax.ShapeDtypeStruct((1, 512), F32)],
        compiler_params=_params(("arbitrary",)),
    )(pu, pu, pg, dmp, pg, dmp, pw, ps)


def _inproj_bwd_x(dsec, wt, x, g, dxo, exchange=None, scatter=None, tm=512):
    s = x.shape[0]
    n_steps = s // tm
    carried = list(exchange or scatter or ())
    n_carry = len(carried)

    def body(*refs):
        d_refs = refs[0:7]
        wt_ref, x_ref, g_ref, dxo_ref = refs[7:11]
        carry_src = refs[11:11 + n_carry]
        dxi_ref, dg_ref = refs[11 + n_carry:13 + n_carry]
        carry_dst = refs[13 + n_carry:13 + 2 * n_carry]
        sems = refs[13 + 2 * n_carry:]
        step = pl.program_id(0)

        def comm(phase):
            if exchange is not None:
                (_exchange_start if phase == 0 else _exchange_finish)(carry_src, carry_dst, None, sems)
            else:
                big = tuple(zip(carry_src[:2], carry_dst[:2]))
                (_scatter_start if phase == 0 else _scatter_finish)(big, (carry_src[2], carry_dst[2]), sems)

        @pl.when(step == 0)
        def _():
            dg_ref[...] = jnp.zeros_like(dg_ref)
            if n_carry:
                comm(0)

        dh = None
        for d_ref, (off, ncol) in zip(d_refs, SECTIONS):
            t = jnp.dot(d_ref[...], wt_ref[off:off + ncol, :], preferred_element_type=F32)
            dh = t if dh is None else dh + t
        xf = x_ref[...]
        r = lax.rsqrt(jnp.mean(xf * xf, axis=-1, keepdims=True) + RMS_EPS)
        xh = xf * r
        u = dh * g_ref[...]
        dxi_ref[...] = dxo_ref[...] + r * (u - xh * jnp.mean(xh * u, axis=-1, keepdims=True))
        dg_ref[...] += jnp.sum(dh * xh, axis=0, keepdims=True)

        if n_carry:
            @pl.when(step == n_steps - 1)
            def _():
                comm(1)

    if exchange is not None:
        name, carry_shapes, n_sems = "inproj_bwd_x_exchange", _exchange_shapes(*exchange), 3
    elif scatter is not None:
        name, n_sems = "inproj_bwd_x_scatter", SCATTER_SEMS
        carry_shapes = _scatter_shapes(*scatter[:2]) + [jax.ShapeDtypeStruct((3, SMALL_HALF, 128), F32)]
    else:
        name, carry_shapes, n_sems = "inproj_bwd_x", [], 0
    return pl.pallas_call(
        body, name=name, grid=(n_steps,),
        in_specs=[_row_spec(tm, ncol) for _, ncol in SECTIONS]
                 + [_full_spec((IN_COLS_PAD, D_MODEL)), _row_spec(tm, D_MODEL), _full_spec((1, D_MODEL)),
                    _row_spec(tm, D_MODEL)] + [ANY] * n_carry,
        out_specs=[_row_spec(tm, D_MODEL), _full_spec((1, D_MODEL))] + [ANY] * n_carry,
        out_shape=[jax.ShapeDtypeStruct((s, D_MODEL), F32), jax.ShapeDtypeStruct((1, D_MODEL), F32)] + carry_shapes,
        scratch_shapes=[pltpu.SemaphoreType.DMA((n_sems,))] * 2 if n_carry else [],
        compiler_params=_params(("arbitrary",)),
    )(*dsec, wt, x, g, dxo, *carried)


def _inproj_bwd_w(h, dsec, tm=512):
    s = h.shape[0]
    n_steps = s // tm

    def body(*refs):
        h_ref = refs[0]
        d_refs = refs[1:8]
        out_ref, gw_ref = refs[8:]
        step = pl.program_id(0)

        @pl.when(step == 0)
        def _():
            gw_ref[...] = jnp.zeros_like(gw_ref)

        hh = h_ref[...]
        for d_ref, (off, ncol) in zip(d_refs, SECTIONS):
            gw_ref[:, off:off + ncol] += lax.dot_general(hh, d_ref[...], TN, preferred_element_type=F32)

        @pl.when(step == n_steps - 1)
        def _():
            for k in range(N_CHIPS):
                out_ref[k] = gw_ref[:, k * SHARD_COLS:(k + 1) * SHARD_COLS]

    return pl.pallas_call(
        body, name="inproj_bwd_w", grid=(n_steps,),
        in_specs=[_row_spec(tm, D_MODEL)] + [_row_spec(tm, ncol) for _, ncol in SECTIONS],
        out_specs=_full_spec((N_CHIPS, D_MODEL, SHARD_COLS)),
        out_shape=jax.ShapeDtypeStruct((N_CHIPS, D_MODEL, SHARD_COLS), F32),
        scratch_shapes=[pltpu.VMEM((D_MODEL, IN_COLS_PAD), F32)],
        compiler_params=_params(("arbitrary",), vmem_mb=60),
    )(h, *dsec)


def _elementwise(fn, name, n_out, arrays, tm):
    shape = arrays[0].shape
    rest = shape[1:]
    n_in = len(arrays)

    def body(*refs):
        outs = fn(*[r[...] for r in refs[:n_in]])
        for r, val in zip(refs[n_in:], outs):
            r[...] = val

    spec = pl.BlockSpec((tm,) + rest, lambda i: (i,) + (0,) * len(rest))
    return pl.pallas_call(
        body, name=name, grid=(shape[0] // tm,),
        in_specs=[spec] * n_in, out_specs=[spec] * n_out, out_shape=[jax.ShapeDtypeStruct(shape, F32)] * n_out,
        compiler_params=_params(("parallel",)),
    )(*arrays)


def _add2(a, b, name, tm):
    return _elementwise(lambda p, q: (p + q,), name, 1, [a, b], tm)[0]


def _chip_sum(place, g, r, name, tm):
    n_layers, _, rows, cols = g.shape
    half = rows // 2
    nb = half // tm

    def body(place_ref, g_ref, r_ref, o_ref):
        o_ref[...] = (g_ref[...] + r_ref[...]).astype(BF16)

    blk = (None, None, tm, cols)
    return pl.pallas_call(
        body, name=name,
        grid_spec=pltpu.PrefetchScalarGridSpec(
            num_scalar_prefetch=1, grid=(n_layers, N_CHIPS, nb),
            in_specs=[pl.BlockSpec(blk, lambda l, k, i, p: (l, k, p[1] * nb + i, 0)),
                      pl.BlockSpec(blk, lambda l, k, i, p: (l, k, i, 0))],
            out_specs=pl.BlockSpec(blk, lambda l, k, i, p: (l, k, i, 0))),
        out_shape=jax.ShapeDtypeStruct((n_layers, N_CHIPS, half, cols), BF16),
        compiler_params=_params(("parallel", "parallel", "parallel")),
    )(place, g, r)


def _mesh_sum(place, g, r, got, into, layer, name, tm):
    _, _, rows, cols = g.shape
    half = rows // 2
    nb = half // tm

    def body(place_ref, g_ref, r_ref, got_ref, into_ref, o_ref):
        own = g_ref[...] + r_ref[...]
        o_ref[...] = (own + got_ref[0].astype(F32)) + (got_ref[1].astype(F32) + got_ref[2].astype(F32))

    blk = (None, None, tm, cols)
    return pl.pallas_call(
        body, name=name,
        grid_spec=pltpu.PrefetchScalarGridSpec(
            num_scalar_prefetch=1, grid=(nb,),
            in_specs=[pl.BlockSpec(blk, lambda i, p: (0, p[0], p[1] * nb + i, 0)),
                      pl.BlockSpec(blk, lambda i, p: (0, p[0], i, 0)),
                      pl.BlockSpec((None, 3, tm, cols), lambda i, p: (0, 0, i, 0)),
                      ANY],
            out_specs=pl.BlockSpec((None, tm, cols), lambda i, p: (layer, p[1] * nb + i, 0))),
        out_shape=jax.ShapeDtypeStruct(into.shape, F32),
        input_output_aliases={4: 0},
        compiler_params=_params(("parallel",)),
    )(place, g, r, got, into)


def _mesh_sum_small(place, ps, got, tm):
    nb = SMALL_HALF // tm

    def body(place_ref, ps_ref, got_ref, o_ref):
        o_ref[...] = (ps_ref[...] + got_ref[0]) + (got_ref[1] + got_ref[2])

    return pl.pallas_call(
        body, name="mesh_sum_small",
        grid_spec=pltpu.PrefetchScalarGridSpec(
            num_scalar_prefetch=1, grid=(nb,),
            in_specs=[pl.BlockSpec((tm, 128), lambda i, p: (p[1] * nb + i, 0)),
                      pl.BlockSpec((3, tm, 128), lambda i, p: (0, i, 0))],
            out_specs=pl.BlockSpec((tm, 128), lambda i, p: (p[1] * nb + i, 0))),
        out_shape=jax.ShapeDtypeStruct((SMALL_ROWS, 128), F32),
        compiler_params=_params(("parallel",)),
    )(place, ps, got)


def _adamw(g, w, m, v, name, tm):
    def fn(g, w, m, v):
        m = ADAM_B1 * m + (1.0 - ADAM_B1) * g
        v = ADAM_B2 * v + (1.0 - ADAM_B2) * (g * g)
        m_hat = m / (1.0 - ADAM_B1 ** ADAM_STEP)
        v_hat = v / (1.0 - ADAM_B2 ** ADAM_STEP)
        delta = -ADAM_LR * (m_hat / (jnp.sqrt(v_hat) + ADAM_EPS) + ADAM_WD * w)
        return g, delta, m, v

    return _elementwise(fn, name, 4, [g, w, m, v], tm)


def _position():
    x, y, c = lax.axis_index("x"), lax.axis_index("y"), lax.axis_index("c")
    other_chips = [(x, 1 - y), (1 - x, y), (1 - x, 1 - y)]
    return x, y, c, other_chips


def _remote(src, dst, sems, k, to):
    send_sems, recv_sems = sems
    return pltpu.make_async_remote_copy(src_ref=src, dst_ref=dst, send_sem=send_sems.at[k],
                                        recv_sem=recv_sems.at[k], device_id=to, device_id_type=MESH)


def _comm_call(body, name, n_in, out_shape, n_remote):
    return pl.pallas_call(
        body, name=name, in_specs=[ANY] * n_in, out_specs=[ANY] * len(out_shape), out_shape=out_shape,
        scratch_shapes=[pltpu.SemaphoreType.DMA((n_remote,)), pltpu.SemaphoreType.DMA((n_remote,))],
    )


def _halves(ref_rows, c):
    half = ref_rows // 2
    return pl.ds(half * c, half), pl.ds(half * (1 - c), half)


def _gather_weights(wi):
    def body(wi_ref, gi_ref, send_sems, recv_sems):
        _gather_start((wi_ref,), (gi_ref,), (send_sems, recv_sems))
        _gather_finish((wi_ref,), (gi_ref,), (send_sems, recv_sems))

    return _comm_call(body, "gather_weights", 1, _gather_shapes(wi), GATHER_SEMS)(wi)[0]


GATHER_SEMS = 6
SCATTER_SEMS = 9


def _gather_shapes(*arrays):
    return [jax.ShapeDtypeStruct((a.shape[0], N_CHIPS) + a.shape[1:], a.dtype) for a in arrays]


def _gather_sends(srcs, dsts, sems):
    x, y, c, chips = _position()
    out = []
    for k, (cx, cy) in enumerate(chips):
        for a, (src, dst) in enumerate(zip(srcs, dsts)):
            mine, _ = _halves(src.shape[1], c)
            out.append(_remote(src.at[:, mine], dst.at[:, k + 1, mine], sems, len(srcs) * k + a, (cx, cy, c)))
    return out


def _gather_own(srcs, dsts, local_sems):
    return [pltpu.make_async_copy(src, dst.at[:, 0], local_sems.at[a]) for a, (src, dst) in enumerate(zip(srcs, dsts))]


def _gather_start(srcs, dsts, sems):
    for cp in _gather_sends(srcs, dsts, sems):
        cp.start()


def _gather_finish(srcs, dsts, sems):
    x, y, c, _ = _position()
    n = len(srcs)
    forwards = []
    for k in range(3):
        for a, dst in enumerate(dsts):
            mine, _ = _halves(dst.shape[2], c)
            blk = dst.at[:, k + 1, mine]
            _remote(blk, blk, sems, n * k + a, (x, y, c)).wait_recv()
            forwards.append(_remote(blk, blk, sems, 3 * n + n * k + a, (x, y, 1 - c)))
            forwards[-1].start()
    for k in range(3):
        for a, dst in enumerate(dsts):
            _, other = _halves(dst.shape[2], c)
            blk = dst.at[:, k + 1, other]
            _remote(blk, blk, sems, 3 * n + n * k + a, (x, y, c)).wait_recv()
    for cp in _gather_sends(srcs, dsts, sems) + forwards:
        cp.wait_send()


def _scatter_sends(big, small, sems):
    x, y, c, chips = _position()
    n = len(big) + (small is not None)
    out = []
    for k, (cx, cy) in enumerate(chips):
        ck = 2 * cx + cy
        for a, (src, dst) in enumerate(big):
            out.append(_remote(src.at[:, ck], dst.at[:, k], sems, n * k + a, (cx, cy, c)))
        if small is not None:
            half, _ = _halves(SMALL_ROWS, c)
            out.append(_remote(small[0].at[half], small[1].at[k], sems, n * k + n - 1, (cx, cy, c)))
    return out


def _scatter_start(big, small, sems):
    for cp in _scatter_sends(big, small, sems):
        cp.start()


def _scatter_finish(big, small, sems):
    x, y, c, _ = _position()
    n = len(big) + (small is not None)
    for k in range(3):
        landing = [dst.at[:, k] for _, dst in big] + ([small[1].at[k]] if small is not None else [])
        for a, blk in enumerate(landing):
            _remote(blk, blk, sems, n * k + a, (x, y, c)).wait_recv()
    for cp in _scatter_sends(big, small, sems):
        cp.wait_send()


def _exchange_shapes(gi, go):
    return [jax.ShapeDtypeStruct(a.shape[:2] + (a.shape[2] // 2, a.shape[3]), F32) for a in (gi, go)]


def _exchange_copies(srcs, dsts, small, sems):
    x, y, c, _ = _position()
    cps = [_remote(src.at[:, :, _halves(src.shape[2], c)[1]], dst, sems, a, (x, y, 1 - c))
           for a, (src, dst) in enumerate(zip(srcs, dsts))]
    if small is not None:
        cps.append(_remote(small[0], small[1], sems, len(srcs), (x, y, 1 - c)))
    return cps


def _exchange_start(srcs, dsts, small, sems):
    for cp in _exchange_copies(srcs, dsts, small, sems):
        cp.start()


def _exchange_finish(srcs, dsts, small, sems):
    for cp in _exchange_copies(srcs, dsts, small, sems):
        cp.wait()


def _exchange_sibling(gi, go, sm):
    def body(gi_ref, go_ref, sm_ref, ri_ref, ro_ref, rs_ref, send_sems, recv_sems):
        args = ((gi_ref, go_ref), (ri_ref, ro_ref), (sm_ref, rs_ref), (send_sems, recv_sems))
        _exchange_start(*args)
        _exchange_finish(*args)

    out_shape = _exchange_shapes(gi, go) + [jax.ShapeDtypeStruct(sm.shape, F32)]
    return _comm_call(body, "exchange_sibling", 3, out_shape, 3)(gi, go, sm)


def _scatter_shapes(pi, po):
    return [jax.ShapeDtypeStruct((a.shape[0], 3) + a.shape[2:], a.dtype) for a in (pi, po)]


def _join_sibling(fi, fo, fs, late):
    def body(fi_ref, fo_ref, fs_ref, late_ref, oi_ref, oo_ref, os_ref, sum_ref, send_sems, recv_sems,
             sib_ref, chip_ref, got_ref):
        x, y, c, chips = _position()
        sems = (send_sems, recv_sems)
        sent, got = [], []
        for a, (src, dst) in enumerate(((fi_ref, oi_ref), (fo_ref, oo_ref))):
            mine, other = _halves(src.shape[1], c)
            sent.append(_remote(src.at[:, mine], dst.at[:, mine], sems, a, (x, y, 1 - c)))
            got.append(_remote(dst.at[:, other], dst.at[:, other], sems, a, (x, y, c)))
        mine, other = _halves(SMALL_ROWS, c)
        sent.append(_remote(fs_ref.at[mine], os_ref.at[mine], sems, 2, (x, y, 1 - c)))
        got.append(_remote(os_ref.at[other], os_ref.at[other], sems, 2, (x, y, c)))
        for cp in sent:
            cp.start()

        to_sibling = _remote(late_ref, sib_ref, sems, 3, (x, y, 1 - c))
        to_sibling.start()
        to_sibling.wait()
        chip_ref[...] = late_ref[...] + sib_ref[...]
        to_chips = [_remote(chip_ref, got_ref.at[k], sems, 4 + k, (cx, cy, c)) for k, (cx, cy) in enumerate(chips)]
        for cp in to_chips:
            cp.start()
        for cp in to_chips:
            cp.wait()
        sum_ref[...] = (chip_ref[...] + got_ref[0]) + (got_ref[1] + got_ref[2])

        for cp in got:
            cp.wait_recv()
        for cp in sent:
            cp.wait_send()

    vmem = pl.BlockSpec(memory_space=pltpu.VMEM)
    return pl.pallas_call(
        body, name="join_sibling", in_specs=[ANY] * 3 + [vmem], out_specs=[ANY] * 3 + [vmem],
        out_shape=[jax.ShapeDtypeStruct(a.shape, F32) for a in (fi, fo, fs, late)],
        scratch_shapes=[pltpu.SemaphoreType.DMA((7,)), pltpu.SemaphoreType.DMA((7,)),
                        pltpu.VMEM(late.shape, F32), pltpu.VMEM(late.shape, F32), pltpu.VMEM((3,) + late.shape, F32)],
        input_output_aliases={0: 0, 1: 1, 2: 2},
    )(fi, fo, fs, late)


SMALL_SIZES = (DEPTH * D_MODEL, D_MODEL, DEPTH * POOL_WIDTH, 128, DEPTH * 4 * GROUP_DIM * GROUP_DIM)
LOSS_SLOT = DEPTH * D_MODEL + D_MODEL + DEPTH * POOL_WIDTH + DEPTH * 8


def _pack_small(norm_g, final_g, pool_scale, forget_bias, pool_w, loss=None):
    fb = forget_bias.reshape(-1)
    if loss is not None:
        fb = jnp.concatenate([fb, loss.reshape(1)])
    fb = jnp.pad(fb, (0, 128 - fb.size))
    flat = jnp.concatenate([norm_g.reshape(-1), final_g.reshape(-1), pool_scale.reshape(-1), fb,
                            pool_w.reshape(-1)])
    return jnp.pad(flat, (0, SMALL_ROWS * 128 - flat.size)).reshape(SMALL_ROWS, 128)


def _unpack_small(packed):
    flat = packed.reshape(-1)
    offs = [0]
    for n in SMALL_SIZES:
        offs.append(offs[-1] + n)
    norm_g, final_g, pool_scale, fb, pool_w = [flat[offs[i]:offs[i + 1]] for i in range(5)]
    return (norm_g.reshape(DEPTH, D_MODEL), final_g, pool_scale.reshape(DEPTH, POOL_WIDTH),
            fb[:DEPTH * 8].reshape(DEPTH, 8), pool_w.reshape(DEPTH, 4, GROUP_DIM, GROUP_DIM))


def kernel(x, norm_g, w_in, forget_bias, pool_w, pool_scale, w_out, final_g, loss_target, m_norm_g, m_w_in, m_forget_bias, m_pool_w, m_pool_scale, m_w_out, m_final_g, v_norm_g, v_w_in, v_forget_bias, v_pool_w, v_pool_scale, v_w_out, v_final_g):
    x0 = x[0]
    tgt = loss_target[0]
    s = x0.shape[0]

    me = 2 * lax.axis_index("x") + lax.axis_index("y")
    place = jnp.stack([me, lax.axis_index("c")]).astype(jnp.int32)
    wi_b, wo_b = w_in.astype(BF16), w_out.astype(BF16)

    def whole_w_out(got):
        return jnp.concatenate([lax.dynamic_index_in_dim(got, jnp.bitwise_xor(k, me), axis=1, keepdims=False)[0]
                                for k in range(N_CHIPS)], axis=0)

    pool_w_b = pool_w.astype(BF16)
    fb_pad = jnp.pad(forget_bias, ((0, 0), (0, 128 - forget_bias.shape[1])))
    head_sel = (jnp.arange(ATTN_WIDTH)[None, :] // HEAD_DIM == jnp.arange(8)[:, None]).astype(BF16)

    saved, weights = [], []
    xl = x0
    got_i = _gather_weights(wi_b[0:1])
    for l in range(DEPTH):
        pu, pg, q, k, v, ag, lf, c, h, po, w_t = _inproj_fwd(
            place, xl, norm_g[l][None], None if l > 0 else wi_b[0], got_i, fb_pad[l][None],
            pool_w_b[l], pool_scale[l][None])
        if l + 1 < DEPTH:
            ma, o, lse, got_o, got_i = _attn_fwd(q, k, v, c, ag, carry=(wo_b[l:l + 1], wi_b[l + 1:l + 2]))
        else:
            ma, o, lse, got_o = _attn_fwd(q, k, v, c, ag, carry=(wo_b[l:l + 1],))
        w_o = whole_w_out(got_o)
        weights.append((w_t, w_o))
        saved.append((xl, h, pu, pg, q, k, v, ag, lf, c, po, ma, o, lse))
        if l + 1 < DEPTH:
            xl = _outproj_fwd(xl, po, ma, w_o)
    dx, g_final, loss_part = _loss_head(xl, po, ma, w_o, tgt, final_g[None])

    g_norm, g_fb, g_pw, g_ps = [], [], [], []
    parts = {}
    pending = None
    for l in reversed(range(DEPTH)):
        xl, h, pu, pg, q, k, v, ag, lf, c, po, ma, o, lse = saved[l]
        w_t, w_o = weights[l]
        dmp, do, dag, dt, gwo = _outproj_bwd(dx, po, ma, ag, o, w_o, head_sel)
        if pending is None:
            dq, dk, dv, dcs, drs = _attn_bwd(q, k, v, do, c, lse, dt.reshape(N_PAIRS, 2, s))
        else:
            dq, dk, dv, dcs, drs, qi, qo = _attn_bwd(q, k, v, do, c, lse, dt.reshape(N_PAIRS, 2, s), carry=pending)
            parts[l + 1] += (qi, qo)
        drs = jnp.pad(drs[:, 0:2, :].reshape(8, s).T, ((0, 0), (0, 120)))
        dpf, dfb = _forget_bwd(dcs, drs, lf)
        dpu, dpg, gpw, gps = _pool_bwd(pu, pg, dmp, pool_w_b[l], pool_scale[l][None])
        dsec = (dpu, dpg, dq, dk, dv, dag, dpf)
        gwi = _inproj_bwd_w(h, dsec)
        g_fb.append(dfb[0, :8])
        g_pw.append(gpw)
        g_ps.append(gps[0])
        send_i = gwi[None]
        send_o = gwo.reshape(1, N_CHIPS, SHARD_ROWS, D_MODEL)
        if l > 0:
            dx, gn, ri, ro = _inproj_bwd_x(dsec, w_t, xl, norm_g[l][None], dx, exchange=(send_i, send_o))
            g_norm.append(gn[0])
        else:
            g_norm.append(jnp.zeros((D_MODEL,), F32))
            g_norm, g_fb, g_pw, g_ps = [t[::-1] for t in (g_norm, g_fb, g_pw, g_ps)]
            send_s = _pack_small(jnp.stack(g_norm), g_final[0], jnp.stack(g_ps), jnp.stack(g_fb), jnp.stack(g_pw),
                                 loss=loss_part[0, 0])
            ri, ro, rs = _exchange_sibling(send_i, send_o, send_s)
        pending = (_chip_sum(place, send_i, ri, "chip_sum_w_in", 256),
                   _chip_sum(place, send_o, ro, "chip_sum_w_out", SHARD_ROWS // 2))
        parts[l] = (send_i, ri, send_o, ro)
    ps_ = _add2(send_s, rs, "chip_sum_small", SMALL_ROWS // 4)
    dx, gn0, qi, qo, qs = _inproj_bwd_x(dsec, w_t, xl, norm_g[0][None], dx, scatter=(*pending, ps_))
    parts[0] += (qi, qo)
    fi = lax.empty((DEPTH, D_MODEL, SHARD_COLS), F32)
    fo = lax.empty((DEPTH, SHARD_ROWS, D_MODEL), F32)
    for l in range(DEPTH):
        send_i, ri, send_o, ro, qi, qo = parts[l]
        fi = _mesh_sum(place, send_i, ri, qi, fi, l, "mesh_sum_w_in", 256)
        fo = _mesh_sum(place, send_o, ro, qo, fo, l, "mesh_sum_w_out", SHARD_ROWS // 2)
    hs = _mesh_sum_small(place, ps_, qs, SMALL_HALF // 4)
    fi, fo, fs, gn0 = _join_sibling(fi, fo, hs, gn0.reshape(8, 128))
    fs = jnp.concatenate([fs[0:8] + gn0, fs[8:]], axis=0)

    col_major = lambda a: jnp.transpose(a, (2, 0, 1))
    flat_o = lambda a: a.reshape(-1, D_MODEL)
    out_i = _adamw(col_major(fi), col_major(w_in), col_major(m_w_in), col_major(v_w_in), "adamw_w_in", 77)
    out_i = [jnp.transpose(t, (1, 2, 0)) for t in out_i]
    out_o = _adamw(flat_o(fo), flat_o(w_out), flat_o(m_w_out), flat_o(v_w_out), "adamw_w_out", 256)
    out_s = _adamw(fs, _pack_small(norm_g, final_g, pool_scale, forget_bias, pool_w),
                   _pack_small(m_norm_g, m_final_g, m_pool_scale, m_forget_bias, m_pool_w),
                   _pack_small(v_norm_g, v_final_g, v_pool_scale, v_forget_bias, v_pool_w),
                   "adamw_small", SMALL_ROWS // 4)

    loss = out_s[0].reshape(-1)[LOSS_SLOT]
    groups = []
    for oi, oo, os_ in zip(out_i, out_o, out_s):
        sn, sf, sp, sb, sw = _unpack_small(os_)
        groups.append((sn, oi.reshape(w_in.shape), sb, sw, sp, oo.reshape(w_out.shape), sf))
    return (loss, dx[None]) + tuple(t for grp in groups for t in grp)
```

```python
import jax
import jax.numpy as jnp
from jax import lax
from jax.experimental import pallas as pl
from jax.experimental.pallas import tpu as pltpu

F32 = jnp.float32
BF16 = jnp.bfloat16

D_MODEL = 1024
DEPTH = 4
POOL_WIDTH = 512
ATTN_WIDTH = 512
HEAD_DIM = 64
PAIR = 2 * HEAD_DIM
N_PAIRS = ATTN_WIDTH // PAIR
POOL_WINDOWS = (2, 4, 8, 16)
GROUP_DIM = 128
HALO = 16
IN_COLS = 3080
OFF_F = 3072
IN_COLS_PAD = 3200
SECTIONS = ((0, 512), (512, 512), (1024, 512), (1536, 512), (2048, 512), (2560, 512), (OFF_F, 128))
N_CHIPS = 4
SHARD_COLS = IN_COLS // N_CHIPS
SHARD_ROWS = D_MODEL // N_CHIPS
RMS_EPS = 1e-6
NEG = -1e30
Q_SCALE = 0.125

ADAM_LR = 0.001
ADAM_B1 = 0.9
ADAM_B2 = 0.999
ADAM_EPS = 1e-08
ADAM_WD = 0.01
ADAM_STEP = 10

SMALL_ROWS = 2112
SMALL_HALF = SMALL_ROWS // 2

NT = (((1,), (1,)), ((), ()))
TN = (((0,), (0,)), ((), ()))
MESH = pl.DeviceIdType.MESH
ANY = pl.BlockSpec(memory_space=pl.ANY)


def _params(semantics, vmem_mb=48):
    return pltpu.CompilerParams(dimension_semantics=semantics, vmem_limit_bytes=vmem_mb << 20)


def _row_spec(tm, cols):
    return pl.BlockSpec((tm, cols), lambda i: (i, 0))


def _full_spec(shape):
    return pl.BlockSpec(shape, lambda *_: (0,) * len(shape))


def _sigmoid(x):
    return jax.nn.sigmoid(x)


def _scan_rows(a, reverse=False):
    n = a.shape[0]
    row = lax.broadcasted_iota(jnp.int32, a.shape, 0)
    k = 1
    while k < n:
        if reverse:
            a = a + jnp.where(row < n - k, pltpu.roll(a, n - k, 0), 0.0)
        else:
            a = a + jnp.where(row >= k, pltpu.roll(a, k, 0), 0.0)
        k *= 2
    return a


def _inproj_fwd(place, x, g, own, got, fb, pw, ps, tm=512):
    s = x.shape[0]

    def body(place_ref, x_ref, g_ref, fb_ref, pw_ref, ps_ref, *refs):
        own_ref = refs[0] if own is not None else None
        (got_ref, pu_ref, pg_ref, q_ref, k_ref, v_ref, ag_ref, lf_ref, c_ref, h_ref, po_ref, wt_ref,
         carry_ref, halo_ref, w_ref, piece_ref, sem) = refs[own is not None:]
        step = pl.program_id(0)

        @pl.when(step == 0)
        def _():
            carry_ref[...] = jnp.zeros_like(carry_ref)
            halo_ref[...] = jnp.zeros_like(halo_ref)
            for chip in range(N_CHIPS):
                rel = jnp.bitwise_xor(chip, place_ref[0])

                def fetch(src):
                    cp = pltpu.make_async_copy(src, piece_ref, sem)
                    cp.start()
                    cp.wait()

                if own_ref is None:
                    fetch(got_ref.at[0, rel])
                else:
                    pl.when(rel == 0)(lambda: fetch(own_ref))
                    pl.when(rel != 0)(lambda: fetch(got_ref.at[0, rel]))
                w_ref[:, chip * SHARD_COLS:(chip + 1) * SHARD_COLS] = piece_ref[...]
            w_ref[:, IN_COLS:] = jnp.zeros((D_MODEL, IN_COLS_PAD - IN_COLS), BF16)
            for t in range(IN_COLS_PAD // 128):
                cols = slice(t * 128, (t + 1) * 128)
                wt_ref[cols, :] = jnp.transpose(w_ref[:, cols].astype(F32)).astype(BF16)

        xf = x_ref[...]
        r = lax.rsqrt(jnp.mean(xf * xf, axis=-1, keepdims=True) + RMS_EPS)
        h = (xf * r * g_ref[...]).astype(BF16)
        h_ref[...] = h

        def proj(sec):
            off, n = SECTIONS[sec]
            return jnp.dot(h, w_ref[:, off:off + n], preferred_element_type=F32)

        pu_ref[...] = proj(0).astype(BF16)
        pg_ref[...] = proj(1).astype(BF16)
        u = pu_ref[...].astype(F32)
        d = _pool_delta(u, halo_ref[...], step * tm)
        halo_ref[...] = u[tm - HALO:, :]
        for gi in range(len(POOL_WINDOWS)):
            cols = slice(gi * GROUP_DIM, (gi + 1) * GROUP_DIM)
            zg = jnp.dot(d[gi].astype(BF16), pw_ref[gi], preferred_element_type=F32)
            gate = pg_ref[:, cols].astype(F32)
            po_ref[:, cols] = (zg * ps_ref[:, cols] * (gate * _sigmoid(gate))).astype(BF16)
        q_ref[...] = (proj(2) * Q_SCALE).astype(BF16)
        k_ref[...] = proj(3).astype(BF16)
        v_ref[...] = proj(4).astype(BF16)
        ag_ref[...] = proj(5).astype(BF16)
        z = proj(6) + fb_ref[...]
        lf = jnp.minimum(z, 0.0) - jnp.log(1.0 + jnp.exp(-jnp.abs(z)))
        lf_ref[...] = lf
        c_ref[...] = _scan_rows(lf) + carry_ref[0:1, :]
        carry_ref[0:1, :] = c_ref[tm - 1:tm, :]

    act = jax.ShapeDtypeStruct((s, 512), BF16)
    rows = lambda cols: pl.BlockSpec((tm, cols), lambda i, p: (i, 0))
    whole = lambda shape: pl.BlockSpec(shape, lambda i, p: (0,) * len(shape))
    shards = ([] if own is None else [own]) + [got]
    return pl.pallas_call(
        body, name="inproj_fwd" if own is None else "inproj_fwd_first",
        grid_spec=pltpu.PrefetchScalarGridSpec(
            num_scalar_prefetch=1, grid=(s // tm,),
            in_specs=[rows(D_MODEL), whole((1, D_MODEL)), whole((1, 128)), whole((4, GROUP_DIM, GROUP_DIM)),
                      whole((1, 512))] + [ANY] * len(shards),
            out_specs=[rows(512)] * 6 + [rows(128), rows(128), rows(D_MODEL), rows(512),
                                         whole((IN_COLS_PAD, D_MODEL))],
            scratch_shapes=[pltpu.VMEM((8, 128), F32), pltpu.VMEM((HALO, 512), F32),
                            pltpu.VMEM((D_MODEL, IN_COLS_PAD), BF16),
                            pltpu.VMEM((D_MODEL, SHARD_COLS), BF16), pltpu.SemaphoreType.DMA]),
        out_shape=[act] * 6 + [jax.ShapeDtypeStruct((s, 128), F32), jax.ShapeDtypeStruct((s, 128), F32),
                               jax.ShapeDtypeStruct((s, D_MODEL), BF16), act,
                               jax.ShapeDtypeStruct((IN_COLS_PAD, D_MODEL), BF16)],
        compiler_params=_params(("arbitrary",), vmem_mb=56),
    )(place, x, g, fb, pw, ps, *shards)


def _window_sums(ext, forward):
    n = ext.shape[0]
    outs = []
    for gi, w in enumerate(POOL_WINDOWS):
        a = ext[:, gi * GROUP_DIM:(gi + 1) * GROUP_DIM]
        k = 1
        while k < w:
            a = a + pltpu.roll(a, (n - k) if forward else k, 0)
            k *= 2
        outs.append(a)
    return outs


def _over_count(a, row0, gi):
    w = POOL_WINDOWS[gi]
    t = row0 + lax.broadcasted_iota(jnp.int32, (HALO, GROUP_DIM), 0)
    head = a[0:HALO, :] / jnp.minimum(t + 1, w).astype(F32)
    return jnp.concatenate([head, a[HALO:, :] * (1.0 / w)], axis=0)


def _pool_delta(u, halo, row0):
    sums = _window_sums(jnp.concatenate([halo, u], axis=0), forward=False)
    return [_over_count(sums[gi][HALO:, :], row0, gi) - u[:, gi * GROUP_DIM:(gi + 1) * GROUP_DIM]
            for gi in range(len(POOL_WINDOWS))]


def _split3(x):
    hi = x.astype(BF16).astype(F32)
    r1 = x - hi
    lo = r1.astype(BF16).astype(F32)
    return hi, lo, (r1 - lo).astype(BF16).astype(F32)


def _own(idx, h):
    return (idx < HEAD_DIM) if h == 0 else (idx >= HEAD_DIM)


def _spare(h):
    return HEAD_DIM if h == 0 else 0


def _attn_fwd(q, k, v, c, ag, carry=None, tq=512, tk=512, chunk=512):
    s = q.shape[0]
    n_carry = 0 if carry is None else len(carry)

    def body(*refs):
        q_ref, k_ref, v_ref, c_ref, ag_ref = refs[:5]
        carry_src = refs[5:5 + n_carry]
        ma_ref, o_ref, lse_ref = refs[5 + n_carry:8 + n_carry]
        carry_dst = refs[8 + n_carry:8 + 2 * n_carry]
        kx_ref, vt_ref, acc0_ref, acc1_ref, sta_ref, stb_ref = refs[8 + 2 * n_carry:14 + 2 * n_carry]
        carry_sems = refs[14 + 2 * n_carry:]
        j = pl.program_id(0)
        i = pl.program_id(1)

        if n_carry:
            @pl.when((j == 0) & (i == 0))
            def _():
                _gather_start(carry_src, carry_dst, carry_sems[:2])
                for cp in _gather_own(carry_src, carry_dst, carry_sems[2]):
                    cp.start()

        @pl.when(i == 0)
        def _():
            def prep(ch, carry):
                rows = pl.ds(pl.multiple_of(ch * chunk, chunk), chunk)
                kk = k_ref[rows, :].astype(F32)
                cc = c_ref[rows, :]
                vt = jnp.transpose(v_ref[rows, :].astype(F32))
                lane = lax.broadcasted_iota(jnp.int32, (chunk, PAIR), 1)
                sub = lax.broadcasted_iota(jnp.int32, (PAIR, chunk), 0)
                for h in range(2):
                    ccol = jnp.sum(jnp.where(lane == 2 * j + h, cc, 0.0), axis=1, keepdims=True)
                    kx = jnp.where(_own(lane, h), kk, 0.0)
                    for t, part in enumerate(_split3(-ccol)):
                        kx = jnp.where(lane == _spare(h) + t, part, kx)
                    kx_ref[h, rows, :] = kx.astype(BF16)
                    vt_ref[h, :, rows] = jnp.where(_own(sub, h), vt,
                                                   jnp.where(sub == _spare(h), 1.0, 0.0)).astype(BF16)
                return carry

            lax.fori_loop(0, s // chunk, prep, 0)

        qq = q_ref[...].astype(F32)
        lane_q = lax.broadcasted_iota(jnp.int32, (tq, PAIR), 1)
        qx = []
        for h in range(2):
            ones = (lane_q >= _spare(h)) & (lane_q < _spare(h) + 3)
            qx.append(jnp.transpose(jnp.where(_own(lane_q, h), qq, jnp.where(ones, 1.0, 0.0))).astype(BF16))
        accs = (acc0_ref, acc1_ref)
        for acc in accs:
            acc[...] = jnp.zeros_like(acc)

        def rows_of(kb):
            return pl.ds(pl.multiple_of(kb * tk, tk), tk)

        def scores(kb, dst):
            for h in range(2):
                dst[h] = jnp.dot(kx_ref[h, rows_of(kb), :], qx[h], preferred_element_type=F32)

        def consume(kb, src, m, masked):
            m_out, alpha, pv = [], [], []
            for h in range(2):
                sh = src[h]
                if masked:
                    kpos = kb * tk + lax.broadcasted_iota(jnp.int32, (tk, tq), 0)
                    qpos = i * tq + lax.broadcasted_iota(jnp.int32, (tk, tq), 1)
                    sh = jnp.where(qpos >= kpos, sh, NEG)
                m_new = jnp.maximum(m[h], jnp.max(sh, axis=0, keepdims=True))
                pt = jnp.exp(sh - m_new).astype(BF16)
                alpha.append(jnp.exp(m[h] - m_new))
                pv.append(jnp.dot(vt_ref[h, :, rows_of(kb)], pt, preferred_element_type=F32))
                m_out.append(m_new)
            for h in range(2):
                accs[h][...] = accs[h][...] * alpha[h] + pv[h]
            return tuple(m_out)

        n_full = (i * tq) // tk

        def two_blocks(t, m):
            kb = 2 * t
            scores(kb + 1, stb_ref)
            m = consume(kb, sta_ref, m, False)
            scores(kb + 2, sta_ref)
            return consume(kb + 1, stb_ref, m, False)

        def odd_tail(m):
            scores(n_full, stb_ref)
            m = consume(n_full - 1, sta_ref, m, False)
            return consume(n_full, stb_ref, m, True)

        def even_tail(m):
            return consume(n_full, sta_ref, m, True)

        scores(0, sta_ref)
        m0 = jnp.full((1, tq), NEG, F32)
        n_two = n_full // 2
        m = lax.fori_loop(0, n_two // 2, lambda u, m: two_blocks(2 * u + 1, two_blocks(2 * u, m)), (m0, m0))
        m = lax.cond(lax.rem(n_two, 2) == 1, lambda m: two_blocks(n_two - 1, m), lambda m: m, m)
        m = lax.cond(lax.rem(n_full, 2) == 1, odd_tail, even_tail, m)

        sub_o = lax.broadcasted_iota(jnp.int32, (PAIR, tq), 0)
        den = [accs[h][_spare(h):_spare(h) + 1, :] for h in range(2)]
        ot = jnp.where(sub_o < HEAD_DIM, acc0_ref[...] / den[0], acc1_ref[...] / den[1])
        o = jnp.transpose(ot)
        o_ref[...] = o.astype(BF16)
        gate = ag_ref[...].astype(F32)
        ma_ref[...] = (o * (gate * _sigmoid(gate))).astype(BF16)
        sub8 = lax.broadcasted_iota(jnp.int32, (8, tq), 0)
        lse_ref[...] = jnp.where(sub8 == 0, m[0] + jnp.log(den[0]), m[1] + jnp.log(den[1]))

        if n_carry:
            @pl.when((j == N_PAIRS - 1) & (i == s // tq - 1))
            def _():
                _gather_finish(carry_src, carry_dst, carry_sems[:2])
                for cp in _gather_own(carry_src, carry_dst, carry_sems[2]):
                    cp.wait()

    carried = [] if carry is None else list(carry)
    carry_sem = ([pltpu.SemaphoreType.DMA((GATHER_SEMS * n_carry,))] * 2
                 + [pltpu.SemaphoreType.DMA((n_carry,))]) if n_carry else []
    return pl.pallas_call(
        body, name="attn_fwd_gather" if n_carry else "attn_fwd", grid=(N_PAIRS, s // tq),
        in_specs=[pl.BlockSpec((tq, PAIR), lambda j, i: (i, j)),
                  pl.BlockSpec((s, PAIR), lambda j, i: (0, j)),
                  pl.BlockSpec((s, PAIR), lambda j, i: (0, j)),
                  pl.BlockSpec((s, 128), lambda j, i: (0, 0)),
                  pl.BlockSpec((tq, PAIR), lambda j, i: (i, j))] + [ANY] * n_carry,
        out_specs=[pl.BlockSpec((tq, PAIR), lambda j, i: (i, j)),
                   pl.BlockSpec((tq, PAIR), lambda j, i: (i, j)),
                   pl.BlockSpec((None, 8, tq), lambda j, i: (j, 0, i))] + [ANY] * n_carry,
        out_shape=[jax.ShapeDtypeStruct((s, ATTN_WIDTH), BF16), jax.ShapeDtypeStruct((s, ATTN_WIDTH), BF16),
                   jax.ShapeDtypeStruct((N_PAIRS, 8, s), F32)] + (_gather_shapes(*carried) if n_carry else []),
        scratch_shapes=[pltpu.VMEM((2, s, PAIR), BF16), pltpu.VMEM((2, PAIR, s), BF16),
                        pltpu.VMEM((PAIR, tq), F32), pltpu.VMEM((PAIR, tq), F32),
                        pltpu.VMEM((2, tk, tq), F32), pltpu.VMEM((2, tk, tq), F32)] + carry_sem,
        compiler_params=_params(("arbitrary", "arbitrary")),
    )(q, k, v, c, ag, *carried)


def _outproj_fwd(x, po, ma, wo, tm=512):
    s = x.shape[0]

    def body(x_ref, po_ref, ma_ref, wo_ref, xn_ref):
        xn_ref[...] = (x_ref[...]
                       + jnp.dot(po_ref[...], wo_ref[0:POOL_WIDTH, :], preferred_element_type=F32)
                       + jnp.dot(ma_ref[...], wo_ref[POOL_WIDTH:, :], preferred_element_type=F32))

    return pl.pallas_call(
        body, name="outproj_fwd", grid=(s // tm,),
        in_specs=[_row_spec(tm, D_MODEL), _row_spec(tm, 512), _row_spec(tm, 512),
                  _full_spec((D_MODEL, D_MODEL))],
        out_specs=_row_spec(tm, D_MODEL),
        out_shape=jax.ShapeDtypeStruct((s, D_MODEL), F32),
        compiler_params=_params(("parallel",)),
    )(x, po, ma, wo)


def _loss_head(x, po, ma, wo, tgt, g, tm=512):
    s = x.shape[0]

    def body(x_ref, po_ref, ma_ref, wo_ref, t_ref, g_ref, dx_ref, dg_ref, loss_ref):
        @pl.when(pl.program_id(0) == 0)
        def _():
            dg_ref[...] = jnp.zeros_like(dg_ref)
            loss_ref[...] = jnp.zeros_like(loss_ref)

        xf = (x_ref[...]
              + jnp.dot(po_ref[...], wo_ref[0:POOL_WIDTH, :], preferred_element_type=F32)
              + jnp.dot(ma_ref[...], wo_ref[POOL_WIDTH:, :], preferred_element_type=F32))
        r = lax.rsqrt(jnp.mean(xf * xf, axis=-1, keepdims=True) + RMS_EPS)
        xh = xf * r
        gg = g_ref[...]
        e = xh * gg - t_ref[...]
        loss_ref[...] += 0.5 * jnp.sum(jnp.mean(e * e, axis=-1, keepdims=True))
        dy = e * (1.0 / D_MODEL)
        u = dy * gg
        dx_ref[...] = r * (u - xh * jnp.mean(xh * u, axis=-1, keepdims=True))
        dg_ref[...] += jnp.sum(dy * xh, axis=0, keepdims=True)

    return pl.pallas_call(
        body, name="loss_head", grid=(s // tm,),
        in_specs=[_row_spec(tm, D_MODEL), _row_spec(tm, 512), _row_spec(tm, 512), _full_spec((D_MODEL, D_MODEL)),
                  _row_spec(tm, D_MODEL), _full_spec((1, D_MODEL))],
        out_specs=[_row_spec(tm, D_MODEL), _full_spec((1, D_MODEL)), _full_spec((8, 128))],
        out_shape=[jax.ShapeDtypeStruct((s, D_MODEL), F32), jax.ShapeDtypeStruct((1, D_MODEL), F32),
                   jax.ShapeDtypeStruct((8, 128), F32)],
        compiler_params=_params(("arbitrary",)),
    )(x, po, ma, wo, tgt, g)


def _outproj_bwd(dx, po, ma, ag, o, wo, head_sel, tm=512):
    s = dx.shape[0]

    def body(dx_ref, po_ref, ma_ref, ag_ref, o_ref, wo_ref, sel_ref, dmp_ref, do_ref, dag_ref, dt_ref, gwo_ref):
        @pl.when(pl.program_id(0) == 0)
        def _():
            gwo_ref[...] = jnp.zeros_like(gwo_ref)

        dxb = dx_ref[...].astype(BF16)
        dm = lax.dot_general(dxb, wo_ref[...], NT, preferred_element_type=F32)
        dmp_ref[...] = dm[:, 0:POOL_WIDTH].astype(BF16)
        dma = dm[:, POOL_WIDTH:]
        gate = ag_ref[...].astype(F32)
        of = o_ref[...].astype(F32)
        sg = _sigmoid(gate)
        do = dma * (gate * sg)
        do_ref[...] = do.astype(BF16)
        dag_ref[...] = (dma * of * (sg * (1.0 + gate * (1.0 - sg)))).astype(BF16)
        prod = do * of
        hi = prod.astype(BF16)
        lo = (prod - hi.astype(F32)).astype(BF16)
        dt_ref[...] = (lax.dot_general(sel_ref[...], hi, NT, preferred_element_type=F32)
                       + lax.dot_general(sel_ref[...], lo, NT, preferred_element_type=F32))
        gwo_ref[0:POOL_WIDTH, :] += lax.dot_general(po_ref[...], dxb, TN, preferred_element_type=F32)
        gwo_ref[POOL_WIDTH:, :] += lax.dot_general(ma_ref[...], dxb, TN, preferred_element_type=F32)

    act = jax.ShapeDtypeStruct((s, 512), BF16)
    return pl.pallas_call(
        body, name="outproj_bwd", grid=(s // tm,),
        in_specs=[_row_spec(tm, D_MODEL)] + [_row_spec(tm, 512)] * 4
                 + [_full_spec((D_MODEL, D_MODEL)), _full_spec((8, ATTN_WIDTH))],
        out_specs=[_row_spec(tm, 512)] * 3 + [pl.BlockSpec((8, tm), lambda i: (0, i)),
                                              _full_spec((D_MODEL, D_MODEL))],
        out_shape=[act, act, act, jax.ShapeDtypeStruct((8, s), F32),
                   jax.ShapeDtypeStruct((D_MODEL, D_MODEL), F32)],
        compiler_params=_params(("arbitrary",)),
    )(dx, po, ma, ag, o, wo, head_sel)


def _attn_bwd(q, k, v, do, c, lse, dt, carry=None, tq=256, chunk=512):
    s = q.shape[0]
    tk = 2 * tq
    nq = s // tq
    n_carry = 0 if carry is None else len(carry)

    def body(*refs):
        q_ref, k_ref, v_ref, do_ref, c_ref, lse_ref, dt_ref = refs[:7]
        carry_src = refs[7:7 + n_carry]
        dq_ref, dk_ref, dv_ref, dcs_ref, drs_ref = refs[7 + n_carry:12 + n_carry]
        carry_dst = refs[12 + n_carry:12 + 2 * n_carry]
        (qxt_ref, doxt_ref, qm_ref, dox_ref, dqt0_ref, dqt1_ref, dk0_ref, dk1_ref, dvx_ref,
         kx_ref, vx_ref, ktm_ref, sa_ref, sb_ref) = refs[12 + 2 * n_carry:26 + 2 * n_carry]
        carry_sems = refs[26 + 2 * n_carry:]
        carry_big = tuple(zip(carry_src, carry_dst))
        j = pl.program_id(0)
        b = pl.program_id(1)
        dqts = (dqt0_ref, dqt1_ref)
        dks = (dk0_ref, dk1_ref)

        if n_carry:
            @pl.when((j == 0) & (b == 0))
            def _():
                _scatter_start(carry_big, None, carry_sems)

        @pl.when(b == 0)
        def _():
            def prep(ch, carry):
                rows = pl.ds(pl.multiple_of(ch * chunk, chunk), chunk)
                qq = q_ref[rows, :].astype(F32)
                dd = do_ref[rows, :].astype(F32)
                qt = jnp.transpose(qq)
                ddt = jnp.transpose(dd)
                lane = lax.broadcasted_iota(jnp.int32, (chunk, PAIR), 1)
                sub = lax.broadcasted_iota(jnp.int32, (PAIR, chunk), 0)
                for h in range(2):
                    sp = _spare(h)
                    qm_ref[h, rows, :] = jnp.where(_own(lane, h), qq, jnp.where(lane == sp, 1.0, 0.0)).astype(BF16)
                    dox_ref[h, rows, :] = jnp.where(_own(lane, h), dd, 0.0).astype(BF16)
                    qx = jnp.where(_own(sub, h), qt, jnp.where((sub >= sp) & (sub < sp + 3), 1.0, 0.0))
                    for t, part in enumerate(_split3(-lse_ref[pl.ds(h, 1), rows])):
                        qx = jnp.where(sub == sp + 3 + t, part, qx)
                    qxt_ref[h, :, rows] = qx.astype(BF16)
                    dx = jnp.where(_own(sub, h), ddt, 0.0)
                    for t, part in enumerate(_split3(-dt_ref[pl.ds(h, 1), rows])):
                        dx = jnp.where(sub == sp + t, part, dx)
                    doxt_ref[h, :, rows] = dx.astype(BF16)
                return carry

            lax.fori_loop(0, s // chunk, prep, 0)
            for ref in dqts:
                ref[...] = jnp.zeros_like(ref)

        kk = k_ref[...].astype(F32)
        vv = v_ref[...].astype(F32)
        cc = c_ref[...]
        kt = jnp.transpose(kk)
        lane = lax.broadcasted_iota(jnp.int32, (tk, PAIR), 1)
        sub = lax.broadcasted_iota(jnp.int32, (PAIR, tk), 0)
        for h in range(2):
            sp = _spare(h)
            ccol = jnp.sum(jnp.where(lane == 2 * j + h, cc, 0.0), axis=1, keepdims=True)
            kx = jnp.where(_own(lane, h), kk, jnp.where((lane >= sp + 3) & (lane < sp + 6), 1.0, 0.0))
            for t, part in enumerate(_split3(-ccol)):
                kx = jnp.where(lane == sp + t, part, kx)
            kx_ref[h] = kx.astype(BF16)
            vx_ref[h] = jnp.where(_own(lane, h), vv, jnp.where((lane >= sp) & (lane < sp + 3), 1.0, 0.0)).astype(BF16)
            ktm_ref[h] = jnp.where(_own(sub, h), kt, jnp.where(sub == sp, 1.0, 0.0)).astype(BF16)
        for ref in dks:
            ref[...] = jnp.zeros_like(ref)
        dvx_ref[...] = jnp.zeros_like(dvx_ref)

        def cols_of(i):
            return pl.ds(pl.multiple_of(i * tq, tq), tq)

        def scores(i, dst):
            cols = cols_of(jnp.minimum(i, nq - 1))
            for h in range(2):
                dst[h] = jnp.dot(kx_ref[h], qxt_ref[h, :, cols], preferred_element_type=F32)
                dst[2 + h] = jnp.dot(vx_ref[h], doxt_ref[h, :, cols], preferred_element_type=F32)

        def consume(i, src, masked):
            cols = cols_of(i)
            for h in range(2):
                arg = src[h]
                if masked:
                    kpos = b * tk + lax.broadcasted_iota(jnp.int32, (tk, tq), 0)
                    qpos = i * tq + lax.broadcasted_iota(jnp.int32, (tk, tq), 1)
                    arg = jnp.where(qpos >= kpos, arg, NEG)
                pt = jnp.exp(arg)
                dst = (pt * src[2 + h]).astype(BF16)
                dvx_ref[...] += jnp.dot(pt.astype(BF16), dox_ref[h, cols, :], preferred_element_type=F32)
                dks[h][...] += jnp.dot(dst, qm_ref[h, cols, :], preferred_element_type=F32)
                dqts[h][:, cols] += jnp.dot(ktm_ref[h], dst, preferred_element_type=F32)

        i0 = 2 * b

        def two_blocks(t, masked):
            i = i0 + 2 * t
            scores(i + 1, sb_ref)
            consume(i, sa_ref, masked)
            scores(i + 2, sa_ref)
            consume(i + 1, sb_ref, masked)

        def four_blocks(u, carry):
            two_blocks(1 + 2 * u, False)
            two_blocks(2 + 2 * u, False)
            return carry

        scores(i0, sa_ref)
        two_blocks(0, True)
        n_open = (nq - i0) // 2 - 1
        lax.fori_loop(0, n_open // 2, four_blocks, 0)

        @pl.when(lax.rem(n_open, 2) == 1)
        def _():
            two_blocks(n_open, False)

        dv_ref[...] = dvx_ref[...].astype(BF16)
        dk_ref[...] = jnp.where(lane < HEAD_DIM, dk0_ref[...], dk1_ref[...]).astype(BF16)
        dcs_ref[...] = jnp.where(lane == _spare(0), dk0_ref[...], jnp.where(lane == _spare(1), dk1_ref[...], 0.0))

        @pl.when(b == s // tk - 1)
        def _():
            sub8 = lax.broadcasted_iota(jnp.int32, (8, s), 0)
            drs_ref[...] = jnp.where(sub8 == 0, dqt0_ref[_spare(0):_spare(0) + 1, :],
                                     jnp.where(sub8 == 1, dqt1_ref[_spare(1):_spare(1) + 1, :], 0.0))
            sub_c = lax.broadcasted_iota(jnp.int32, (PAIR, chunk), 0)
            for ch in range(s // chunk):
                rows = pl.ds(ch * chunk, chunk)
                both = jnp.where(sub_c < HEAD_DIM, dqt0_ref[:, rows], dqt1_ref[:, rows])
                dq_ref[rows, :] = (jnp.transpose(both) * Q_SCALE).astype(BF16)

        if n_carry:
            @pl.when((j == N_PAIRS - 1) & (b == s // tk - 1))
            def _():
                _scatter_finish(carry_big, None, carry_sems)

    act = jax.ShapeDtypeStruct((s, ATTN_WIDTH), BF16)
    pair_rows = pl.BlockSpec((s, PAIR), lambda j, b: (0, j))
    pair_blk = pl.BlockSpec((tk, PAIR), lambda j, b: (b, j))
    stat = pl.BlockSpec((None, 2, s), lambda j, b: (j, 0, 0))
    carried = [] if carry is None else list(carry)
    carry_sem = [pltpu.SemaphoreType.DMA((SCATTER_SEMS,))] * 2 if n_carry else []
    return pl.pallas_call(
        body, name="attn_bwd_scatter" if n_carry else "attn_bwd", grid=(N_PAIRS, s // tk),
        in_specs=[pair_rows, pair_blk, pair_blk, pair_rows,
                  pl.BlockSpec((tk, 128), lambda j, b: (b, 0)),
                  pl.BlockSpec((None, 8, s), lambda j, b: (j, 0, 0)), stat] + [ANY] * n_carry,
        out_specs=[pair_rows, pair_blk, pair_blk, pair_blk,
                   pl.BlockSpec((None, 8, s), lambda j, b: (j, 0, 0))] + [ANY] * n_carry,
        out_shape=[act, act, act, jax.ShapeDtypeStruct((s, N_PAIRS * 128), F32),
                   jax.ShapeDtypeStruct((N_PAIRS, 8, s), F32)] + (_scatter_shapes(*carried) if n_carry else []),
        scratch_shapes=[pltpu.VMEM((2, PAIR, s), BF16), pltpu.VMEM((2, PAIR, s), BF16),
                        pltpu.VMEM((2, s, PAIR), BF16), pltpu.VMEM((2, s, PAIR), BF16),
                        pltpu.VMEM((PAIR, s), F32), pltpu.VMEM((PAIR, s), F32),
                        pltpu.VMEM((tk, PAIR), F32), pltpu.VMEM((tk, PAIR), F32), pltpu.VMEM((tk, PAIR), F32),
                        pltpu.VMEM((2, tk, PAIR), BF16), pltpu.VMEM((2, tk, PAIR), BF16),
                        pltpu.VMEM((2, PAIR, tk), BF16),
                        pltpu.VMEM((4, tk, tq), F32), pltpu.VMEM((4, tk, tq), F32)] + carry_sem,
        compiler_params=_params(("arbitrary", "arbitrary")),
    )(q, k, v, do, c, lse, dt, *carried)


def _forget_bwd(dcs, drs, lf, tm=512):
    s = lf.shape[0]
    n = s // tm

    def body(dcs_ref, drs_ref, lf_ref, dpf_ref, dfb_ref, carry_ref):
        @pl.when(pl.program_id(0) == 0)
        def _():
            carry_ref[...] = jnp.zeros_like(carry_ref)
            dfb_ref[...] = jnp.zeros_like(dfb_ref)

        lane = lax.broadcasted_iota(jnp.int32, (tm, 128), 1)
        dc = jnp.zeros((tm, 128), F32)
        for pj in range(N_PAIRS):
            rows = jnp.concatenate([drs_ref[pj], jnp.zeros((120, tm), F32)], axis=0)
            cols = jnp.transpose(rows)
            if pj:
                cols = pltpu.roll(cols, 2 * pj, 1)
            dc = dc + jnp.where((lane == 2 * pj) | (lane == 2 * pj + 1), cols, 0.0)
            blk = dcs_ref[:, pj * 128:(pj + 1) * 128]
            for h in range(2):
                head = 2 * pj + h
                moved = pltpu.roll(blk, (head - _spare(h)) % 128, 1) if head != _spare(h) else blk
                dc = dc - jnp.where(lane == head, moved, 0.0)
        dlf = _scan_rows(dc, reverse=True) + carry_ref[0:1, :]
        carry_ref[...] = dlf[0:8, :]
        dz = jnp.where(lane < 8, dlf * (1.0 - jnp.exp(lf_ref[...])), 0.0)
        dpf_ref[...] = dz.astype(BF16)
        dfb_ref[...] += jnp.sum(dz, axis=0, keepdims=True)

    return pl.pallas_call(
        body, name="forget_bwd", grid=(n,),
        in_specs=[pl.BlockSpec((tm, N_PAIRS * 128), lambda i: (n - 1 - i, 0)),
                  pl.BlockSpec((N_PAIRS, 8, tm), lambda i: (0, 0, n - 1 - i)),
                  pl.BlockSpec((tm, 128), lambda i: (n - 1 - i, 0))],
        out_specs=[pl.BlockSpec((tm, 128), lambda i: (n - 1 - i, 0)), _full_spec((1, 128))],
        out_shape=[jax.ShapeDtypeStruct((s, 128), BF16), jax.ShapeDtypeStruct((1, 128), F32)],
        scratch_shapes=[pltpu.VMEM((8, 128), F32)],
        compiler_params=_params(("arbitrary",)),
    )(dcs, drs, lf)


def _pool_bwd(pu, pg, dmp, pw, ps, tm=512):
    s = pu.shape[0]
    hb = tm // HALO
    n = s // tm
    last_halo = s // HALO - 1

    def body(pu_ref, halo_ref, pg_ref, dmp_ref, pgn_ref, dmpn_ref, pw_ref, ps_ref,
             dpu_ref, dpg_ref, gpw_ref, gps_ref):
        i = pl.program_id(0)

        @pl.when(i == 0)
        def _():
            gpw_ref[...] = jnp.zeros_like(gpw_ref)
            gps_ref[...] = jnp.zeros_like(gps_ref)

        u = pu_ref[...].astype(F32)
        halo = jnp.where(i == 0, 0.0, halo_ref[...].astype(F32))
        d = _pool_delta(u, halo, i * tm)
        e_parts, dd_parts = [], []
        for gi in range(len(POOL_WINDOWS)):
            cols = slice(gi * GROUP_DIM, (gi + 1) * GROUP_DIM)
            wg = pw_ref[gi]
            scale = ps_ref[:, cols]
            db = d[gi].astype(BF16)
            z = jnp.dot(db, wg, preferred_element_type=F32)
            gate = pg_ref[:, cols].astype(F32)
            sg = _sigmoid(gate)
            dm = dmp_ref[:, cols].astype(F32)
            dy = dm * (gate * sg)
            dpg_ref[:, cols] = (dm * (z * scale) * (sg * (1.0 + gate * (1.0 - sg)))).astype(BF16)
            gps_ref[:, cols] += jnp.sum(dy * z, axis=0, keepdims=True)
            dz = (dy * scale).astype(BF16)
            gpw_ref[gi] += lax.dot_general(db, dz, TN, preferred_element_type=F32)
            dd = lax.dot_general(dz, wg, NT, preferred_element_type=F32)
            gate_n = pgn_ref[:, cols].astype(F32)
            dz_n = (dmpn_ref[:, cols].astype(F32) * (gate_n * _sigmoid(gate_n)) * scale).astype(BF16)
            dd_n = lax.dot_general(dz_n, wg, NT, preferred_element_type=F32)
            dd_n = jnp.where(i == n - 1, 0.0, dd_n)
            dd_parts.append(dd)
            e_parts.append(jnp.concatenate([_over_count(dd, i * tm, gi), dd_n * (1.0 / POOL_WINDOWS[gi])], axis=0))
        lead = _window_sums(jnp.concatenate(e_parts, axis=1), forward=True)
        for gi in range(len(POOL_WINDOWS)):
            cols = slice(gi * GROUP_DIM, (gi + 1) * GROUP_DIM)
            dpu_ref[:, cols] = (lead[gi][0:tm, :] - dd_parts[gi]).astype(BF16)

    act = jax.ShapeDtypeStruct((s, 512), BF16)
    prev_halo = pl.BlockSpec((HALO, 512), lambda i: (jnp.maximum(i * hb - 1, 0), 0))
    next_halo = pl.BlockSpec((HALO, 512), lambda i: (jnp.minimum((i + 1) * hb, last_halo), 0))
    return pl.pallas_call(
        body, name="pool_bwd", grid=(n,),
        in_specs=[_row_spec(tm, 512), prev_halo, _row_spec(tm, 512), _row_spec(tm, 512), next_halo, next_halo,
                  _full_spec((4, GROUP_DIM, GROUP_DIM)), _full_spec((1, 512))],
        out_specs=[_row_spec(tm, 512), _row_spec(tm, 512), _full_spec((4, GROUP_DIM, GROUP_DIM)),
                   _full_spec((1, 512))],
        out_shape=[act, act, jax.ShapeDtypeStruct((4, GROUP_DIM, GROUP_DIM), F32),
                   jax.ShapeDtypeStruct((1, 512), F32)],
        compiler_params=_params(("arbitrary",)),
    )(pu, pu, pg, dmp, pg, dmp, pw, ps)


def _inproj_bwd_x(dsec, wt, x, g, dxo, exchange=None, scatter=None, tm=512):
    s = x.shape[0]
    n_steps = s // tm
    carried = list(exchange or scatter or ())
    n_carry = len(carried)

    def body(*refs):
        d_refs = refs[0:7]
        wt_ref, x_ref, g_ref, dxo_ref = refs[7:11]
        carry_src = refs[11:11 + n_carry]
        dxi_ref, dg_ref = refs[11 + n_carry:13 + n_carry]
        carry_dst = refs[13 + n_carry:13 + 2 * n_carry]
        sems = refs[13 + 2 * n_carry:]
        step = pl.program_id(0)

        def comm(phase):
            if exchange is not None:
                (_exchange_start if phase == 0 else _exchange_finish)(carry_src, carry_dst, None, sems)
            else:
                big = tuple(zip(carry_src[:2], carry_dst[:2]))
                (_scatter_start if phase == 0 else _scatter_finish)(big, (carry_src[2], carry_dst[2]), sems)

        @pl.when(step == 0)
        def _():
            dg_ref[...] = jnp.zeros_like(dg_ref)
            if n_carry:
                comm(0)

        dh = None
        for d_ref, (off, ncol) in zip(d_refs, SECTIONS):
            t = jnp.dot(d_ref[...], wt_ref[off:off + ncol, :], preferred_element_type=F32)
            dh = t if dh is None else dh + t
        xf = x_ref[...]
        r = lax.rsqrt(jnp.mean(xf * xf, axis=-1, keepdims=True) + RMS_EPS)
        xh = xf * r
        u = dh * g_ref[...]
        dxi_ref[...] = dxo_ref[...] + r * (u - xh * jnp.mean(xh * u, axis=-1, keepdims=True))
        dg_ref[...] += jnp.sum(dh * xh, axis=0, keepdims=True)

        if n_carry:
            @pl.when(step == n_steps - 1)
            def _():
                comm(1)

    if exchange is not None:
        name, carry_shapes, n_sems = "inproj_bwd_x_exchange", _exchange_shapes(*exchange), 3
    elif scatter is not None:
        name, n_sems = "inproj_bwd_x_scatter", SCATTER_SEMS
        carry_shapes = _scatter_shapes(*scatter[:2]) + [jax.ShapeDtypeStruct((3, SMALL_HALF, 128), F32)]
    else:
        name, carry_shapes, n_sems = "inproj_bwd_x", [], 0
    return pl.pallas_call(
        body, name=name, grid=(n_steps,),
        in_specs=[_row_spec(tm, ncol) for _, ncol in SECTIONS]
                 + [_full_spec((IN_COLS_PAD, D_MODEL)), _row_spec(tm, D_MODEL), _full_spec((1, D_MODEL)),
                    _row_spec(tm, D_MODEL)] + [ANY] * n_carry,
        out_specs=[_row_spec(tm, D_MODEL), _full_spec((1, D_MODEL))] + [ANY] * n_carry,
        out_shape=[jax.ShapeDtypeStruct((s, D_MODEL), F32), jax.ShapeDtypeStruct((1, D_MODEL), F32)] + carry_shapes,
        scratch_shapes=[pltpu.SemaphoreType.DMA((n_sems,))] * 2 if n_carry else [],
        compiler_params=_params(("arbitrary",)),
    )(*dsec, wt, x, g, dxo, *carried)


def _inproj_bwd_w(h, dsec, tm=512):
    s = h.shape[0]
    n_steps = s // tm

    def body(*refs):
        h_ref = refs[0]
        d_refs = refs[1:8]
        out_ref, gw_ref = refs[8:]
        step = pl.program_id(0)

        @pl.when(step == 0)
        def _():
            gw_ref[...] = jnp.zeros_like(gw_ref)

        hh = h_ref[...]
        for d_ref, (off, ncol) in zip(d_refs, SECTIONS):
            gw_ref[:, off:off + ncol] += lax.dot_general(hh, d_ref[...], TN, preferred_element_type=F32)

        @pl.when(step == n_steps - 1)
        def _():
            for k in range(N_CHIPS):
                out_ref[k] = gw_ref[:, k * SHARD_COLS:(k + 1) * SHARD_COLS]

    return pl.pallas_call(
        body, name="inproj_bwd_w", grid=(n_steps,),
        in_specs=[_row_spec(tm, D_MODEL)] + [_row_spec(tm, ncol) for _, ncol in SECTIONS],
        out_specs=_full_spec((N_CHIPS, D_MODEL, SHARD_COLS)),
        out_shape=jax.ShapeDtypeStruct((N_CHIPS, D_MODEL, SHARD_COLS), F32),
        scratch_shapes=[pltpu.VMEM((D_MODEL, IN_COLS_PAD), F32)],
        compiler_params=_params(("arbitrary",), vmem_mb=60),
    )(h, *dsec)


def _elementwise(fn, name, n_out, arrays, tm):
    shape = arrays[0].shape
    rest = shape[1:]
    n_in = len(arrays)

    def body(*refs):
        outs = fn(*[r[...] for r in refs[:n_in]])
        for r, val in zip(refs[n_in:], outs):
            r[...] = val

    spec = pl.BlockSpec((tm,) + rest, lambda i: (i,) + (0,) * len(rest))
    return pl.pallas_call(
        body, name=name, grid=(shape[0] // tm,),
        in_specs=[spec] * n_in, out_specs=[spec] * n_out, out_shape=[jax.ShapeDtypeStruct(shape, F32)] * n_out,
        compiler_params=_params(("parallel",)),
    )(*arrays)


def _add2(a, b, name, tm):
    return _elementwise(lambda p, q: (p + q,), name, 1, [a, b], tm)[0]


def _chip_sum(place, g, r, name, tm):
    n_layers, _, rows, cols = g.shape
    half = rows // 2
    nb = half // tm

    def body(place_ref, g_ref, r_ref, o_ref):
        o_ref[...] = (g_ref[...] + r_ref[...]).astype(BF16)

    blk = (None, None, tm, cols)
    return pl.pallas_call(
        body, name=name,
        grid_spec=pltpu.PrefetchScalarGridSpec(
            num_scalar_prefetch=1, grid=(n_layers, N_CHIPS, nb),
            in_specs=[pl.BlockSpec(blk, lambda l, k, i, p: (l, k, p[1] * nb + i, 0)),
                      pl.BlockSpec(blk, lambda l, k, i, p: (l, k, i, 0))],
            out_specs=pl.BlockSpec(blk, lambda l, k, i, p: (l, k, i, 0))),
        out_shape=jax.ShapeDtypeStruct((n_layers, N_CHIPS, half, cols), BF16),
        compiler_params=_params(("parallel", "parallel", "parallel")),
    )(place, g, r)


def _mesh_sum(place, g, r, got, into, layer, name, tm):
    _, _, rows, cols = g.shape
    half = rows // 2
    nb = half // tm

    def body(place_ref, g_ref, r_ref, got_ref, into_ref, o_ref):
        own = g_ref[...] + r_ref[...]
        o_ref[...] = (own + got_ref[0].astype(F32)) + (got_ref[1].astype(F32) + got_ref[2].astype(F32))

    blk = (None, None, tm, cols)
    return pl.pallas_call(
        body, name=name,
        grid_spec=pltpu.PrefetchScalarGridSpec(
            num_scalar_prefetch=1, grid=(nb,),
            in_specs=[pl.BlockSpec(blk, lambda i, p: (0, p[0], p[1] * nb + i, 0)),
                      pl.BlockSpec(blk, lambda i, p: (0, p[0], i, 0)),
                      pl.BlockSpec((None, 3, tm, cols), lambda i, p: (0, 0, i, 0)),
                      ANY],
            out_specs=pl.BlockSpec((None, tm, cols), lambda i, p: (layer, p[1] * nb + i, 0))),
        out_shape=jax.ShapeDtypeStruct(into.shape, F32),
        input_output_aliases={4: 0},
        compiler_params=_params(("parallel",)),
    )(place, g, r, got, into)


def _mesh_sum_small(place, ps, got, tm):
    nb = SMALL_HALF // tm

    def body(place_ref, ps_ref, got_ref, o_ref):
        o_ref[...] = (ps_ref[...] + got_ref[0]) + (got_ref[1] + got_ref[2])

    return pl.pallas_call(
        body, name="mesh_sum_small",
        grid_spec=pltpu.PrefetchScalarGridSpec(
            num_scalar_prefetch=1, grid=(nb,),
            in_specs=[pl.BlockSpec((tm, 128), lambda i, p: (p[1] * nb + i, 0)),
                      pl.BlockSpec((3, tm, 128), lambda i, p: (0, i, 0))],
            out_specs=pl.BlockSpec((tm, 128), lambda i, p: (p[1] * nb + i, 0))),
        out_shape=jax.ShapeDtypeStruct((SMALL_ROWS, 128), F32),
        compiler_params=_params(("parallel",)),
    )(place, ps, got)


def _adamw(g, w, m, v, name, tm):
    def fn(g, w, m, v):
        m = ADAM_B1 * m + (1.0 - ADAM_B1) * g
        v = ADAM_B2 * v + (1.0 - ADAM_B2) * (g * g)
        m_hat = m / (1.0 - ADAM_B1 ** ADAM_STEP)
        v_hat = v / (1.0 - ADAM_B2 ** ADAM_STEP)
        delta = -ADAM_LR * (m_hat / (jnp.sqrt(v_hat) + ADAM_EPS) + ADAM_WD * w)
        return g, delta, m, v

    return _elementwise(fn, name, 4, [g, w, m, v], tm)


def _position():
    x, y, c = lax.axis_index("x"), lax.axis_index("y"), lax.axis_index("c")
    other_chips = [(x, 1 - y), (1 - x, y), (1 - x, 1 - y)]
    return x, y, c, other_chips


def _remote(src, dst, sems, k, to):
    send_sems, recv_sems = sems
    return pltpu.make_async_remote_copy(src_ref=src, dst_ref=dst, send_sem=send_sems.at[k],
                                        recv_sem=recv_sems.at[k], device_id=to, device_id_type=MESH)


def _comm_call(body, name, n_in, out_shape, n_remote):
    return pl.pallas_call(
        body, name=name, in_specs=[ANY] * n_in, out_specs=[ANY] * len(out_shape), out_shape=out_shape,
        scratch_shapes=[pltpu.SemaphoreType.DMA((n_remote,)), pltpu.SemaphoreType.DMA((n_remote,))],
    )


def _halves(ref_rows, c):
    half = ref_rows // 2
    return pl.ds(half * c, half), pl.ds(half * (1 - c), half)


def _gather_weights(wi):
    def body(wi_ref, gi_ref, send_sems, recv_sems):
        _gather_start((wi_ref,), (gi_ref,), (send_sems, recv_sems))
        _gather_finish((wi_ref,), (gi_ref,), (send_sems, recv_sems))

    return _comm_call(body, "gather_weights", 1, _gather_shapes(wi), GATHER_SEMS)(wi)[0]


GATHER_SEMS = 6
SCATTER_SEMS = 9


def _gather_shapes(*arrays):
    return [jax.ShapeDtypeStruct((a.shape[0], N_CHIPS) + a.shape[1:], a.dtype) for a in arrays]


def _gather_sends(srcs, dsts, sems):
    x, y, c, chips = _position()
    out = []
    for k, (cx, cy) in enumerate(chips):
        for a, (src, dst) in enumerate(zip(srcs, dsts)):
            mine, _ = _halves(src.shape[1], c)
            out.append(_remote(src.at[:, mine], dst.at[:, k + 1, mine], sems, len(srcs) * k + a, (cx, cy, c)))
    return out


def _gather_own(srcs, dsts, local_sems):
    return [pltpu.make_async_copy(src, dst.at[:, 0], local_sems.at[a]) for a, (src, dst) in enumerate(zip(srcs, dsts))]


def _gather_start(srcs, dsts, sems):
    for cp in _gather_sends(srcs, dsts, sems):
        cp.start()


def _gather_finish(srcs, dsts, sems):
    x, y, c, _ = _position()
    n = len(srcs)
    forwards = []
    for k in range(3):
        for a, dst in enumerate(dsts):
            mine, _ = _halves(dst.shape[2], c)
            blk = dst.at[:, k + 1, mine]
            _remote(blk, blk, sems, n * k + a, (x, y, c)).wait_recv()
            forwards.append(_remote(blk, blk, sems, 3 * n + n * k + a, (x, y, 1 - c)))
            forwards[-1].start()
    for k in range(3):
        for a, dst in enumerate(dsts):
            _, other = _halves(dst.shape[2], c)
            blk = dst.at[:, k + 1, other]
            _remote(blk, blk, sems, 3 * n + n * k + a, (x, y, c)).wait_recv()
    for cp in _gather_sends(srcs, dsts, sems) + forwards:
        cp.wait_send()


def _scatter_sends(big, small, sems):
    x, y, c, chips = _position()
    n = len(big) + (small is not None)
    out = []
    for k, (cx, cy) in enumerate(chips):
        ck = 2 * cx + cy
        for a, (src, dst) in enumerate(big):
            out.append(_remote(src.at[:, ck], dst.at[:, k], sems, n * k + a, (cx, cy, c)))
        if small is not None:
            half, _ = _halves(SMALL_ROWS, c)
            out.append(_remote(small[0].at[half], small[1].at[k], sems, n * k + n - 1, (cx, cy, c)))
    return out


def _scatter_start(big, small, sems):
    for cp in _scatter_sends(big, small, sems):
        cp.start()


def _scatter_finish(big, small, sems):
    x, y, c, _ = _position()
    n = len(big) + (small is not None)
    for k in range(3):
        landing = [dst.at[:, k] for _, dst in big] + ([small[1].at[k]] if small is not None else [])
        for a, blk in enumerate(landing):
            _remote(blk, blk, sems, n * k + a, (x, y, c)).wait_recv()
    for cp in _scatter_sends(big, small, sems):
        cp.wait_send()


def _exchange_shapes(gi, go):
    return [jax.ShapeDtypeStruct(a.shape[:2] + (a.shape[2] // 2, a.shape[3]), F32) for a in (gi, go)]


def _exchange_copies(srcs, dsts, small, sems):
    x, y, c, _ = _position()
    cps = [_remote(src.at[:, :, _halves(src.shape[2], c)[1]], dst, sems, a, (x, y, 1 - c))
           for a, (src, dst) in enumerate(zip(srcs, dsts))]
    if small is not None:
        cps.append(_remote(small[0], small[1], sems, len(srcs), (x, y, 1 - c)))
    return cps


def _exchange_start(srcs, dsts, small, sems):
    for cp in _exchange_copies(srcs, dsts, small, sems):
        cp.start()


def _exchange_finish(srcs, dsts, small, sems):
    for cp in _exchange_copies(srcs, dsts, small, sems):
        cp.wait()


def _exchange_sibling(gi, go, sm):
    def body(gi_ref, go_ref, sm_ref, ri_ref, ro_ref, rs_ref, send_sems, recv_sems):
        args = ((gi_ref, go_ref), (ri_ref, ro_ref), (sm_ref, rs_ref), (send_sems, recv_sems))
        _exchange_start(*args)
        _exchange_finish(*args)

    out_shape = _exchange_shapes(gi, go) + [jax.ShapeDtypeStruct(sm.shape, F32)]
    return _comm_call(body, "exchange_sibling", 3, out_shape, 3)(gi, go, sm)


def _scatter_shapes(pi, po):
    return [jax.ShapeDtypeStruct((a.shape[0], 3) + a.shape[2:], a.dtype) for a in (pi, po)]


def _join_sibling(fi, fo, fs, late):
    def body(fi_ref, fo_ref, fs_ref, late_ref, oi_ref, oo_ref, os_ref, sum_ref, send_sems, recv_sems,
             sib_ref, chip_ref, got_ref):
        x, y, c, chips = _position()
        sems = (send_sems, recv_sems)
        sent, got = [], []
        for a, (src, dst) in enumerate(((fi_ref, oi_ref), (fo_ref, oo_ref))):
            mine, other = _halves(src.shape[1], c)
            sent.append(_remote(src.at[:, mine], dst.at[:, mine], sems, a, (x, y, 1 - c)))
            got.append(_remote(dst.at[:, other], dst.at[:, other], sems, a, (x, y, c)))
        mine, other = _halves(SMALL_ROWS, c)
        sent.append(_remote(fs_ref.at[mine], os_ref.at[mine], sems, 2, (x, y, 1 - c)))
        got.append(_remote(os_ref.at[other], os_ref.at[other], sems, 2, (x, y, c)))
        for cp in sent:
            cp.start()

        to_sibling = _remote(late_ref, sib_ref, sems, 3, (x, y, 1 - c))
        to_sibling.start()
        to_sibling.wait()
        chip_ref[...] = late_ref[...] + sib_ref[...]
        to_chips = [_remote(chip_ref, got_ref.at[k], sems, 4 + k, (cx, cy, c)) for k, (cx, cy) in enumerate(chips)]
        for cp in to_chips:
            cp.start()
        for cp in to_chips:
            cp.wait()
        sum_ref[...] = (chip_ref[...] + got_ref[0]) + (got_ref[1] + got_ref[2])

        for cp in got:
            cp.wait_recv()
        for cp in sent:
            cp.wait_send()

    vmem = pl.BlockSpec(memory_space=pltpu.VMEM)
    return pl.pallas_call(
        body, name="join_sibling", in_specs=[ANY] * 3 + [vmem], out_specs=[ANY] * 3 + [vmem],
        out_shape=[jax.ShapeDtypeStruct(a.shape, F32) for a in (fi, fo, fs, late)],
        scratch_shapes=[pltpu.SemaphoreType.DMA((7,)), pltpu.SemaphoreType.DMA((7,)),
                        pltpu.VMEM(late.shape, F32), pltpu.VMEM(late.shape, F32), pltpu.VMEM((3,) + late.shape, F32)],
        input_output_aliases={0: 0, 1: 1, 2: 2},
    )(fi, fo, fs, late)


SMALL_SIZES = (DEPTH * D_MODEL, D_MODEL, DEPTH * POOL_WIDTH, 128, DEPTH * 4 * GROUP_DIM * GROUP_DIM)
LOSS_SLOT = DEPTH * D_MODEL + D_MODEL + DEPTH * POOL_WIDTH + DEPTH * 8


def _pack_small(norm_g, final_g, pool_scale, forget_bias, pool_w, loss=None):
    fb = forget_bias.reshape(-1)
    if loss is not None:
        fb = jnp.concatenate([fb, loss.reshape(1)])
    fb = jnp.pad(fb, (0, 128 - fb.size))
    flat = jnp.concatenate([norm_g.reshape(-1), final_g.reshape(-1), pool_scale.reshape(-1), fb,
                            pool_w.reshape(-1)])
    return jnp.pad(flat, (0, SMALL_ROWS * 128 - flat.size)).reshape(SMALL_ROWS, 128)


def _unpack_small(packed):
    flat = packed.reshape(-1)
    offs = [0]
    for n in SMALL_SIZES:
        offs.append(offs[-1] + n)
    norm_g, final_g, pool_scale, fb, pool_w = [flat[offs[i]:offs[i + 1]] for i in range(5)]
    return (norm_g.reshape(DEPTH, D_MODEL), final_g, pool_scale.reshape(DEPTH, POOL_WIDTH),
            fb[:DEPTH * 8].reshape(DEPTH, 8), pool_w.reshape(DEPTH, 4, GROUP_DIM, GROUP_DIM))


def kernel(x, norm_g, w_in, forget_bias, pool_w, pool_scale, w_out, final_g, loss_target, m_norm_g, m_w_in, m_forget_bias, m_pool_w, m_pool_scale, m_w_out, m_final_g, v_norm_g, v_w_in, v_forget_bias, v_pool_w, v_pool_scale, v_w_out, v_final_g):
    x0 = x[0]
    tgt = loss_target[0]
    s = x0.shape[0]

    me = 2 * lax.axis_index("x") + lax.axis_index("y")
    place = jnp.stack([me, lax.axis_index("c")]).astype(jnp.int32)
    wi_b, wo_b = w_in.astype(BF16), w_out.astype(BF16)

    def whole_w_out(got):
        return jnp.concatenate([lax.dynamic_index_in_dim(got, jnp.bitwise_xor(k, me), axis=1, keepdims=False)[0]
                                for k in range(N_CHIPS)], axis=0)

    pool_w_b = pool_w.astype(BF16)
    fb_pad = jnp.pad(forget_bias, ((0, 0), (0, 128 - forget_bias.shape[1])))
    head_sel = (jnp.arange(ATTN_WIDTH)[None, :] // HEAD_DIM == jnp.arange(8)[:, None]).astype(BF16)

    saved, weights = [], []
    xl = x0
    got_i = _gather_weights(wi_b[0:1])
    for l in range(DEPTH):
        pu, pg, q, k, v, ag, lf, c, h, po, w_t = _inproj_fwd(
            place, xl, norm_g[l][None], None if l > 0 else wi_b[0], got_i, fb_pad[l][None],
            pool_w_b[l], pool_scale[l][None])
        if l + 1 < DEPTH:
            ma, o, lse, got_o, got_i = _attn_fwd(q, k, v, c, ag, carry=(wo_b[l:l + 1], wi_b[l + 1:l + 2]))
        else:
            ma, o, lse, got_o = _attn_fwd(q, k, v, c, ag, carry=(wo_b[l:l + 1],))
        w_o = whole_w_out(got_o)
        weights.append((w_t, w_o))
        saved.append((xl, h, pu, pg, q, k, v, ag, lf, c, po, ma, o, lse))
        if l + 1 < DEPTH:
            xl = _outproj_fwd(xl, po, ma, w_o)
    dx, g_final, loss_part = _loss_head(xl, po, ma, w_o, tgt, final_g[None])

    g_norm, g_fb, g_pw, g_ps = [], [], [], []
    parts = {}
    pending = None
    for l in reversed(range(DEPTH)):
        xl, h, pu, pg, q, k, v, ag, lf, c, po, ma, o, lse = saved[l]
        w_t, w_o = weights[l]
        dmp, do, dag, dt, gwo = _outproj_bwd(dx, po, ma, ag, o, w_o, head_sel)
        if pending is None:
            dq, dk, dv, dcs, drs = _attn_bwd(q, k, v, do, c, lse, dt.reshape(N_PAIRS, 2, s))
        else:
            dq, dk, dv, dcs, drs, qi, qo = _attn_bwd(q, k, v, do, c, lse, dt.reshape(N_PAIRS, 2, s), carry=pending)
            parts[l + 1] += (qi, qo)
        dpf, dfb = _forget_bwd(dcs, drs, lf)
        dpu, dpg, gpw, gps = _pool_bwd(pu, pg, dmp, pool_w_b[l], pool_scale[l][None])
        dsec = (dpu, dpg, dq, dk, dv, dag, dpf)
        gwi = _inproj_bwd_w(h, dsec)
        g_fb.append(dfb[0, :8])
        g_pw.append(gpw)
        g_ps.append(gps[0])
        send_i = gwi[None]
        send_o = gwo.reshape(1, N_CHIPS, SHARD_ROWS, D_MODEL)
        if l > 0:
            dx, gn, ri, ro = _inproj_bwd_x(dsec, w_t, xl, norm_g[l][None], dx, exchange=(send_i, send_o))
            g_norm.append(gn[0])
        else:
            g_norm.append(jnp.zeros((D_MODEL,), F32))
            g_norm, g_fb, g_pw, g_ps = [t[::-1] for t in (g_norm, g_fb, g_pw, g_ps)]
            send_s = _pack_small(jnp.stack(g_norm), g_final[0], jnp.stack(g_ps), jnp.stack(g_fb), jnp.stack(g_pw),
                                 loss=loss_part[0, 0])
            ri, ro, rs = _exchange_sibling(send_i, send_o, send_s)
        pending = (_chip_sum(place, send_i, ri, "chip_sum_w_in", 256),
                   _chip_sum(place, send_o, ro, "chip_sum_w_out", SHARD_ROWS // 2))
        parts[l] = (send_i, ri, send_o, ro)
    ps_ = _add2(send_s, rs, "chip_sum_small", SMALL_ROWS // 4)
    dx, gn0, qi, qo, qs = _inproj_bwd_x(dsec, w_t, xl, norm_g[0][None], dx, scatter=(*pending, ps_))
    parts[0] += (qi, qo)
    fi = lax.empty((DEPTH, D_MODEL, SHARD_COLS), F32)
    fo = lax.empty((DEPTH, SHARD_ROWS, D_MODEL), F32)
    for l in range(DEPTH):
        send_i, ri, send_o, ro, qi, qo = parts[l]
        fi = _mesh_sum(place, send_i, ri, qi, fi, l, "mesh_sum_w_in", 256)
        fo = _mesh_sum(place, send_o, ro, qo, fo, l, "mesh_sum_w_out", SHARD_ROWS // 2)
    hs = _mesh_sum_small(place, ps_, qs, SMALL_HALF // 4)
    fi, fo, fs, gn0 = _join_sibling(fi, fo, hs, gn0.reshape(8, 128))
    fs = jnp.concatenate([fs[0:8] + gn0, fs[8:]], axis=0)

    col_major = lambda a: jnp.transpose(a, (2, 0, 1))
    flat_o = lambda a: a.reshape(-1, D_MODEL)
    out_i = _adamw(col_major(fi), col_major(w_in), col_major(m_w_in), col_major(v_w_in), "adamw_w_in", 77)
    out_i = [jnp.transpose(t, (1, 2, 0)) for t in out_i]
    out_o = _adamw(flat_o(fo), flat_o(w_out), flat_o(m_w_out), flat_o(v_w_out), "adamw_w_out", 256)
    out_s = _adamw(fs, _pack_small(norm_g, final_g, pool_scale, forget_bias, pool_w),
                   _pack_small(m_norm_g, m_final_g, m_pool_scale, m_forget_bias, m_pool_w),
                   _pack_small(v_norm_g, v_final_g, v_pool_scale, v_forget_bias, v_pool_w),
                   "adamw_small", SMALL_ROWS // 4)

    loss = out_s[0].reshape(-1)[LOSS_SLOT]
    groups = []
    for oi, oo, os_ in zip(out_i, out_o, out_s):
        sn, sf, sp, sb, sw = _unpack_small(os_)
        groups.append((sn, oi.reshape(w_in.shape), sb, sw, sp, oo.reshape(w_out.shape), sf))
    return (loss, dx[None]) + tuple(t for grp in groups for t in grp)
```

```python
import jax
import jax.numpy as jnp
from jax import lax
from jax.experimental import pallas as pl
from jax.experimental.pallas import tpu as pltpu

F32 = jnp.float32
BF16 = jnp.bfloat16

D_MODEL = 1024
DEPTH = 4
POOL_WIDTH = 512
ATTN_WIDTH = 512
HEAD_DIM = 64
PAIR = 2 * HEAD_DIM
N_PAIRS = ATTN_WIDTH // PAIR
POOL_WINDOWS = (2, 4, 8, 16)
GROUP_DIM = 128
HALO = 16
IN_COLS = 3080
OFF_F = 3072
IN_COLS_PAD = 3200
SECTIONS = ((0, 512), (512, 512), (1024, 512), (1536, 512), (2048, 512), (2560, 512), (OFF_F, 128))
N_CHIPS = 4
SHARD_COLS = IN_COLS // N_CHIPS
SHARD_ROWS = D_MODEL // N_CHIPS
RMS_EPS = 1e-6
NEG = -1e30
Q_SCALE = 0.125

ADAM_LR = 0.001
ADAM_B1 = 0.9
ADAM_B2 = 0.999
ADAM_EPS = 1e-08
ADAM_WD = 0.01
ADAM_STEP = 10

SMALL_ROWS = 2112
SMALL_HALF = SMALL_ROWS // 2

NT = (((1,), (1,)), ((), ()))
TN = (((0,), (0,)), ((), ()))
MESH = pl.DeviceIdType.MESH
ANY = pl.BlockSpec(memory_space=pl.ANY)


def _params(semantics, vmem_mb=48):
    return pltpu.CompilerParams(dimension_semantics=semantics, vmem_limit_bytes=vmem_mb << 20)


def _row_spec(tm, cols):
    return pl.BlockSpec((tm, cols), lambda i: (i, 0))


def _full_spec(shape):
    return pl.BlockSpec(shape, lambda *_: (0,) * len(shape))


def _sigmoid(x):
    return jax.nn.sigmoid(x)


def _scan_rows(a, reverse=False):
    n = a.shape[0]
    row = lax.broadcasted_iota(jnp.int32, a.shape, 0)
    k = 1
    while k < n:
        if reverse:
            a = a + jnp.where(row < n - k, pltpu.roll(a, n - k, 0), 0.0)
        else:
            a = a + jnp.where(row >= k, pltpu.roll(a, k, 0), 0.0)
        k *= 2
    return a


def _inproj_fwd(place, x, g, own, got, fb, pw, ps, tm=512):
    s = x.shape[0]

    def body(place_ref, x_ref, g_ref, fb_ref, pw_ref, ps_ref, *refs):
        own_ref = refs[0] if own is not None else None
        (got_ref, pu_ref, pg_ref, q_ref, k_ref, v_ref, ag_ref, lf_ref, c_ref, h_ref, po_ref, wt_ref,
         carry_ref, halo_ref, w_ref, piece_ref, sem) = refs[own is not None:]
        step = pl.program_id(0)

        @pl.when(step == 0)
        def _():
            carry_ref[...] = jnp.zeros_like(carry_ref)
            halo_ref[...] = jnp.zeros_like(halo_ref)
            for chip in range(N_CHIPS):
                rel = jnp.bitwise_xor(chip, place_ref[0])

                def fetch(src):
                    cp = pltpu.make_async_copy(src, piece_ref, sem)
                    cp.start()
                    cp.wait()

                if own_ref is None:
                    fetch(got_ref.at[0, rel])
                else:
                    pl.when(rel == 0)(lambda: fetch(own_ref))
                    pl.when(rel != 0)(lambda: fetch(got_ref.at[0, rel]))
                w_ref[:, chip * SHARD_COLS:(chip + 1) * SHARD_COLS] = piece_ref[...]
            w_ref[:, IN_COLS:] = jnp.zeros((D_MODEL, IN_COLS_PAD - IN_COLS), BF16)
            for t in range(IN_COLS_PAD // 128):
                cols = slice(t * 128, (t + 1) * 128)
                wt_ref[cols, :] = jnp.transpose(w_ref[:, cols].astype(F32)).astype(BF16)

        xf = x_ref[...]
        r = lax.rsqrt(jnp.mean(xf * xf, axis=-1, keepdims=True) + RMS_EPS)
        h = (xf * r * g_ref[...]).astype(BF16)
        h_ref[...] = h

        def proj(sec):
            off, n = SECTIONS[sec]
            return jnp.dot(h, w_ref[:, off:off + n], preferred_element_type=F32)

        pu_ref[...] = proj(0).astype(BF16)
        pg_ref[...] = proj(1).astype(BF16)
        u = pu_ref[...].astype(F32)
        d = _pool_delta(u, halo_ref[...], step * tm)
        halo_ref[...] = u[tm - HALO:, :]
        for gi in range(len(POOL_WINDOWS)):
            cols = slice(gi * GROUP_DIM, (gi + 1) * GROUP_DIM)
            zg = jnp.dot(d[gi].astype(BF16), pw_ref[gi], preferred_element_type=F32)
            gate = pg_ref[:, cols].astype(F32)
            po_ref[:, cols] = (zg * ps_ref[:, cols] * (gate * _sigmoid(gate))).astype(BF16)
        q_ref[...] = (proj(2) * Q_SCALE).astype(BF16)
        k_ref[...] = proj(3).astype(BF16)
        v_ref[...] = proj(4).astype(BF16)
        ag_ref[...] = proj(5).astype(BF16)
        z = proj(6) + fb_ref[...]
        lf = jnp.minimum(z, 0.0) - jnp.log(1.0 + jnp.exp(-jnp.abs(z)))
        lf_ref[...] = lf
        c_ref[...] = _scan_rows(lf) + carry_ref[0:1, :]
        carry_ref[0:1, :] = c_ref[tm - 1:tm, :]

    act = jax.ShapeDtypeStruct((s, 512), BF16)
    rows = lambda cols: pl.BlockSpec((tm, cols), lambda i, p: (i, 0))
    whole = lambda shape: pl.BlockSpec(shape, lambda i, p: (0,) * len(shape))
    shards = ([] if own is None else [own]) + [got]
    return pl.pallas_call(
        body, name="inproj_fwd" if own is None else "inproj_fwd_first",
        grid_spec=pltpu.PrefetchScalarGridSpec(
            num_scalar_prefetch=1, grid=(s // tm,),
            in_specs=[rows(D_MODEL), whole((1, D_MODEL)), whole((1, 128)), whole((4, GROUP_DIM, GROUP_DIM)),
                      whole((1, 512))] + [ANY] * len(shards),
            out_specs=[rows(512)] * 6 + [rows(128), rows(128), rows(D_MODEL), rows(512),
                                         whole((IN_COLS_PAD, D_MODEL))],
            scratch_shapes=[pltpu.VMEM((8, 128), F32), pltpu.VMEM((HALO, 512), F32),
                            pltpu.VMEM((D_MODEL, IN_COLS_PAD), BF16),
                            pltpu.VMEM((D_MODEL, SHARD_COLS), BF16), pltpu.SemaphoreType.DMA]),
        out_shape=[act] * 6 + [jax.ShapeDtypeStruct((s, 128), F32), jax.ShapeDtypeStruct((s, 128), F32),
                               jax.ShapeDtypeStruct((s, D_MODEL), BF16), act,
                               jax.ShapeDtypeStruct((IN_COLS_PAD, D_MODEL), BF16)],
        compiler_params=_params(("arbitrary",), vmem_mb=56),
    )(place, x, g, fb, pw, ps, *shards)


def _window_sums(ext, forward):
    n = ext.shape[0]
    outs = []
    for gi, w in enumerate(POOL_WINDOWS):
        a = ext[:, gi * GROUP_DIM:(gi + 1) * GROUP_DIM]
        k = 1
        while k < w:
            a = a + pltpu.roll(a, (n - k) if forward else k, 0)
            k *= 2
        outs.append(a)
    return outs


def _over_count(a, row0, gi):
    w = POOL_WINDOWS[gi]
    t = row0 + lax.broadcasted_iota(jnp.int32, (HALO, GROUP_DIM), 0)
    head = a[0:HALO, :] / jnp.minimum(t + 1, w).astype(F32)
    return jnp.concatenate([head, a[HALO:, :] * (1.0 / w)], axis=0)


def _pool_delta(u, halo, row0):
    sums = _window_sums(jnp.concatenate([halo, u], axis=0), forward=False)
    return [_over_count(sums[gi][HALO:, :], row0, gi) - u[:, gi * GROUP_DIM:(gi + 1) * GROUP_DIM]
            for gi in range(len(POOL_WINDOWS))]


def _split3(x):
    hi = x.astype(BF16).astype(F32)
    r1 = x - hi
    lo = r1.astype(BF16).astype(F32)
    return hi, lo, (r1 - lo).astype(BF16).astype(F32)


def _own(idx, h):
    return (idx < HEAD_DIM) if h == 0 else (idx >= HEAD_DIM)


def _spare(h):
    return HEAD_DIM if h == 0 else 0


def _attn_fwd(q, k, v, c, ag, carry=None, tq=512, tk=512, chunk=512):
    s = q.shape[0]
    n_carry = 0 if carry is None else len(carry)

    def body(*refs):
        q_ref, k_ref, v_ref, c_ref, ag_ref = refs[:5]
        carry_src = refs[5:5 + n_carry]
        ma_ref, o_ref, lse_ref = refs[5 + n_carry:8 + n_carry]
        carry_dst = refs[8 + n_carry:8 + 2 * n_carry]
        kx_ref, vt_ref, acc0_ref, acc1_ref, sta_ref, stb_ref = refs[8 + 2 * n_carry:14 + 2 * n_carry]
        carry_sems = refs[14 + 2 * n_carry:]
        j = pl.program_id(0)
        i = pl.program_id(1)

        if n_carry:
            @pl.when((j == 0) & (i == 0))
            def _():
                _gather_start(carry_src, carry_dst, carry_sems[:2])
                for cp in _gather_own(carry_src, carry_dst, carry_sems[2]):
                    cp.start()

        @pl.when(i == 0)
        def _():
            def prep(ch, carry):
                rows = pl.ds(pl.multiple_of(ch * chunk, chunk), chunk)
                kk = k_ref[rows, :].astype(F32)
                cc = c_ref[rows, :]
                vt = jnp.transpose(v_ref[rows, :].astype(F32))
                lane = lax.broadcasted_iota(jnp.int32, (chunk, PAIR), 1)
                sub = lax.broadcasted_iota(jnp.int32, (PAIR, chunk), 0)
                for h in range(2):
                    ccol = jnp.sum(jnp.where(lane == 2 * j + h, cc, 0.0), axis=1, keepdims=True)
                    kx = jnp.where(_own(lane, h), kk, 0.0)
                    for t, part in enumerate(_split3(-ccol)):
                        kx = jnp.where(lane == _spare(h) + t, part, kx)
                    kx_ref[h, rows, :] = kx.astype(BF16)
                    vt_ref[h, :, rows] = jnp.where(_own(sub, h), vt,
                                                   jnp.where(sub == _spare(h), 1.0, 0.0)).astype(BF16)
                return carry

            lax.fori_loop(0, s // chunk, prep, 0)

        qq = q_ref[...].astype(F32)
        lane_q = lax.broadcasted_iota(jnp.int32, (tq, PAIR), 1)
        qx = []
        for h in range(2):
            ones = (lane_q >= _spare(h)) & (lane_q < _spare(h) + 3)
            qx.append(jnp.transpose(jnp.where(_own(lane_q, h), qq, jnp.where(ones, 1.0, 0.0))).astype(BF16))
        accs = (acc0_ref, acc1_ref)
        for acc in accs:
            acc[...] = jnp.zeros_like(acc)

        def rows_of(kb):
            return pl.ds(pl.multiple_of(kb * tk, tk), tk)

        def scores(kb, dst):
            for h in range(2):
                dst[h] = jnp.dot(kx_ref[h, rows_of(kb), :], qx[h], preferred_element_type=F32)

        def consume(kb, src, m, masked):
            m_out, alpha, pv = [], [], []
            for h in range(2):
                sh = src[h]
                if masked:
                    kpos = kb * tk + lax.broadcasted_iota(jnp.int32, (tk, tq), 0)
                    qpos = i * tq + lax.broadcasted_iota(jnp.int32, (tk, tq), 1)
                    sh = jnp.where(qpos >= kpos, sh, NEG)
                m_new = jnp.maximum(m[h], jnp.max(sh, axis=0, keepdims=True))
                pt = jnp.exp(sh - m_new).astype(BF16)
                alpha.append(jnp.exp(m[h] - m_new))
                pv.append(jnp.dot(vt_ref[h, :, rows_of(kb)], pt, preferred_element_type=F32))
                m_out.append(m_new)
            for h in range(2):
                accs[h][...] = accs[h][...] * alpha[h] + pv[h]
            return tuple(m_out)

        n_full = (i * tq) // tk

        def two_blocks(t, m):
            kb = 2 * t
            scores(kb + 1, stb_ref)
            m = consume(kb, sta_ref, m, False)
            scores(kb + 2, sta_ref)
            return consume(kb + 1, stb_ref, m, False)

        def odd_tail(m):
            scores(n_full, stb_ref)
            m = consume(n_full - 1, sta_ref, m, False)
            return consume(n_full, stb_ref, m, True)

        def even_tail(m):
            return consume(n_full, sta_ref, m, True)

        scores(0, sta_ref)
        m0 = jnp.full((1, tq), NEG, F32)
        n_two = n_full // 2
        m = lax.fori_loop(0, n_two // 2, lambda u, m: two_blocks(2 * u + 1, two_blocks(2 * u, m)), (m0, m0))
        m = lax.cond(lax.rem(n_two, 2) == 1, lambda m: two_blocks(n_two - 1, m), lambda m: m, m)
        m = lax.cond(lax.rem(n_full, 2) == 1, odd_tail, even_tail, m)

        sub_o = lax.broadcasted_iota(jnp.int32, (PAIR, tq), 0)
        den = [accs[h][_spare(h):_spare(h) + 1, :] for h in range(2)]
        ot = jnp.where(sub_o < HEAD_DIM, acc0_ref[...] / den[0], acc1_ref[...] / den[1])
        o = jnp.transpose(ot)
        o_ref[...] = o.astype(BF16)
        gate = ag_ref[...].astype(F32)
        ma_ref[...] = (o * (gate * _sigmoid(gate))).astype(BF16)
        sub8 = lax.broadcasted_iota(jnp.int32, (8, tq), 0)
        lse_ref[...] = jnp.where(sub8 == 0, m[0] + jnp.log(den[0]), m[1] + jnp.log(den[1]))

        if n_carry:
            @pl.when((j == N_PAIRS - 1) & (i == s // tq - 1))
            def _():
                _gather_finish(carry_src, carry_dst, carry_sems[:2])
                for cp in _gather_own(carry_src, carry_dst, carry_sems[2]):
                    cp.wait()

    carried = [] if carry is None else list(carry)
    carry_sem = ([pltpu.SemaphoreType.DMA((GATHER_SEMS * n_carry,))] * 2
                 + [pltpu.SemaphoreType.DMA((n_carry,))]) if n_carry else []
    return pl.pallas_call(
        body, name="attn_fwd_gather" if n_carry else "attn_fwd", grid=(N_PAIRS, s // tq),
        in_specs=[pl.BlockSpec((tq, PAIR), lambda j, i: (i, j)),
                  pl.BlockSpec((s, PAIR), lambda j, i: (0, j)),
                  pl.BlockSpec((s, PAIR), lambda j, i: (0, j)),
                  pl.BlockSpec((s, 128), lambda j, i: (0, 0)),
                  pl.BlockSpec((tq, PAIR), lambda j, i: (i, j))] + [ANY] * n_carry,
        out_specs=[pl.BlockSpec((tq, PAIR), lambda j, i: (i, j)),
                   pl.BlockSpec((tq, PAIR), lambda j, i: (i, j)),
                   pl.BlockSpec((None, 8, tq), lambda j, i: (j, 0, i))] + [ANY] * n_carry,
        out_shape=[jax.ShapeDtypeStruct((s, ATTN_WIDTH), BF16), jax.ShapeDtypeStruct((s, ATTN_WIDTH), BF16),
                   jax.ShapeDtypeStruct((N_PAIRS, 8, s), F32)] + (_gather_shapes(*carried) if n_carry else []),
        scratch_shapes=[pltpu.VMEM((2, s, PAIR), BF16), pltpu.VMEM((2, PAIR, s), BF16),
                        pltpu.VMEM((PAIR, tq), F32), pltpu.VMEM((PAIR, tq), F32),
                        pltpu.VMEM((2, tk, tq), F32), pltpu.VMEM((2, tk, tq), F32)] + carry_sem,
        compiler_params=_params(("arbitrary", "arbitrary")),
    )(q, k, v, c, ag, *carried)


def _outproj_fwd(x, po, ma, wo, tm=512):
    s = x.shape[0]

    def body(x_ref, po_ref, ma_ref, wo_ref, xn_ref):
        xn_ref[...] = (x_ref[...]
                       + jnp.dot(po_ref[...], wo_ref[0:POOL_WIDTH, :], preferred_element_type=F32)
                       + jnp.dot(ma_ref[...], wo_ref[POOL_WIDTH:, :], preferred_element_type=F32))

    return pl.pallas_call(
        body, name="outproj_fwd", grid=(s // tm,),
        in_specs=[_row_spec(tm, D_MODEL), _row_spec(tm, 512), _row_spec(tm, 512),
                  _full_spec((D_MODEL, D_MODEL))],
        out_specs=_row_spec(tm, D_MODEL),
        out_shape=jax.ShapeDtypeStruct((s, D_MODEL), F32),
        compiler_params=_params(("parallel",)),
    )(x, po, ma, wo)


def _loss_head(x, po, ma, wo, tgt, g, tm=512):
    s = x.shape[0]

    def body(x_ref, po_ref, ma_ref, wo_ref, t_ref, g_ref, dx_ref, dg_ref, loss_ref):
        @pl.when(pl.program_id(0) == 0)
        def _():
            dg_ref[...] = jnp.zeros_like(dg_ref)
            loss_ref[...] = jnp.zeros_like(loss_ref)

        xf = (x_ref[...]
              + jnp.dot(po_ref[...], wo_ref[0:POOL_WIDTH, :], preferred_element_type=F32)
              + jnp.dot(ma_ref[...], wo_ref[POOL_WIDTH:, :], preferred_element_type=F32))
        r = lax.rsqrt(jnp.mean(xf * xf, axis=-1, keepdims=True) + RMS_EPS)
        xh = xf * r
        gg = g_ref[...]
        e = xh * gg - t_ref[...]
        loss_ref[...] += 0.5 * jnp.sum(jnp.mean(e * e, axis=-1, keepdims=True))
        dy = e * (1.0 / D_MODEL)
        u = dy * gg
        dx_ref[...] = r * (u - xh * jnp.mean(xh * u, axis=-1, keepdims=True))
        dg_ref[...] += jnp.sum(dy * xh, axis=0, keepdims=True)

    return pl.pallas_call(
        body, name="loss_head", grid=(s // tm,),
        in_specs=[_row_spec(tm, D_MODEL), _row_spec(tm, 512), _row_spec(tm, 512), _full_spec((D_MODEL, D_MODEL)),
                  _row_spec(tm, D_MODEL), _full_spec((1, D_MODEL))],
        out_specs=[_row_spec(tm, D_MODEL), _full_spec((1, D_MODEL)), _full_spec((8, 128))],
        out_shape=[jax.ShapeDtypeStruct((s, D_MODEL), F32), jax.ShapeDtypeStruct((1, D_MODEL), F32),
                   jax.ShapeDtypeStruct((8, 128), F32)],
        compiler_params=_params(("arbitrary",)),
    )(x, po, ma, wo, tgt, g)


def _outproj_bwd(dx, po, ma, ag, o, wo, head_sel, tm=512):
    s = dx.shape[0]

    def body(dx_ref, po_ref, ma_ref, ag_ref, o_ref, wo_ref, sel_ref, dmp_ref, do_ref, dag_ref, dt_ref, gwo_ref):
        @pl.when(pl.program_id(0) == 0)
        def _():
            gwo_ref[...] = jnp.zeros_like(gwo_ref)

        dxb = dx_ref[...].astype(BF16)
        dm = lax.dot_general(dxb, wo_ref[...], NT, preferred_element_type=F32)
        dmp_ref[...] = dm[:, 0:POOL_WIDTH].astype(BF16)
        dma = dm[:, POOL_WIDTH:]
        gate = ag_ref[...].astype(F32)
        of = o_ref[...].astype(F32)
        sg = _sigmoid(gate)
        do = dma * (gate * sg)
        do_ref[...] = do.astype(BF16)
        dag_ref[...] = (dma * of * (sg * (1.0 + gate * (1.0 - sg)))).astype(BF16)
        prod = do * of
        hi = prod.astype(BF16)
        lo = (prod - hi.astype(F32)).astype(BF16)
        dt_ref[...] = (lax.dot_general(sel_ref[...], hi, NT, preferred_element_type=F32)
                       + lax.dot_general(sel_ref[...], lo, NT, preferred_element_type=F32))
        gwo_ref[0:POOL_WIDTH, :] += lax.dot_general(po_ref[...], dxb, TN, preferred_element_type=F32)
        gwo_ref[POOL_WIDTH:, :] += lax.dot_general(ma_ref[...], dxb, TN, preferred_element_type=F32)

    act = jax.ShapeDtypeStruct((s, 512), BF16)
    return pl.pallas_call(
        body, name="outproj_bwd", grid=(s // tm,),
        in_specs=[_row_spec(tm, D_MODEL)] + [_row_spec(tm, 512)] * 4
                 + [_full_spec((D_MODEL, D_MODEL)), _full_spec((8, ATTN_WIDTH))],
        out_specs=[_row_spec(tm, 512)] * 3 + [pl.BlockSpec((8, tm), lambda i: (0, i)),
                                              _full_spec((D_MODEL, D_MODEL))],
        out_shape=[act, act, act, jax.ShapeDtypeStruct((8, s), F32),
                   jax.ShapeDtypeStruct((D_MODEL, D_MODEL), F32)],
        compiler_params=_params(("arbitrary",)),
    )(dx, po, ma, ag, o, wo, head_sel)


def _attn_bwd(q, k, v, do, c, lse, dt, carry=None, tq=256, chunk=512):
    s = q.shape[0]
    tk = 2 * tq
    nq = s // tq
    n_carry = 0 if carry is None else len(carry)

    def body(*refs):
        q_ref, k_ref, v_ref, do_ref, c_ref, lse_ref, dt_ref = refs[:7]
        carry_src = refs[7:7 + n_carry]
        dq_ref, dk_ref, dv_ref, dcs_ref, drs_ref = refs[7 + n_carry:12 + n_carry]
        carry_dst = refs[12 + n_carry:12 + 2 * n_carry]
        (qxt_ref, doxt_ref, qm_ref, dox_ref, dqt0_ref, dqt1_ref, dk0_ref, dk1_ref, dvx_ref,
         kx_ref, vx_ref, ktm_ref, sa_ref, sb_ref) = refs[12 + 2 * n_carry:26 + 2 * n_carry]
        carry_sems = refs[26 + 2 * n_carry:]
        carry_big = tuple(zip(carry_src, carry_dst))
        j = pl.program_id(0)
        b = pl.program_id(1)
        dqts = (dqt0_ref, dqt1_ref)
        dks = (dk0_ref, dk1_ref)

        if n_carry:
            @pl.when((j == 0) & (b == 0))
            def _():
                _scatter_start(carry_big, None, carry_sems)

        @pl.when(b == 0)
        def _():
            def prep(ch, carry):
                rows = pl.ds(pl.multiple_of(ch * chunk, chunk), chunk)
                qq = q_ref[rows, :].astype(F32)
                dd = do_ref[rows, :].astype(F32)
                qt = jnp.transpose(qq)
                ddt = jnp.transpose(dd)
                lane = lax.broadcasted_iota(jnp.int32, (chunk, PAIR), 1)
                sub = lax.broadcasted_iota(jnp.int32, (PAIR, chunk), 0)
                for h in range(2):
                    sp = _spare(h)
                    qm_ref[h, rows, :] = jnp.where(_own(lane, h), qq, jnp.where(lane == sp, 1.0, 0.0)).astype(BF16)
                    dox_ref[h, rows, :] = jnp.where(_own(lane, h), dd, 0.0).astype(BF16)
                    qx = jnp.where(_own(sub, h), qt, jnp.where((sub >= sp) & (sub < sp + 3), 1.0, 0.0))
                    for t, part in enumerate(_split3(-lse_ref[pl.ds(h, 1), rows])):
                        qx = jnp.where(sub == sp + 3 + t, part, qx)
                    qxt_ref[h, :, rows] = qx.astype(BF16)
                    dx = jnp.where(_own(sub, h), ddt, 0.0)
                    for t, part in enumerate(_split3(-dt_ref[pl.ds(h, 1), rows])):
                        dx = jnp.where(sub == sp + t, part, dx)
                    doxt_ref[h, :, rows] = dx.astype(BF16)
                return carry

            lax.fori_loop(0, s // chunk, prep, 0)
            for ref in dqts:
                ref[...] = jnp.zeros_like(ref)

        kk = k_ref[...].astype(F32)
        vv = v_ref[...].astype(F32)
        cc = c_ref[...]
        kt = jnp.transpose(kk)
        lane = lax.broadcasted_iota(jnp.int32, (tk, PAIR), 1)
        sub = lax.broadcasted_iota(jnp.int32, (PAIR, tk), 0)
        for h in range(2):
            sp = _spare(h)
            ccol = jnp.sum(jnp.where(lane == 2 * j + h, cc, 0.0), axis=1, keepdims=True)
            kx = jnp.where(_own(lane, h), kk, jnp.where((lane >= sp + 3) & (lane < sp + 6), 1.0, 0.0))
            for t, part in enumerate(_split3(-ccol)):
                kx = jnp.where(lane == sp + t, part, kx)
            kx_ref[h] = kx.astype(BF16)
            vx_ref[h] = jnp.where(_own(lane, h), vv, jnp.where((lane >= sp) & (lane < sp + 3), 1.0, 0.0)).astype(BF16)
            ktm_ref[h] = jnp.where(_own(sub, h), kt, jnp.where(sub == sp, 1.0, 0.0)).astype(BF16)
        for ref in dks:
            ref[...] = jnp.zeros_like(ref)
        dvx_ref[...] = jnp.zeros_like(dvx_ref)

        def cols_of(i):
            return pl.ds(pl.multiple_of(i * tq, tq), tq)

        def scores(i, dst):
            cols = cols_of(jnp.minimum(i, nq - 1))
            for h in range(2):
                dst[h] = jnp.dot(kx_ref[h], qxt_ref[h, :, cols], preferred_element_type=F32)
                dst[2 + h] = jnp.dot(vx_ref[h], doxt_ref[h, :, cols], preferred_element_type=F32)

        def consume(i, src, masked):
            cols = cols_of(i)
            for h in range(2):
                arg = src[h]
                if masked:
                    kpos = b * tk + lax.broadcasted_iota(jnp.int32, (tk, tq), 0)
                    qpos = i * tq + lax.broadcasted_iota(jnp.int32, (tk, tq), 1)
                    arg = jnp.where(qpos >= kpos, arg, NEG)
                pt = jnp.exp(arg)
                dst = (pt * src[2 + h]).astype(BF16)
                dvx_ref[...] += jnp.dot(pt.astype(BF16), dox_ref[h, cols, :], preferred_element_type=F32)
                dks[h][...] += jnp.dot(dst, qm_ref[h, cols, :], preferred_element_type=F32)
                dqts[h][:, cols] += jnp.dot(ktm_ref[h], dst, preferred_element_type=F32)

        i0 = 2 * b

        def two_blocks(t, masked):
            i = i0 + 2 * t
            scores(i + 1, sb_ref)
            consume(i, sa_ref, masked)
            scores(i + 2, sa_ref)
            consume(i + 1, sb_ref, masked)

        def four_blocks(u, carry):
            two_blocks(1 + 2 * u, False)
            two_blocks(2 + 2 * u, False)
            return carry

        scores(i0, sa_ref)
        two_blocks(0, True)
        n_open = (nq - i0) // 2 - 1
        lax.fori_loop(0, n_open // 2, four_blocks, 0)

        @pl.when(lax.rem(n_open, 2) == 1)
        def _():
            two_blocks(n_open, False)

        dv_ref[...] = dvx_ref[...].astype(BF16)
        dk_ref[...] = jnp.where(lane < HEAD_DIM, dk0_ref[...], dk1_ref[...]).astype(BF16)
        dcs_ref[...] = jnp.where(lane == _spare(0), dk0_ref[...], jnp.where(lane == _spare(1), dk1_ref[...], 0.0))

        @pl.when(b == s // tk - 1)
        def _():
            sub8 = lax.broadcasted_iota(jnp.int32, (8, s), 0)
            drs_ref[...] = jnp.where(sub8 == 0, dqt0_ref[_spare(0):_spare(0) + 1, :],
                                     jnp.where(sub8 == 1, dqt1_ref[_spare(1):_spare(1) + 1, :], 0.0))
            sub_c = lax.broadcasted_iota(jnp.int32, (PAIR, chunk), 0)
            for ch in range(s // chunk):
                rows = pl.ds(ch * chunk, chunk)
                both = jnp.where(sub_c < HEAD_DIM, dqt0_ref[:, rows], dqt1_ref[:, rows])
                dq_ref[rows, :] = (jnp.transpose(both) * Q_SCALE).astype(BF16)

        if n_carry:
            @pl.when((j == N_PAIRS - 1) & (b == s // tk - 1))
            def _():
                _scatter_finish(carry_big, None, carry_sems)

    act = jax.ShapeDtypeStruct((s, ATTN_WIDTH), BF16)
    pair_rows = pl.BlockSpec((s, PAIR), lambda j, b: (0, j))
    pair_blk = pl.BlockSpec((tk, PAIR), lambda j, b: (b, j))
    stat = pl.BlockSpec((None, 2, s), lambda j, b: (j, 0, 0))
    carried = [] if carry is None else list(carry)
    carry_sem = [pltpu.SemaphoreType.DMA((SCATTER_SEMS,))] * 2 if n_carry else []
    return pl.pallas_call(
        body, name="attn_bwd_scatter" if n_carry else "attn_bwd", grid=(N_PAIRS, s // tk),
        in_specs=[pair_rows, pair_blk, pair_blk, pair_rows,
                  pl.BlockSpec((tk, 128), lambda j, b: (b, 0)),
                  pl.BlockSpec((None, 8, s), lambda j, b: (j, 0, 0)), stat] + [ANY] * n_carry,
        out_specs=[pair_rows, pair_blk, pair_blk, pair_blk,
                   pl.BlockSpec((None, 8, s), lambda j, b: (j, 0, 0))] + [ANY] * n_carry,
        out_shape=[act, act, act, jax.ShapeDtypeStruct((s, N_PAIRS * 128), F32),
                   jax.ShapeDtypeStruct((N_PAIRS, 8, s), F32)] + (_scatter_shapes(*carried) if n_carry else []),
        scratch_shapes=[pltpu.VMEM((2, PAIR, s), BF16), pltpu.VMEM((2, PAIR, s), BF16),
                        pltpu.VMEM((2, s, PAIR), BF16), pltpu.VMEM((2, s, PAIR), BF16),
                        pltpu.VMEM((PAIR, s), F32), pltpu.VMEM((PAIR, s), F32),
                        pltpu.VMEM((tk, PAIR), F32), pltpu.VMEM((tk, PAIR), F32), pltpu.VMEM((tk, PAIR), F32),
                        pltpu.VMEM((2, tk, PAIR), BF16), pltpu.VMEM((2, tk, PAIR), BF16),
                        pltpu.VMEM((2, PAIR, tk), BF16),
                        pltpu.VMEM((4, tk, tq), F32), pltpu.VMEM((4, tk, tq), F32)] + carry_sem,
        compiler_params=_params(("arbitrary", "arbitrary")),
    )(q, k, v, do, c, lse, dt, *carried)


def _forget_bwd(dcs, drs, lf, tm=512):
    s = lf.shape[0]
    n = s // tm

    def body(dcs_ref, drs_ref, lf_ref, dpf_ref, dfb_ref, carry_ref):
        @pl.when(pl.program_id(0) == 0)
        def _():
            carry_ref[...] = jnp.zeros_like(carry_ref)
            dfb_ref[...] = jnp.zeros_like(dfb_ref)

        lane = lax.broadcasted_iota(jnp.int32, (tm, 128), 1)
        dc = jnp.zeros((tm, 128), F32)
        for pj in range(N_PAIRS):
            rows = jnp.concatenate([drs_ref[pj], jnp.zeros((120, tm), F32)], axis=0)
            cols = jnp.transpose(rows)
            if pj:
                cols = pltpu.roll(cols, 2 * pj, 1)
            dc = dc + jnp.where((lane == 2 * pj) | (lane == 2 * pj + 1), cols, 0.0)
            blk = dcs_ref[:, pj * 128:(pj + 1) * 128]
            for h in range(2):
                head = 2 * pj + h
                moved = pltpu.roll(blk, (head - _spare(h)) % 128, 1) if head != _spare(h) else blk
                dc = dc - jnp.where(lane == head, moved, 0.0)
        dlf = _scan_rows(dc, reverse=True) + carry_ref[0:1, :]
        carry_ref[...] = dlf[0:8, :]
        dz = jnp.where(lane < 8, dlf * (1.0 - jnp.exp(lf_ref[...])), 0.0)
        dpf_ref[...] = dz.astype(BF16)
        dfb_ref[...] += jnp.sum(dz, axis=0, keepdims=True)

    return pl.pallas_call(
        body, name="forget_bwd", grid=(n,),
        in_specs=[pl.BlockSpec((tm, N_PAIRS * 128), lambda i: (n - 1 - i, 0)),
                  pl.BlockSpec((N_PAIRS, 8, tm), lambda i: (0, 0, n - 1 - i)),
                  pl.BlockSpec((tm, 128), lambda i: (n - 1 - i, 0))],
        out_specs=[pl.BlockSpec((tm, 128), lambda i: (n - 1 - i, 0)), _full_spec((1, 128))],
        out_shape=[jax.ShapeDtypeStruct((s, 128), BF16), jax.ShapeDtypeStruct((1, 128), F32)],
        scratch_shapes=[pltpu.VMEM((8, 128), F32)],
        compiler_params=_params(("arbitrary",)),
    )(dcs, drs, lf)


def _pool_bwd(pu, pg, dmp, pw, ps, tm=512):
    s = pu.shape[0]
    hb = tm // HALO
    n = s // tm
    last_halo = s // HALO - 1

    def body(pu_ref, halo_ref, pg_ref, dmp_ref, pgn_ref, dmpn_ref, pw_ref, ps_ref,
             dpu_ref, dpg_ref, gpw_ref, gps_ref):
        i = pl.program_id(0)

        @pl.when(i == 0)
        def _():
            gpw_ref[...] = jnp.zeros_like(gpw_ref)
            gps_ref[...] = jnp.zeros_like(gps_ref)

        u = pu_ref[...].astype(F32)
        halo = jnp.where(i == 0, 0.0, halo_ref[...].astype(F32))
        d = _pool_delta(u, halo, i * tm)
        e_parts, dd_parts = [], []
        for gi in range(len(POOL_WINDOWS)):
            cols = slice(gi * GROUP_DIM, (gi + 1) * GROUP_DIM)
            wg = pw_ref[gi]
            scale = ps_ref[:, cols]
            db = d[gi].astype(BF16)
            z = jnp.dot(db, wg, preferred_element_type=F32)
            gate = pg_ref[:, cols].astype(F32)
            sg = _sigmoid(gate)
            dm = dmp_ref[:, cols].astype(F32)
            dy = dm * (gate * sg)
            dpg_ref[:, cols] = (dm * (z * scale) * (sg * (1.0 + gate * (1.0 - sg)))).astype(BF16)
            gps_ref[:, cols] += jnp.sum(dy * z, axis=0, keepdims=True)
            dz = (dy * scale).astype(BF16)
            gpw_ref[gi] += lax.dot_general(db, dz, TN, preferred_element_type=F32)
            dd = lax.dot_general(dz, wg, NT, preferred_element_type=F32)
            gate_n = pgn_ref[:, cols].astype(F32)
            dz_n = (dmpn_ref[:, cols].astype(F32) * (gate_n * _sigmoid(gate_n)) * scale).astype(BF16)
            dd_n = lax.dot_general(dz_n, wg, NT, preferred_element_type=F32)
            dd_n = jnp.where(i == n - 1, 0.0, dd_n)
            dd_parts.append(dd)
            e_parts.append(jnp.concatenate([_over_count(dd, i * tm, gi), dd_n * (1.0 / POOL_WINDOWS[gi])], axis=0))
        lead = _window_sums(jnp.concatenate(e_parts, axis=1), forward=True)
        for gi in range(len(POOL_WINDOWS)):
            cols = slice(gi * GROUP_DIM, (gi + 1) * GROUP_DIM)
            dpu_ref[:, cols] = (lead[gi][0:tm, :] - dd_parts[gi]).astype(BF16)

    act = jax.ShapeDtypeStruct((s, 512), BF16)
    prev_halo = pl.BlockSpec((HALO, 512), lambda i: (jnp.maximum(i * hb - 1, 0), 0))
    next_halo = pl.BlockSpec((HALO, 512), lambda i: (jnp.minimum((i + 1) * hb, last_halo), 0))
    return pl.pallas_call(
        body, name="pool_bwd", grid=(n,),
        in_specs=[_row_spec(tm, 512), prev_halo, _row_spec(tm, 512), _row_spec(tm, 512), next_halo, next_halo,
                  _full_spec((4, GROUP_DIM, GROUP_DIM)), _full_spec((1, 512))],
        out_specs=[_row_spec(tm, 512), _row_spec(tm, 512), _full_spec((4, GROUP_DIM, GROUP_DIM)),
                   _full_spec((1, 512))],
        out_shape=[act, act, jax.ShapeDtypeStruct((4, GROUP_DIM, GROUP_DIM), F32),
                   jax.ShapeDtypeStruct((1, 512), F32)],
        compiler_params=_params(("arbitrary",)),
    )(pu, pu, pg, dmp, pg, dmp, pw, ps)


def _inproj_bwd_x(dsec, wt, x, g, dxo, exchange=None, scatter=None, tm=512):
    s = x.shape[0]
    n_steps = s // tm
    carried = list(exchange or scatter or ())
    n_carry = len(carried)

    def body(*refs):
        d_refs = refs[0:7]
        wt_ref, x_ref, g_ref, dxo_ref = refs[7:11]
        carry_src = refs[11:11 + n_carry]
        dxi_ref, dg_ref = refs[11 + n_carry:13 + n_carry]
        carry_dst = refs[13 + n_carry:13 + 2 * n_carry]
        sems = refs[13 + 2 * n_carry:]
        step = pl.program_id(0)

        def comm(phase):
            if exchange is not None:
                (_exchange_start if phase == 0 else _exchange_finish)(carry_src, carry_dst, None, sems)
            else:
                big = tuple(zip(carry_src[:2], carry_dst[:2]))
                (_scatter_start if phase == 0 else _scatter_finish)(big, (carry_src[2], carry_dst[2]), sems)

        @pl.when(step == 0)
        def _():
            dg_ref[...] = jnp.zeros_like(dg_ref)
            if n_carry:
                comm(0)

        dh = None
        for d_ref, (off, ncol) in zip(d_refs, SECTIONS):
            t = jnp.dot(d_ref[...], wt_ref[off:off + ncol, :], preferred_element_type=F32)
            dh = t if dh is None else dh + t
        xf = x_ref[...]
        r = lax.rsqrt(jnp.mean(xf * xf, axis=-1, keepdims=True) + RMS_EPS)
        xh = xf * r
        u = dh * g_ref[...]
        dxi_ref[...] = dxo_ref[...] + r * (u - xh * jnp.mean(xh * u, axis=-1, keepdims=True))
        dg_ref[...] += jnp.sum(dh * xh, axis=0, keepdims=True)

        if n_carry:
            @pl.when(step == n_steps - 1)
            def _():
                comm(1)

    if exchange is not None:
        name, carry_shapes, n_sems = "inproj_bwd_x_exchange", _exchange_shapes(*exchange), 3
    elif scatter is not None:
        name, n_sems = "inproj_bwd_x_scatter", SCATTER_SEMS
        carry_shapes = _scatter_shapes(*scatter[:2]) + [jax.ShapeDtypeStruct((3, SMALL_HALF, 128), F32)]
    else:
        name, carry_shapes, n_sems = "inproj_bwd_x", [], 0
    return pl.pallas_call(
        body, name=name, grid=(n_steps,),
        in_specs=[_row_spec(tm, ncol) for _, ncol in SECTIONS]
                 + [_full_spec((IN_COLS_PAD, D_MODEL)), _row_spec(tm, D_MODEL), _full_spec((1, D_MODEL)),
                    _row_spec(tm, D_MODEL)] + [ANY] * n_carry,
        out_specs=[_row_spec(tm, D_MODEL), _full_spec((1, D_MODEL))] + [ANY] * n_carry,
        out_shape=[jax.ShapeDtypeStruct((s, D_MODEL), F32), jax.ShapeDtypeStruct((1, D_MODEL), F32)] + carry_shapes,
        scratch_shapes=[pltpu.SemaphoreType.DMA((n_sems,))] * 2 if n_carry else [],
        compiler_params=_params(("arbitrary",)),
    )(*dsec, wt, x, g, dxo, *carried)


def _inproj_bwd_w(h, dsec, tm=512):
    s = h.shape[0]
    n_steps = s // tm

    def body(*refs):
        h_ref = refs[0]
        d_refs = refs[1:8]
        out_ref, gw_ref = refs[8:]
        step = pl.program_id(0)

        @pl.when(step == 0)
        def _():
            gw_ref[...] = jnp.zeros_like(gw_ref)

        hh = h_ref[...]
        for d_ref, (off, ncol) in zip(d_refs, SECTIONS):
            gw_ref[:, off:off + ncol] += lax.dot_general(hh, d_ref[...], TN, preferred_element_type=F32)

        @pl.when(step == n_steps - 1)
        def _():
            for k in range(N_CHIPS):
                out_ref[k] = gw_ref[:, k * SHARD_COLS:(k + 1) * SHARD_COLS]

    return pl.pallas_call(
        body, name="inproj_bwd_w", grid=(n_steps,),
        in_specs=[_row_spec(tm, D_MODEL)] + [_row_spec(tm, ncol) for _, ncol in SECTIONS],
        out_specs=_full_spec((N_CHIPS, D_MODEL, SHARD_COLS)),
        out_shape=jax.ShapeDtypeStruct((N_CHIPS, D_MODEL, SHARD_COLS), F32),
        scratch_shapes=[pltpu.VMEM((D_MODEL, IN_COLS_PAD), F32)],
        compiler_params=_params(("arbitrary",), vmem_mb=60),
    )(h, *dsec)


def _elementwise(fn, name, n_out, arrays, tm):
    shape = arrays[0].shape
    rest = shape[1:]
    n_in = len(arrays)

    def body(*refs):
        outs = fn(*[r[...] for r in refs[:n_in]])
        for r, val in zip(refs[n_in:], outs):
            r[...] = val

    spec = pl.BlockSpec((tm,) + rest, lambda i: (i,) + (0,) * len(rest))
    return pl.pallas_call(
        body, name=name, grid=(shape[0] // tm,),
        in_specs=[spec] * n_in, out_specs=[spec] * n_out, out_shape=[jax.ShapeDtypeStruct(shape, F32)] * n_out,
        compiler_params=_params(("parallel",)),
    )(*arrays)


def _add2(a, b, name, tm):
    return _elementwise(lambda p, q: (p + q,), name, 1, [a, b], tm)[0]


def _chip_sum(place, gi, ri, go, ro, nb=2):
    half_i, cols_i = ri.shape[2:]
    half_o, cols_o = ro.shape[2:]
    tm = half_i // nb

    def body(place_ref, gi_ref, ri_ref, go_ref, ro_ref, pi_ref, po_ref):
        pi_ref[...] = (gi_ref[...] + ri_ref[...]).astype(BF16)
        po_ref[...] = (go_ref[...] + ro_ref[...]).astype(BF16)

    blk_i, blk_o = (None, None, tm, cols_i), (None, None, half_o, cols_o)
    return pl.pallas_call(
        body, name="chip_sum",
        grid_spec=pltpu.PrefetchScalarGridSpec(
            num_scalar_prefetch=1, grid=(N_CHIPS, nb),
            in_specs=[pl.BlockSpec(blk_i, lambda k, i, p: (0, k, p[1] * nb + i, 0)),
                      pl.BlockSpec(blk_i, lambda k, i, p: (0, k, i, 0)),
                      pl.BlockSpec(blk_o, lambda k, i, p: (0, k, p[1], 0)),
                      pl.BlockSpec(blk_o, lambda k, i, p: (0, k, 0, 0))],
            out_specs=[pl.BlockSpec(blk_i, lambda k, i, p: (0, k, i, 0)),
                       pl.BlockSpec(blk_o, lambda k, i, p: (0, k, 0, 0))]),
        out_shape=[jax.ShapeDtypeStruct(ri.shape, BF16), jax.ShapeDtypeStruct(ro.shape, BF16)],
        compiler_params=_params(("arbitrary", "arbitrary")),
    )(place, gi, ri, go, ro)


def _mesh_sum(place, g, r, got, into, layer, name, tm):
    _, _, rows, cols = g.shape
    half = rows // 2
    nb = half // tm

    def body(place_ref, g_ref, r_ref, got_ref, into_ref, o_ref):
        own = g_ref[...] + r_ref[...]
        o_ref[...] = (own + got_ref[0].astype(F32)) + (got_ref[1].astype(F32) + got_ref[2].astype(F32))

    blk = (None, None, tm, cols)
    return pl.pallas_call(
        body, name=name,
        grid_spec=pltpu.PrefetchScalarGridSpec(
            num_scalar_prefetch=1, grid=(nb,),
            in_specs=[pl.BlockSpec(blk, lambda i, p: (0, p[0], p[1] * nb + i, 0)),
                      pl.BlockSpec(blk, lambda i, p: (0, p[0], i, 0)),
                      pl.BlockSpec((None, 3, tm, cols), lambda i, p: (0, 0, i, 0)),
                      ANY],
            out_specs=pl.BlockSpec((None, tm, cols), lambda i, p: (layer, p[1] * nb + i, 0))),
        out_shape=jax.ShapeDtypeStruct(into.shape, F32),
        input_output_aliases={4: 0},
        compiler_params=_params(("parallel",)),
    )(place, g, r, got, into)


def _mesh_sum_small(place, ps, got, tm):
    nb = SMALL_HALF // tm

    def body(place_ref, ps_ref, got_ref, o_ref):
        o_ref[...] = (ps_ref[...] + got_ref[0]) + (got_ref[1] + got_ref[2])

    return pl.pallas_call(
        body, name="mesh_sum_small",
        grid_spec=pltpu.PrefetchScalarGridSpec(
            num_scalar_prefetch=1, grid=(nb,),
            in_specs=[pl.BlockSpec((tm, 128), lambda i, p: (p[1] * nb + i, 0)),
                      pl.BlockSpec((3, tm, 128), lambda i, p: (0, i, 0))],
            out_specs=pl.BlockSpec((tm, 128), lambda i, p: (p[1] * nb + i, 0))),
        out_shape=jax.ShapeDtypeStruct((SMALL_ROWS, 128), F32),
        compiler_params=_params(("parallel",)),
    )(place, ps, got)


def _adamw(g, w, m, v, name, tm):
    def fn(g, w, m, v):
        m = ADAM_B1 * m + (1.0 - ADAM_B1) * g
        v = ADAM_B2 * v + (1.0 - ADAM_B2) * (g * g)
        m_hat = m / (1.0 - ADAM_B1 ** ADAM_STEP)
        v_hat = v / (1.0 - ADAM_B2 ** ADAM_STEP)
        delta = -ADAM_LR * (m_hat / (jnp.sqrt(v_hat) + ADAM_EPS) + ADAM_WD * w)
        return g, delta, m, v

    return _elementwise(fn, name, 4, [g, w, m, v], tm)


def _position():
    x, y, c = lax.axis_index("x"), lax.axis_index("y"), lax.axis_index("c")
    other_chips = [(x, 1 - y), (1 - x, y), (1 - x, 1 - y)]
    return x, y, c, other_chips


def _remote(src, dst, sems, k, to):
    send_sems, recv_sems = sems
    return pltpu.make_async_remote_copy(src_ref=src, dst_ref=dst, send_sem=send_sems.at[k],
                                        recv_sem=recv_sems.at[k], device_id=to, device_id_type=MESH)


def _comm_call(body, name, n_in, out_shape, n_remote):
    return pl.pallas_call(
        body, name=name, in_specs=[ANY] * n_in, out_specs=[ANY] * len(out_shape), out_shape=out_shape,
        scratch_shapes=[pltpu.SemaphoreType.DMA((n_remote,)), pltpu.SemaphoreType.DMA((n_remote,))],
    )


def _halves(ref_rows, c):
    half = ref_rows // 2
    return pl.ds(half * c, half), pl.ds(half * (1 - c), half)


def _gather_weights(wi):
    def body(wi_ref, gi_ref, send_sems, recv_sems):
        _gather_start((wi_ref,), (gi_ref,), (send_sems, recv_sems))
        _gather_finish((wi_ref,), (gi_ref,), (send_sems, recv_sems))

    return _comm_call(body, "gather_weights", 1, _gather_shapes(wi), GATHER_SEMS)(wi)[0]


GATHER_SEMS = 6
SCATTER_SEMS = 9


def _gather_shapes(*arrays):
    return [jax.ShapeDtypeStruct((a.shape[0], N_CHIPS) + a.shape[1:], a.dtype) for a in arrays]


def _gather_sends(srcs, dsts, sems):
    x, y, c, chips = _position()
    out = []
    for k, (cx, cy) in enumerate(chips):
        for a, (src, dst) in enumerate(zip(srcs, dsts)):
            mine, _ = _halves(src.shape[1], c)
            out.append(_remote(src.at[:, mine], dst.at[:, k + 1, mine], sems, len(srcs) * k + a, (cx, cy, c)))
    return out


def _gather_own(srcs, dsts, local_sems):
    return [pltpu.make_async_copy(src, dst.at[:, 0], local_sems.at[a]) for a, (src, dst) in enumerate(zip(srcs, dsts))]


def _gather_start(srcs, dsts, sems):
    for cp in _gather_sends(srcs, dsts, sems):
        cp.start()


def _gather_finish(srcs, dsts, sems):
    x, y, c, _ = _position()
    n = len(srcs)
    forwards = []
    for k in range(3):
        for a, dst in enumerate(dsts):
            mine, _ = _halves(dst.shape[2], c)
            blk = dst.at[:, k + 1, mine]
            _remote(blk, blk, sems, n * k + a, (x, y, c)).wait_recv()
            forwards.append(_remote(blk, blk, sems, 3 * n + n * k + a, (x, y, 1 - c)))
            forwards[-1].start()
    for k in range(3):
        for a, dst in enumerate(dsts):
            _, other = _halves(dst.shape[2], c)
            blk = dst.at[:, k + 1, other]
            _remote(blk, blk, sems, 3 * n + n * k + a, (x, y, c)).wait_recv()
    for cp in _gather_sends(srcs, dsts, sems) + forwards:
        cp.wait_send()


def _scatter_sends(big, small, sems):
    x, y, c, chips = _position()
    n = len(big) + (small is not None)
    out = []
    for k, (cx, cy) in enumerate(chips):
        ck = 2 * cx + cy
        for a, (src, dst) in enumerate(big):
            out.append(_remote(src.at[:, ck], dst.at[:, k], sems, n * k + a, (cx, cy, c)))
        if small is not None:
            half, _ = _halves(SMALL_ROWS, c)
            out.append(_remote(small[0].at[half], small[1].at[k], sems, n * k + n - 1, (cx, cy, c)))
    return out


def _scatter_start(big, small, sems):
    for cp in _scatter_sends(big, small, sems):
        cp.start()


def _scatter_finish(big, small, sems):
    x, y, c, _ = _position()
    n = len(big) + (small is not None)
    for k in range(3):
        landing = [dst.at[:, k] for _, dst in big] + ([small[1].at[k]] if small is not None else [])
        for a, blk in enumerate(landing):
            _remote(blk, blk, sems, n * k + a, (x, y, c)).wait_recv()
    for cp in _scatter_sends(big, small, sems):
        cp.wait_send()


def _exchange_shapes(gi, go):
    return [jax.ShapeDtypeStruct(a.shape[:2] + (a.shape[2] // 2, a.shape[3]), F32) for a in (gi, go)]


def _exchange_copies(srcs, dsts, small, sems):
    x, y, c, _ = _position()
    cps = [_remote(src.at[:, :, _halves(src.shape[2], c)[1]], dst, sems, a, (x, y, 1 - c))
           for a, (src, dst) in enumerate(zip(srcs, dsts))]
    if small is not None:
        cps.append(_remote(small[0], small[1], sems, len(srcs), (x, y, 1 - c)))
    return cps


def _exchange_start(srcs, dsts, small, sems):
    for cp in _exchange_copies(srcs, dsts, small, sems):
        cp.start()


def _exchange_finish(srcs, dsts, small, sems):
    for cp in _exchange_copies(srcs, dsts, small, sems):
        cp.wait()


def _exchange_sibling(gi, go, sm):
    def body(gi_ref, go_ref, sm_ref, ri_ref, ro_ref, rs_ref, send_sems, recv_sems):
        args = ((gi_ref, go_ref), (ri_ref, ro_ref), (sm_ref, rs_ref), (send_sems, recv_sems))
        _exchange_start(*args)
        _exchange_finish(*args)

    out_shape = _exchange_shapes(gi, go) + [jax.ShapeDtypeStruct(sm.shape, F32)]
    return _comm_call(body, "exchange_sibling", 3, out_shape, 3)(gi, go, sm)


def _scatter_shapes(pi, po):
    return [jax.ShapeDtypeStruct((a.shape[0], 3) + a.shape[2:], a.dtype) for a in (pi, po)]


def _join_sibling(fi, fo, fs, late):
    def body(fi_ref, fo_ref, fs_ref, late_ref, oi_ref, oo_ref, os_ref, sum_ref, send_sems, recv_sems,
             sib_ref, chip_ref, got_ref):
        x, y, c, chips = _position()
        sems = (send_sems, recv_sems)
        sent, got = [], []
        for a, (src, dst) in enumerate(((fi_ref, oi_ref), (fo_ref, oo_ref))):
            mine, other = _halves(src.shape[1], c)
            sent.append(_remote(src.at[:, mine], dst.at[:, mine], sems, a, (x, y, 1 - c)))
            got.append(_remote(dst.at[:, other], dst.at[:, other], sems, a, (x, y, c)))
        mine, other = _halves(SMALL_ROWS, c)
        sent.append(_remote(fs_ref.at[mine], os_ref.at[mine], sems, 2, (x, y, 1 - c)))
        got.append(_remote(os_ref.at[other], os_ref.at[other], sems, 2, (x, y, c)))
        for cp in sent:
            cp.start()

        to_sibling = _remote(late_ref, sib_ref, sems, 3, (x, y, 1 - c))
        to_sibling.start()
        to_sibling.wait()
        chip_ref[...] = late_ref[...] + sib_ref[...]
        to_chips = [_remote(chip_ref, got_ref.at[k], sems, 4 + k, (cx, cy, c)) for k, (cx, cy) in enumerate(chips)]
        for cp in to_chips:
            cp.start()
        for cp in to_chips:
            cp.wait()
        sum_ref[...] = (chip_ref[...] + got_ref[0]) + (got_ref[1] + got_ref[2])

        for cp in got:
            cp.wait_recv()
        for cp in sent:
            cp.wait_send()

    vmem = pl.BlockSpec(memory_space=pltpu.VMEM)
    return pl.pallas_call(
        body, name="join_sibling", in_specs=[ANY] * 3 + [vmem], out_specs=[ANY] * 3 + [vmem],
        out_shape=[jax.ShapeDtypeStruct(a.shape, F32) for a in (fi, fo, fs, late)],
        scratch_shapes=[pltpu.SemaphoreType.DMA((7,)), pltpu.SemaphoreType.DMA((7,)),
                        pltpu.VMEM(late.shape, F32), pltpu.VMEM(late.shape, F32), pltpu.VMEM((3,) + late.shape, F32)],
        input_output_aliases={0: 0, 1: 1, 2: 2},
    )(fi, fo, fs, late)


SMALL_SIZES = (DEPTH * D_MODEL, D_MODEL, DEPTH * POOL_WIDTH, 128, DEPTH * 4 * GROUP_DIM * GROUP_DIM)
LOSS_SLOT = DEPTH * D_MODEL + D_MODEL + DEPTH * POOL_WIDTH + DEPTH * 8


def _pack_small(norm_g, final_g, pool_scale, forget_bias, pool_w, loss=None):
    fb = forget_bias.reshape(-1)
    if loss is not None:
        fb = jnp.concatenate([fb, loss.reshape(1)])
    fb = jnp.pad(fb, (0, 128 - fb.size))
    flat = jnp.concatenate([norm_g.reshape(-1), final_g.reshape(-1), pool_scale.reshape(-1), fb,
                            pool_w.reshape(-1)])
    return jnp.pad(flat, (0, SMALL_ROWS * 128 - flat.size)).reshape(SMALL_ROWS, 128)


def _unpack_small(packed):
    flat = packed.reshape(-1)
    offs = [0]
    for n in SMALL_SIZES:
        offs.append(offs[-1] + n)
    norm_g, final_g, pool_scale, fb, pool_w = [flat[offs[i]:offs[i + 1]] for i in range(5)]
    return (norm_g.reshape(DEPTH, D_MODEL), final_g, pool_scale.reshape(DEPTH, POOL_WIDTH),
            fb[:DEPTH * 8].reshape(DEPTH, 8), pool_w.reshape(DEPTH, 4, GROUP_DIM, GROUP_DIM))


def kernel(x, norm_g, w_in, forget_bias, pool_w, pool_scale, w_out, final_g, loss_target, m_norm_g, m_w_in, m_forget_bias, m_pool_w, m_pool_scale, m_w_out, m_final_g, v_norm_g, v_w_in, v_forget_bias, v_pool_w, v_pool_scale, v_w_out, v_final_g):
    x0 = x[0]
    tgt = loss_target[0]
    s = x0.shape[0]

    me = 2 * lax.axis_index("x") + lax.axis_index("y")
    place = jnp.stack([me, lax.axis_index("c")]).astype(jnp.int32)
    wi_b, wo_b = w_in.astype(BF16), w_out.astype(BF16)

    def whole_w_out(got):
        return jnp.concatenate([lax.dynamic_index_in_dim(got, jnp.bitwise_xor(k, me), axis=1, keepdims=False)[0]
                                for k in range(N_CHIPS)], axis=0)

    pool_w_b = pool_w.astype(BF16)
    fb_pad = jnp.pad(forget_bias, ((0, 0), (0, 128 - forget_bias.shape[1])))
    head_sel = (jnp.arange(ATTN_WIDTH)[None, :] // HEAD_DIM == jnp.arange(8)[:, None]).astype(BF16)

    saved, weights = [], []
    xl = x0
    got_i = _gather_weights(wi_b[0:1])
    for l in range(DEPTH):
        pu, pg, q, k, v, ag, lf, c, h, po, w_t = _inproj_fwd(
            place, xl, norm_g[l][None], None if l > 0 else wi_b[0], got_i, fb_pad[l][None],
            pool_w_b[l], pool_scale[l][None])
        if l + 1 < DEPTH:
            ma, o, lse, got_o, got_i = _attn_fwd(q, k, v, c, ag, carry=(wo_b[l:l + 1], wi_b[l + 1:l + 2]))
        else:
            ma, o, lse, got_o = _attn_fwd(q, k, v, c, ag, carry=(wo_b[l:l + 1],))
        w_o = whole_w_out(got_o)
        weights.append((w_t, w_o))
        saved.append((xl, h, pu, pg, q, k, v, ag, lf, c, po, ma, o, lse))
        if l + 1 < DEPTH:
            xl = _outproj_fwd(xl, po, ma, w_o)
    dx, g_final, loss_part = _loss_head(xl, po, ma, w_o, tgt, final_g[None])

    g_norm, g_fb, g_pw, g_ps = [], [], [], []
    parts = {}
    pending = None
    for l in reversed(range(DEPTH)):
        xl, h, pu, pg, q, k, v, ag, lf, c, po, ma, o, lse = saved[l]
        w_t, w_o = weights[l]
        dmp, do, dag, dt, gwo = _outproj_bwd(dx, po, ma, ag, o, w_o, head_sel)
        if pending is None:
            dq, dk, dv, dcs, drs = _attn_bwd(q, k, v, do, c, lse, dt.reshape(N_PAIRS, 2, s))
        else:
            dq, dk, dv, dcs, drs, qi, qo = _attn_bwd(q, k, v, do, c, lse, dt.reshape(N_PAIRS, 2, s), carry=pending)
            parts[l + 1] += (qi, qo)
        dpf, dfb = _forget_bwd(dcs, drs, lf)
        dpu, dpg, gpw, gps = _pool_bwd(pu, pg, dmp, pool_w_b[l], pool_scale[l][None])
        dsec = (dpu, dpg, dq, dk, dv, dag, dpf)
        gwi = _inproj_bwd_w(h, dsec)
        g_fb.append(dfb[0, :8])
        g_pw.append(gpw)
        g_ps.append(gps[0])
        send_i = gwi[None]
        send_o = gwo.reshape(1, N_CHIPS, SHARD_ROWS, D_MODEL)
        if l > 0:
            dx, gn, ri, ro = _inproj_bwd_x(dsec, w_t, xl, norm_g[l][None], dx, exchange=(send_i, send_o))
            g_norm.append(gn[0])
        else:
            g_norm.append(jnp.zeros((D_MODEL,), F32))
            g_norm, g_fb, g_pw, g_ps = [t[::-1] for t in (g_norm, g_fb, g_pw, g_ps)]
            send_s = _pack_small(jnp.stack(g_norm), g_final[0], jnp.stack(g_ps), jnp.stack(g_fb), jnp.stack(g_pw),
                                 loss=loss_part[0, 0])
            ri, ro, rs = _exchange_sibling(send_i, send_o, send_s)
        pending = tuple(_chip_sum(place, send_i, ri, send_o, ro))
        parts[l] = (send_i, ri, send_o, ro)
    ps_ = _add2(send_s, rs, "chip_sum_small", SMALL_ROWS // 4)
    dx, gn0, qi, qo, qs = _inproj_bwd_x(dsec, w_t, xl, norm_g[0][None], dx, scatter=(*pending, ps_))
    parts[0] += (qi, qo)
    fi = lax.empty((DEPTH, D_MODEL, SHARD_COLS), F32)
    fo = lax.empty((DEPTH, SHARD_ROWS, D_MODEL), F32)
    for l in range(DEPTH):
        send_i, ri, send_o, ro, qi, qo = parts[l]
        fi = _mesh_sum(place, send_i, ri, qi, fi, l, "mesh_sum_w_in", 256)
        fo = _mesh_sum(place, send_o, ro, qo, fo, l, "mesh_sum_w_out", SHARD_ROWS // 2)
    hs = _mesh_sum_small(place, ps_, qs, SMALL_HALF // 4)
    fi, fo, fs, gn0 = _join_sibling(fi, fo, hs, gn0.reshape(8, 128))
    fs = jnp.concatenate([fs[0:8] + gn0, fs[8:]], axis=0)

    col_major = lambda a: jnp.transpose(a, (2, 0, 1))
    flat_o = lambda a: a.reshape(-1, D_MODEL)
    out_i = _adamw(col_major(fi), col_major(w_in), col_major(m_w_in), col_major(v_w_in), "adamw_w_in", 77)
    out_i = [jnp.transpose(t, (1, 2, 0)) for t in out_i]
    out_o = _adamw(flat_o(fo), flat_o(w_out), flat_o(m_w_out), flat_o(v_w_out), "adamw_w_out", 256)
    out_s = _adamw(fs, _pack_small(norm_g, final_g, pool_scale, forget_bias, pool_w),
                   _pack_small(m_norm_g, m_final_g, m_pool_scale, m_forget_bias, m_pool_w),
                   _pack_small(v_norm_g, v_final_g, v_pool_scale, v_forget_bias, v_pool_w),
                   "adamw_small", SMALL_ROWS // 4)

    loss = out_s[0].reshape(-1)[LOSS_SLOT]
    groups = []
    for oi, oo, os_ in zip(out_i, out_o, out_s):
        sn, sf, sp, sb, sw = _unpack_small(os_)
        groups.append((sn, oi.reshape(w_in.shape), sb, sw, sp, oo.reshape(w_out.shape), sf))
    return (loss, dx[None]) + tuple(t for grp in groups for t in grp)
```

```python
import jax
import jax.numpy as jnp
from jax import lax
from jax.experimental import pallas as pl
from jax.experimental.pallas import tpu as pltpu

F32 = jnp.float32
BF16 = jnp.bfloat16

D_MODEL = 1024
DEPTH = 4
POOL_WIDTH = 512
ATTN_WIDTH = 512
HEAD_DIM = 64
PAIR = 2 * HEAD_DIM
N_PAIRS = ATTN_WIDTH // PAIR
POOL_WINDOWS = (2, 4, 8, 16)
GROUP_DIM = 128
HALO = 16
IN_COLS = 3080
OFF_F = 3072
IN_COLS_PAD = 3200
SECTIONS = ((0, 512), (512, 512), (1024, 512), (1536, 512), (2048, 512), (2560, 512), (OFF_F, 128))
N_CHIPS = 4
SHARD_COLS = IN_COLS // N_CHIPS
SHARD_ROWS = D_MODEL // N_CHIPS
RMS_EPS = 1e-6
NEG = -1e30
Q_SCALE = 0.125

ADAM_LR = 0.001
ADAM_B1 = 0.9
ADAM_B2 = 0.999
ADAM_EPS = 1e-08
ADAM_WD = 0.01
ADAM_STEP = 10

SMALL_ROWS = 2112
SMALL_HALF = SMALL_ROWS // 2

NT = (((1,), (1,)), ((), ()))
TN = (((0,), (0,)), ((), ()))
MESH = pl.DeviceIdType.MESH
ANY = pl.BlockSpec(memory_space=pl.ANY)


def _params(semantics, vmem_mb=48):
    return pltpu.CompilerParams(dimension_semantics=semantics, vmem_limit_bytes=vmem_mb << 20)


def _row_spec(tm, cols):
    return pl.BlockSpec((tm, cols), lambda i: (i, 0))


def _full_spec(shape):
    return pl.BlockSpec(shape, lambda *_: (0,) * len(shape))


def _sigmoid(x):
    return jax.nn.sigmoid(x)


def _scan_rows(a, reverse=False):
    n = a.shape[0]
    row = lax.broadcasted_iota(jnp.int32, a.shape, 0)
    k = 1
    while k < n:
        if reverse:
            a = a + jnp.where(row < n - k, pltpu.roll(a, n - k, 0), 0.0)
        else:
            a = a + jnp.where(row >= k, pltpu.roll(a, k, 0), 0.0)
        k *= 2
    return a


def _inproj_fwd(place, x, g, own, got, fb, pw, ps, tm=512):
    s = x.shape[0]

    def body(place_ref, x_ref, g_ref, fb_ref, pw_ref, ps_ref, *refs):
        own_ref = refs[0] if own is not None else None
        (got_ref, pu_ref, pg_ref, q_ref, k_ref, v_ref, ag_ref, lf_ref, c_ref, h_ref, po_ref, wt_ref,
         carry_ref, halo_ref, w_ref, piece_ref, sem) = refs[own is not None:]
        step = pl.program_id(0)

        @pl.when(step == 0)
        def _():
            carry_ref[...] = jnp.zeros_like(carry_ref)
            halo_ref[...] = jnp.zeros_like(halo_ref)
            for chip in range(N_CHIPS):
                rel = jnp.bitwise_xor(chip, place_ref[0])

                def fetch(src):
                    cp = pltpu.make_async_copy(src, piece_ref, sem)
                    cp.start()
                    cp.wait()

                if own_ref is None:
                    fetch(got_ref.at[0, rel])
                else:
                    pl.when(rel == 0)(lambda: fetch(own_ref))
                    pl.when(rel != 0)(lambda: fetch(got_ref.at[0, rel]))
                w_ref[:, chip * SHARD_COLS:(chip + 1) * SHARD_COLS] = piece_ref[...]
            w_ref[:, IN_COLS:] = jnp.zeros((D_MODEL, IN_COLS_PAD - IN_COLS), BF16)
            for t in range(IN_COLS_PAD // 128):
                cols = slice(t * 128, (t + 1) * 128)
                wt_ref[cols, :] = jnp.transpose(w_ref[:, cols].astype(F32)).astype(BF16)

        xf = x_ref[...]
        r = lax.rsqrt(jnp.mean(xf * xf, axis=-1, keepdims=True) + RMS_EPS)
        h = (xf * r * g_ref[...]).astype(BF16)
        h_ref[...] = h

        def proj(sec):
            off, n = SECTIONS[sec]
            return jnp.dot(h, w_ref[:, off:off + n], preferred_element_type=F32)

        pu_ref[...] = proj(0).astype(BF16)
        pg_ref[...] = proj(1).astype(BF16)
        u = pu_ref[...].astype(F32)
        d = _pool_delta(u, halo_ref[...], step * tm)
        halo_ref[...] = u[tm - HALO:, :]
        for gi in range(len(POOL_WINDOWS)):
            cols = slice(gi * GROUP_DIM, (gi + 1) * GROUP_DIM)
            zg = jnp.dot(d[gi].astype(BF16), pw_ref[gi], preferred_element_type=F32)
            gate = pg_ref[:, cols].astype(F32)
            po_ref[:, cols] = (zg * ps_ref[:, cols] * (gate * _sigmoid(gate))).astype(BF16)
        q_ref[...] = (proj(2) * Q_SCALE).astype(BF16)
        k_ref[...] = proj(3).astype(BF16)
        v_ref[...] = proj(4).astype(BF16)
        ag_ref[...] = proj(5).astype(BF16)
        z = proj(6) + fb_ref[...]
        lf = jnp.minimum(z, 0.0) - jnp.log(1.0 + jnp.exp(-jnp.abs(z)))
        lf_ref[...] = lf
        c_ref[...] = _scan_rows(lf) + carry_ref[0:1, :]
        carry_ref[0:1, :] = c_ref[tm - 1:tm, :]

    act = jax.ShapeDtypeStruct((s, 512), BF16)
    rows = lambda cols: pl.BlockSpec((tm, cols), lambda i, p: (i, 0))
    whole = lambda shape: pl.BlockSpec(shape, lambda i, p: (0,) * len(shape))
    shards = ([] if own is None else [own]) + [got]
    return pl.pallas_call(
        body, name="inproj_fwd" if own is None else "inproj_fwd_first",
        grid_spec=pltpu.PrefetchScalarGridSpec(
            num_scalar_prefetch=1, grid=(s // tm,),
            in_specs=[rows(D_MODEL), whole((1, D_MODEL)), whole((1, 128)), whole((4, GROUP_DIM, GROUP_DIM)),
                      whole((1, 512))] + [ANY] * len(shards),
            out_specs=[rows(512)] * 6 + [rows(128), rows(128), rows(D_MODEL), rows(512),
                                         whole((IN_COLS_PAD, D_MODEL))],
            scratch_shapes=[pltpu.VMEM((8, 128), F32), pltpu.VMEM((HALO, 512), F32),
                            pltpu.VMEM((D_MODEL, IN_COLS_PAD), BF16),
                            pltpu.VMEM((D_MODEL, SHARD_COLS), BF16), pltpu.SemaphoreType.DMA]),
        out_shape=[act] * 6 + [jax.ShapeDtypeStruct((s, 128), F32), jax.ShapeDtypeStruct((s, 128), F32),
                               jax.ShapeDtypeStruct((s, D_MODEL), BF16), act,
                               jax.ShapeDtypeStruct((IN_COLS_PAD, D_MODEL), BF16)],
        compiler_params=_params(("arbitrary",), vmem_mb=56),
    )(place, x, g, fb, pw, ps, *shards)


def _window_sums(ext, forward):
    n = ext.shape[0]
    outs = []
    for gi, w in enumerate(POOL_WINDOWS):
        a = ext[:, gi * GROUP_DIM:(gi + 1) * GROUP_DIM]
        k = 1
        while k < w:
            a = a + pltpu.roll(a, (n - k) if forward else k, 0)
            k *= 2
        outs.append(a)
    return outs


def _over_count(a, row0, gi):
    w = POOL_WINDOWS[gi]
    t = row0 + lax.broadcasted_iota(jnp.int32, (HALO, GROUP_DIM), 0)
    head = a[0:HALO, :] / jnp.minimum(t + 1, w).astype(F32)
    return jnp.concatenate([head, a[HALO:, :] * (1.0 / w)], axis=0)


def _pool_delta(u, halo, row0):
    sums = _window_sums(jnp.concatenate([halo, u], axis=0), forward=False)
    return [_over_count(sums[gi][HALO:, :], row0, gi) - u[:, gi * GROUP_DIM:(gi + 1) * GROUP_DIM]
            for gi in range(len(POOL_WINDOWS))]


def _split3(x):
    hi = x.astype(BF16).astype(F32)
    r1 = x - hi
    lo = r1.astype(BF16).astype(F32)
    return hi, lo, (r1 - lo).astype(BF16).astype(F32)


def _own(idx, h):
    return (idx < HEAD_DIM) if h == 0 else (idx >= HEAD_DIM)


def _spare(h):
    return HEAD_DIM if h == 0 else 0


def _attn_fwd(q, k, v, c, ag, carry=None, tq=512, tk=512, chunk=512):
    s = q.shape[0]
    n_carry = 0 if carry is None else len(carry)

    def body(*refs):
        q_ref, k_ref, v_ref, c_ref, ag_ref = refs[:5]
        carry_src = refs[5:5 + n_carry]
        ma_ref, o_ref, lse_ref = refs[5 + n_carry:8 + n_carry]
        carry_dst = refs[8 + n_carry:8 + 2 * n_carry]
        kx_ref, vt_ref, acc0_ref, acc1_ref, sta_ref, stb_ref = refs[8 + 2 * n_carry:14 + 2 * n_carry]
        carry_sems = refs[14 + 2 * n_carry:]
        j = pl.program_id(0)
        i = pl.program_id(1)

        if n_carry:
            @pl.when((j == 0) & (i == 0))
            def _():
                _gather_start(carry_src, carry_dst, carry_sems[:2])
                for cp in _gather_own(carry_src, carry_dst, carry_sems[2]):
                    cp.start()

        @pl.when(i == 0)
        def _():
            def prep(ch, carry):
                rows = pl.ds(pl.multiple_of(ch * chunk, chunk), chunk)
                kk = k_ref[rows, :].astype(F32)
                cc = c_ref[rows, :]
                vt = jnp.transpose(v_ref[rows, :].astype(F32))
                lane = lax.broadcasted_iota(jnp.int32, (chunk, PAIR), 1)
                sub = lax.broadcasted_iota(jnp.int32, (PAIR, chunk), 0)
                for h in range(2):
                    ccol = jnp.sum(jnp.where(lane == 2 * j + h, cc, 0.0), axis=1, keepdims=True)
                    kx = jnp.where(_own(lane, h), kk, 0.0)
                    for t, part in enumerate(_split3(-ccol)):
                        kx = jnp.where(lane == _spare(h) + t, part, kx)
                    kx_ref[h, rows, :] = kx.astype(BF16)
                    vt_ref[h, :, rows] = jnp.where(_own(sub, h), vt,
                                                   jnp.where(sub == _spare(h), 1.0, 0.0)).astype(BF16)
                return carry

            lax.fori_loop(0, s // chunk, prep, 0)

        qq = q_ref[...].astype(F32)
        lane_q = lax.broadcasted_iota(jnp.int32, (tq, PAIR), 1)
        qx = []
        for h in range(2):
            ones = (lane_q >= _spare(h)) & (lane_q < _spare(h) + 3)
            qx.append(jnp.transpose(jnp.where(_own(lane_q, h), qq, jnp.where(ones, 1.0, 0.0))).astype(BF16))
        accs = (acc0_ref, acc1_ref)
        for acc in accs:
            acc[...] = jnp.zeros_like(acc)

        def rows_of(kb):
            return pl.ds(pl.multiple_of(kb * tk, tk), tk)

        def scores(kb, dst):
            for h in range(2):
                dst[h] = jnp.dot(kx_ref[h, rows_of(kb), :], qx[h], preferred_element_type=F32)

        def consume(kb, src, m, masked):
            m_out, alpha, pv = [], [], []
            for h in range(2):
                sh = src[h]
                if masked:
                    kpos = kb * tk + lax.broadcasted_iota(jnp.int32, (tk, tq), 0)
                    qpos = i * tq + lax.broadcasted_iota(jnp.int32, (tk, tq), 1)
                    sh = jnp.where(qpos >= kpos, sh, NEG)
                m_new = jnp.maximum(m[h], jnp.max(sh, axis=0, keepdims=True))
                pt = jnp.exp(sh - m_new).astype(BF16)
                alpha.append(jnp.exp(m[h] - m_new))
                pv.append(jnp.dot(vt_ref[h, :, rows_of(kb)], pt, preferred_element_type=F32))
                m_out.append(m_new)
            for h in range(2):
                accs[h][...] = accs[h][...] * alpha[h] + pv[h]
            return tuple(m_out)

        n_full = (i * tq) // tk

        def two_blocks(t, m):
            kb = 2 * t
            scores(kb + 1, stb_ref)
            m = consume(kb, sta_ref, m, False)
            scores(kb + 2, sta_ref)
            return consume(kb + 1, stb_ref, m, False)

        def odd_tail(m):
            scores(n_full, stb_ref)
            m = consume(n_full - 1, sta_ref, m, False)
            return consume(n_full, stb_ref, m, True)

        def even_tail(m):
            return consume(n_full, sta_ref, m, True)

        scores(0, sta_ref)
        m0 = jnp.full((1, tq), NEG, F32)
        n_two = n_full // 2
        m = lax.fori_loop(0, n_two // 2, lambda u, m: two_blocks(2 * u + 1, two_blocks(2 * u, m)), (m0, m0))
        m = lax.cond(lax.rem(n_two, 2) == 1, lambda m: two_blocks(n_two - 1, m), lambda m: m, m)
        m = lax.cond(lax.rem(n_full, 2) == 1, odd_tail, even_tail, m)

        sub_o = lax.broadcasted_iota(jnp.int32, (PAIR, tq), 0)
        den = [accs[h][_spare(h):_spare(h) + 1, :] for h in range(2)]
        ot = jnp.where(sub_o < HEAD_DIM, acc0_ref[...] / den[0], acc1_ref[...] / den[1])
        o = jnp.transpose(ot)
        o_ref[...] = o.astype(BF16)
        gate = ag_ref[...].astype(F32)
        ma_ref[...] = (o * (gate * _sigmoid(gate))).astype(BF16)
        sub8 = lax.broadcasted_iota(jnp.int32, (8, tq), 0)
        lse_ref[...] = jnp.where(sub8 == 0, m[0] + jnp.log(den[0]), m[1] + jnp.log(den[1]))

        if n_carry:
            @pl.when((j == N_PAIRS - 1) & (i == s // tq - 1))
            def _():
                _gather_finish(carry_src, carry_dst, carry_sems[:2])
                for cp in _gather_own(carry_src, carry_dst, carry_sems[2]):
                    cp.wait()

    carried = [] if carry is None else list(carry)
    carry_sem = ([pltpu.SemaphoreType.DMA((GATHER_SEMS * n_carry,))] * 2
                 + [pltpu.SemaphoreType.DMA((n_carry,))]) if n_carry else []
    return pl.pallas_call(
        body, name="attn_fwd_gather" if n_carry else "attn_fwd", grid=(N_PAIRS, s // tq),
        in_specs=[pl.BlockSpec((tq, PAIR), lambda j, i: (i, j)),
                  pl.BlockSpec((s, PAIR), lambda j, i: (0, j)),
                  pl.BlockSpec((s, PAIR), lambda j, i: (0, j)),
                  pl.BlockSpec((s, 128), lambda j, i: (0, 0)),
                  pl.BlockSpec((tq, PAIR), lambda j, i: (i, j))] + [ANY] * n_carry,
        out_specs=[pl.BlockSpec((tq, PAIR), lambda j, i: (i, j)),
                   pl.BlockSpec((tq, PAIR), lambda j, i: (i, j)),
                   pl.BlockSpec((None, 8, tq), lambda j, i: (j, 0, i))] + [ANY] * n_carry,
        out_shape=[jax.ShapeDtypeStruct((s, ATTN_WIDTH), BF16), jax.ShapeDtypeStruct((s, ATTN_WIDTH), BF16),
                   jax.ShapeDtypeStruct((N_PAIRS, 8, s), F32)] + (_gather_shapes(*carried) if n_carry else []),
        scratch_shapes=[pltpu.VMEM((2, s, PAIR), BF16), pltpu.VMEM((2, PAIR, s), BF16),
                        pltpu.VMEM((PAIR, tq), F32), pltpu.VMEM((PAIR, tq), F32),
                        pltpu.VMEM((2, tk, tq), F32), pltpu.VMEM((2, tk, tq), F32)] + carry_sem,
        compiler_params=_params(("arbitrary", "arbitrary")),
    )(q, k, v, c, ag, *carried)


def _outproj_fwd(x, po, ma, wo, tm=512):
    s = x.shape[0]

    def body(x_ref, po_ref, ma_ref, wo_ref, xn_ref):
        xn_ref[...] = (x_ref[...]
                       + jnp.dot(po_ref[...], wo_ref[0:POOL_WIDTH, :], preferred_element_type=F32)
                       + jnp.dot(ma_ref[...], wo_ref[POOL_WIDTH:, :], preferred_element_type=F32))

    return pl.pallas_call(
        body, name="outproj_fwd", grid=(s // tm,),
        in_specs=[_row_spec(tm, D_MODEL), _row_spec(tm, 512), _row_spec(tm, 512),
                  _full_spec((D_MODEL, D_MODEL))],
        out_specs=_row_spec(tm, D_MODEL),
        out_shape=jax.ShapeDtypeStruct((s, D_MODEL), F32),
        compiler_params=_params(("parallel",)),
    )(x, po, ma, wo)


def _loss_head(x, po, ma, wo, tgt, g, tm=512):
    s = x.shape[0]

    def body(x_ref, po_ref, ma_ref, wo_ref, t_ref, g_ref, dx_ref, dg_ref, loss_ref):
        @pl.when(pl.program_id(0) == 0)
        def _():
            dg_ref[...] = jnp.zeros_like(dg_ref)
            loss_ref[...] = jnp.zeros_like(loss_ref)

        xf = (x_ref[...]
              + jnp.dot(po_ref[...], wo_ref[0:POOL_WIDTH, :], preferred_element_type=F32)
              + jnp.dot(ma_ref[...], wo_ref[POOL_WIDTH:, :], preferred_element_type=F32))
        r = lax.rsqrt(jnp.mean(xf * xf, axis=-1, keepdims=True) + RMS_EPS)
        xh = xf * r
        gg = g_ref[...]
        e = xh * gg - t_ref[...]
        loss_ref[...] += 0.5 * jnp.sum(jnp.mean(e * e, axis=-1, keepdims=True))
        dy = e * (1.0 / D_MODEL)
        u = dy * gg
        dx_ref[...] = r * (u - xh * jnp.mean(xh * u, axis=-1, keepdims=True))
        dg_ref[...] += jnp.sum(dy * xh, axis=0, keepdims=True)

    return pl.pallas_call(
        body, name="loss_head", grid=(s // tm,),
        in_specs=[_row_spec(tm, D_MODEL), _row_spec(tm, 512), _row_spec(tm, 512), _full_spec((D_MODEL, D_MODEL)),
                  _row_spec(tm, D_MODEL), _full_spec((1, D_MODEL))],
        out_specs=[_row_spec(tm, D_MODEL), _full_spec((1, D_MODEL)), _full_spec((8, 128))],
        out_shape=[jax.ShapeDtypeStruct((s, D_MODEL), F32), jax.ShapeDtypeStruct((1, D_MODEL), F32),
                   jax.ShapeDtypeStruct((8, 128), F32)],
        compiler_params=_params(("arbitrary",)),
    )(x, po, ma, wo, tgt, g)


def _outproj_bwd(dx, po, ma, ag, o, wo, head_sel, tm=512):
    s = dx.shape[0]

    def body(dx_ref, po_ref, ma_ref, ag_ref, o_ref, wo_ref, sel_ref, dmp_ref, do_ref, dag_ref, dt_ref, gwo_ref):
        @pl.when(pl.program_id(0) == 0)
        def _():
            gwo_ref[...] = jnp.zeros_like(gwo_ref)

        dxb = dx_ref[...].astype(BF16)
        dm = lax.dot_general(dxb, wo_ref[...], NT, preferred_element_type=F32)
        dmp_ref[...] = dm[:, 0:POOL_WIDTH].astype(BF16)
        dma = dm[:, POOL_WIDTH:]
        gate = ag_ref[...].astype(F32)
        of = o_ref[...].astype(F32)
        sg = _sigmoid(gate)
        do = dma * (gate * sg)
        do_ref[...] = do.astype(BF16)
        dag_ref[...] = (dma * of * (sg * (1.0 + gate * (1.0 - sg)))).astype(BF16)
        prod = do * of
        hi = prod.astype(BF16)
        lo = (prod - hi.astype(F32)).astype(BF16)
        dt_ref[...] = (lax.dot_general(sel_ref[...], hi, NT, preferred_element_type=F32)
                       + lax.dot_general(sel_ref[...], lo, NT, preferred_element_type=F32))
        gwo_ref[0:POOL_WIDTH, :] += lax.dot_general(po_ref[...], dxb, TN, preferred_element_type=F32)
        gwo_ref[POOL_WIDTH:, :] += lax.dot_general(ma_ref[...], dxb, TN, preferred_element_type=F32)

    act = jax.ShapeDtypeStruct((s, 512), BF16)
    return pl.pallas_call(
        body, name="outproj_bwd", grid=(s // tm,),
        in_specs=[_row_spec(tm, D_MODEL)] + [_row_spec(tm, 512)] * 4
                 + [_full_spec((D_MODEL, D_MODEL)), _full_spec((8, ATTN_WIDTH))],
        out_specs=[_row_spec(tm, 512)] * 3 + [pl.BlockSpec((8, tm), lambda i: (0, i)),
                                              _full_spec((D_MODEL, D_MODEL))],
        out_shape=[act, act, act, jax.ShapeDtypeStruct((8, s), F32),
                   jax.ShapeDtypeStruct((D_MODEL, D_MODEL), F32)],
        compiler_params=_params(("arbitrary",)),
    )(dx, po, ma, ag, o, wo, head_sel)


def _attn_bwd(q, k, v, do, c, lse, dt, carry=None, tq=256, chunk=512):
    s = q.shape[0]
    tk = 2 * tq
    nq = s // tq
    n_carry = 0 if carry is None else len(carry)

    def body(*refs):
        q_ref, k_ref, v_ref, do_ref, c_ref, lse_ref, dt_ref = refs[:7]
        carry_src = refs[7:7 + n_carry]
        dq_ref, dk_ref, dv_ref, dcs_ref, drs_ref = refs[7 + n_carry:12 + n_carry]
        carry_dst = refs[12 + n_carry:12 + 2 * n_carry]
        (qxt_ref, doxt_ref, qm_ref, dox_ref, dqt0_ref, dqt1_ref, dk0_ref, dk1_ref, dvx_ref,
         kx_ref, vx_ref, ktm_ref, sa_ref, sb_ref) = refs[12 + 2 * n_carry:26 + 2 * n_carry]
        carry_sems = refs[26 + 2 * n_carry:]
        carry_big = tuple(zip(carry_src, carry_dst))
        j = pl.program_id(0)
        b = pl.program_id(1)
        dqts = (dqt0_ref, dqt1_ref)
        dks = (dk0_ref, dk1_ref)

        if n_carry:
            @pl.when((j == 0) & (b == 0))
            def _():
                _scatter_start(carry_big, None, carry_sems)

        @pl.when(b == 0)
        def _():
            def prep(ch, carry):
                rows = pl.ds(pl.multiple_of(ch * chunk, chunk), chunk)
                qq = q_ref[rows, :].astype(F32)
                dd = do_ref[rows, :].astype(F32)
                qt = jnp.transpose(qq)
                ddt = jnp.transpose(dd)
                lane = lax.broadcasted_iota(jnp.int32, (chunk, PAIR), 1)
                sub = lax.broadcasted_iota(jnp.int32, (PAIR, chunk), 0)
                for h in range(2):
                    sp = _spare(h)
                    qm_ref[h, rows, :] = jnp.where(_own(lane, h), qq, jnp.where(lane == sp, 1.0, 0.0)).astype(BF16)
                    dox_ref[h, rows, :] = jnp.where(_own(lane, h), dd, 0.0).astype(BF16)
                    qx = jnp.where(_own(sub, h), qt, jnp.where((sub >= sp) & (sub < sp + 3), 1.0, 0.0))
                    for t, part in enumerate(_split3(-lse_ref[pl.ds(h, 1), rows])):
                        qx = jnp.where(sub == sp + 3 + t, part, qx)
                    qxt_ref[h, :, rows] = qx.astype(BF16)
                    dx = jnp.where(_own(sub, h), ddt, 0.0)
                    for t, part in enumerate(_split3(-dt_ref[pl.ds(h, 1), rows])):
                        dx = jnp.where(sub == sp + t, part, dx)
                    doxt_ref[h, :, rows] = dx.astype(BF16)
                return carry

            lax.fori_loop(0, s // chunk, prep, 0)
            for ref in dqts:
                ref[...] = jnp.zeros_like(ref)

        kk = k_ref[...].astype(F32)
        vv = v_ref[...].astype(F32)
        cc = c_ref[...]
        kt = jnp.transpose(kk)
        lane = lax.broadcasted_iota(jnp.int32, (tk, PAIR), 1)
        sub = lax.broadcasted_iota(jnp.int32, (PAIR, tk), 0)
        for h in range(2):
            sp = _spare(h)
            ccol = jnp.sum(jnp.where(lane == 2 * j + h, cc, 0.0), axis=1, keepdims=True)
            kx = jnp.where(_own(lane, h), kk, jnp.where((lane >= sp + 3) & (lane < sp + 6), 1.0, 0.0))
            for t, part in enumerate(_split3(-ccol)):
                kx = jnp.where(lane == sp + t, part, kx)
            kx_ref[h] = kx.astype(BF16)
            vx_ref[h] = jnp.where(_own(lane, h), vv, jnp.where((lane >= sp) & (lane < sp + 3), 1.0, 0.0)).astype(BF16)
            ktm_ref[h] = jnp.where(_own(sub, h), kt, jnp.where(sub == sp, 1.0, 0.0)).astype(BF16)
        for ref in dks:
            ref[...] = jnp.zeros_like(ref)
        dvx_ref[...] = jnp.zeros_like(dvx_ref)

        def cols_of(i):
            return pl.ds(pl.multiple_of(i * tq, tq), tq)

        def scores(i, dst):
            cols = cols_of(jnp.minimum(i, nq - 1))
            for h in range(2):
                dst[h] = jnp.dot(kx_ref[h], qxt_ref[h, :, cols], preferred_element_type=F32)
                dst[2 + h] = jnp.dot(vx_ref[h], doxt_ref[h, :, cols], preferred_element_type=F32)

        def consume(i, src, masked):
            cols = cols_of(i)
            for h in range(2):
                arg = src[h]
                if masked:
                    kpos = b * tk + lax.broadcasted_iota(jnp.int32, (tk, tq), 0)
                    qpos = i * tq + lax.broadcasted_iota(jnp.int32, (tk, tq), 1)
                    arg = jnp.where(qpos >= kpos, arg, NEG)
                pt = jnp.exp(arg)
                dst = (pt * src[2 + h]).astype(BF16)
                dvx_ref[...] += jnp.dot(pt.astype(BF16), dox_ref[h, cols, :], preferred_element_type=F32)
                dks[h][...] += jnp.dot(dst, qm_ref[h, cols, :], preferred_element_type=F32)
                dqts[h][:, cols] += jnp.dot(ktm_ref[h], dst, preferred_element_type=F32)

        i0 = 2 * b

        def two_blocks(t, masked):
            i = i0 + 2 * t
            scores(i + 1, sb_ref)
            consume(i, sa_ref, masked)
            scores(i + 2, sa_ref)
            consume(i + 1, sb_ref, masked)

        def four_blocks(u, carry):
            two_blocks(1 + 2 * u, False)
            two_blocks(2 + 2 * u, False)
            return carry

        scores(i0, sa_ref)
        two_blocks(0, True)
        n_open = (nq - i0) // 2 - 1
        lax.fori_loop(0, n_open // 2, four_blocks, 0)

        @pl.when(lax.rem(n_open, 2) == 1)
        def _():
            two_blocks(n_open, False)

        dv_ref[...] = dvx_ref[...].astype(BF16)
        dk_ref[...] = jnp.where(lane < HEAD_DIM, dk0_ref[...], dk1_ref[...]).astype(BF16)
        dcs_ref[...] = jnp.where(lane == _spare(0), dk0_ref[...], jnp.where(lane == _spare(1), dk1_ref[...], 0.0))

        @pl.when(b == s // tk - 1)
        def _():
            sub8 = lax.broadcasted_iota(jnp.int32, (8, s), 0)
            drs_ref[...] = jnp.where(sub8 == 0, dqt0_ref[_spare(0):_spare(0) + 1, :],
                                     jnp.where(sub8 == 1, dqt1_ref[_spare(1):_spare(1) + 1, :], 0.0))
            sub_c = lax.broadcasted_iota(jnp.int32, (PAIR, chunk), 0)
            for ch in range(s // chunk):
                rows = pl.ds(ch * chunk, chunk)
                both = jnp.where(sub_c < HEAD_DIM, dqt0_ref[:, rows], dqt1_ref[:, rows])
                dq_ref[rows, :] = (jnp.transpose(both) * Q_SCALE).astype(BF16)

        if n_carry:
            @pl.when((j == N_PAIRS - 1) & (b == s // tk - 1))
            def _():
                _scatter_finish(carry_big, None, carry_sems)

    act = jax.ShapeDtypeStruct((s, ATTN_WIDTH), BF16)
    pair_rows = pl.BlockSpec((s, PAIR), lambda j, b: (0, j))
    pair_blk = pl.BlockSpec((tk, PAIR), lambda j, b: (b, j))
    stat = pl.BlockSpec((None, 2, s), lambda j, b: (j, 0, 0))
    carried = [] if carry is None else list(carry)
    carry_sem = [pltpu.SemaphoreType.DMA((SCATTER_SEMS,))] * 2 if n_carry else []
    return pl.pallas_call(
        body, name="attn_bwd_scatter" if n_carry else "attn_bwd", grid=(N_PAIRS, s // tk),
        in_specs=[pair_rows, pair_blk, pair_blk, pair_rows,
                  pl.BlockSpec((tk, 128), lambda j, b: (b, 0)),
                  pl.BlockSpec((None, 8, s), lambda j, b: (j, 0, 0)), stat] + [ANY] * n_carry,
        out_specs=[pair_rows, pair_blk, pair_blk, pair_blk,
                   pl.BlockSpec((None, 8, s), lambda j, b: (j, 0, 0))] + [ANY] * n_carry,
        out_shape=[act, act, act, jax.ShapeDtypeStruct((s, N_PAIRS * 128), F32),
                   jax.ShapeDtypeStruct((N_PAIRS, 8, s), F32)] + (_scatter_shapes(*carried) if n_carry else []),
        scratch_shapes=[pltpu.VMEM((2, PAIR, s), BF16), pltpu.VMEM((2, PAIR, s), BF16),
                        pltpu.VMEM((2, s, PAIR), BF16), pltpu.VMEM((2, s, PAIR), BF16),
                        pltpu.VMEM((PAIR, s), F32), pltpu.VMEM((PAIR, s), F32),
                        pltpu.VMEM((tk, PAIR), F32), pltpu.VMEM((tk, PAIR), F32), pltpu.VMEM((tk, PAIR), F32),
                        pltpu.VMEM((2, tk, PAIR), BF16), pltpu.VMEM((2, tk, PAIR), BF16),
                        pltpu.VMEM((2, PAIR, tk), BF16),
                        pltpu.VMEM((4, tk, tq), F32), pltpu.VMEM((4, tk, tq), F32)] + carry_sem,
        compiler_params=_params(("arbitrary", "arbitrary")),
    )(q, k, v, do, c, lse, dt, *carried)


def _forget_bwd(dcs, drs, lf, tm=512):
    s = lf.shape[0]
    n = s // tm

    def body(dcs_ref, drs_ref, lf_ref, dpf_ref, dfb_ref, carry_ref):
        @pl.when(pl.program_id(0) == 0)
        def _():
            carry_ref[...] = jnp.zeros_like(carry_ref)
            dfb_ref[...] = jnp.zeros_like(dfb_ref)

        lane = lax.broadcasted_iota(jnp.int32, (tm, 128), 1)
        dc = jnp.zeros((tm, 128), F32)
        for pj in range(N_PAIRS):
            rows = jnp.concatenate([drs_ref[pj], jnp.zeros((120, tm), F32)], axis=0)
            cols = jnp.transpose(rows)
            if pj:
                cols = pltpu.roll(cols, 2 * pj, 1)
            dc = dc + jnp.where((lane == 2 * pj) | (lane == 2 * pj + 1), cols, 0.0)
            blk = dcs_ref[:, pj * 128:(pj + 1) * 128]
            for h in range(2):
                head = 2 * pj + h
                moved = pltpu.roll(blk, (head - _spare(h)) % 128, 1) if head != _spare(h) else blk
                dc = dc - jnp.where(lane == head, moved, 0.0)
        dlf = _scan_rows(dc, reverse=True) + carry_ref[0:1, :]
        carry_ref[...] = dlf[0:8, :]
        dz = jnp.where(lane < 8, dlf * (1.0 - jnp.exp(lf_ref[...])), 0.0)
        dpf_ref[...] = dz.astype(BF16)
        dfb_ref[...] += jnp.sum(dz, axis=0, keepdims=True)

    return pl.pallas_call(
        body, name="forget_bwd", grid=(n,),
        in_specs=[pl.BlockSpec((tm, N_PAIRS * 128), lambda i: (n - 1 - i, 0)),
                  pl.BlockSpec((N_PAIRS, 8, tm), lambda i: (0, 0, n - 1 - i)),
                  pl.BlockSpec((tm, 128), lambda i: (n - 1 - i, 0))],
        out_specs=[pl.BlockSpec((tm, 128), lambda i: (n - 1 - i, 0)), _full_spec((1, 128))],
        out_shape=[jax.ShapeDtypeStruct((s, 128), BF16), jax.ShapeDtypeStruct((1, 128), F32)],
        scratch_shapes=[pltpu.VMEM((8, 128), F32)],
        compiler_params=_params(("arbitrary",)),
    )(dcs, drs, lf)


def _pool_bwd(pu, pg, dmp, pw, ps, tm=512):
    s = pu.shape[0]
    hb = tm // HALO
    n = s // tm
    last_halo = s // HALO - 1

    def body(pu_ref, halo_ref, pg_ref, dmp_ref, pgn_ref, dmpn_ref, pw_ref, ps_ref,
             dpu_ref, dpg_ref, gpw_ref, gps_ref):
        i = pl.program_id(0)

        @pl.when(i == 0)
        def _():
            gpw_ref[...] = jnp.zeros_like(gpw_ref)
            gps_ref[...] = jnp.zeros_like(gps_ref)

        u = pu_ref[...].astype(F32)
        halo = jnp.where(i == 0, 0.0, halo_ref[...].astype(F32))
        d = _pool_delta(u, halo, i * tm)
        e_parts, dd_parts = [], []
        for gi in range(len(POOL_WINDOWS)):
            cols = slice(gi * GROUP_DIM, (gi + 1) * GROUP_DIM)
            wg = pw_ref[gi]
            scale = ps_ref[:, cols]
            db = d[gi].astype(BF16)
            z = jnp.dot(db, wg, preferred_element_type=F32)
            gate = pg_ref[:, cols].astype(F32)
            sg = _sigmoid(gate)
            dm = dmp_ref[:, cols].astype(F32)
            dy = dm * (gate * sg)
            dpg_ref[:, cols] = (dm * (z * scale) * (sg * (1.0 + gate * (1.0 - sg)))).astype(BF16)
            gps_ref[:, cols] += jnp.sum(dy * z, axis=0, keepdims=True)
            dz = (dy * scale).astype(BF16)
            gpw_ref[gi] += lax.dot_general(db, dz, TN, preferred_element_type=F32)
            dd = lax.dot_general(dz, wg, NT, preferred_element_type=F32)
            gate_n = pgn_ref[:, cols].astype(F32)
            dz_n = (dmpn_ref[:, cols].astype(F32) * (gate_n * _sigmoid(gate_n)) * scale).astype(BF16)
            dd_n = lax.dot_general(dz_n, wg, NT, preferred_element_type=F32)
            dd_n = jnp.where(i == n - 1, 0.0, dd_n)
            dd_parts.append(dd)
            e_parts.append(jnp.concatenate([_over_count(dd, i * tm, gi), dd_n * (1.0 / POOL_WINDOWS[gi])], axis=0))
        lead = _window_sums(jnp.concatenate(e_parts, axis=1), forward=True)
        for gi in range(len(POOL_WINDOWS)):
            cols = slice(gi * GROUP_DIM, (gi + 1) * GROUP_DIM)
            dpu_ref[:, cols] = (lead[gi][0:tm, :] - dd_parts[gi]).astype(BF16)

    act = jax.ShapeDtypeStruct((s, 512), BF16)
    prev_halo = pl.BlockSpec((HALO, 512), lambda i: (jnp.maximum(i * hb - 1, 0), 0))
    next_halo = pl.BlockSpec((HALO, 512), lambda i: (jnp.minimum((i + 1) * hb, last_halo), 0))
    return pl.pallas_call(
        body, name="pool_bwd", grid=(n,),
        in_specs=[_row_spec(tm, 512), prev_halo, _row_spec(tm, 512), _row_spec(tm, 512), next_halo, next_halo,
                  _full_spec((4, GROUP_DIM, GROUP_DIM)), _full_spec((1, 512))],
        out_specs=[_row_spec(tm, 512), _row_spec(tm, 512), _full_spec((4, GROUP_DIM, GROUP_DIM)),
                   _full_spec((1, 512))],
        out_shape=[act, act, jax.ShapeDtypeStruct((4, GROUP_DIM, GROUP_DIM), F32),
                   jax.ShapeDtypeStruct((1, 512), F32)],
        compiler_params=_params(("arbitrary",)),
    )(pu, pu, pg, dmp, pg, dmp, pw, ps)


RING = 3


def _inproj_bwd_x(dsec, wt, x, g, dxo, exchange=None, scatter=None, tm=512):
    s = x.shape[0]
    n_steps = s // tm
    carried = list(exchange or scatter or ())
    n_carry = len(carried)

    def body(*refs):
        d_refs = refs[0:7]
        wt_ref, x_ref, g_ref, dxo_ref = refs[7:11]
        carry_src = refs[11:11 + n_carry]
        dxi_ref, dg_ref = refs[11 + n_carry:13 + n_carry]
        carry_dst = refs[13 + n_carry:13 + 2 * n_carry]
        xbuf_ref, dbuf_ref, xsem, dsem = refs[13 + 2 * n_carry:17 + 2 * n_carry]
        sems = refs[17 + 2 * n_carry:]
        step = pl.program_id(0)

        def fetch(j):
            slot = lax.rem(j, RING)
            rows = pl.ds(pl.multiple_of(j * tm, tm), tm)
            return [pltpu.make_async_copy(x_ref.at[rows], xbuf_ref.at[slot], xsem.at[slot]),
                    pltpu.make_async_copy(dxo_ref.at[rows], dbuf_ref.at[slot], dsem.at[slot])]

        def comm(phase):
            if exchange is not None:
                (_exchange_start if phase == 0 else _exchange_finish)(carry_src, carry_dst, None, sems)
            else:
                big = tuple(zip(carry_src[:2], carry_dst[:2]))
                (_scatter_start if phase == 0 else _scatter_finish)(big, (carry_src[2], carry_dst[2]), sems)

        @pl.when(step == 0)
        def _():
            dg_ref[...] = jnp.zeros_like(dg_ref)
            for j in range(RING - 1):
                for cp in fetch(j):
                    cp.start()
            if n_carry:
                comm(0)

        @pl.when(step + RING - 1 < n_steps)
        def _():
            for cp in fetch(step + RING - 1):
                cp.start()

        dh = None
        for d_ref, (off, ncol) in zip(d_refs, SECTIONS):
            t = jnp.dot(d_ref[...], wt_ref[off:off + ncol, :], preferred_element_type=F32)
            dh = t if dh is None else dh + t
        for cp in fetch(step):
            cp.wait()
        slot = lax.rem(step, RING)
        xf = xbuf_ref[slot]
        r = lax.rsqrt(jnp.mean(xf * xf, axis=-1, keepdims=True) + RMS_EPS)
        xh = xf * r
        u = dh * g_ref[...]
        dxi_ref[...] = dbuf_ref[slot] + r * (u - xh * jnp.mean(xh * u, axis=-1, keepdims=True))
        dg_ref[...] += jnp.sum(dh * xh, axis=0, keepdims=True)

        if n_carry:
            @pl.when(step == n_steps - 1)
            def _():
                comm(1)

    if exchange is not None:
        name, carry_shapes, n_sems = "inproj_bwd_x_exchange", _exchange_shapes(*exchange), 3
    elif scatter is not None:
        name, n_sems = "inproj_bwd_x_scatter", SCATTER_SEMS
        carry_shapes = _scatter_shapes(*scatter[:2]) + [jax.ShapeDtypeStruct((3, SMALL_HALF, 128), F32)]
    else:
        name, carry_shapes, n_sems = "inproj_bwd_x", [], 0
    return pl.pallas_call(
        body, name=name, grid=(n_steps,),
        in_specs=[_row_spec(tm, ncol) for _, ncol in SECTIONS]
                 + [_full_spec((IN_COLS_PAD, D_MODEL)), ANY, _full_spec((1, D_MODEL)), ANY] + [ANY] * n_carry,
        out_specs=[_row_spec(tm, D_MODEL), _full_spec((1, D_MODEL))] + [ANY] * n_carry,
        out_shape=[jax.ShapeDtypeStruct((s, D_MODEL), F32), jax.ShapeDtypeStruct((1, D_MODEL), F32)] + carry_shapes,
        scratch_shapes=[pltpu.VMEM((RING, tm, D_MODEL), F32), pltpu.VMEM((RING, tm, D_MODEL), F32),
                        pltpu.SemaphoreType.DMA((RING,)), pltpu.SemaphoreType.DMA((RING,))]
                       + ([pltpu.SemaphoreType.DMA((n_sems,))] * 2 if n_carry else []),
        compiler_params=_params(("arbitrary",)),
    )(*dsec, wt, x, g, dxo, *carried)


def _inproj_bwd_w(h, dsec, tm=512):
    s = h.shape[0]
    n_steps = s // tm

    def body(*refs):
        h_ref = refs[0]
        d_refs = refs[1:8]
        out_ref, gw_ref = refs[8:]
        step = pl.program_id(0)

        @pl.when(step == 0)
        def _():
            gw_ref[...] = jnp.zeros_like(gw_ref)

        hh = h_ref[...]
        for d_ref, (off, ncol) in zip(d_refs, SECTIONS):
            gw_ref[:, off:off + ncol] += lax.dot_general(hh, d_ref[...], TN, preferred_element_type=F32)

        @pl.when(step == n_steps - 1)
        def _():
            for k in range(N_CHIPS):
                out_ref[k] = gw_ref[:, k * SHARD_COLS:(k + 1) * SHARD_COLS]

    return pl.pallas_call(
        body, name="inproj_bwd_w", grid=(n_steps,),
        in_specs=[_row_spec(tm, D_MODEL)] + [_row_spec(tm, ncol) for _, ncol in SECTIONS],
        out_specs=_full_spec((N_CHIPS, D_MODEL, SHARD_COLS)),
        out_shape=jax.ShapeDtypeStruct((N_CHIPS, D_MODEL, SHARD_COLS), F32),
        scratch_shapes=[pltpu.VMEM((D_MODEL, IN_COLS_PAD), F32)],
        compiler_params=_params(("arbitrary",), vmem_mb=60),
    )(h, *dsec)


def _elementwise(fn, name, n_out, arrays, tm):
    shape = arrays[0].shape
    rest = shape[1:]
    n_in = len(arrays)

    def body(*refs):
        outs = fn(*[r[...] for r in refs[:n_in]])
        for r, val in zip(refs[n_in:], outs):
            r[...] = val

    spec = pl.BlockSpec((tm,) + rest, lambda i: (i,) + (0,) * len(rest))
    return pl.pallas_call(
        body, name=name, grid=(shape[0] // tm,),
        in_specs=[spec] * n_in, out_specs=[spec] * n_out, out_shape=[jax.ShapeDtypeStruct(shape, F32)] * n_out,
        compiler_params=_params(("parallel",)),
    )(*arrays)


def _add2(a, b, name, tm):
    return _elementwise(lambda p, q: (p + q,), name, 1, [a, b], tm)[0]


def _chip_sum(place, gi, ri, go, ro, nb=2):
    half_i, cols_i = ri.shape[2:]
    half_o, cols_o = ro.shape[2:]
    tm = half_i // nb

    def body(place_ref, gi_ref, ri_ref, go_ref, ro_ref, pi_ref, po_ref):
        pi_ref[...] = (gi_ref[...] + ri_ref[...]).astype(BF16)
        po_ref[...] = (go_ref[...] + ro_ref[...]).astype(BF16)

    blk_i, blk_o = (None, None, tm, cols_i), (None, None, half_o, cols_o)
    return pl.pallas_call(
        body, name="chip_sum",
        grid_spec=pltpu.PrefetchScalarGridSpec(
            num_scalar_prefetch=1, grid=(N_CHIPS, nb),
            in_specs=[pl.BlockSpec(blk_i, lambda k, i, p: (0, k, p[1] * nb + i, 0)),
                      pl.BlockSpec(blk_i, lambda k, i, p: (0, k, i, 0)),
                      pl.BlockSpec(blk_o, lambda k, i, p: (0, k, p[1], 0)),
                      pl.BlockSpec(blk_o, lambda k, i, p: (0, k, 0, 0))],
            out_specs=[pl.BlockSpec(blk_i, lambda k, i, p: (0, k, i, 0)),
                       pl.BlockSpec(blk_o, lambda k, i, p: (0, k, 0, 0))]),
        out_shape=[jax.ShapeDtypeStruct(ri.shape, BF16), jax.ShapeDtypeStruct(ro.shape, BF16)],
        compiler_params=_params(("arbitrary", "arbitrary")),
    )(place, gi, ri, go, ro)


def _mesh_sum(place, g, r, got, into, layer, name, tm):
    _, _, rows, cols = g.shape
    half = rows // 2
    nb = half // tm

    def body(place_ref, g_ref, r_ref, got_ref, into_ref, o_ref):
        own = g_ref[...] + r_ref[...]
        o_ref[...] = (own + got_ref[0].astype(F32)) + (got_ref[1].astype(F32) + got_ref[2].astype(F32))

    blk = (None, None, tm, cols)
    return pl.pallas_call(
        body, name=name,
        grid_spec=pltpu.PrefetchScalarGridSpec(
            num_scalar_prefetch=1, grid=(nb,),
            in_specs=[pl.BlockSpec(blk, lambda i, p: (0, p[0], p[1] * nb + i, 0)),
                      pl.BlockSpec(blk, lambda i, p: (0, p[0], i, 0)),
                      pl.BlockSpec((None, 3, tm, cols), lambda i, p: (0, 0, i, 0)),
                      ANY],
            out_specs=pl.BlockSpec((None, tm, cols), lambda i, p: (layer, p[1] * nb + i, 0))),
        out_shape=jax.ShapeDtypeStruct(into.shape, F32),
        input_output_aliases={4: 0},
        compiler_params=_params(("parallel",)),
    )(place, g, r, got, into)


def _mesh_sum_small(place, ps, got, tm):
    nb = SMALL_HALF // tm

    def body(place_ref, ps_ref, got_ref, o_ref):
        o_ref[...] = (ps_ref[...] + got_ref[0]) + (got_ref[1] + got_ref[2])

    return pl.pallas_call(
        body, name="mesh_sum_small",
        grid_spec=pltpu.PrefetchScalarGridSpec(
            num_scalar_prefetch=1, grid=(nb,),
            in_specs=[pl.BlockSpec((tm, 128), lambda i, p: (p[1] * nb + i, 0)),
                      pl.BlockSpec((3, tm, 128), lambda i, p: (0, i, 0))],
            out_specs=pl.BlockSpec((tm, 128), lambda i, p: (p[1] * nb + i, 0))),
        out_shape=jax.ShapeDtypeStruct((SMALL_ROWS, 128), F32),
        compiler_params=_params(("parallel",)),
    )(place, ps, got)


def _adamw(g, w, m, v, name, tm):
    def fn(g, w, m, v):
        m = ADAM_B1 * m + (1.0 - ADAM_B1) * g
        v = ADAM_B2 * v + (1.0 - ADAM_B2) * (g * g)
        m_hat = m / (1.0 - ADAM_B1 ** ADAM_STEP)
        v_hat = v / (1.0 - ADAM_B2 ** ADAM_STEP)
        delta = -ADAM_LR * (m_hat / (jnp.sqrt(v_hat) + ADAM_EPS) + ADAM_WD * w)
        return g, delta, m, v

    return _elementwise(fn, name, 4, [g, w, m, v], tm)


def _position():
    x, y, c = lax.axis_index("x"), lax.axis_index("y"), lax.axis_index("c")
    other_chips = [(x, 1 - y), (1 - x, y), (1 - x, 1 - y)]
    return x, y, c, other_chips


def _remote(src, dst, sems, k, to):
    send_sems, recv_sems = sems
    return pltpu.make_async_remote_copy(src_ref=src, dst_ref=dst, send_sem=send_sems.at[k],
                                        recv_sem=recv_sems.at[k], device_id=to, device_id_type=MESH)


def _comm_call(body, name, n_in, out_shape, n_remote):
    return pl.pallas_call(
        body, name=name, in_specs=[ANY] * n_in, out_specs=[ANY] * len(out_shape), out_shape=out_shape,
        scratch_shapes=[pltpu.SemaphoreType.DMA((n_remote,)), pltpu.SemaphoreType.DMA((n_remote,))],
    )


def _halves(ref_rows, c):
    half = ref_rows // 2
    return pl.ds(half * c, half), pl.ds(half * (1 - c), half)


def _gather_weights(wi):
    def body(wi_ref, gi_ref, send_sems, recv_sems):
        _gather_start((wi_ref,), (gi_ref,), (send_sems, recv_sems))
        _gather_finish((wi_ref,), (gi_ref,), (send_sems, recv_sems))

    return _comm_call(body, "gather_weights", 1, _gather_shapes(wi), GATHER_SEMS)(wi)[0]


GATHER_SEMS = 6
SCATTER_SEMS = 9


def _gather_shapes(*arrays):
    return [jax.ShapeDtypeStruct((a.shape[0], N_CHIPS) + a.shape[1:], a.dtype) for a in arrays]


def _gather_sends(srcs, dsts, sems):
    x, y, c, chips = _position()
    out = []
    for k, (cx, cy) in enumerate(chips):
        for a, (src, dst) in enumerate(zip(srcs, dsts)):
            mine, _ = _halves(src.shape[1], c)
            out.append(_remote(src.at[:, mine], dst.at[:, k + 1, mine], sems, len(srcs) * k + a, (cx, cy, c)))
    return out


def _gather_own(srcs, dsts, local_sems):
    return [pltpu.make_async_copy(src, dst.at[:, 0], local_sems.at[a]) for a, (src, dst) in enumerate(zip(srcs, dsts))]


def _gather_start(srcs, dsts, sems):
    for cp in _gather_sends(srcs, dsts, sems):
        cp.start()


def _gather_finish(srcs, dsts, sems):
    x, y, c, _ = _position()
    n = len(srcs)
    forwards = []
    for k in range(3):
        for a, dst in enumerate(dsts):
            mine, _ = _halves(dst.shape[2], c)
            blk = dst.at[:, k + 1, mine]
            _remote(blk, blk, sems, n * k + a, (x, y, c)).wait_recv()
            forwards.append(_remote(blk, blk, sems, 3 * n + n * k + a, (x, y, 1 - c)))
            forwards[-1].start()
    for k in range(3):
        for a, dst in enumerate(dsts):
            _, other = _halves(dst.shape[2], c)
            blk = dst.at[:, k + 1, other]
            _remote(blk, blk, sems, 3 * n + n * k + a, (x, y, c)).wait_recv()
    for cp in _gather_sends(srcs, dsts, sems) + forwards:
        cp.wait_send()


def _scatter_sends(big, small, sems):
    x, y, c, chips = _position()
    n = len(big) + (small is not None)
    out = []
    for k, (cx, cy) in enumerate(chips):
        ck = 2 * cx + cy
        for a, (src, dst) in enumerate(big):
            out.append(_remote(src.at[:, ck], dst.at[:, k], sems, n * k + a, (cx, cy, c)))
        if small is not None:
            half, _ = _halves(SMALL_ROWS, c)
            out.append(_remote(small[0].at[half], small[1].at[k], sems, n * k + n - 1, (cx, cy, c)))
    return out


def _scatter_start(big, small, sems):
    for cp in _scatter_sends(big, small, sems):
        cp.start()


def _scatter_finish(big, small, sems):
    x, y, c, _ = _position()
    n = len(big) + (small is not None)
    for k in range(3):
        landing = [dst.at[:, k] for _, dst in big] + ([small[1].at[k]] if small is not None else [])
        for a, blk in enumerate(landing):
            _remote(blk, blk, sems, n * k + a, (x, y, c)).wait_recv()
    for cp in _scatter_sends(big, small, sems):
        cp.wait_send()


def _exchange_shapes(gi, go):
    return [jax.ShapeDtypeStruct(a.shape[:2] + (a.shape[2] // 2, a.shape[3]), F32) for a in (gi, go)]


def _exchange_copies(srcs, dsts, small, sems):
    x, y, c, _ = _position()
    cps = [_remote(src.at[:, :, _halves(src.shape[2], c)[1]], dst, sems, a, (x, y, 1 - c))
           for a, (src, dst) in enumerate(zip(srcs, dsts))]
    if small is not None:
        cps.append(_remote(small[0], small[1], sems, len(srcs), (x, y, 1 - c)))
    return cps


def _exchange_start(srcs, dsts, small, sems):
    for cp in _exchange_copies(srcs, dsts, small, sems):
        cp.start()


def _exchange_finish(srcs, dsts, small, sems):
    for cp in _exchange_copies(srcs, dsts, small, sems):
        cp.wait()


def _exchange_sibling(gi, go, sm):
    def body(gi_ref, go_ref, sm_ref, ri_ref, ro_ref, rs_ref, send_sems, recv_sems):
        args = ((gi_ref, go_ref), (ri_ref, ro_ref), (sm_ref, rs_ref), (send_sems, recv_sems))
        _exchange_start(*args)
        _exchange_finish(*args)

    out_shape = _exchange_shapes(gi, go) + [jax.ShapeDtypeStruct(sm.shape, F32)]
    return _comm_call(body, "exchange_sibling", 3, out_shape, 3)(gi, go, sm)


def _scatter_shapes(pi, po):
    return [jax.ShapeDtypeStruct((a.shape[0], 3) + a.shape[2:], a.dtype) for a in (pi, po)]


def _join_sibling(fi, fo, fs, late):
    def body(fi_ref, fo_ref, fs_ref, late_ref, oi_ref, oo_ref, os_ref, sum_ref, send_sems, recv_sems,
             sib_ref, chip_ref, got_ref):
        x, y, c, chips = _position()
        sems = (send_sems, recv_sems)
        sent, got = [], []
        for a, (src, dst) in enumerate(((fi_ref, oi_ref), (fo_ref, oo_ref))):
            mine, other = _halves(src.shape[1], c)
            sent.append(_remote(src.at[:, mine], dst.at[:, mine], sems, a, (x, y, 1 - c)))
            got.append(_remote(dst.at[:, other], dst.at[:, other], sems, a, (x, y, c)))
        mine, other = _halves(SMALL_ROWS, c)
        sent.append(_remote(fs_ref.at[mine], os_ref.at[mine], sems, 2, (x, y, 1 - c)))
        got.append(_remote(os_ref.at[other], os_ref.at[other], sems, 2, (x, y, c)))
        for cp in sent:
            cp.start()

        to_sibling = _remote(late_ref, sib_ref, sems, 3, (x, y, 1 - c))
        to_sibling.start()
        to_sibling.wait()
        chip_ref[...] = late_ref[...] + sib_ref[...]
        to_chips = [_remote(chip_ref, got_ref.at[k], sems, 4 + k, (cx, cy, c)) for k, (cx, cy) in enumerate(chips)]
        for cp in to_chips:
            cp.start()
        for cp in to_chips:
            cp.wait()
        sum_ref[...] = (chip_ref[...] + got_ref[0]) + (got_ref[1] + got_ref[2])

        for cp in got:
            cp.wait_recv()
        for cp in sent:
            cp.wait_send()

    vmem = pl.BlockSpec(memory_space=pltpu.VMEM)
    return pl.pallas_call(
        body, name="join_sibling", in_specs=[ANY] * 3 + [vmem], out_specs=[ANY] * 3 + [vmem],
        out_shape=[jax.ShapeDtypeStruct(a.shape, F32) for a in (fi, fo, fs, late)],
        scratch_shapes=[pltpu.SemaphoreType.DMA((7,)), pltpu.SemaphoreType.DMA((7,)),
                        pltpu.VMEM(late.shape, F32), pltpu.VMEM(late.shape, F32), pltpu.VMEM((3,) + late.shape, F32)],
        input_output_aliases={0: 0, 1: 1, 2: 2},
    )(fi, fo, fs, late)


SMALL_SIZES = (DEPTH * D_MODEL, D_MODEL, DEPTH * POOL_WIDTH, 128, DEPTH * 4 * GROUP_DIM * GROUP_DIM)
LOSS_SLOT = DEPTH * D_MODEL + D_MODEL + DEPTH * POOL_WIDTH + DEPTH * 8


def _pack_small(norm_g, final_g, pool_scale, forget_bias, pool_w, loss=None):
    fb = forget_bias.reshape(-1)
    if loss is not None:
        fb = jnp.concatenate([fb, loss.reshape(1)])
    fb = jnp.pad(fb, (0, 128 - fb.size))
    flat = jnp.concatenate([norm_g.reshape(-1), final_g.reshape(-1), pool_scale.reshape(-1), fb,
                            pool_w.reshape(-1)])
    return jnp.pad(flat, (0, SMALL_ROWS * 128 - flat.size)).reshape(SMALL_ROWS, 128)


def _unpack_small(packed):
    flat = packed.reshape(-1)
    offs = [0]
    for n in SMALL_SIZES:
        offs.append(offs[-1] + n)
    norm_g, final_g, pool_scale, fb, pool_w = [flat[offs[i]:offs[i + 1]] for i in range(5)]
    return (norm_g.reshape(DEPTH, D_MODEL), final_g, pool_scale.reshape(DEPTH, POOL_WIDTH),
            fb[:DEPTH * 8].reshape(DEPTH, 8), pool_w.reshape(DEPTH, 4, GROUP_DIM, GROUP_DIM))


def kernel(x, norm_g, w_in, forget_bias, pool_w, pool_scale, w_out, final_g, loss_target, m_norm_g, m_w_in, m_forget_bias, m_pool_w, m_pool_scale, m_w_out, m_final_g, v_norm_g, v_w_in, v_forget_bias, v_pool_w, v_pool_scale, v_w_out, v_final_g):
    x0 = x[0]
    tgt = loss_target[0]
    s = x0.shape[0]

    me = 2 * lax.axis_index("x") + lax.axis_index("y")
    place = jnp.stack([me, lax.axis_index("c")]).astype(jnp.int32)
    wi_b, wo_b = w_in.astype(BF16), w_out.astype(BF16)

    def whole_w_out(got):
        return jnp.concatenate([lax.dynamic_index_in_dim(got, jnp.bitwise_xor(k, me), axis=1, keepdims=False)[0]
                                for k in range(N_CHIPS)], axis=0)

    pool_w_b = pool_w.astype(BF16)
    fb_pad = jnp.pad(forget_bias, ((0, 0), (0, 128 - forget_bias.shape[1])))
    head_sel = (jnp.arange(ATTN_WIDTH)[None, :] // HEAD_DIM == jnp.arange(8)[:, None]).astype(BF16)

    saved, weights = [], []
    xl = x0
    got_i = _gather_weights(wi_b[0:1])
    for l in range(DEPTH):
        pu, pg, q, k, v, ag, lf, c, h, po, w_t = _inproj_fwd(
            place, xl, norm_g[l][None], None if l > 0 else wi_b[0], got_i, fb_pad[l][None],
            pool_w_b[l], pool_scale[l][None])
        if l + 1 < DEPTH:
            ma, o, lse, got_o, got_i = _attn_fwd(q, k, v, c, ag, carry=(wo_b[l:l + 1], wi_b[l + 1:l + 2]))
        else:
            ma, o, lse, got_o = _attn_fwd(q, k, v, c, ag, carry=(wo_b[l:l + 1],))
        w_o = whole_w_out(got_o)
        weights.append((w_t, w_o))
        saved.append((xl, h, pu, pg, q, k, v, ag, lf, c, po, ma, o, lse))
        if l + 1 < DEPTH:
            xl = _outproj_fwd(xl, po, ma, w_o)
    dx, g_final, loss_part = _loss_head(xl, po, ma, w_o, tgt, final_g[None])

    g_norm, g_fb, g_pw, g_ps = [], [], [], []
    parts = {}
    pending = None
    for l in reversed(range(DEPTH)):
        xl, h, pu, pg, q, k, v, ag, lf, c, po, ma, o, lse = saved[l]
        w_t, w_o = weights[l]
        dmp, do, dag, dt, gwo = _outproj_bwd(dx, po, ma, ag, o, w_o, head_sel)
        if pending is None:
            dq, dk, dv, dcs, drs = _attn_bwd(q, k, v, do, c, lse, dt.reshape(N_PAIRS, 2, s))
        else:
            dq, dk, dv, dcs, drs, qi, qo = _attn_bwd(q, k, v, do, c, lse, dt.reshape(N_PAIRS, 2, s), carry=pending)
            parts[l + 1] += (qi, qo)
        dpf, dfb = _forget_bwd(dcs, drs, lf)
        dpu, dpg, gpw, gps = _pool_bwd(pu, pg, dmp, pool_w_b[l], pool_scale[l][None])
        dsec = (dpu, dpg, dq, dk, dv, dag, dpf)
        gwi = _inproj_bwd_w(h, dsec)
        g_fb.append(dfb[0, :8])
        g_pw.append(gpw)
        g_ps.append(gps[0])
        send_i = gwi[None]
        send_o = gwo.reshape(1, N_CHIPS, SHARD_ROWS, D_MODEL)
        if l > 0:
            dx, gn, ri, ro = _inproj_bwd_x(dsec, w_t, xl, norm_g[l][None], dx, exchange=(send_i, send_o))
            g_norm.append(gn[0])
        else:
            g_norm.append(jnp.zeros((D_MODEL,), F32))
            g_norm, g_fb, g_pw, g_ps = [t[::-1] for t in (g_norm, g_fb, g_pw, g_ps)]
            send_s = _pack_small(jnp.stack(g_norm), g_final[0], jnp.stack(g_ps), jnp.stack(g_fb), jnp.stack(g_pw),
                                 loss=loss_part[0, 0])
            ri, ro, rs = _exchange_sibling(send_i, send_o, send_s)
        pending = tuple(_chip_sum(place, send_i, ri, send_o, ro))
        parts[l] = (send_i, ri, send_o, ro)
    ps_ = _add2(send_s, rs, "chip_sum_small", SMALL_ROWS // 4)
    dx, gn0, qi, qo, qs = _inproj_bwd_x(dsec, w_t, xl, norm_g[0][None], dx, scatter=(*pending, ps_))
    parts[0] += (qi, qo)
    fi = lax.empty((DEPTH, D_MODEL, SHARD_COLS), F32)
    fo = lax.empty((DEPTH, SHARD_ROWS, D_MODEL), F32)
    for l in range(DEPTH):
        send_i, ri, send_o, ro, qi, qo = parts[l]
        fi = _mesh_sum(place, send_i, ri, qi, fi, l, "mesh_sum_w_in", 256)
        fo = _mesh_sum(place, send_o, ro, qo, fo, l, "mesh_sum_w_out", SHARD_ROWS // 2)
    hs = _mesh_sum_small(place, ps_, qs, SMALL_HALF // 4)
    fi, fo, fs, gn0 = _join_sibling(fi, fo, hs, gn0.reshape(8, 128))
    fs = jnp.concatenate([fs[0:8] + gn0, fs[8:]], axis=0)

    col_major = lambda a: jnp.transpose(a, (2, 0, 1))
    flat_o = lambda a: a.reshape(-1, D_MODEL)
    out_i = _adamw(col_major(fi), col_major(w_in), col_major(m_w_in), col_major(v_w_in), "adamw_w_in", 77)
    out_i = [jnp.transpose(t, (1, 2, 0)) for t in out_i]
    out_o = _adamw(flat_o(fo), flat_o(w_out), flat_o(m_w_out), flat_o(v_w_out), "adamw_w_out", 256)
    out_s = _adamw(fs, _pack_small(norm_g, final_g, pool_scale, forget_bias, pool_w),
                   _pack_small(m_norm_g, m_final_g, m_pool_scale, m_forget_bias, m_pool_w),
                   _pack_small(v_norm_g, v_final_g, v_pool_scale, v_forget_bias, v_pool_w),
                   "adamw_small", SMALL_ROWS // 4)

    loss = out_s[0].reshape(-1)[LOSS_SLOT]
    groups = []
    for oi, oo, os_ in zip(out_i, out_o, out_s):
        sn, sf, sp, sb, sw = _unpack_small(os_)
        groups.append((sn, oi.reshape(w_in.shape), sb, sw, sp, oo.reshape(w_out.shape), sf))
    return (loss, dx[None]) + tuple(t for grp in groups for t in grp)
```
